```python
import jax
import jax.numpy as jnp
from jax import lax
import numpy as np

D_MODEL = 1024
BATCH = 32
SEQ = 256
DEPTH = 2
DEC_BATCH = 2
DEC_SEQ = 4096
PAST_LEN = 256

GRID_W = 64
ROPE_THETA = 10000.0
EPS = 1e-6
Q_BLOCK = 128
HG_HEADS = 4
HG_DK = 128
HG_DV = 128
HG_CHUNK = 32
N_HG_LAYERS = (DEPTH + 1) // 2
MLA_HEADS = 8
MLA_NOPE = 64
MLA_ROPE = 32
MLA_V = 64
MLA_QK = MLA_NOPE + MLA_ROPE
MLA_Q_LORA = 256
MLA_KV_LORA = 128
MLA_SCALE = MLA_QK ** -0.5
SWA_HEADS = 16
SWA_KV_HEADS = 4
SWA_HD = 64
SWA_WINDOW = 128
SWA_BLOCK = 128
SWA_SCALE = SWA_HD ** -0.5
N_EXPERTS = 32
TOP_K = 4
D_FF = 1024
SWIGLU_LIMIT = 7.0
SWIGLU_ALPHA = 1.702
MOE_BLOCK = 128

AB_SPLITS = (HG_HEADS * HG_DK,) * 3 + (HG_HEADS * HG_DV,) * 2 + (MLA_Q_LORA, MLA_KV_LORA, MLA_ROPE)
AB_IN = sum(AB_SPLITS)
AB_OUT = HG_HEADS * HG_DV + MLA_HEADS * MLA_V
C_SPLITS = (SWA_HEADS * SWA_HD, SWA_KV_HEADS * SWA_HD, SWA_KV_HEADS * SWA_HD)
C_IN = sum(C_SPLITS)
C_OUT = SWA_HEADS * SWA_HD

kernel_name = 'hgrn2_mla_swa_moe_diffusion_step'


def split_cols(z, sizes):
    offs, acc = [], 0
    for s in sizes[:-1]:
        acc += s
        offs.append(acc)
    return jnp.split(z, offs, axis=-1)


def rms_norm(x, g):
    xf = x.astype(jnp.float32)
    y = xf * lax.rsqrt(jnp.mean(xf * xf, axis=-1, keepdims=True) + EPS)
    return (y * g.astype(jnp.float32)).astype(x.dtype)


def ada_params(cond, p):
    mod = jax.nn.silu(cond) @ p['ada_w'] + p['ada_b']
    if mod.ndim == 2:
        mod = mod[:, None, :]
    return jnp.split(mod, 6, axis=-1)


def modulate(x, g, shift, scale):
    return rms_norm(x, g) * (1.0 + scale) + shift


def axial_rope_tables(rows, n_rot):
    t = jnp.arange(rows * GRID_W)
    row = (t // GRID_W).astype(jnp.float32)
    col = (t % GRID_W).astype(jnp.float32)
    n_freq = n_rot // 4
    inv = jnp.power(ROPE_THETA, -jnp.arange(n_freq, dtype=jnp.float32) / n_freq)
    ang = jnp.concatenate([row[:, None] * inv, col[:, None] * inv], axis=-1)
    return jnp.cos(ang), jnp.sin(ang)


def rope_tail(x, cos, sin):
    n_rot = cos.shape[-1] * 2
    x_pass, x_rot = x[..., :x.shape[-1] - n_rot], x[..., x.shape[-1] - n_rot:]
    x1, x2 = jnp.split(x_rot, 2, axis=-1)
    c = cos[None, :, None, :].astype(x.dtype)
    s = sin[None, :, None, :].astype(x.dtype)
    return jnp.concatenate([x_pass, x1 * c - x2 * s, x1 * s + x2 * c], axis=-1)


def hgrn_lower_bound(lb_logits, a_idx):
    lb = jnp.cumsum(jax.nn.softmax(lb_logits.astype(jnp.float32), axis=1), axis=1)[:, a_idx]
    return lb.reshape(2, HG_HEADS, HG_DK)


def hgrn_scan(q, k, v, logf, s0):
    B, N, H, _ = q.shape
    nc = N // HG_CHUNK

    def chunks(a):
        return a.reshape(B, nc, HG_CHUNK, H, a.shape[-1]).transpose(1, 0, 3, 2, 4)

    tri = jnp.tril(jnp.ones((HG_CHUNK, HG_CHUNK), dtype=bool))[:, :, None]

    def step(s, inp):
        qc, kc, vc, gc = inp
        g = jnp.cumsum(gc, axis=2)
        diff = g[:, :, :, None, :] - g[:, :, None, :, :]
        decay = jnp.exp(jnp.where(tri, diff, -jnp.inf))
        scores = jnp.einsum('bhtk,bhtsk,bhsk->bhts', qc, decay, kc)
        o = jnp.einsum('bhtk,bhkv->bhtv', qc * jnp.exp(g), s) + jnp.einsum('bhts,bhsv->bhtv', scores, vc)
        g_end = g[:, :, -1:, :]
        s = jnp.exp(g_end[:, :, 0, :, None]) * s + jnp.einsum('bhsk,bhsv->bhkv', kc * jnp.exp(g_end - g), vc)
        return s, o

    s_fin, o = lax.scan(step, s0, (chunks(q), chunks(k), chunks(v), chunks(logf)))
    return o.transpose(1, 0, 3, 2, 4).reshape(B, N, H, v.shape[-1]), s_fin


def hgrn_bidir(q, f_fwd, f_bwd, v, lb, s0_fwd, s0_bwd):
    q = q.astype(jnp.float32)
    v = v.astype(jnp.float32)

    def gates(f_pre, lb_d):
        f = lb_d + (1.0 - lb_d) * jax.nn.sigmoid(f_pre.astype(jnp.float32))
        return jnp.log(f), 1.0 - f

    logf, kf = gates(f_fwd, lb[0])
    o_f, s_f = hgrn_scan(q, kf, v, logf, s0_fwd.astype(jnp.float32))
    logb, kb = gates(f_bwd, lb[1])
    rev = lambda a: jnp.flip(a, axis=1)
    o_b, s_b = hgrn_scan(rev(q), rev(kb), rev(v), rev(logb), s0_bwd.astype(jnp.float32))
    return o_f + rev(o_b), s_f, s_b


def hgrn_readout(o, g, norm_g):
    B, N = o.shape[:2]
    o = rms_norm(o, norm_g).reshape(B, N, -1)
    return (o * jax.nn.silu(g.astype(jnp.float32))).astype(g.dtype)


def mla_queries(qa, p):
    B, N, _ = qa.shape
    q = (rms_norm(qa, p['qa_norm']) @ p['q_b']).reshape(B, N, MLA_HEADS, MLA_QK)
    return rms_norm(q, p['q_norm'])


def mla_keys_values(ckv, kpe, p):
    B, N, _ = ckv.shape
    kv = (ckv @ p['kv_b']).reshape(B, N, MLA_HEADS, MLA_NOPE + MLA_V)
    k_nope, v = kv[..., :MLA_NOPE], kv[..., MLA_NOPE:]
    k_pe = jnp.broadcast_to(kpe[:, :, None, :], (B, N, MLA_HEADS, MLA_ROPE)).astype(k_nope.dtype)
    return rms_norm(jnp.concatenate([k_nope, k_pe], axis=-1), p['k_norm']), v


def dense_attention(q, k, v, scale):
    B, N, H, d = q.shape
    qb = q.reshape(B, N // Q_BLOCK, Q_BLOCK, H, d).swapaxes(0, 1)

    def one(qblk):
        s = jnp.einsum('bqhd,bkhd->bhqk', qblk, k).astype(jnp.float32) * scale
        pr = jax.nn.softmax(s, axis=-1).astype(v.dtype)
        return jnp.einsum('bhqk,bkhv->bqhv', pr, v)

    o = lax.map(one, qb)
    return o.swapaxes(0, 1).reshape(B, N, H, v.shape[-1])


def mixer_ab_context(h, p, lb):
    B, L, _ = h.shape
    aq, af, abw, ai, ag, qa, kva, kpe = split_cols(h @ p['w_in'], AB_SPLITS)
    hd = lambda t, d: t.reshape(B, L, HG_HEADS, d)
    s_zero = jnp.zeros((B, HG_HEADS, HG_DK, HG_DV), jnp.float32)
    o_hg, s_f, s_b = hgrn_bidir(hd(aq, HG_DK), hd(af, HG_DK), hd(abw, HG_DK), hd(ai, HG_DV), lb, s_zero, s_zero)
    o_a = hgrn_readout(o_hg, ag, p['hg_out_norm'])
    ckv = rms_norm(kva, p['kva_norm'])
    q = mla_queries(qa, p)
    k, v = mla_keys_values(ckv, kpe, p)
    o_b = dense_attention(q, k, v, MLA_SCALE).reshape(B, L, -1)
    out = jnp.concatenate([o_a, o_b.astype(o_a.dtype)], axis=-1) @ p['w_out']
    return out, (s_f, s_b, ckv, kpe)


def mixer_ab_latent(h, cache, p, lb, cos, sin):
    s_f0, s_b0, ckv_ctx, kpe_ctx = cache
    B, N, _ = h.shape
    aq, af, abw, ai, ag, qa, kva, kpe = split_cols(h @ p['w_in'], AB_SPLITS)
    hd = lambda t, d: t.reshape(B, N, HG_HEADS, d)
    o_hg, _, _ = hgrn_bidir(hd(aq, HG_DK), hd(af, HG_DK), hd(abw, HG_DK), hd(ai, HG_DV), lb, s_f0, s_b0)
    o_a = hgrn_readout(o_hg, ag, p['hg_out_norm'])
    q = rope_tail(mla_queries(qa, p), cos, sin)
    k_lat, v_lat = mla_keys_values(rms_norm(kva, p['kva_norm']), kpe, p)
    k_lat = rope_tail(k_lat, cos, sin)
    k_ctx, v_ctx = mla_keys_values(ckv_ctx, kpe_ctx, p)
    k = jnp.concatenate([k_ctx.astype(k_lat.dtype), k_lat], axis=1)
    v = jnp.concatenate([v_ctx.astype(v_lat.dtype), v_lat], axis=1)
    o_b = dense_attention(q, k, v, MLA_SCALE).reshape(B, N, -1)
    return jnp.concatenate([o_a, o_b.astype(o_a.dtype)], axis=-1) @ p['w_out']


def softmax_with_sink(s, sink_f):
    col = jnp.broadcast_to(sink_f[None, :, :, None, None], s.shape[:-1] + (1,))
    return jax.nn.softmax(jnp.concatenate([s, col], axis=-1), axis=-1)[..., :-1]


def c_project(h, p):
    B, N, _ = h.shape
    q, k, v = split_cols(h @ p['w_in'], C_SPLITS)
    q = rms_norm(q.reshape(B, N, SWA_HEADS, SWA_HD), p['q_norm'])
    k = rms_norm(k.reshape(B, N, SWA_KV_HEADS, SWA_HD), p['k_norm'])
    return q, k, v.reshape(B, N, SWA_KV_HEADS, SWA_HD)


def gqa_sink_dense(q, k, v, sink):
    B, L, H, d = q.shape
    G = H // SWA_KV_HEADS
    qb = q.reshape(B, L // Q_BLOCK, Q_BLOCK, SWA_KV_HEADS, G, d).swapaxes(0, 1)
    sink_f = sink.astype(jnp.float32).reshape(SWA_KV_HEADS, G)

    def one(qblk):
        s = jnp.einsum('bqngd,bknd->bngqk', qblk, k).astype(jnp.float32) * SWA_SCALE
        pr = softmax_with_sink(s, sink_f).astype(v.dtype)
        return jnp.einsum('bngqk,bknd->bqngd', pr, v)

    o = lax.map(one, qb)
    return o.swapaxes(0, 1).reshape(B, L, H * d)


def gqa_window_sink(q, k, v, k_ctx, v_ctx, sink):
    B, N, H, d = q.shape
    G = H // SWA_KV_HEADS
    nb = N // SWA_BLOCK
    Lc = k_ctx.shape[1]
    qb = q.reshape(B, nb, SWA_BLOCK, SWA_KV_HEADS, G, d).swapaxes(0, 1)

    def band(t):
        tp = jnp.pad(t, ((0, 0), (SWA_BLOCK, SWA_BLOCK), (0, 0), (0, 0)))
        tp = tp.reshape(B, nb + 2, SWA_BLOCK, SWA_KV_HEADS, d)
        tb = jnp.concatenate([tp[:, :-2], tp[:, 1:-1], tp[:, 2:]], axis=2)
        return tb.swapaxes(0, 1)

    kb, vb = band(k), band(v)
    k_ctx = k_ctx.astype(q.dtype)
    v_ctx = v_ctx.astype(v.dtype)
    qi = jnp.arange(SWA_BLOCK)[:, None]
    kj = jnp.arange(3 * SWA_BLOCK)[None, :]
    in_window = jnp.abs(kj - SWA_BLOCK - qi) <= SWA_WINDOW
    sink_f = sink.astype(jnp.float32).reshape(SWA_KV_HEADS, G)

    def one(args):
        qblk, kblk, vblk, c = args
        kpos = c * SWA_BLOCK - SWA_BLOCK + kj
        valid = in_window & (kpos >= 0) & (kpos < N)
        s_ctx = jnp.einsum('bqngd,bknd->bngqk', qblk, k_ctx).astype(jnp.float32) * SWA_SCALE
        s_loc = jnp.einsum('bqngd,bknd->bngqk', qblk, kblk).astype(jnp.float32) * SWA_SCALE
        s_loc = jnp.where(valid, s_loc, -jnp.inf)
        pr = softmax_with_sink(jnp.concatenate([s_ctx, s_loc], axis=-1), sink_f).astype(v.dtype)
        return (jnp.einsum('bngqk,bknd->bqngd', pr[..., :Lc], v_ctx)
                + jnp.einsum('bngqk,bknd->bqngd', pr[..., Lc:], vblk))

    o = lax.map(one, (qb, kb, vb, jnp.arange(nb)))
    return o.swapaxes(0, 1).reshape(B, N, H * d)


def mixer_c_context(h, p):
    q, k, v = c_project(h, p)
    return gqa_sink_dense(q, k, v, p['sink']) @ p['w_out'], (k, v)


def mixer_c_latent(h, cache, p, cos, sin):
    k_ctx, v_ctx = cache
    q, k, v = c_project(h, p)
    q, k = rope_tail(q, cos, sin), rope_tail(k, cos, sin)
    return gqa_window_sink(q, k, v, k_ctx, v_ctx, p['sink']) @ p['w_out']


def clamped_swiglu(gu):
    gate, up = jnp.split(gu, 2, axis=-1)
    gate = jnp.minimum(gate, SWIGLU_LIMIT)
    up = jnp.clip(up, -SWIGLU_LIMIT, SWIGLU_LIMIT)
    return gate * jax.nn.sigmoid(SWIGLU_ALPHA * gate) * (up + 1.0)


def moe_ffn(h, p):
    shape = h.shape
    x = h.reshape(-1, shape[-1])
    T, D = x.shape
    logits = (x @ p['router_w']).astype(jnp.float32) + p['router_b'].astype(jnp.float32)
    top_logits, top_idx = lax.top_k(logits, TOP_K)
    gates = jax.nn.softmax(top_logits, axis=-1)
    n_assign = T * TOP_K
    flat_e = top_idx.reshape(-1).astype(jnp.int32)
    order = jnp.argsort(flat_e).astype(jnp.int32)
    sorted_e = flat_e[order]
    sorted_tok = order // TOP_K
    counts = jnp.zeros((N_EXPERTS,), jnp.int32).at[flat_e].add(1)
    padded = (counts + MOE_BLOCK - 1) // MOE_BLOCK * MOE_BLOCK
    pad_end = jnp.cumsum(padded)
    pad_start = pad_end - padded
    start = jnp.cumsum(counts) - counts
    dest = pad_start[sorted_e] + jnp.arange(n_assign, dtype=jnp.int32) - start[sorted_e]
    n_blocks = -(-n_assign // MOE_BLOCK) + N_EXPERTS
    row_tok = jnp.full((n_blocks * MOE_BLOCK,), T, jnp.int32).at[dest].set(sorted_tok)
    x_pad = jnp.concatenate([x, jnp.zeros((1, D), x.dtype)], axis=0)
    xb = x_pad[row_tok].reshape(n_blocks, MOE_BLOCK, D)
    block_e = jnp.minimum(jnp.searchsorted(pad_end, jnp.arange(n_blocks, dtype=jnp.int32) * MOE_BLOCK, side='right'), N_EXPERTS - 1)

    def expert_block(args):
        xblk, e = args
        return clamped_swiglu(xblk @ p['w_gu'][e] + p['b_gu'][e]) @ p['w_down'][e] + p['b_down'][e]

    yb = lax.map(expert_block, (xb, block_e)).reshape(n_blocks * MOE_BLOCK, -1)
    dest_of = jnp.zeros((n_assign,), jnp.int32).at[order].set(dest)
    y = yb[dest_of].reshape(T, TOP_K, -1)
    return jnp.einsum('tk,tkd->td', gates.astype(y.dtype), y).reshape(shape)


def setup_inputs(seed: int = 0) -> dict:
    key = jax.random.key(seed)
    ks = iter(jax.random.split(key, 64))

    def nrm(shape, scale=1.0):
        return jax.random.normal(next(ks), shape, jnp.float32) * scale

    def gain(n):
        return 1.0 + nrm((n,), 0.02)

    inp = {}
    inp['x_prompt'] = nrm((BATCH, SEQ, D_MODEL))
    inp['x_sample'] = nrm((DEC_BATCH, DEC_SEQ, D_MODEL))
    inp['state_l0_hgrn_fwd'] = nrm((DEC_BATCH, HG_HEADS, HG_DK, HG_DV), 0.5)
    inp['state_l0_hgrn_bwd'] = nrm((DEC_BATCH, HG_HEADS, HG_DK, HG_DV), 0.5)
    inp['cache_l0_mla_ckv'] = nrm((DEC_BATCH, PAST_LEN, MLA_KV_LORA))
    inp['cache_l0_mla_kpe'] = nrm((DEC_BATCH, PAST_LEN, MLA_ROPE))
    inp['cache_l1_k'] = nrm((DEC_BATCH, PAST_LEN, SWA_KV_HEADS, SWA_HD))
    inp['cache_l1_v'] = nrm((DEC_BATCH, PAST_LEN, SWA_KV_HEADS, SWA_HD))
    inp['c'] = nrm((DEC_BATCH, D_MODEL))
    inp['c_ctx'] = nrm((D_MODEL,))
    inp['hgrn_lb_logits'] = nrm((2, N_HG_LAYERS + 1, HG_HEADS * HG_DK), 0.5)
    for l in range(DEPTH):
        pre = 'l%d_' % l
        inp[pre + 'ada_w'] = nrm((D_MODEL, 6 * D_MODEL), 0.5 * D_MODEL ** -0.5)
        inp[pre + 'ada_b'] = nrm((6 * D_MODEL,), 0.02)
        inp[pre + 'norm1'] = gain(D_MODEL)
        inp[pre + 'norm2'] = gain(D_MODEL)
        if l % 2 == 0:
            inp[pre + 'w_in'] = nrm((D_MODEL, AB_IN), D_MODEL ** -0.5)
            inp[pre + 'hg_out_norm'] = gain(HG_DV)
            inp[pre + 'qa_norm'] = gain(MLA_Q_LORA)
            inp[pre + 'q_b'] = nrm((MLA_Q_LORA, MLA_HEADS * MLA_QK), MLA_Q_LORA ** -0.5)
            inp[pre + 'kva_norm'] = gain(MLA_KV_LORA)
            inp[pre + 'kv_b'] = nrm((MLA_KV_LORA, MLA_HEADS * (MLA_NOPE + MLA_V)), MLA_KV_LORA ** -0.5)
            inp[pre + 'q_norm'] = gain(MLA_QK)
            inp[pre + 'k_norm'] = gain(MLA_QK)
            inp[pre + 'w_out'] = nrm((AB_OUT, D_MODEL), AB_OUT ** -0.5)
        else:
            inp[pre + 'w_in'] = nrm((D_MODEL, C_IN), D_MODEL ** -0.5)
            inp[pre + 'q_norm'] = gain(SWA_HD)
            inp[pre + 'k_norm'] = gain(SWA_HD)
            inp[pre + 'sink'] = nrm((SWA_HEADS,), 0.5)
            inp[pre + 'w_out'] = nrm((C_OUT, D_MODEL), C_OUT ** -0.5)
        inp[pre + 'router_w'] = nrm((D_MODEL, N_EXPERTS), D_MODEL ** -0.5)
        inp[pre + 'router_b'] = nrm((N_EXPERTS,), 0.01)
        inp[pre + 'w_gu'] = nrm((N_EXPERTS, D_MODEL, 2 * D_FF), D_MODEL ** -0.5)
        inp[pre + 'b_gu'] = nrm((N_EXPERTS, 2 * D_FF), 0.01)
        inp[pre + 'w_down'] = nrm((N_EXPERTS, D_FF, D_MODEL), D_FF ** -0.5)
        inp[pre + 'b_down'] = nrm((N_EXPERTS, D_MODEL), 0.01)
    return inp


def reference(x_prompt, x_sample, state_l0_hgrn_fwd, state_l0_hgrn_bwd, cache_l0_mla_ckv, cache_l0_mla_kpe,
              cache_l1_k, cache_l1_v, c, c_ctx, hgrn_lb_logits,
              l0_ada_w, l0_ada_b, l0_norm1, l0_norm2, l0_w_in, l0_hg_out_norm, l0_qa_norm, l0_q_b,
              l0_kva_norm, l0_kv_b, l0_q_norm, l0_k_norm, l0_w_out, l0_router_w, l0_router_b,
              l0_w_gu, l0_b_gu, l0_w_down, l0_b_down,
              l1_ada_w, l1_ada_b, l1_norm1, l1_norm2, l1_w_in, l1_q_norm, l1_k_norm, l1_sink, l1_w_out,
              l1_router_w, l1_router_b, l1_w_gu, l1_b_gu, l1_w_down, l1_b_down):
    layers = [
        dict(ada_w=l0_ada_w, ada_b=l0_ada_b, norm1=l0_norm1, norm2=l0_norm2, w_in=l0_w_in,
             hg_out_norm=l0_hg_out_norm, qa_norm=l0_qa_norm, q_b=l0_q_b, kva_norm=l0_kva_norm,
             kv_b=l0_kv_b, q_norm=l0_q_norm, k_norm=l0_k_norm, w_out=l0_w_out,
             router_w=l0_router_w, router_b=l0_router_b, w_gu=l0_w_gu, b_gu=l0_b_gu,
             w_down=l0_w_down, b_down=l0_b_down),
        dict(ada_w=l1_ada_w, ada_b=l1_ada_b, norm1=l1_norm1, norm2=l1_norm2, w_in=l1_w_in,
             q_norm=l1_q_norm, k_norm=l1_k_norm, sink=l1_sink, w_out=l1_w_out,
             router_w=l1_router_w, router_b=l1_router_b, w_gu=l1_w_gu, b_gu=l1_b_gu,
             w_down=l1_w_down, b_down=l1_b_down),
    ]

    x = x_prompt
    ctx_states = []
    for l in range(DEPTH):
        p = layers[l]
        sh1, sc1, g1, sh2, sc2, g2 = ada_params(c_ctx, p)
        h = modulate(x, p['norm1'], sh1, sc1)
        if l % 2 == 0:
            mix, st = mixer_ab_context(h, p, hgrn_lower_bound(hgrn_lb_logits, l // 2))
        else:
            mix, st = mixer_c_context(h, p)
        x = x + g1 * mix
        x = x + g2 * moe_ffn(modulate(x, p['norm2'], sh2, sc2), p)
        ctx_states.append(st)
    y_prompt = x

    ROWS = x_sample.shape[1] // GRID_W
    cos_b, sin_b = axial_rope_tables(ROWS, MLA_ROPE)
    cos_c, sin_c = axial_rope_tables(ROWS, SWA_HD)
    caches = [(state_l0_hgrn_fwd, state_l0_hgrn_bwd, cache_l0_mla_ckv, cache_l0_mla_kpe),
              (cache_l1_k, cache_l1_v)]
    x = x_sample
    for l in range(DEPTH):
        p = layers[l]
        sh1, sc1, g1, sh2, sc2, g2 = ada_params(c, p)
        h = modulate(x, p['norm1'], sh1, sc1)
        if l % 2 == 0:
            mix = mixer_ab_latent(h, caches[l], p, hgrn_lower_bound(hgrn_lb_logits, l // 2), cos_b, sin_b)
        else:
            mix = mixer_c_latent(h, caches[l], p, cos_c, sin_c)
        x = x + g1 * mix
        x = x + g2 * moe_ffn(modulate(x, p['norm2'], sh2, sc2), p)
    y_sample = x

    (new_hg_fwd, new_hg_bwd, new_mla_ckv, new_mla_kpe), (new_l1_k, new_l1_v) = ctx_states
    return (y_prompt, y_sample, new_hg_fwd, new_hg_bwd, new_mla_ckv, new_mla_kpe, new_l1_k, new_l1_v)
```

```python
import functools

import numpy as np
import jax
import jax.numpy as jnp
from jax import lax
from jax.experimental import pallas as pl
from jax.experimental.pallas import tpu as pltpu

F32 = jnp.float32
BF16 = jnp.bfloat16

D_MODEL = 1024
GRID_W = 64
ROPE_THETA = 10000.0
EPS = 1e-6
HG_HEADS = 4
HG_DK = 128
HG_DV = 128
MLA_HEADS = 8
MLA_NOPE = 64
MLA_ROPE = 32
MLA_V = 64
MLA_QK = MLA_NOPE + MLA_ROPE
MLA_Q_LORA = 256
MLA_KV_LORA = 128
MLA_SCALE = MLA_QK ** -0.5
SWA_HEADS = 16
SWA_KV_HEADS = 4
SWA_HD = 64
SWA_WINDOW = 128
SWA_SCALE = SWA_HD ** -0.5
N_EXPERTS = 32
TOP_K = 4
D_FF = 1024
SWIGLU_LIMIT = 7.0
SWIGLU_ALPHA = 1.702

LANES = 128
HG_CHUNK = 128
HG_LEVELS = 7
AB_IN_PAD = 3072
MOE_ROWS = 256
NEG_BIG = -1e30
VMEM_LIMIT = 48 * 1024 * 1024


def _cparams(sem):
    return pltpu.CompilerParams(dimension_semantics=sem, vmem_limit_bytes=VMEM_LIMIT)


def _dot(a, b):
    return jnp.dot(a, b, preferred_element_type=F32)


def _dot_nt(a, b):
    return lax.dot_general(a, b, (((1,), (1,)), ((), ())), preferred_element_type=F32)


def _dot_tn(a, b):
    return lax.dot_general(a, b, (((0,), (0,)), ((), ())), preferred_element_type=F32)


def _split2(x):
    hi = x.astype(BF16)
    lo = (x - hi.astype(F32)).astype(BF16)
    return hi, lo


def _dot_hp(a, b):
    ah, al = _split2(a)
    bh, bl = _split2(b)
    return _dot(ah, bh) + _dot(ah, bl) + _dot(al, bh)


def _sigmoid(x):
    return 1.0 / (1.0 + jnp.exp(-x))


def _modnorm(x, gain, shift, scale):
    y = x * lax.rsqrt(jnp.mean(x * x, axis=-1, keepdims=True) + EPS)
    return y * gain * (1.0 + scale) + shift


def _ada_kernel(c_ref, w_ref, b_ref, o_ref):
    c = c_ref[...]
    o_ref[...] = _dot_hp(c * _sigmoid(c), w_ref[...]) + b_ref[...]


def ada_params(cond8, w, b):
    n = w.shape[1]
    tn = 1024
    return pl.pallas_call(
        _ada_kernel,
        grid=(n // tn,),
        in_specs=[pl.BlockSpec((8, D_MODEL), lambda j: (0, 0)),
                  pl.BlockSpec((D_MODEL, tn), lambda j: (0, j)),
                  pl.BlockSpec((1, tn), lambda j: (0, j))],
        out_specs=pl.BlockSpec((8, tn), lambda j: (0, j)),
        out_shape=jax.ShapeDtypeStruct((8, n), F32),
        compiler_params=_cparams(("parallel",)),
        name="ada_params",
    )(cond8, w, b.reshape(1, n))


def _modnorm_matmul_kernel(x_ref, g_ref, sh_ref, sc_ref, w_ref, o_ref):
    h = _modnorm(x_ref[...], g_ref[...], sh_ref[0], sc_ref[0])
    o_ref[...] = _dot(h.astype(BF16), w_ref[...])


def modnorm_matmul(x, gain, shift, scale, w_bf16, tokens_per_group, tm=256):
    t, d = x.shape
    n = w_bf16.shape[1]
    tiles_per_group = tokens_per_group // tm
    grp = lambda i: (i // tiles_per_group, 0, 0)
    return pl.pallas_call(
        _modnorm_matmul_kernel,
        grid=(t // tm,),
        in_specs=[pl.BlockSpec((tm, d), lambda i: (i, 0)),
                  pl.BlockSpec((1, d), lambda i: (0, 0)),
                  pl.BlockSpec((1, 1, d), grp),
                  pl.BlockSpec((1, 1, d), grp),
                  pl.BlockSpec((d, n), lambda i: (0, 0))],
        out_specs=pl.BlockSpec((tm, n), lambda i: (i, 0)),
        out_shape=jax.ShapeDtypeStruct((t, n), F32),
        compiler_params=_cparams(("parallel",)),
        name="modnorm_matmul",
    )(x, gain.reshape(1, d), shift, scale, w_bf16)


def _hgrn_constants():
    c = HG_CHUNK
    t = np.arange(c)[:, None]
    u = np.arange(c)[None, :]
    mats = [(u <= t), (u > t)]
    for l in range(HG_LEVELS):
        m = 1 << l
        r = (t // (2 * m)) * (2 * m) + m - 1
        mats.append((u > np.minimum(t, r)) & (u <= np.maximum(t, r)))
    fwd = np.concatenate(mats, axis=0).astype(np.float32)
    bwd = np.concatenate([mm[::-1, ::-1] for mm in mats], axis=0).astype(np.float32)
    x = np.bitwise_xor(t, u)
    lvl = np.where(x > 0, np.floor(np.log2(np.maximum(x, 1))), HG_LEVELS).astype(np.int32)
    lv_f = np.where(t >= u, lvl, -1).astype(np.int32)
    return fwd, bwd, lv_f, lv_f.T.copy()


def _hgrn_direction(q, fpre, v, lb, mcat, lv, st, edge_row):
    c = HG_CHUNK
    f = lb + (1.0 - lb) * _sigmoid(fpre)
    logf = jnp.log(f)
    kk = 1.0 - f
    hi, lo = _split2(logf)
    x2 = _dot(mcat, jnp.concatenate([hi, lo], axis=1))
    xs = x2[:, :c] + x2[:, c:]
    g = xs[0:c]
    g_rest = xs[c:2 * c]
    qb = q.astype(BF16)
    o = _dot_nt((q * jnp.exp(g)).astype(BF16), st.astype(BF16))
    a = jnp.where(lv == HG_LEVELS, _dot_nt(qb, kk.astype(BF16)), 0.0)
    for l in range(HG_LEVELS):
        e = jnp.exp(xs[(2 + l) * c:(3 + l) * c])
        p = _dot_nt((q * e).astype(BF16), (kk * e).astype(BF16))
        a = jnp.where(lv == l, p, a)
    vb = v.astype(BF16)
    o = o + _dot(a.astype(BF16), vb)
    g_edge = g[edge_row:edge_row + 1, :]
    st_new = st * jnp.exp(g_edge) + _dot_tn(vb, (kk * jnp.exp(g_rest)).astype(BF16))
    return o, st_new


def _hgrn_kernel(*refs, has_init):
    if has_init:
        (qf_ref, qb_ref, ff_ref, fb_ref, vf_ref, vb_ref, lbl_ref, mf_ref, mb_ref, lvf_ref, lvb_ref,
         s0f_ref, s0b_ref, of_ref, ob_ref, sf_ref, sb_ref, stf, stb) = refs
    else:
        (qf_ref, qb_ref, ff_ref, fb_ref, vf_ref, vb_ref, lbl_ref, mf_ref, mb_ref, lvf_ref, lvb_ref,
         of_ref, ob_ref, sf_ref, sb_ref, stf, stb) = refs
    c = pl.program_id(2)
    nc = pl.num_programs(2)

    @pl.when(c == 0)
    def _():
        if has_init:
            stf[...] = s0f_ref[0, 0].T
            stb[...] = s0b_ref[0, 0].T
        else:
            stf[...] = jnp.zeros_like(stf)
            stb[...] = jnp.zeros_like(stb)

    rows = [lbl_ref[:, j, :] for j in range(lbl_ref.shape[1])]
    mx = functools.reduce(jnp.maximum, rows)
    ex = [jnp.exp(r - mx) for r in rows]
    lb = ex[0] / functools.reduce(lambda a, b: a + b, ex)

    o_f, st_f = _hgrn_direction(qf_ref[0], ff_ref[0], vf_ref[0], lb[0:1], mf_ref[...], lvf_ref[...],
                                stf[...], HG_CHUNK - 1)
    o_b, st_b = _hgrn_direction(qb_ref[0], fb_ref[0], vb_ref[0], lb[1:2], mb_ref[...], lvb_ref[...],
                                stb[...], 0)
    of_ref[0] = o_f
    ob_ref[0] = o_b
    stf[...] = st_f
    stb[...] = st_b

    @pl.when(c == nc - 1)
    def _():
        sf_ref[0, 0] = st_f.T
        sb_ref[0, 0] = st_b.T


def hgrn_bidir(z, lb_logits, s0f, s0b, batch, seq):
    nc = seq // HG_CHUNK
    z3 = z.reshape(batch, seq, z.shape[1])
    mf, mb, lvf, lvb = _hgrn_constants()
    has_init = s0f is not None
    blk = (1, HG_CHUNK, LANES)
    fwd = lambda off: pl.BlockSpec(blk, lambda b, h, c: (b, c, off + h))
    bwd = lambda off: pl.BlockSpec(blk, lambda b, h, c: (b, nc - 1 - c, off + h))
    full = lambda a: pl.BlockSpec(a.shape, lambda b, h, c: (0,) * a.ndim)
    st_spec = pl.BlockSpec((1, 1, HG_DK, HG_DV), lambda b, h, c: (b, h, 0, 0))
    consts = [jnp.asarray(mf, BF16), jnp.asarray(mb, BF16), jnp.asarray(lvf), jnp.asarray(lvb)]
    in_specs = [fwd(0), bwd(0), fwd(4), bwd(8), fwd(12), bwd(12),
                pl.BlockSpec((2, lb_logits.shape[1], LANES), lambda b, h, c: (0, 0, h))]
    in_specs += [full(a) for a in consts]
    args = [z3] * 6 + [lb_logits] + consts
    if has_init:
        in_specs += [st_spec, st_spec]
        args += [s0f, s0b]
    o_shape = jax.ShapeDtypeStruct((batch, seq, HG_HEADS * HG_DV), F32)
    s_shape = jax.ShapeDtypeStruct((batch, HG_HEADS, HG_DK, HG_DV), F32)
    o_f, o_b, s_f, s_b = pl.pallas_call(
        functools.partial(_hgrn_kernel, has_init=has_init),
        grid=(batch, HG_HEADS, nc),
        in_specs=in_specs,
        out_specs=[pl.BlockSpec(blk, lambda b, h, c: (b, c, h)),
                   pl.BlockSpec(blk, lambda b, h, c: (b, nc - 1 - c, h)),
                   st_spec, st_spec],
        out_shape=[o_shape, o_shape, s_shape, s_shape],
        scratch_shapes=[pltpu.VMEM((HG_DV, HG_DK), F32), pltpu.VMEM((HG_DV, HG_DK), F32)],
        compiler_params=_cparams(("parallel", "parallel", "arbitrary")),
        name="hgrn_bidir",
    )(*args)
    t = batch * seq
    return o_f.reshape(t, -1), o_b.reshape(t, -1), s_f, s_b


def _axial_tables(n_tokens, n_rot):
    t = jnp.arange(n_tokens)
    row = (t // GRID_W).astype(F32)
    col = (t % GRID_W).astype(F32)
    n_freq = n_rot // 4
    inv = jnp.power(ROPE_THETA, -jnp.arange(n_freq, dtype=F32) / n_freq)
    ang = jnp.concatenate([row[:, None] * inv, col[:, None] * inv], axis=-1)
    return jnp.cos(ang), jnp.sin(ang)


def _rope_lane_tables(n_tokens, n_rot, head_width, first_rot_lane):
    cos, sin = _axial_tables(n_tokens, n_rot)
    half = n_rot // 2
    c_head = jnp.ones((n_tokens, head_width), F32)
    sa_head = jnp.zeros((n_tokens, head_width), F32)
    sb_head = jnp.zeros((n_tokens, head_width), F32)
    a0, a1, a2 = first_rot_lane, first_rot_lane + half, first_rot_lane + n_rot
    c_head = c_head.at[:, a0:a1].set(cos).at[:, a1:a2].set(cos)
    sa_head = sa_head.at[:, a0:a1].set(-sin)
    sb_head = sb_head.at[:, a1:a2].set(sin)
    reps = LANES // head_width
    tile = lambda a: jnp.tile(a, (1, reps))
    return tile(c_head), tile(sa_head), tile(sb_head)


def _rope(x, c, sa, sb, half):
    return x * c + pltpu.roll(x, LANES - half, 1) * sa + pltpu.roll(x, half, 1) * sb


def _mla_q_kernel(*refs, rope):
    if rope:
        qa_ref, qan_ref, qb_ref, qn_ref, c_ref, sa_ref, sb_ref, o_ref = refs
    else:
        qa_ref, qan_ref, qb_ref, qn_ref, o_ref = refs
    qa = qa_ref[...]
    qn = qa * lax.rsqrt(jnp.mean(qa * qa, axis=-1, keepdims=True) + EPS) * qan_ref[...]
    qfull = _dot(qn.astype(BF16), qb_ref[...])
    outs = []
    for h in range(MLA_HEADS):
        qh = qfull[:, h * LANES:(h + 1) * LANES]
        ms = jnp.sum(qh * qh, axis=-1, keepdims=True) * (1.0 / MLA_QK)
        qh = qh * lax.rsqrt(ms + EPS) * qn_ref[...]
        if rope:
            qh = _rope(qh, c_ref[...], sa_ref[...], sb_ref[...], MLA_ROPE // 2)
        outs.append((qh * MLA_SCALE).astype(BF16))
    o_ref[...] = jnp.concatenate(outs, axis=1)


def mla_queries(z, qa_norm, q_b_pad, q_norm_pad, tables, tokens_per_batch, tm=256):
    t = z.shape[0]
    rope = tables is not None
    row = lambda i: (0, 0)
    in_specs = [pl.BlockSpec((tm, MLA_Q_LORA), lambda i: (i, 2560 // MLA_Q_LORA)),
                pl.BlockSpec((1, MLA_Q_LORA), row),
                pl.BlockSpec(q_b_pad.shape, row),
                pl.BlockSpec((1, LANES), row)]
    args = [z, qa_norm.reshape(1, -1), q_b_pad, q_norm_pad]
    if rope:
        tpb = tokens_per_batch // tm
        in_specs += [pl.BlockSpec((tm, LANES), lambda i: (i % tpb, 0))] * 3
        args += list(tables)
    return pl.pallas_call(
        functools.partial(_mla_q_kernel, rope=rope),
        grid=(t // tm,),
        in_specs=in_specs,
        out_specs=pl.BlockSpec((tm, MLA_HEADS * LANES), lambda i: (i, 0)),
        out_shape=jax.ShapeDtypeStruct((t, MLA_HEADS * LANES), BF16),
        compiler_params=_cparams(("parallel",)),
        name="mla_queries",
    )(*args)


def _mla_kv_kernel(*refs, norm_input, rope):
    if rope:
        kva_ref, kpe_ref, kvan_ref, wk_ref, wv_ref, kn_ref, c_ref, sa_ref, sb_ref, ckv_ref, k_ref, v_ref = refs
    else:
        kva_ref, kpe_ref, kvan_ref, wk_ref, wv_ref, kn_ref, ckv_ref, k_ref, v_ref = refs
    ckv = kva_ref[...]
    if norm_input:
        ckv = ckv * lax.rsqrt(jnp.mean(ckv * ckv, axis=-1, keepdims=True) + EPS) * kvan_ref[...]
    ckv_ref[...] = ckv
    cb = ckv.astype(BF16)
    knope = _dot(cb, wk_ref[...])
    v_ref[...] = _dot(cb, wv_ref[...]).astype(BF16)
    kpe = pltpu.roll(kpe_ref[...], MLA_NOPE, 1)
    outs = []
    for h in range(MLA_HEADS):
        kh = knope[:, h * LANES:(h + 1) * LANES] + kpe
        ms = jnp.sum(kh * kh, axis=-1, keepdims=True) * (1.0 / MLA_QK)
        kh = kh * lax.rsqrt(ms + EPS) * kn_ref[...]
        if rope:
            kh = _rope(kh, c_ref[...], sa_ref[...], sb_ref[...], MLA_ROPE // 2)
        outs.append(kh.astype(BF16))
    k_ref[...] = jnp.concatenate(outs, axis=1)


def mla_keys_values(kva_src, kva_col, kpe_src, kpe_col, kva_norm, wk_pad, wv, k_norm_pad, tables,
                    tokens_per_batch, norm_input, tm=256):
    t = kva_src.shape[0]
    rope = tables is not None
    row = lambda i: (0, 0)
    in_specs = [pl.BlockSpec((tm, LANES), lambda i: (i, kva_col)),
                pl.BlockSpec((tm, LANES), lambda i: (i, kpe_col)),
                pl.BlockSpec((1, LANES), row),
                pl.BlockSpec(wk_pad.shape, row),
                pl.BlockSpec(wv.shape, row),
                pl.BlockSpec((1, LANES), row)]
    args = [kva_src, kpe_src, kva_norm.reshape(1, -1), wk_pad, wv, k_norm_pad]
    if rope:
        tpb = tokens_per_batch // tm
        in_specs += [pl.BlockSpec((tm, LANES), lambda i: (i % tpb, 0))] * 3
        args += list(tables)
    return pl.pallas_call(
        functools.partial(_mla_kv_kernel, norm_input=norm_input, rope=rope),
        grid=(t // tm,),
        in_specs=in_specs,
        out_specs=[pl.BlockSpec((tm, LANES), lambda i: (i, 0)),
                   pl.BlockSpec((tm, MLA_HEADS * LANES), lambda i: (i, 0)),
                   pl.BlockSpec((tm, MLA_HEADS * MLA_V), lambda i: (i, 0))],
        out_shape=[jax.ShapeDtypeStruct((t, LANES), F32),
                   jax.ShapeDtypeStruct((t, MLA_HEADS * LANES), BF16),
                   jax.ShapeDtypeStruct((t, MLA_HEADS * MLA_V), BF16)],
        compiler_params=_cparams(("parallel",)),
        name="mla_keys_values",
    )(*args)


def _mla_attn_kernel(q_ref, k_ref, v_ref, o_ref):
    outs = []
    for j in range(2):
        q = q_ref[0][:, j * LANES:(j + 1) * LANES]
        k = k_ref[0][:, j * LANES:(j + 1) * LANES]
        v = v_ref[0][:, j * MLA_V:(j + 1) * MLA_V]
        s = _dot_nt(q, k)
        p = jnp.exp(s - jnp.max(s, axis=-1, keepdims=True))
        l = jnp.sum(p, axis=-1, keepdims=True)
        outs.append(_dot(p.astype(BF16), v) / l)
    o_ref[0] = jnp.concatenate(outs, axis=1)


def mla_attention(q, k, v, batch, n_q, n_k, tq=256):
    q3 = q.reshape(batch, n_q, -1)
    k3 = k.reshape(batch, n_k, -1)
    v3 = v.reshape(batch, n_k, -1)
    out = pl.pallas_call(
        _mla_attn_kernel,
        grid=(batch, MLA_HEADS // 2, n_q // tq),
        in_specs=[pl.BlockSpec((1, tq, 2 * LANES), lambda b, j, i: (b, i, j)),
                  pl.BlockSpec((1, n_k, 2 * LANES), lambda b, j, i: (b, 0, j)),
                  pl.BlockSpec((1, n_k, 2 * MLA_V), lambda b, j, i: (b, 0, j))],
        out_specs=pl.BlockSpec((1, tq, 2 * MLA_V), lambda b, j, i: (b, i, j)),
        out_shape=jax.ShapeDtypeStruct((batch, n_q, MLA_HEADS * MLA_V), F32),
        compiler_params=_cparams(("parallel", "parallel", "arbitrary")),
        name="mla_attention",
    )(q3, k3, v3)
    return out.reshape(batch * n_q, -1)


def _out0_kernel(of_ref, ob_ref, ag_ref, hgn_ref, om_ref, w_ref, x_ref, g_ref, o_ref):
    o = of_ref[...] + ob_ref[...]
    ag = ag_ref[...]
    parts = []
    for h in range(HG_HEADS):
        oh = o[:, h * HG_DV:(h + 1) * HG_DV]
        oh = oh * lax.rsqrt(jnp.mean(oh * oh, axis=-1, keepdims=True) + EPS) * hgn_ref[...]
        gh = ag[:, h * HG_DV:(h + 1) * HG_DV]
        parts.append((oh * (gh * _sigmoid(gh))).astype(BF16))
    oa = jnp.concatenate(parts, axis=1)
    n_a = HG_HEADS * HG_DV
    mix = _dot(oa, w_ref[0:n_a, :]) + _dot(om_ref[...].astype(BF16), w_ref[n_a:, :])
    o_ref[...] = x_ref[...] + g_ref[0] * mix


def out_proj_layer0(o_f, o_b, z, hg_norm, o_mla, w_out_bf16, x, gate, tokens_per_group, tm=256):
    t, d = x.shape
    n_a = HG_HEADS * HG_DV
    tiles_per_group = tokens_per_group // tm
    tile = lambda w: pl.BlockSpec((tm, w), lambda i: (i, 0))
    return pl.pallas_call(
        _out0_kernel,
        grid=(t // tm,),
        in_specs=[tile(n_a), tile(n_a),
                  pl.BlockSpec((tm, n_a), lambda i: (i, 2048 // n_a)),
                  pl.BlockSpec((1, HG_DV), lambda i: (0, 0)),
                  tile(o_mla.shape[1]),
                  pl.BlockSpec(w_out_bf16.shape, lambda i: (0, 0)),
                  tile(d),
                  pl.BlockSpec((1, 1, d), lambda i: (i // tiles_per_group, 0, 0))],
        out_specs=tile(d),
        out_shape=jax.ShapeDtypeStruct((t, d), F32),
        compiler_params=_cparams(("parallel",)),
        name="out_proj_layer0",
    )(o_f, o_b, z, hg_norm.reshape(1, -1), o_mla, w_out_bf16, x, gate)


def _out1_kernel(a_ref, w_ref, x_ref, g_ref, o_ref):
    o_ref[...] = x_ref[...] + g_ref[0] * _dot(a_ref[...].astype(BF16), w_ref[...])


def out_proj_layer1(a, w_out_bf16, x, gate, tokens_per_group, tm=256):
    t, d = x.shape
    tiles_per_group = tokens_per_group // tm
    tile = lambda w: pl.BlockSpec((tm, w), lambda i: (i, 0))
    return pl.pallas_call(
        _out1_kernel,
        grid=(t // tm,),
        in_specs=[tile(a.shape[1]),
                  pl.BlockSpec(w_out_bf16.shape, lambda i: (0, 0)),
                  tile(d),
                  pl.BlockSpec((1, 1, d), lambda i: (i // tiles_per_group, 0, 0))],
        out_specs=tile(d),
        out_shape=jax.ShapeDtypeStruct((t, d), F32),
        compiler_params=_cparams(("parallel",)),
        name="out_proj_layer1",
    )(a, w_out_bf16, x, gate)


def _head_rms(x, gain2):
    sq = x * x
    lane = lax.broadcasted_iota(jnp.int32, x.shape, 1)
    first = lane < SWA_HD
    lo = jnp.sum(jnp.where(first, sq, 0.0), axis=-1, keepdims=True)
    hi = jnp.sum(jnp.where(first, 0.0, sq), axis=-1, keepdims=True)
    ms = jnp.where(first, lo, hi) * (1.0 / SWA_HD)
    return x * lax.rsqrt(ms + EPS) * gain2


def _swa_prep_kernel(*refs, rope):
    if rope:
        zq_ref, zk_ref, zv_ref, qn_ref, kn_ref, c_ref, sa_ref, sb_ref, q_ref, kc_ref, k_ref, v_ref = refs
    else:
        zq_ref, zk_ref, zv_ref, qn_ref, kn_ref, q_ref, kc_ref, k_ref, v_ref = refs
    half = SWA_HD // 2

    def rot(x):
        return _rope(x, c_ref[...], sa_ref[...], sb_ref[...], half) if rope else x

    zq = zq_ref[...]
    qs = []
    for p in range(zq.shape[1] // LANES):
        x = _head_rms(zq[:, p * LANES:(p + 1) * LANES], qn_ref[...])
        qs.append((rot(x) * SWA_SCALE).astype(BF16))
    q_ref[...] = jnp.concatenate(qs, axis=1)
    zk = zk_ref[...]
    kn, kr = [], []
    for p in range(zk.shape[1] // LANES):
        x = _head_rms(zk[:, p * LANES:(p + 1) * LANES], kn_ref[...])
        kn.append(x)
        kr.append(rot(x).astype(BF16))
    kc_ref[...] = jnp.concatenate(kn, axis=1)
    k_ref[...] = jnp.concatenate(kr, axis=1)
    v_ref[...] = zv_ref[...].astype(BF16)


def swa_prep(z, q_norm2, k_norm2, tables, tokens_per_batch, tm=256):
    t = z.shape[0]
    nq = SWA_HEADS * SWA_HD
    nkv = SWA_KV_HEADS * SWA_HD
    rope = tables is not None
    row = lambda i: (0, 0)
    in_specs = [pl.BlockSpec((tm, nq), lambda i: (i, 0)),
                pl.BlockSpec((tm, nkv), lambda i: (i, nq // nkv)),
                pl.BlockSpec((tm, nkv), lambda i: (i, nq // nkv + 1)),
                pl.BlockSpec((1, LANES), row),
                pl.BlockSpec((1, LANES), row)]
    args = [z, z, z, q_norm2, k_norm2]
    if rope:
        tpb = tokens_per_batch // tm
        in_specs += [pl.BlockSpec((tm, LANES), lambda i: (i % tpb, 0))] * 3
        args += list(tables)
    tile = lambda w: pl.BlockSpec((tm, w), lambda i: (i, 0))
    return pl.pallas_call(
        functools.partial(_swa_prep_kernel, rope=rope),
        grid=(t // tm,),
        in_specs=in_specs,
        out_specs=[tile(nq), tile(nkv), tile(nkv), tile(nkv)],
        out_shape=[jax.ShapeDtypeStruct((t, nq), BF16),
                   jax.ShapeDtypeStruct((t, nkv), F32),
                   jax.ShapeDtypeStruct((t, nkv), BF16),
                   jax.ShapeDtypeStruct((t, nkv), BF16)],
        compiler_params=_cparams(("parallel",)),
        name="swa_prep",
    )(*args)


def _sink_attention(q8, k_all, v_all, bias, sink_ref, pair):
    outs = []
    group = SWA_HEADS // SWA_KV_HEADS
    for n in range(2):
        kn = k_all[:, n * SWA_HD:(n + 1) * SWA_HD]
        vn = v_all[:, n * SWA_HD:(n + 1) * SWA_HD]
        for g in range(group):
            hq = n * group + g
            s = _dot_nt(q8[:, hq * SWA_HD:(hq + 1) * SWA_HD], kn)
            if bias is not None:
                s = s + bias
            sk = sink_ref[pair * 2 * group + hq]
            m = jnp.maximum(jnp.max(s, axis=-1, keepdims=True), sk)
            p = jnp.exp(s - m)
            l = jnp.sum(p, axis=-1, keepdims=True) + jnp.exp(sk - m)
            outs.append(_dot(p.astype(BF16), vn) / l)
    return jnp.concatenate(outs, axis=1)


def _swa_dense_kernel(sink_ref, q_ref, k_ref, v_ref, o_ref):
    o_ref[0] = _sink_attention(q_ref[0], k_ref[0], v_ref[0], None, sink_ref, pl.program_id(1))


def swa_dense(q, k, v, sink, batch, seq):
    nq = SWA_HEADS * SWA_HD
    q3 = q.reshape(batch, seq, nq)
    k3 = k.reshape(batch, seq, -1)
    v3 = v.reshape(batch, seq, -1)
    out = pl.pallas_call(
        _swa_dense_kernel,
        grid=(batch, 2),
        in_specs=[pl.BlockSpec(memory_space=pltpu.SMEM),
                  pl.BlockSpec((1, seq, nq // 2), lambda b, j: (b, 0, j)),
                  pl.BlockSpec((1, seq, LANES), lambda b, j: (b, 0, j)),
                  pl.BlockSpec((1, seq, LANES), lambda b, j: (b, 0, j))],
        out_specs=pl.BlockSpec((1, seq, nq // 2), lambda b, j: (b, 0, j)),
        out_shape=jax.ShapeDtypeStruct((batch, seq, nq), F32),
        compiler_params=_cparams(("parallel", "parallel")),
        name="swa_dense",
    )(sink, q3, k3, v3)
    return out.reshape(batch * seq, nq)


def _swa_window_kernel(sink_ref, q_ref, kc_ref, vc_ref, kp_ref, k0_ref, kn_ref, vp_ref, v0_ref, vn_ref, o_ref):
    i = pl.program_id(2)
    nb = pl.num_programs(2)
    w = SWA_WINDOW
    n_ctx = kc_ref.shape[1]
    k_all = jnp.concatenate([kc_ref[0], kp_ref[0], k0_ref[0], kn_ref[0]], axis=0)
    v_all = jnp.concatenate([vc_ref[0], vp_ref[0], v0_ref[0], vn_ref[0]], axis=0)
    r = lax.broadcasted_iota(jnp.int32, (w, n_ctx + 3 * w), 0)
    col = lax.broadcasted_iota(jnp.int32, (w, n_ctx + 3 * w), 1) - n_ctx
    prev_bias = jnp.where(i > 0, 0.0, NEG_BIG)
    next_bias = jnp.where(i < nb - 1, 0.0, NEG_BIG)
    bias = jnp.where(
        col < w,
        jnp.where(col < 0, 0.0, jnp.where(col >= r, prev_bias, NEG_BIG)),
        jnp.where(col < 2 * w, 0.0, jnp.where(col - 2 * w <= r, next_bias, NEG_BIG)))
    o_ref[0] = _sink_attention(q_ref[0], k_all, v_all, bias, sink_ref, pl.program_id(1))


def swa_window(q, k, v, k_ctx, v_ctx, sink, batch, seq, n_ctx):
    nq = SWA_HEADS * SWA_HD
    w = SWA_WINDOW
    nb = seq // w
    q3 = q.reshape(batch, seq, nq)
    k3 = k.reshape(batch, seq, -1)
    v3 = v.reshape(batch, seq, -1)
    ctx = pl.BlockSpec((1, n_ctx, LANES), lambda b, j, i: (b, 0, j))
    prv = pl.BlockSpec((1, w, LANES), lambda b, j, i: (b, jnp.maximum(i - 1, 0), j))
    cur = pl.BlockSpec((1, w, LANES), lambda b, j, i: (b, i, j))
    nxt = pl.BlockSpec((1, w, LANES), lambda b, j, i: (b, jnp.minimum(i + 1, nb - 1), j))
    out = pl.pallas_call(
        _swa_window_kernel,
        grid=(batch, 2, nb),
        in_specs=[pl.BlockSpec(memory_space=pltpu.SMEM),
                  pl.BlockSpec((1, w, nq // 2), lambda b, j, i: (b, i, j)),
                  ctx, ctx, prv, cur, nxt, prv, cur, nxt],
        out_specs=pl.BlockSpec((1, w, nq // 2), lambda b, j, i: (b, i, j)),
        out_shape=jax.ShapeDtypeStruct((batch, seq, nq), F32),
        compiler_params=_cparams(("parallel", "parallel", "arbitrary")),
        name="swa_window",
    )(sink, q3, k_ctx, v_ctx, k3, k3, k3, v3, v3, v3)
    return out.reshape(batch * seq, nq)


def _router_kernel(x_ref, g_ref, sh_ref, sc_ref, rw_ref, rb_ref, h_ref, idx_ref, gate_ref):
    h = _modnorm(x_ref[...], g_ref[...], sh_ref[0], sc_ref[0])
    h_ref[...] = h
    logits = _dot_hp(h, rw_ref[...]) + rb_ref[...]
    lane = lax.broadcasted_iota(jnp.int32, logits.shape, 1)
    work = logits
    vals, idxs = [], []
    for _ in range(TOP_K):
        m = jnp.max(work, axis=-1, keepdims=True)
        ik = jnp.min(jnp.where(work == m, lane, LANES), axis=-1, keepdims=True)
        vals.append(m)
        idxs.append(ik)
        work = jnp.where(lane == ik, 2.0 * NEG_BIG, work)
    es = [jnp.exp(v - vals[0]) for v in vals]
    denom = es[0] + es[1] + es[2] + es[3]
    idx_out = jnp.zeros(logits.shape, jnp.int32)
    gate_out = jnp.zeros(logits.shape, F32)
    for k in range(TOP_K):
        idx_out = jnp.where(lane == k, idxs[k], idx_out)
        gate_out = jnp.where(lane == k, es[k] / denom, gate_out)
    idx_ref[...] = idx_out
    gate_ref[...] = gate_out


def moe_router(x, gain, shift, scale, rw_pad, rb_pad, tokens_per_group, tm=256):
    t, d = x.shape
    tiles_per_group = tokens_per_group // tm
    grp = lambda i: (i // tiles_per_group, 0, 0)
    tile = lambda w: pl.BlockSpec((tm, w), lambda i: (i, 0))
    return pl.pallas_call(
        _router_kernel,
        grid=(t // tm,),
        in_specs=[tile(d),
                  pl.BlockSpec((1, d), lambda i: (0, 0)),
                  pl.BlockSpec((1, 1, d), grp),
                  pl.BlockSpec((1, 1, d), grp),
                  pl.BlockSpec((d, LANES), lambda i: (0, 0)),
                  pl.BlockSpec((1, LANES), lambda i: (0, 0))],
        out_specs=[tile(d), tile(LANES), tile(LANES)],
        out_shape=[jax.ShapeDtypeStruct((t, d), F32),
                   jax.ShapeDtypeStruct((t, LANES), jnp.int32),
                   jax.ShapeDtypeStruct((t, LANES), F32)],
        compiler_params=_cparams(("parallel",)),
        name="moe_router",
    )(x, gain.reshape(1, d), shift, scale, rw_pad, rb_pad)


def _row_gather_start(src_hbm, dst_rows, sem, row_of, n_rows):
    def body(r, carry):
        pltpu.make_async_copy(src_hbm.at[pl.ds(row_of(r), 1), :], dst_rows.at[pl.ds(r, 1), :], sem).start()
        return carry
    lax.fori_loop(0, n_rows, body, 0, unroll=8)


def _expert_kernel(be_ref, rt_ref, nu_ref, h_hbm, wgu_ref, bgu_ref, wd_ref, bd_ref, o_ref, buf, sem):
    b = pl.program_id(0)
    n_used = nu_ref[0]
    slot = b % 2

    def start(blk, s):
        _row_gather_start(h_hbm, buf.at[s], sem.at[s], lambda r: rt_ref[blk * MOE_ROWS + r], MOE_ROWS)

    @pl.when(b == 0)
    def _():
        start(0, 0)

    @pl.when(b + 1 < n_used)
    def _():
        start(b + 1, 1 - slot)

    @pl.when(b < n_used)
    def _():
        pltpu.make_async_copy(buf.at[slot], buf.at[slot], sem.at[slot]).wait()
        xb = buf[slot].astype(BF16)
        gu = _dot(xb, wgu_ref[0]) + bgu_ref[0]
        gate = jnp.minimum(gu[:, :D_FF], SWIGLU_LIMIT)
        up = jnp.clip(gu[:, D_FF:], -SWIGLU_LIMIT, SWIGLU_LIMIT)
        act = gate * _sigmoid(SWIGLU_ALPHA * gate) * (up + 1.0)
        o_ref[...] = _dot(act.astype(BF16), wd_ref[0]) + bd_ref[0]

    @pl.when(b >= n_used)
    def _():
        o_ref[...] = jnp.zeros_like(o_ref)


def moe_experts(h, block_e, row_tok, n_used, w_gu, b_gu, w_down, b_down):
    n_blocks = block_e.shape[0]
    d = h.shape[1]
    grid_spec = pltpu.PrefetchScalarGridSpec(
        num_scalar_prefetch=3,
        grid=(n_blocks,),
        in_specs=[pl.BlockSpec(memory_space=pl.ANY),
                  pl.BlockSpec((1, d, 2 * D_FF), lambda b, be, rt, nu: (be[b], 0, 0)),
                  pl.BlockSpec((1, 1, 2 * D_FF), lambda b, be, rt, nu: (be[b], 0, 0)),
                  pl.BlockSpec((1, D_FF, d), lambda b, be, rt, nu: (be[b], 0, 0)),
                  pl.BlockSpec((1, 1, d), lambda b, be, rt, nu: (be[b], 0, 0))],
        out_specs=pl.BlockSpec((MOE_ROWS, d), lambda b, be, rt, nu: (b, 0)),
        scratch_shapes=[pltpu.VMEM((2, MOE_ROWS, d), F32), pltpu.SemaphoreType.DMA((2,))],
    )
    return pl.pallas_call(
        _expert_kernel,
        grid_spec=grid_spec,
        out_shape=jax.ShapeDtypeStruct((n_blocks * MOE_ROWS, d), F32),
        compiler_params=_cparams(("arbitrary",)),
        name="moe_experts",
    )(block_e, row_tok, n_used, h, w_gu, b_gu.reshape(N_EXPERTS, 1, -1), w_down, b_down.reshape(N_EXPERTS, 1, -1))


COMBINE_TM = 128


def _combine_kernel(dest_ref, y_hbm, gate_ref, x_ref, g_ref, o_ref, buf, sem):
    i = pl.program_id(0)
    n = pl.num_programs(0)
    slot = i % 2

    def start(tile, s):
        for k in range(TOP_K):
            _row_gather_start(y_hbm, buf.at[s, k], sem.at[s],
                              lambda r: dest_ref[(tile * COMBINE_TM + r) * TOP_K + k], COMBINE_TM)

    @pl.when(i == 0)
    def _():
        start(0, 0)

    @pl.when(i + 1 < n)
    def _():
        start(i + 1, 1 - slot)

    pltpu.make_async_copy(buf.at[slot], buf.at[slot], sem.at[slot]).wait()
    gates = gate_ref[...]
    acc = gates[:, 0:1] * buf[slot, 0]
    for k in range(1, TOP_K):
        acc = acc + gates[:, k:k + 1] * buf[slot, k]
    o_ref[...] = x_ref[...] + g_ref[0] * acc


def moe_combine(yb, dest, gates, x, gate_mod, tokens_per_group):
    t, d = x.shape
    tm = COMBINE_TM
    tiles_per_group = tokens_per_group // tm
    grid_spec = pltpu.PrefetchScalarGridSpec(
        num_scalar_prefetch=1,
        grid=(t // tm,),
        in_specs=[pl.BlockSpec(memory_space=pl.ANY),
                  pl.BlockSpec((tm, LANES), lambda i, de: (i, 0)),
                  pl.BlockSpec((tm, d), lambda i, de: (i, 0)),
                  pl.BlockSpec((1, 1, d), lambda i, de: (i // tiles_per_group, 0, 0))],
        out_specs=pl.BlockSpec((tm, d), lambda i, de: (i, 0)),
        scratch_shapes=[pltpu.VMEM((2, TOP_K, tm, d), F32), pltpu.SemaphoreType.DMA((2,))],
    )
    return pl.pallas_call(
        _combine_kernel,
        grid_spec=grid_spec,
        out_shape=jax.ShapeDtypeStruct((t, d), F32),
        compiler_params=_cparams(("arbitrary",)),
        name="moe_combine",
    )(dest, yb, gates, x, gate_mod)


def _assignment_tables(top_idx):
    t = top_idx.shape[0]
    n_assign = t * TOP_K
    flat_e = top_idx.reshape(-1)
    onehot = (flat_e[:, None] == jnp.arange(N_EXPERTS, dtype=jnp.int32)[None, :]).astype(jnp.int32)
    csum = jnp.cumsum(onehot, axis=0)
    counts = csum[-1]
    rank = jnp.sum(csum * onehot, axis=1) - 1
    padded = (counts + MOE_ROWS - 1) // MOE_ROWS * MOE_ROWS
    pad_end = jnp.cumsum(padded)
    pad_start = pad_end - padded
    dest = (pad_start[flat_e] + rank).astype(jnp.int32)
    n_blocks = n_assign // MOE_ROWS + N_EXPERTS
    row_tok = jnp.zeros((n_blocks * MOE_ROWS,), jnp.int32).at[dest].set(
        jnp.arange(n_assign, dtype=jnp.int32) // TOP_K)
    block_e = jnp.minimum(
        jnp.searchsorted(pad_end, jnp.arange(n_blocks, dtype=jnp.int32) * MOE_ROWS, side='right'),
        N_EXPERTS - 1).astype(jnp.int32)
    n_used = (pad_end[-1:] // MOE_ROWS).astype(jnp.int32)
    return dest, row_tok, block_e, n_used


def moe_layer(x, p, shift, scale, gate_mod, tokens_per_group):
    h, idx, gates = moe_router(x, p['norm2'], shift, scale, p['rw_pad'], p['rb_pad'], tokens_per_group)
    dest, row_tok, block_e, n_used = _assignment_tables(idx[:, :TOP_K])
    yb = moe_experts(h, block_e, row_tok, n_used, p['w_gu'], p['b_gu'], p['w_down'], p['b_down'])
    return moe_combine(yb, dest, gates, x, gate_mod, tokens_per_group)


def _pad_lanes(a, width):
    return jnp.pad(a, [(0, 0)] * (a.ndim - 1) + [(0, width - a.shape[-1])])


def _prep_common(p):
    p['rw_pad'] = _pad_lanes(p['router_w'], LANES)
    p['rb_pad'] = jnp.concatenate(
        [p['router_b'].astype(F32), jnp.full((LANES - N_EXPERTS,), NEG_BIG, F32)]).reshape(1, LANES)
    p['w_gu'] = p['w_gu'].astype(BF16)
    p['w_down'] = p['w_down'].astype(BF16)
    p['w_out'] = p['w_out'].astype(BF16)
    return p


def _prep_layer0(p):
    p = _prep_common(dict(p))
    p['w_in'] = _pad_lanes(p['w_in'], AB_IN_PAD).astype(BF16)
    q_b = p['q_b'].reshape(MLA_Q_LORA, MLA_HEADS, MLA_QK)
    p['q_b_pad'] = _pad_lanes(q_b, LANES).reshape(MLA_Q_LORA, MLA_HEADS * LANES).astype(BF16)
    kv_b = p['kv_b'].reshape(MLA_KV_LORA, MLA_HEADS, MLA_NOPE + MLA_V)
    p['wk_pad'] = _pad_lanes(kv_b[:, :, :MLA_NOPE], LANES).reshape(MLA_KV_LORA, MLA_HEADS * LANES).astype(BF16)
    p['wv'] = kv_b[:, :, MLA_NOPE:].reshape(MLA_KV_LORA, MLA_HEADS * MLA_V).astype(BF16)
    p['q_norm_pad'] = _pad_lanes(p['q_norm'].reshape(1, -1), LANES)
    p['k_norm_pad'] = _pad_lanes(p['k_norm'].reshape(1, -1), LANES)
    return p


def _prep_layer1(p):
    p = _prep_common(dict(p))
    p['w_in'] = p['w_in'].astype(BF16)
    p['q_norm2'] = jnp.tile(p['q_norm'].reshape(1, -1), (1, LANES // SWA_HD))
    p['k_norm2'] = jnp.tile(p['k_norm'].reshape(1, -1), (1, LANES // SWA_HD))
    return p


def _group_forward(x3, mods, p0, p1, lb_logits, caches, latent):
    batch, seq, d = x3.shape
    t = batch * seq
    x = x3.reshape(t, d)
    tpg = seq if latent else t
    sh1, sc1, g1, sh2, sc2, g2 = mods[0]

    z = modnorm_matmul(x, p0['norm1'], sh1, sc1, p0['w_in'], tpg)
    tab_b = _rope_lane_tables(seq, MLA_ROPE, LANES, MLA_NOPE) if latent else None
    s0f, s0b = (caches['hg_f'], caches['hg_b']) if latent else (None, None)
    o_f, o_b, s_f, s_b = hgrn_bidir(z, lb_logits, s0f, s0b, batch, seq)
    q = mla_queries(z, p0['qa_norm'], p0['q_b_pad'], p0['q_norm_pad'], tab_b, seq)
    ckv, k, v = mla_keys_values(z, 2816 // LANES, z, 2944 // LANES, p0['kva_norm'], p0['wk_pad'], p0['wv'],
                                p0['k_norm_pad'], tab_b, seq, norm_input=True)
    n_k = seq
    if latent:
        n_ctx = caches['ckv'].shape[1]
        ckv_c = caches['ckv'].reshape(batch * n_ctx, MLA_KV_LORA)
        kpe_c = _pad_lanes(caches['kpe'].reshape(batch * n_ctx, MLA_ROPE), LANES)
        _, k_c, v_c = mla_keys_values(ckv_c, 0, kpe_c, 0, p0['kva_norm'], p0['wk_pad'], p0['wv'],
                                      p0['k_norm_pad'], None, n_ctx, norm_input=False)
        cat = lambda a, b: jnp.concatenate([a.reshape(batch, n_ctx, -1), b.reshape(batch, seq, -1)],
                                           axis=1).reshape(batch * (n_ctx + seq), -1)
        k, v = cat(k_c, k), cat(v_c, v)
        n_k = n_ctx + seq
    o_mla = mla_attention(q, k, v, batch, seq, n_k)
    x = out_proj_layer0(o_f, o_b, z, p0['hg_out_norm'], o_mla, p0['w_out'], x, g1, tpg)
    x = moe_layer(x, p0, sh2, sc2, g2, tpg)
    state0 = (s_f, s_b, ckv.reshape(batch, seq, MLA_KV_LORA), z[:, 2944:2944 + MLA_ROPE].reshape(batch, seq, MLA_ROPE))

    sh1, sc1, g1, sh2, sc2, g2 = mods[1]
    z = modnorm_matmul(x, p1['norm1'], sh1, sc1, p1['w_in'], tpg)
    tab_c = _rope_lane_tables(seq, SWA_HD, SWA_HD, 0) if latent else None
    q, k_cache, k, v = swa_prep(z, p1['q_norm2'], p1['k_norm2'], tab_c, seq)
    sink = p1['sink'].astype(F32)
    if latent:
        n_ctx = caches['k1'].shape[1]
        k_c = caches['k1'].reshape(batch, n_ctx, -1).astype(BF16)
        v_c = caches['v1'].reshape(batch, n_ctx, -1).astype(BF16)
        a = swa_window(q, k, v, k_c, v_c, sink, batch, seq, n_ctx)
    else:
        a = swa_dense(q, k, v, sink, batch, seq)
    x = out_proj_layer1(a, p1['w_out'], x, g1, tpg)
    x = moe_layer(x, p1, sh2, sc2, g2, tpg)
    nkv = SWA_KV_HEADS * SWA_HD
    state1 = (k_cache.reshape(batch, seq, SWA_KV_HEADS, SWA_HD),
              z[:, SWA_HEADS * SWA_HD + nkv:].reshape(batch, seq, SWA_KV_HEADS, SWA_HD))
    return x.reshape(batch, seq, d), state0, state1


def kernel(x_prompt, x_sample, state_l0_hgrn_fwd, state_l0_hgrn_bwd, cache_l0_mla_ckv, cache_l0_mla_kpe, cache_l1_k, cache_l1_v, c, c_ctx, hgrn_lb_logits, l0_ada_w, l0_ada_b, l0_norm1, l0_norm2, l0_w_in, l0_hg_out_norm, l0_qa_norm, l0_q_b, l0_kva_norm, l0_kv_b, l0_q_norm, l0_k_norm, l0_w_out, l0_router_w, l0_router_b, l0_w_gu, l0_b_gu, l0_w_down, l0_b_down, l1_ada_w, l1_ada_b, l1_norm1, l1_norm2, l1_w_in, l1_q_norm, l1_k_norm, l1_sink, l1_w_out, l1_router_w, l1_router_b, l1_w_gu, l1_b_gu, l1_w_down, l1_b_down):
    p0 = _prep_layer0(dict(norm1=l0_norm1, norm2=l0_norm2, w_in=l0_w_in, hg_out_norm=l0_hg_out_norm,
                           qa_norm=l0_qa_norm, q_b=l0_q_b, kva_norm=l0_kva_norm, kv_b=l0_kv_b,
                           q_norm=l0_q_norm, k_norm=l0_k_norm, w_out=l0_w_out, router_w=l0_router_w,
                           router_b=l0_router_b, w_gu=l0_w_gu, b_gu=l0_b_gu, w_down=l0_w_down,
                           b_down=l0_b_down))
    p1 = _prep_layer1(dict(norm1=l1_norm1, norm2=l1_norm2, w_in=l1_w_in, q_norm=l1_q_norm, k_norm=l1_k_norm,
                           sink=l1_sink, w_out=l1_w_out, router_w=l1_router_w, router_b=l1_router_b,
                           w_gu=l1_w_gu, b_gu=l1_b_gu, w_down=l1_w_down, b_down=l1_b_down))
    dec_batch = c.shape[0]
    d = c.shape[1]
    cond8 = jnp.concatenate([c_ctx[None, :], c, jnp.zeros((8 - 1 - dec_batch, d), F32)], axis=0)
    mods_ctx, mods_lat = [], []
    for w, b in ((l0_ada_w, l0_ada_b), (l1_ada_w, l1_ada_b)):
        mod = ada_params(cond8, w, b)
        mods_ctx.append([m.reshape(1, 1, d) for m in jnp.split(mod[0:1], 6, axis=-1)])
        mods_lat.append([m.reshape(dec_batch, 1, d) for m in jnp.split(mod[1:1 + dec_batch], 6, axis=-1)])

    y_prompt, st0, st1 = _group_forward(x_prompt, mods_ctx, p0, p1, hgrn_lb_logits, None, latent=False)
    caches = dict(hg_f=state_l0_hgrn_fwd, hg_b=state_l0_hgrn_bwd, ckv=cache_l0_mla_ckv, kpe=cache_l0_mla_kpe,
                  k1=cache_l1_k, v1=cache_l1_v)
    y_sample, _, _ = _group_forward(x_sample, mods_lat, p0, p1, hgrn_lb_logits, caches, latent=True)
    return (y_prompt, y_sample, st0[0], st0[1], st0[2], st0[3], st1[0], st1[1])
```

```python
import functools

import numpy as np
import jax
import jax.numpy as jnp
from jax import lax
from jax.experimental import pallas as pl
from jax.experimental.pallas import tpu as pltpu

F32 = jnp.float32
BF16 = jnp.bfloat16

D_MODEL = 1024
GRID_W = 64
ROPE_THETA = 10000.0
EPS = 1e-6
HG_HEADS = 4
HG_DK = 128
HG_DV = 128
MLA_HEADS = 8
MLA_NOPE = 64
MLA_ROPE = 32
MLA_V = 64
MLA_QK = MLA_NOPE + MLA_ROPE
MLA_Q_LORA = 256
MLA_KV_LORA = 128
MLA_SCALE = MLA_QK ** -0.5
SWA_HEADS = 16
SWA_KV_HEADS = 4
SWA_HD = 64
SWA_WINDOW = 128
SWA_SCALE = SWA_HD ** -0.5
SWA_GROUP = SWA_HEADS // SWA_KV_HEADS
N_EXPERTS = 32
TOP_K = 4
D_FF = 1024
SWIGLU_LIMIT = 7.0
SWIGLU_ALPHA = 1.702

LANES = 128
HG_CHUNK = 128
HG_LEVELS = 7
HG_MXU_LEVELS = 3
AB_IN_PAD = 3072
MOE_ROWS = 256
NEG_BIG = -1e30
VMEM_LIMIT = 48 * 1024 * 1024
EXPERT_VMEM_LIMIT = 56 * 1024 * 1024


def _cparams(sem):
    return pltpu.CompilerParams(dimension_semantics=sem, vmem_limit_bytes=VMEM_LIMIT)


def _dot(a, b):
    return jnp.dot(a, b, preferred_element_type=F32)


def _dot_nt(a, b):
    return lax.dot_general(a, b, (((1,), (1,)), ((), ())), preferred_element_type=F32)


def _dot_tn(a, b):
    return lax.dot_general(a, b, (((0,), (0,)), ((), ())), preferred_element_type=F32)


def _split2(x):
    hi = x.astype(BF16)
    lo = (x - hi.astype(F32)).astype(BF16)
    return hi, lo


def _dot_hp(a, b):
    ah, al = _split2(a)
    bh, bl = _split2(b)
    return _dot(ah, bh) + _dot(ah, bl) + _dot(al, bh)


def _sigmoid(x):
    return 1.0 / (1.0 + jnp.exp(-x))


def _modnorm(x, gain, shift, scale):
    y = x * lax.rsqrt(jnp.mean(x * x, axis=-1, keepdims=True) + EPS)
    return y * gain * (1.0 + scale) + shift


def _ada_kernel(c_ref, w_ref, b_ref, o_ref):
    c = c_ref[...]
    o_ref[...] = _dot_hp(c * _sigmoid(c), w_ref[...]) + b_ref[...]


def ada_params(cond8, w, b):
    n = w.shape[1]
    tn = 1024
    return pl.pallas_call(
        _ada_kernel,
        grid=(n // tn,),
        in_specs=[pl.BlockSpec((8, D_MODEL), lambda j: (0, 0)),
                  pl.BlockSpec((D_MODEL, tn), lambda j: (0, j)),
                  pl.BlockSpec((1, tn), lambda j: (0, j))],
        out_specs=pl.BlockSpec((8, tn), lambda j: (0, j)),
        out_shape=jax.ShapeDtypeStruct((8, n), F32),
        compiler_params=_cparams(("parallel",)),
        name="ada_params",
    )(cond8, w, b.reshape(1, n))


def _modnorm_matmul_kernel(x_ref, g_ref, sh_ref, sc_ref, w_ref, o_ref):
    h = _modnorm(x_ref[...], g_ref[...], sh_ref[0], sc_ref[0])
    o_ref[...] = _dot(h.astype(BF16), w_ref[...])


def modnorm_matmul(x, gain, shift, scale, w_bf16, tokens_per_group, tm=256):
    t, d = x.shape
    n = w_bf16.shape[1]
    tiles_per_group = tokens_per_group // tm
    grp = lambda i: (i // tiles_per_group, 0, 0)
    return pl.pallas_call(
        _modnorm_matmul_kernel,
        grid=(t // tm,),
        in_specs=[pl.BlockSpec((tm, d), lambda i: (i, 0)),
                  pl.BlockSpec((1, d), lambda i: (0, 0)),
                  pl.BlockSpec((1, 1, d), grp),
                  pl.BlockSpec((1, 1, d), grp),
                  pl.BlockSpec((d, n), lambda i: (0, 0))],
        out_specs=pl.BlockSpec((tm, n), lambda i: (i, 0)),
        out_shape=jax.ShapeDtypeStruct((t, n), F32),
        compiler_params=_cparams(("parallel",)),
        name="modnorm_matmul",
    )(x, gain.reshape(1, d), shift, scale, w_bf16)


def _hgrn_constants():
    c = HG_CHUNK
    t = np.arange(c)[:, None]
    u = np.arange(c)[None, :]
    mats = [(u <= t), (u > t)]
    for l in range(HG_MXU_LEVELS):
        m = 1 << l
        r = (t // (2 * m)) * (2 * m) + m - 1
        mats.append((u > np.minimum(t, r)) & (u <= np.maximum(t, r)))
    fwd = np.concatenate(mats, axis=0).astype(np.float32)
    bwd = np.concatenate([mm[::-1, ::-1] for mm in mats], axis=0).astype(np.float32)
    x = np.bitwise_xor(t, u)
    lvl = np.where(x > 0, np.floor(np.log2(np.maximum(x, 1))), HG_LEVELS).astype(np.int32)
    lv_f = np.where(t >= u, lvl, -1).astype(np.int32)
    return fwd, bwd, lv_f, lv_f.T.copy()


def _hgrn_direction(qs, fpres, vs, lbs, mcat, lv, sts, forward):
    c = HG_CHUNK
    n = len(qs)
    fs = [lb + (1.0 - lb) * _sigmoid(fp) for lb, fp in zip(lbs, fpres)]
    kks = [1.0 - f for f in fs]
    parts = []
    for f in fs:
        parts += list(_split2(jnp.log(f)))
    x_all = _dot(mcat, jnp.concatenate(parts, axis=1))
    xs = [x_all[:, 2 * i * c:(2 * i + 1) * c] + x_all[:, (2 * i + 1) * c:(2 * i + 2) * c] for i in range(n)]
    gs = [x[0:c] for x in xs]
    qbs = [q.astype(BF16) for q in qs]
    kbs = [kk.astype(BF16) for kk in kks]
    vbs = [v.astype(BF16) for v in vs]
    os_ = [_dot_nt((q * jnp.exp(g)).astype(BF16), st.astype(BF16)) for q, g, st in zip(qs, gs, sts)]
    accs = [jnp.where(lv == HG_LEVELS, _dot_nt(qb, kb), 0.0) for qb, kb in zip(qbs, kbs)]
    for l in range(HG_LEVELS):
        for i in range(n):
            if l < HG_MXU_LEVELS:
                x = xs[i][(2 + l) * c:(3 + l) * c]
            else:
                m = 1 << l
                ref_rows = [j * 2 * m + (m - 1 if forward else m) for j in range(c // (2 * m))]
                g_ref = jnp.concatenate(
                    [jnp.broadcast_to(gs[i][r:r + 1, :], (2 * m, c)) for r in ref_rows], axis=0)
                x = -jnp.abs(gs[i] - g_ref)
            e = jnp.exp(x)
            p = _dot_nt((qs[i] * e).astype(BF16), (kks[i] * e).astype(BF16))
            accs[i] = jnp.where(lv == l, p, accs[i])
    edge_row = c - 1 if forward else 0
    outs, new_sts = [], []
    for i in range(n):
        outs.append(os_[i] + _dot(accs[i].astype(BF16), vbs[i]))
        k_end = (kks[i] * jnp.exp(xs[i][c:2 * c])).astype(BF16)
        new_sts.append(sts[i] * jnp.exp(gs[i][edge_row:edge_row + 1, :]) + _dot_tn(vbs[i], k_end))
    return outs, new_sts


def _hgrn_kernel(*refs, has_init):
    if has_init:
        (qf_ref, qb_ref, ff_ref, fb_ref, vf_ref, vb_ref, lbl_ref, mf_ref, mb_ref, lvf_ref, lvb_ref,
         s0f_ref, s0b_ref, of_ref, ob_ref, sf_ref, sb_ref, stf, stb) = refs
    else:
        (qf_ref, qb_ref, ff_ref, fb_ref, vf_ref, vb_ref, lbl_ref, mf_ref, mb_ref, lvf_ref, lvb_ref,
         of_ref, ob_ref, sf_ref, sb_ref, stf, stb) = refs
    c = pl.program_id(1)
    nc = pl.num_programs(1)

    @pl.when(c == 0)
    def _():
        for h in range(HG_HEADS):
            if has_init:
                stf[h] = s0f_ref[0, h].T
                stb[h] = s0b_ref[0, h].T
            else:
                stf[h] = jnp.zeros((HG_DV, HG_DK), F32)
                stb[h] = jnp.zeros((HG_DV, HG_DK), F32)

    rows = [lbl_ref[:, j, :] for j in range(lbl_ref.shape[1])]
    mx = functools.reduce(jnp.maximum, rows)
    ex = [jnp.exp(r - mx) for r in rows]
    lb = ex[0] / functools.reduce(lambda a, b: a + b, ex)

    heads = [slice(h * LANES, (h + 1) * LANES) for h in range(HG_HEADS)]
    o_f, st_f = _hgrn_direction([qf_ref[0, :, hs] for hs in heads], [ff_ref[0, :, hs] for hs in heads],
                                [vf_ref[0, :, hs] for hs in heads], [lb[0:1, hs] for hs in heads],
                                mf_ref[...], lvf_ref[...], [stf[h] for h in range(HG_HEADS)], True)
    o_b, st_b = _hgrn_direction([qb_ref[0, :, hs] for hs in heads], [fb_ref[0, :, hs] for hs in heads],
                                [vb_ref[0, :, hs] for hs in heads], [lb[1:2, hs] for hs in heads],
                                mb_ref[...], lvb_ref[...], [stb[h] for h in range(HG_HEADS)], False)
    for h, hs in enumerate(heads):
        of_ref[0, :, hs] = o_f[h]
        ob_ref[0, :, hs] = o_b[h]
        stf[h] = st_f[h]
        stb[h] = st_b[h]

    @pl.when(c == nc - 1)
    def _():
        for h in range(HG_HEADS):
            sf_ref[0, h] = stf[h].T
            sb_ref[0, h] = stb[h].T


def hgrn_bidir(z, lb_logits, s0f, s0b, batch, seq):
    nc = seq // HG_CHUNK
    z3 = z.reshape(batch, seq, z.shape[1])
    mf, mb, lvf, lvb = _hgrn_constants()
    has_init = s0f is not None
    width = HG_HEADS * LANES
    blk = (1, HG_CHUNK, width)
    fwd = lambda off: pl.BlockSpec(blk, lambda b, c: (b, c, off))
    bwd = lambda off: pl.BlockSpec(blk, lambda b, c: (b, nc - 1 - c, off))
    full = lambda a: pl.BlockSpec(a.shape, lambda b, c: (0,) * a.ndim)
    st_spec = pl.BlockSpec((1, HG_HEADS, HG_DK, HG_DV), lambda b, c: (b, 0, 0, 0))
    consts = [jnp.asarray(mf, BF16), jnp.asarray(mb, BF16), jnp.asarray(lvf), jnp.asarray(lvb)]
    in_specs = [fwd(0), bwd(0), fwd(1), bwd(2), fwd(3), bwd(3),
                pl.BlockSpec(lb_logits.shape, lambda b, c: (0, 0, 0))]
    in_specs += [full(a) for a in consts]
    args = [z3] * 6 + [lb_logits] + consts
    if has_init:
        in_specs += [st_spec, st_spec]
        args += [s0f, s0b]
    o_shape = jax.ShapeDtypeStruct((batch, seq, width), F32)
    s_shape = jax.ShapeDtypeStruct((batch, HG_HEADS, HG_DK, HG_DV), F32)
    o_f, o_b, s_f, s_b = pl.pallas_call(
        functools.partial(_hgrn_kernel, has_init=has_init),
        grid=(batch, nc),
        in_specs=in_specs,
        out_specs=[pl.BlockSpec(blk, lambda b, c: (b, c, 0)),
                   pl.BlockSpec(blk, lambda b, c: (b, nc - 1 - c, 0)),
                   st_spec, st_spec],
        out_shape=[o_shape, o_shape, s_shape, s_shape],
        scratch_shapes=[pltpu.VMEM((HG_HEADS, HG_DV, HG_DK), F32), pltpu.VMEM((HG_HEADS, HG_DV, HG_DK), F32)],
        compiler_params=_cparams(("parallel", "arbitrary")),
        name="hgrn_bidir",
    )(*args)
    t = batch * seq
    return o_f.reshape(t, -1), o_b.reshape(t, -1), s_f, s_b


def _axial_tables(n_tokens, n_rot):
    t = jnp.arange(n_tokens)
    row = (t // GRID_W).astype(F32)
    col = (t % GRID_W).astype(F32)
    n_freq = n_rot // 4
    inv = jnp.power(ROPE_THETA, -jnp.arange(n_freq, dtype=F32) / n_freq)
    ang = jnp.concatenate([row[:, None] * inv, col[:, None] * inv], axis=-1)
    return jnp.cos(ang), jnp.sin(ang)


def _rope_lane_tables(n_tokens, n_rot, head_width, first_rot_lane):
    cos, sin = _axial_tables(n_tokens, n_rot)
    half = n_rot // 2
    c_head = jnp.ones((n_tokens, head_width), F32)
    sa_head = jnp.zeros((n_tokens, head_width), F32)
    sb_head = jnp.zeros((n_tokens, head_width), F32)
    a0, a1, a2 = first_rot_lane, first_rot_lane + half, first_rot_lane + n_rot
    c_head = c_head.at[:, a0:a1].set(cos).at[:, a1:a2].set(cos)
    sa_head = sa_head.at[:, a0:a1].set(-sin)
    sb_head = sb_head.at[:, a1:a2].set(sin)
    reps = LANES // head_width
    tile = lambda a: jnp.tile(a, (1, reps))
    return tile(c_head), tile(sa_head), tile(sb_head)


def _rope(x, c, sa, sb, half):
    return x * c + pltpu.roll(x, LANES - half, 1) * sa + pltpu.roll(x, half, 1) * sb


def _mla_q_kernel(*refs, rope):
    if rope:
        qa_ref, qan_ref, qb_ref, qn_ref, c_ref, sa_ref, sb_ref, o_ref = refs
    else:
        qa_ref, qan_ref, qb_ref, qn_ref, o_ref = refs
    qa = qa_ref[...]
    qn = qa * lax.rsqrt(jnp.mean(qa * qa, axis=-1, keepdims=True) + EPS) * qan_ref[...]
    qfull = _dot(qn.astype(BF16), qb_ref[...])
    outs = []
    for h in range(MLA_HEADS):
        qh = qfull[:, h * LANES:(h + 1) * LANES]
        ms = jnp.sum(qh * qh, axis=-1, keepdims=True) * (1.0 / MLA_QK)
        qh = qh * lax.rsqrt(ms + EPS) * qn_ref[...]
        if rope:
            qh = _rope(qh, c_ref[...], sa_ref[...], sb_ref[...], MLA_ROPE // 2)
        outs.append((qh * MLA_SCALE).astype(BF16))
    o_ref[...] = jnp.concatenate(outs, axis=1)


def mla_queries(z, qa_norm, q_b_pad, q_norm_pad, tables, tokens_per_batch, tm=256):
    t = z.shape[0]
    rope = tables is not None
    row = lambda i: (0, 0)
    in_specs = [pl.BlockSpec((tm, MLA_Q_LORA), lambda i: (i, 2560 // MLA_Q_LORA)),
                pl.BlockSpec((1, MLA_Q_LORA), row),
                pl.BlockSpec(q_b_pad.shape, row),
                pl.BlockSpec((1, LANES), row)]
    args = [z, qa_norm.reshape(1, -1), q_b_pad, q_norm_pad]
    if rope:
        tpb = tokens_per_batch // tm
        in_specs += [pl.BlockSpec((tm, LANES), lambda i: (i % tpb, 0))] * 3
        args += list(tables)
    return pl.pallas_call(
        functools.partial(_mla_q_kernel, rope=rope),
        grid=(t // tm,),
        in_specs=in_specs,
        out_specs=pl.BlockSpec((tm, MLA_HEADS * LANES), lambda i: (i, 0)),
        out_shape=jax.ShapeDtypeStruct((t, MLA_HEADS * LANES), BF16),
        compiler_params=_cparams(("parallel",)),
        name="mla_queries",
    )(*args)


def _mla_kv_kernel(*refs, norm_input, rope):
    if rope:
        kva_ref, kpe_ref, kvan_ref, wk_ref, wv_ref, kn_ref, c_ref, sa_ref, sb_ref, ckv_ref, k_ref, v_ref = refs
    else:
        kva_ref, kpe_ref, kvan_ref, wk_ref, wv_ref, kn_ref, ckv_ref, k_ref, v_ref = refs
    ckv = kva_ref[...]
    if norm_input:
        ckv = ckv * lax.rsqrt(jnp.mean(ckv * ckv, axis=-1, keepdims=True) + EPS) * kvan_ref[...]
    ckv_ref[...] = ckv
    cb = ckv.astype(BF16)
    knope = _dot(cb, wk_ref[...])
    v_ref[...] = _dot(cb, wv_ref[...]).astype(BF16)
    kpe = pltpu.roll(kpe_ref[...], MLA_NOPE, 1)
    outs = []
    for h in range(MLA_HEADS):
        kh = knope[:, h * LANES:(h + 1) * LANES] + kpe
        ms = jnp.sum(kh * kh, axis=-1, keepdims=True) * (1.0 / MLA_QK)
        kh = kh * lax.rsqrt(ms + EPS) * kn_ref[...]
        if rope:
            kh = _rope(kh, c_ref[...], sa_ref[...], sb_ref[...], MLA_ROPE // 2)
        outs.append(kh.astype(BF16))
    k_ref[...] = jnp.concatenate(outs, axis=1)


def mla_keys_values(kva_src, kva_col, kpe_src, kpe_col, kva_norm, wk_pad, wv, k_norm_pad, tables,
                    tokens_per_batch, norm_input, tm=256):
    t = kva_src.shape[0]
    rope = tables is not None
    row = lambda i: (0, 0)
    in_specs = [pl.BlockSpec((tm, LANES), lambda i: (i, kva_col)),
                pl.BlockSpec((tm, LANES), lambda i: (i, kpe_col)),
                pl.BlockSpec((1, LANES), row),
                pl.BlockSpec(wk_pad.shape, row),
                pl.BlockSpec(wv.shape, row),
                pl.BlockSpec((1, LANES), row)]
    args = [kva_src, kpe_src, kva_norm.reshape(1, -1), wk_pad, wv, k_norm_pad]
    if rope:
        tpb = tokens_per_batch // tm
        in_specs += [pl.BlockSpec((tm, LANES), lambda i: (i % tpb, 0))] * 3
        args += list(tables)
    return pl.pallas_call(
        functools.partial(_mla_kv_kernel, norm_input=norm_input, rope=rope),
        grid=(t // tm,),
        in_specs=in_specs,
        out_specs=[pl.BlockSpec((tm, LANES), lambda i: (i, 0)),
                   pl.BlockSpec((tm, MLA_HEADS * LANES), lambda i: (i, 0)),
                   pl.BlockSpec((tm, MLA_HEADS * MLA_V), lambda i: (i, 0))],
        out_shape=[jax.ShapeDtypeStruct((t, LANES), F32),
                   jax.ShapeDtypeStruct((t, MLA_HEADS * LANES), BF16),
                   jax.ShapeDtypeStruct((t, MLA_HEADS * MLA_V), BF16)],
        compiler_params=_cparams(("parallel",)),
        name="mla_keys_values",
    )(*args)


def _mla_attn_kernel(q_ref, k_ref, v_ref, o_ref):
    outs = []
    for j in range(2):
        q = q_ref[0][:, j * LANES:(j + 1) * LANES]
        k = k_ref[0][:, j * LANES:(j + 1) * LANES]
        v = v_ref[0][:, j * MLA_V:(j + 1) * MLA_V]
        s = _dot_nt(q, k)
        p = jnp.exp(s - jnp.max(s, axis=-1, keepdims=True))
        l = jnp.sum(p, axis=-1, keepdims=True)
        outs.append(_dot(p.astype(BF16), v) / l)
    o_ref[0] = jnp.concatenate(outs, axis=1)


def mla_attention(q, k, v, batch, n_q, n_k, tq=256):
    q3 = q.reshape(batch, n_q, -1)
    k3 = k.reshape(batch, n_k, -1)
    v3 = v.reshape(batch, n_k, -1)
    out = pl.pallas_call(
        _mla_attn_kernel,
        grid=(batch, MLA_HEADS // 2, n_q // tq),
        in_specs=[pl.BlockSpec((1, tq, 2 * LANES), lambda b, j, i: (b, i, j)),
                  pl.BlockSpec((1, n_k, 2 * LANES), lambda b, j, i: (b, 0, j)),
                  pl.BlockSpec((1, n_k, 2 * MLA_V), lambda b, j, i: (b, 0, j))],
        out_specs=pl.BlockSpec((1, tq, 2 * MLA_V), lambda b, j, i: (b, i, j)),
        out_shape=jax.ShapeDtypeStruct((batch, n_q, MLA_HEADS * MLA_V), F32),
        compiler_params=_cparams(("parallel", "parallel", "arbitrary")),
        name="mla_attention",
    )(q3, k3, v3)
    return out.reshape(batch * n_q, -1)


def _out0_kernel(of_ref, ob_ref, ag_ref, hgn_ref, om_ref, w_ref, x_ref, g_ref, o_ref):
    o = of_ref[...] + ob_ref[...]
    ag = ag_ref[...]
    parts = []
    for h in range(HG_HEADS):
        oh = o[:, h * HG_DV:(h + 1) * HG_DV]
        oh = oh * lax.rsqrt(jnp.mean(oh * oh, axis=-1, keepdims=True) + EPS) * hgn_ref[...]
        gh = ag[:, h * HG_DV:(h + 1) * HG_DV]
        parts.append((oh * (gh * _sigmoid(gh))).astype(BF16))
    oa = jnp.concatenate(parts, axis=1)
    n_a = HG_HEADS * HG_DV
    mix = _dot(oa, w_ref[0:n_a, :]) + _dot(om_ref[...].astype(BF16), w_ref[n_a:, :])
    o_ref[...] = x_ref[...] + g_ref[0] * mix


def out_proj_layer0(o_f, o_b, z, hg_norm, o_mla, w_out_bf16, x, gate, tokens_per_group, tm=256):
    t, d = x.shape
    n_a = HG_HEADS * HG_DV
    tiles_per_group = tokens_per_group // tm
    tile = lambda w: pl.BlockSpec((tm, w), lambda i: (i, 0))
    return pl.pallas_call(
        _out0_kernel,
        grid=(t // tm,),
        in_specs=[tile(n_a), tile(n_a),
                  pl.BlockSpec((tm, n_a), lambda i: (i, 2048 // n_a)),
                  pl.BlockSpec((1, HG_DV), lambda i: (0, 0)),
                  tile(o_mla.shape[1]),
                  pl.BlockSpec(w_out_bf16.shape, lambda i: (0, 0)),
                  tile(d),
                  pl.BlockSpec((1, 1, d), lambda i: (i // tiles_per_group, 0, 0))],
        out_specs=tile(d),
        out_shape=jax.ShapeDtypeStruct((t, d), F32),
        compiler_params=_cparams(("parallel",)),
        name="out_proj_layer0",
    )(o_f, o_b, z, hg_norm.reshape(1, -1), o_mla, w_out_bf16, x, gate)


def _out1_kernel(a_ref, w_ref, x_ref, g_ref, o_ref):
    o_ref[...] = x_ref[...] + g_ref[0] * _dot(a_ref[...].astype(BF16), w_ref[...])


def out_proj_layer1(a, w_out_bf16, x, gate, tokens_per_group, tm=256):
    t, d = x.shape
    tiles_per_group = tokens_per_group // tm
    tile = lambda w: pl.BlockSpec((tm, w), lambda i: (i, 0))
    return pl.pallas_call(
        _out1_kernel,
        grid=(t // tm,),
        in_specs=[tile(a.shape[1]),
                  pl.BlockSpec(w_out_bf16.shape, lambda i: (0, 0)),
                  tile(d),
                  pl.BlockSpec((1, 1, d), lambda i: (i // tiles_per_group, 0, 0))],
        out_specs=tile(d),
        out_shape=jax.ShapeDtypeStruct((t, d), F32),
        compiler_params=_cparams(("parallel",)),
        name="out_proj_layer1",
    )(a, w_out_bf16, x, gate)


def _head_rms(x, gain2):
    sq = x * x
    lane = lax.broadcasted_iota(jnp.int32, x.shape, 1)
    first = lane < SWA_HD
    lo = jnp.sum(jnp.where(first, sq, 0.0), axis=-1, keepdims=True)
    hi = jnp.sum(jnp.where(first, 0.0, sq), axis=-1, keepdims=True)
    ms = jnp.where(first, lo, hi) * (1.0 / SWA_HD)
    return x * lax.rsqrt(ms + EPS) * gain2


def _swa_prep_kernel(*refs, rope):
    if rope:
        zq_ref, zk_ref, zv_ref, qn_ref, kn_ref, c_ref, sa_ref, sb_ref, q_ref, kc_ref, k_ref, v_ref = refs
    else:
        zq_ref, zk_ref, zv_ref, qn_ref, kn_ref, q_ref, kc_ref, k_ref, v_ref = refs
    half = SWA_HD // 2

    def rot(x):
        return _rope(x, c_ref[...], sa_ref[...], sb_ref[...], half) if rope else x

    zq = zq_ref[...]
    qs = []
    for p in range(zq.shape[1] // LANES):
        x = _head_rms(zq[:, p * LANES:(p + 1) * LANES], qn_ref[...])
        qs.append((rot(x) * SWA_SCALE).astype(BF16))
    q_ref[...] = jnp.concatenate(qs, axis=1)
    zk = zk_ref[...]
    kn, kr = [], []
    for p in range(zk.shape[1] // LANES):
        x = _head_rms(zk[:, p * LANES:(p + 1) * LANES], kn_ref[...])
        kn.append(x)
        kr.append(rot(x).astype(BF16))
    kc_ref[...] = jnp.concatenate(kn, axis=1)
    k_ref[...] = jnp.concatenate(kr, axis=1)
    v_ref[...] = zv_ref[...].astype(BF16)


def swa_prep(z, q_norm2, k_norm2, tables, tokens_per_batch, tm=256):
    t = z.shape[0]
    nq = SWA_HEADS * SWA_HD
    nkv = SWA_KV_HEADS * SWA_HD
    rope = tables is not None
    row = lambda i: (0, 0)
    in_specs = [pl.BlockSpec((tm, nq), lambda i: (i, 0)),
                pl.BlockSpec((tm, nkv), lambda i: (i, nq // nkv)),
                pl.BlockSpec((tm, nkv), lambda i: (i, nq // nkv + 1)),
                pl.BlockSpec((1, LANES), row),
                pl.BlockSpec((1, LANES), row)]
    args = [z, z, z, q_norm2, k_norm2]
    if rope:
        tpb = tokens_per_batch // tm
        in_specs += [pl.BlockSpec((tm, LANES), lambda i: (i % tpb, 0))] * 3
        args += list(tables)
    tile = lambda w: pl.BlockSpec((tm, w), lambda i: (i, 0))
    return pl.pallas_call(
        functools.partial(_swa_prep_kernel, rope=rope),
        grid=(t // tm,),
        in_specs=in_specs,
        out_specs=[tile(nq), tile(nkv), tile(nkv), tile(nkv)],
        out_shape=[jax.ShapeDtypeStruct((t, nq), BF16),
                   jax.ShapeDtypeStruct((t, nkv), F32),
                   jax.ShapeDtypeStruct((t, nkv), BF16),
                   jax.ShapeDtypeStruct((t, nkv), BF16)],
        compiler_params=_cparams(("parallel",)),
        name="swa_prep",
    )(*args)


def _sink_softmax_pv(s, sk, vn):
    m = jnp.maximum(jnp.max(s, axis=-1, keepdims=True), sk)
    p = jnp.exp(s - m)
    l = jnp.sum(p, axis=-1, keepdims=True) + jnp.exp(sk - m)
    return _dot(p.astype(BF16), vn) / l


def _sink_attention(q8, k_all, v_all, bias, sink_ref, pair, stack):
    outs = []
    tq = q8.shape[0]
    for n in range(2):
        kn = k_all[:, n * SWA_HD:(n + 1) * SWA_HD]
        vn = v_all[:, n * SWA_HD:(n + 1) * SWA_HD]
        heads = [n * SWA_GROUP + g for g in range(SWA_GROUP)]
        qs = [q8[:, hq * SWA_HD:(hq + 1) * SWA_HD] for hq in heads]
        sinks = [sink_ref[pair * 2 * SWA_GROUP + hq] for hq in heads]
        if stack:
            sk = jnp.concatenate([jnp.full((tq, 1), s, F32) for s in sinks], axis=0)
            s = _dot_nt(jnp.concatenate(qs, axis=0), kn)
            if bias is not None:
                s = s + bias
            o4 = _sink_softmax_pv(s, sk, vn)
            outs += [o4[g * tq:(g + 1) * tq] for g in range(SWA_GROUP)]
        else:
            for q, sk in zip(qs, sinks):
                s = _dot_nt(q, kn)
                if bias is not None:
                    s = s + bias
                outs.append(_sink_softmax_pv(s, sk, vn))
    return jnp.concatenate(outs, axis=1)


def _swa_dense_kernel(sink_ref, q_ref, k_ref, v_ref, o_ref):
    o_ref[0] = _sink_attention(q_ref[0], k_ref[0], v_ref[0], None, sink_ref, pl.program_id(1), stack=False)


def swa_dense(q, k, v, sink, batch, seq):
    nq = SWA_HEADS * SWA_HD
    q3 = q.reshape(batch, seq, nq)
    k3 = k.reshape(batch, seq, -1)
    v3 = v.reshape(batch, seq, -1)
    out = pl.pallas_call(
        _swa_dense_kernel,
        grid=(batch, 2),
        in_specs=[pl.BlockSpec(memory_space=pltpu.SMEM),
                  pl.BlockSpec((1, seq, nq // 2), lambda b, j: (b, 0, j)),
                  pl.BlockSpec((1, seq, LANES), lambda b, j: (b, 0, j)),
                  pl.BlockSpec((1, seq, LANES), lambda b, j: (b, 0, j))],
        out_specs=pl.BlockSpec((1, seq, nq // 2), lambda b, j: (b, 0, j)),
        out_shape=jax.ShapeDtypeStruct((batch, seq, nq), F32),
        compiler_params=_cparams(("parallel", "parallel")),
        name="swa_dense",
    )(sink, q3, k3, v3)
    return out.reshape(batch * seq, nq)


def _swa_window_kernel(sink_ref, q_ref, kc_ref, vc_ref, kp_ref, k0_ref, kn_ref, vp_ref, v0_ref, vn_ref, o_ref):
    i = pl.program_id(2)
    nb = pl.num_programs(2)
    w = SWA_WINDOW
    n_ctx = kc_ref.shape[1]
    k_all = jnp.concatenate([kc_ref[0], kp_ref[0], k0_ref[0], kn_ref[0]], axis=0)
    v_all = jnp.concatenate([vc_ref[0], vp_ref[0], v0_ref[0], vn_ref[0]], axis=0)
    shape = (SWA_GROUP * w, n_ctx + 3 * w)
    r = lax.broadcasted_iota(jnp.int32, shape, 0) % w
    col = lax.broadcasted_iota(jnp.int32, shape, 1) - n_ctx
    prev_bias = jnp.where(i > 0, 0.0, NEG_BIG)
    next_bias = jnp.where(i < nb - 1, 0.0, NEG_BIG)
    bias = jnp.where(
        col < w,
        jnp.where(col < 0, 0.0, jnp.where(col >= r, prev_bias, NEG_BIG)),
        jnp.where(col < 2 * w, 0.0, jnp.where(col - 2 * w <= r, next_bias, NEG_BIG)))
    o_ref[0] = _sink_attention(q_ref[0], k_all, v_all, bias, sink_ref, pl.program_id(1), stack=True)


def swa_window(q, k, v, k_ctx, v_ctx, sink, batch, seq, n_ctx):
    nq = SWA_HEADS * SWA_HD
    w = SWA_WINDOW
    nb = seq // w
    q3 = q.reshape(batch, seq, nq)
    k3 = k.reshape(batch, seq, -1)
    v3 = v.reshape(batch, seq, -1)
    ctx = pl.BlockSpec((1, n_ctx, LANES), lambda b, j, i: (b, 0, j))
    prv = pl.BlockSpec((1, w, LANES), lambda b, j, i: (b, jnp.maximum(i - 1, 0), j))
    cur = pl.BlockSpec((1, w, LANES), lambda b, j, i: (b, i, j))
    nxt = pl.BlockSpec((1, w, LANES), lambda b, j, i: (b, jnp.minimum(i + 1, nb - 1), j))
    out = pl.pallas_call(
        _swa_window_kernel,
        grid=(batch, 2, nb),
        in_specs=[pl.BlockSpec(memory_space=pltpu.SMEM),
                  pl.BlockSpec((1, w, nq // 2), lambda b, j, i: (b, i, j)),
                  ctx, ctx, prv, cur, nxt, prv, cur, nxt],
        out_specs=pl.BlockSpec((1, w, nq // 2), lambda b, j, i: (b, i, j)),
        out_shape=jax.ShapeDtypeStruct((batch, seq, nq), F32),
        compiler_params=_cparams(("parallel", "parallel", "arbitrary")),
        name="swa_window",
    )(sink, q3, k_ctx, v_ctx, k3, k3, k3, v3, v3, v3)
    return out.reshape(batch * seq, nq)


def _router_kernel(x_ref, g_ref, sh_ref, sc_ref, rw_ref, rb_ref, tri_ref, h_ref, idx_ref, gate_ref, rank_ref,
                   cnt_ref, cnt):
    @pl.when(pl.program_id(0) == 0)
    def _():
        cnt[...] = jnp.zeros_like(cnt)

    h = _modnorm(x_ref[...], g_ref[...], sh_ref[0], sc_ref[0])
    h_ref[...] = h
    logits = _dot_hp(h, rw_ref[...]) + rb_ref[...]
    lane = lax.broadcasted_iota(jnp.int32, logits.shape, 1)
    work = logits
    vals, idxs = [], []
    for _ in range(TOP_K):
        m = jnp.max(work, axis=-1, keepdims=True)
        ik = jnp.min(jnp.where(work == m, lane, LANES), axis=-1, keepdims=True)
        vals.append(m)
        idxs.append(ik)
        work = jnp.where(lane == ik, 2.0 * NEG_BIG, work)
    es = [jnp.exp(v - vals[0]) for v in vals]
    denom = es[0] + es[1] + es[2] + es[3]
    idx_out = jnp.zeros(logits.shape, jnp.int32)
    gate_out = jnp.zeros(logits.shape, F32)
    for k in range(TOP_K):
        idx_out = jnp.where(lane == k, idxs[k], idx_out)
        gate_out = jnp.where(lane == k, es[k] / denom, gate_out)
    idx_ref[...] = idx_out
    gate_ref[...] = gate_out
    chosen = jnp.zeros(logits.shape, F32)
    for k in range(TOP_K):
        chosen = jnp.where(lane == idxs[k], 1.0, chosen)
    before = _dot(tri_ref[...], chosen.astype(BF16)) + cnt[...]
    rank_out = jnp.zeros(logits.shape, jnp.int32)
    for k in range(TOP_K):
        rk = jnp.sum(jnp.where(lane == idxs[k], before, 0.0), axis=-1, keepdims=True)
        rank_out = jnp.where(lane == k, rk.astype(jnp.int32), rank_out)
    rank_ref[...] = rank_out
    cnt[...] = cnt[...] + jnp.sum(chosen, axis=0, keepdims=True)
    cnt_ref[...] = cnt[...]


def moe_router(x, gain, shift, scale, rw_pad, rb_pad, tokens_per_group, tm=256):
    t, d = x.shape
    tiles_per_group = tokens_per_group // tm
    grp = lambda i: (i // tiles_per_group, 0, 0)
    tile = lambda w: pl.BlockSpec((tm, w), lambda i: (i, 0))
    tri = jnp.asarray(np.tril(np.ones((tm, tm), np.float32), -1), BF16)
    return pl.pallas_call(
        _router_kernel,
        grid=(t // tm,),
        in_specs=[tile(d),
                  pl.BlockSpec((1, d), lambda i: (0, 0)),
                  pl.BlockSpec((1, 1, d), grp),
                  pl.BlockSpec((1, 1, d), grp),
                  pl.BlockSpec((d, LANES), lambda i: (0, 0)),
                  pl.BlockSpec((1, LANES), lambda i: (0, 0)),
                  pl.BlockSpec((tm, tm), lambda i: (0, 0))],
        out_specs=[tile(d), tile(LANES), tile(LANES), tile(LANES),
                   pl.BlockSpec((1, LANES), lambda i: (0, 0))],
        out_shape=[jax.ShapeDtypeStruct((t, d), F32),
                   jax.ShapeDtypeStruct((t, LANES), jnp.int32),
                   jax.ShapeDtypeStruct((t, LANES), F32),
                   jax.ShapeDtypeStruct((t, LANES), jnp.int32),
                   jax.ShapeDtypeStruct((1, LANES), F32)],
        scratch_shapes=[pltpu.VMEM((1, LANES), F32)],
        compiler_params=_cparams(("arbitrary",)),
        name="moe_router",
    )(x, gain.reshape(1, d), shift, scale, rw_pad, rb_pad, tri)


def _row_gather_start(src_hbm, dst_rows, sem, row_of, n_rows):
    def body(r, carry):
        pltpu.make_async_copy(src_hbm.at[pl.ds(row_of(r), 1), :], dst_rows.at[pl.ds(r, 1), :], sem).start()
        return carry
    lax.fori_loop(0, n_rows, body, 0, unroll=8)


FFN_COLS = 256
FFN_PIECES = 2 * D_FF // FFN_COLS + D_MODEL // FFN_COLS
SEM_GATHER_A, SEM_GATHER_B, SEM_WRITE_A, SEM_WRITE_B = range(4)


def _ffn_block(x_ref, y_ref, wgu, bgu_ref, wd, bd_ref, issue):
    xb = x_ref[...].astype(BF16)
    acts = []
    for c in range(D_FF // FFN_COLS):
        lo, hi = c * FFN_COLS, (c + 1) * FFN_COLS
        gate = _dot(xb, wgu[:, lo:hi]) + bgu_ref[0][:, lo:hi]
        issue(2 * c)
        up = _dot(xb, wgu[:, D_FF + lo:D_FF + hi]) + bgu_ref[0][:, D_FF + lo:D_FF + hi]
        issue(2 * c + 1)
        gate = jnp.minimum(gate, SWIGLU_LIMIT)
        up = jnp.clip(up, -SWIGLU_LIMIT, SWIGLU_LIMIT)
        acts.append((gate * _sigmoid(SWIGLU_ALPHA * gate) * (up + 1.0)).astype(BF16))
    act = jnp.concatenate(acts, axis=1)
    for n in range(D_MODEL // FFN_COLS):
        lo, hi = n * FFN_COLS, (n + 1) * FFN_COLS
        y_ref[:, lo:hi] = _dot(act, wd[:, lo:hi]) + bd_ref[0][:, lo:hi]
        issue(2 * D_FF // FFN_COLS + n)


def _expert_kernel(nb_ref, bs_ref, rt_ref, h_hbm, wgu_ref, bgu_ref, wd_ref, bd_ref, y_hbm,
                   wgu_bf, wd_bf, xa, xb, ya, yb, sem):
    e = pl.program_id(0)
    nb = nb_ref[e]
    b0 = bs_ref[e]
    shares = np.array_split(np.arange(MOE_ROWS), FFN_PIECES)

    def gather_issue(dst, s, blk):
        base = blk * MOE_ROWS

        def issue(k):
            for r in shares[k]:
                r = int(r)
                pltpu.make_async_copy(h_hbm.at[pl.ds(rt_ref[base + r], 1), :], dst.at[pl.ds(r, 1), :],
                                      sem.at[s]).start()
        return issue

    def gather_wait(dst, s):
        pltpu.make_async_copy(dst, dst, sem.at[s]).wait()

    def write(src, s, blk):
        return pltpu.make_async_copy(src, y_hbm.at[pl.ds(blk * MOE_ROWS, MOE_ROWS), :], sem.at[s])

    def ffn(x_ref, y_ref, issue):
        _ffn_block(x_ref, y_ref, wgu_bf, bgu_ref, wd_bf, bd_ref, issue)

    @pl.when(nb > 0)
    def _():
        first = gather_issue(xa, SEM_GATHER_A, b0)
        n_slabs = 4
        per = FFN_PIECES // n_slabs
        for s in range(n_slabs):
            rows = slice(s * D_MODEL // n_slabs, (s + 1) * D_MODEL // n_slabs)
            wgu_bf[rows, :] = wgu_ref[0, rows, :].astype(BF16)
            wd_bf[rows, :] = wd_ref[0, rows, :].astype(BF16)
            for k in range(s * per, (s + 1) * per):
                first(k)
        n_pairs = nb // 2
        last = b0 + nb - 1

        def pair(jj, carry):
            j = b0 + 2 * jj
            gather_wait(xa, SEM_GATHER_A)

            @pl.when(jj > 0)
            def _():
                write(ya, SEM_WRITE_A, j).wait()

            ffn(xa, ya, gather_issue(xb, SEM_GATHER_B, j + 1))
            write(ya, SEM_WRITE_A, j).start()
            gather_wait(xb, SEM_GATHER_B)

            @pl.when(jj > 0)
            def _():
                write(yb, SEM_WRITE_B, j).wait()

            ffn(xb, yb, gather_issue(xa, SEM_GATHER_A, jnp.minimum(j + 2, last)))
            write(yb, SEM_WRITE_B, j + 1).start()
            return carry

        lax.fori_loop(0, n_pairs, pair, 0)
        gather_wait(xa, SEM_GATHER_A)

        @pl.when(nb % 2 == 1)
        def _():
            @pl.when(n_pairs > 0)
            def _():
                write(ya, SEM_WRITE_A, last).wait()

            ffn(xa, ya, lambda k: None)
            write(ya, SEM_WRITE_A, last).start()

        write(ya, SEM_WRITE_A, last).wait()

        @pl.when(n_pairs > 0)
        def _():
            write(yb, SEM_WRITE_B, last).wait()

    @pl.when(e == N_EXPERTS - 1)
    def _():
        n_blocks = y_hbm.shape[0] // MOE_ROWS
        ya[...] = jnp.zeros_like(ya)

        def fill(blk, carry):
            write(ya, SEM_WRITE_A, blk).start()
            return carry

        def drain(blk, carry):
            write(ya, SEM_WRITE_A, blk).wait()
            return carry

        lax.fori_loop(b0 + nb, n_blocks, fill, 0)
        lax.fori_loop(b0 + nb, n_blocks, drain, 0)


def moe_experts(h, n_blk, blk_start, row_tok, w_gu, b_gu, w_down, b_down):
    n_rows = row_tok.shape[0]
    d = h.shape[1]
    grid_spec = pltpu.PrefetchScalarGridSpec(
        num_scalar_prefetch=3,
        grid=(N_EXPERTS,),
        in_specs=[pl.BlockSpec(memory_space=pl.ANY),
                  pl.BlockSpec((1, d, 2 * D_FF), lambda e, nb, bs, rt: (e, 0, 0)),
                  pl.BlockSpec((1, 1, 2 * D_FF), lambda e, nb, bs, rt: (e, 0, 0)),
                  pl.BlockSpec((1, D_FF, d), lambda e, nb, bs, rt: (e, 0, 0)),
                  pl.BlockSpec((1, 1, d), lambda e, nb, bs, rt: (e, 0, 0))],
        out_specs=pl.BlockSpec(memory_space=pl.ANY),
        scratch_shapes=[pltpu.VMEM((d, 2 * D_FF), BF16), pltpu.VMEM((D_FF, d), BF16),
                        pltpu.VMEM((MOE_ROWS, d), F32), pltpu.VMEM((MOE_ROWS, d), F32),
                        pltpu.VMEM((MOE_ROWS, d), F32), pltpu.VMEM((MOE_ROWS, d), F32),
                        pltpu.SemaphoreType.DMA((4,))],
    )
    return pl.pallas_call(
        _expert_kernel,
        grid_spec=grid_spec,
        out_shape=jax.ShapeDtypeStruct((n_rows, d), F32),
        compiler_params=pltpu.CompilerParams(dimension_semantics=("arbitrary",),
                                             vmem_limit_bytes=EXPERT_VMEM_LIMIT),
        name="moe_experts",
    )(n_blk, blk_start, row_tok, h, w_gu, b_gu.reshape(N_EXPERTS, 1, -1), w_down, b_down.reshape(N_EXPERTS, 1, -1))


COMBINE_TM = 128


def _combine_kernel(dest_ref, y_hbm, gate_ref, x_ref, g_ref, o_ref, buf, sem):
    i = pl.program_id(0)
    n = pl.num_programs(0)
    slot = i % 2

    def start(tile, s):
        for k in range(TOP_K):
            _row_gather_start(y_hbm, buf.at[s, k], sem.at[s],
                              lambda r: dest_ref[(tile * COMBINE_TM + r) * TOP_K + k], COMBINE_TM)

    @pl.when(i == 0)
    def _():
        start(0, 0)

    @pl.when(i + 1 < n)
    def _():
        start(i + 1, 1 - slot)

    pltpu.make_async_copy(buf.at[slot], buf.at[slot], sem.at[slot]).wait()
    gates = gate_ref[...]
    acc = gates[:, 0:1] * buf[slot, 0]
    for k in range(1, TOP_K):
        acc = acc + gates[:, k:k + 1] * buf[slot, k]
    o_ref[...] = x_ref[...] + g_ref[0] * acc


def moe_combine(yb, dest, gates, x, gate_mod, tokens_per_group):
    t, d = x.shape
    tm = COMBINE_TM
    tiles_per_group = tokens_per_group // tm
    grid_spec = pltpu.PrefetchScalarGridSpec(
        num_scalar_prefetch=1,
        grid=(t // tm,),
        in_specs=[pl.BlockSpec(memory_space=pl.ANY),
                  pl.BlockSpec((tm, LANES), lambda i, de: (i, 0)),
                  pl.BlockSpec((tm, d), lambda i, de: (i, 0)),
                  pl.BlockSpec((1, 1, d), lambda i, de: (i // tiles_per_group, 0, 0))],
        out_specs=pl.BlockSpec((tm, d), lambda i, de: (i, 0)),
        scratch_shapes=[pltpu.VMEM((2, TOP_K, tm, d), F32), pltpu.SemaphoreType.DMA((2,))],
    )
    return pl.pallas_call(
        _combine_kernel,
        grid_spec=grid_spec,
        out_shape=jax.ShapeDtypeStruct((t, d), F32),
        compiler_params=_cparams(("arbitrary",)),
        name="moe_combine",
    )(dest, yb, gates, x, gate_mod)


def _assignment_tables(top_idx, rank, counts):
    t = top_idx.shape[0]
    n_assign = t * TOP_K
    padded = (counts + MOE_ROWS - 1) // MOE_ROWS * MOE_ROWS
    pad_start = jnp.cumsum(padded) - padded
    experts = jnp.arange(N_EXPERTS, dtype=jnp.int32)
    start_of = jnp.sum(jnp.where(top_idx[:, :, None] == experts, pad_start, 0), axis=-1)
    dest = (start_of + rank).reshape(-1).astype(jnp.int32)
    n_rows = (n_assign // MOE_ROWS + N_EXPERTS) * MOE_ROWS
    row_tok = jnp.zeros((n_rows,), jnp.int32).at[dest].set(jnp.arange(n_assign, dtype=jnp.int32) // TOP_K)
    return dest, row_tok, (padded // MOE_ROWS).astype(jnp.int32), (pad_start // MOE_ROWS).astype(jnp.int32)


def moe_layer(x, p, shift, scale, gate_mod, tokens_per_group):
    h, idx, gates, rank, cnt = moe_router(x, p['norm2'], shift, scale, p['rw_pad'], p['rb_pad'], tokens_per_group)
    counts = cnt[0, :N_EXPERTS].astype(jnp.int32)
    dest, row_tok, n_blk, blk_start = _assignment_tables(idx[:, :TOP_K], rank[:, :TOP_K], counts)
    yb = moe_experts(h, n_blk, blk_start, row_tok, p['w_gu'], p['b_gu'], p['w_down'], p['b_down'])
    return moe_combine(yb, dest, gates, x, gate_mod, tokens_per_group)


def _pad_lanes(a, width):
    return jnp.pad(a, [(0, 0)] * (a.ndim - 1) + [(0, width - a.shape[-1])])


def _prep_common(p):
    p['rw_pad'] = _pad_lanes(p['router_w'], LANES)
    p['rb_pad'] = jnp.concatenate(
        [p['router_b'].astype(F32), jnp.full((LANES - N_EXPERTS,), NEG_BIG, F32)]).reshape(1, LANES)
    p['w_out'] = p['w_out'].astype(BF16)
    return p


def _prep_layer0(p):
    p = _prep_common(dict(p))
    p['w_in'] = _pad_lanes(p['w_in'], AB_IN_PAD).astype(BF16)
    q_b = p['q_b'].reshape(MLA_Q_LORA, MLA_HEADS, MLA_QK)
    p['q_b_pad'] = _pad_lanes(q_b, LANES).reshape(MLA_Q_LORA, MLA_HEADS * LANES).astype(BF16)
    kv_b = p['kv_b'].reshape(MLA_KV_LORA, MLA_HEADS, MLA_NOPE + MLA_V)
    p['wk_pad'] = _pad_lanes(kv_b[:, :, :MLA_NOPE], LANES).reshape(MLA_KV_LORA, MLA_HEADS * LANES).astype(BF16)
    p['wv'] = kv_b[:, :, MLA_NOPE:].reshape(MLA_KV_LORA, MLA_HEADS * MLA_V).astype(BF16)
    p['q_norm_pad'] = _pad_lanes(p['q_norm'].reshape(1, -1), LANES)
    p['k_norm_pad'] = _pad_lanes(p['k_norm'].reshape(1, -1), LANES)
    return p


def _prep_layer1(p):
    p = _prep_common(dict(p))
    p['w_in'] = p['w_in'].astype(BF16)
    p['q_norm2'] = jnp.tile(p['q_norm'].reshape(1, -1), (1, LANES // SWA_HD))
    p['k_norm2'] = jnp.tile(p['k_norm'].reshape(1, -1), (1, LANES // SWA_HD))
    return p


def _group_forward(x3, mods, p0, p1, lb_logits, caches, latent):
    batch, seq, d = x3.shape
    t = batch * seq
    x = x3.reshape(t, d)
    tpg = seq if latent else t
    sh1, sc1, g1, sh2, sc2, g2 = mods[0]

    z = modnorm_matmul(x, p0['norm1'], sh1, sc1, p0['w_in'], tpg)
    tab_b = _rope_lane_tables(seq, MLA_ROPE, LANES, MLA_NOPE) if latent else None
    s0f, s0b = (caches['hg_f'], caches['hg_b']) if latent else (None, None)
    o_f, o_b, s_f, s_b = hgrn_bidir(z, lb_logits, s0f, s0b, batch, seq)
    q = mla_queries(z, p0['qa_norm'], p0['q_b_pad'], p0['q_norm_pad'], tab_b, seq)
    ckv, k, v = mla_keys_values(z, 2816 // LANES, z, 2944 // LANES, p0['kva_norm'], p0['wk_pad'], p0['wv'],
                                p0['k_norm_pad'], tab_b, seq, norm_input=True)
    n_k = seq
    if latent:
        n_ctx = caches['ckv'].shape[1]
        ckv_c = caches['ckv'].reshape(batch * n_ctx, MLA_KV_LORA)
        kpe_c = _pad_lanes(caches['kpe'].reshape(batch * n_ctx, MLA_ROPE), LANES)
        _, k_c, v_c = mla_keys_values(ckv_c, 0, kpe_c, 0, p0['kva_norm'], p0['wk_pad'], p0['wv'],
                                      p0['k_norm_pad'], None, n_ctx, norm_input=False)
        cat = lambda a, b: jnp.concatenate([a.reshape(batch, n_ctx, -1), b.reshape(batch, seq, -1)],
                                           axis=1).reshape(batch * (n_ctx + seq), -1)
        k, v = cat(k_c, k), cat(v_c, v)
        n_k = n_ctx + seq
    o_mla = mla_attention(q, k, v, batch, seq, n_k)
    x = out_proj_layer0(o_f, o_b, z, p0['hg_out_norm'], o_mla, p0['w_out'], x, g1, tpg)
    x = moe_layer(x, p0, sh2, sc2, g2, tpg)
    state0 = (s_f, s_b, ckv.reshape(batch, seq, MLA_KV_LORA), z[:, 2944:2944 + MLA_ROPE].reshape(batch, seq, MLA_ROPE))

    sh1, sc1, g1, sh2, sc2, g2 = mods[1]
    z = modnorm_matmul(x, p1['norm1'], sh1, sc1, p1['w_in'], tpg)
    tab_c = _rope_lane_tables(seq, SWA_HD, SWA_HD, 0) if latent else None
    q, k_cache, k, v = swa_prep(z, p1['q_norm2'], p1['k_norm2'], tab_c, seq)
    sink = p1['sink'].astype(F32)
    if latent:
        n_ctx = caches['k1'].shape[1]
        k_c = caches['k1'].reshape(batch, n_ctx, -1).astype(BF16)
        v_c = caches['v1'].reshape(batch, n_ctx, -1).astype(BF16)
        a = swa_window(q, k, v, k_c, v_c, sink, batch, seq, n_ctx)
    else:
        a = swa_dense(q, k, v, sink, batch, seq)
    x = out_proj_layer1(a, p1['w_out'], x, g1, tpg)
    x = moe_layer(x, p1, sh2, sc2, g2, tpg)
    nkv = SWA_KV_HEADS * SWA_HD
    state1 = (k_cache.reshape(batch, seq, SWA_KV_HEADS, SWA_HD),
              z[:, SWA_HEADS * SWA_HD + nkv:].reshape(batch, seq, SWA_KV_HEADS, SWA_HD))
    return x.reshape(batch, seq, d), state0, state1


def kernel(x_prompt, x_sample, state_l0_hgrn_fwd, state_l0_hgrn_bwd, cache_l0_mla_ckv, cache_l0_mla_kpe, cache_l1_k, cache_l1_v, c, c_ctx, hgrn_lb_logits, l0_ada_w, l0_ada_b, l0_norm1, l0_norm2, l0_w_in, l0_hg_out_norm, l0_qa_norm, l0_q_b, l0_kva_norm, l0_kv_b, l0_q_norm, l0_k_norm, l0_w_out, l0_router_w, l0_router_b, l0_w_gu, l0_b_gu, l0_w_down, l0_b_down, l1_ada_w, l1_ada_b, l1_norm1, l1_norm2, l1_w_in, l1_q_norm, l1_k_norm, l1_sink, l1_w_out, l1_router_w, l1_router_b, l1_w_gu, l1_b_gu, l1_w_down, l1_b_down):
    p0 = _prep_layer0(dict(norm1=l0_norm1, norm2=l0_norm2, w_in=l0_w_in, hg_out_norm=l0_hg_out_norm,
                           qa_norm=l0_qa_norm, q_b=l0_q_b, kva_norm=l0_kva_norm, kv_b=l0_kv_b,
                           q_norm=l0_q_norm, k_norm=l0_k_norm, w_out=l0_w_out, router_w=l0_router_w,
                           router_b=l0_router_b, w_gu=l0_w_gu, b_gu=l0_b_gu, w_down=l0_w_down,
                           b_down=l0_b_down))
    p1 = _prep_layer1(dict(norm1=l1_norm1, norm2=l1_norm2, w_in=l1_w_in, q_norm=l1_q_norm, k_norm=l1_k_norm,
                           sink=l1_sink, w_out=l1_w_out, router_w=l1_router_w, router_b=l1_router_b,
                           w_gu=l1_w_gu, b_gu=l1_b_gu, w_down=l1_w_down, b_down=l1_b_down))
    dec_batch = c.shape[0]
    d = c.shape[1]
    cond8 = jnp.concatenate([c_ctx[None, :], c, jnp.zeros((8 - 1 - dec_batch, d), F32)], axis=0)
    mods_ctx, mods_lat = [], []
    for w, b in ((l0_ada_w, l0_ada_b), (l1_ada_w, l1_ada_b)):
        mod = ada_params(cond8, w, b)
        mods_ctx.append([m.reshape(1, 1, d) for m in jnp.split(mod[0:1], 6, axis=-1)])
        mods_lat.append([m.reshape(dec_batch, 1, d) for m in jnp.split(mod[1:1 + dec_batch], 6, axis=-1)])

    y_prompt, st0, st1 = _group_forward(x_prompt, mods_ctx, p0, p1, hgrn_lb_logits, None, latent=False)
    caches = dict(hg_f=state_l0_hgrn_fwd, hg_b=state_l0_hgrn_bwd, ckv=cache_l0_mla_ckv, kpe=cache_l0_mla_kpe,
                  k1=cache_l1_k, v1=cache_l1_v)
    y_sample, _, _ = _group_forward(x_sample, mods_lat, p0, p1, hgrn_lb_logits, caches, latent=True)
    return (y_prompt, y_sample, st0[0], st0[1], st0[2], st0[3], st1[0], st1[1])
```

```python
import functools

import numpy as np
import jax
import jax.numpy as jnp
from jax import lax
from jax.experimental import pallas as pl
from jax.experimental.pallas import tpu as pltpu

F32 = jnp.float32
BF16 = jnp.bfloat16

D_MODEL = 1024
GRID_W = 64
ROPE_THETA = 10000.0
EPS = 1e-6
HG_HEADS = 4
HG_DK = 128
HG_DV = 128
MLA_HEADS = 8
MLA_NOPE = 64
MLA_ROPE = 32
MLA_V = 64
MLA_QK = MLA_NOPE + MLA_ROPE
MLA_Q_LORA = 256
MLA_KV_LORA = 128
MLA_SCALE = MLA_QK ** -0.5
SWA_HEADS = 16
SWA_KV_HEADS = 4
SWA_HD = 64
SWA_WINDOW = 128
SWA_SCALE = SWA_HD ** -0.5
SWA_GROUP = SWA_HEADS // SWA_KV_HEADS
N_EXPERTS = 32
TOP_K = 4
D_FF = 1024
SWIGLU_LIMIT = 7.0
SWIGLU_ALPHA = 1.702

LANES = 128
HG_CHUNK = 128
HG_LEVELS = 7
HG_MXU_LEVELS = 3
AB_IN_PAD = 3072
MOE_ROWS = 256
NEG_BIG = -1e30
VMEM_LIMIT = 48 * 1024 * 1024
EXPERT_VMEM_LIMIT = 56 * 1024 * 1024


def _cparams(sem):
    return pltpu.CompilerParams(dimension_semantics=sem, vmem_limit_bytes=VMEM_LIMIT)


def _dot(a, b):
    return jnp.dot(a, b, preferred_element_type=F32)


def _dot_nt(a, b):
    return lax.dot_general(a, b, (((1,), (1,)), ((), ())), preferred_element_type=F32)


def _dot_tn(a, b):
    return lax.dot_general(a, b, (((0,), (0,)), ((), ())), preferred_element_type=F32)


def _split2(x):
    hi = x.astype(BF16)
    lo = (x - hi.astype(F32)).astype(BF16)
    return hi, lo


def _dot_hp(a, b):
    ah, al = _split2(a)
    bh, bl = _split2(b)
    return _dot(ah, bh) + _dot(ah, bl) + _dot(al, bh)


def _sigmoid(x):
    return 1.0 / (1.0 + jnp.exp(-x))


def _modnorm(x, gain, shift, scale):
    y = x * lax.rsqrt(jnp.mean(x * x, axis=-1, keepdims=True) + EPS)
    return y * gain * (1.0 + scale) + shift


def _ada_kernel(c_ref, w_ref, b_ref, o_ref):
    c = c_ref[...]
    o_ref[...] = _dot_hp(c * _sigmoid(c), w_ref[...]) + b_ref[...]


def ada_params(cond8, w, b):
    n = w.shape[1]
    tn = 1024
    return pl.pallas_call(
        _ada_kernel,
        grid=(n // tn,),
        in_specs=[pl.BlockSpec((8, D_MODEL), lambda j: (0, 0)),
                  pl.BlockSpec((D_MODEL, tn), lambda j: (0, j)),
                  pl.BlockSpec((1, tn), lambda j: (0, j))],
        out_specs=pl.BlockSpec((8, tn), lambda j: (0, j)),
        out_shape=jax.ShapeDtypeStruct((8, n), F32),
        compiler_params=_cparams(("parallel",)),
        name="ada_params",
    )(cond8, w, b.reshape(1, n))


def _modnorm_matmul_kernel(x_ref, g_ref, sh_ref, sc_ref, w_ref, o_ref):
    h = _modnorm(x_ref[...], g_ref[...], sh_ref[0], sc_ref[0])
    o_ref[...] = _dot(h.astype(BF16), w_ref[...])


def modnorm_matmul(x, gain, shift, scale, w_bf16, tokens_per_group, tm=256):
    t, d = x.shape
    n = w_bf16.shape[1]
    tiles_per_group = tokens_per_group // tm
    grp = lambda i: (i // tiles_per_group, 0, 0)
    return pl.pallas_call(
        _modnorm_matmul_kernel,
        grid=(t // tm,),
        in_specs=[pl.BlockSpec((tm, d), lambda i: (i, 0)),
                  pl.BlockSpec((1, d), lambda i: (0, 0)),
                  pl.BlockSpec((1, 1, d), grp),
                  pl.BlockSpec((1, 1, d), grp),
                  pl.BlockSpec((d, n), lambda i: (0, 0))],
        out_specs=pl.BlockSpec((tm, n), lambda i: (i, 0)),
        out_shape=jax.ShapeDtypeStruct((t, n), F32),
        compiler_params=_cparams(("parallel",)),
        name="modnorm_matmul",
    )(x, gain.reshape(1, d), shift, scale, w_bf16)


def _hgrn_constants():
    c = HG_CHUNK
    t = np.arange(c)[:, None]
    u = np.arange(c)[None, :]
    mats = [(u <= t), (u > t)]
    for l in range(HG_MXU_LEVELS):
        m = 1 << l
        r = (t // (2 * m)) * (2 * m) + m - 1
        mats.append((u > np.minimum(t, r)) & (u <= np.maximum(t, r)))
    fwd = np.concatenate(mats, axis=0).astype(np.float32)
    bwd = np.concatenate([mm[::-1, ::-1] for mm in mats], axis=0).astype(np.float32)
    x = np.bitwise_xor(t, u)
    lvl = np.where(x > 0, np.floor(np.log2(np.maximum(x, 1))), HG_LEVELS).astype(np.int32)
    lv_f = np.where(t >= u, lvl, -1).astype(np.int32)
    return fwd, bwd, lv_f, lv_f.T.copy()


def _hgrn_direction(qs, fpres, vs, lbs, mcat, lv, sts, forward):
    c = HG_CHUNK
    n = len(qs)
    fs = [lb + (1.0 - lb) * _sigmoid(fp) for lb, fp in zip(lbs, fpres)]
    kks = [1.0 - f for f in fs]
    parts = []
    for f in fs:
        parts += list(_split2(jnp.log(f)))
    x_all = _dot(mcat, jnp.concatenate(parts, axis=1))
    xs = [x_all[:, 2 * i * c:(2 * i + 1) * c] + x_all[:, (2 * i + 1) * c:(2 * i + 2) * c] for i in range(n)]
    gs = [x[0:c] for x in xs]
    qbs = [q.astype(BF16) for q in qs]
    kbs = [kk.astype(BF16) for kk in kks]
    vbs = [v.astype(BF16) for v in vs]
    os_ = [_dot_nt((q * jnp.exp(g)).astype(BF16), st.astype(BF16)) for q, g, st in zip(qs, gs, sts)]
    accs = [jnp.where(lv == HG_LEVELS, _dot_nt(qb, kb), 0.0) for qb, kb in zip(qbs, kbs)]
    for l in range(HG_LEVELS):
        for i in range(n):
            if l < HG_MXU_LEVELS:
                x = xs[i][(2 + l) * c:(3 + l) * c]
            else:
                m = 1 << l
                ref_rows = [j * 2 * m + (m - 1 if forward else m) for j in range(c // (2 * m))]
                g_ref = jnp.concatenate(
                    [jnp.broadcast_to(gs[i][r:r + 1, :], (2 * m, c)) for r in ref_rows], axis=0)
                x = -jnp.abs(gs[i] - g_ref)
            e = jnp.exp(x)
            p = _dot_nt((qs[i] * e).astype(BF16), (kks[i] * e).astype(BF16))
            accs[i] = jnp.where(lv == l, p, accs[i])
    edge_row = c - 1 if forward else 0
    outs, new_sts = [], []
    for i in range(n):
        outs.append(os_[i] + _dot(accs[i].astype(BF16), vbs[i]))
        k_end = (kks[i] * jnp.exp(xs[i][c:2 * c])).astype(BF16)
        new_sts.append(sts[i] * jnp.exp(gs[i][edge_row:edge_row + 1, :]) + _dot_tn(vbs[i], k_end))
    return outs, new_sts


def _hgrn_kernel(*refs, has_init):
    if has_init:
        (qf_ref, qb_ref, ff_ref, fb_ref, vf_ref, vb_ref, lbl_ref, mf_ref, mb_ref, lvf_ref, lvb_ref,
         s0f_ref, s0b_ref, of_ref, ob_ref, sf_ref, sb_ref, stf, stb) = refs
    else:
        (qf_ref, qb_ref, ff_ref, fb_ref, vf_ref, vb_ref, lbl_ref, mf_ref, mb_ref, lvf_ref, lvb_ref,
         of_ref, ob_ref, sf_ref, sb_ref, stf, stb) = refs
    c = pl.program_id(1)
    nc = pl.num_programs(1)

    @pl.when(c == 0)
    def _():
        for h in range(HG_HEADS):
            if has_init:
                stf[h] = s0f_ref[0, h].T
                stb[h] = s0b_ref[0, h].T
            else:
                stf[h] = jnp.zeros((HG_DV, HG_DK), F32)
                stb[h] = jnp.zeros((HG_DV, HG_DK), F32)

    rows = [lbl_ref[:, j, :] for j in range(lbl_ref.shape[1])]
    mx = functools.reduce(jnp.maximum, rows)
    ex = [jnp.exp(r - mx) for r in rows]
    lb = ex[0] / functools.reduce(lambda a, b: a + b, ex)

    heads = [slice(h * LANES, (h + 1) * LANES) for h in range(HG_HEADS)]
    o_f, st_f = _hgrn_direction([qf_ref[0, :, hs] for hs in heads], [ff_ref[0, :, hs] for hs in heads],
                                [vf_ref[0, :, hs] for hs in heads], [lb[0:1, hs] for hs in heads],
                                mf_ref[...], lvf_ref[...], [stf[h] for h in range(HG_HEADS)], True)
    o_b, st_b = _hgrn_direction([qb_ref[0, :, hs] for hs in heads], [fb_ref[0, :, hs] for hs in heads],
                                [vb_ref[0, :, hs] for hs in heads], [lb[1:2, hs] for hs in heads],
                                mb_ref[...], lvb_ref[...], [stb[h] for h in range(HG_HEADS)], False)
    for h, hs in enumerate(heads):
        of_ref[0, :, hs] = o_f[h]
        ob_ref[0, :, hs] = o_b[h]
        stf[h] = st_f[h]
        stb[h] = st_b[h]

    @pl.when(c == nc - 1)
    def _():
        for h in range(HG_HEADS):
            sf_ref[0, h] = stf[h].T
            sb_ref[0, h] = stb[h].T


def hgrn_bidir(z, lb_logits, s0f, s0b, batch, seq):
    nc = seq // HG_CHUNK
    z3 = z.reshape(batch, seq, z.shape[1])
    mf, mb, lvf, lvb = _hgrn_constants()
    has_init = s0f is not None
    width = HG_HEADS * LANES
    blk = (1, HG_CHUNK, width)
    fwd = lambda off: pl.BlockSpec(blk, lambda b, c: (b, c, off))
    bwd = lambda off: pl.BlockSpec(blk, lambda b, c: (b, nc - 1 - c, off))
    full = lambda a: pl.BlockSpec(a.shape, lambda b, c: (0,) * a.ndim)
    st_spec = pl.BlockSpec((1, HG_HEADS, HG_DK, HG_DV), lambda b, c: (b, 0, 0, 0))
    consts = [jnp.asarray(mf, BF16), jnp.asarray(mb, BF16), jnp.asarray(lvf), jnp.asarray(lvb)]
    in_specs = [fwd(0), bwd(0), fwd(1), bwd(2), fwd(3), bwd(3),
                pl.BlockSpec(lb_logits.shape, lambda b, c: (0, 0, 0))]
    in_specs += [full(a) for a in consts]
    args = [z3] * 6 + [lb_logits] + consts
    if has_init:
        in_specs += [st_spec, st_spec]
        args += [s0f, s0b]
    o_shape = jax.ShapeDtypeStruct((batch, seq, width), F32)
    s_shape = jax.ShapeDtypeStruct((batch, HG_HEADS, HG_DK, HG_DV), F32)
    o_f, o_b, s_f, s_b = pl.pallas_call(
        functools.partial(_hgrn_kernel, has_init=has_init),
        grid=(batch, nc),
        in_specs=in_specs,
        out_specs=[pl.BlockSpec(blk, lambda b, c: (b, c, 0)),
                   pl.BlockSpec(blk, lambda b, c: (b, nc - 1 - c, 0)),
                   st_spec, st_spec],
        out_shape=[o_shape, o_shape, s_shape, s_shape],
        scratch_shapes=[pltpu.VMEM((HG_HEADS, HG_DV, HG_DK), F32), pltpu.VMEM((HG_HEADS, HG_DV, HG_DK), F32)],
        compiler_params=_cparams(("parallel", "arbitrary")),
        name="hgrn_bidir",
    )(*args)
    t = batch * seq
    return o_f.reshape(t, -1), o_b.reshape(t, -1), s_f, s_b


def _axial_tables(n_tokens, n_rot):
    t = jnp.arange(n_tokens)
    row = (t // GRID_W).astype(F32)
    col = (t % GRID_W).astype(F32)
    n_freq = n_rot // 4
    inv = jnp.power(ROPE_THETA, -jnp.arange(n_freq, dtype=F32) / n_freq)
    ang = jnp.concatenate([row[:, None] * inv, col[:, None] * inv], axis=-1)
    return jnp.cos(ang), jnp.sin(ang)


def _rope_lane_tables(n_tokens, n_rot, head_width, first_rot_lane):
    cos, sin = _axial_tables(n_tokens, n_rot)
    half = n_rot // 2
    c_head = jnp.ones((n_tokens, head_width), F32)
    sa_head = jnp.zeros((n_tokens, head_width), F32)
    sb_head = jnp.zeros((n_tokens, head_width), F32)
    a0, a1, a2 = first_rot_lane, first_rot_lane + half, first_rot_lane + n_rot
    c_head = c_head.at[:, a0:a1].set(cos).at[:, a1:a2].set(cos)
    sa_head = sa_head.at[:, a0:a1].set(-sin)
    sb_head = sb_head.at[:, a1:a2].set(sin)
    reps = LANES // head_width
    tile = lambda a: jnp.tile(a, (1, reps))
    return tile(c_head), tile(sa_head), tile(sb_head)


def _rope(x, c, sa, sb, half):
    return x * c + pltpu.roll(x, LANES - half, 1) * sa + pltpu.roll(x, half, 1) * sb


def _mla_q_kernel(*refs, rope):
    if rope:
        qa_ref, qan_ref, qb_ref, qn_ref, c_ref, sa_ref, sb_ref, o_ref = refs
    else:
        qa_ref, qan_ref, qb_ref, qn_ref, o_ref = refs
    qa = qa_ref[...]
    qn = qa * lax.rsqrt(jnp.mean(qa * qa, axis=-1, keepdims=True) + EPS) * qan_ref[...]
    qfull = _dot(qn.astype(BF16), qb_ref[...])
    outs = []
    for h in range(MLA_HEADS):
        qh = qfull[:, h * LANES:(h + 1) * LANES]
        ms = jnp.sum(qh * qh, axis=-1, keepdims=True) * (1.0 / MLA_QK)
        qh = qh * lax.rsqrt(ms + EPS) * qn_ref[...]
        if rope:
            qh = _rope(qh, c_ref[...], sa_ref[...], sb_ref[...], MLA_ROPE // 2)
        outs.append((qh * MLA_SCALE).astype(BF16))
    o_ref[...] = jnp.concatenate(outs, axis=1)


def mla_queries(z, qa_norm, q_b_pad, q_norm_pad, tables, tokens_per_batch, tm=256):
    t = z.shape[0]
    rope = tables is not None
    row = lambda i: (0, 0)
    in_specs = [pl.BlockSpec((tm, MLA_Q_LORA), lambda i: (i, 2560 // MLA_Q_LORA)),
                pl.BlockSpec((1, MLA_Q_LORA), row),
                pl.BlockSpec(q_b_pad.shape, row),
                pl.BlockSpec((1, LANES), row)]
    args = [z, qa_norm.reshape(1, -1), q_b_pad, q_norm_pad]
    if rope:
        tpb = tokens_per_batch // tm
        in_specs += [pl.BlockSpec((tm, LANES), lambda i: (i % tpb, 0))] * 3
        args += list(tables)
    return pl.pallas_call(
        functools.partial(_mla_q_kernel, rope=rope),
        grid=(t // tm,),
        in_specs=in_specs,
        out_specs=pl.BlockSpec((tm, MLA_HEADS * LANES), lambda i: (i, 0)),
        out_shape=jax.ShapeDtypeStruct((t, MLA_HEADS * LANES), BF16),
        compiler_params=_cparams(("parallel",)),
        name="mla_queries",
    )(*args)


def _mla_kv_kernel(*refs, norm_input, rope):
    if rope:
        kva_ref, kpe_ref, kvan_ref, wk_ref, wv_ref, kn_ref, c_ref, sa_ref, sb_ref, ckv_ref, k_ref, v_ref = refs
    else:
        kva_ref, kpe_ref, kvan_ref, wk_ref, wv_ref, kn_ref, ckv_ref, k_ref, v_ref = refs
    ckv = kva_ref[...]
    if norm_input:
        ckv = ckv * lax.rsqrt(jnp.mean(ckv * ckv, axis=-1, keepdims=True) + EPS) * kvan_ref[...]
    ckv_ref[...] = ckv
    cb = ckv.astype(BF16)
    knope = _dot(cb, wk_ref[...])
    v_ref[...] = _dot(cb, wv_ref[...]).astype(BF16)
    kpe = pltpu.roll(kpe_ref[...], MLA_NOPE, 1)
    outs = []
    for h in range(MLA_HEADS):
        kh = knope[:, h * LANES:(h + 1) * LANES] + kpe
        ms = jnp.sum(kh * kh, axis=-1, keepdims=True) * (1.0 / MLA_QK)
        kh = kh * lax.rsqrt(ms + EPS) * kn_ref[...]
        if rope:
            kh = _rope(kh, c_ref[...], sa_ref[...], sb_ref[...], MLA_ROPE // 2)
        outs.append(kh.astype(BF16))
    k_ref[...] = jnp.concatenate(outs, axis=1)


def mla_keys_values(kva_src, kva_col, kpe_src, kpe_col, kva_norm, wk_pad, wv, k_norm_pad, tables,
                    tokens_per_batch, norm_input, tm=256):
    t = kva_src.shape[0]
    rope = tables is not None
    row = lambda i: (0, 0)
    in_specs = [pl.BlockSpec((tm, LANES), lambda i: (i, kva_col)),
                pl.BlockSpec((tm, LANES), lambda i: (i, kpe_col)),
                pl.BlockSpec((1, LANES), row),
                pl.BlockSpec(wk_pad.shape, row),
                pl.BlockSpec(wv.shape, row),
                pl.BlockSpec((1, LANES), row)]
    args = [kva_src, kpe_src, kva_norm.reshape(1, -1), wk_pad, wv, k_norm_pad]
    if rope:
        tpb = tokens_per_batch // tm
        in_specs += [pl.BlockSpec((tm, LANES), lambda i: (i % tpb, 0))] * 3
        args += list(tables)
    return pl.pallas_call(
        functools.partial(_mla_kv_kernel, norm_input=norm_input, rope=rope),
        grid=(t // tm,),
        in_specs=in_specs,
        out_specs=[pl.BlockSpec((tm, LANES), lambda i: (i, 0)),
                   pl.BlockSpec((tm, MLA_HEADS * LANES), lambda i: (i, 0)),
                   pl.BlockSpec((tm, MLA_HEADS * MLA_V), lambda i: (i, 0))],
        out_shape=[jax.ShapeDtypeStruct((t, LANES), F32),
                   jax.ShapeDtypeStruct((t, MLA_HEADS * LANES), BF16),
                   jax.ShapeDtypeStruct((t, MLA_HEADS * MLA_V), BF16)],
        compiler_params=_cparams(("parallel",)),
        name="mla_keys_values",
    )(*args)


def _mla_attn_kernel(q_ref, k_ref, v_ref, o_ref):
    outs = []
    for j in range(2):
        q = q_ref[0][:, j * LANES:(j + 1) * LANES]
        k = k_ref[0][:, j * LANES:(j + 1) * LANES]
        v = v_ref[0][:, j * MLA_V:(j + 1) * MLA_V]
        s = _dot_nt(q, k)
        p = jnp.exp(s - jnp.max(s, axis=-1, keepdims=True))
        l = jnp.sum(p, axis=-1, keepdims=True)
        outs.append(_dot(p.astype(BF16), v) / l)
    o_ref[0] = jnp.concatenate(outs, axis=1)


def mla_attention(q, k, v, batch, n_q, n_k, tq=256):
    q3 = q.reshape(batch, n_q, -1)
    k3 = k.reshape(batch, n_k, -1)
    v3 = v.reshape(batch, n_k, -1)
    out = pl.pallas_call(
        _mla_attn_kernel,
        grid=(batch, MLA_HEADS // 2, n_q // tq),
        in_specs=[pl.BlockSpec((1, tq, 2 * LANES), lambda b, j, i: (b, i, j)),
                  pl.BlockSpec((1, n_k, 2 * LANES), lambda b, j, i: (b, 0, j)),
                  pl.BlockSpec((1, n_k, 2 * MLA_V), lambda b, j, i: (b, 0, j))],
        out_specs=pl.BlockSpec((1, tq, 2 * MLA_V), lambda b, j, i: (b, i, j)),
        out_shape=jax.ShapeDtypeStruct((batch, n_q, MLA_HEADS * MLA_V), F32),
        compiler_params=_cparams(("parallel", "parallel", "arbitrary")),
        name="mla_attention",
    )(q3, k3, v3)
    return out.reshape(batch * n_q, -1)


def _out0_kernel(of_ref, ob_ref, ag_ref, hgn_ref, om_ref, w_ref, x_ref, g_ref, o_ref):
    o = of_ref[...] + ob_ref[...]
    ag = ag_ref[...]
    parts = []
    for h in range(HG_HEADS):
        oh = o[:, h * HG_DV:(h + 1) * HG_DV]
        oh = oh * lax.rsqrt(jnp.mean(oh * oh, axis=-1, keepdims=True) + EPS) * hgn_ref[...]
        gh = ag[:, h * HG_DV:(h + 1) * HG_DV]
        parts.append((oh * (gh * _sigmoid(gh))).astype(BF16))
    oa = jnp.concatenate(parts, axis=1)
    n_a = HG_HEADS * HG_DV
    mix = _dot(oa, w_ref[0:n_a, :]) + _dot(om_ref[...].astype(BF16), w_ref[n_a:, :])
    o_ref[...] = x_ref[...] + g_ref[0] * mix


def out_proj_layer0(o_f, o_b, z, hg_norm, o_mla, w_out_bf16, x, gate, tokens_per_group, tm=256):
    t, d = x.shape
    n_a = HG_HEADS * HG_DV
    tiles_per_group = tokens_per_group // tm
    tile = lambda w: pl.BlockSpec((tm, w), lambda i: (i, 0))
    return pl.pallas_call(
        _out0_kernel,
        grid=(t // tm,),
        in_specs=[tile(n_a), tile(n_a),
                  pl.BlockSpec((tm, n_a), lambda i: (i, 2048 // n_a)),
                  pl.BlockSpec((1, HG_DV), lambda i: (0, 0)),
                  tile(o_mla.shape[1]),
                  pl.BlockSpec(w_out_bf16.shape, lambda i: (0, 0)),
                  tile(d),
                  pl.BlockSpec((1, 1, d), lambda i: (i // tiles_per_group, 0, 0))],
        out_specs=tile(d),
        out_shape=jax.ShapeDtypeStruct((t, d), F32),
        compiler_params=_cparams(("parallel",)),
        name="out_proj_layer0",
    )(o_f, o_b, z, hg_norm.reshape(1, -1), o_mla, w_out_bf16, x, gate)


def _out1_kernel(a_ref, w_ref, x_ref, g_ref, o_ref):
    o_ref[...] = x_ref[...] + g_ref[0] * _dot(a_ref[...].astype(BF16), w_ref[...])


def out_proj_layer1(a, w_out_bf16, x, gate, tokens_per_group, tm=256):
    t, d = x.shape
    tiles_per_group = tokens_per_group // tm
    tile = lambda w: pl.BlockSpec((tm, w), lambda i: (i, 0))
    return pl.pallas_call(
        _out1_kernel,
        grid=(t // tm,),
        in_specs=[tile(a.shape[1]),
                  pl.BlockSpec(w_out_bf16.shape, lambda i: (0, 0)),
                  tile(d),
                  pl.BlockSpec((1, 1, d), lambda i: (i // tiles_per_group, 0, 0))],
        out_specs=tile(d),
        out_shape=jax.ShapeDtypeStruct((t, d), F32),
        compiler_params=_cparams(("parallel",)),
        name="out_proj_layer1",
    )(a, w_out_bf16, x, gate)


def _head_rms(x, gain2):
    sq = x * x
    lane = lax.broadcasted_iota(jnp.int32, x.shape, 1)
    first = lane < SWA_HD
    lo = jnp.sum(jnp.where(first, sq, 0.0), axis=-1, keepdims=True)
    hi = jnp.sum(jnp.where(first, 0.0, sq), axis=-1, keepdims=True)
    ms = jnp.where(first, lo, hi) * (1.0 / SWA_HD)
    return x * lax.rsqrt(ms + EPS) * gain2


def _swa_prep_kernel(*refs, rope):
    if rope:
        zq_ref, zk_ref, zv_ref, qn_ref, kn_ref, c_ref, sa_ref, sb_ref, q_ref, kc_ref, k_ref, v_ref = refs
    else:
        zq_ref, zk_ref, zv_ref, qn_ref, kn_ref, q_ref, kc_ref, k_ref, v_ref = refs
    half = SWA_HD // 2

    def rot(x):
        return _rope(x, c_ref[...], sa_ref[...], sb_ref[...], half) if rope else x

    zq = zq_ref[...]
    qs = []
    for p in range(zq.shape[1] // LANES):
        x = _head_rms(zq[:, p * LANES:(p + 1) * LANES], qn_ref[...])
        qs.append((rot(x) * SWA_SCALE).astype(BF16))
    q_ref[...] = jnp.concatenate(qs, axis=1)
    zk = zk_ref[...]
    kn, kr = [], []
    for p in range(zk.shape[1] // LANES):
        x = _head_rms(zk[:, p * LANES:(p + 1) * LANES], kn_ref[...])
        kn.append(x)
        kr.append(rot(x).astype(BF16))
    kc_ref[...] = jnp.concatenate(kn, axis=1)
    k_ref[...] = jnp.concatenate(kr, axis=1)
    v_ref[...] = zv_ref[...].astype(BF16)


def swa_prep(z, q_norm2, k_norm2, tables, tokens_per_batch, tm=256):
    t = z.shape[0]
    nq = SWA_HEADS * SWA_HD
    nkv = SWA_KV_HEADS * SWA_HD
    rope = tables is not None
    row = lambda i: (0, 0)
    in_specs = [pl.BlockSpec((tm, nq), lambda i: (i, 0)),
                pl.BlockSpec((tm, nkv), lambda i: (i, nq // nkv)),
                pl.BlockSpec((tm, nkv), lambda i: (i, nq // nkv + 1)),
                pl.BlockSpec((1, LANES), row),
                pl.BlockSpec((1, LANES), row)]
    args = [z, z, z, q_norm2, k_norm2]
    if rope:
        tpb = tokens_per_batch // tm
        in_specs += [pl.BlockSpec((tm, LANES), lambda i: (i % tpb, 0))] * 3
        args += list(tables)
    tile = lambda w: pl.BlockSpec((tm, w), lambda i: (i, 0))
    return pl.pallas_call(
        functools.partial(_swa_prep_kernel, rope=rope),
        grid=(t // tm,),
        in_specs=in_specs,
        out_specs=[tile(nq), tile(nkv), tile(nkv), tile(nkv)],
        out_shape=[jax.ShapeDtypeStruct((t, nq), BF16),
                   jax.ShapeDtypeStruct((t, nkv), F32),
                   jax.ShapeDtypeStruct((t, nkv), BF16),
                   jax.ShapeDtypeStruct((t, nkv), BF16)],
        compiler_params=_cparams(("parallel",)),
        name="swa_prep",
    )(*args)


def _sink_softmax_pv(s, sk, vn):
    m = jnp.maximum(jnp.max(s, axis=-1, keepdims=True), sk)
    p = jnp.exp(s - m)
    l = jnp.sum(p, axis=-1, keepdims=True) + jnp.exp(sk - m)
    return _dot(p.astype(BF16), vn) / l


def _sink_attention(q8, k_all, v_all, bias, sink_ref, pair, stack):
    outs = []
    tq = q8.shape[0]
    for n in range(2):
        kn = k_all[:, n * SWA_HD:(n + 1) * SWA_HD]
        vn = v_all[:, n * SWA_HD:(n + 1) * SWA_HD]
        heads = [n * SWA_GROUP + g for g in range(SWA_GROUP)]
        qs = [q8[:, hq * SWA_HD:(hq + 1) * SWA_HD] for hq in heads]
        sinks = [sink_ref[pair * 2 * SWA_GROUP + hq] for hq in heads]
        if stack:
            sk = jnp.concatenate([jnp.full((tq, 1), s, F32) for s in sinks], axis=0)
            s = _dot_nt(jnp.concatenate(qs, axis=0), kn)
            if bias is not None:
                s = s + bias
            o4 = _sink_softmax_pv(s, sk, vn)
            outs += [o4[g * tq:(g + 1) * tq] for g in range(SWA_GROUP)]
        else:
            for q, sk in zip(qs, sinks):
                s = _dot_nt(q, kn)
                if bias is not None:
                    s = s + bias
                outs.append(_sink_softmax_pv(s, sk, vn))
    return jnp.concatenate(outs, axis=1)


def _swa_dense_kernel(sink_ref, q_ref, k_ref, v_ref, o_ref):
    o_ref[0] = _sink_attention(q_ref[0], k_ref[0], v_ref[0], None, sink_ref, pl.program_id(1), stack=False)


def swa_dense(q, k, v, sink, batch, seq):
    nq = SWA_HEADS * SWA_HD
    q3 = q.reshape(batch, seq, nq)
    k3 = k.reshape(batch, seq, -1)
    v3 = v.reshape(batch, seq, -1)
    out = pl.pallas_call(
        _swa_dense_kernel,
        grid=(batch, 2),
        in_specs=[pl.BlockSpec(memory_space=pltpu.SMEM),
                  pl.BlockSpec((1, seq, nq // 2), lambda b, j: (b, 0, j)),
                  pl.BlockSpec((1, seq, LANES), lambda b, j: (b, 0, j)),
                  pl.BlockSpec((1, seq, LANES), lambda b, j: (b, 0, j))],
        out_specs=pl.BlockSpec((1, seq, nq // 2), lambda b, j: (b, 0, j)),
        out_shape=jax.ShapeDtypeStruct((batch, seq, nq), F32),
        compiler_params=_cparams(("parallel", "parallel")),
        name="swa_dense",
    )(sink, q3, k3, v3)
    return out.reshape(batch * seq, nq)


def _swa_window_kernel(sink_ref, q_ref, kc_ref, vc_ref, kp_ref, k0_ref, kn_ref, vp_ref, v0_ref, vn_ref, o_ref):
    i = pl.program_id(2)
    nb = pl.num_programs(2)
    w = SWA_WINDOW
    n_ctx = kc_ref.shape[1]
    k_all = jnp.concatenate([kc_ref[0], kp_ref[0], k0_ref[0], kn_ref[0]], axis=0)
    v_all = jnp.concatenate([vc_ref[0], vp_ref[0], v0_ref[0], vn_ref[0]], axis=0)
    shape = (SWA_GROUP * w, n_ctx + 3 * w)
    r = lax.broadcasted_iota(jnp.int32, shape, 0) % w
    col = lax.broadcasted_iota(jnp.int32, shape, 1) - n_ctx
    prev_bias = jnp.where(i > 0, 0.0, NEG_BIG)
    next_bias = jnp.where(i < nb - 1, 0.0, NEG_BIG)
    bias = jnp.where(
        col < w,
        jnp.where(col < 0, 0.0, jnp.where(col >= r, prev_bias, NEG_BIG)),
        jnp.where(col < 2 * w, 0.0, jnp.where(col - 2 * w <= r, next_bias, NEG_BIG)))
    o_ref[0] = _sink_attention(q_ref[0], k_all, v_all, bias, sink_ref, pl.program_id(1), stack=True)


def swa_window(q, k, v, k_ctx, v_ctx, sink, batch, seq, n_ctx):
    nq = SWA_HEADS * SWA_HD
    w = SWA_WINDOW
    nb = seq // w
    q3 = q.reshape(batch, seq, nq)
    k3 = k.reshape(batch, seq, -1)
    v3 = v.reshape(batch, seq, -1)
    ctx = pl.BlockSpec((1, n_ctx, LANES), lambda b, j, i: (b, 0, j))
    prv = pl.BlockSpec((1, w, LANES), lambda b, j, i: (b, jnp.maximum(i - 1, 0), j))
    cur = pl.BlockSpec((1, w, LANES), lambda b, j, i: (b, i, j))
    nxt = pl.BlockSpec((1, w, LANES), lambda b, j, i: (b, jnp.minimum(i + 1, nb - 1), j))
    out = pl.pallas_call(
        _swa_window_kernel,
        grid=(batch, 2, nb),
        in_specs=[pl.BlockSpec(memory_space=pltpu.SMEM),
                  pl.BlockSpec((1, w, nq // 2), lambda b, j, i: (b, i, j)),
                  ctx, ctx, prv, cur, nxt, prv, cur, nxt],
        out_specs=pl.BlockSpec((1, w, nq // 2), lambda b, j, i: (b, i, j)),
        out_shape=jax.ShapeDtypeStruct((batch, seq, nq), F32),
        compiler_params=_cparams(("parallel", "parallel", "arbitrary")),
        name="swa_window",
    )(sink, q3, k_ctx, v_ctx, k3, k3, k3, v3, v3, v3)
    return out.reshape(batch * seq, nq)


def _router_kernel(x_ref, g_ref, sh_ref, sc_ref, rw_ref, rb_ref, tri_ref, h_ref, idx_ref, gate_ref, rank_ref,
                   cnt_ref, cnt):
    @pl.when(pl.program_id(0) == 0)
    def _():
        cnt[...] = jnp.zeros_like(cnt)

    h = _modnorm(x_ref[...], g_ref[...], sh_ref[0], sc_ref[0])
    h_ref[...] = h
    logits = _dot_hp(h, rw_ref[...]) + rb_ref[...]
    lane = lax.broadcasted_iota(jnp.int32, logits.shape, 1)
    work = logits
    vals, idxs = [], []
    for _ in range(TOP_K):
        m = jnp.max(work, axis=-1, keepdims=True)
        ik = jnp.min(jnp.where(work == m, lane, LANES), axis=-1, keepdims=True)
        vals.append(m)
        idxs.append(ik)
        work = jnp.where(lane == ik, 2.0 * NEG_BIG, work)
    es = [jnp.exp(v - vals[0]) for v in vals]
    denom = es[0] + es[1] + es[2] + es[3]
    idx_out = jnp.zeros(logits.shape, jnp.int32)
    gate_out = jnp.zeros(logits.shape, F32)
    for k in range(TOP_K):
        idx_out = jnp.where(lane == k, idxs[k], idx_out)
        gate_out = jnp.where(lane == k, es[k] / denom, gate_out)
    idx_ref[...] = idx_out
    gate_ref[...] = gate_out
    chosen = jnp.zeros(logits.shape, F32)
    for k in range(TOP_K):
        chosen = jnp.where(lane == idxs[k], 1.0, chosen)
    before = _dot(tri_ref[...], chosen.astype(BF16)) + cnt[...]
    rank_out = jnp.zeros(logits.shape, jnp.int32)
    for k in range(TOP_K):
        rk = jnp.sum(jnp.where(lane == idxs[k], before, 0.0), axis=-1, keepdims=True)
        rank_out = jnp.where(lane == k, rk.astype(jnp.int32), rank_out)
    rank_ref[...] = rank_out
    cnt[...] = cnt[...] + jnp.sum(chosen, axis=0, keepdims=True)
    cnt_ref[...] = cnt[...]


def moe_router(x, gain, shift, scale, rw_pad, rb_pad, tokens_per_group, tm=256):
    t, d = x.shape
    tiles_per_group = tokens_per_group // tm
    grp = lambda i: (i // tiles_per_group, 0, 0)
    tile = lambda w: pl.BlockSpec((tm, w), lambda i: (i, 0))
    tri = jnp.asarray(np.tril(np.ones((tm, tm), np.float32), -1), BF16)
    return pl.pallas_call(
        _router_kernel,
        grid=(t // tm,),
        in_specs=[tile(d),
                  pl.BlockSpec((1, d), lambda i: (0, 0)),
                  pl.BlockSpec((1, 1, d), grp),
                  pl.BlockSpec((1, 1, d), grp),
                  pl.BlockSpec((d, LANES), lambda i: (0, 0)),
                  pl.BlockSpec((1, LANES), lambda i: (0, 0)),
                  pl.BlockSpec((tm, tm), lambda i: (0, 0))],
        out_specs=[tile(d), tile(LANES), tile(LANES), tile(LANES),
                   pl.BlockSpec((1, LANES), lambda i: (0, 0))],
        out_shape=[jax.ShapeDtypeStruct((t, d), F32),
                   jax.ShapeDtypeStruct((t, LANES), jnp.int32),
                   jax.ShapeDtypeStruct((t, LANES), F32),
                   jax.ShapeDtypeStruct((t, LANES), jnp.int32),
                   jax.ShapeDtypeStruct((1, LANES), F32)],
        scratch_shapes=[pltpu.VMEM((1, LANES), F32)],
        compiler_params=_cparams(("arbitrary",)),
        name="moe_router",
    )(x, gain.reshape(1, d), shift, scale, rw_pad, rb_pad, tri)


def _row_gather_start(src_hbm, dst_rows, sem, row_of, n_rows):
    def body(r, carry):
        pltpu.make_async_copy(src_hbm.at[pl.ds(row_of(r), 1), :], dst_rows.at[pl.ds(r, 1), :], sem).start()
        return carry
    lax.fori_loop(0, n_rows, body, 0, unroll=8)


FFN_COLS = 256
FFN_PIECES = 2 * D_FF // FFN_COLS + D_MODEL // FFN_COLS
SEM_GATHER_A, SEM_GATHER_B, SEM_WRITE_A, SEM_WRITE_B = range(4)
GATHER_STRIDE = 37


def _ffn_block(x_ref, y_ref, wgu, bgu_ref, wd, bd_ref, issue):
    xb = x_ref[...].astype(BF16)
    acts = []
    for c in range(D_FF // FFN_COLS):
        lo, hi = c * FFN_COLS, (c + 1) * FFN_COLS
        gate = _dot(xb, wgu[:, lo:hi]) + bgu_ref[0][:, lo:hi]
        issue(2 * c)
        up = _dot(xb, wgu[:, D_FF + lo:D_FF + hi]) + bgu_ref[0][:, D_FF + lo:D_FF + hi]
        issue(2 * c + 1)
        gate = jnp.minimum(gate, SWIGLU_LIMIT)
        up = jnp.clip(up, -SWIGLU_LIMIT, SWIGLU_LIMIT)
        acts.append((gate * _sigmoid(SWIGLU_ALPHA * gate) * (up + 1.0)).astype(BF16))
    act = jnp.concatenate(acts, axis=1)
    for n in range(D_MODEL // FFN_COLS):
        lo, hi = n * FFN_COLS, (n + 1) * FFN_COLS
        y_ref[:, lo:hi] = _dot(act, wd[:, lo:hi]) + bd_ref[0][:, lo:hi]
        issue(2 * D_FF // FFN_COLS + n)


def _expert_kernel(nb_ref, bs_ref, rt_ref, h_hbm, wgu_ref, bgu_ref, wd_ref, bd_ref, y_hbm,
                   wgu_bf, wd_bf, xa, xb, ya, yb, sem):
    e = pl.program_id(0)
    nb = nb_ref[e]
    b0 = bs_ref[e]
    order = (np.arange(MOE_ROWS) * GATHER_STRIDE) % MOE_ROWS
    shares = np.array_split(order, FFN_PIECES)

    def gather_issue(dst, s, blk):
        base = blk * MOE_ROWS

        def issue(k):
            for r in shares[k]:
                r = int(r)
                pltpu.make_async_copy(h_hbm.at[pl.ds(rt_ref[base + r], 1), :], dst.at[pl.ds(r, 1), :],
                                      sem.at[s]).start()
        return issue

    def gather_wait(dst, s):
        pltpu.make_async_copy(dst, dst, sem.at[s]).wait()

    def write(src, s, blk):
        return pltpu.make_async_copy(src, y_hbm.at[pl.ds(blk * MOE_ROWS, MOE_ROWS), :], sem.at[s])

    def ffn(x_ref, y_ref, issue):
        _ffn_block(x_ref, y_ref, wgu_bf, bgu_ref, wd_bf, bd_ref, issue)

    @pl.when(nb > 0)
    def _():
        first = gather_issue(xa, SEM_GATHER_A, b0)
        n_slabs = 4
        per = FFN_PIECES // n_slabs
        for s in range(n_slabs):
            rows = slice(s * D_MODEL // n_slabs, (s + 1) * D_MODEL // n_slabs)
            wgu_bf[rows, :] = wgu_ref[0, rows, :].astype(BF16)
            wd_bf[rows, :] = wd_ref[0, rows, :].astype(BF16)
            for k in range(s * per, (s + 1) * per):
                first(k)
        n_pairs = nb // 2
        last = b0 + nb - 1

        def pair(jj, carry):
            j = b0 + 2 * jj
            gather_wait(xa, SEM_GATHER_A)

            @pl.when(jj > 0)
            def _():
                write(ya, SEM_WRITE_A, j).wait()

            ffn(xa, ya, gather_issue(xb, SEM_GATHER_B, j + 1))
            write(ya, SEM_WRITE_A, j).start()
            gather_wait(xb, SEM_GATHER_B)

            @pl.when(jj > 0)
            def _():
                write(yb, SEM_WRITE_B, j).wait()

            ffn(xb, yb, gather_issue(xa, SEM_GATHER_A, jnp.minimum(j + 2, last)))
            write(yb, SEM_WRITE_B, j + 1).start()
            return carry

        lax.fori_loop(0, n_pairs, pair, 0)
        gather_wait(xa, SEM_GATHER_A)

        @pl.when(nb % 2 == 1)
        def _():
            @pl.when(n_pairs > 0)
            def _():
                write(ya, SEM_WRITE_A, last).wait()

            ffn(xa, ya, lambda k: None)
            write(ya, SEM_WRITE_A, last).start()

        write(ya, SEM_WRITE_A, last).wait()

        @pl.when(n_pairs > 0)
        def _():
            write(yb, SEM_WRITE_B, last).wait()

    @pl.when(e == N_EXPERTS - 1)
    def _():
        n_blocks = y_hbm.shape[0] // MOE_ROWS
        ya[...] = jnp.zeros_like(ya)

        def fill(blk, carry):
            write(ya, SEM_WRITE_A, blk).start()
            return carry

        def drain(blk, carry):
            write(ya, SEM_WRITE_A, blk).wait()
            return carry

        lax.fori_loop(b0 + nb, n_blocks, fill, 0)
        lax.fori_loop(b0 + nb, n_blocks, drain, 0)


def moe_experts(h, n_blk, blk_start, row_tok, w_gu, b_gu, w_down, b_down):
    n_rows = row_tok.shape[0]
    d = h.shape[1]
    grid_spec = pltpu.PrefetchScalarGridSpec(
        num_scalar_prefetch=3,
        grid=(N_EXPERTS,),
        in_specs=[pl.BlockSpec(memory_space=pl.ANY),
                  pl.BlockSpec((1, d, 2 * D_FF), lambda e, nb, bs, rt: (e, 0, 0)),
                  pl.BlockSpec((1, 1, 2 * D_FF), lambda e, nb, bs, rt: (e, 0, 0)),
                  pl.BlockSpec((1, D_FF, d), lambda e, nb, bs, rt: (e, 0, 0)),
                  pl.BlockSpec((1, 1, d), lambda e, nb, bs, rt: (e, 0, 0))],
        out_specs=pl.BlockSpec(memory_space=pl.ANY),
        scratch_shapes=[pltpu.VMEM((d, 2 * D_FF), BF16), pltpu.VMEM((D_FF, d), BF16),
                        pltpu.VMEM((MOE_ROWS, d), F32), pltpu.VMEM((MOE_ROWS, d), F32),
                        pltpu.VMEM((MOE_ROWS, d), F32), pltpu.VMEM((MOE_ROWS, d), F32),
                        pltpu.SemaphoreType.DMA((4,))],
    )
    return pl.pallas_call(
        _expert_kernel,
        grid_spec=grid_spec,
        out_shape=jax.ShapeDtypeStruct((n_rows, d), F32),
        compiler_params=pltpu.CompilerParams(dimension_semantics=("arbitrary",),
                                             vmem_limit_bytes=EXPERT_VMEM_LIMIT),
        name="moe_experts",
    )(n_blk, blk_start, row_tok, h, w_gu, b_gu.reshape(N_EXPERTS, 1, -1), w_down, b_down.reshape(N_EXPERTS, 1, -1))


COMBINE_TM = 128


def _combine_kernel(dest_ref, y_hbm, gate_ref, x_ref, g_ref, o_ref, buf, sem):
    i = pl.program_id(0)
    n = pl.num_programs(0)
    slot = i % 2

    def start(tile, s):
        for k in range(TOP_K):
            _row_gather_start(y_hbm, buf.at[s, k], sem.at[s],
                              lambda r: dest_ref[(tile * COMBINE_TM + r) * TOP_K + k], COMBINE_TM)

    @pl.when(i == 0)
    def _():
        start(0, 0)

    @pl.when(i + 1 < n)
    def _():
        start(i + 1, 1 - slot)

    pltpu.make_async_copy(buf.at[slot], buf.at[slot], sem.at[slot]).wait()
    gates = gate_ref[...]
    acc = gates[:, 0:1] * buf[slot, 0]
    for k in range(1, TOP_K):
        acc = acc + gates[:, k:k + 1] * buf[slot, k]
    o_ref[...] = x_ref[...] + g_ref[0] * acc


def moe_combine(yb, dest, gates, x, gate_mod, tokens_per_group):
    t, d = x.shape
    tm = COMBINE_TM
    tiles_per_group = tokens_per_group // tm
    grid_spec = pltpu.PrefetchScalarGridSpec(
        num_scalar_prefetch=1,
        grid=(t // tm,),
        in_specs=[pl.BlockSpec(memory_space=pl.ANY),
                  pl.BlockSpec((tm, LANES), lambda i, de: (i, 0)),
                  pl.BlockSpec((tm, d), lambda i, de: (i, 0)),
                  pl.BlockSpec((1, 1, d), lambda i, de: (i // tiles_per_group, 0, 0))],
        out_specs=pl.BlockSpec((tm, d), lambda i, de: (i, 0)),
        scratch_shapes=[pltpu.VMEM((2, TOP_K, tm, d), F32), pltpu.SemaphoreType.DMA((2,))],
    )
    return pl.pallas_call(
        _combine_kernel,
        grid_spec=grid_spec,
        out_shape=jax.ShapeDtypeStruct((t, d), F32),
        compiler_params=_cparams(("arbitrary",)),
        name="moe_combine",
    )(dest, yb, gates, x, gate_mod)


def _assignment_tables(top_idx, rank, counts):
    t = top_idx.shape[0]
    n_assign = t * TOP_K
    padded = (counts + MOE_ROWS - 1) // MOE_ROWS * MOE_ROWS
    pad_start = jnp.cumsum(padded) - padded
    experts = jnp.arange(N_EXPERTS, dtype=jnp.int32)
    start_of = jnp.sum(jnp.where(top_idx[:, :, None] == experts, pad_start, 0), axis=-1)
    dest = (start_of + rank).reshape(-1).astype(jnp.int32)
    n_rows = (n_assign // MOE_ROWS + N_EXPERTS) * MOE_ROWS
    row_tok = jnp.zeros((n_rows,), jnp.int32).at[dest].set(jnp.arange(n_assign, dtype=jnp.int32) // TOP_K)
    return dest, row_tok, (padded // MOE_ROWS).astype(jnp.int32), (pad_start // MOE_ROWS).astype(jnp.int32)


def moe_layer(x, p, shift, scale, gate_mod, tokens_per_group):
    h, idx, gates, rank, cnt = moe_router(x, p['norm2'], shift, scale, p['rw_pad'], p['rb_pad'], tokens_per_group)
    counts = cnt[0, :N_EXPERTS].astype(jnp.int32)
    dest, row_tok, n_blk, blk_start = _assignment_tables(idx[:, :TOP_K], rank[:, :TOP_K], counts)
    yb = moe_experts(h, n_blk, blk_start, row_tok, p['w_gu'], p['b_gu'], p['w_down'], p['b_down'])
    return moe_combine(yb, dest, gates, x, gate_mod, tokens_per_group)


def _pad_lanes(a, width):
    return jnp.pad(a, [(0, 0)] * (a.ndim - 1) + [(0, width - a.shape[-1])])


def _prep_common(p):
    p['rw_pad'] = _pad_lanes(p['router_w'], LANES)
    p['rb_pad'] = jnp.concatenate(
        [p['router_b'].astype(F32), jnp.full((LANES - N_EXPERTS,), NEG_BIG, F32)]).reshape(1, LANES)
    p['w_out'] = p['w_out'].astype(BF16)
    return p


def _prep_layer0(p):
    p = _prep_common(dict(p))
    p['w_in'] = _pad_lanes(p['w_in'], AB_IN_PAD).astype(BF16)
    q_b = p['q_b'].reshape(MLA_Q_LORA, MLA_HEADS, MLA_QK)
    p['q_b_pad'] = _pad_lanes(q_b, LANES).reshape(MLA_Q_LORA, MLA_HEADS * LANES).astype(BF16)
    kv_b = p['kv_b'].reshape(MLA_KV_LORA, MLA_HEADS, MLA_NOPE + MLA_V)
    p['wk_pad'] = _pad_lanes(kv_b[:, :, :MLA_NOPE], LANES).reshape(MLA_KV_LORA, MLA_HEADS * LANES).astype(BF16)
    p['wv'] = kv_b[:, :, MLA_NOPE:].reshape(MLA_KV_LORA, MLA_HEADS * MLA_V).astype(BF16)
    p['q_norm_pad'] = _pad_lanes(p['q_norm'].reshape(1, -1), LANES)
    p['k_norm_pad'] = _pad_lanes(p['k_norm'].reshape(1, -1), LANES)
    return p


def _prep_layer1(p):
    p = _prep_common(dict(p))
    p['w_in'] = p['w_in'].astype(BF16)
    p['q_norm2'] = jnp.tile(p['q_norm'].reshape(1, -1), (1, LANES // SWA_HD))
    p['k_norm2'] = jnp.tile(p['k_norm'].reshape(1, -1), (1, LANES // SWA_HD))
    return p


def _group_forward(x3, mods, p0, p1, lb_logits, caches, latent):
    batch, seq, d = x3.shape
    t = batch * seq
    x = x3.reshape(t, d)
    tpg = seq if latent else t
    sh1, sc1, g1, sh2, sc2, g2 = mods[0]

    z = modnorm_matmul(x, p0['norm1'], sh1, sc1, p0['w_in'], tpg)
    tab_b = _rope_lane_tables(seq, MLA_ROPE, LANES, MLA_NOPE) if latent else None
    s0f, s0b = (caches['hg_f'], caches['hg_b']) if latent else (None, None)
    o_f, o_b, s_f, s_b = hgrn_bidir(z, lb_logits, s0f, s0b, batch, seq)
    q = mla_queries(z, p0['qa_norm'], p0['q_b_pad'], p0['q_norm_pad'], tab_b, seq)
    ckv, k, v = mla_keys_values(z, 2816 // LANES, z, 2944 // LANES, p0['kva_norm'], p0['wk_pad'], p0['wv'],
                                p0['k_norm_pad'], tab_b, seq, norm_input=True)
    n_k = seq
    if latent:
        n_ctx = caches['ckv'].shape[1]
        ckv_c = caches['ckv'].reshape(batch * n_ctx, MLA_KV_LORA)
        kpe_c = _pad_lanes(caches['kpe'].reshape(batch * n_ctx, MLA_ROPE), LANES)
        _, k_c, v_c = mla_keys_values(ckv_c, 0, kpe_c, 0, p0['kva_norm'], p0['wk_pad'], p0['wv'],
                                      p0['k_norm_pad'], None, n_ctx, norm_input=False)
        cat = lambda a, b: jnp.concatenate([a.reshape(batch, n_ctx, -1), b.reshape(batch, seq, -1)],
                                           axis=1).reshape(batch * (n_ctx + seq), -1)
        k, v = cat(k_c, k), cat(v_c, v)
        n_k = n_ctx + seq
    o_mla = mla_attention(q, k, v, batch, seq, n_k)
    x = out_proj_layer0(o_f, o_b, z, p0['hg_out_norm'], o_mla, p0['w_out'], x, g1, tpg)
    x = moe_layer(x, p0, sh2, sc2, g2, tpg)
    state0 = (s_f, s_b, ckv.reshape(batch, seq, MLA_KV_LORA), z[:, 2944:2944 + MLA_ROPE].reshape(batch, seq, MLA_ROPE))

    sh1, sc1, g1, sh2, sc2, g2 = mods[1]
    z = modnorm_matmul(x, p1['norm1'], sh1, sc1, p1['w_in'], tpg)
    tab_c = _rope_lane_tables(seq, SWA_HD, SWA_HD, 0) if latent else None
    q, k_cache, k, v = swa_prep(z, p1['q_norm2'], p1['k_norm2'], tab_c, seq)
    sink = p1['sink'].astype(F32)
    if latent:
        n_ctx = caches['k1'].shape[1]
        k_c = caches['k1'].reshape(batch, n_ctx, -1).astype(BF16)
        v_c = caches['v1'].reshape(batch, n_ctx, -1).astype(BF16)
        a = swa_window(q, k, v, k_c, v_c, sink, batch, seq, n_ctx)
    else:
        a = swa_dense(q, k, v, sink, batch, seq)
    x = out_proj_layer1(a, p1['w_out'], x, g1, tpg)
    x = moe_layer(x, p1, sh2, sc2, g2, tpg)
    nkv = SWA_KV_HEADS * SWA_HD
    state1 = (k_cache.reshape(batch, seq, SWA_KV_HEADS, SWA_HD),
              z[:, SWA_HEADS * SWA_HD + nkv:].reshape(batch, seq, SWA_KV_HEADS, SWA_HD))
    return x.reshape(batch, seq, d), state0, state1


def kernel(x_prompt, x_sample, state_l0_hgrn_fwd, state_l0_hgrn_bwd, cache_l0_mla_ckv, cache_l0_mla_kpe, cache_l1_k, cache_l1_v, c, c_ctx, hgrn_lb_logits, l0_ada_w, l0_ada_b, l0_norm1, l0_norm2, l0_w_in, l0_hg_out_norm, l0_qa_norm, l0_q_b, l0_kva_norm, l0_kv_b, l0_q_norm, l0_k_norm, l0_w_out, l0_router_w, l0_router_b, l0_w_gu, l0_b_gu, l0_w_down, l0_b_down, l1_ada_w, l1_ada_b, l1_norm1, l1_norm2, l1_w_in, l1_q_norm, l1_k_norm, l1_sink, l1_w_out, l1_router_w, l1_router_b, l1_w_gu, l1_b_gu, l1_w_down, l1_b_down):
    p0 = _prep_layer0(dict(norm1=l0_norm1, norm2=l0_norm2, w_in=l0_w_in, hg_out_norm=l0_hg_out_norm,
                           qa_norm=l0_qa_norm, q_b=l0_q_b, kva_norm=l0_kva_norm, kv_b=l0_kv_b,
                           q_norm=l0_q_norm, k_norm=l0_k_norm, w_out=l0_w_out, router_w=l0_router_w,
                           router_b=l0_router_b, w_gu=l0_w_gu, b_gu=l0_b_gu, w_down=l0_w_down,
                           b_down=l0_b_down))
    p1 = _prep_layer1(dict(norm1=l1_norm1, norm2=l1_norm2, w_in=l1_w_in, q_norm=l1_q_norm, k_norm=l1_k_norm,
                           sink=l1_sink, w_out=l1_w_out, router_w=l1_router_w, router_b=l1_router_b,
                           w_gu=l1_w_gu, b_gu=l1_b_gu, w_down=l1_w_down, b_down=l1_b_down))
    dec_batch = c.shape[0]
    d = c.shape[1]
    cond8 = jnp.concatenate([c_ctx[None, :], c, jnp.zeros((8 - 1 - dec_batch, d), F32)], axis=0)
    mods_ctx, mods_lat = [], []
    for w, b in ((l0_ada_w, l0_ada_b), (l1_ada_w, l1_ada_b)):
        mod = ada_params(cond8, w, b)
        mods_ctx.append([m.reshape(1, 1, d) for m in jnp.split(mod[0:1], 6, axis=-1)])
        mods_lat.append([m.reshape(dec_batch, 1, d) for m in jnp.split(mod[1:1 + dec_batch], 6, axis=-1)])

    y_prompt, st0, st1 = _group_forward(x_prompt, mods_ctx, p0, p1, hgrn_lb_logits, None, latent=False)
    caches = dict(hg_f=state_l0_hgrn_fwd, hg_b=state_l0_hgrn_bwd, ckv=cache_l0_mla_ckv, kpe=cache_l0_mla_kpe,
                  k1=cache_l1_k, v1=cache_l1_v)
    y_sample, _, _ = _group_forward(x_sample, mods_lat, p0, p1, hgrn_lb_logits, caches, latent=True)
    return (y_prompt, y_sample, st0[0], st0[1], st0[2], st0[3], st1[0], st1[1])
```

```python
import functools

import numpy as np
import jax
import jax.numpy as jnp
from jax import lax
from jax.experimental import pallas as pl
from jax.experimental.pallas import tpu as pltpu

F32 = jnp.float32
BF16 = jnp.bfloat16

D_MODEL = 1024
GRID_W = 64
ROPE_THETA = 10000.0
EPS = 1e-6
HG_HEADS = 4
HG_DK = 128
HG_DV = 128
MLA_HEADS = 8
MLA_NOPE = 64
MLA_ROPE = 32
MLA_V = 64
MLA_QK = MLA_NOPE + MLA_ROPE
MLA_Q_LORA = 256
MLA_KV_LORA = 128
MLA_SCALE = MLA_QK ** -0.5
SWA_HEADS = 16
SWA_KV_HEADS = 4
SWA_HD = 64
SWA_WINDOW = 128
SWA_SCALE = SWA_HD ** -0.5
SWA_GROUP = SWA_HEADS // SWA_KV_HEADS
N_EXPERTS = 32
TOP_K = 4
D_FF = 1024
SWIGLU_LIMIT = 7.0
SWIGLU_ALPHA = 1.702

LANES = 128
HG_CHUNK = 128
HG_LEVELS = 7
HG_MXU_LEVELS = 3
AB_IN_PAD = 3072
MOE_ROWS = 256
NEG_BIG = -1e30
VMEM_LIMIT = 48 * 1024 * 1024
EXPERT_VMEM_LIMIT = 56 * 1024 * 1024


def _cparams(sem):
    return pltpu.CompilerParams(dimension_semantics=sem, vmem_limit_bytes=VMEM_LIMIT)


def _dot(a, b):
    return jnp.dot(a, b, preferred_element_type=F32)


def _dot_nt(a, b):
    return lax.dot_general(a, b, (((1,), (1,)), ((), ())), preferred_element_type=F32)


def _dot_tn(a, b):
    return lax.dot_general(a, b, (((0,), (0,)), ((), ())), preferred_element_type=F32)


def _split2(x):
    hi = x.astype(BF16)
    lo = (x - hi.astype(F32)).astype(BF16)
    return hi, lo


def _dot_hp(a, b):
    ah, al = _split2(a)
    bh, bl = _split2(b)
    return _dot(ah, bh) + _dot(ah, bl) + _dot(al, bh)


def _sigmoid(x):
    return 1.0 / (1.0 + jnp.exp(-x))


def _modnorm(x, gain, shift, scale):
    y = x * lax.rsqrt(jnp.mean(x * x, axis=-1, keepdims=True) + EPS)
    return y * gain * (1.0 + scale) + shift


def _ada_kernel(c_ref, w_ref, b_ref, o_ref):
    c = c_ref[...]
    o_ref[...] = _dot_hp(c * _sigmoid(c), w_ref[...]) + b_ref[...]


def ada_params(cond8, w, b):
    n = w.shape[1]
    tn = 1024
    return pl.pallas_call(
        _ada_kernel,
        grid=(n // tn,),
        in_specs=[pl.BlockSpec((8, D_MODEL), lambda j: (0, 0)),
                  pl.BlockSpec((D_MODEL, tn), lambda j: (0, j)),
                  pl.BlockSpec((1, tn), lambda j: (0, j))],
        out_specs=pl.BlockSpec((8, tn), lambda j: (0, j)),
        out_shape=jax.ShapeDtypeStruct((8, n), F32),
        compiler_params=_cparams(("parallel",)),
        name="ada_params",
    )(cond8, w, b.reshape(1, n))


def _modnorm_matmul_kernel(x_ref, g_ref, sh_ref, sc_ref, w_ref, o_ref):
    h = _modnorm(x_ref[...], g_ref[...], sh_ref[0], sc_ref[0])
    o_ref[...] = _dot(h.astype(BF16), w_ref[...])


def modnorm_matmul(x, gain, shift, scale, w_bf16, tokens_per_group, tm=256):
    t, d = x.shape
    n = w_bf16.shape[1]
    tiles_per_group = tokens_per_group // tm
    grp = lambda i: (i // tiles_per_group, 0, 0)
    return pl.pallas_call(
        _modnorm_matmul_kernel,
        grid=(t // tm,),
        in_specs=[pl.BlockSpec((tm, d), lambda i: (i, 0)),
                  pl.BlockSpec((1, d), lambda i: (0, 0)),
                  pl.BlockSpec((1, 1, d), grp),
                  pl.BlockSpec((1, 1, d), grp),
                  pl.BlockSpec((d, n), lambda i: (0, 0))],
        out_specs=pl.BlockSpec((tm, n), lambda i: (i, 0)),
        out_shape=jax.ShapeDtypeStruct((t, n), F32),
        compiler_params=_cparams(("parallel",)),
        name="modnorm_matmul",
    )(x, gain.reshape(1, d), shift, scale, w_bf16)


def _hgrn_constants():
    c = HG_CHUNK
    t = np.arange(c)[:, None]
    u = np.arange(c)[None, :]
    mats = [(u <= t), (u > t)]
    for l in range(HG_MXU_LEVELS):
        m = 1 << l
        r = (t // (2 * m)) * (2 * m) + m - 1
        mats.append((u > np.minimum(t, r)) & (u <= np.maximum(t, r)))
    fwd = np.concatenate(mats, axis=0).astype(np.float32)
    bwd = np.concatenate([mm[::-1, ::-1] for mm in mats], axis=0).astype(np.float32)
    x = np.bitwise_xor(t, u)
    lvl = np.where(x > 0, np.floor(np.log2(np.maximum(x, 1))), HG_LEVELS).astype(np.int32)
    lv_f = np.where(t >= u, lvl, -1).astype(np.int32)
    return fwd, bwd, lv_f, lv_f.T.copy()


def _hgrn_direction(qs, fpres, vs, lbs, mcat, lv, sts, forward):
    c = HG_CHUNK
    n = len(qs)
    fs = [lb + (1.0 - lb) * _sigmoid(fp) for lb, fp in zip(lbs, fpres)]
    kks = [1.0 - f for f in fs]
    parts = []
    for f in fs:
        parts += list(_split2(jnp.log(f)))
    x_all = _dot(mcat, jnp.concatenate(parts, axis=1))
    xs = [x_all[:, 2 * i * c:(2 * i + 1) * c] + x_all[:, (2 * i + 1) * c:(2 * i + 2) * c] for i in range(n)]
    gs = [x[0:c] for x in xs]
    qbs = [q.astype(BF16) for q in qs]
    kbs = [kk.astype(BF16) for kk in kks]
    vbs = [v.astype(BF16) for v in vs]
    os_ = [_dot_nt((q * jnp.exp(g)).astype(BF16), st.astype(BF16)) for q, g, st in zip(qs, gs, sts)]
    accs = [jnp.where(lv == HG_LEVELS, _dot_nt(qb, kb), 0.0) for qb, kb in zip(qbs, kbs)]
    for l in range(HG_LEVELS):
        for i in range(n):
            if l < HG_MXU_LEVELS:
                x = xs[i][(2 + l) * c:(3 + l) * c]
            else:
                m = 1 << l
                ref_rows = [j * 2 * m + (m - 1 if forward else m) for j in range(c // (2 * m))]
                g_ref = jnp.concatenate(
                    [jnp.broadcast_to(gs[i][r:r + 1, :], (2 * m, c)) for r in ref_rows], axis=0)
                x = -jnp.abs(gs[i] - g_ref)
            e = jnp.exp(x)
            p = _dot_nt((qs[i] * e).astype(BF16), (kks[i] * e).astype(BF16))
            accs[i] = jnp.where(lv == l, p, accs[i])
    edge_row = c - 1 if forward else 0
    outs, new_sts = [], []
    for i in range(n):
        outs.append(os_[i] + _dot(accs[i].astype(BF16), vbs[i]))
        k_end = (kks[i] * jnp.exp(xs[i][c:2 * c])).astype(BF16)
        new_sts.append(sts[i] * jnp.exp(gs[i][edge_row:edge_row + 1, :]) + _dot_tn(vbs[i], k_end))
    return outs, new_sts


def _hgrn_kernel(*refs, has_init):
    if has_init:
        (qf_ref, qb_ref, ff_ref, fb_ref, vf_ref, vb_ref, lbl_ref, mf_ref, mb_ref, lvf_ref, lvb_ref,
         s0f_ref, s0b_ref, of_ref, ob_ref, sf_ref, sb_ref, stf, stb) = refs
    else:
        (qf_ref, qb_ref, ff_ref, fb_ref, vf_ref, vb_ref, lbl_ref, mf_ref, mb_ref, lvf_ref, lvb_ref,
         of_ref, ob_ref, sf_ref, sb_ref, stf, stb) = refs
    c = pl.program_id(1)
    nc = pl.num_programs(1)

    @pl.when(c == 0)
    def _():
        for h in range(HG_HEADS):
            if has_init:
                stf[h] = s0f_ref[0, h].T
                stb[h] = s0b_ref[0, h].T
            else:
                stf[h] = jnp.zeros((HG_DV, HG_DK), F32)
                stb[h] = jnp.zeros((HG_DV, HG_DK), F32)

    rows = [lbl_ref[:, j, :] for j in range(lbl_ref.shape[1])]
    mx = functools.reduce(jnp.maximum, rows)
    ex = [jnp.exp(r - mx) for r in rows]
    lb = ex[0] / functools.reduce(lambda a, b: a + b, ex)

    heads = [slice(h * LANES, (h + 1) * LANES) for h in range(HG_HEADS)]
    o_f, st_f = _hgrn_direction([qf_ref[0, :, hs] for hs in heads], [ff_ref[0, :, hs] for hs in heads],
                                [vf_ref[0, :, hs] for hs in heads], [lb[0:1, hs] for hs in heads],
                                mf_ref[...], lvf_ref[...], [stf[h] for h in range(HG_HEADS)], True)
    o_b, st_b = _hgrn_direction([qb_ref[0, :, hs] for hs in heads], [fb_ref[0, :, hs] for hs in heads],
                                [vb_ref[0, :, hs] for hs in heads], [lb[1:2, hs] for hs in heads],
                                mb_ref[...], lvb_ref[...], [stb[h] for h in range(HG_HEADS)], False)
    for h, hs in enumerate(heads):
        of_ref[0, :, hs] = o_f[h]
        ob_ref[0, :, hs] = o_b[h]
        stf[h] = st_f[h]
        stb[h] = st_b[h]

    @pl.when(c == nc - 1)
    def _():
        for h in range(HG_HEADS):
            sf_ref[0, h] = stf[h].T
            sb_ref[0, h] = stb[h].T


def hgrn_bidir(z, lb_logits, s0f, s0b, batch, seq):
    nc = seq // HG_CHUNK
    z3 = z.reshape(batch, seq, z.shape[1])
    mf, mb, lvf, lvb = _hgrn_constants()
    has_init = s0f is not None
    width = HG_HEADS * LANES
    blk = (1, HG_CHUNK, width)
    fwd = lambda off: pl.BlockSpec(blk, lambda b, c: (b, c, off))
    bwd = lambda off: pl.BlockSpec(blk, lambda b, c: (b, nc - 1 - c, off))
    full = lambda a: pl.BlockSpec(a.shape, lambda b, c: (0,) * a.ndim)
    st_spec = pl.BlockSpec((1, HG_HEADS, HG_DK, HG_DV), lambda b, c: (b, 0, 0, 0))
    consts = [jnp.asarray(mf, BF16), jnp.asarray(mb, BF16), jnp.asarray(lvf), jnp.asarray(lvb)]
    in_specs = [fwd(0), bwd(0), fwd(1), bwd(2), fwd(3), bwd(3),
                pl.BlockSpec(lb_logits.shape, lambda b, c: (0, 0, 0))]
    in_specs += [full(a) for a in consts]
    args = [z3] * 6 + [lb_logits] + consts
    if has_init:
        in_specs += [st_spec, st_spec]
        args += [s0f, s0b]
    o_shape = jax.ShapeDtypeStruct((batch, seq, width), F32)
    s_shape = jax.ShapeDtypeStruct((batch, HG_HEADS, HG_DK, HG_DV), F32)
    o_f, o_b, s_f, s_b = pl.pallas_call(
        functools.partial(_hgrn_kernel, has_init=has_init),
        grid=(batch, nc),
        in_specs=in_specs,
        out_specs=[pl.BlockSpec(blk, lambda b, c: (b, c, 0)),
                   pl.BlockSpec(blk, lambda b, c: (b, nc - 1 - c, 0)),
                   st_spec, st_spec],
        out_shape=[o_shape, o_shape, s_shape, s_shape],
        scratch_shapes=[pltpu.VMEM((HG_HEADS, HG_DV, HG_DK), F32), pltpu.VMEM((HG_HEADS, HG_DV, HG_DK), F32)],
        compiler_params=_cparams(("parallel", "arbitrary")),
        name="hgrn_bidir",
    )(*args)
    t = batch * seq
    return o_f.reshape(t, -1), o_b.reshape(t, -1), s_f, s_b


def _axial_tables(n_tokens, n_rot):
    t = jnp.arange(n_tokens)
    row = (t // GRID_W).astype(F32)
    col = (t % GRID_W).astype(F32)
    n_freq = n_rot // 4
    inv = jnp.power(ROPE_THETA, -jnp.arange(n_freq, dtype=F32) / n_freq)
    ang = jnp.concatenate([row[:, None] * inv, col[:, None] * inv], axis=-1)
    return jnp.cos(ang), jnp.sin(ang)


def _rope_lane_tables(n_tokens, n_rot, head_width, first_rot_lane):
    cos, sin = _axial_tables(n_tokens, n_rot)
    half = n_rot // 2
    c_head = jnp.ones((n_tokens, head_width), F32)
    sa_head = jnp.zeros((n_tokens, head_width), F32)
    sb_head = jnp.zeros((n_tokens, head_width), F32)
    a0, a1, a2 = first_rot_lane, first_rot_lane + half, first_rot_lane + n_rot
    c_head = c_head.at[:, a0:a1].set(cos).at[:, a1:a2].set(cos)
    sa_head = sa_head.at[:, a0:a1].set(-sin)
    sb_head = sb_head.at[:, a1:a2].set(sin)
    reps = LANES // head_width
    tile = lambda a: jnp.tile(a, (1, reps))
    return tile(c_head), tile(sa_head), tile(sb_head)


def _rope(x, c, sa, sb, half):
    return x * c + pltpu.roll(x, LANES - half, 1) * sa + pltpu.roll(x, half, 1) * sb


def _mla_q_kernel(*refs, rope):
    if rope:
        qa_ref, qan_ref, qb_ref, qn_ref, c_ref, sa_ref, sb_ref, o_ref = refs
    else:
        qa_ref, qan_ref, qb_ref, qn_ref, o_ref = refs
    qa = qa_ref[...]
    qn = qa * lax.rsqrt(jnp.mean(qa * qa, axis=-1, keepdims=True) + EPS) * qan_ref[...]
    qfull = _dot(qn.astype(BF16), qb_ref[...])
    outs = []
    for h in range(MLA_HEADS):
        qh = qfull[:, h * LANES:(h + 1) * LANES]
        ms = jnp.sum(qh * qh, axis=-1, keepdims=True) * (1.0 / MLA_QK)
        qh = qh * lax.rsqrt(ms + EPS) * qn_ref[...]
        if rope:
            qh = _rope(qh, c_ref[...], sa_ref[...], sb_ref[...], MLA_ROPE // 2)
        outs.append((qh * MLA_SCALE).astype(BF16))
    o_ref[...] = jnp.concatenate(outs, axis=1)


def mla_queries(z, qa_norm, q_b_pad, q_norm_pad, tables, tokens_per_batch, tm=256):
    t = z.shape[0]
    rope = tables is not None
    row = lambda i: (0, 0)
    in_specs = [pl.BlockSpec((tm, MLA_Q_LORA), lambda i: (i, 2560 // MLA_Q_LORA)),
                pl.BlockSpec((1, MLA_Q_LORA), row),
                pl.BlockSpec(q_b_pad.shape, row),
                pl.BlockSpec((1, LANES), row)]
    args = [z, qa_norm.reshape(1, -1), q_b_pad, q_norm_pad]
    if rope:
        tpb = tokens_per_batch // tm
        in_specs += [pl.BlockSpec((tm, LANES), lambda i: (i % tpb, 0))] * 3
        args += list(tables)
    return pl.pallas_call(
        functools.partial(_mla_q_kernel, rope=rope),
        grid=(t // tm,),
        in_specs=in_specs,
        out_specs=pl.BlockSpec((tm, MLA_HEADS * LANES), lambda i: (i, 0)),
        out_shape=jax.ShapeDtypeStruct((t, MLA_HEADS * LANES), BF16),
        compiler_params=_cparams(("parallel",)),
        name="mla_queries",
    )(*args)


def _mla_kv_kernel(*refs, norm_input, rope):
    if rope:
        kva_ref, kpe_ref, kvan_ref, wk_ref, wv_ref, kn_ref, c_ref, sa_ref, sb_ref, ckv_ref, k_ref, v_ref = refs
    else:
        kva_ref, kpe_ref, kvan_ref, wk_ref, wv_ref, kn_ref, ckv_ref, k_ref, v_ref = refs
    ckv = kva_ref[...]
    if norm_input:
        ckv = ckv * lax.rsqrt(jnp.mean(ckv * ckv, axis=-1, keepdims=True) + EPS) * kvan_ref[...]
    ckv_ref[...] = ckv
    cb = ckv.astype(BF16)
    knope = _dot(cb, wk_ref[...])
    v_ref[...] = _dot(cb, wv_ref[...]).astype(BF16)
    kpe = pltpu.roll(kpe_ref[...], MLA_NOPE, 1)
    outs = []
    for h in range(MLA_HEADS):
        kh = knope[:, h * LANES:(h + 1) * LANES] + kpe
        ms = jnp.sum(kh * kh, axis=-1, keepdims=True) * (1.0 / MLA_QK)
        kh = kh * lax.rsqrt(ms + EPS) * kn_ref[...]
        if rope:
            kh = _rope(kh, c_ref[...], sa_ref[...], sb_ref[...], MLA_ROPE // 2)
        outs.append(kh.astype(BF16))
    k_ref[...] = jnp.concatenate(outs, axis=1)


def mla_keys_values(kva_src, kva_col, kpe_src, kpe_col, kva_norm, wk_pad, wv, k_norm_pad, tables,
                    tokens_per_batch, norm_input, tm=256):
    t = kva_src.shape[0]
    rope = tables is not None
    row = lambda i: (0, 0)
    in_specs = [pl.BlockSpec((tm, LANES), lambda i: (i, kva_col)),
                pl.BlockSpec((tm, LANES), lambda i: (i, kpe_col)),
                pl.BlockSpec((1, LANES), row),
                pl.BlockSpec(wk_pad.shape, row),
                pl.BlockSpec(wv.shape, row),
                pl.BlockSpec((1, LANES), row)]
    args = [kva_src, kpe_src, kva_norm.reshape(1, -1), wk_pad, wv, k_norm_pad]
    if rope:
        tpb = tokens_per_batch // tm
        in_specs += [pl.BlockSpec((tm, LANES), lambda i: (i % tpb, 0))] * 3
        args += list(tables)
    return pl.pallas_call(
        functools.partial(_mla_kv_kernel, norm_input=norm_input, rope=rope),
        grid=(t // tm,),
        in_specs=in_specs,
        out_specs=[pl.BlockSpec((tm, LANES), lambda i: (i, 0)),
                   pl.BlockSpec((tm, MLA_HEADS * LANES), lambda i: (i, 0)),
                   pl.BlockSpec((tm, MLA_HEADS * MLA_V), lambda i: (i, 0))],
        out_shape=[jax.ShapeDtypeStruct((t, LANES), F32),
                   jax.ShapeDtypeStruct((t, MLA_HEADS * LANES), BF16),
                   jax.ShapeDtypeStruct((t, MLA_HEADS * MLA_V), BF16)],
        compiler_params=_cparams(("parallel",)),
        name="mla_keys_values",
    )(*args)


def _mla_attn_kernel(q_ref, k_ref, v_ref, o_ref):
    outs = []
    for j in range(2):
        q = q_ref[0][:, j * LANES:(j + 1) * LANES]
        k = k_ref[0][:, j * LANES:(j + 1) * LANES]
        v = v_ref[0][:, j * MLA_V:(j + 1) * MLA_V]
        s = _dot_nt(q, k)
        p = jnp.exp(s - jnp.max(s, axis=-1, keepdims=True))
        l = jnp.sum(p, axis=-1, keepdims=True)
        outs.append(_dot(p.astype(BF16), v) / l)
    o_ref[0] = jnp.concatenate(outs, axis=1)


def mla_attention(q, k, v, batch, n_q, n_k, tq=256):
    q3 = q.reshape(batch, n_q, -1)
    k3 = k.reshape(batch, n_k, -1)
    v3 = v.reshape(batch, n_k, -1)
    out = pl.pallas_call(
        _mla_attn_kernel,
        grid=(batch, MLA_HEADS // 2, n_q // tq),
        in_specs=[pl.BlockSpec((1, tq, 2 * LANES), lambda b, j, i: (b, i, j)),
                  pl.BlockSpec((1, n_k, 2 * LANES), lambda b, j, i: (b, 0, j)),
                  pl.BlockSpec((1, n_k, 2 * MLA_V), lambda b, j, i: (b, 0, j))],
        out_specs=pl.BlockSpec((1, tq, 2 * MLA_V), lambda b, j, i: (b, i, j)),
        out_shape=jax.ShapeDtypeStruct((batch, n_q, MLA_HEADS * MLA_V), F32),
        compiler_params=_cparams(("parallel", "parallel", "arbitrary")),
        name="mla_attention",
    )(q3, k3, v3)
    return out.reshape(batch * n_q, -1)


def _out0_kernel(of_ref, ob_ref, ag_ref, hgn_ref, om_ref, w_ref, x_ref, g_ref, o_ref):
    o = of_ref[...] + ob_ref[...]
    ag = ag_ref[...]
    parts = []
    for h in range(HG_HEADS):
        oh = o[:, h * HG_DV:(h + 1) * HG_DV]
        oh = oh * lax.rsqrt(jnp.mean(oh * oh, axis=-1, keepdims=True) + EPS) * hgn_ref[...]
        gh = ag[:, h * HG_DV:(h + 1) * HG_DV]
        parts.append((oh * (gh * _sigmoid(gh))).astype(BF16))
    oa = jnp.concatenate(parts, axis=1)
    n_a = HG_HEADS * HG_DV
    mix = _dot(oa, w_ref[0:n_a, :]) + _dot(om_ref[...].astype(BF16), w_ref[n_a:, :])
    o_ref[...] = x_ref[...] + g_ref[0] * mix


def out_proj_layer0(o_f, o_b, z, hg_norm, o_mla, w_out_bf16, x, gate, tokens_per_group, tm=256):
    t, d = x.shape
    n_a = HG_HEADS * HG_DV
    tiles_per_group = tokens_per_group // tm
    tile = lambda w: pl.BlockSpec((tm, w), lambda i: (i, 0))
    return pl.pallas_call(
        _out0_kernel,
        grid=(t // tm,),
        in_specs=[tile(n_a), tile(n_a),
                  pl.BlockSpec((tm, n_a), lambda i: (i, 2048 // n_a)),
                  pl.BlockSpec((1, HG_DV), lambda i: (0, 0)),
                  tile(o_mla.shape[1]),
                  pl.BlockSpec(w_out_bf16.shape, lambda i: (0, 0)),
                  tile(d),
                  pl.BlockSpec((1, 1, d), lambda i: (i // tiles_per_group, 0, 0))],
        out_specs=tile(d),
        out_shape=jax.ShapeDtypeStruct((t, d), F32),
        compiler_params=_cparams(("parallel",)),
        name="out_proj_layer0",
    )(o_f, o_b, z, hg_norm.reshape(1, -1), o_mla, w_out_bf16, x, gate)


def _out1_kernel(a_ref, w_ref, x_ref, g_ref, o_ref):
    o_ref[...] = x_ref[...] + g_ref[0] * _dot(a_ref[...].astype(BF16), w_ref[...])


def out_proj_layer1(a, w_out_bf16, x, gate, tokens_per_group, tm=256):
    t, d = x.shape
    tiles_per_group = tokens_per_group // tm
    tile = lambda w: pl.BlockSpec((tm, w), lambda i: (i, 0))
    return pl.pallas_call(
        _out1_kernel,
        grid=(t // tm,),
        in_specs=[tile(a.shape[1]),
                  pl.BlockSpec(w_out_bf16.shape, lambda i: (0, 0)),
                  tile(d),
                  pl.BlockSpec((1, 1, d), lambda i: (i // tiles_per_group, 0, 0))],
        out_specs=tile(d),
        out_shape=jax.ShapeDtypeStruct((t, d), F32),
        compiler_params=_cparams(("parallel",)),
        name="out_proj_layer1",
    )(a, w_out_bf16, x, gate)


def _head_rms(x, gain2):
    sq = x * x
    lane = lax.broadcasted_iota(jnp.int32, x.shape, 1)
    first = lane < SWA_HD
    lo = jnp.sum(jnp.where(first, sq, 0.0), axis=-1, keepdims=True)
    hi = jnp.sum(jnp.where(first, 0.0, sq), axis=-1, keepdims=True)
    ms = jnp.where(first, lo, hi) * (1.0 / SWA_HD)
    return x * lax.rsqrt(ms + EPS) * gain2


def _swa_prep_kernel(*refs, rope):
    if rope:
        zq_ref, zk_ref, zv_ref, qn_ref, kn_ref, c_ref, sa_ref, sb_ref, q_ref, kc_ref, k_ref, v_ref = refs
    else:
        zq_ref, zk_ref, zv_ref, qn_ref, kn_ref, q_ref, kc_ref, k_ref, v_ref = refs
    half = SWA_HD // 2

    def rot(x):
        return _rope(x, c_ref[...], sa_ref[...], sb_ref[...], half) if rope else x

    zq = zq_ref[...]
    qs = []
    for p in range(zq.shape[1] // LANES):
        x = _head_rms(zq[:, p * LANES:(p + 1) * LANES], qn_ref[...])
        qs.append((rot(x) * SWA_SCALE).astype(BF16))
    q_ref[...] = jnp.concatenate(qs, axis=1)
    zk = zk_ref[...]
    kn, kr = [], []
    for p in range(zk.shape[1] // LANES):
        x = _head_rms(zk[:, p * LANES:(p + 1) * LANES], kn_ref[...])
        kn.append(x)
        kr.append(rot(x).astype(BF16))
    kc_ref[...] = jnp.concatenate(kn, axis=1)
    k_ref[...] = jnp.concatenate(kr, axis=1)
    v_ref[...] = zv_ref[...].astype(BF16)


def swa_prep(z, q_norm2, k_norm2, tables, tokens_per_batch, tm=256):
    t = z.shape[0]
    nq = SWA_HEADS * SWA_HD
    nkv = SWA_KV_HEADS * SWA_HD
    rope = tables is not None
    row = lambda i: (0, 0)
    in_specs = [pl.BlockSpec((tm, nq), lambda i: (i, 0)),
                pl.BlockSpec((tm, nkv), lambda i: (i, nq // nkv)),
                pl.BlockSpec((tm, nkv), lambda i: (i, nq // nkv + 1)),
                pl.BlockSpec((1, LANES), row),
                pl.BlockSpec((1, LANES), row)]
    args = [z, z, z, q_norm2, k_norm2]
    if rope:
        tpb = tokens_per_batch // tm
        in_specs += [pl.BlockSpec((tm, LANES), lambda i: (i % tpb, 0))] * 3
        args += list(tables)
    tile = lambda w: pl.BlockSpec((tm, w), lambda i: (i, 0))
    return pl.pallas_call(
        functools.partial(_swa_prep_kernel, rope=rope),
        grid=(t // tm,),
        in_specs=in_specs,
        out_specs=[tile(nq), tile(nkv), tile(nkv), tile(nkv)],
        out_shape=[jax.ShapeDtypeStruct((t, nq), BF16),
                   jax.ShapeDtypeStruct((t, nkv), F32),
                   jax.ShapeDtypeStruct((t, nkv), BF16),
                   jax.ShapeDtypeStruct((t, nkv), BF16)],
        compiler_params=_cparams(("parallel",)),
        name="swa_prep",
    )(*args)


def _sink_softmax_pv(s, sk, vn):
    m = jnp.maximum(jnp.max(s, axis=-1, keepdims=True), sk)
    p = jnp.exp(s - m)
    l = jnp.sum(p, axis=-1, keepdims=True) + jnp.exp(sk - m)
    return _dot(p.astype(BF16), vn) / l


def _sink_attention(q8, k_all, v_all, bias, sink_ref, pair, stack):
    outs = []
    tq = q8.shape[0]
    for n in range(2):
        kn = k_all[:, n * SWA_HD:(n + 1) * SWA_HD]
        vn = v_all[:, n * SWA_HD:(n + 1) * SWA_HD]
        heads = [n * SWA_GROUP + g for g in range(SWA_GROUP)]
        qs = [q8[:, hq * SWA_HD:(hq + 1) * SWA_HD] for hq in heads]
        sinks = [sink_ref[pair * 2 * SWA_GROUP + hq] for hq in heads]
        if stack:
            sk = jnp.concatenate([jnp.full((tq, 1), s, F32) for s in sinks], axis=0)
            s = _dot_nt(jnp.concatenate(qs, axis=0), kn)
            if bias is not None:
                s = s + bias
            o4 = _sink_softmax_pv(s, sk, vn)
            outs += [o4[g * tq:(g + 1) * tq] for g in range(SWA_GROUP)]
        else:
            for q, sk in zip(qs, sinks):
                s = _dot_nt(q, kn)
                if bias is not None:
                    s = s + bias
                outs.append(_sink_softmax_pv(s, sk, vn))
    return jnp.concatenate(outs, axis=1)


def _swa_dense_kernel(sink_ref, q_ref, k_ref, v_ref, o_ref):
    o_ref[0] = _sink_attention(q_ref[0], k_ref[0], v_ref[0], None, sink_ref, pl.program_id(1), stack=False)


def swa_dense(q, k, v, sink, batch, seq):
    nq = SWA_HEADS * SWA_HD
    q3 = q.reshape(batch, seq, nq)
    k3 = k.reshape(batch, seq, -1)
    v3 = v.reshape(batch, seq, -1)
    out = pl.pallas_call(
        _swa_dense_kernel,
        grid=(batch, 2),
        in_specs=[pl.BlockSpec(memory_space=pltpu.SMEM),
                  pl.BlockSpec((1, seq, nq // 2), lambda b, j: (b, 0, j)),
                  pl.BlockSpec((1, seq, LANES), lambda b, j: (b, 0, j)),
                  pl.BlockSpec((1, seq, LANES), lambda b, j: (b, 0, j))],
        out_specs=pl.BlockSpec((1, seq, nq // 2), lambda b, j: (b, 0, j)),
        out_shape=jax.ShapeDtypeStruct((batch, seq, nq), F32),
        compiler_params=_cparams(("parallel", "parallel")),
        name="swa_dense",
    )(sink, q3, k3, v3)
    return out.reshape(batch * seq, nq)


def _swa_window_kernel(sink_ref, q_ref, kc_ref, vc_ref, kp_ref, k0_ref, kn_ref, vp_ref, v0_ref, vn_ref, o_ref):
    i = pl.program_id(2)
    nb = pl.num_programs(2)
    w = SWA_WINDOW
    n_ctx = kc_ref.shape[1]
    k_all = jnp.concatenate([kc_ref[0], kp_ref[0], k0_ref[0], kn_ref[0]], axis=0)
    v_all = jnp.concatenate([vc_ref[0], vp_ref[0], v0_ref[0], vn_ref[0]], axis=0)
    shape = (SWA_GROUP * w, n_ctx + 3 * w)
    r = lax.broadcasted_iota(jnp.int32, shape, 0) % w
    col = lax.broadcasted_iota(jnp.int32, shape, 1) - n_ctx
    prev_bias = jnp.where(i > 0, 0.0, NEG_BIG)
    next_bias = jnp.where(i < nb - 1, 0.0, NEG_BIG)
    bias = jnp.where(
        col < w,
        jnp.where(col < 0, 0.0, jnp.where(col >= r, prev_bias, NEG_BIG)),
        jnp.where(col < 2 * w, 0.0, jnp.where(col - 2 * w <= r, next_bias, NEG_BIG)))
    o_ref[0] = _sink_attention(q_ref[0], k_all, v_all, bias, sink_ref, pl.program_id(1), stack=True)


def swa_window(q, k, v, k_ctx, v_ctx, sink, batch, seq, n_ctx):
    nq = SWA_HEADS * SWA_HD
    w = SWA_WINDOW
    nb = seq // w
    q3 = q.reshape(batch, seq, nq)
    k3 = k.reshape(batch, seq, -1)
    v3 = v.reshape(batch, seq, -1)
    ctx = pl.BlockSpec((1, n_ctx, LANES), lambda b, j, i: (b, 0, j))
    prv = pl.BlockSpec((1, w, LANES), lambda b, j, i: (b, jnp.maximum(i - 1, 0), j))
    cur = pl.BlockSpec((1, w, LANES), lambda b, j, i: (b, i, j))
    nxt = pl.BlockSpec((1, w, LANES), lambda b, j, i: (b, jnp.minimum(i + 1, nb - 1), j))
    out = pl.pallas_call(
        _swa_window_kernel,
        grid=(batch, 2, nb),
        in_specs=[pl.BlockSpec(memory_space=pltpu.SMEM),
                  pl.BlockSpec((1, w, nq // 2), lambda b, j, i: (b, i, j)),
                  ctx, ctx, prv, cur, nxt, prv, cur, nxt],
        out_specs=pl.BlockSpec((1, w, nq // 2), lambda b, j, i: (b, i, j)),
        out_shape=jax.ShapeDtypeStruct((batch, seq, nq), F32),
        compiler_params=_cparams(("parallel", "parallel", "arbitrary")),
        name="swa_window",
    )(sink, q3, k_ctx, v_ctx, k3, k3, k3, v3, v3, v3)
    return out.reshape(batch * seq, nq)


def _router_kernel(x_ref, g_ref, sh_ref, sc_ref, rw_ref, rb_ref, tri_ref, h_ref, idx_ref, gate_ref, rank_ref,
                   cnt_ref, cnt):
    @pl.when(pl.program_id(0) == 0)
    def _():
        cnt[...] = jnp.zeros_like(cnt)

    h = _modnorm(x_ref[...], g_ref[...], sh_ref[0], sc_ref[0])
    h_ref[...] = h
    logits = _dot_hp(h, rw_ref[...]) + rb_ref[...]
    lane = lax.broadcasted_iota(jnp.int32, logits.shape, 1)
    work = logits
    vals, idxs = [], []
    for _ in range(TOP_K):
        m = jnp.max(work, axis=-1, keepdims=True)
        ik = jnp.min(jnp.where(work == m, lane, LANES), axis=-1, keepdims=True)
        vals.append(m)
        idxs.append(ik)
        work = jnp.where(lane == ik, 2.0 * NEG_BIG, work)
    es = [jnp.exp(v - vals[0]) for v in vals]
    denom = es[0] + es[1] + es[2] + es[3]
    idx_out = jnp.zeros(logits.shape, jnp.int32)
    gate_out = jnp.zeros(logits.shape, F32)
    for k in range(TOP_K):
        idx_out = jnp.where(lane == k, idxs[k], idx_out)
        gate_out = jnp.where(lane == k, es[k] / denom, gate_out)
    idx_ref[...] = idx_out
    gate_ref[...] = gate_out
    chosen = jnp.zeros(logits.shape, F32)
    for k in range(TOP_K):
        chosen = jnp.where(lane == idxs[k], 1.0, chosen)
    before = _dot(tri_ref[...], chosen.astype(BF16)) + cnt[...]
    rank_out = jnp.zeros(logits.shape, jnp.int32)
    for k in range(TOP_K):
        rk = jnp.sum(jnp.where(lane == idxs[k], before, 0.0), axis=-1, keepdims=True)
        rank_out = jnp.where(lane == k, rk.astype(jnp.int32), rank_out)
    rank_ref[...] = rank_out
    cnt[...] = cnt[...] + jnp.sum(chosen, axis=0, keepdims=True)
    cnt_ref[...] = cnt[...]


def moe_router(x, gain, shift, scale, rw_pad, rb_pad, tokens_per_group, tm=256):
    t, d = x.shape
    tiles_per_group = tokens_per_group // tm
    grp = lambda i: (i // tiles_per_group, 0, 0)
    tile = lambda w: pl.BlockSpec((tm, w), lambda i: (i, 0))
    tri = jnp.asarray(np.tril(np.ones((tm, tm), np.float32), -1), BF16)
    return pl.pallas_call(
        _router_kernel,
        grid=(t // tm,),
        in_specs=[tile(d),
                  pl.BlockSpec((1, d), lambda i: (0, 0)),
                  pl.BlockSpec((1, 1, d), grp),
                  pl.BlockSpec((1, 1, d), grp),
                  pl.BlockSpec((d, LANES), lambda i: (0, 0)),
                  pl.BlockSpec((1, LANES), lambda i: (0, 0)),
                  pl.BlockSpec((tm, tm), lambda i: (0, 0))],
        out_specs=[tile(d), tile(LANES), tile(LANES), tile(LANES),
                   pl.BlockSpec((1, LANES), lambda i: (0, 0))],
        out_shape=[jax.ShapeDtypeStruct((t, d), F32),
                   jax.ShapeDtypeStruct((t, LANES), jnp.int32),
                   jax.ShapeDtypeStruct((t, LANES), F32),
                   jax.ShapeDtypeStruct((t, LANES), jnp.int32),
                   jax.ShapeDtypeStruct((1, LANES), F32)],
        scratch_shapes=[pltpu.VMEM((1, LANES), F32)],
        compiler_params=_cparams(("arbitrary",)),
        name="moe_router",
    )(x, gain.reshape(1, d), shift, scale, rw_pad, rb_pad, tri)


def _row_gather_start(src_hbm, dst_rows, sem, row_of, n_rows):
    def body(r, carry):
        pltpu.make_async_copy(src_hbm.at[pl.ds(row_of(r), 1), :], dst_rows.at[pl.ds(r, 1), :], sem).start()
        return carry
    lax.fori_loop(0, n_rows, body, 0, unroll=8)


FFN_COLS = 256
FFN_PIECES = 2 * D_FF // FFN_COLS + D_MODEL // FFN_COLS
SEM_GATHER_A, SEM_GATHER_B, SEM_WRITE_A, SEM_WRITE_B = range(4)


def _ffn_block(x_ref, y_ref, wgu, bgu_ref, wd, bd_ref, issue):
    xb = x_ref[...].astype(BF16)
    acts = []
    for c in range(D_FF // FFN_COLS):
        lo, hi = c * FFN_COLS, (c + 1) * FFN_COLS
        gate = _dot(xb, wgu[:, lo:hi]) + bgu_ref[0][:, lo:hi]
        issue(2 * c)
        up = _dot(xb, wgu[:, D_FF + lo:D_FF + hi]) + bgu_ref[0][:, D_FF + lo:D_FF + hi]
        issue(2 * c + 1)
        gate = jnp.minimum(gate, SWIGLU_LIMIT)
        up = jnp.clip(up, -SWIGLU_LIMIT, SWIGLU_LIMIT)
        acts.append((gate * _sigmoid(SWIGLU_ALPHA * gate) * (up + 1.0)).astype(BF16))
    act = jnp.concatenate(acts, axis=1)
    for n in range(D_MODEL // FFN_COLS):
        lo, hi = n * FFN_COLS, (n + 1) * FFN_COLS
        y_ref[:, lo:hi] = _dot(act, wd[:, lo:hi]) + bd_ref[0][:, lo:hi]
        issue(2 * D_FF // FFN_COLS + n)


def _expert_kernel(nb_ref, bs_ref, rt_ref, h_hbm, wgu_ref, bgu_ref, wd_ref, bd_ref, y_hbm,
                   wgu_bf, wd_bf, xa, xb, ya, yb, sem):
    e = pl.program_id(0)
    nb = nb_ref[e]
    b0 = bs_ref[e]
    shares = np.array_split(np.arange(MOE_ROWS), FFN_PIECES)

    def gather_issue(dst, s, blk):
        base = blk * MOE_ROWS

        def issue(k):
            for r in shares[k]:
                r = int(r)
                pltpu.make_async_copy(h_hbm.at[pl.ds(rt_ref[base + r], 1), :], dst.at[pl.ds(r, 1), :],
                                      sem.at[s]).start()
        return issue

    def gather_wait(dst, s):
        pltpu.make_async_copy(dst, dst, sem.at[s]).wait()

    def write(src, s, blk):
        return pltpu.make_async_copy(src, y_hbm.at[pl.ds(blk * MOE_ROWS, MOE_ROWS), :], sem.at[s])

    def ffn(x_ref, y_ref, issue):
        _ffn_block(x_ref, y_ref, wgu_bf, bgu_ref, wd_bf, bd_ref, issue)

    @pl.when(nb > 0)
    def _():
        first = gather_issue(xa, SEM_GATHER_A, b0)
        n_slabs = 4
        per = FFN_PIECES // n_slabs
        for s in range(n_slabs):
            rows = slice(s * D_MODEL // n_slabs, (s + 1) * D_MODEL // n_slabs)
            wgu_bf[rows, :] = wgu_ref[0, rows, :].astype(BF16)
            wd_bf[rows, :] = wd_ref[0, rows, :].astype(BF16)
            for k in range(s * per, (s + 1) * per):
                first(k)
        n_pairs = nb // 2
        last = b0 + nb - 1

        def pair(jj, carry):
            j = b0 + 2 * jj
            gather_wait(xa, SEM_GATHER_A)

            @pl.when(jj > 0)
            def _():
                write(ya, SEM_WRITE_A, j).wait()

            ffn(xa, ya, gather_issue(xb, SEM_GATHER_B, j + 1))
            write(ya, SEM_WRITE_A, j).start()
            gather_wait(xb, SEM_GATHER_B)

            @pl.when(jj > 0)
            def _():
                write(yb, SEM_WRITE_B, j).wait()

            ffn(xb, yb, gather_issue(xa, SEM_GATHER_A, jnp.minimum(j + 2, last)))
            write(yb, SEM_WRITE_B, j + 1).start()
            return carry

        lax.fori_loop(0, n_pairs, pair, 0)
        gather_wait(xa, SEM_GATHER_A)

        @pl.when(nb % 2 == 1)
        def _():
            @pl.when(n_pairs > 0)
            def _():
                write(ya, SEM_WRITE_A, last).wait()

            ffn(xa, ya, lambda k: None)
            write(ya, SEM_WRITE_A, last).start()

        write(ya, SEM_WRITE_A, last).wait()

        @pl.when(n_pairs > 0)
        def _():
            write(yb, SEM_WRITE_B, last).wait()

    @pl.when(e == N_EXPERTS - 1)
    def _():
        n_blocks = y_hbm.shape[0] // MOE_ROWS
        ya[...] = jnp.zeros_like(ya)

        def fill(blk, carry):
            write(ya, SEM_WRITE_A, blk).start()
            return carry

        def drain(blk, carry):
            write(ya, SEM_WRITE_A, blk).wait()
            return carry

        lax.fori_loop(b0 + nb, n_blocks, fill, 0)
        lax.fori_loop(b0 + nb, n_blocks, drain, 0)


def moe_experts(h, n_blk, blk_start, row_tok, w_gu, b_gu, w_down, b_down):
    n_rows = row_tok.shape[0]
    d = h.shape[1]
    grid_spec = pltpu.PrefetchScalarGridSpec(
        num_scalar_prefetch=3,
        grid=(N_EXPERTS,),
        in_specs=[pl.BlockSpec(memory_space=pl.ANY),
                  pl.BlockSpec((1, d, 2 * D_FF), lambda e, nb, bs, rt: (e, 0, 0)),
                  pl.BlockSpec((1, 1, 2 * D_FF), lambda e, nb, bs, rt: (e, 0, 0)),
                  pl.BlockSpec((1, D_FF, d), lambda e, nb, bs, rt: (e, 0, 0)),
                  pl.BlockSpec((1, 1, d), lambda e, nb, bs, rt: (e, 0, 0))],
        out_specs=pl.BlockSpec(memory_space=pl.ANY),
        scratch_shapes=[pltpu.VMEM((d, 2 * D_FF), BF16), pltpu.VMEM((D_FF, d), BF16),
                        pltpu.VMEM((MOE_ROWS, d), F32), pltpu.VMEM((MOE_ROWS, d), F32),
                        pltpu.VMEM((MOE_ROWS, d), F32), pltpu.VMEM((MOE_ROWS, d), F32),
                        pltpu.SemaphoreType.DMA((4,))],
    )
    return pl.pallas_call(
        _expert_kernel,
        grid_spec=grid_spec,
        out_shape=jax.ShapeDtypeStruct((n_rows, d), F32),
        compiler_params=pltpu.CompilerParams(dimension_semantics=("arbitrary",),
                                             vmem_limit_bytes=EXPERT_VMEM_LIMIT),
        name="moe_experts",
    )(n_blk, blk_start, row_tok, h, w_gu, b_gu.reshape(N_EXPERTS, 1, -1), w_down, b_down.reshape(N_EXPERTS, 1, -1))


COMBINE_TM = 128


def _combine_kernel(dest_ref, y_hbm, gate_ref, x_ref, g_ref, o_ref, buf, sem):
    i = pl.program_id(0)
    n = pl.num_programs(0)
    slot = i % 2

    def start(tile, s):
        for k in range(TOP_K):
            _row_gather_start(y_hbm, buf.at[s, k], sem.at[s],
                              lambda r: dest_ref[(tile * COMBINE_TM + r) * TOP_K + k], COMBINE_TM)

    @pl.when(i == 0)
    def _():
        start(0, 0)

    @pl.when(i + 1 < n)
    def _():
        start(i + 1, 1 - slot)

    pltpu.make_async_copy(buf.at[slot], buf.at[slot], sem.at[slot]).wait()
    gates = gate_ref[...]
    acc = gates[:, 0:1] * buf[slot, 0]
    for k in range(1, TOP_K):
        acc = acc + gates[:, k:k + 1] * buf[slot, k]
    o_ref[...] = x_ref[...] + g_ref[0] * acc


def moe_combine(yb, dest, gates, x, gate_mod, tokens_per_group):
    t, d = x.shape
    tm = COMBINE_TM
    tiles_per_group = tokens_per_group // tm
    grid_spec = pltpu.PrefetchScalarGridSpec(
        num_scalar_prefetch=1,
        grid=(t // tm,),
        in_specs=[pl.BlockSpec(memory_space=pl.ANY),
                  pl.BlockSpec((tm, LANES), lambda i, de: (i, 0)),
                  pl.BlockSpec((tm, d), lambda i, de: (i, 0)),
                  pl.BlockSpec((1, 1, d), lambda i, de: (i // tiles_per_group, 0, 0))],
        out_specs=pl.BlockSpec((tm, d), lambda i, de: (i, 0)),
        scratch_shapes=[pltpu.VMEM((2, TOP_K, tm, d), F32), pltpu.SemaphoreType.DMA((2,))],
    )
    return pl.pallas_call(
        _combine_kernel,
        grid_spec=grid_spec,
        out_shape=jax.ShapeDtypeStruct((t, d), F32),
        compiler_params=_cparams(("arbitrary",)),
        name="moe_combine",
    )(dest, yb, gates, x, gate_mod)


def _assignment_tables(top_idx, rank, counts):
    t = top_idx.shape[0]
    n_assign = t * TOP_K
    padded = (counts + MOE_ROWS - 1) // MOE_ROWS * MOE_ROWS
    pad_start = jnp.cumsum(padded) - padded
    experts = jnp.arange(N_EXPERTS, dtype=jnp.int32)
    start_of = jnp.sum(jnp.where(top_idx[:, :, None] == experts, pad_start, 0), axis=-1)
    dest = (start_of + rank).reshape(-1).astype(jnp.int32)
    n_rows = (n_assign // MOE_ROWS + N_EXPERTS) * MOE_ROWS
    row_tok = (jnp.arange(n_rows, dtype=jnp.int32) % t).at[dest].set(
        jnp.arange(n_assign, dtype=jnp.int32) // TOP_K)
    return dest, row_tok, (padded // MOE_ROWS).astype(jnp.int32), (pad_start // MOE_ROWS).astype(jnp.int32)


def moe_layer(x, p, shift, scale, gate_mod, tokens_per_group):
    h, idx, gates, rank, cnt = moe_router(x, p['norm2'], shift, scale, p['rw_pad'], p['rb_pad'], tokens_per_group)
    counts = cnt[0, :N_EXPERTS].astype(jnp.int32)
    dest, row_tok, n_blk, blk_start = _assignment_tables(idx[:, :TOP_K], rank[:, :TOP_K], counts)
    yb = moe_experts(h, n_blk, blk_start, row_tok, p['w_gu'], p['b_gu'], p['w_down'], p['b_down'])
    return moe_combine(yb, dest, gates, x, gate_mod, tokens_per_group)


def _pad_lanes(a, width):
    return jnp.pad(a, [(0, 0)] * (a.ndim - 1) + [(0, width - a.shape[-1])])


def _prep_common(p):
    p['rw_pad'] = _pad_lanes(p['router_w'], LANES)
    p['rb_pad'] = jnp.concatenate(
        [p['router_b'].astype(F32), jnp.full((LANES - N_EXPERTS,), NEG_BIG, F32)]).reshape(1, LANES)
    p['w_out'] = p['w_out'].astype(BF16)
    return p


def _prep_layer0(p):
    p = _prep_common(dict(p))
    p['w_in'] = _pad_lanes(p['w_in'], AB_IN_PAD).astype(BF16)
    q_b = p['q_b'].reshape(MLA_Q_LORA, MLA_HEADS, MLA_QK)
    p['q_b_pad'] = _pad_lanes(q_b, LANES).reshape(MLA_Q_LORA, MLA_HEADS * LANES).astype(BF16)
    kv_b = p['kv_b'].reshape(MLA_KV_LORA, MLA_HEADS, MLA_NOPE + MLA_V)
    p['wk_pad'] = _pad_lanes(kv_b[:, :, :MLA_NOPE], LANES).reshape(MLA_KV_LORA, MLA_HEADS * LANES).astype(BF16)
    p['wv'] = kv_b[:, :, MLA_NOPE:].reshape(MLA_KV_LORA, MLA_HEADS * MLA_V).astype(BF16)
    p['q_norm_pad'] = _pad_lanes(p['q_norm'].reshape(1, -1), LANES)
    p['k_norm_pad'] = _pad_lanes(p['k_norm'].reshape(1, -1), LANES)
    return p


def _prep_layer1(p):
    p = _prep_common(dict(p))
    p['w_in'] = p['w_in'].astype(BF16)
    p['q_norm2'] = jnp.tile(p['q_norm'].reshape(1, -1), (1, LANES // SWA_HD))
    p['k_norm2'] = jnp.tile(p['k_norm'].reshape(1, -1), (1, LANES // SWA_HD))
    return p


def _group_forward(x3, mods, p0, p1, lb_logits, caches, latent):
    batch, seq, d = x3.shape
    t = batch * seq
    x = x3.reshape(t, d)
    tpg = seq if latent else t
    sh1, sc1, g1, sh2, sc2, g2 = mods[0]

    z = modnorm_matmul(x, p0['norm1'], sh1, sc1, p0['w_in'], tpg)
    tab_b = _rope_lane_tables(seq, MLA_ROPE, LANES, MLA_NOPE) if latent else None
    s0f, s0b = (caches['hg_f'], caches['hg_b']) if latent else (None, None)
    o_f, o_b, s_f, s_b = hgrn_bidir(z, lb_logits, s0f, s0b, batch, seq)
    q = mla_queries(z, p0['qa_norm'], p0['q_b_pad'], p0['q_norm_pad'], tab_b, seq)
    ckv, k, v = mla_keys_values(z, 2816 // LANES, z, 2944 // LANES, p0['kva_norm'], p0['wk_pad'], p0['wv'],
                                p0['k_norm_pad'], tab_b, seq, norm_input=True)
    n_k = seq
    if latent:
        n_ctx = caches['ckv'].shape[1]
        ckv_c = caches['ckv'].reshape(batch * n_ctx, MLA_KV_LORA)
        kpe_c = _pad_lanes(caches['kpe'].reshape(batch * n_ctx, MLA_ROPE), LANES)
        _, k_c, v_c = mla_keys_values(ckv_c, 0, kpe_c, 0, p0['kva_norm'], p0['wk_pad'], p0['wv'],
                                      p0['k_norm_pad'], None, n_ctx, norm_input=False)
        cat = lambda a, b: jnp.concatenate([a.reshape(batch, n_ctx, -1), b.reshape(batch, seq, -1)],
                                           axis=1).reshape(batch * (n_ctx + seq), -1)
        k, v = cat(k_c, k), cat(v_c, v)
        n_k = n_ctx + seq
    o_mla = mla_attention(q, k, v, batch, seq, n_k)
    x = out_proj_layer0(o_f, o_b, z, p0['hg_out_norm'], o_mla, p0['w_out'], x, g1, tpg)
    x = moe_layer(x, p0, sh2, sc2, g2, tpg)
    state0 = (s_f, s_b, ckv.reshape(batch, seq, MLA_KV_LORA), z[:, 2944:2944 + MLA_ROPE].reshape(batch, seq, MLA_ROPE))

    sh1, sc1, g1, sh2, sc2, g2 = mods[1]
    z = modnorm_matmul(x, p1['norm1'], sh1, sc1, p1['w_in'], tpg)
    tab_c = _rope_lane_tables(seq, SWA_HD, SWA_HD, 0) if latent else None
    q, k_cache, k, v = swa_prep(z, p1['q_norm2'], p1['k_norm2'], tab_c, seq)
    sink = p1['sink'].astype(F32)
    if latent:
        n_ctx = caches['k1'].shape[1]
        k_c = caches['k1'].reshape(batch, n_ctx, -1).astype(BF16)
        v_c = caches['v1'].reshape(batch, n_ctx, -1).astype(BF16)
        a = swa_window(q, k, v, k_c, v_c, sink, batch, seq, n_ctx)
    else:
        a = swa_dense(q, k, v, sink, batch, seq)
    x = out_proj_layer1(a, p1['w_out'], x, g1, tpg)
    x = moe_layer(x, p1, sh2, sc2, g2, tpg)
    nkv = SWA_KV_HEADS * SWA_HD
    state1 = (k_cache.reshape(batch, seq, SWA_KV_HEADS, SWA_HD),
              z[:, SWA_HEADS * SWA_HD + nkv:].reshape(batch, seq, SWA_KV_HEADS, SWA_HD))
    return x.reshape(batch, seq, d), state0, state1


def kernel(x_prompt, x_sample, state_l0_hgrn_fwd, state_l0_hgrn_bwd, cache_l0_mla_ckv, cache_l0_mla_kpe, cache_l1_k, cache_l1_v, c, c_ctx, hgrn_lb_logits, l0_ada_w, l0_ada_b, l0_norm1, l0_norm2, l0_w_in, l0_hg_out_norm, l0_qa_norm, l0_q_b, l0_kva_norm, l0_kv_b, l0_q_norm, l0_k_norm, l0_w_out, l0_router_w, l0_router_b, l0_w_gu, l0_b_gu, l0_w_down, l0_b_down, l1_ada_w, l1_ada_b, l1_norm1, l1_norm2, l1_w_in, l1_q_norm, l1_k_norm, l1_sink, l1_w_out, l1_router_w, l1_router_b, l1_w_gu, l1_b_gu, l1_w_down, l1_b_down):
    p0 = _prep_layer0(dict(norm1=l0_norm1, norm2=l0_norm2, w_in=l0_w_in, hg_out_norm=l0_hg_out_norm,
                           qa_norm=l0_qa_norm, q_b=l0_q_b, kva_norm=l0_kva_norm, kv_b=l0_kv_b,
                           q_norm=l0_q_norm, k_norm=l0_k_norm, w_out=l0_w_out, router_w=l0_router_w,
                           router_b=l0_router_b, w_gu=l0_w_gu, b_gu=l0_b_gu, w_down=l0_w_down,
                           b_down=l0_b_down))
    p1 = _prep_layer1(dict(norm1=l1_norm1, norm2=l1_norm2, w_in=l1_w_in, q_norm=l1_q_norm, k_norm=l1_k_norm,
                           sink=l1_sink, w_out=l1_w_out, router_w=l1_router_w, router_b=l1_router_b,
                           w_gu=l1_w_gu, b_gu=l1_b_gu, w_down=l1_w_down, b_down=l1_b_down))
    dec_batch = c.shape[0]
    d = c.shape[1]
    cond8 = jnp.concatenate([c_ctx[None, :], c, jnp.zeros((8 - 1 - dec_batch, d), F32)], axis=0)
    mods_ctx, mods_lat = [], []
    for w, b in ((l0_ada_w, l0_ada_b), (l1_ada_w, l1_ada_b)):
        mod = ada_params(cond8, w, b)
        mods_ctx.append([m.reshape(1, 1, d) for m in jnp.split(mod[0:1], 6, axis=-1)])
        mods_lat.append([m.reshape(dec_batch, 1, d) for m in jnp.split(mod[1:1 + dec_batch], 6, axis=-1)])

    y_prompt, st0, st1 = _group_forward(x_prompt, mods_ctx, p0, p1, hgrn_lb_logits, None, latent=False)
    caches = dict(hg_f=state_l0_hgrn_fwd, hg_b=state_l0_hgrn_bwd, ckv=cache_l0_mla_ckv, kpe=cache_l0_mla_kpe,
                  k1=cache_l1_k, v1=cache_l1_v)
    y_sample, _, _ = _group_forward(x_sample, mods_lat, p0, p1, hgrn_lb_logits, caches, latent=True)
    return (y_prompt, y_sample, st0[0], st0[1], st0[2], st0[3], st1[0], st1[1])
```

```python
import functools

import numpy as np
import jax
import jax.numpy as jnp
from jax import lax
from jax.experimental import pallas as pl
from jax.experimental.pallas import tpu as pltpu

F32 = jnp.float32
BF16 = jnp.bfloat16

D_MODEL = 1024
GRID_W = 64
ROPE_THETA = 10000.0
EPS = 1e-6
HG_HEADS = 4
HG_DK = 128
HG_DV = 128
MLA_HEADS = 8
MLA_NOPE = 64
MLA_ROPE = 32
MLA_V = 64
MLA_QK = MLA_NOPE + MLA_ROPE
MLA_Q_LORA = 256
MLA_KV_LORA = 128
MLA_SCALE = MLA_QK ** -0.5
SWA_HEADS = 16
SWA_KV_HEADS = 4
SWA_HD = 64
SWA_WINDOW = 128
SWA_SCALE = SWA_HD ** -0.5
SWA_GROUP = SWA_HEADS // SWA_KV_HEADS
N_EXPERTS = 32
TOP_K = 4
D_FF = 1024
SWIGLU_LIMIT = 7.0
SWIGLU_ALPHA = 1.702

LANES = 128
HG_CHUNK = 128
HG_LEVELS = 7
HG_MXU_LEVELS = 3
AB_IN_PAD = 3072
MOE_ROWS = 256
NEG_BIG = -1e30
VMEM_LIMIT = 48 * 1024 * 1024
EXPERT_VMEM_LIMIT = 56 * 1024 * 1024


def _cparams(sem):
    return pltpu.CompilerParams(dimension_semantics=sem, vmem_limit_bytes=VMEM_LIMIT)


def _dot(a, b):
    return jnp.dot(a, b, preferred_element_type=F32)


def _dot_nt(a, b):
    return lax.dot_general(a, b, (((1,), (1,)), ((), ())), preferred_element_type=F32)


def _dot_tn(a, b):
    return lax.dot_general(a, b, (((0,), (0,)), ((), ())), preferred_element_type=F32)


def _split2(x):
    hi = x.astype(BF16)
    lo = (x - hi.astype(F32)).astype(BF16)
    return hi, lo


def _dot_hp(a, b):
    ah, al = _split2(a)
    bh, bl = _split2(b)
    return _dot(ah, bh) + _dot(ah, bl) + _dot(al, bh)


def _sigmoid(x):
    return 1.0 / (1.0 + jnp.exp(-x))


def _modnorm(x, gain, shift, scale):
    y = x * lax.rsqrt(jnp.mean(x * x, axis=-1, keepdims=True) + EPS)
    return y * gain * (1.0 + scale) + shift


def _ada_kernel(c_ref, w_ref, b_ref, o_ref):
    c = c_ref[...]
    o_ref[...] = _dot_hp(c * _sigmoid(c), w_ref[...]) + b_ref[...]


def ada_params(cond8, w, b):
    n = w.shape[1]
    tn = 1024
    return pl.pallas_call(
        _ada_kernel,
        grid=(n // tn,),
        in_specs=[pl.BlockSpec((8, D_MODEL), lambda j: (0, 0)),
                  pl.BlockSpec((D_MODEL, tn), lambda j: (0, j)),
                  pl.BlockSpec((1, tn), lambda j: (0, j))],
        out_specs=pl.BlockSpec((8, tn), lambda j: (0, j)),
        out_shape=jax.ShapeDtypeStruct((8, n), F32),
        compiler_params=_cparams(("parallel",)),
        name="ada_params",
    )(cond8, w, b.reshape(1, n))


def _modnorm_matmul_kernel(x_ref, g_ref, sh_ref, sc_ref, w_ref, o_ref):
    h = _modnorm(x_ref[...], g_ref[...], sh_ref[0], sc_ref[0])
    o_ref[...] = _dot(h.astype(BF16), w_ref[...])


def modnorm_matmul(x, gain, shift, scale, w_bf16, tokens_per_group, tm=256):
    t, d = x.shape
    n = w_bf16.shape[1]
    tiles_per_group = tokens_per_group // tm
    grp = lambda i: (i // tiles_per_group, 0, 0)
    return pl.pallas_call(
        _modnorm_matmul_kernel,
        grid=(t // tm,),
        in_specs=[pl.BlockSpec((tm, d), lambda i: (i, 0)),
                  pl.BlockSpec((1, d), lambda i: (0, 0)),
                  pl.BlockSpec((1, 1, d), grp),
                  pl.BlockSpec((1, 1, d), grp),
                  pl.BlockSpec((d, n), lambda i: (0, 0))],
        out_specs=pl.BlockSpec((tm, n), lambda i: (i, 0)),
        out_shape=jax.ShapeDtypeStruct((t, n), F32),
        compiler_params=_cparams(("parallel",)),
        name="modnorm_matmul",
    )(x, gain.reshape(1, d), shift, scale, w_bf16)


def _hgrn_constants():
    c = HG_CHUNK
    t = np.arange(c)[:, None]
    u = np.arange(c)[None, :]
    mats = [(u <= t), (u > t)]
    for l in range(HG_MXU_LEVELS):
        m = 1 << l
        r = (t // (2 * m)) * (2 * m) + m - 1
        mats.append((u > np.minimum(t, r)) & (u <= np.maximum(t, r)))
    fwd = np.concatenate(mats, axis=0).astype(np.float32)
    bwd = np.concatenate([mm[::-1, ::-1] for mm in mats], axis=0).astype(np.float32)
    x = np.bitwise_xor(t, u)
    lvl = np.where(x > 0, np.floor(np.log2(np.maximum(x, 1))), HG_LEVELS).astype(np.int32)
    lv_f = np.where(t >= u, lvl, -1).astype(np.int32)
    return fwd, bwd, lv_f, lv_f.T.copy()


def _hgrn_direction(qs, fpres, vs, lbs, mcat, lv, sts, forward):
    c = HG_CHUNK
    n = len(qs)
    fs = [lb + (1.0 - lb) * _sigmoid(fp) for lb, fp in zip(lbs, fpres)]
    kks = [1.0 - f for f in fs]
    parts = []
    for f in fs:
        parts += list(_split2(jnp.log(f)))
    x_all = _dot(mcat, jnp.concatenate(parts, axis=1))
    xs = [x_all[:, 2 * i * c:(2 * i + 1) * c] + x_all[:, (2 * i + 1) * c:(2 * i + 2) * c] for i in range(n)]
    gs = [x[0:c] for x in xs]
    qbs = [q.astype(BF16) for q in qs]
    kbs = [kk.astype(BF16) for kk in kks]
    vbs = [v.astype(BF16) for v in vs]
    os_ = [_dot_nt((q * jnp.exp(g)).astype(BF16), st.astype(BF16)) for q, g, st in zip(qs, gs, sts)]
    accs = [jnp.where(lv == HG_LEVELS, _dot_nt(qb, kb), 0.0) for qb, kb in zip(qbs, kbs)]
    for l in range(HG_LEVELS):
        for i in range(n):
            if l < HG_MXU_LEVELS:
                x = xs[i][(2 + l) * c:(3 + l) * c]
            else:
                m = 1 << l
                ref_rows = [j * 2 * m + (m - 1 if forward else m) for j in range(c // (2 * m))]
                g_ref = jnp.concatenate(
                    [jnp.broadcast_to(gs[i][r:r + 1, :], (2 * m, c)) for r in ref_rows], axis=0)
                x = -jnp.abs(gs[i] - g_ref)
            e = jnp.exp(x)
            p = _dot_nt((qs[i] * e).astype(BF16), (kks[i] * e).astype(BF16))
            accs[i] = jnp.where(lv == l, p, accs[i])
    edge_row = c - 1 if forward else 0
    outs, new_sts = [], []
    for i in range(n):
        outs.append(os_[i] + _dot(accs[i].astype(BF16), vbs[i]))
        k_end = (kks[i] * jnp.exp(xs[i][c:2 * c])).astype(BF16)
        new_sts.append(sts[i] * jnp.exp(gs[i][edge_row:edge_row + 1, :]) + _dot_tn(vbs[i], k_end))
    return outs, new_sts


def _hgrn_kernel(*refs, has_init):
    if has_init:
        (qf_ref, qb_ref, ff_ref, fb_ref, vf_ref, vb_ref, lbl_ref, mf_ref, mb_ref, lvf_ref, lvb_ref,
         s0f_ref, s0b_ref, of_ref, ob_ref, sf_ref, sb_ref, stf, stb) = refs
    else:
        (qf_ref, qb_ref, ff_ref, fb_ref, vf_ref, vb_ref, lbl_ref, mf_ref, mb_ref, lvf_ref, lvb_ref,
         of_ref, ob_ref, sf_ref, sb_ref, stf, stb) = refs
    c = pl.program_id(1)
    nc = pl.num_programs(1)

    @pl.when(c == 0)
    def _():
        for h in range(HG_HEADS):
            if has_init:
                stf[h] = s0f_ref[0, h].T
                stb[h] = s0b_ref[0, h].T
            else:
                stf[h] = jnp.zeros((HG_DV, HG_DK), F32)
                stb[h] = jnp.zeros((HG_DV, HG_DK), F32)

    rows = [lbl_ref[:, j, :] for j in range(lbl_ref.shape[1])]
    mx = functools.reduce(jnp.maximum, rows)
    ex = [jnp.exp(r - mx) for r in rows]
    lb = ex[0] / functools.reduce(lambda a, b: a + b, ex)

    heads = [slice(h * LANES, (h + 1) * LANES) for h in range(HG_HEADS)]
    o_f, st_f = _hgrn_direction([qf_ref[0, :, hs] for hs in heads], [ff_ref[0, :, hs] for hs in heads],
                                [vf_ref[0, :, hs] for hs in heads], [lb[0:1, hs] for hs in heads],
                                mf_ref[...], lvf_ref[...], [stf[h] for h in range(HG_HEADS)], True)
    o_b, st_b = _hgrn_direction([qb_ref[0, :, hs] for hs in heads], [fb_ref[0, :, hs] for hs in heads],
                                [vb_ref[0, :, hs] for hs in heads], [lb[1:2, hs] for hs in heads],
                                mb_ref[...], lvb_ref[...], [stb[h] for h in range(HG_HEADS)], False)
    for h, hs in enumerate(heads):
        of_ref[0, :, hs] = o_f[h]
        ob_ref[0, :, hs] = o_b[h]
        stf[h] = st_f[h]
        stb[h] = st_b[h]

    @pl.when(c == nc - 1)
    def _():
        for h in range(HG_HEADS):
            sf_ref[0, h] = stf[h].T
            sb_ref[0, h] = stb[h].T


def hgrn_bidir(z, lb_logits, s0f, s0b, batch, seq):
    nc = seq // HG_CHUNK
    z3 = z.reshape(batch, seq, z.shape[1])
    mf, mb, lvf, lvb = _hgrn_constants()
    has_init = s0f is not None
    width = HG_HEADS * LANES
    blk = (1, HG_CHUNK, width)
    fwd = lambda off: pl.BlockSpec(blk, lambda b, c: (b, c, off))
    bwd = lambda off: pl.BlockSpec(blk, lambda b, c: (b, nc - 1 - c, off))
    full = lambda a: pl.BlockSpec(a.shape, lambda b, c: (0,) * a.ndim)
    st_spec = pl.BlockSpec((1, HG_HEADS, HG_DK, HG_DV), lambda b, c: (b, 0, 0, 0))
    consts = [jnp.asarray(mf, BF16), jnp.asarray(mb, BF16), jnp.asarray(lvf), jnp.asarray(lvb)]
    in_specs = [fwd(0), bwd(0), fwd(1), bwd(2), fwd(3), bwd(3),
                pl.BlockSpec(lb_logits.shape, lambda b, c: (0, 0, 0))]
    in_specs += [full(a) for a in consts]
    args = [z3] * 6 + [lb_logits] + consts
    if has_init:
        in_specs += [st_spec, st_spec]
        args += [s0f, s0b]
    o_shape = jax.ShapeDtypeStruct((batch, seq, width), F32)
    s_shape = jax.ShapeDtypeStruct((batch, HG_HEADS, HG_DK, HG_DV), F32)
    o_f, o_b, s_f, s_b = pl.pallas_call(
        functools.partial(_hgrn_kernel, has_init=has_init),
        grid=(batch, nc),
        in_specs=in_specs,
        out_specs=[pl.BlockSpec(blk, lambda b, c: (b, c, 0)),
                   pl.BlockSpec(blk, lambda b, c: (b, nc - 1 - c, 0)),
                   st_spec, st_spec],
        out_shape=[o_shape, o_shape, s_shape, s_shape],
        scratch_shapes=[pltpu.VMEM((HG_HEADS, HG_DV, HG_DK), F32), pltpu.VMEM((HG_HEADS, HG_DV, HG_DK), F32)],
        compiler_params=_cparams(("parallel", "arbitrary")),
        name="hgrn_bidir",
    )(*args)
    t = batch * seq
    return o_f.reshape(t, -1), o_b.reshape(t, -1), s_f, s_b


def _axial_tables(n_tokens, n_rot):
    t = jnp.arange(n_tokens)
    row = (t // GRID_W).astype(F32)
    col = (t % GRID_W).astype(F32)
    n_freq = n_rot // 4
    inv = jnp.power(ROPE_THETA, -jnp.arange(n_freq, dtype=F32) / n_freq)
    ang = jnp.concatenate([row[:, None] * inv, col[:, None] * inv], axis=-1)
    return jnp.cos(ang), jnp.sin(ang)


def _rope_lane_tables(n_tokens, n_rot, head_width, first_rot_lane):
    cos, sin = _axial_tables(n_tokens, n_rot)
    half = n_rot // 2
    c_head = jnp.ones((n_tokens, head_width), F32)
    sa_head = jnp.zeros((n_tokens, head_width), F32)
    sb_head = jnp.zeros((n_tokens, head_width), F32)
    a0, a1, a2 = first_rot_lane, first_rot_lane + half, first_rot_lane + n_rot
    c_head = c_head.at[:, a0:a1].set(cos).at[:, a1:a2].set(cos)
    sa_head = sa_head.at[:, a0:a1].set(-sin)
    sb_head = sb_head.at[:, a1:a2].set(sin)
    reps = LANES // head_width
    tile = lambda a: jnp.tile(a, (1, reps))
    return tile(c_head), tile(sa_head), tile(sb_head)


def _rope(x, c, sa, sb, half):
    return x * c + pltpu.roll(x, LANES - half, 1) * sa + pltpu.roll(x, half, 1) * sb


def _mla_q_kernel(*refs, rope):
    if rope:
        qa_ref, qan_ref, qb_ref, qn_ref, c_ref, sa_ref, sb_ref, o_ref = refs
    else:
        qa_ref, qan_ref, qb_ref, qn_ref, o_ref = refs
    qa = qa_ref[...]
    qn = qa * lax.rsqrt(jnp.mean(qa * qa, axis=-1, keepdims=True) + EPS) * qan_ref[...]
    qfull = _dot(qn.astype(BF16), qb_ref[...])
    outs = []
    for h in range(MLA_HEADS):
        qh = qfull[:, h * LANES:(h + 1) * LANES]
        ms = jnp.sum(qh * qh, axis=-1, keepdims=True) * (1.0 / MLA_QK)
        qh = qh * lax.rsqrt(ms + EPS) * qn_ref[...]
        if rope:
            qh = _rope(qh, c_ref[...], sa_ref[...], sb_ref[...], MLA_ROPE // 2)
        outs.append((qh * MLA_SCALE).astype(BF16))
    o_ref[...] = jnp.concatenate(outs, axis=1)


def mla_queries(z, qa_norm, q_b_pad, q_norm_pad, tables, tokens_per_batch, tm=256):
    t = z.shape[0]
    rope = tables is not None
    row = lambda i: (0, 0)
    in_specs = [pl.BlockSpec((tm, MLA_Q_LORA), lambda i: (i, 2560 // MLA_Q_LORA)),
                pl.BlockSpec((1, MLA_Q_LORA), row),
                pl.BlockSpec(q_b_pad.shape, row),
                pl.BlockSpec((1, LANES), row)]
    args = [z, qa_norm.reshape(1, -1), q_b_pad, q_norm_pad]
    if rope:
        tpb = tokens_per_batch // tm
        in_specs += [pl.BlockSpec((tm, LANES), lambda i: (i % tpb, 0))] * 3
        args += list(tables)
    return pl.pallas_call(
        functools.partial(_mla_q_kernel, rope=rope),
        grid=(t // tm,),
        in_specs=in_specs,
        out_specs=pl.BlockSpec((tm, MLA_HEADS * LANES), lambda i: (i, 0)),
        out_shape=jax.ShapeDtypeStruct((t, MLA_HEADS * LANES), BF16),
        compiler_params=_cparams(("parallel",)),
        name="mla_queries",
    )(*args)


def _mla_kv_kernel(*refs, norm_input, rope):
    if rope:
        kva_ref, kpe_ref, kvan_ref, wk_ref, wv_ref, kn_ref, c_ref, sa_ref, sb_ref, ckv_ref, k_ref, v_ref = refs
    else:
        kva_ref, kpe_ref, kvan_ref, wk_ref, wv_ref, kn_ref, ckv_ref, k_ref, v_ref = refs
    ckv = kva_ref[...]
    if norm_input:
        ckv = ckv * lax.rsqrt(jnp.mean(ckv * ckv, axis=-1, keepdims=True) + EPS) * kvan_ref[...]
    ckv_ref[...] = ckv
    cb = ckv.astype(BF16)
    knope = _dot(cb, wk_ref[...])
    v_ref[...] = _dot(cb, wv_ref[...]).astype(BF16)
    kpe = pltpu.roll(kpe_ref[...], MLA_NOPE, 1)
    outs = []
    for h in range(MLA_HEADS):
        kh = knope[:, h * LANES:(h + 1) * LANES] + kpe
        ms = jnp.sum(kh * kh, axis=-1, keepdims=True) * (1.0 / MLA_QK)
        kh = kh * lax.rsqrt(ms + EPS) * kn_ref[...]
        if rope:
            kh = _rope(kh, c_ref[...], sa_ref[...], sb_ref[...], MLA_ROPE // 2)
        outs.append(kh.astype(BF16))
    k_ref[...] = jnp.concatenate(outs, axis=1)


def mla_keys_values(kva_src, kva_col, kpe_src, kpe_col, kva_norm, wk_pad, wv, k_norm_pad, tables,
                    tokens_per_batch, norm_input, tm=256):
    t = kva_src.shape[0]
    rope = tables is not None
    row = lambda i: (0, 0)
    in_specs = [pl.BlockSpec((tm, LANES), lambda i: (i, kva_col)),
                pl.BlockSpec((tm, LANES), lambda i: (i, kpe_col)),
                pl.BlockSpec((1, LANES), row),
                pl.BlockSpec(wk_pad.shape, row),
                pl.BlockSpec(wv.shape, row),
                pl.BlockSpec((1, LANES), row)]
    args = [kva_src, kpe_src, kva_norm.reshape(1, -1), wk_pad, wv, k_norm_pad]
    if rope:
        tpb = tokens_per_batch // tm
        in_specs += [pl.BlockSpec((tm, LANES), lambda i: (i % tpb, 0))] * 3
        args += list(tables)
    return pl.pallas_call(
        functools.partial(_mla_kv_kernel, norm_input=norm_input, rope=rope),
        grid=(t // tm,),
        in_specs=in_specs,
        out_specs=[pl.BlockSpec((tm, LANES), lambda i: (i, 0)),
                   pl.BlockSpec((tm, MLA_HEADS * LANES), lambda i: (i, 0)),
                   pl.BlockSpec((tm, MLA_HEADS * MLA_V), lambda i: (i, 0))],
        out_shape=[jax.ShapeDtypeStruct((t, LANES), F32),
                   jax.ShapeDtypeStruct((t, MLA_HEADS * LANES), BF16),
                   jax.ShapeDtypeStruct((t, MLA_HEADS * MLA_V), BF16)],
        compiler_params=_cparams(("parallel",)),
        name="mla_keys_values",
    )(*args)


def _mla_attn_kernel(q_ref, k_ref, v_ref, o_ref):
    outs = []
    for j in range(2):
        q = q_ref[0][:, j * LANES:(j + 1) * LANES]
        k = k_ref[0][:, j * LANES:(j + 1) * LANES]
        v = v_ref[0][:, j * MLA_V:(j + 1) * MLA_V]
        s = _dot_nt(q, k)
        p = jnp.exp(s - jnp.max(s, axis=-1, keepdims=True))
        l = jnp.sum(p, axis=-1, keepdims=True)
        outs.append(_dot(p.astype(BF16), v) / l)
    o_ref[0] = jnp.concatenate(outs, axis=1)


def mla_attention(q, k, v, batch, n_q, n_k, tq=256):
    q3 = q.reshape(batch, n_q, -1)
    k3 = k.reshape(batch, n_k, -1)
    v3 = v.reshape(batch, n_k, -1)
    out = pl.pallas_call(
        _mla_attn_kernel,
        grid=(batch, MLA_HEADS // 2, n_q // tq),
        in_specs=[pl.BlockSpec((1, tq, 2 * LANES), lambda b, j, i: (b, i, j)),
                  pl.BlockSpec((1, n_k, 2 * LANES), lambda b, j, i: (b, 0, j)),
                  pl.BlockSpec((1, n_k, 2 * MLA_V), lambda b, j, i: (b, 0, j))],
        out_specs=pl.BlockSpec((1, tq, 2 * MLA_V), lambda b, j, i: (b, i, j)),
        out_shape=jax.ShapeDtypeStruct((batch, n_q, MLA_HEADS * MLA_V), F32),
        compiler_params=_cparams(("parallel", "parallel", "arbitrary")),
        name="mla_attention",
    )(q3, k3, v3)
    return out.reshape(batch * n_q, -1)


def _out0_kernel(of_ref, ob_ref, ag_ref, hgn_ref, om_ref, w_ref, x_ref, g_ref, o_ref):
    o = of_ref[...] + ob_ref[...]
    ag = ag_ref[...]
    parts = []
    for h in range(HG_HEADS):
        oh = o[:, h * HG_DV:(h + 1) * HG_DV]
        oh = oh * lax.rsqrt(jnp.mean(oh * oh, axis=-1, keepdims=True) + EPS) * hgn_ref[...]
        gh = ag[:, h * HG_DV:(h + 1) * HG_DV]
        parts.append((oh * (gh * _sigmoid(gh))).astype(BF16))
    oa = jnp.concatenate(parts, axis=1)
    n_a = HG_HEADS * HG_DV
    mix = _dot(oa, w_ref[0:n_a, :]) + _dot(om_ref[...].astype(BF16), w_ref[n_a:, :])
    o_ref[...] = x_ref[...] + g_ref[0] * mix


def out_proj_layer0(o_f, o_b, z, hg_norm, o_mla, w_out_bf16, x, gate, tokens_per_group, tm=256):
    t, d = x.shape
    n_a = HG_HEADS * HG_DV
    tiles_per_group = tokens_per_group // tm
    tile = lambda w: pl.BlockSpec((tm, w), lambda i: (i, 0))
    return pl.pallas_call(
        _out0_kernel,
        grid=(t // tm,),
        in_specs=[tile(n_a), tile(n_a),
                  pl.BlockSpec((tm, n_a), lambda i: (i, 2048 // n_a)),
                  pl.BlockSpec((1, HG_DV), lambda i: (0, 0)),
                  tile(o_mla.shape[1]),
                  pl.BlockSpec(w_out_bf16.shape, lambda i: (0, 0)),
                  tile(d),
                  pl.BlockSpec((1, 1, d), lambda i: (i // tiles_per_group, 0, 0))],
        out_specs=tile(d),
        out_shape=jax.ShapeDtypeStruct((t, d), F32),
        compiler_params=_cparams(("parallel",)),
        name="out_proj_layer0",
    )(o_f, o_b, z, hg_norm.reshape(1, -1), o_mla, w_out_bf16, x, gate)


def _out1_kernel(a_ref, w_ref, x_ref, g_ref, o_ref):
    o_ref[...] = x_ref[...] + g_ref[0] * _dot(a_ref[...].astype(BF16), w_ref[...])


def out_proj_layer1(a, w_out_bf16, x, gate, tokens_per_group, tm=256):
    t, d = x.shape
    tiles_per_group = tokens_per_group // tm
    tile = lambda w: pl.BlockSpec((tm, w), lambda i: (i, 0))
    return pl.pallas_call(
        _out1_kernel,
        grid=(t // tm,),
        in_specs=[tile(a.shape[1]),
                  pl.BlockSpec(w_out_bf16.shape, lambda i: (0, 0)),
                  tile(d),
                  pl.BlockSpec((1, 1, d), lambda i: (i // tiles_per_group, 0, 0))],
        out_specs=tile(d),
        out_shape=jax.ShapeDtypeStruct((t, d), F32),
        compiler_params=_cparams(("parallel",)),
        name="out_proj_layer1",
    )(a, w_out_bf16, x, gate)


def _head_rms(x, gain2):
    sq = x * x
    lane = lax.broadcasted_iota(jnp.int32, x.shape, 1)
    first = lane < SWA_HD
    lo = jnp.sum(jnp.where(first, sq, 0.0), axis=-1, keepdims=True)
    hi = jnp.sum(jnp.where(first, 0.0, sq), axis=-1, keepdims=True)
    ms = jnp.where(first, lo, hi) * (1.0 / SWA_HD)
    return x * lax.rsqrt(ms + EPS) * gain2


def _swa_prep_kernel(*refs, rope):
    if rope:
        zq_ref, zk_ref, zv_ref, qn_ref, kn_ref, c_ref, sa_ref, sb_ref, q_ref, kc_ref, k_ref, v_ref = refs
    else:
        zq_ref, zk_ref, zv_ref, qn_ref, kn_ref, q_ref, kc_ref, k_ref, v_ref = refs
    half = SWA_HD // 2

    def rot(x):
        return _rope(x, c_ref[...], sa_ref[...], sb_ref[...], half) if rope else x

    zq = zq_ref[...]
    qs = []
    for p in range(zq.shape[1] // LANES):
        x = _head_rms(zq[:, p * LANES:(p + 1) * LANES], qn_ref[...])
        qs.append((rot(x) * SWA_SCALE).astype(BF16))
    q_ref[...] = jnp.concatenate(qs, axis=1)
    zk = zk_ref[...]
    kn, kr = [], []
    for p in range(zk.shape[1] // LANES):
        x = _head_rms(zk[:, p * LANES:(p + 1) * LANES], kn_ref[...])
        kn.append(x)
        kr.append(rot(x).astype(BF16))
    kc_ref[...] = jnp.concatenate(kn, axis=1)
    k_ref[...] = jnp.concatenate(kr, axis=1)
    v_ref[...] = zv_ref[...].astype(BF16)


def swa_prep(z, q_norm2, k_norm2, tables, tokens_per_batch, tm=256):
    t = z.shape[0]
    nq = SWA_HEADS * SWA_HD
    nkv = SWA_KV_HEADS * SWA_HD
    rope = tables is not None
    row = lambda i: (0, 0)
    in_specs = [pl.BlockSpec((tm, nq), lambda i: (i, 0)),
                pl.BlockSpec((tm, nkv), lambda i: (i, nq // nkv)),
                pl.BlockSpec((tm, nkv), lambda i: (i, nq // nkv + 1)),
                pl.BlockSpec((1, LANES), row),
                pl.BlockSpec((1, LANES), row)]
    args = [z, z, z, q_norm2, k_norm2]
    if rope:
        tpb = tokens_per_batch // tm
        in_specs += [pl.BlockSpec((tm, LANES), lambda i: (i % tpb, 0))] * 3
        args += list(tables)
    tile = lambda w: pl.BlockSpec((tm, w), lambda i: (i, 0))
    return pl.pallas_call(
        functools.partial(_swa_prep_kernel, rope=rope),
        grid=(t // tm,),
        in_specs=in_specs,
        out_specs=[tile(nq), tile(nkv), tile(nkv), tile(nkv)],
        out_shape=[jax.ShapeDtypeStruct((t, nq), BF16),
                   jax.ShapeDtypeStruct((t, nkv), F32),
                   jax.ShapeDtypeStruct((t, nkv), BF16),
                   jax.ShapeDtypeStruct((t, nkv), BF16)],
        compiler_params=_cparams(("parallel",)),
        name="swa_prep",
    )(*args)


def _sink_softmax_pv(s, sk, vn):
    m = jnp.maximum(jnp.max(s, axis=-1, keepdims=True), sk)
    p = jnp.exp(s - m)
    l = jnp.sum(p, axis=-1, keepdims=True) + jnp.exp(sk - m)
    return _dot(p.astype(BF16), vn) / l


def _sink_attention(q8, k_all, v_all, bias, sink_ref, pair, stack):
    outs = []
    tq = q8.shape[0]
    for n in range(2):
        kn = k_all[:, n * SWA_HD:(n + 1) * SWA_HD]
        vn = v_all[:, n * SWA_HD:(n + 1) * SWA_HD]
        heads = [n * SWA_GROUP + g for g in range(SWA_GROUP)]
        qs = [q8[:, hq * SWA_HD:(hq + 1) * SWA_HD] for hq in heads]
        sinks = [sink_ref[pair * 2 * SWA_GROUP + hq] for hq in heads]
        if stack:
            sk = jnp.concatenate([jnp.full((tq, 1), s, F32) for s in sinks], axis=0)
            s = _dot_nt(jnp.concatenate(qs, axis=0), kn)
            if bias is not None:
                s = s + bias
            o4 = _sink_softmax_pv(s, sk, vn)
            outs += [o4[g * tq:(g + 1) * tq] for g in range(SWA_GROUP)]
        else:
            for q, sk in zip(qs, sinks):
                s = _dot_nt(q, kn)
                if bias is not None:
                    s = s + bias
                outs.append(_sink_softmax_pv(s, sk, vn))
    return jnp.concatenate(outs, axis=1)


def _swa_dense_kernel(sink_ref, q_ref, k_ref, v_ref, o_ref):
    o_ref[0] = _sink_attention(q_ref[0], k_ref[0], v_ref[0], None, sink_ref, pl.program_id(1), stack=False)


def swa_dense(q, k, v, sink, batch, seq):
    nq = SWA_HEADS * SWA_HD
    q3 = q.reshape(batch, seq, nq)
    k3 = k.reshape(batch, seq, -1)
    v3 = v.reshape(batch, seq, -1)
    out = pl.pallas_call(
        _swa_dense_kernel,
        grid=(batch, 2),
        in_specs=[pl.BlockSpec(memory_space=pltpu.SMEM),
                  pl.BlockSpec((1, seq, nq // 2), lambda b, j: (b, 0, j)),
                  pl.BlockSpec((1, seq, LANES), lambda b, j: (b, 0, j)),
                  pl.BlockSpec((1, seq, LANES), lambda b, j: (b, 0, j))],
        out_specs=pl.BlockSpec((1, seq, nq // 2), lambda b, j: (b, 0, j)),
        out_shape=jax.ShapeDtypeStruct((batch, seq, nq), F32),
        compiler_params=_cparams(("parallel", "parallel")),
        name="swa_dense",
    )(sink, q3, k3, v3)
    return out.reshape(batch * seq, nq)


def _swa_window_kernel(sink_ref, q_ref, kc_ref, vc_ref, kp_ref, k0_ref, kn_ref, vp_ref, v0_ref, vn_ref, o_ref):
    i = pl.program_id(2)
    nb = pl.num_programs(2)
    w = SWA_WINDOW
    n_ctx = kc_ref.shape[1]
    k_all = jnp.concatenate([kc_ref[0], kp_ref[0], k0_ref[0], kn_ref[0]], axis=0)
    v_all = jnp.concatenate([vc_ref[0], vp_ref[0], v0_ref[0], vn_ref[0]], axis=0)
    shape = (SWA_GROUP * w, n_ctx + 3 * w)
    r = lax.broadcasted_iota(jnp.int32, shape, 0) % w
    col = lax.broadcasted_iota(jnp.int32, shape, 1) - n_ctx
    prev_bias = jnp.where(i > 0, 0.0, NEG_BIG)
    next_bias = jnp.where(i < nb - 1, 0.0, NEG_BIG)
    bias = jnp.where(
        col < w,
        jnp.where(col < 0, 0.0, jnp.where(col >= r, prev_bias, NEG_BIG)),
        jnp.where(col < 2 * w, 0.0, jnp.where(col - 2 * w <= r, next_bias, NEG_BIG)))
    o_ref[0] = _sink_attention(q_ref[0], k_all, v_all, bias, sink_ref, pl.program_id(1), stack=True)


def swa_window(q, k, v, k_ctx, v_ctx, sink, batch, seq, n_ctx):
    nq = SWA_HEADS * SWA_HD
    w = SWA_WINDOW
    nb = seq // w
    q3 = q.reshape(batch, seq, nq)
    k3 = k.reshape(batch, seq, -1)
    v3 = v.reshape(batch, seq, -1)
    ctx = pl.BlockSpec((1, n_ctx, LANES), lambda b, j, i: (b, 0, j))
    prv = pl.BlockSpec((1, w, LANES), lambda b, j, i: (b, jnp.maximum(i - 1, 0), j))
    cur = pl.BlockSpec((1, w, LANES), lambda b, j, i: (b, i, j))
    nxt = pl.BlockSpec((1, w, LANES), lambda b, j, i: (b, jnp.minimum(i + 1, nb - 1), j))
    out = pl.pallas_call(
        _swa_window_kernel,
        grid=(batch, 2, nb),
        in_specs=[pl.BlockSpec(memory_space=pltpu.SMEM),
                  pl.BlockSpec((1, w, nq // 2), lambda b, j, i: (b, i, j)),
                  ctx, ctx, prv, cur, nxt, prv, cur, nxt],
        out_specs=pl.BlockSpec((1, w, nq // 2), lambda b, j, i: (b, i, j)),
        out_shape=jax.ShapeDtypeStruct((batch, seq, nq), F32),
        compiler_params=_cparams(("parallel", "parallel", "arbitrary")),
        name="swa_window",
    )(sink, q3, k_ctx, v_ctx, k3, k3, k3, v3, v3, v3)
    return out.reshape(batch * seq, nq)


def _router_kernel(x_ref, g_ref, sh_ref, sc_ref, rw_ref, rb_ref, tri_ref, h_ref, idx_ref, gate_ref, rank_ref,
                   cnt_ref, cnt):
    @pl.when(pl.program_id(0) == 0)
    def _():
        cnt[...] = jnp.zeros_like(cnt)

    h = _modnorm(x_ref[...], g_ref[...], sh_ref[0], sc_ref[0])
    h_ref[...] = h
    logits = _dot_hp(h, rw_ref[...]) + rb_ref[...]
    lane = lax.broadcasted_iota(jnp.int32, logits.shape, 1)
    work = logits
    vals, idxs = [], []
    for _ in range(TOP_K):
        m = jnp.max(work, axis=-1, keepdims=True)
        ik = jnp.min(jnp.where(work == m, lane, LANES), axis=-1, keepdims=True)
        vals.append(m)
        idxs.append(ik)
        work = jnp.where(lane == ik, 2.0 * NEG_BIG, work)
    es = [jnp.exp(v - vals[0]) for v in vals]
    denom = es[0] + es[1] + es[2] + es[3]
    idx_out = jnp.zeros(logits.shape, jnp.int32)
    gate_out = jnp.zeros(logits.shape, F32)
    for k in range(TOP_K):
        idx_out = jnp.where(lane == k, idxs[k], idx_out)
        gate_out = jnp.where(lane == k, es[k] / denom, gate_out)
    idx_ref[...] = idx_out
    gate_ref[...] = gate_out
    chosen = jnp.zeros(logits.shape, F32)
    for k in range(TOP_K):
        chosen = jnp.where(lane == idxs[k], 1.0, chosen)
    before = _dot(tri_ref[...], chosen.astype(BF16)) + cnt[...]
    rank_out = jnp.zeros(logits.shape, jnp.int32)
    for k in range(TOP_K):
        rk = jnp.sum(jnp.where(lane == idxs[k], before, 0.0), axis=-1, keepdims=True)
        rank_out = jnp.where(lane == k, rk.astype(jnp.int32), rank_out)
    rank_ref[...] = rank_out
    cnt[...] = cnt[...] + jnp.sum(chosen, axis=0, keepdims=True)
    cnt_ref[...] = cnt[...]


def moe_router(x, gain, shift, scale, rw_pad, rb_pad, tokens_per_group, tm=256):
    t, d = x.shape
    tiles_per_group = tokens_per_group // tm
    grp = lambda i: (i // tiles_per_group, 0, 0)
    tile = lambda w: pl.BlockSpec((tm, w), lambda i: (i, 0))
    tri = jnp.asarray(np.tril(np.ones((tm, tm), np.float32), -1), BF16)
    return pl.pallas_call(
        _router_kernel,
        grid=(t // tm,),
        in_specs=[tile(d),
                  pl.BlockSpec((1, d), lambda i: (0, 0)),
                  pl.BlockSpec((1, 1, d), grp),
                  pl.BlockSpec((1, 1, d), grp),
                  pl.BlockSpec((d, LANES), lambda i: (0, 0)),
                  pl.BlockSpec((1, LANES), lambda i: (0, 0)),
                  pl.BlockSpec((tm, tm), lambda i: (0, 0))],
        out_specs=[tile(d), tile(LANES), tile(LANES), tile(LANES),
                   pl.BlockSpec((1, LANES), lambda i: (0, 0))],
        out_shape=[jax.ShapeDtypeStruct((t, d), F32),
                   jax.ShapeDtypeStruct((t, LANES), jnp.int32),
                   jax.ShapeDtypeStruct((t, LANES), F32),
                   jax.ShapeDtypeStruct((t, LANES), jnp.int32),
                   jax.ShapeDtypeStruct((1, LANES), F32)],
        scratch_shapes=[pltpu.VMEM((1, LANES), F32)],
        compiler_params=_cparams(("arbitrary",)),
        name="moe_router",
    )(x, gain.reshape(1, d), shift, scale, rw_pad, rb_pad, tri)


FFN_COLS = 256
FFN_PIECES = 2 * D_FF // FFN_COLS + D_MODEL // FFN_COLS
SEM_GATHER_A, SEM_GATHER_B, SEM_WRITE_A, SEM_WRITE_B = range(4)


def _ffn_block(x_ref, y_ref, wgu, bgu_ref, wd, bd_ref, issue):
    xb = x_ref[...].astype(BF16)
    acts = []
    for c in range(D_FF // FFN_COLS):
        lo, hi = c * FFN_COLS, (c + 1) * FFN_COLS
        gate = _dot(xb, wgu[:, lo:hi]) + bgu_ref[0][:, lo:hi]
        issue(2 * c)
        up = _dot(xb, wgu[:, D_FF + lo:D_FF + hi]) + bgu_ref[0][:, D_FF + lo:D_FF + hi]
        issue(2 * c + 1)
        gate = jnp.minimum(gate, SWIGLU_LIMIT)
        up = jnp.clip(up, -SWIGLU_LIMIT, SWIGLU_LIMIT)
        acts.append((gate * _sigmoid(SWIGLU_ALPHA * gate) * (up + 1.0)).astype(BF16))
    act = jnp.concatenate(acts, axis=1)
    for n in range(D_MODEL // FFN_COLS):
        lo, hi = n * FFN_COLS, (n + 1) * FFN_COLS
        y_ref[:, lo:hi] = _dot(act, wd[:, lo:hi]) + bd_ref[0][:, lo:hi]
        issue(2 * D_FF // FFN_COLS + n)


def _expert_kernel(nb_ref, bs_ref, rt_ref, ra_ref, h_hbm, wgu_ref, bgu_ref, wd_ref, bd_ref, y_hbm,
                   wgu_bf, wd_bf, xa, xb, ya, yb, sem):
    e = pl.program_id(0)
    nb = nb_ref[e]
    b0 = bs_ref[e]
    shares = np.array_split(np.arange(MOE_ROWS), FFN_PIECES)
    spare_base = y_hbm.shape[0] - MOE_ROWS

    def gather_issue(dst, s, blk):
        base = blk * MOE_ROWS

        def issue(k):
            for r in shares[k]:
                r = int(r)
                pltpu.make_async_copy(h_hbm.at[pl.ds(rt_ref[base + r], 1), :], dst.at[pl.ds(r, 1), :],
                                      sem.at[s]).start()
        return issue

    def scatter_issue(src, s, blk, valid=None):
        base = jnp.maximum(blk, 0) * MOE_ROWS

        def issue(k):
            for r in shares[k]:
                r = int(r)
                row = ra_ref[base + r]
                ok = row >= 0 if valid is None else jnp.logical_and(valid, row >= 0)
                row = jnp.where(ok, row, spare_base + r)
                pltpu.make_async_copy(src.at[pl.ds(r, 1), :], y_hbm.at[pl.ds(row, 1), :], sem.at[s]).start()
        return issue

    def wait(buf, s):
        pltpu.make_async_copy(buf, buf, sem.at[s]).wait()

    def ffn(x_ref, y_ref, *issues):
        def issue(k):
            for f in issues:
                f(k)
        _ffn_block(x_ref, y_ref, wgu_bf, bgu_ref, wd_bf, bd_ref, issue)

    def issue_all(f):
        for k in range(FFN_PIECES):
            f(k)

    @pl.when(e == 0)
    def _():
        ya[...] = jnp.zeros_like(ya)
        yb[...] = jnp.zeros_like(yb)

    @pl.when(nb > 0)
    def _():
        first = gather_issue(xa, SEM_GATHER_A, b0)
        n_slabs = 4
        per = FFN_PIECES // n_slabs
        for s in range(n_slabs):
            rows = slice(s * D_MODEL // n_slabs, (s + 1) * D_MODEL // n_slabs)
            wgu_bf[rows, :] = wgu_ref[0, rows, :].astype(BF16)
            wd_bf[rows, :] = wd_ref[0, rows, :].astype(BF16)
            for k in range(s * per, (s + 1) * per):
                first(k)
        n_pairs = nb // 2
        last = b0 + nb - 1

        def pair(jj, carry):
            j = b0 + 2 * jj
            wait(xa, SEM_GATHER_A)

            @pl.when(jj > 0)
            def _():
                wait(ya, SEM_WRITE_A)

            ffn(xa, ya, gather_issue(xb, SEM_GATHER_B, j + 1), scatter_issue(yb, SEM_WRITE_B, j - 1, jj > 0))
            wait(xb, SEM_GATHER_B)
            wait(yb, SEM_WRITE_B)
            ffn(xb, yb, gather_issue(xa, SEM_GATHER_A, jnp.minimum(j + 2, last)), scatter_issue(ya, SEM_WRITE_A, j))
            return carry

        lax.fori_loop(0, n_pairs, pair, 0)
        wait(xa, SEM_GATHER_A)

        @pl.when(nb % 2 == 1)
        def _():
            @pl.when(n_pairs > 0)
            def _():
                wait(ya, SEM_WRITE_A)

            ffn(xa, ya, scatter_issue(yb, SEM_WRITE_B, last - 1, n_pairs > 0))
            wait(yb, SEM_WRITE_B)
            issue_all(scatter_issue(ya, SEM_WRITE_A, last))
            wait(ya, SEM_WRITE_A)

        @pl.when(nb % 2 == 0)
        def _():
            issue_all(scatter_issue(yb, SEM_WRITE_B, last))
            wait(ya, SEM_WRITE_A)
            wait(yb, SEM_WRITE_B)


def moe_experts(h, n_blk, blk_start, row_tok, row_slot, w_gu, b_gu, w_down, b_down):
    t, d = h.shape
    per_expert = lambda e, nb, bs, rt, ra: (e, 0, 0)
    grid_spec = pltpu.PrefetchScalarGridSpec(
        num_scalar_prefetch=4,
        grid=(N_EXPERTS,),
        in_specs=[pl.BlockSpec(memory_space=pl.ANY),
                  pl.BlockSpec((1, d, 2 * D_FF), per_expert),
                  pl.BlockSpec((1, 1, 2 * D_FF), per_expert),
                  pl.BlockSpec((1, D_FF, d), per_expert),
                  pl.BlockSpec((1, 1, d), per_expert)],
        out_specs=pl.BlockSpec(memory_space=pl.ANY),
        scratch_shapes=[pltpu.VMEM((d, 2 * D_FF), BF16), pltpu.VMEM((D_FF, d), BF16),
                        pltpu.VMEM((MOE_ROWS, d), F32), pltpu.VMEM((MOE_ROWS, d), F32),
                        pltpu.VMEM((MOE_ROWS, d), F32), pltpu.VMEM((MOE_ROWS, d), F32),
                        pltpu.SemaphoreType.DMA((4,))],
    )
    return pl.pallas_call(
        _expert_kernel,
        grid_spec=grid_spec,
        out_shape=jax.ShapeDtypeStruct((t * TOP_K + MOE_ROWS, d), F32),
        compiler_params=pltpu.CompilerParams(dimension_semantics=("arbitrary",),
                                             vmem_limit_bytes=EXPERT_VMEM_LIMIT),
        name="moe_experts",
    )(n_blk, blk_start, row_tok, row_slot, h, w_gu, b_gu.reshape(N_EXPERTS, 1, -1), w_down,
      b_down.reshape(N_EXPERTS, 1, -1))


def _combine_kernel(y_ref, gate_ref, x_ref, g_ref, o_ref):
    d = x_ref.shape[1]
    gates = gate_ref[...]
    acc = gates[:, 0:1] * y_ref[:, 0:d]
    for k in range(1, TOP_K):
        acc = acc + gates[:, k:k + 1] * y_ref[:, k * d:(k + 1) * d]
    o_ref[...] = x_ref[...] + g_ref[0] * acc


def moe_combine(y_slots, gates, x, gate_mod, tokens_per_group, tm=256):
    t, d = x.shape
    tiles_per_group = tokens_per_group // tm
    y4 = y_slots.reshape(-1, TOP_K * d)
    return pl.pallas_call(
        _combine_kernel,
        grid=(t // tm,),
        in_specs=[pl.BlockSpec((tm, TOP_K * d), lambda i: (i, 0)),
                  pl.BlockSpec((tm, LANES), lambda i: (i, 0)),
                  pl.BlockSpec((tm, d), lambda i: (i, 0)),
                  pl.BlockSpec((1, 1, d), lambda i: (i // tiles_per_group, 0, 0))],
        out_specs=pl.BlockSpec((tm, d), lambda i: (i, 0)),
        out_shape=jax.ShapeDtypeStruct((t, d), F32),
        compiler_params=_cparams(("parallel",)),
        name="moe_combine",
    )(y4, gates, x, gate_mod)


def _assignment_tables(top_idx, rank, counts):
    t = top_idx.shape[0]
    n_assign = t * TOP_K
    padded = (counts + MOE_ROWS - 1) // MOE_ROWS * MOE_ROWS
    pad_start = jnp.cumsum(padded) - padded
    experts = jnp.arange(N_EXPERTS, dtype=jnp.int32)
    start_of = jnp.sum(jnp.where(top_idx[:, :, None] == experts, pad_start, 0), axis=-1)
    dest = (start_of + rank).reshape(-1).astype(jnp.int32)
    n_rows = (n_assign // MOE_ROWS + N_EXPERTS) * MOE_ROWS
    row_slot = jnp.full((n_rows,), -1, jnp.int32).at[dest].set(jnp.arange(n_assign, dtype=jnp.int32))
    row_tok = jnp.where(row_slot >= 0, row_slot // TOP_K, jnp.arange(n_rows, dtype=jnp.int32) % t)
    return row_tok, row_slot, (padded // MOE_ROWS).astype(jnp.int32), (pad_start // MOE_ROWS).astype(jnp.int32)


def moe_layer(x, p, shift, scale, gate_mod, tokens_per_group):
    h, idx, gates, rank, cnt = moe_router(x, p['norm2'], shift, scale, p['rw_pad'], p['rb_pad'], tokens_per_group)
    counts = cnt[0, :N_EXPERTS].astype(jnp.int32)
    row_tok, row_slot, n_blk, blk_start = _assignment_tables(idx[:, :TOP_K], rank[:, :TOP_K], counts)
    y_slots = moe_experts(h, n_blk, blk_start, row_tok, row_slot, p['w_gu'], p['b_gu'], p['w_down'], p['b_down'])
    return moe_combine(y_slots, gates, x, gate_mod, tokens_per_group)


def _pad_lanes(a, width):
    return jnp.pad(a, [(0, 0)] * (a.ndim - 1) + [(0, width - a.shape[-1])])


def _prep_common(p):
    p['rw_pad'] = _pad_lanes(p['router_w'], LANES)
    p['rb_pad'] = jnp.concatenate(
        [p['router_b'].astype(F32), jnp.full((LANES - N_EXPERTS,), NEG_BIG, F32)]).reshape(1, LANES)
    p['w_out'] = p['w_out'].astype(BF16)
    return p


def _prep_layer0(p):
    p = _prep_common(dict(p))
    p['w_in'] = _pad_lanes(p['w_in'], AB_IN_PAD).astype(BF16)
    q_b = p['q_b'].reshape(MLA_Q_LORA, MLA_HEADS, MLA_QK)
    p['q_b_pad'] = _pad_lanes(q_b, LANES).reshape(MLA_Q_LORA, MLA_HEADS * LANES).astype(BF16)
    kv_b = p['kv_b'].reshape(MLA_KV_LORA, MLA_HEADS, MLA_NOPE + MLA_V)
    p['wk_pad'] = _pad_lanes(kv_b[:, :, :MLA_NOPE], LANES).reshape(MLA_KV_LORA, MLA_HEADS * LANES).astype(BF16)
    p['wv'] = kv_b[:, :, MLA_NOPE:].reshape(MLA_KV_LORA, MLA_HEADS * MLA_V).astype(BF16)
    p['q_norm_pad'] = _pad_lanes(p['q_norm'].reshape(1, -1), LANES)
    p['k_norm_pad'] = _pad_lanes(p['k_norm'].reshape(1, -1), LANES)
    return p


def _prep_layer1(p):
    p = _prep_common(dict(p))
    p['w_in'] = p['w_in'].astype(BF16)
    p['q_norm2'] = jnp.tile(p['q_norm'].reshape(1, -1), (1, LANES // SWA_HD))
    p['k_norm2'] = jnp.tile(p['k_norm'].reshape(1, -1), (1, LANES // SWA_HD))
    return p


def _group_forward(x3, mods, p0, p1, lb_logits, caches, latent):
    batch, seq, d = x3.shape
    t = batch * seq
    x = x3.reshape(t, d)
    tpg = seq if latent else t
    sh1, sc1, g1, sh2, sc2, g2 = mods[0]

    z = modnorm_matmul(x, p0['norm1'], sh1, sc1, p0['w_in'], tpg)
    tab_b = _rope_lane_tables(seq, MLA_ROPE, LANES, MLA_NOPE) if latent else None
    s0f, s0b = (caches['hg_f'], caches['hg_b']) if latent else (None, None)
    o_f, o_b, s_f, s_b = hgrn_bidir(z, lb_logits, s0f, s0b, batch, seq)
    q = mla_queries(z, p0['qa_norm'], p0['q_b_pad'], p0['q_norm_pad'], tab_b, seq)
    ckv, k, v = mla_keys_values(z, 2816 // LANES, z, 2944 // LANES, p0['kva_norm'], p0['wk_pad'], p0['wv'],
                                p0['k_norm_pad'], tab_b, seq, norm_input=True)
    n_k = seq
    if latent:
        n_ctx = caches['ckv'].shape[1]
        ckv_c = caches['ckv'].reshape(batch * n_ctx, MLA_KV_LORA)
        kpe_c = _pad_lanes(caches['kpe'].reshape(batch * n_ctx, MLA_ROPE), LANES)
        _, k_c, v_c = mla_keys_values(ckv_c, 0, kpe_c, 0, p0['kva_norm'], p0['wk_pad'], p0['wv'],
                                      p0['k_norm_pad'], None, n_ctx, norm_input=False)
        cat = lambda a, b: jnp.concatenate([a.reshape(batch, n_ctx, -1), b.reshape(batch, seq, -1)],
                                           axis=1).reshape(batch * (n_ctx + seq), -1)
        k, v = cat(k_c, k), cat(v_c, v)
        n_k = n_ctx + seq
    o_mla = mla_attention(q, k, v, batch, seq, n_k)
    x = out_proj_layer0(o_f, o_b, z, p0['hg_out_norm'], o_mla, p0['w_out'], x, g1, tpg)
    x = moe_layer(x, p0, sh2, sc2, g2, tpg)
    state0 = (s_f, s_b, ckv.reshape(batch, seq, MLA_KV_LORA), z[:, 2944:2944 + MLA_ROPE].reshape(batch, seq, MLA_ROPE))

    sh1, sc1, g1, sh2, sc2, g2 = mods[1]
    z = modnorm_matmul(x, p1['norm1'], sh1, sc1, p1['w_in'], tpg)
    tab_c = _rope_lane_tables(seq, SWA_HD, SWA_HD, 0) if latent else None
    q, k_cache, k, v = swa_prep(z, p1['q_norm2'], p1['k_norm2'], tab_c, seq)
    sink = p1['sink'].astype(F32)
    if latent:
        n_ctx = caches['k1'].shape[1]
        k_c = caches['k1'].reshape(batch, n_ctx, -1).astype(BF16)
        v_c = caches['v1'].reshape(batch, n_ctx, -1).astype(BF16)
        a = swa_window(q, k, v, k_c, v_c, sink, batch, seq, n_ctx)
    else:
        a = swa_dense(q, k, v, sink, batch, seq)
    x = out_proj_layer1(a, p1['w_out'], x, g1, tpg)
    x = moe_layer(x, p1, sh2, sc2, g2, tpg)
    nkv = SWA_KV_HEADS * SWA_HD
    state1 = (k_cache.reshape(batch, seq, SWA_KV_HEADS, SWA_HD),
              z[:, SWA_HEADS * SWA_HD + nkv:].reshape(batch, seq, SWA_KV_HEADS, SWA_HD))
    return x.reshape(batch, seq, d), state0, state1


def kernel(x_prompt, x_sample, state_l0_hgrn_fwd, state_l0_hgrn_bwd, cache_l0_mla_ckv, cache_l0_mla_kpe, cache_l1_k, cache_l1_v, c, c_ctx, hgrn_lb_logits, l0_ada_w, l0_ada_b, l0_norm1, l0_norm2, l0_w_in, l0_hg_out_norm, l0_qa_norm, l0_q_b, l0_kva_norm, l0_kv_b, l0_q_norm, l0_k_norm, l0_w_out, l0_router_w, l0_router_b, l0_w_gu, l0_b_gu, l0_w_down, l0_b_down, l1_ada_w, l1_ada_b, l1_norm1, l1_norm2, l1_w_in, l1_q_norm, l1_k_norm, l1_sink, l1_w_out, l1_router_w, l1_router_b, l1_w_gu, l1_b_gu, l1_w_down, l1_b_down):
    p0 = _prep_layer0(dict(norm1=l0_norm1, norm2=l0_norm2, w_in=l0_w_in, hg_out_norm=l0_hg_out_norm,
                           qa_norm=l0_qa_norm, q_b=l0_q_b, kva_norm=l0_kva_norm, kv_b=l0_kv_b,
                           q_norm=l0_q_norm, k_norm=l0_k_norm, w_out=l0_w_out, router_w=l0_router_w,
                           router_b=l0_router_b, w_gu=l0_w_gu, b_gu=l0_b_gu, w_down=l0_w_down,
                           b_down=l0_b_down))
    p1 = _prep_layer1(dict(norm1=l1_norm1, norm2=l1_norm2, w_in=l1_w_in, q_norm=l1_q_norm, k_norm=l1_k_norm,
                           sink=l1_sink, w_out=l1_w_out, router_w=l1_router_w, router_b=l1_router_b,
                           w_gu=l1_w_gu, b_gu=l1_b_gu, w_down=l1_w_down, b_down=l1_b_down))
    dec_batch = c.shape[0]
    d = c.shape[1]
    cond8 = jnp.concatenate([c_ctx[None, :], c, jnp.zeros((8 - 1 - dec_batch, d), F32)], axis=0)
    mods_ctx, mods_lat = [], []
    for w, b in ((l0_ada_w, l0_ada_b), (l1_ada_w, l1_ada_b)):
        mod = ada_params(cond8, w, b)
        mods_ctx.append([m.reshape(1, 1, d) for m in jnp.split(mod[0:1], 6, axis=-1)])
        mods_lat.append([m.reshape(dec_batch, 1, d) for m in jnp.split(mod[1:1 + dec_batch], 6, axis=-1)])

    y_prompt, st0, st1 = _group_forward(x_prompt, mods_ctx, p0, p1, hgrn_lb_logits, None, latent=False)
    caches = dict(hg_f=state_l0_hgrn_fwd, hg_b=state_l0_hgrn_bwd, ckv=cache_l0_mla_ckv, kpe=cache_l0_mla_kpe,
                  k1=cache_l1_k, v1=cache_l1_v)
    y_sample, _, _ = _group_forward(x_sample, mods_lat, p0, p1, hgrn_lb_logits, caches, latent=True)
    return (y_prompt, y_sample, st0[0], st0[1], st0[2], st0[3], st1[0], st1[1])
```

```python
import functools

import numpy as np
import jax
import jax.numpy as jnp
from jax import lax
from jax.experimental import pallas as pl
from jax.experimental.pallas import tpu as pltpu

F32 = jnp.float32
BF16 = jnp.bfloat16

D_MODEL = 1024
GRID_W = 64
ROPE_THETA = 10000.0
EPS = 1e-6
HG_HEADS = 4
HG_DK = 128
HG_DV = 128
MLA_HEADS = 8
MLA_NOPE = 64
MLA_ROPE = 32
MLA_V = 64
MLA_QK = MLA_NOPE + MLA_ROPE
MLA_Q_LORA = 256
MLA_KV_LORA = 128
MLA_SCALE = MLA_QK ** -0.5
SWA_HEADS = 16
SWA_KV_HEADS = 4
SWA_HD = 64
SWA_WINDOW = 128
SWA_SCALE = SWA_HD ** -0.5
SWA_GROUP = SWA_HEADS // SWA_KV_HEADS
N_EXPERTS = 32
TOP_K = 4
D_FF = 1024
SWIGLU_LIMIT = 7.0
SWIGLU_ALPHA = 1.702

LANES = 128
HG_CHUNK = 128
HG_LEVELS = 7
HG_MXU_LEVELS = 3
AB_IN_PAD = 3072
MOE_ROWS = 256
NEG_BIG = -1e30
VMEM_LIMIT = 48 * 1024 * 1024
EXPERT_VMEM_LIMIT = 56 * 1024 * 1024


def _cparams(sem):
    return pltpu.CompilerParams(dimension_semantics=sem, vmem_limit_bytes=VMEM_LIMIT)


def _dot(a, b):
    return jnp.dot(a, b, preferred_element_type=F32)


def _dot_nt(a, b):
    return lax.dot_general(a, b, (((1,), (1,)), ((), ())), preferred_element_type=F32)


def _dot_tn(a, b):
    return lax.dot_general(a, b, (((0,), (0,)), ((), ())), preferred_element_type=F32)


def _split2(x):
    hi = x.astype(BF16)
    lo = (x - hi.astype(F32)).astype(BF16)
    return hi, lo


def _dot_hp(a, b):
    ah, al = _split2(a)
    bh, bl = _split2(b)
    return _dot(ah, bh) + _dot(ah, bl) + _dot(al, bh)


def _sigmoid(x):
    return 1.0 / (1.0 + jnp.exp(-x))


def _modnorm(x, gain, shift, scale):
    y = x * lax.rsqrt(jnp.mean(x * x, axis=-1, keepdims=True) + EPS)
    return y * gain * (1.0 + scale) + shift


def _ada_kernel(c_ref, w_ref, b_ref, o_ref):
    c = c_ref[...]
    o_ref[...] = _dot_hp(c * _sigmoid(c), w_ref[...]) + b_ref[...]


def ada_params(cond8, w, b):
    n = w.shape[1]
    tn = 1024
    return pl.pallas_call(
        _ada_kernel,
        grid=(n // tn,),
        in_specs=[pl.BlockSpec((8, D_MODEL), lambda j: (0, 0)),
                  pl.BlockSpec((D_MODEL, tn), lambda j: (0, j)),
                  pl.BlockSpec((1, tn), lambda j: (0, j))],
        out_specs=pl.BlockSpec((8, tn), lambda j: (0, j)),
        out_shape=jax.ShapeDtypeStruct((8, n), F32),
        compiler_params=_cparams(("parallel",)),
        name="ada_params",
    )(cond8, w, b.reshape(1, n))


def _modnorm_matmul_kernel(x_ref, g_ref, sh_ref, sc_ref, w_ref, o_ref):
    h = _modnorm(x_ref[...], g_ref[...], sh_ref[0], sc_ref[0])
    o_ref[...] = _dot(h.astype(BF16), w_ref[...])


def modnorm_matmul(x, gain, shift, scale, w_bf16, tokens_per_group, tm=256):
    t, d = x.shape
    n = w_bf16.shape[1]
    tiles_per_group = tokens_per_group // tm
    grp = lambda i: (i // tiles_per_group, 0, 0)
    return pl.pallas_call(
        _modnorm_matmul_kernel,
        grid=(t // tm,),
        in_specs=[pl.BlockSpec((tm, d), lambda i: (i, 0)),
                  pl.BlockSpec((1, d), lambda i: (0, 0)),
                  pl.BlockSpec((1, 1, d), grp),
                  pl.BlockSpec((1, 1, d), grp),
                  pl.BlockSpec((d, n), lambda i: (0, 0))],
        out_specs=pl.BlockSpec((tm, n), lambda i: (i, 0)),
        out_shape=jax.ShapeDtypeStruct((t, n), F32),
        compiler_params=_cparams(("parallel",)),
        name="modnorm_matmul",
    )(x, gain.reshape(1, d), shift, scale, w_bf16)


def _hgrn_constants():
    c = HG_CHUNK
    t = np.arange(c)[:, None]
    u = np.arange(c)[None, :]
    mats = [(u <= t), (u > t)]
    for l in range(HG_MXU_LEVELS):
        m = 1 << l
        r = (t // (2 * m)) * (2 * m) + m - 1
        mats.append((u > np.minimum(t, r)) & (u <= np.maximum(t, r)))
    fwd = np.concatenate(mats, axis=0).astype(np.float32)
    bwd = np.concatenate([mm[::-1, ::-1] for mm in mats], axis=0).astype(np.float32)
    x = np.bitwise_xor(t, u)
    lvl = np.where(x > 0, np.floor(np.log2(np.maximum(x, 1))), HG_LEVELS).astype(np.int32)
    lv_f = np.where(t >= u, lvl, -1).astype(np.int32)
    return fwd, bwd, lv_f, lv_f.T.copy()


def _hgrn_direction(qs, fpres, vs, lbs, mcat, lv, sts, forward):
    c = HG_CHUNK
    n = len(qs)
    fs = [lb + (1.0 - lb) * _sigmoid(fp) for lb, fp in zip(lbs, fpres)]
    kks = [1.0 - f for f in fs]
    parts = []
    for f in fs:
        parts += list(_split2(jnp.log(f)))
    x_all = _dot(mcat, jnp.concatenate(parts, axis=1))
    xs = [x_all[:, 2 * i * c:(2 * i + 1) * c] + x_all[:, (2 * i + 1) * c:(2 * i + 2) * c] for i in range(n)]
    gs = [x[0:c] for x in xs]
    qbs = [q.astype(BF16) for q in qs]
    kbs = [kk.astype(BF16) for kk in kks]
    vbs = [v.astype(BF16) for v in vs]
    os_ = [_dot_nt((q * jnp.exp(g)).astype(BF16), st.astype(BF16)) for q, g, st in zip(qs, gs, sts)]
    accs = [jnp.where(lv == HG_LEVELS, _dot_nt(qb, kb), 0.0) for qb, kb in zip(qbs, kbs)]
    for l in range(HG_LEVELS):
        for i in range(n):
            if l < HG_MXU_LEVELS:
                x = xs[i][(2 + l) * c:(3 + l) * c]
            else:
                m = 1 << l
                ref_rows = [j * 2 * m + (m - 1 if forward else m) for j in range(c // (2 * m))]
                g_ref = jnp.concatenate(
                    [jnp.broadcast_to(gs[i][r:r + 1, :], (2 * m, c)) for r in ref_rows], axis=0)
                x = -jnp.abs(gs[i] - g_ref)
            e = jnp.exp(x)
            p = _dot_nt((qs[i] * e).astype(BF16), (kks[i] * e).astype(BF16))
            accs[i] = jnp.where(lv == l, p, accs[i])
    edge_row = c - 1 if forward else 0
    outs, new_sts = [], []
    for i in range(n):
        outs.append(os_[i] + _dot(accs[i].astype(BF16), vbs[i]))
        k_end = (kks[i] * jnp.exp(xs[i][c:2 * c])).astype(BF16)
        new_sts.append(sts[i] * jnp.exp(gs[i][edge_row:edge_row + 1, :]) + _dot_tn(vbs[i], k_end))
    return outs, new_sts


def _hgrn_kernel(*refs, has_init):
    if has_init:
        (qf_ref, qb_ref, ff_ref, fb_ref, vf_ref, vb_ref, lbl_ref, mf_ref, mb_ref, lvf_ref, lvb_ref,
         s0f_ref, s0b_ref, of_ref, ob_ref, sf_ref, sb_ref, stf, stb) = refs
    else:
        (qf_ref, qb_ref, ff_ref, fb_ref, vf_ref, vb_ref, lbl_ref, mf_ref, mb_ref, lvf_ref, lvb_ref,
         of_ref, ob_ref, sf_ref, sb_ref, stf, stb) = refs
    c = pl.program_id(1)
    nc = pl.num_programs(1)

    @pl.when(c == 0)
    def _():
        for h in range(HG_HEADS):
            if has_init:
                stf[h] = s0f_ref[0, h].T
                stb[h] = s0b_ref[0, h].T
            else:
                stf[h] = jnp.zeros((HG_DV, HG_DK), F32)
                stb[h] = jnp.zeros((HG_DV, HG_DK), F32)

    rows = [lbl_ref[:, j, :] for j in range(lbl_ref.shape[1])]
    mx = functools.reduce(jnp.maximum, rows)
    ex = [jnp.exp(r - mx) for r in rows]
    lb = ex[0] / functools.reduce(lambda a, b: a + b, ex)

    heads = [slice(h * LANES, (h + 1) * LANES) for h in range(HG_HEADS)]
    o_f, st_f = _hgrn_direction([qf_ref[0, :, hs] for hs in heads], [ff_ref[0, :, hs] for hs in heads],
                                [vf_ref[0, :, hs] for hs in heads], [lb[0:1, hs] for hs in heads],
                                mf_ref[...], lvf_ref[...], [stf[h] for h in range(HG_HEADS)], True)
    o_b, st_b = _hgrn_direction([qb_ref[0, :, hs] for hs in heads], [fb_ref[0, :, hs] for hs in heads],
                                [vb_ref[0, :, hs] for hs in heads], [lb[1:2, hs] for hs in heads],
                                mb_ref[...], lvb_ref[...], [stb[h] for h in range(HG_HEADS)], False)
    for h, hs in enumerate(heads):
        of_ref[0, :, hs] = o_f[h]
        ob_ref[0, :, hs] = o_b[h]
        stf[h] = st_f[h]
        stb[h] = st_b[h]

    @pl.when(c == nc - 1)
    def _():
        for h in range(HG_HEADS):
            sf_ref[0, h] = stf[h].T
            sb_ref[0, h] = stb[h].T


def hgrn_bidir(z, lb_logits, s0f, s0b, batch, seq):
    nc = seq // HG_CHUNK
    z3 = z.reshape(batch, seq, z.shape[1])
    mf, mb, lvf, lvb = _hgrn_constants()
    has_init = s0f is not None
    width = HG_HEADS * LANES
    blk = (1, HG_CHUNK, width)
    fwd = lambda off: pl.BlockSpec(blk, lambda b, c: (b, c, off))
    bwd = lambda off: pl.BlockSpec(blk, lambda b, c: (b, nc - 1 - c, off))
    full = lambda a: pl.BlockSpec(a.shape, lambda b, c: (0,) * a.ndim)
    st_spec = pl.BlockSpec((1, HG_HEADS, HG_DK, HG_DV), lambda b, c: (b, 0, 0, 0))
    consts = [jnp.asarray(mf, BF16), jnp.asarray(mb, BF16), jnp.asarray(lvf), jnp.asarray(lvb)]
    in_specs = [fwd(0), bwd(0), fwd(1), bwd(2), fwd(3), bwd(3),
                pl.BlockSpec(lb_logits.shape, lambda b, c: (0, 0, 0))]
    in_specs += [full(a) for a in consts]
    args = [z3] * 6 + [lb_logits] + consts
    if has_init:
        in_specs += [st_spec, st_spec]
        args += [s0f, s0b]
    o_shape = jax.ShapeDtypeStruct((batch, seq, width), F32)
    s_shape = jax.ShapeDtypeStruct((batch, HG_HEADS, HG_DK, HG_DV), F32)
    o_f, o_b, s_f, s_b = pl.pallas_call(
        functools.partial(_hgrn_kernel, has_init=has_init),
        grid=(batch, nc),
        in_specs=in_specs,
        out_specs=[pl.BlockSpec(blk, lambda b, c: (b, c, 0)),
                   pl.BlockSpec(blk, lambda b, c: (b, nc - 1 - c, 0)),
                   st_spec, st_spec],
        out_shape=[o_shape, o_shape, s_shape, s_shape],
        scratch_shapes=[pltpu.VMEM((HG_HEADS, HG_DV, HG_DK), F32), pltpu.VMEM((HG_HEADS, HG_DV, HG_DK), F32)],
        compiler_params=_cparams(("parallel", "arbitrary")),
        name="hgrn_bidir",
    )(*args)
    t = batch * seq
    return o_f.reshape(t, -1), o_b.reshape(t, -1), s_f, s_b


def _axial_tables(n_tokens, n_rot):
    t = jnp.arange(n_tokens)
    row = (t // GRID_W).astype(F32)
    col = (t % GRID_W).astype(F32)
    n_freq = n_rot // 4
    inv = jnp.power(ROPE_THETA, -jnp.arange(n_freq, dtype=F32) / n_freq)
    ang = jnp.concatenate([row[:, None] * inv, col[:, None] * inv], axis=-1)
    return jnp.cos(ang), jnp.sin(ang)


def _rope_lane_tables(n_tokens, n_rot, head_width, first_rot_lane):
    cos, sin = _axial_tables(n_tokens, n_rot)
    half = n_rot // 2
    c_head = jnp.ones((n_tokens, head_width), F32)
    sa_head = jnp.zeros((n_tokens, head_width), F32)
    sb_head = jnp.zeros((n_tokens, head_width), F32)
    a0, a1, a2 = first_rot_lane, first_rot_lane + half, first_rot_lane + n_rot
    c_head = c_head.at[:, a0:a1].set(cos).at[:, a1:a2].set(cos)
    sa_head = sa_head.at[:, a0:a1].set(-sin)
    sb_head = sb_head.at[:, a1:a2].set(sin)
    reps = LANES // head_width
    tile = lambda a: jnp.tile(a, (1, reps))
    return tile(c_head), tile(sa_head), tile(sb_head)


def _rope(x, c, sa, sb, half):
    return x * c + pltpu.roll(x, LANES - half, 1) * sa + pltpu.roll(x, half, 1) * sb


def _mla_q_kernel(*refs, rope):
    if rope:
        qa_ref, qan_ref, qb_ref, qn_ref, c_ref, sa_ref, sb_ref, o_ref = refs
    else:
        qa_ref, qan_ref, qb_ref, qn_ref, o_ref = refs
    qa = qa_ref[...]
    qn = qa * lax.rsqrt(jnp.mean(qa * qa, axis=-1, keepdims=True) + EPS) * qan_ref[...]
    qfull = _dot(qn.astype(BF16), qb_ref[...])
    outs = []
    for h in range(MLA_HEADS):
        qh = qfull[:, h * LANES:(h + 1) * LANES]
        ms = jnp.sum(qh * qh, axis=-1, keepdims=True) * (1.0 / MLA_QK)
        qh = qh * lax.rsqrt(ms + EPS) * qn_ref[...]
        if rope:
            qh = _rope(qh, c_ref[...], sa_ref[...], sb_ref[...], MLA_ROPE // 2)
        outs.append((qh * MLA_SCALE).astype(BF16))
    o_ref[...] = jnp.concatenate(outs, axis=1)


def mla_queries(z, qa_norm, q_b_pad, q_norm_pad, tables, tokens_per_batch, tm=256):
    t = z.shape[0]
    rope = tables is not None
    row = lambda i: (0, 0)
    in_specs = [pl.BlockSpec((tm, MLA_Q_LORA), lambda i: (i, 2560 // MLA_Q_LORA)),
                pl.BlockSpec((1, MLA_Q_LORA), row),
                pl.BlockSpec(q_b_pad.shape, row),
                pl.BlockSpec((1, LANES), row)]
    args = [z, qa_norm.reshape(1, -1), q_b_pad, q_norm_pad]
    if rope:
        tpb = tokens_per_batch // tm
        in_specs += [pl.BlockSpec((tm, LANES), lambda i: (i % tpb, 0))] * 3
        args += list(tables)
    return pl.pallas_call(
        functools.partial(_mla_q_kernel, rope=rope),
        grid=(t // tm,),
        in_specs=in_specs,
        out_specs=pl.BlockSpec((tm, MLA_HEADS * LANES), lambda i: (i, 0)),
        out_shape=jax.ShapeDtypeStruct((t, MLA_HEADS * LANES), BF16),
        compiler_params=_cparams(("parallel",)),
        name="mla_queries",
    )(*args)


def _mla_kv_kernel(*refs, norm_input, rope):
    if rope:
        kva_ref, kpe_ref, kvan_ref, wk_ref, wv_ref, kn_ref, c_ref, sa_ref, sb_ref, ckv_ref, k_ref, v_ref = refs
    else:
        kva_ref, kpe_ref, kvan_ref, wk_ref, wv_ref, kn_ref, ckv_ref, k_ref, v_ref = refs
    ckv = kva_ref[...]
    if norm_input:
        ckv = ckv * lax.rsqrt(jnp.mean(ckv * ckv, axis=-1, keepdims=True) + EPS) * kvan_ref[...]
    ckv_ref[...] = ckv
    cb = ckv.astype(BF16)
    knope = _dot(cb, wk_ref[...])
    v_ref[...] = _dot(cb, wv_ref[...]).astype(BF16)
    kpe = pltpu.roll(kpe_ref[...], MLA_NOPE, 1)
    outs = []
    for h in range(MLA_HEADS):
        kh = knope[:, h * LANES:(h + 1) * LANES] + kpe
        ms = jnp.sum(kh * kh, axis=-1, keepdims=True) * (1.0 / MLA_QK)
        kh = kh * lax.rsqrt(ms + EPS) * kn_ref[...]
        if rope:
            kh = _rope(kh, c_ref[...], sa_ref[...], sb_ref[...], MLA_ROPE // 2)
        outs.append(kh.astype(BF16))
    k_ref[...] = jnp.concatenate(outs, axis=1)


def mla_keys_values(kva_src, kva_col, kpe_src, kpe_col, kva_norm, wk_pad, wv, k_norm_pad, tables,
                    tokens_per_batch, norm_input, tm=256):
    t = kva_src.shape[0]
    rope = tables is not None
    row = lambda i: (0, 0)
    in_specs = [pl.BlockSpec((tm, LANES), lambda i: (i, kva_col)),
                pl.BlockSpec((tm, LANES), lambda i: (i, kpe_col)),
                pl.BlockSpec((1, LANES), row),
                pl.BlockSpec(wk_pad.shape, row),
                pl.BlockSpec(wv.shape, row),
                pl.BlockSpec((1, LANES), row)]
    args = [kva_src, kpe_src, kva_norm.reshape(1, -1), wk_pad, wv, k_norm_pad]
    if rope:
        tpb = tokens_per_batch // tm
        in_specs += [pl.BlockSpec((tm, LANES), lambda i: (i % tpb, 0))] * 3
        args += list(tables)
    return pl.pallas_call(
        functools.partial(_mla_kv_kernel, norm_input=norm_input, rope=rope),
        grid=(t // tm,),
        in_specs=in_specs,
        out_specs=[pl.BlockSpec((tm, LANES), lambda i: (i, 0)),
                   pl.BlockSpec((tm, MLA_HEADS * LANES), lambda i: (i, 0)),
                   pl.BlockSpec((tm, MLA_HEADS * MLA_V), lambda i: (i, 0))],
        out_shape=[jax.ShapeDtypeStruct((t, LANES), F32),
                   jax.ShapeDtypeStruct((t, MLA_HEADS * LANES), BF16),
                   jax.ShapeDtypeStruct((t, MLA_HEADS * MLA_V), BF16)],
        compiler_params=_cparams(("parallel",)),
        name="mla_keys_values",
    )(*args)


def _mla_attn_kernel(q_ref, k_ref, v_ref, o_ref):
    outs = []
    for j in range(2):
        q = q_ref[0][:, j * LANES:(j + 1) * LANES]
        k = k_ref[0][:, j * LANES:(j + 1) * LANES]
        v = v_ref[0][:, j * MLA_V:(j + 1) * MLA_V]
        s = _dot_nt(q, k)
        p = jnp.exp(s - jnp.max(s, axis=-1, keepdims=True))
        l = jnp.sum(p, axis=-1, keepdims=True)
        outs.append(_dot(p.astype(BF16), v) / l)
    o_ref[0] = jnp.concatenate(outs, axis=1)


def mla_attention(q, k, v, batch, n_q, n_k, tq=256):
    q3 = q.reshape(batch, n_q, -1)
    k3 = k.reshape(batch, n_k, -1)
    v3 = v.reshape(batch, n_k, -1)
    out = pl.pallas_call(
        _mla_attn_kernel,
        grid=(batch, MLA_HEADS // 2, n_q // tq),
        in_specs=[pl.BlockSpec((1, tq, 2 * LANES), lambda b, j, i: (b, i, j)),
                  pl.BlockSpec((1, n_k, 2 * LANES), lambda b, j, i: (b, 0, j)),
                  pl.BlockSpec((1, n_k, 2 * MLA_V), lambda b, j, i: (b, 0, j))],
        out_specs=pl.BlockSpec((1, tq, 2 * MLA_V), lambda b, j, i: (b, i, j)),
        out_shape=jax.ShapeDtypeStruct((batch, n_q, MLA_HEADS * MLA_V), F32),
        compiler_params=_cparams(("parallel", "parallel", "arbitrary")),
        name="mla_attention",
    )(q3, k3, v3)
    return out.reshape(batch * n_q, -1)


def _out0_kernel(of_ref, ob_ref, ag_ref, hgn_ref, om_ref, w_ref, x_ref, g_ref, o_ref):
    o = of_ref[...] + ob_ref[...]
    ag = ag_ref[...]
    parts = []
    for h in range(HG_HEADS):
        oh = o[:, h * HG_DV:(h + 1) * HG_DV]
        oh = oh * lax.rsqrt(jnp.mean(oh * oh, axis=-1, keepdims=True) + EPS) * hgn_ref[...]
        gh = ag[:, h * HG_DV:(h + 1) * HG_DV]
        parts.append((oh * (gh * _sigmoid(gh))).astype(BF16))
    oa = jnp.concatenate(parts, axis=1)
    n_a = HG_HEADS * HG_DV
    mix = _dot(oa, w_ref[0:n_a, :]) + _dot(om_ref[...].astype(BF16), w_ref[n_a:, :])
    o_ref[...] = x_ref[...] + g_ref[0] * mix


def out_proj_layer0(o_f, o_b, z, hg_norm, o_mla, w_out_bf16, x, gate, tokens_per_group, tm=256):
    t, d = x.shape
    n_a = HG_HEADS * HG_DV
    tiles_per_group = tokens_per_group // tm
    tile = lambda w: pl.BlockSpec((tm, w), lambda i: (i, 0))
    return pl.pallas_call(
        _out0_kernel,
        grid=(t // tm,),
        in_specs=[tile(n_a), tile(n_a),
                  pl.BlockSpec((tm, n_a), lambda i: (i, 2048 // n_a)),
                  pl.BlockSpec((1, HG_DV), lambda i: (0, 0)),
                  tile(o_mla.shape[1]),
                  pl.BlockSpec(w_out_bf16.shape, lambda i: (0, 0)),
                  tile(d),
                  pl.BlockSpec((1, 1, d), lambda i: (i // tiles_per_group, 0, 0))],
        out_specs=tile(d),
        out_shape=jax.ShapeDtypeStruct((t, d), F32),
        compiler_params=_cparams(("parallel",)),
        name="out_proj_layer0",
    )(o_f, o_b, z, hg_norm.reshape(1, -1), o_mla, w_out_bf16, x, gate)


def _out1_kernel(a_ref, w_ref, x_ref, g_ref, o_ref):
    o_ref[...] = x_ref[...] + g_ref[0] * _dot(a_ref[...].astype(BF16), w_ref[...])


def out_proj_layer1(a, w_out_bf16, x, gate, tokens_per_group, tm=256):
    t, d = x.shape
    tiles_per_group = tokens_per_group // tm
    tile = lambda w: pl.BlockSpec((tm, w), lambda i: (i, 0))
    return pl.pallas_call(
        _out1_kernel,
        grid=(t // tm,),
        in_specs=[tile(a.shape[1]),
                  pl.BlockSpec(w_out_bf16.shape, lambda i: (0, 0)),
                  tile(d),
                  pl.BlockSpec((1, 1, d), lambda i: (i // tiles_per_group, 0, 0))],
        out_specs=tile(d),
        out_shape=jax.ShapeDtypeStruct((t, d), F32),
        compiler_params=_cparams(("parallel",)),
        name="out_proj_layer1",
    )(a, w_out_bf16, x, gate)


def _head_rms(x, gain2):
    sq = x * x
    lane = lax.broadcasted_iota(jnp.int32, x.shape, 1)
    first = lane < SWA_HD
    lo = jnp.sum(jnp.where(first, sq, 0.0), axis=-1, keepdims=True)
    hi = jnp.sum(jnp.where(first, 0.0, sq), axis=-1, keepdims=True)
    ms = jnp.where(first, lo, hi) * (1.0 / SWA_HD)
    return x * lax.rsqrt(ms + EPS) * gain2


def _swa_prep_kernel(*refs, rope):
    if rope:
        zq_ref, zk_ref, zv_ref, qn_ref, kn_ref, c_ref, sa_ref, sb_ref, q_ref, kc_ref, k_ref, v_ref = refs
    else:
        zq_ref, zk_ref, zv_ref, qn_ref, kn_ref, q_ref, kc_ref, k_ref, v_ref = refs
    half = SWA_HD // 2

    def rot(x):
        return _rope(x, c_ref[...], sa_ref[...], sb_ref[...], half) if rope else x

    zq = zq_ref[...]
    qs = []
    for p in range(zq.shape[1] // LANES):
        x = _head_rms(zq[:, p * LANES:(p + 1) * LANES], qn_ref[...])
        qs.append((rot(x) * SWA_SCALE).astype(BF16))
    q_ref[...] = jnp.concatenate(qs, axis=1)
    zk = zk_ref[...]
    kn, kr = [], []
    for p in range(zk.shape[1] // LANES):
        x = _head_rms(zk[:, p * LANES:(p + 1) * LANES], kn_ref[...])
        kn.append(x)
        kr.append(rot(x).astype(BF16))
    kc_ref[...] = jnp.concatenate(kn, axis=1)
    k_ref[...] = jnp.concatenate(kr, axis=1)
    v_ref[...] = zv_ref[...].astype(BF16)


def swa_prep(z, q_norm2, k_norm2, tables, tokens_per_batch, tm=256):
    t = z.shape[0]
    nq = SWA_HEADS * SWA_HD
    nkv = SWA_KV_HEADS * SWA_HD
    rope = tables is not None
    row = lambda i: (0, 0)
    in_specs = [pl.BlockSpec((tm, nq), lambda i: (i, 0)),
                pl.BlockSpec((tm, nkv), lambda i: (i, nq // nkv)),
                pl.BlockSpec((tm, nkv), lambda i: (i, nq // nkv + 1)),
                pl.BlockSpec((1, LANES), row),
                pl.BlockSpec((1, LANES), row)]
    args = [z, z, z, q_norm2, k_norm2]
    if rope:
        tpb = tokens_per_batch // tm
        in_specs += [pl.BlockSpec((tm, LANES), lambda i: (i % tpb, 0))] * 3
        args += list(tables)
    tile = lambda w: pl.BlockSpec((tm, w), lambda i: (i, 0))
    return pl.pallas_call(
        functools.partial(_swa_prep_kernel, rope=rope),
        grid=(t // tm,),
        in_specs=in_specs,
        out_specs=[tile(nq), tile(nkv), tile(nkv), tile(nkv)],
        out_shape=[jax.ShapeDtypeStruct((t, nq), BF16),
                   jax.ShapeDtypeStruct((t, nkv), F32),
                   jax.ShapeDtypeStruct((t, nkv), BF16),
                   jax.ShapeDtypeStruct((t, nkv), BF16)],
        compiler_params=_cparams(("parallel",)),
        name="swa_prep",
    )(*args)


def _sink_softmax_pv(s, sk, vn):
    m = jnp.maximum(jnp.max(s, axis=-1, keepdims=True), sk)
    p = jnp.exp(s - m)
    l = jnp.sum(p, axis=-1, keepdims=True) + jnp.exp(sk - m)
    return _dot(p.astype(BF16), vn) / l


def _sink_attention(q8, k_all, v_all, bias, sink_ref, pair, stack):
    outs = []
    tq = q8.shape[0]
    for n in range(2):
        kn = k_all[:, n * SWA_HD:(n + 1) * SWA_HD]
        vn = v_all[:, n * SWA_HD:(n + 1) * SWA_HD]
        heads = [n * SWA_GROUP + g for g in range(SWA_GROUP)]
        qs = [q8[:, hq * SWA_HD:(hq + 1) * SWA_HD] for hq in heads]
        sinks = [sink_ref[pair * 2 * SWA_GROUP + hq] for hq in heads]
        if stack:
            sk = jnp.concatenate([jnp.full((tq, 1), s, F32) for s in sinks], axis=0)
            s = _dot_nt(jnp.concatenate(qs, axis=0), kn)
            if bias is not None:
                s = s + bias
            o4 = _sink_softmax_pv(s, sk, vn)
            outs += [o4[g * tq:(g + 1) * tq] for g in range(SWA_GROUP)]
        else:
            for q, sk in zip(qs, sinks):
                s = _dot_nt(q, kn)
                if bias is not None:
                    s = s + bias
                outs.append(_sink_softmax_pv(s, sk, vn))
    return jnp.concatenate(outs, axis=1)


def _swa_dense_kernel(sink_ref, q_ref, k_ref, v_ref, o_ref):
    o_ref[0] = _sink_attention(q_ref[0], k_ref[0], v_ref[0], None, sink_ref, pl.program_id(1), stack=False)


def swa_dense(q, k, v, sink, batch, seq):
    nq = SWA_HEADS * SWA_HD
    q3 = q.reshape(batch, seq, nq)
    k3 = k.reshape(batch, seq, -1)
    v3 = v.reshape(batch, seq, -1)
    out = pl.pallas_call(
        _swa_dense_kernel,
        grid=(batch, 2),
        in_specs=[pl.BlockSpec(memory_space=pltpu.SMEM),
                  pl.BlockSpec((1, seq, nq // 2), lambda b, j: (b, 0, j)),
                  pl.BlockSpec((1, seq, LANES), lambda b, j: (b, 0, j)),
                  pl.BlockSpec((1, seq, LANES), lambda b, j: (b, 0, j))],
        out_specs=pl.BlockSpec((1, seq, nq // 2), lambda b, j: (b, 0, j)),
        out_shape=jax.ShapeDtypeStruct((batch, seq, nq), F32),
        compiler_params=_cparams(("parallel", "parallel")),
        name="swa_dense",
    )(sink, q3, k3, v3)
    return out.reshape(batch * seq, nq)


def _swa_window_kernel(sink_ref, q_ref, kc_ref, vc_ref, kp_ref, k0_ref, kn_ref, vp_ref, v0_ref, vn_ref, o_ref):
    i = pl.program_id(2)
    nb = pl.num_programs(2)
    w = SWA_WINDOW
    n_ctx = kc_ref.shape[1]
    k_all = jnp.concatenate([kc_ref[0], kp_ref[0], k0_ref[0], kn_ref[0]], axis=0)
    v_all = jnp.concatenate([vc_ref[0], vp_ref[0], v0_ref[0], vn_ref[0]], axis=0)
    shape = (SWA_GROUP * w, n_ctx + 3 * w)
    r = lax.broadcasted_iota(jnp.int32, shape, 0) % w
    col = lax.broadcasted_iota(jnp.int32, shape, 1) - n_ctx
    prev_bias = jnp.where(i > 0, 0.0, NEG_BIG)
    next_bias = jnp.where(i < nb - 1, 0.0, NEG_BIG)
    bias = jnp.where(
        col < w,
        jnp.where(col < 0, 0.0, jnp.where(col >= r, prev_bias, NEG_BIG)),
        jnp.where(col < 2 * w, 0.0, jnp.where(col - 2 * w <= r, next_bias, NEG_BIG)))
    o_ref[0] = _sink_attention(q_ref[0], k_all, v_all, bias, sink_ref, pl.program_id(1), stack=True)


def swa_window(q, k, v, k_ctx, v_ctx, sink, batch, seq, n_ctx):
    nq = SWA_HEADS * SWA_HD
    w = SWA_WINDOW
    nb = seq // w
    q3 = q.reshape(batch, seq, nq)
    k3 = k.reshape(batch, seq, -1)
    v3 = v.reshape(batch, seq, -1)
    ctx = pl.BlockSpec((1, n_ctx, LANES), lambda b, j, i: (b, 0, j))
    prv = pl.BlockSpec((1, w, LANES), lambda b, j, i: (b, jnp.maximum(i - 1, 0), j))
    cur = pl.BlockSpec((1, w, LANES), lambda b, j, i: (b, i, j))
    nxt = pl.BlockSpec((1, w, LANES), lambda b, j, i: (b, jnp.minimum(i + 1, nb - 1), j))
    out = pl.pallas_call(
        _swa_window_kernel,
        grid=(batch, 2, nb),
        in_specs=[pl.BlockSpec(memory_space=pltpu.SMEM),
                  pl.BlockSpec((1, w, nq // 2), lambda b, j, i: (b, i, j)),
                  ctx, ctx, prv, cur, nxt, prv, cur, nxt],
        out_specs=pl.BlockSpec((1, w, nq // 2), lambda b, j, i: (b, i, j)),
        out_shape=jax.ShapeDtypeStruct((batch, seq, nq), F32),
        compiler_params=_cparams(("parallel", "parallel", "arbitrary")),
        name="swa_window",
    )(sink, q3, k_ctx, v_ctx, k3, k3, k3, v3, v3, v3)
    return out.reshape(batch * seq, nq)


def _router_kernel(x_ref, g_ref, sh_ref, sc_ref, rw_ref, rb_ref, tri_ref, h_ref, idx_ref, gate_ref, rank_ref,
                   cnt_ref, cnt, *, tiles_per_part):
    @pl.when(pl.program_id(0) % tiles_per_part == 0)
    def _():
        cnt[...] = jnp.zeros_like(cnt)

    h = _modnorm(x_ref[...], g_ref[...], sh_ref[0], sc_ref[0])
    bits = lax.bitcast_convert_type(h.astype(BF16).astype(F32), jnp.uint32)
    half = h.shape[1] // 2
    h_ref[...] = (bits[:, :half] >> 16) | (bits[:, half:] & jnp.uint32(0xFFFF0000))
    logits = _dot_hp(h, rw_ref[...]) + rb_ref[...]
    lane = lax.broadcasted_iota(jnp.int32, logits.shape, 1)
    work = logits
    vals, idxs = [], []
    for _ in range(TOP_K):
        m = jnp.max(work, axis=-1, keepdims=True)
        ik = jnp.min(jnp.where(work == m, lane, LANES), axis=-1, keepdims=True)
        vals.append(m)
        idxs.append(ik)
        work = jnp.where(lane == ik, 2.0 * NEG_BIG, work)
    es = [jnp.exp(v - vals[0]) for v in vals]
    denom = es[0] + es[1] + es[2] + es[3]
    idx_out = jnp.zeros(logits.shape, jnp.int32)
    gate_out = jnp.zeros(logits.shape, F32)
    for k in range(TOP_K):
        idx_out = jnp.where(lane == k, idxs[k], idx_out)
        gate_out = jnp.where(lane == k, es[k] / denom, gate_out)
    idx_ref[...] = idx_out
    gate_ref[...] = gate_out
    chosen = jnp.zeros(logits.shape, F32)
    for k in range(TOP_K):
        chosen = jnp.where(lane == idxs[k], 1.0, chosen)
    before = _dot(tri_ref[...], chosen.astype(BF16)) + cnt[...]
    rank_out = jnp.zeros(logits.shape, jnp.int32)
    for k in range(TOP_K):
        rk = jnp.sum(jnp.where(lane == idxs[k], before, 0.0), axis=-1, keepdims=True)
        rank_out = jnp.where(lane == k, rk.astype(jnp.int32), rank_out)
    rank_ref[...] = rank_out
    cnt[...] = cnt[...] + jnp.sum(chosen, axis=0, keepdims=True)
    cnt_ref[0] = cnt[...]


def moe_router(x, gain, shift, scale, rw_pad, rb_pad, tokens_per_group, tm=256):
    t, d = x.shape
    tiles_per_group = tokens_per_group // tm
    tiles_per_part = MOE_PART // tm
    grp = lambda i: (i // tiles_per_group, 0, 0)
    tile = lambda w: pl.BlockSpec((tm, w), lambda i: (i, 0))
    tri = jnp.asarray(np.tril(np.ones((tm, tm), np.float32), -1), BF16)
    return pl.pallas_call(
        functools.partial(_router_kernel, tiles_per_part=tiles_per_part),
        grid=(t // tm,),
        in_specs=[tile(d),
                  pl.BlockSpec((1, d), lambda i: (0, 0)),
                  pl.BlockSpec((1, 1, d), grp),
                  pl.BlockSpec((1, 1, d), grp),
                  pl.BlockSpec((d, LANES), lambda i: (0, 0)),
                  pl.BlockSpec((1, LANES), lambda i: (0, 0)),
                  pl.BlockSpec((tm, tm), lambda i: (0, 0))],
        out_specs=[tile(d // 2), tile(LANES), tile(LANES), tile(LANES),
                   pl.BlockSpec((1, 1, LANES), lambda i: (i // tiles_per_part, 0, 0))],
        out_shape=[jax.ShapeDtypeStruct((t, d // 2), jnp.uint32),
                   jax.ShapeDtypeStruct((t, LANES), jnp.int32),
                   jax.ShapeDtypeStruct((t, LANES), F32),
                   jax.ShapeDtypeStruct((t, LANES), jnp.int32),
                   jax.ShapeDtypeStruct((t // MOE_PART, 1, LANES), F32)],
        scratch_shapes=[pltpu.VMEM((1, LANES), F32)],
        compiler_params=_cparams(("arbitrary",)),
        name="moe_router",
    )(x, gain.reshape(1, d), shift, scale, rw_pad, rb_pad, tri)


FFN_COLS = 256
FFN_PIECES = 2 * D_FF // FFN_COLS + D_MODEL // FFN_COLS
MOE_PART = 4096


def _unpack_rows(words):
    lo = lax.bitcast_convert_type(words << 16, F32)
    hi = lax.bitcast_convert_type(words & jnp.uint32(0xFFFF0000), F32)
    return jnp.concatenate([lo, hi], axis=1).astype(BF16)


def _ffn_block(x_ref, y_ref, wgu_ref, bgu_ref, wd_ref, bd_ref, between):
    xb = _unpack_rows(x_ref[...])
    acts = []
    for c in range(D_FF // FFN_COLS):
        lo, hi = c * FFN_COLS, (c + 1) * FFN_COLS
        gate = _dot(xb, wgu_ref[0, :, lo:hi]) + bgu_ref[0][:, lo:hi]
        between(2 * c)
        up = _dot(xb, wgu_ref[0, :, D_FF + lo:D_FF + hi]) + bgu_ref[0][:, D_FF + lo:D_FF + hi]
        between(2 * c + 1)
        gate = jnp.minimum(gate, SWIGLU_LIMIT)
        up = jnp.clip(up, -SWIGLU_LIMIT, SWIGLU_LIMIT)
        acts.append((gate * _sigmoid(SWIGLU_ALPHA * gate) * (up + 1.0)).astype(BF16))
    act = jnp.concatenate(acts, axis=1)
    for n in range(D_MODEL // FFN_COLS):
        lo, hi = n * FFN_COLS, (n + 1) * FFN_COLS
        y_ref[:, lo:hi] = _dot(act, wd_ref[0, :, lo:hi]) + bd_ref[0][:, lo:hi]
        between(2 * D_FF // FFN_COLS + n)


def _expert_kernel(nb_ref, bs_ref, slot_ref, gate_ref, hp_hbm, wgu_ref, bgu_ref, wd_ref, bd_ref, out_hbm,
                   hbuf, acc, xa, xb, ya, yb, sem):
    part = pl.program_id(0)
    e = pl.program_id(1)
    idx = part * N_EXPERTS + e
    nb = nb_ref[idx]
    b0 = bs_ref[idx]
    shares = np.array_split(np.arange(MOE_ROWS), FFN_PIECES)

    @pl.when(e == 0)
    def _():
        load = pltpu.make_async_copy(hp_hbm.at[pl.ds(part * MOE_PART, MOE_PART), :], hbuf, sem.at[0])
        load.start()
        acc[...] = jnp.zeros_like(acc)
        ya[...] = jnp.zeros_like(ya)
        yb[...] = jnp.zeros_like(yb)
        load.wait()

    def local_token(slot):
        return lax.shift_right_logical(slot, TOP_K.bit_length() - 1) & (MOE_PART - 1)

    def fetch_rows(dst, blk):
        base = blk * MOE_ROWS

        def emit(k):
            for r in shares[k]:
                r = int(r)
                dst[r:r + 1, :] = hbuf[pl.ds(local_token(jnp.maximum(slot_ref[base + r], 0)), 1), :]
        return emit

    def add_rows(src, blk, valid=None):
        base = jnp.maximum(blk, 0) * MOE_ROWS

        def emit(k):
            for r in shares[k]:
                r = int(r)
                slot = slot_ref[base + r]
                ok = slot >= 0 if valid is None else jnp.logical_and(valid, slot >= 0)
                slot = jnp.maximum(slot, 0)
                row = jnp.where(ok, local_token(slot), MOE_PART)
                g = jnp.where(ok, gate_ref[slot], 0.0)
                acc[pl.ds(row, 1), :] = acc[pl.ds(row, 1), :] + g * src[r:r + 1, :]
        return emit

    def ffn(x_ref, y_ref, *emitters):
        def between(k):
            for f in emitters:
                f(k)
        _ffn_block(x_ref, y_ref, wgu_ref, bgu_ref, wd_ref, bd_ref, between)

    def emit_all(f):
        for k in range(FFN_PIECES):
            f(k)

    @pl.when(nb > 0)
    def _():
        emit_all(fetch_rows(xa, b0))
        n_pairs = nb // 2
        last = b0 + nb - 1

        def pair(jj, carry):
            j = b0 + 2 * jj
            ffn(xa, ya, fetch_rows(xb, j + 1), add_rows(yb, j - 1, jj > 0))
            ffn(xb, yb, fetch_rows(xa, jnp.minimum(j + 2, last)), add_rows(ya, j))
            return carry

        lax.fori_loop(0, n_pairs, pair, 0)

        @pl.when(nb % 2 == 1)
        def _():
            ffn(xa, ya, add_rows(yb, last - 1, n_pairs > 0))
            emit_all(add_rows(ya, last))

        @pl.when(nb % 2 == 0)
        def _():
            emit_all(add_rows(yb, last))

    @pl.when(e == N_EXPERTS - 1)
    def _():
        store = pltpu.make_async_copy(acc.at[pl.ds(0, MOE_PART), :],
                                      out_hbm.at[pl.ds(part * MOE_PART, MOE_PART), :], sem.at[0])
        store.start()
        store.wait()


def moe_experts(hp, n_blk, blk_start, row_slot, gates_flat, w_gu, b_gu, w_down, b_down):
    t = hp.shape[0]
    d = D_MODEL
    per_expert = lambda p, e, nb, bs, sl: (e, 0, 0)
    grid_spec = pltpu.PrefetchScalarGridSpec(
        num_scalar_prefetch=3,
        grid=(t // MOE_PART, N_EXPERTS),
        in_specs=[pl.BlockSpec(memory_space=pltpu.SMEM),
                  pl.BlockSpec(memory_space=pl.ANY),
                  pl.BlockSpec((1, d, 2 * D_FF), per_expert),
                  pl.BlockSpec((1, 1, 2 * D_FF), per_expert),
                  pl.BlockSpec((1, D_FF, d), per_expert),
                  pl.BlockSpec((1, 1, d), per_expert)],
        out_specs=pl.BlockSpec(memory_space=pl.ANY),
        scratch_shapes=[pltpu.VMEM((MOE_PART, d // 2), jnp.uint32),
                        pltpu.VMEM((MOE_PART + 8, d), F32),
                        pltpu.VMEM((MOE_ROWS, d // 2), jnp.uint32), pltpu.VMEM((MOE_ROWS, d // 2), jnp.uint32),
                        pltpu.VMEM((MOE_ROWS, d), F32), pltpu.VMEM((MOE_ROWS, d), F32),
                        pltpu.SemaphoreType.DMA((1,))],
    )
    return pl.pallas_call(
        _expert_kernel,
        grid_spec=grid_spec,
        out_shape=jax.ShapeDtypeStruct((t, d), F32),
        compiler_params=pltpu.CompilerParams(dimension_semantics=("arbitrary", "arbitrary"),
                                             vmem_limit_bytes=EXPERT_VMEM_LIMIT),
        name="moe_experts",
    )(n_blk, blk_start, row_slot, gates_flat, hp, w_gu, b_gu.reshape(N_EXPERTS, 1, -1), w_down,
      b_down.reshape(N_EXPERTS, 1, -1))


def _residual_kernel(y_ref, x_ref, g_ref, o_ref):
    o_ref[...] = x_ref[...] + g_ref[0] * y_ref[...]


def gated_residual(y, x, gate_mod, tokens_per_group, tm=512):
    t, d = x.shape
    tiles_per_group = tokens_per_group // tm
    tile = pl.BlockSpec((tm, d), lambda i: (i, 0))
    return pl.pallas_call(
        _residual_kernel,
        grid=(t // tm,),
        in_specs=[tile, tile, pl.BlockSpec((1, 1, d), lambda i: (i // tiles_per_group, 0, 0))],
        out_specs=tile,
        out_shape=jax.ShapeDtypeStruct((t, d), F32),
        compiler_params=_cparams(("parallel",)),
        name="moe_residual",
    )(y, x, gate_mod)


def _assignment_tables(top_idx, rank, counts):
    t = top_idx.shape[0]
    n_parts = t // MOE_PART
    n_assign = t * TOP_K
    padded = ((counts + MOE_ROWS - 1) // MOE_ROWS * MOE_ROWS).reshape(-1)
    pad_start = (jnp.cumsum(padded) - padded).reshape(n_parts, N_EXPERTS)
    experts = jnp.arange(N_EXPERTS, dtype=jnp.int32)
    by_part = top_idx.reshape(n_parts, MOE_PART, TOP_K)
    start_of = jnp.sum(jnp.where(by_part[..., None] == experts, pad_start[:, None, None, :], 0), axis=-1)
    dest = (start_of.reshape(t, TOP_K) + rank).reshape(-1).astype(jnp.int32)
    n_rows = (n_assign // MOE_ROWS + n_parts * N_EXPERTS) * MOE_ROWS
    row_slot = jnp.full((n_rows,), -1, jnp.int32).at[dest].set(jnp.arange(n_assign, dtype=jnp.int32))
    return row_slot, (padded // MOE_ROWS).astype(jnp.int32), (pad_start.reshape(-1) // MOE_ROWS).astype(jnp.int32)


def moe_layer(x, p, shift, scale, gate_mod, tokens_per_group):
    hp, idx, gates, rank, cnt = moe_router(x, p['norm2'], shift, scale, p['rw_pad'], p['rb_pad'], tokens_per_group)
    counts = cnt[:, 0, :N_EXPERTS].astype(jnp.int32)
    row_slot, n_blk, blk_start = _assignment_tables(idx[:, :TOP_K], rank[:, :TOP_K], counts)
    y = moe_experts(hp, n_blk, blk_start, row_slot, gates[:, :TOP_K].reshape(-1), p['w_gu_bf'], p['b_gu'],
                    p['w_down_bf'], p['b_down'])
    return gated_residual(y, x, gate_mod, tokens_per_group)


def _pad_lanes(a, width):
    return jnp.pad(a, [(0, 0)] * (a.ndim - 1) + [(0, width - a.shape[-1])])


def _prep_common(p):
    p['rw_pad'] = _pad_lanes(p['router_w'], LANES)
    p['rb_pad'] = jnp.concatenate(
        [p['router_b'].astype(F32), jnp.full((LANES - N_EXPERTS,), NEG_BIG, F32)]).reshape(1, LANES)
    p['w_out'] = p['w_out'].astype(BF16)
    p['w_gu_bf'] = p['w_gu'].astype(BF16)
    p['w_down_bf'] = p['w_down'].astype(BF16)
    return p


def _prep_layer0(p):
    p = _prep_common(dict(p))
    p['w_in'] = _pad_lanes(p['w_in'], AB_IN_PAD).astype(BF16)
    q_b = p['q_b'].reshape(MLA_Q_LORA, MLA_HEADS, MLA_QK)
    p['q_b_pad'] = _pad_lanes(q_b, LANES).reshape(MLA_Q_LORA, MLA_HEADS * LANES).astype(BF16)
    kv_b = p['kv_b'].reshape(MLA_KV_LORA, MLA_HEADS, MLA_NOPE + MLA_V)
    p['wk_pad'] = _pad_lanes(kv_b[:, :, :MLA_NOPE], LANES).reshape(MLA_KV_LORA, MLA_HEADS * LANES).astype(BF16)
    p['wv'] = kv_b[:, :, MLA_NOPE:].reshape(MLA_KV_LORA, MLA_HEADS * MLA_V).astype(BF16)
    p['q_norm_pad'] = _pad_lanes(p['q_norm'].reshape(1, -1), LANES)
    p['k_norm_pad'] = _pad_lanes(p['k_norm'].reshape(1, -1), LANES)
    return p


def _prep_layer1(p):
    p = _prep_common(dict(p))
    p['w_in'] = p['w_in'].astype(BF16)
    p['q_norm2'] = jnp.tile(p['q_norm'].reshape(1, -1), (1, LANES // SWA_HD))
    p['k_norm2'] = jnp.tile(p['k_norm'].reshape(1, -1), (1, LANES // SWA_HD))
    return p


def _group_forward(x3, mods, p0, p1, lb_logits, caches, latent):
    batch, seq, d = x3.shape
    t = batch * seq
    x = x3.reshape(t, d)
    tpg = seq if latent else t
    sh1, sc1, g1, sh2, sc2, g2 = mods[0]

    z = modnorm_matmul(x, p0['norm1'], sh1, sc1, p0['w_in'], tpg)
    tab_b = _rope_lane_tables(seq, MLA_ROPE, LANES, MLA_NOPE) if latent else None
    s0f, s0b = (caches['hg_f'], caches['hg_b']) if latent else (None, None)
    o_f, o_b, s_f, s_b = hgrn_bidir(z, lb_logits, s0f, s0b, batch, seq)
    q = mla_queries(z, p0['qa_norm'], p0['q_b_pad'], p0['q_norm_pad'], tab_b, seq)
    ckv, k, v = mla_keys_values(z, 2816 // LANES, z, 2944 // LANES, p0['kva_norm'], p0['wk_pad'], p0['wv'],
                                p0['k_norm_pad'], tab_b, seq, norm_input=True)
    n_k = seq
    if latent:
        n_ctx = caches['ckv'].shape[1]
        ckv_c = caches['ckv'].reshape(batch * n_ctx, MLA_KV_LORA)
        kpe_c = _pad_lanes(caches['kpe'].reshape(batch * n_ctx, MLA_ROPE), LANES)
        _, k_c, v_c = mla_keys_values(ckv_c, 0, kpe_c, 0, p0['kva_norm'], p0['wk_pad'], p0['wv'],
                                      p0['k_norm_pad'], None, n_ctx, norm_input=False)
        cat = lambda a, b: jnp.concatenate([a.reshape(batch, n_ctx, -1), b.reshape(batch, seq, -1)],
                                           axis=1).reshape(batch * (n_ctx + seq), -1)
        k, v = cat(k_c, k), cat(v_c, v)
        n_k = n_ctx + seq
    o_mla = mla_attention(q, k, v, batch, seq, n_k)
    x = out_proj_layer0(o_f, o_b, z, p0['hg_out_norm'], o_mla, p0['w_out'], x, g1, tpg)
    x = moe_layer(x, p0, sh2, sc2, g2, tpg)
    state0 = (s_f, s_b, ckv.reshape(batch, seq, MLA_KV_LORA), z[:, 2944:2944 + MLA_ROPE].reshape(batch, seq, MLA_ROPE))

    sh1, sc1, g1, sh2, sc2, g2 = mods[1]
    z = modnorm_matmul(x, p1['norm1'], sh1, sc1, p1['w_in'], tpg)
    tab_c = _rope_lane_tables(seq, SWA_HD, SWA_HD, 0) if latent else None
    q, k_cache, k, v = swa_prep(z, p1['q_norm2'], p1['k_norm2'], tab_c, seq)
    sink = p1['sink'].astype(F32)
    if latent:
        n_ctx = caches['k1'].shape[1]
        k_c = caches['k1'].reshape(batch, n_ctx, -1).astype(BF16)
        v_c = caches['v1'].reshape(batch, n_ctx, -1).astype(BF16)
        a = swa_window(q, k, v, k_c, v_c, sink, batch, seq, n_ctx)
    else:
        a = swa_dense(q, k, v, sink, batch, seq)
    x = out_proj_layer1(a, p1['w_out'], x, g1, tpg)
    x = moe_layer(x, p1, sh2, sc2, g2, tpg)
    nkv = SWA_KV_HEADS * SWA_HD
    state1 = (k_cache.reshape(batch, seq, SWA_KV_HEADS, SWA_HD),
              z[:, SWA_HEADS * SWA_HD + nkv:].reshape(batch, seq, SWA_KV_HEADS, SWA_HD))
    return x.reshape(batch, seq, d), state0, state1


def kernel(x_prompt, x_sample, state_l0_hgrn_fwd, state_l0_hgrn_bwd, cache_l0_mla_ckv, cache_l0_mla_kpe, cache_l1_k, cache_l1_v, c, c_ctx, hgrn_lb_logits, l0_ada_w, l0_ada_b, l0_norm1, l0_norm2, l0_w_in, l0_hg_out_norm, l0_qa_norm, l0_q_b, l0_kva_norm, l0_kv_b, l0_q_norm, l0_k_norm, l0_w_out, l0_router_w, l0_router_b, l0_w_gu, l0_b_gu, l0_w_down, l0_b_down, l1_ada_w, l1_ada_b, l1_norm1, l1_norm2, l1_w_in, l1_q_norm, l1_k_norm, l1_sink, l1_w_out, l1_router_w, l1_router_b, l1_w_gu, l1_b_gu, l1_w_down, l1_b_down):
    p0 = _prep_layer0(dict(norm1=l0_norm1, norm2=l0_norm2, w_in=l0_w_in, hg_out_norm=l0_hg_out_norm,
                           qa_norm=l0_qa_norm, q_b=l0_q_b, kva_norm=l0_kva_norm, kv_b=l0_kv_b,
                           q_norm=l0_q_norm, k_norm=l0_k_norm, w_out=l0_w_out, router_w=l0_router_w,
                           router_b=l0_router_b, w_gu=l0_w_gu, b_gu=l0_b_gu, w_down=l0_w_down,
                           b_down=l0_b_down))
    p1 = _prep_layer1(dict(norm1=l1_norm1, norm2=l1_norm2, w_in=l1_w_in, q_norm=l1_q_norm, k_norm=l1_k_norm,
                           sink=l1_sink, w_out=l1_w_out, router_w=l1_router_w, router_b=l1_router_b,
                           w_gu=l1_w_gu, b_gu=l1_b_gu, w_down=l1_w_down, b_down=l1_b_down))
    dec_batch = c.shape[0]
    d = c.shape[1]
    cond8 = jnp.concatenate([c_ctx[None, :], c, jnp.zeros((8 - 1 - dec_batch, d), F32)], axis=0)
    mods_ctx, mods_lat = [], []
    for w, b in ((l0_ada_w, l0_ada_b), (l1_ada_w, l1_ada_b)):
        mod = ada_params(cond8, w, b)
        mods_ctx.append([m.reshape(1, 1, d) for m in jnp.split(mod[0:1], 6, axis=-1)])
        mods_lat.append([m.reshape(dec_batch, 1, d) for m in jnp.split(mod[1:1 + dec_batch], 6, axis=-1)])

    y_prompt, st0, st1 = _group_forward(x_prompt, mods_ctx, p0, p1, hgrn_lb_logits, None, latent=False)
    caches = dict(hg_f=state_l0_hgrn_fwd, hg_b=state_l0_hgrn_bwd, ckv=cache_l0_mla_ckv, kpe=cache_l0_mla_kpe,
                  k1=cache_l1_k, v1=cache_l1_v)
    y_sample, _, _ = _group_forward(x_sample, mods_lat, p0, p1, hgrn_lb_logits, caches, latent=True)
    return (y_prompt, y_sample, st0[0], st0[1], st0[2], st0[3], st1[0], st1[1])
```

```python
import functools

import numpy as np
import jax
import jax.numpy as jnp
from jax import lax
from jax.experimental import pallas as pl
from jax.experimental.pallas import tpu as pltpu

F32 = jnp.float32
BF16 = jnp.bfloat16

D_MODEL = 1024
GRID_W = 64
ROPE_THETA = 10000.0
EPS = 1e-6
HG_HEADS = 4
HG_DK = 128
HG_DV = 128
MLA_HEADS = 8
MLA_NOPE = 64
MLA_ROPE = 32
MLA_V = 64
MLA_QK = MLA_NOPE + MLA_ROPE
MLA_Q_LORA = 256
MLA_KV_LORA = 128
MLA_SCALE = MLA_QK ** -0.5
SWA_HEADS = 16
SWA_KV_HEADS = 4
SWA_HD = 64
SWA_WINDOW = 128
SWA_SCALE = SWA_HD ** -0.5
SWA_GROUP = SWA_HEADS // SWA_KV_HEADS
N_EXPERTS = 32
TOP_K = 4
D_FF = 1024
SWIGLU_LIMIT = 7.0
SWIGLU_ALPHA = 1.702

LANES = 128
HG_CHUNK = 128
HG_LEVELS = 7
HG_MXU_LEVELS = 3
AB_IN_PAD = 3072
MOE_ROWS = 256
NEG_BIG = -1e30
VMEM_LIMIT = 48 * 1024 * 1024
EXPERT_VMEM_LIMIT = 56 * 1024 * 1024


def _cparams(sem):
    return pltpu.CompilerParams(dimension_semantics=sem, vmem_limit_bytes=VMEM_LIMIT)


def _dot(a, b):
    return jnp.dot(a, b, preferred_element_type=F32)


def _dot_nt(a, b):
    return lax.dot_general(a, b, (((1,), (1,)), ((), ())), preferred_element_type=F32)


def _dot_tn(a, b):
    return lax.dot_general(a, b, (((0,), (0,)), ((), ())), preferred_element_type=F32)


def _split2(x):
    hi = x.astype(BF16)
    lo = (x - hi.astype(F32)).astype(BF16)
    return hi, lo


def _dot_hp(a, b):
    ah, al = _split2(a)
    bh, bl = _split2(b)
    return _dot(ah, bh) + _dot(ah, bl) + _dot(al, bh)


def _sigmoid(x):
    return 1.0 / (1.0 + jnp.exp(-x))


def _modnorm(x, gain, shift, scale):
    y = x * lax.rsqrt(jnp.mean(x * x, axis=-1, keepdims=True) + EPS)
    return y * gain * (1.0 + scale) + shift


def _ada_kernel(c_ref, w_ref, b_ref, o_ref):
    c = c_ref[...]
    o_ref[...] = _dot_hp(c * _sigmoid(c), w_ref[...]) + b_ref[...]


def ada_params(cond8, w, b):
    n = w.shape[1]
    tn = 1024
    return pl.pallas_call(
        _ada_kernel,
        grid=(n // tn,),
        in_specs=[pl.BlockSpec((8, D_MODEL), lambda j: (0, 0)),
                  pl.BlockSpec((D_MODEL, tn), lambda j: (0, j)),
                  pl.BlockSpec((1, tn), lambda j: (0, j))],
        out_specs=pl.BlockSpec((8, tn), lambda j: (0, j)),
        out_shape=jax.ShapeDtypeStruct((8, n), F32),
        compiler_params=_cparams(("parallel",)),
        name="ada_params",
    )(cond8, w, b.reshape(1, n))


def _modnorm_matmul_kernel(x_ref, g_ref, sh_ref, sc_ref, w_ref, o_ref):
    h = _modnorm(x_ref[...], g_ref[...], sh_ref[0], sc_ref[0])
    o_ref[...] = _dot(h.astype(BF16), w_ref[...])


def modnorm_matmul(x, gain, shift, scale, w_bf16, tokens_per_group, tm=256):
    t, d = x.shape
    n = w_bf16.shape[1]
    tiles_per_group = tokens_per_group // tm
    grp = lambda i: (i // tiles_per_group, 0, 0)
    return pl.pallas_call(
        _modnorm_matmul_kernel,
        grid=(t // tm,),
        in_specs=[pl.BlockSpec((tm, d), lambda i: (i, 0)),
                  pl.BlockSpec((1, d), lambda i: (0, 0)),
                  pl.BlockSpec((1, 1, d), grp),
                  pl.BlockSpec((1, 1, d), grp),
                  pl.BlockSpec((d, n), lambda i: (0, 0))],
        out_specs=pl.BlockSpec((tm, n), lambda i: (i, 0)),
        out_shape=jax.ShapeDtypeStruct((t, n), F32),
        compiler_params=_cparams(("parallel",)),
        name="modnorm_matmul",
    )(x, gain.reshape(1, d), shift, scale, w_bf16)


def _hgrn_constants():
    c = HG_CHUNK
    t = np.arange(c)[:, None]
    u = np.arange(c)[None, :]
    mats = [(u <= t), (u > t)]
    for l in range(HG_MXU_LEVELS):
        m = 1 << l
        r = (t // (2 * m)) * (2 * m) + m - 1
        mats.append((u > np.minimum(t, r)) & (u <= np.maximum(t, r)))
    fwd = np.concatenate(mats, axis=0).astype(np.float32)
    bwd = np.concatenate([mm[::-1, ::-1] for mm in mats], axis=0).astype(np.float32)
    x = np.bitwise_xor(t, u)
    lvl = np.where(x > 0, np.floor(np.log2(np.maximum(x, 1))), HG_LEVELS).astype(np.int32)
    lv_f = np.where(t >= u, lvl, -1).astype(np.int32)
    return fwd, bwd, lv_f, lv_f.T.copy()


def _hgrn_direction(qs, fpres, vs, lbs, mcat, lv, sts, forward):
    c = HG_CHUNK
    n = len(qs)
    fs = [lb + (1.0 - lb) * _sigmoid(fp) for lb, fp in zip(lbs, fpres)]
    kks = [1.0 - f for f in fs]
    parts = []
    for f in fs:
        parts += list(_split2(jnp.log(f)))
    x_all = _dot(mcat, jnp.concatenate(parts, axis=1))
    xs = [x_all[:, 2 * i * c:(2 * i + 1) * c] + x_all[:, (2 * i + 1) * c:(2 * i + 2) * c] for i in range(n)]
    gs = [x[0:c] for x in xs]
    qbs = [q.astype(BF16) for q in qs]
    kbs = [kk.astype(BF16) for kk in kks]
    vbs = [v.astype(BF16) for v in vs]
    os_ = [_dot_nt((q * jnp.exp(g)).astype(BF16), st.astype(BF16)) for q, g, st in zip(qs, gs, sts)]
    accs = [jnp.where(lv == HG_LEVELS, _dot_nt(qb, kb), 0.0) for qb, kb in zip(qbs, kbs)]
    for l in range(HG_LEVELS):
        for i in range(n):
            if l < HG_MXU_LEVELS:
                x = xs[i][(2 + l) * c:(3 + l) * c]
            else:
                m = 1 << l
                ref_rows = [j * 2 * m + (m - 1 if forward else m) for j in range(c // (2 * m))]
                g_ref = jnp.concatenate(
                    [jnp.broadcast_to(gs[i][r:r + 1, :], (2 * m, c)) for r in ref_rows], axis=0)
                x = -jnp.abs(gs[i] - g_ref)
            e = jnp.exp(x)
            p = _dot_nt((qs[i] * e).astype(BF16), (kks[i] * e).astype(BF16))
            accs[i] = jnp.where(lv == l, p, accs[i])
    edge_row = c - 1 if forward else 0
    outs, new_sts = [], []
    for i in range(n):
        outs.append(os_[i] + _dot(accs[i].astype(BF16), vbs[i]))
        k_end = (kks[i] * jnp.exp(xs[i][c:2 * c])).astype(BF16)
        new_sts.append(sts[i] * jnp.exp(gs[i][edge_row:edge_row + 1, :]) + _dot_tn(vbs[i], k_end))
    return outs, new_sts


def _hgrn_kernel(*refs, has_init):
    if has_init:
        (qf_ref, qb_ref, ff_ref, fb_ref, vf_ref, vb_ref, lbl_ref, mf_ref, mb_ref, lvf_ref, lvb_ref,
         s0f_ref, s0b_ref, of_ref, ob_ref, sf_ref, sb_ref, stf, stb) = refs
    else:
        (qf_ref, qb_ref, ff_ref, fb_ref, vf_ref, vb_ref, lbl_ref, mf_ref, mb_ref, lvf_ref, lvb_ref,
         of_ref, ob_ref, sf_ref, sb_ref, stf, stb) = refs
    c = pl.program_id(1)
    nc = pl.num_programs(1)

    @pl.when(c == 0)
    def _():
        for h in range(HG_HEADS):
            if has_init:
                stf[h] = s0f_ref[0, h].T
                stb[h] = s0b_ref[0, h].T
            else:
                stf[h] = jnp.zeros((HG_DV, HG_DK), F32)
                stb[h] = jnp.zeros((HG_DV, HG_DK), F32)

    rows = [lbl_ref[:, j, :] for j in range(lbl_ref.shape[1])]
    mx = functools.reduce(jnp.maximum, rows)
    ex = [jnp.exp(r - mx) for r in rows]
    lb = ex[0] / functools.reduce(lambda a, b: a + b, ex)

    heads = [slice(h * LANES, (h + 1) * LANES) for h in range(HG_HEADS)]
    o_f, st_f = _hgrn_direction([qf_ref[0, :, hs] for hs in heads], [ff_ref[0, :, hs] for hs in heads],
                                [vf_ref[0, :, hs] for hs in heads], [lb[0:1, hs] for hs in heads],
                                mf_ref[...], lvf_ref[...], [stf[h] for h in range(HG_HEADS)], True)
    o_b, st_b = _hgrn_direction([qb_ref[0, :, hs] for hs in heads], [fb_ref[0, :, hs] for hs in heads],
                                [vb_ref[0, :, hs] for hs in heads], [lb[1:2, hs] for hs in heads],
                                mb_ref[...], lvb_ref[...], [stb[h] for h in range(HG_HEADS)], False)
    for h, hs in enumerate(heads):
        of_ref[0, :, hs] = o_f[h]
        ob_ref[0, :, hs] = o_b[h]
        stf[h] = st_f[h]
        stb[h] = st_b[h]

    @pl.when(c == nc - 1)
    def _():
        for h in range(HG_HEADS):
            sf_ref[0, h] = stf[h].T
            sb_ref[0, h] = stb[h].T


def hgrn_bidir(z, lb_logits, s0f, s0b, batch, seq):
    nc = seq // HG_CHUNK
    z3 = z.reshape(batch, seq, z.shape[1])
    mf, mb, lvf, lvb = _hgrn_constants()
    has_init = s0f is not None
    width = HG_HEADS * LANES
    blk = (1, HG_CHUNK, width)
    fwd = lambda off: pl.BlockSpec(blk, lambda b, c: (b, c, off))
    bwd = lambda off: pl.BlockSpec(blk, lambda b, c: (b, nc - 1 - c, off))
    full = lambda a: pl.BlockSpec(a.shape, lambda b, c: (0,) * a.ndim)
    st_spec = pl.BlockSpec((1, HG_HEADS, HG_DK, HG_DV), lambda b, c: (b, 0, 0, 0))
    consts = [jnp.asarray(mf, BF16), jnp.asarray(mb, BF16), jnp.asarray(lvf), jnp.asarray(lvb)]
    in_specs = [fwd(0), bwd(0), fwd(1), bwd(2), fwd(3), bwd(3),
                pl.BlockSpec(lb_logits.shape, lambda b, c: (0, 0, 0))]
    in_specs += [full(a) for a in consts]
    args = [z3] * 6 + [lb_logits] + consts
    if has_init:
        in_specs += [st_spec, st_spec]
        args += [s0f, s0b]
    o_shape = jax.ShapeDtypeStruct((batch, seq, width), F32)
    s_shape = jax.ShapeDtypeStruct((batch, HG_HEADS, HG_DK, HG_DV), F32)
    o_f, o_b, s_f, s_b = pl.pallas_call(
        functools.partial(_hgrn_kernel, has_init=has_init),
        grid=(batch, nc),
        in_specs=in_specs,
        out_specs=[pl.BlockSpec(blk, lambda b, c: (b, c, 0)),
                   pl.BlockSpec(blk, lambda b, c: (b, nc - 1 - c, 0)),
                   st_spec, st_spec],
        out_shape=[o_shape, o_shape, s_shape, s_shape],
        scratch_shapes=[pltpu.VMEM((HG_HEADS, HG_DV, HG_DK), F32), pltpu.VMEM((HG_HEADS, HG_DV, HG_DK), F32)],
        compiler_params=_cparams(("parallel", "arbitrary")),
        name="hgrn_bidir",
    )(*args)
    t = batch * seq
    return o_f.reshape(t, -1), o_b.reshape(t, -1), s_f, s_b


def _axial_tables(n_tokens, n_rot):
    t = jnp.arange(n_tokens)
    row = (t // GRID_W).astype(F32)
    col = (t % GRID_W).astype(F32)
    n_freq = n_rot // 4
    inv = jnp.power(ROPE_THETA, -jnp.arange(n_freq, dtype=F32) / n_freq)
    ang = jnp.concatenate([row[:, None] * inv, col[:, None] * inv], axis=-1)
    return jnp.cos(ang), jnp.sin(ang)


def _rope_lane_tables(n_tokens, n_rot, head_width, first_rot_lane):
    cos, sin = _axial_tables(n_tokens, n_rot)
    half = n_rot // 2
    c_head = jnp.ones((n_tokens, head_width), F32)
    sa_head = jnp.zeros((n_tokens, head_width), F32)
    sb_head = jnp.zeros((n_tokens, head_width), F32)
    a0, a1, a2 = first_rot_lane, first_rot_lane + half, first_rot_lane + n_rot
    c_head = c_head.at[:, a0:a1].set(cos).at[:, a1:a2].set(cos)
    sa_head = sa_head.at[:, a0:a1].set(-sin)
    sb_head = sb_head.at[:, a1:a2].set(sin)
    reps = LANES // head_width
    tile = lambda a: jnp.tile(a, (1, reps))
    return tile(c_head), tile(sa_head), tile(sb_head)


def _rope(x, c, sa, sb, half):
    return x * c + pltpu.roll(x, LANES - half, 1) * sa + pltpu.roll(x, half, 1) * sb


def _mla_q_kernel(*refs, rope):
    if rope:
        qa_ref, qan_ref, qb_ref, qn_ref, c_ref, sa_ref, sb_ref, o_ref = refs
    else:
        qa_ref, qan_ref, qb_ref, qn_ref, o_ref = refs
    qa = qa_ref[...]
    qn = qa * lax.rsqrt(jnp.mean(qa * qa, axis=-1, keepdims=True) + EPS) * qan_ref[...]
    qfull = _dot(qn.astype(BF16), qb_ref[...])
    outs = []
    for h in range(MLA_HEADS):
        qh = qfull[:, h * LANES:(h + 1) * LANES]
        ms = jnp.sum(qh * qh, axis=-1, keepdims=True) * (1.0 / MLA_QK)
        qh = qh * lax.rsqrt(ms + EPS) * qn_ref[...]
        if rope:
            qh = _rope(qh, c_ref[...], sa_ref[...], sb_ref[...], MLA_ROPE // 2)
        outs.append((qh * MLA_SCALE).astype(BF16))
    o_ref[...] = jnp.concatenate(outs, axis=1)


def mla_queries(z, qa_norm, q_b_pad, q_norm_pad, tables, tokens_per_batch, tm=256):
    t = z.shape[0]
    rope = tables is not None
    row = lambda i: (0, 0)
    in_specs = [pl.BlockSpec((tm, MLA_Q_LORA), lambda i: (i, 2560 // MLA_Q_LORA)),
                pl.BlockSpec((1, MLA_Q_LORA), row),
                pl.BlockSpec(q_b_pad.shape, row),
                pl.BlockSpec((1, LANES), row)]
    args = [z, qa_norm.reshape(1, -1), q_b_pad, q_norm_pad]
    if rope:
        tpb = tokens_per_batch // tm
        in_specs += [pl.BlockSpec((tm, LANES), lambda i: (i % tpb, 0))] * 3
        args += list(tables)
    return pl.pallas_call(
        functools.partial(_mla_q_kernel, rope=rope),
        grid=(t // tm,),
        in_specs=in_specs,
        out_specs=pl.BlockSpec((tm, MLA_HEADS * LANES), lambda i: (i, 0)),
        out_shape=jax.ShapeDtypeStruct((t, MLA_HEADS * LANES), BF16),
        compiler_params=_cparams(("parallel",)),
        name="mla_queries",
    )(*args)


def _mla_kv_kernel(*refs, norm_input, rope):
    if rope:
        kva_ref, kpe_ref, kvan_ref, wk_ref, wv_ref, kn_ref, c_ref, sa_ref, sb_ref, ckv_ref, k_ref, v_ref = refs
    else:
        kva_ref, kpe_ref, kvan_ref, wk_ref, wv_ref, kn_ref, ckv_ref, k_ref, v_ref = refs
    ckv = kva_ref[...]
    if norm_input:
        ckv = ckv * lax.rsqrt(jnp.mean(ckv * ckv, axis=-1, keepdims=True) + EPS) * kvan_ref[...]
    ckv_ref[...] = ckv
    cb = ckv.astype(BF16)
    knope = _dot(cb, wk_ref[...])
    v_ref[...] = _dot(cb, wv_ref[...]).astype(BF16)
    kpe = pltpu.roll(kpe_ref[...], MLA_NOPE, 1)
    outs = []
    for h in range(MLA_HEADS):
        kh = knope[:, h * LANES:(h + 1) * LANES] + kpe
        ms = jnp.sum(kh * kh, axis=-1, keepdims=True) * (1.0 / MLA_QK)
        kh = kh * lax.rsqrt(ms + EPS) * kn_ref[...]
        if rope:
            kh = _rope(kh, c_ref[...], sa_ref[...], sb_ref[...], MLA_ROPE // 2)
        outs.append(kh.astype(BF16))
    k_ref[...] = jnp.concatenate(outs, axis=1)


def mla_keys_values(kva_src, kva_col, kpe_src, kpe_col, kva_norm, wk_pad, wv, k_norm_pad, tables,
                    tokens_per_batch, norm_input, tm=256):
    t = kva_src.shape[0]
    rope = tables is not None
    row = lambda i: (0, 0)
    in_specs = [pl.BlockSpec((tm, LANES), lambda i: (i, kva_col)),
                pl.BlockSpec((tm, LANES), lambda i: (i, kpe_col)),
                pl.BlockSpec((1, LANES), row),
                pl.BlockSpec(wk_pad.shape, row),
                pl.BlockSpec(wv.shape, row),
                pl.BlockSpec((1, LANES), row)]
    args = [kva_src, kpe_src, kva_norm.reshape(1, -1), wk_pad, wv, k_norm_pad]
    if rope:
        tpb = tokens_per_batch // tm
        in_specs += [pl.BlockSpec((tm, LANES), lambda i: (i % tpb, 0))] * 3
        args += list(tables)
    return pl.pallas_call(
        functools.partial(_mla_kv_kernel, norm_input=norm_input, rope=rope),
        grid=(t // tm,),
        in_specs=in_specs,
        out_specs=[pl.BlockSpec((tm, LANES), lambda i: (i, 0)),
                   pl.BlockSpec((tm, MLA_HEADS * LANES), lambda i: (i, 0)),
                   pl.BlockSpec((tm, MLA_HEADS * MLA_V), lambda i: (i, 0))],
        out_shape=[jax.ShapeDtypeStruct((t, LANES), F32),
                   jax.ShapeDtypeStruct((t, MLA_HEADS * LANES), BF16),
                   jax.ShapeDtypeStruct((t, MLA_HEADS * MLA_V), BF16)],
        compiler_params=_cparams(("parallel",)),
        name="mla_keys_values",
    )(*args)


def _mla_attn_kernel(q_ref, k_ref, v_ref, o_ref):
    outs = []
    for j in range(2):
        q = q_ref[0][:, j * LANES:(j + 1) * LANES]
        k = k_ref[0][:, j * LANES:(j + 1) * LANES]
        v = v_ref[0][:, j * MLA_V:(j + 1) * MLA_V]
        s = _dot_nt(q, k)
        p = jnp.exp(s - jnp.max(s, axis=-1, keepdims=True))
        l = jnp.sum(p, axis=-1, keepdims=True)
        outs.append(_dot(p.astype(BF16), v) / l)
    o_ref[0] = jnp.concatenate(outs, axis=1)


def mla_attention(q, k, v, batch, n_q, n_k, tq=256):
    q3 = q.reshape(batch, n_q, -1)
    k3 = k.reshape(batch, n_k, -1)
    v3 = v.reshape(batch, n_k, -1)
    out = pl.pallas_call(
        _mla_attn_kernel,
        grid=(batch, MLA_HEADS // 2, n_q // tq),
        in_specs=[pl.BlockSpec((1, tq, 2 * LANES), lambda b, j, i: (b, i, j)),
                  pl.BlockSpec((1, n_k, 2 * LANES), lambda b, j, i: (b, 0, j)),
                  pl.BlockSpec((1, n_k, 2 * MLA_V), lambda b, j, i: (b, 0, j))],
        out_specs=pl.BlockSpec((1, tq, 2 * MLA_V), lambda b, j, i: (b, i, j)),
        out_shape=jax.ShapeDtypeStruct((batch, n_q, MLA_HEADS * MLA_V), F32),
        compiler_params=_cparams(("parallel", "parallel", "arbitrary")),
        name="mla_attention",
    )(q3, k3, v3)
    return out.reshape(batch * n_q, -1)


def _out0_kernel(of_ref, ob_ref, ag_ref, hgn_ref, om_ref, w_ref, x_ref, g_ref, o_ref):
    o = of_ref[...] + ob_ref[...]
    ag = ag_ref[...]
    parts = []
    for h in range(HG_HEADS):
        oh = o[:, h * HG_DV:(h + 1) * HG_DV]
        oh = oh * lax.rsqrt(jnp.mean(oh * oh, axis=-1, keepdims=True) + EPS) * hgn_ref[...]
        gh = ag[:, h * HG_DV:(h + 1) * HG_DV]
        parts.append((oh * (gh * _sigmoid(gh))).astype(BF16))
    oa = jnp.concatenate(parts, axis=1)
    n_a = HG_HEADS * HG_DV
    mix = _dot(oa, w_ref[0:n_a, :]) + _dot(om_ref[...].astype(BF16), w_ref[n_a:, :])
    o_ref[...] = x_ref[...] + g_ref[0] * mix


def out_proj_layer0(o_f, o_b, z, hg_norm, o_mla, w_out_bf16, x, gate, tokens_per_group, tm=256):
    t, d = x.shape
    n_a = HG_HEADS * HG_DV
    tiles_per_group = tokens_per_group // tm
    tile = lambda w: pl.BlockSpec((tm, w), lambda i: (i, 0))
    return pl.pallas_call(
        _out0_kernel,
        grid=(t // tm,),
        in_specs=[tile(n_a), tile(n_a),
                  pl.BlockSpec((tm, n_a), lambda i: (i, 2048 // n_a)),
                  pl.BlockSpec((1, HG_DV), lambda i: (0, 0)),
                  tile(o_mla.shape[1]),
                  pl.BlockSpec(w_out_bf16.shape, lambda i: (0, 0)),
                  tile(d),
                  pl.BlockSpec((1, 1, d), lambda i: (i // tiles_per_group, 0, 0))],
        out_specs=tile(d),
        out_shape=jax.ShapeDtypeStruct((t, d), F32),
        compiler_params=_cparams(("parallel",)),
        name="out_proj_layer0",
    )(o_f, o_b, z, hg_norm.reshape(1, -1), o_mla, w_out_bf16, x, gate)


def _out1_kernel(a_ref, w_ref, x_ref, g_ref, o_ref):
    o_ref[...] = x_ref[...] + g_ref[0] * _dot(a_ref[...].astype(BF16), w_ref[...])


def out_proj_layer1(a, w_out_bf16, x, gate, tokens_per_group, tm=256):
    t, d = x.shape
    tiles_per_group = tokens_per_group // tm
    tile = lambda w: pl.BlockSpec((tm, w), lambda i: (i, 0))
    return pl.pallas_call(
        _out1_kernel,
        grid=(t // tm,),
        in_specs=[tile(a.shape[1]),
                  pl.BlockSpec(w_out_bf16.shape, lambda i: (0, 0)),
                  tile(d),
                  pl.BlockSpec((1, 1, d), lambda i: (i // tiles_per_group, 0, 0))],
        out_specs=tile(d),
        out_shape=jax.ShapeDtypeStruct((t, d), F32),
        compiler_params=_cparams(("parallel",)),
        name="out_proj_layer1",
    )(a, w_out_bf16, x, gate)


def _head_rms(x, gain2):
    sq = x * x
    lane = lax.broadcasted_iota(jnp.int32, x.shape, 1)
    first = lane < SWA_HD
    lo = jnp.sum(jnp.where(first, sq, 0.0), axis=-1, keepdims=True)
    hi = jnp.sum(jnp.where(first, 0.0, sq), axis=-1, keepdims=True)
    ms = jnp.where(first, lo, hi) * (1.0 / SWA_HD)
    return x * lax.rsqrt(ms + EPS) * gain2


def _swa_prep_kernel(*refs, rope):
    if rope:
        zq_ref, zk_ref, zv_ref, qn_ref, kn_ref, c_ref, sa_ref, sb_ref, q_ref, kc_ref, k_ref, v_ref = refs
    else:
        zq_ref, zk_ref, zv_ref, qn_ref, kn_ref, q_ref, kc_ref, k_ref, v_ref = refs
    half = SWA_HD // 2

    def rot(x):
        return _rope(x, c_ref[...], sa_ref[...], sb_ref[...], half) if rope else x

    zq = zq_ref[...]
    qs = []
    for p in range(zq.shape[1] // LANES):
        x = _head_rms(zq[:, p * LANES:(p + 1) * LANES], qn_ref[...])
        qs.append((rot(x) * SWA_SCALE).astype(BF16))
    q_ref[...] = jnp.concatenate(qs, axis=1)
    zk = zk_ref[...]
    kn, kr = [], []
    for p in range(zk.shape[1] // LANES):
        x = _head_rms(zk[:, p * LANES:(p + 1) * LANES], kn_ref[...])
        kn.append(x)
        kr.append(rot(x).astype(BF16))
    kc_ref[...] = jnp.concatenate(kn, axis=1)
    k_ref[...] = jnp.concatenate(kr, axis=1)
    v_ref[...] = zv_ref[...].astype(BF16)


def swa_prep(z, q_norm2, k_norm2, tables, tokens_per_batch, tm=256):
    t = z.shape[0]
    nq = SWA_HEADS * SWA_HD
    nkv = SWA_KV_HEADS * SWA_HD
    rope = tables is not None
    row = lambda i: (0, 0)
    in_specs = [pl.BlockSpec((tm, nq), lambda i: (i, 0)),
                pl.BlockSpec((tm, nkv), lambda i: (i, nq // nkv)),
                pl.BlockSpec((tm, nkv), lambda i: (i, nq // nkv + 1)),
                pl.BlockSpec((1, LANES), row),
                pl.BlockSpec((1, LANES), row)]
    args = [z, z, z, q_norm2, k_norm2]
    if rope:
        tpb = tokens_per_batch // tm
        in_specs += [pl.BlockSpec((tm, LANES), lambda i: (i % tpb, 0))] * 3
        args += list(tables)
    tile = lambda w: pl.BlockSpec((tm, w), lambda i: (i, 0))
    return pl.pallas_call(
        functools.partial(_swa_prep_kernel, rope=rope),
        grid=(t // tm,),
        in_specs=in_specs,
        out_specs=[tile(nq), tile(nkv), tile(nkv), tile(nkv)],
        out_shape=[jax.ShapeDtypeStruct((t, nq), BF16),
                   jax.ShapeDtypeStruct((t, nkv), F32),
                   jax.ShapeDtypeStruct((t, nkv), BF16),
                   jax.ShapeDtypeStruct((t, nkv), BF16)],
        compiler_params=_cparams(("parallel",)),
        name="swa_prep",
    )(*args)


def _sink_softmax_pv(s, sk, vn):
    m = jnp.maximum(jnp.max(s, axis=-1, keepdims=True), sk)
    p = jnp.exp(s - m)
    l = jnp.sum(p, axis=-1, keepdims=True) + jnp.exp(sk - m)
    return _dot(p.astype(BF16), vn) / l


def _sink_attention(q8, k_all, v_all, bias, sink_ref, pair, stack):
    outs = []
    tq = q8.shape[0]
    for n in range(2):
        kn = k_all[:, n * SWA_HD:(n + 1) * SWA_HD]
        vn = v_all[:, n * SWA_HD:(n + 1) * SWA_HD]
        heads = [n * SWA_GROUP + g for g in range(SWA_GROUP)]
        qs = [q8[:, hq * SWA_HD:(hq + 1) * SWA_HD] for hq in heads]
        sinks = [sink_ref[pair * 2 * SWA_GROUP + hq] for hq in heads]
        if stack:
            sk = jnp.concatenate([jnp.full((tq, 1), s, F32) for s in sinks], axis=0)
            s = _dot_nt(jnp.concatenate(qs, axis=0), kn)
            if bias is not None:
                s = s + bias
            o4 = _sink_softmax_pv(s, sk, vn)
            outs += [o4[g * tq:(g + 1) * tq] for g in range(SWA_GROUP)]
        else:
            for q, sk in zip(qs, sinks):
                s = _dot_nt(q, kn)
                if bias is not None:
                    s = s + bias
                outs.append(_sink_softmax_pv(s, sk, vn))
    return jnp.concatenate(outs, axis=1)


def _swa_dense_kernel(sink_ref, q_ref, k_ref, v_ref, o_ref):
    o_ref[0] = _sink_attention(q_ref[0], k_ref[0], v_ref[0], None, sink_ref, pl.program_id(1), stack=False)


def swa_dense(q, k, v, sink, batch, seq):
    nq = SWA_HEADS * SWA_HD
    q3 = q.reshape(batch, seq, nq)
    k3 = k.reshape(batch, seq, -1)
    v3 = v.reshape(batch, seq, -1)
    out = pl.pallas_call(
        _swa_dense_kernel,
        grid=(batch, 2),
        in_specs=[pl.BlockSpec(memory_space=pltpu.SMEM),
                  pl.BlockSpec((1, seq, nq // 2), lambda b, j: (b, 0, j)),
                  pl.BlockSpec((1, seq, LANES), lambda b, j: (b, 0, j)),
                  pl.BlockSpec((1, seq, LANES), lambda b, j: (b, 0, j))],
        out_specs=pl.BlockSpec((1, seq, nq // 2), lambda b, j: (b, 0, j)),
        out_shape=jax.ShapeDtypeStruct((batch, seq, nq), F32),
        compiler_params=_cparams(("parallel", "parallel")),
        name="swa_dense",
    )(sink, q3, k3, v3)
    return out.reshape(batch * seq, nq)


def _swa_window_kernel(sink_ref, q_ref, kc_ref, vc_ref, kp_ref, k0_ref, kn_ref, vp_ref, v0_ref, vn_ref, o_ref):
    i = pl.program_id(2)
    nb = pl.num_programs(2)
    w = SWA_WINDOW
    n_ctx = kc_ref.shape[1]
    k_all = jnp.concatenate([kc_ref[0], kp_ref[0], k0_ref[0], kn_ref[0]], axis=0)
    v_all = jnp.concatenate([vc_ref[0], vp_ref[0], v0_ref[0], vn_ref[0]], axis=0)
    shape = (SWA_GROUP * w, n_ctx + 3 * w)
    r = lax.broadcasted_iota(jnp.int32, shape, 0) % w
    col = lax.broadcasted_iota(jnp.int32, shape, 1) - n_ctx
    prev_bias = jnp.where(i > 0, 0.0, NEG_BIG)
    next_bias = jnp.where(i < nb - 1, 0.0, NEG_BIG)
    bias = jnp.where(
        col < w,
        jnp.where(col < 0, 0.0, jnp.where(col >= r, prev_bias, NEG_BIG)),
        jnp.where(col < 2 * w, 0.0, jnp.where(col - 2 * w <= r, next_bias, NEG_BIG)))
    o_ref[0] = _sink_attention(q_ref[0], k_all, v_all, bias, sink_ref, pl.program_id(1), stack=True)


def swa_window(q, k, v, k_ctx, v_ctx, sink, batch, seq, n_ctx):
    nq = SWA_HEADS * SWA_HD
    w = SWA_WINDOW
    nb = seq // w
    q3 = q.reshape(batch, seq, nq)
    k3 = k.reshape(batch, seq, -1)
    v3 = v.reshape(batch, seq, -1)
    ctx = pl.BlockSpec((1, n_ctx, LANES), lambda b, j, i: (b, 0, j))
    prv = pl.BlockSpec((1, w, LANES), lambda b, j, i: (b, jnp.maximum(i - 1, 0), j))
    cur = pl.BlockSpec((1, w, LANES), lambda b, j, i: (b, i, j))
    nxt = pl.BlockSpec((1, w, LANES), lambda b, j, i: (b, jnp.minimum(i + 1, nb - 1), j))
    out = pl.pallas_call(
        _swa_window_kernel,
        grid=(batch, 2, nb),
        in_specs=[pl.BlockSpec(memory_space=pltpu.SMEM),
                  pl.BlockSpec((1, w, nq // 2), lambda b, j, i: (b, i, j)),
                  ctx, ctx, prv, cur, nxt, prv, cur, nxt],
        out_specs=pl.BlockSpec((1, w, nq // 2), lambda b, j, i: (b, i, j)),
        out_shape=jax.ShapeDtypeStruct((batch, seq, nq), F32),
        compiler_params=_cparams(("parallel", "parallel", "arbitrary")),
        name="swa_window",
    )(sink, q3, k_ctx, v_ctx, k3, k3, k3, v3, v3, v3)
    return out.reshape(batch * seq, nq)


def _router_kernel(x_ref, g_ref, sh_ref, sc_ref, rw_ref, rb_ref, tri_ref, h_ref, idx_ref, gate_ref, rank_ref,
                   cnt_ref, cnt, *, tiles_per_part):
    @pl.when(pl.program_id(0) % tiles_per_part == 0)
    def _():
        cnt[...] = jnp.zeros_like(cnt)

    h = _modnorm(x_ref[...], g_ref[...], sh_ref[0], sc_ref[0])
    bits = lax.bitcast_convert_type(h.astype(BF16).astype(F32), jnp.uint32)
    half = h.shape[1] // 2
    h_ref[...] = (bits[:, :half] >> 16) | (bits[:, half:] & jnp.uint32(0xFFFF0000))
    logits = _dot_hp(h, rw_ref[...]) + rb_ref[...]
    lane = lax.broadcasted_iota(jnp.int32, logits.shape, 1)
    work = logits
    vals, idxs = [], []
    for _ in range(TOP_K):
        m = jnp.max(work, axis=-1, keepdims=True)
        ik = jnp.min(jnp.where(work == m, lane, LANES), axis=-1, keepdims=True)
        vals.append(m)
        idxs.append(ik)
        work = jnp.where(lane == ik, 2.0 * NEG_BIG, work)
    es = [jnp.exp(v - vals[0]) for v in vals]
    denom = es[0] + es[1] + es[2] + es[3]
    idx_out = jnp.zeros(logits.shape, jnp.int32)
    gate_out = jnp.zeros(logits.shape, F32)
    for k in range(TOP_K):
        idx_out = jnp.where(lane == k, idxs[k], idx_out)
        gate_out = jnp.where(lane == k, es[k] / denom, gate_out)
    idx_ref[...] = idx_out
    gate_ref[...] = gate_out
    chosen = jnp.zeros(logits.shape, F32)
    for k in range(TOP_K):
        chosen = jnp.where(lane == idxs[k], 1.0, chosen)
    before = _dot(tri_ref[...], chosen.astype(BF16)) + cnt[...]
    rank_out = jnp.zeros(logits.shape, jnp.int32)
    for k in range(TOP_K):
        rk = jnp.sum(jnp.where(lane == idxs[k], before, 0.0), axis=-1, keepdims=True)
        rank_out = jnp.where(lane == k, rk.astype(jnp.int32), rank_out)
    rank_ref[...] = rank_out
    cnt[...] = cnt[...] + jnp.sum(chosen, axis=0, keepdims=True)
    cnt_ref[0] = cnt[...]


def moe_router(x, gain, shift, scale, rw_pad, rb_pad, tokens_per_group, tm=256):
    t, d = x.shape
    tiles_per_group = tokens_per_group // tm
    tiles_per_part = MOE_PART // tm
    grp = lambda i: (i // tiles_per_group, 0, 0)
    tile = lambda w: pl.BlockSpec((tm, w), lambda i: (i, 0))
    tri = jnp.asarray(np.tril(np.ones((tm, tm), np.float32), -1), BF16)
    return pl.pallas_call(
        functools.partial(_router_kernel, tiles_per_part=tiles_per_part),
        grid=(t // tm,),
        in_specs=[tile(d),
                  pl.BlockSpec((1, d), lambda i: (0, 0)),
                  pl.BlockSpec((1, 1, d), grp),
                  pl.BlockSpec((1, 1, d), grp),
                  pl.BlockSpec((d, LANES), lambda i: (0, 0)),
                  pl.BlockSpec((1, LANES), lambda i: (0, 0)),
                  pl.BlockSpec((tm, tm), lambda i: (0, 0))],
        out_specs=[tile(d // 2), tile(LANES), tile(LANES), tile(LANES),
                   pl.BlockSpec((1, 1, LANES), lambda i: (i // tiles_per_part, 0, 0))],
        out_shape=[jax.ShapeDtypeStruct((t, d // 2), jnp.uint32),
                   jax.ShapeDtypeStruct((t, LANES), jnp.int32),
                   jax.ShapeDtypeStruct((t, LANES), F32),
                   jax.ShapeDtypeStruct((t, LANES), jnp.int32),
                   jax.ShapeDtypeStruct((t // MOE_PART, 1, LANES), F32)],
        scratch_shapes=[pltpu.VMEM((1, LANES), F32)],
        compiler_params=_cparams(("arbitrary",)),
        name="moe_router",
    )(x, gain.reshape(1, d), shift, scale, rw_pad, rb_pad, tri)


FFN_COLS = 256
FFN_PIECES = 2 * D_FF // FFN_COLS + D_MODEL // FFN_COLS
MOE_PART = 4096


def _unpack_rows(words):
    lo = lax.bitcast_convert_type(words << 16, F32)
    hi = lax.bitcast_convert_type(words & jnp.uint32(0xFFFF0000), F32)
    return jnp.concatenate([lo, hi], axis=1).astype(BF16)


def _ffn_block(x_ref, y_ref, wgu_ref, bgu_ref, wd_ref, bd_ref, between):
    xb = _unpack_rows(x_ref[...])
    acts = []
    for c in range(D_FF // FFN_COLS):
        lo, hi = c * FFN_COLS, (c + 1) * FFN_COLS
        gate = _dot(xb, wgu_ref[0, :, lo:hi]) + bgu_ref[0][:, lo:hi]
        between(2 * c)
        up = _dot(xb, wgu_ref[0, :, D_FF + lo:D_FF + hi]) + bgu_ref[0][:, D_FF + lo:D_FF + hi]
        between(2 * c + 1)
        gate = jnp.minimum(gate, SWIGLU_LIMIT)
        up = jnp.clip(up, -SWIGLU_LIMIT, SWIGLU_LIMIT)
        acts.append((gate * _sigmoid(SWIGLU_ALPHA * gate) * (up + 1.0)).astype(BF16))
    act = jnp.concatenate(acts, axis=1)
    for n in range(D_MODEL // FFN_COLS):
        lo, hi = n * FFN_COLS, (n + 1) * FFN_COLS
        y_ref[:, lo:hi] = _dot(act, wd_ref[0, :, lo:hi]) + bd_ref[0][:, lo:hi]
        between(2 * D_FF // FFN_COLS + n)


def _expert_kernel(nb_ref, bs_ref, loc_ref, gate_ref, hp_hbm, wgu_ref, bgu_ref, wd_ref, bd_ref, out_hbm,
                   hbuf, acc, xbuf, ybuf, sem):
    part = pl.program_id(0)
    e = pl.program_id(1)
    idx = part * N_EXPERTS + e
    nb = nb_ref[idx]
    b0 = bs_ref[idx]
    shares = np.array_split(np.arange(MOE_ROWS), FFN_PIECES)

    def fetch_rows(blk):
        base = blk * MOE_ROWS
        dst = xbuf.at[blk % 2]

        def emit(k):
            for r in shares[k]:
                r = int(r)
                dst[r:r + 1, :] = hbuf[pl.ds(loc_ref[base + r], 1), :]
        return emit

    def add_rows(blk):
        base = blk * MOE_ROWS
        src = ybuf.at[blk % 2]

        def emit(k):
            for r in shares[k]:
                r = int(r)
                row = loc_ref[base + r]
                acc[pl.ds(row, 1), :] = acc[pl.ds(row, 1), :] + gate_ref[base + r] * src[r:r + 1, :]
        return emit

    def emit_all(f):
        for k in range(FFN_PIECES):
            f(k)

    @pl.when(e == 0)
    def _():
        load = pltpu.make_async_copy(hp_hbm.at[pl.ds(part * MOE_PART, MOE_PART), :],
                                     hbuf.at[pl.ds(0, MOE_PART), :], sem.at[0])
        load.start()
        hbuf[MOE_PART:, :] = jnp.zeros((hbuf.shape[0] - MOE_PART, hbuf.shape[1]), hbuf.dtype)
        acc[...] = jnp.zeros_like(acc)
        ybuf[...] = jnp.zeros_like(ybuf)
        load.wait()
        emit_all(fetch_rows(b0))

    def block(i, carry):
        fetch, add = fetch_rows(i + 1), add_rows(i - 1)

        def between(k):
            fetch(k)
            add(k)
        _ffn_block(xbuf.at[i % 2], ybuf.at[i % 2], wgu_ref, bgu_ref, wd_ref, bd_ref, between)
        return carry

    lax.fori_loop(b0, b0 + nb, block, 0)

    @pl.when(e == N_EXPERTS - 1)
    def _():
        emit_all(add_rows(b0 + nb - 1))
        store = pltpu.make_async_copy(acc.at[pl.ds(0, MOE_PART), :],
                                      out_hbm.at[pl.ds(part * MOE_PART, MOE_PART), :], sem.at[0])
        store.start()
        store.wait()


def moe_experts(hp, n_blk, blk_start, row_loc, row_gate, w_gu, b_gu, w_down, b_down):
    t = hp.shape[0]
    d = D_MODEL
    per_expert = lambda p, e, nb, bs, sl: (e, 0, 0)
    grid_spec = pltpu.PrefetchScalarGridSpec(
        num_scalar_prefetch=3,
        grid=(t // MOE_PART, N_EXPERTS),
        in_specs=[pl.BlockSpec(memory_space=pltpu.SMEM),
                  pl.BlockSpec(memory_space=pl.ANY),
                  pl.BlockSpec((1, d, 2 * D_FF), per_expert),
                  pl.BlockSpec((1, 1, 2 * D_FF), per_expert),
                  pl.BlockSpec((1, D_FF, d), per_expert),
                  pl.BlockSpec((1, 1, d), per_expert)],
        out_specs=pl.BlockSpec(memory_space=pl.ANY),
        scratch_shapes=[pltpu.VMEM((MOE_PART + 8, d // 2), jnp.uint32),
                        pltpu.VMEM((MOE_PART + 8, d), F32),
                        pltpu.VMEM((2, MOE_ROWS, d // 2), jnp.uint32),
                        pltpu.VMEM((2, MOE_ROWS, d), F32),
                        pltpu.SemaphoreType.DMA((1,))],
    )
    return pl.pallas_call(
        _expert_kernel,
        grid_spec=grid_spec,
        out_shape=jax.ShapeDtypeStruct((t, d), F32),
        compiler_params=pltpu.CompilerParams(dimension_semantics=("arbitrary", "arbitrary"),
                                             vmem_limit_bytes=EXPERT_VMEM_LIMIT),
        name="moe_experts",
    )(n_blk, blk_start, row_loc, row_gate, hp, w_gu, b_gu.reshape(N_EXPERTS, 1, -1), w_down,
      b_down.reshape(N_EXPERTS, 1, -1))


def _residual_kernel(y_ref, x_ref, g_ref, o_ref):
    o_ref[...] = x_ref[...] + g_ref[0] * y_ref[...]


def gated_residual(y, x, gate_mod, tokens_per_group, tm=512):
    t, d = x.shape
    tiles_per_group = tokens_per_group // tm
    tile = pl.BlockSpec((tm, d), lambda i: (i, 0))
    return pl.pallas_call(
        _residual_kernel,
        grid=(t // tm,),
        in_specs=[tile, tile, pl.BlockSpec((1, 1, d), lambda i: (i // tiles_per_group, 0, 0))],
        out_specs=tile,
        out_shape=jax.ShapeDtypeStruct((t, d), F32),
        compiler_params=_cparams(("parallel",)),
        name="moe_residual",
    )(y, x, gate_mod)


def _assignment_tables(top_idx, rank, gates, counts):
    t = top_idx.shape[0]
    n_parts = t // MOE_PART
    n_assign = t * TOP_K
    part_blocks = MOE_PART * TOP_K // MOE_ROWS + N_EXPERTS + 1
    n_rows = (1 + n_parts * part_blocks) * MOE_ROWS
    padded = (counts + MOE_ROWS - 1) // MOE_ROWS * MOE_ROWS
    part_first = (1 + jnp.arange(n_parts, dtype=jnp.int32) * part_blocks) * MOE_ROWS
    row_start = part_first[:, None] + jnp.cumsum(padded, axis=1) - padded
    experts = jnp.arange(N_EXPERTS, dtype=jnp.int32)
    by_part = top_idx.reshape(n_parts, MOE_PART, TOP_K)
    start_of = jnp.sum(jnp.where(by_part[..., None] == experts, row_start[:, None, None, :], 0), axis=-1)
    dest = (start_of.reshape(t, TOP_K) + rank).reshape(-1).astype(jnp.int32)
    local = (jnp.arange(n_assign, dtype=jnp.int32) // TOP_K) % MOE_PART
    vals = jnp.stack([local, lax.bitcast_convert_type(gates[:, :TOP_K].reshape(-1), jnp.int32)], axis=1)
    init = jnp.broadcast_to(jnp.array([MOE_PART, 0], jnp.int32), (n_rows, 2))
    table = init.at[dest].set(vals)
    row_loc = table[:, 0]
    row_gate = lax.bitcast_convert_type(table[:, 1], F32)
    return row_loc, row_gate, (padded // MOE_ROWS).reshape(-1).astype(jnp.int32), \
        (row_start // MOE_ROWS).reshape(-1).astype(jnp.int32)


def moe_layer(x, p, shift, scale, gate_mod, tokens_per_group):
    hp, idx, gates, rank, cnt = moe_router(x, p['norm2'], shift, scale, p['rw_pad'], p['rb_pad'], tokens_per_group)
    counts = cnt[:, 0, :N_EXPERTS].astype(jnp.int32)
    row_loc, row_gate, n_blk, blk_start = _assignment_tables(idx[:, :TOP_K], rank[:, :TOP_K], gates, counts)
    y = moe_experts(hp, n_blk, blk_start, row_loc, row_gate, p['w_gu_bf'], p['b_gu'], p['w_down_bf'], p['b_down'])
    return gated_residual(y, x, gate_mod, tokens_per_group)


def _pad_lanes(a, width):
    return jnp.pad(a, [(0, 0)] * (a.ndim - 1) + [(0, width - a.shape[-1])])


def _prep_common(p):
    p['rw_pad'] = _pad_lanes(p['router_w'], LANES)
    p['rb_pad'] = jnp.concatenate(
        [p['router_b'].astype(F32), jnp.full((LANES - N_EXPERTS,), NEG_BIG, F32)]).reshape(1, LANES)
    p['w_out'] = p['w_out'].astype(BF16)
    p['w_gu_bf'] = p['w_gu'].astype(BF16)
    p['w_down_bf'] = p['w_down'].astype(BF16)
    return p


def _prep_layer0(p):
    p = _prep_common(dict(p))
    p['w_in'] = _pad_lanes(p['w_in'], AB_IN_PAD).astype(BF16)
    q_b = p['q_b'].reshape(MLA_Q_LORA, MLA_HEADS, MLA_QK)
    p['q_b_pad'] = _pad_lanes(q_b, LANES).reshape(MLA_Q_LORA, MLA_HEADS * LANES).astype(BF16)
    kv_b = p['kv_b'].reshape(MLA_KV_LORA, MLA_HEADS, MLA_NOPE + MLA_V)
    p['wk_pad'] = _pad_lanes(kv_b[:, :, :MLA_NOPE], LANES).reshape(MLA_KV_LORA, MLA_HEADS * LANES).astype(BF16)
    p['wv'] = kv_b[:, :, MLA_NOPE:].reshape(MLA_KV_LORA, MLA_HEADS * MLA_V).astype(BF16)
    p['q_norm_pad'] = _pad_lanes(p['q_norm'].reshape(1, -1), LANES)
    p['k_norm_pad'] = _pad_lanes(p['k_norm'].reshape(1, -1), LANES)
    return p


def _prep_layer1(p):
    p = _prep_common(dict(p))
    p['w_in'] = p['w_in'].astype(BF16)
    p['q_norm2'] = jnp.tile(p['q_norm'].reshape(1, -1), (1, LANES // SWA_HD))
    p['k_norm2'] = jnp.tile(p['k_norm'].reshape(1, -1), (1, LANES // SWA_HD))
    return p


def _group_forward(x3, mods, p0, p1, lb_logits, caches, latent):
    batch, seq, d = x3.shape
    t = batch * seq
    x = x3.reshape(t, d)
    tpg = seq if latent else t
    sh1, sc1, g1, sh2, sc2, g2 = mods[0]

    z = modnorm_matmul(x, p0['norm1'], sh1, sc1, p0['w_in'], tpg)
    tab_b = _rope_lane_tables(seq, MLA_ROPE, LANES, MLA_NOPE) if latent else None
    s0f, s0b = (caches['hg_f'], caches['hg_b']) if latent else (None, None)
    o_f, o_b, s_f, s_b = hgrn_bidir(z, lb_logits, s0f, s0b, batch, seq)
    q = mla_queries(z, p0['qa_norm'], p0['q_b_pad'], p0['q_norm_pad'], tab_b, seq)
    ckv, k, v = mla_keys_values(z, 2816 // LANES, z, 2944 // LANES, p0['kva_norm'], p0['wk_pad'], p0['wv'],
                                p0['k_norm_pad'], tab_b, seq, norm_input=True)
    n_k = seq
    if latent:
        n_ctx = caches['ckv'].shape[1]
        ckv_c = caches['ckv'].reshape(batch * n_ctx, MLA_KV_LORA)
        kpe_c = _pad_lanes(caches['kpe'].reshape(batch * n_ctx, MLA_ROPE), LANES)
        _, k_c, v_c = mla_keys_values(ckv_c, 0, kpe_c, 0, p0['kva_norm'], p0['wk_pad'], p0['wv'],
                                      p0['k_norm_pad'], None, n_ctx, norm_input=False)
        cat = lambda a, b: jnp.concatenate([a.reshape(batch, n_ctx, -1), b.reshape(batch, seq, -1)],
                                           axis=1).reshape(batch * (n_ctx + seq), -1)
        k, v = cat(k_c, k), cat(v_c, v)
        n_k = n_ctx + seq
    o_mla = mla_attention(q, k, v, batch, seq, n_k)
    x = out_proj_layer0(o_f, o_b, z, p0['hg_out_norm'], o_mla, p0['w_out'], x, g1, tpg)
    x = moe_layer(x, p0, sh2, sc2, g2, tpg)
    state0 = (s_f, s_b, ckv.reshape(batch, seq, MLA_KV_LORA), z[:, 2944:2944 + MLA_ROPE].reshape(batch, seq, MLA_ROPE))

    sh1, sc1, g1, sh2, sc2, g2 = mods[1]
    z = modnorm_matmul(x, p1['norm1'], sh1, sc1, p1['w_in'], tpg)
    tab_c = _rope_lane_tables(seq, SWA_HD, SWA_HD, 0) if latent else None
    q, k_cache, k, v = swa_prep(z, p1['q_norm2'], p1['k_norm2'], tab_c, seq)
    sink = p1['sink'].astype(F32)
    if latent:
        n_ctx = caches['k1'].shape[1]
        k_c = caches['k1'].reshape(batch, n_ctx, -1).astype(BF16)
        v_c = caches['v1'].reshape(batch, n_ctx, -1).astype(BF16)
        a = swa_window(q, k, v, k_c, v_c, sink, batch, seq, n_ctx)
    else:
        a = swa_dense(q, k, v, sink, batch, seq)
    x = out_proj_layer1(a, p1['w_out'], x, g1, tpg)
    x = moe_layer(x, p1, sh2, sc2, g2, tpg)
    nkv = SWA_KV_HEADS * SWA_HD
    state1 = (k_cache.reshape(batch, seq, SWA_KV_HEADS, SWA_HD),
              z[:, SWA_HEADS * SWA_HD + nkv:].reshape(batch, seq, SWA_KV_HEADS, SWA_HD))
    return x.reshape(batch, seq, d), state0, state1


def kernel(x_prompt, x_sample, state_l0_hgrn_fwd, state_l0_hgrn_bwd, cache_l0_mla_ckv, cache_l0_mla_kpe, cache_l1_k, cache_l1_v, c, c_ctx, hgrn_lb_logits, l0_ada_w, l0_ada_b, l0_norm1, l0_norm2, l0_w_in, l0_hg_out_norm, l0_qa_norm, l0_q_b, l0_kva_norm, l0_kv_b, l0_q_norm, l0_k_norm, l0_w_out, l0_router_w, l0_router_b, l0_w_gu, l0_b_gu, l0_w_down, l0_b_down, l1_ada_w, l1_ada_b, l1_norm1, l1_norm2, l1_w_in, l1_q_norm, l1_k_norm, l1_sink, l1_w_out, l1_router_w, l1_router_b, l1_w_gu, l1_b_gu, l1_w_down, l1_b_down):
    p0 = _prep_layer0(dict(norm1=l0_norm1, norm2=l0_norm2, w_in=l0_w_in, hg_out_norm=l0_hg_out_norm,
                           qa_norm=l0_qa_norm, q_b=l0_q_b, kva_norm=l0_kva_norm, kv_b=l0_kv_b,
                           q_norm=l0_q_norm, k_norm=l0_k_norm, w_out=l0_w_out, router_w=l0_router_w,
                           router_b=l0_router_b, w_gu=l0_w_gu, b_gu=l0_b_gu, w_down=l0_w_down,
                           b_down=l0_b_down))
    p1 = _prep_layer1(dict(norm1=l1_norm1, norm2=l1_norm2, w_in=l1_w_in, q_norm=l1_q_norm, k_norm=l1_k_norm,
                           sink=l1_sink, w_out=l1_w_out, router_w=l1_router_w, router_b=l1_router_b,
                           w_gu=l1_w_gu, b_gu=l1_b_gu, w_down=l1_w_down, b_down=l1_b_down))
    dec_batch = c.shape[0]
    d = c.shape[1]
    cond8 = jnp.concatenate([c_ctx[None, :], c, jnp.zeros((8 - 1 - dec_batch, d), F32)], axis=0)
    mods_ctx, mods_lat = [], []
    for w, b in ((l0_ada_w, l0_ada_b), (l1_ada_w, l1_ada_b)):
        mod = ada_params(cond8, w, b)
        mods_ctx.append([m.reshape(1, 1, d) for m in jnp.split(mod[0:1], 6, axis=-1)])
        mods_lat.append([m.reshape(dec_batch, 1, d) for m in jnp.split(mod[1:1 + dec_batch], 6, axis=-1)])

    y_prompt, st0, st1 = _group_forward(x_prompt, mods_ctx, p0, p1, hgrn_lb_logits, None, latent=False)
    caches = dict(hg_f=state_l0_hgrn_fwd, hg_b=state_l0_hgrn_bwd, ckv=cache_l0_mla_ckv, kpe=cache_l0_mla_kpe,
                  k1=cache_l1_k, v1=cache_l1_v)
    y_sample, _, _ = _group_forward(x_sample, mods_lat, p0, p1, hgrn_lb_logits, caches, latent=True)
    return (y_prompt, y_sample, st0[0], st0[1], st0[2], st0[3], st1[0], st1[1])
```

```python
import functools

import numpy as np
import jax
import jax.numpy as jnp
from jax import lax
from jax.experimental import pallas as pl
from jax.experimental.pallas import tpu as pltpu

F32 = jnp.float32
BF16 = jnp.bfloat16

D_MODEL = 1024
GRID_W = 64
ROPE_THETA = 10000.0
EPS = 1e-6
HG_HEADS = 4
HG_DK = 128
HG_DV = 128
MLA_HEADS = 8
MLA_NOPE = 64
MLA_ROPE = 32
MLA_V = 64
MLA_QK = MLA_NOPE + MLA_ROPE
MLA_Q_LORA = 256
MLA_KV_LORA = 128
MLA_SCALE = MLA_QK ** -0.5
SWA_HEADS = 16
SWA_KV_HEADS = 4
SWA_HD = 64
SWA_WINDOW = 128
SWA_SCALE = SWA_HD ** -0.5
SWA_GROUP = SWA_HEADS // SWA_KV_HEADS
N_EXPERTS = 32
TOP_K = 4
D_FF = 1024
SWIGLU_LIMIT = 7.0
SWIGLU_ALPHA = 1.702

LANES = 128
HG_CHUNK = 128
HG_LEVELS = 7
HG_MXU_LEVELS = 3
AB_IN_PAD = 3072
MOE_ROWS = 256
NEG_BIG = -1e30
VMEM_LIMIT = 48 * 1024 * 1024
EXPERT_VMEM_LIMIT = 56 * 1024 * 1024


def _cparams(sem):
    return pltpu.CompilerParams(dimension_semantics=sem, vmem_limit_bytes=VMEM_LIMIT)


def _dot(a, b):
    return jnp.dot(a, b, preferred_element_type=F32)


def _dot_nt(a, b):
    return lax.dot_general(a, b, (((1,), (1,)), ((), ())), preferred_element_type=F32)


def _dot_tn(a, b):
    return lax.dot_general(a, b, (((0,), (0,)), ((), ())), preferred_element_type=F32)


def _split2(x):
    hi = x.astype(BF16)
    lo = (x - hi.astype(F32)).astype(BF16)
    return hi, lo


def _dot_hp(a, b):
    ah, al = _split2(a)
    bh, bl = _split2(b)
    return _dot(ah, bh) + _dot(ah, bl) + _dot(al, bh)


def _sigmoid(x):
    return 1.0 / (1.0 + jnp.exp(-x))


def _modnorm(x, gain, shift, scale):
    y = x * lax.rsqrt(jnp.mean(x * x, axis=-1, keepdims=True) + EPS)
    return y * gain * (1.0 + scale) + shift


def _ada_kernel(c_ref, w_ref, b_ref, o_ref):
    c = c_ref[...]
    o_ref[...] = _dot_hp(c * _sigmoid(c), w_ref[...]) + b_ref[...]


def ada_params(cond8, w, b):
    n = w.shape[1]
    tn = 1024
    return pl.pallas_call(
        _ada_kernel,
        grid=(n // tn,),
        in_specs=[pl.BlockSpec((8, D_MODEL), lambda j: (0, 0)),
                  pl.BlockSpec((D_MODEL, tn), lambda j: (0, j)),
                  pl.BlockSpec((1, tn), lambda j: (0, j))],
        out_specs=pl.BlockSpec((8, tn), lambda j: (0, j)),
        out_shape=jax.ShapeDtypeStruct((8, n), F32),
        compiler_params=_cparams(("parallel",)),
        name="ada_params",
    )(cond8, w, b.reshape(1, n))


def _modnorm_matmul_kernel(x_ref, g_ref, sh_ref, sc_ref, w_ref, o_ref):
    h = _modnorm(x_ref[...], g_ref[...], sh_ref[0], sc_ref[0])
    o_ref[...] = _dot(h.astype(BF16), w_ref[...])


def modnorm_matmul(x, gain, shift, scale, w_bf16, tokens_per_group, tm=256):
    t, d = x.shape
    n = w_bf16.shape[1]
    tiles_per_group = tokens_per_group // tm
    grp = lambda i: (i // tiles_per_group, 0, 0)
    return pl.pallas_call(
        _modnorm_matmul_kernel,
        grid=(t // tm,),
        in_specs=[pl.BlockSpec((tm, d), lambda i: (i, 0)),
                  pl.BlockSpec((1, d), lambda i: (0, 0)),
                  pl.BlockSpec((1, 1, d), grp),
                  pl.BlockSpec((1, 1, d), grp),
                  pl.BlockSpec((d, n), lambda i: (0, 0))],
        out_specs=pl.BlockSpec((tm, n), lambda i: (i, 0)),
        out_shape=jax.ShapeDtypeStruct((t, n), F32),
        compiler_params=_cparams(("parallel",)),
        name="modnorm_matmul",
    )(x, gain.reshape(1, d), shift, scale, w_bf16)


def _hgrn_constants():
    c = HG_CHUNK
    t = np.arange(c)[:, None]
    u = np.arange(c)[None, :]
    mats = [(u <= t), (u > t)]
    for l in range(HG_MXU_LEVELS):
        m = 1 << l
        r = (t // (2 * m)) * (2 * m) + m - 1
        mats.append((u > np.minimum(t, r)) & (u <= np.maximum(t, r)))
    fwd = np.concatenate(mats, axis=0).astype(np.float32)
    bwd = np.concatenate([mm[::-1, ::-1] for mm in mats], axis=0).astype(np.float32)
    x = np.bitwise_xor(t, u)
    lvl = np.where(x > 0, np.floor(np.log2(np.maximum(x, 1))), HG_LEVELS).astype(np.int32)
    lv_f = np.where(t >= u, lvl, -1).astype(np.int32)
    return fwd, bwd, lv_f, lv_f.T.copy()


def _hgrn_direction(qs, fpres, vs, lbs, mcat, lv, sts, forward):
    c = HG_CHUNK
    n = len(qs)
    fs = [lb + (1.0 - lb) * _sigmoid(fp) for lb, fp in zip(lbs, fpres)]
    kks = [1.0 - f for f in fs]
    parts = []
    for f in fs:
        parts += list(_split2(jnp.log(f)))
    x_all = _dot(mcat, jnp.concatenate(parts, axis=1))
    xs = [x_all[:, 2 * i * c:(2 * i + 1) * c] + x_all[:, (2 * i + 1) * c:(2 * i + 2) * c] for i in range(n)]
    gs = [x[0:c] for x in xs]
    qbs = [q.astype(BF16) for q in qs]
    kbs = [kk.astype(BF16) for kk in kks]
    vbs = [v.astype(BF16) for v in vs]
    os_ = [_dot_nt((q * jnp.exp(g)).astype(BF16), st.astype(BF16)) for q, g, st in zip(qs, gs, sts)]
    accs = [jnp.where(lv == HG_LEVELS, _dot_nt(qb, kb), 0.0) for qb, kb in zip(qbs, kbs)]
    for l in range(HG_LEVELS):
        for i in range(n):
            if l < HG_MXU_LEVELS:
                x = xs[i][(2 + l) * c:(3 + l) * c]
            else:
                m = 1 << l
                ref_rows = [j * 2 * m + (m - 1 if forward else m) for j in range(c // (2 * m))]
                g_ref = jnp.concatenate(
                    [jnp.broadcast_to(gs[i][r:r + 1, :], (2 * m, c)) for r in ref_rows], axis=0)
                x = -jnp.abs(gs[i] - g_ref)
            e = jnp.exp(x)
            p = _dot_nt((qs[i] * e).astype(BF16), (kks[i] * e).astype(BF16))
            accs[i] = jnp.where(lv == l, p, accs[i])
    edge_row = c - 1 if forward else 0
    outs, new_sts = [], []
    for i in range(n):
        outs.append(os_[i] + _dot(accs[i].astype(BF16), vbs[i]))
        k_end = (kks[i] * jnp.exp(xs[i][c:2 * c])).astype(BF16)
        new_sts.append(sts[i] * jnp.exp(gs[i][edge_row:edge_row + 1, :]) + _dot_tn(vbs[i], k_end))
    return outs, new_sts


def _hgrn_kernel(*refs, has_init):
    if has_init:
        (qf_ref, qb_ref, ff_ref, fb_ref, vf_ref, vb_ref, lbl_ref, mf_ref, mb_ref, lvf_ref, lvb_ref,
         s0f_ref, s0b_ref, of_ref, ob_ref, sf_ref, sb_ref, stf, stb) = refs
    else:
        (qf_ref, qb_ref, ff_ref, fb_ref, vf_ref, vb_ref, lbl_ref, mf_ref, mb_ref, lvf_ref, lvb_ref,
         of_ref, ob_ref, sf_ref, sb_ref, stf, stb) = refs
    c = pl.program_id(1)
    nc = pl.num_programs(1)

    @pl.when(c == 0)
    def _():
        for h in range(HG_HEADS):
            if has_init:
                stf[h] = s0f_ref[0, h].T
                stb[h] = s0b_ref[0, h].T
            else:
                stf[h] = jnp.zeros((HG_DV, HG_DK), F32)
                stb[h] = jnp.zeros((HG_DV, HG_DK), F32)

    rows = [lbl_ref[:, j, :] for j in range(lbl_ref.shape[1])]
    mx = functools.reduce(jnp.maximum, rows)
    ex = [jnp.exp(r - mx) for r in rows]
    lb = ex[0] / functools.reduce(lambda a, b: a + b, ex)

    heads = [slice(h * LANES, (h + 1) * LANES) for h in range(HG_HEADS)]
    o_f, st_f = _hgrn_direction([qf_ref[0, :, hs] for hs in heads], [ff_ref[0, :, hs] for hs in heads],
                                [vf_ref[0, :, hs] for hs in heads], [lb[0:1, hs] for hs in heads],
                                mf_ref[...], lvf_ref[...], [stf[h] for h in range(HG_HEADS)], True)
    o_b, st_b = _hgrn_direction([qb_ref[0, :, hs] for hs in heads], [fb_ref[0, :, hs] for hs in heads],
                                [vb_ref[0, :, hs] for hs in heads], [lb[1:2, hs] for hs in heads],
                                mb_ref[...], lvb_ref[...], [stb[h] for h in range(HG_HEADS)], False)
    for h, hs in enumerate(heads):
        of_ref[0, :, hs] = o_f[h]
        ob_ref[0, :, hs] = o_b[h]
        stf[h] = st_f[h]
        stb[h] = st_b[h]

    @pl.when(c == nc - 1)
    def _():
        for h in range(HG_HEADS):
            sf_ref[0, h] = stf[h].T
            sb_ref[0, h] = stb[h].T


def hgrn_bidir(z, lb_logits, s0f, s0b, batch, seq):
    nc = seq // HG_CHUNK
    z3 = z.reshape(batch, seq, z.shape[1])
    mf, mb, lvf, lvb = _hgrn_constants()
    has_init = s0f is not None
    width = HG_HEADS * LANES
    blk = (1, HG_CHUNK, width)
    fwd = lambda off: pl.BlockSpec(blk, lambda b, c: (b, c, off))
    bwd = lambda off: pl.BlockSpec(blk, lambda b, c: (b, nc - 1 - c, off))
    full = lambda a: pl.BlockSpec(a.shape, lambda b, c: (0,) * a.ndim)
    st_spec = pl.BlockSpec((1, HG_HEADS, HG_DK, HG_DV), lambda b, c: (b, 0, 0, 0))
    consts = [jnp.asarray(mf, BF16), jnp.asarray(mb, BF16), jnp.asarray(lvf), jnp.asarray(lvb)]
    in_specs = [fwd(0), bwd(0), fwd(1), bwd(2), fwd(3), bwd(3),
                pl.BlockSpec(lb_logits.shape, lambda b, c: (0, 0, 0))]
    in_specs += [full(a) for a in consts]
    args = [z3] * 6 + [lb_logits] + consts
    if has_init:
        in_specs += [st_spec, st_spec]
        args += [s0f, s0b]
    o_shape = jax.ShapeDtypeStruct((batch, seq, width), F32)
    s_shape = jax.ShapeDtypeStruct((batch, HG_HEADS, HG_DK, HG_DV), F32)
    o_f, o_b, s_f, s_b = pl.pallas_call(
        functools.partial(_hgrn_kernel, has_init=has_init),
        grid=(batch, nc),
        in_specs=in_specs,
        out_specs=[pl.BlockSpec(blk, lambda b, c: (b, c, 0)),
                   pl.BlockSpec(blk, lambda b, c: (b, nc - 1 - c, 0)),
                   st_spec, st_spec],
        out_shape=[o_shape, o_shape, s_shape, s_shape],
        scratch_shapes=[pltpu.VMEM((HG_HEADS, HG_DV, HG_DK), F32), pltpu.VMEM((HG_HEADS, HG_DV, HG_DK), F32)],
        compiler_params=_cparams(("parallel", "arbitrary")),
        name="hgrn_bidir",
    )(*args)
    t = batch * seq
    return o_f.reshape(t, -1), o_b.reshape(t, -1), s_f, s_b


def _axial_tables(n_tokens, n_rot):
    t = jnp.arange(n_tokens)
    row = (t // GRID_W).astype(F32)
    col = (t % GRID_W).astype(F32)
    n_freq = n_rot // 4
    inv = jnp.power(ROPE_THETA, -jnp.arange(n_freq, dtype=F32) / n_freq)
    ang = jnp.concatenate([row[:, None] * inv, col[:, None] * inv], axis=-1)
    return jnp.cos(ang), jnp.sin(ang)


def _rope_lane_tables(n_tokens, n_rot, head_width, first_rot_lane):
    cos, sin = _axial_tables(n_tokens, n_rot)
    half = n_rot // 2
    c_head = jnp.ones((n_tokens, head_width), F32)
    sa_head = jnp.zeros((n_tokens, head_width), F32)
    sb_head = jnp.zeros((n_tokens, head_width), F32)
    a0, a1, a2 = first_rot_lane, first_rot_lane + half, first_rot_lane + n_rot
    c_head = c_head.at[:, a0:a1].set(cos).at[:, a1:a2].set(cos)
    sa_head = sa_head.at[:, a0:a1].set(-sin)
    sb_head = sb_head.at[:, a1:a2].set(sin)
    reps = LANES // head_width
    tile = lambda a: jnp.tile(a, (1, reps))
    return tile(c_head), tile(sa_head), tile(sb_head)


def _rope(x, c, sa, sb, half):
    return x * c + pltpu.roll(x, LANES - half, 1) * sa + pltpu.roll(x, half, 1) * sb


def _mla_q_kernel(*refs, rope):
    if rope:
        qa_ref, qan_ref, qb_ref, qn_ref, c_ref, sa_ref, sb_ref, o_ref = refs
    else:
        qa_ref, qan_ref, qb_ref, qn_ref, o_ref = refs
    qa = qa_ref[...]
    qn = qa * lax.rsqrt(jnp.mean(qa * qa, axis=-1, keepdims=True) + EPS) * qan_ref[...]
    qfull = _dot(qn.astype(BF16), qb_ref[...])
    outs = []
    for h in range(MLA_HEADS):
        qh = qfull[:, h * LANES:(h + 1) * LANES]
        ms = jnp.sum(qh * qh, axis=-1, keepdims=True) * (1.0 / MLA_QK)
        qh = qh * lax.rsqrt(ms + EPS) * qn_ref[...]
        if rope:
            qh = _rope(qh, c_ref[...], sa_ref[...], sb_ref[...], MLA_ROPE // 2)
        outs.append((qh * MLA_SCALE).astype(BF16))
    o_ref[...] = jnp.concatenate(outs, axis=1)


def mla_queries(z, qa_norm, q_b_pad, q_norm_pad, tables, tokens_per_batch, tm=256):
    t = z.shape[0]
    rope = tables is not None
    row = lambda i: (0, 0)
    in_specs = [pl.BlockSpec((tm, MLA_Q_LORA), lambda i: (i, 2560 // MLA_Q_LORA)),
                pl.BlockSpec((1, MLA_Q_LORA), row),
                pl.BlockSpec(q_b_pad.shape, row),
                pl.BlockSpec((1, LANES), row)]
    args = [z, qa_norm.reshape(1, -1), q_b_pad, q_norm_pad]
    if rope:
        tpb = tokens_per_batch // tm
        in_specs += [pl.BlockSpec((tm, LANES), lambda i: (i % tpb, 0))] * 3
        args += list(tables)
    return pl.pallas_call(
        functools.partial(_mla_q_kernel, rope=rope),
        grid=(t // tm,),
        in_specs=in_specs,
        out_specs=pl.BlockSpec((tm, MLA_HEADS * LANES), lambda i: (i, 0)),
        out_shape=jax.ShapeDtypeStruct((t, MLA_HEADS * LANES), BF16),
        compiler_params=_cparams(("parallel",)),
        name="mla_queries",
    )(*args)


def _mla_kv_kernel(*refs, norm_input, rope):
    if rope:
        kva_ref, kpe_ref, kvan_ref, wk_ref, wv_ref, kn_ref, c_ref, sa_ref, sb_ref, ckv_ref, k_ref, v_ref = refs
    else:
        kva_ref, kpe_ref, kvan_ref, wk_ref, wv_ref, kn_ref, ckv_ref, k_ref, v_ref = refs
    ckv = kva_ref[...]
    if norm_input:
        ckv = ckv * lax.rsqrt(jnp.mean(ckv * ckv, axis=-1, keepdims=True) + EPS) * kvan_ref[...]
    ckv_ref[...] = ckv
    cb = ckv.astype(BF16)
    knope = _dot(cb, wk_ref[...])
    v_ref[...] = _dot(cb, wv_ref[...]).astype(BF16)
    kpe = pltpu.roll(kpe_ref[...], MLA_NOPE, 1)
    outs = []
    for h in range(MLA_HEADS):
        kh = knope[:, h * LANES:(h + 1) * LANES] + kpe
        ms = jnp.sum(kh * kh, axis=-1, keepdims=True) * (1.0 / MLA_QK)
        kh = kh * lax.rsqrt(ms + EPS) * kn_ref[...]
        if rope:
            kh = _rope(kh, c_ref[...], sa_ref[...], sb_ref[...], MLA_ROPE // 2)
        outs.append(kh.astype(BF16))
    k_ref[...] = jnp.concatenate(outs, axis=1)


def mla_keys_values(kva_src, kva_col, kpe_src, kpe_col, kva_norm, wk_pad, wv, k_norm_pad, tables,
                    tokens_per_batch, norm_input, tm=256):
    t = kva_src.shape[0]
    rope = tables is not None
    row = lambda i: (0, 0)
    in_specs = [pl.BlockSpec((tm, LANES), lambda i: (i, kva_col)),
                pl.BlockSpec((tm, LANES), lambda i: (i, kpe_col)),
                pl.BlockSpec((1, LANES), row),
                pl.BlockSpec(wk_pad.shape, row),
                pl.BlockSpec(wv.shape, row),
                pl.BlockSpec((1, LANES), row)]
    args = [kva_src, kpe_src, kva_norm.reshape(1, -1), wk_pad, wv, k_norm_pad]
    if rope:
        tpb = tokens_per_batch // tm
        in_specs += [pl.BlockSpec((tm, LANES), lambda i: (i % tpb, 0))] * 3
        args += list(tables)
    return pl.pallas_call(
        functools.partial(_mla_kv_kernel, norm_input=norm_input, rope=rope),
        grid=(t // tm,),
        in_specs=in_specs,
        out_specs=[pl.BlockSpec((tm, LANES), lambda i: (i, 0)),
                   pl.BlockSpec((tm, MLA_HEADS * LANES), lambda i: (i, 0)),
                   pl.BlockSpec((tm, MLA_HEADS * MLA_V), lambda i: (i, 0))],
        out_shape=[jax.ShapeDtypeStruct((t, LANES), F32),
                   jax.ShapeDtypeStruct((t, MLA_HEADS * LANES), BF16),
                   jax.ShapeDtypeStruct((t, MLA_HEADS * MLA_V), BF16)],
        compiler_params=_cparams(("parallel",)),
        name="mla_keys_values",
    )(*args)


def _mla_attn_kernel(q_ref, k_ref, v_ref, o_ref):
    outs = []
    for j in range(2):
        q = q_ref[0][:, j * LANES:(j + 1) * LANES]
        k = k_ref[0][:, j * LANES:(j + 1) * LANES]
        v = v_ref[0][:, j * MLA_V:(j + 1) * MLA_V]
        s = _dot_nt(q, k)
        p = jnp.exp(s - jnp.max(s, axis=-1, keepdims=True))
        l = jnp.sum(p, axis=-1, keepdims=True)
        outs.append(_dot(p.astype(BF16), v) / l)
    o_ref[0] = jnp.concatenate(outs, axis=1)


def mla_attention(q, k, v, batch, n_q, n_k, tq=256):
    q3 = q.reshape(batch, n_q, -1)
    k3 = k.reshape(batch, n_k, -1)
    v3 = v.reshape(batch, n_k, -1)
    out = pl.pallas_call(
        _mla_attn_kernel,
        grid=(batch, MLA_HEADS // 2, n_q // tq),
        in_specs=[pl.BlockSpec((1, tq, 2 * LANES), lambda b, j, i: (b, i, j)),
                  pl.BlockSpec((1, n_k, 2 * LANES), lambda b, j, i: (b, 0, j)),
                  pl.BlockSpec((1, n_k, 2 * MLA_V), lambda b, j, i: (b, 0, j))],
        out_specs=pl.BlockSpec((1, tq, 2 * MLA_V), lambda b, j, i: (b, i, j)),
        out_shape=jax.ShapeDtypeStruct((batch, n_q, MLA_HEADS * MLA_V), F32),
        compiler_params=_cparams(("parallel", "parallel", "arbitrary")),
        name="mla_attention",
    )(q3, k3, v3)
    return out.reshape(batch * n_q, -1)


def _out0_kernel(of_ref, ob_ref, ag_ref, hgn_ref, om_ref, w_ref, x_ref, g_ref, o_ref):
    o = of_ref[...] + ob_ref[...]
    ag = ag_ref[...]
    parts = []
    for h in range(HG_HEADS):
        oh = o[:, h * HG_DV:(h + 1) * HG_DV]
        oh = oh * lax.rsqrt(jnp.mean(oh * oh, axis=-1, keepdims=True) + EPS) * hgn_ref[...]
        gh = ag[:, h * HG_DV:(h + 1) * HG_DV]
        parts.append((oh * (gh * _sigmoid(gh))).astype(BF16))
    oa = jnp.concatenate(parts, axis=1)
    n_a = HG_HEADS * HG_DV
    mix = _dot(oa, w_ref[0:n_a, :]) + _dot(om_ref[...].astype(BF16), w_ref[n_a:, :])
    o_ref[...] = x_ref[...] + g_ref[0] * mix


def out_proj_layer0(o_f, o_b, z, hg_norm, o_mla, w_out_bf16, x, gate, tokens_per_group, tm=256):
    t, d = x.shape
    n_a = HG_HEADS * HG_DV
    tiles_per_group = tokens_per_group // tm
    tile = lambda w: pl.BlockSpec((tm, w), lambda i: (i, 0))
    return pl.pallas_call(
        _out0_kernel,
        grid=(t // tm,),
        in_specs=[tile(n_a), tile(n_a),
                  pl.BlockSpec((tm, n_a), lambda i: (i, 2048 // n_a)),
                  pl.BlockSpec((1, HG_DV), lambda i: (0, 0)),
                  tile(o_mla.shape[1]),
                  pl.BlockSpec(w_out_bf16.shape, lambda i: (0, 0)),
                  tile(d),
                  pl.BlockSpec((1, 1, d), lambda i: (i // tiles_per_group, 0, 0))],
        out_specs=tile(d),
        out_shape=jax.ShapeDtypeStruct((t, d), F32),
        compiler_params=_cparams(("parallel",)),
        name="out_proj_layer0",
    )(o_f, o_b, z, hg_norm.reshape(1, -1), o_mla, w_out_bf16, x, gate)


def _out1_kernel(a_ref, w_ref, x_ref, g_ref, o_ref):
    o_ref[...] = x_ref[...] + g_ref[0] * _dot(a_ref[...].astype(BF16), w_ref[...])


def out_proj_layer1(a, w_out_bf16, x, gate, tokens_per_group, tm=256):
    t, d = x.shape
    tiles_per_group = tokens_per_group // tm
    tile = lambda w: pl.BlockSpec((tm, w), lambda i: (i, 0))
    return pl.pallas_call(
        _out1_kernel,
        grid=(t // tm,),
        in_specs=[tile(a.shape[1]),
                  pl.BlockSpec(w_out_bf16.shape, lambda i: (0, 0)),
                  tile(d),
                  pl.BlockSpec((1, 1, d), lambda i: (i // tiles_per_group, 0, 0))],
        out_specs=tile(d),
        out_shape=jax.ShapeDtypeStruct((t, d), F32),
        compiler_params=_cparams(("parallel",)),
        name="out_proj_layer1",
    )(a, w_out_bf16, x, gate)


def _head_rms(x, gain2):
    sq = x * x
    lane = lax.broadcasted_iota(jnp.int32, x.shape, 1)
    first = lane < SWA_HD
    lo = jnp.sum(jnp.where(first, sq, 0.0), axis=-1, keepdims=True)
    hi = jnp.sum(jnp.where(first, 0.0, sq), axis=-1, keepdims=True)
    ms = jnp.where(first, lo, hi) * (1.0 / SWA_HD)
    return x * lax.rsqrt(ms + EPS) * gain2


def _swa_prep_kernel(*refs, rope):
    if rope:
        zq_ref, zk_ref, zv_ref, qn_ref, kn_ref, c_ref, sa_ref, sb_ref, q_ref, kc_ref, k_ref, v_ref = refs
    else:
        zq_ref, zk_ref, zv_ref, qn_ref, kn_ref, q_ref, kc_ref, k_ref, v_ref = refs
    half = SWA_HD // 2

    def rot(x):
        return _rope(x, c_ref[...], sa_ref[...], sb_ref[...], half) if rope else x

    zq = zq_ref[...]
    qs = []
    for p in range(zq.shape[1] // LANES):
        x = _head_rms(zq[:, p * LANES:(p + 1) * LANES], qn_ref[...])
        qs.append((rot(x) * SWA_SCALE).astype(BF16))
    q_ref[...] = jnp.concatenate(qs, axis=1)
    zk = zk_ref[...]
    kn, kr = [], []
    for p in range(zk.shape[1] // LANES):
        x = _head_rms(zk[:, p * LANES:(p + 1) * LANES], kn_ref[...])
        kn.append(x)
        kr.append(rot(x).astype(BF16))
    kc_ref[...] = jnp.concatenate(kn, axis=1)
    k_ref[...] = jnp.concatenate(kr, axis=1)
    v_ref[...] = zv_ref[...].astype(BF16)


def swa_prep(z, q_norm2, k_norm2, tables, tokens_per_batch, tm=256):
    t = z.shape[0]
    nq = SWA_HEADS * SWA_HD
    nkv = SWA_KV_HEADS * SWA_HD
    rope = tables is not None
    row = lambda i: (0, 0)
    in_specs = [pl.BlockSpec((tm, nq), lambda i: (i, 0)),
                pl.BlockSpec((tm, nkv), lambda i: (i, nq // nkv)),
                pl.BlockSpec((tm, nkv), lambda i: (i, nq // nkv + 1)),
                pl.BlockSpec((1, LANES), row),
                pl.BlockSpec((1, LANES), row)]
    args = [z, z, z, q_norm2, k_norm2]
    if rope:
        tpb = tokens_per_batch // tm
        in_specs += [pl.BlockSpec((tm, LANES), lambda i: (i % tpb, 0))] * 3
        args += list(tables)
    tile = lambda w: pl.BlockSpec((tm, w), lambda i: (i, 0))
    return pl.pallas_call(
        functools.partial(_swa_prep_kernel, rope=rope),
        grid=(t // tm,),
        in_specs=in_specs,
        out_specs=[tile(nq), tile(nkv), tile(nkv), tile(nkv)],
        out_shape=[jax.ShapeDtypeStruct((t, nq), BF16),
                   jax.ShapeDtypeStruct((t, nkv), F32),
                   jax.ShapeDtypeStruct((t, nkv), BF16),
                   jax.ShapeDtypeStruct((t, nkv), BF16)],
        compiler_params=_cparams(("parallel",)),
        name="swa_prep",
    )(*args)


def _sink_softmax_pv(s, sk, vn):
    m = jnp.maximum(jnp.max(s, axis=-1, keepdims=True), sk)
    p = jnp.exp(s - m)
    l = jnp.sum(p, axis=-1, keepdims=True) + jnp.exp(sk - m)
    return _dot(p.astype(BF16), vn) / l


def _sink_attention(q8, k_all, v_all, bias, sink_ref, pair, stack):
    outs = []
    tq = q8.shape[0]
    for n in range(2):
        kn = k_all[:, n * SWA_HD:(n + 1) * SWA_HD]
        vn = v_all[:, n * SWA_HD:(n + 1) * SWA_HD]
        heads = [n * SWA_GROUP + g for g in range(SWA_GROUP)]
        qs = [q8[:, hq * SWA_HD:(hq + 1) * SWA_HD] for hq in heads]
        sinks = [sink_ref[pair * 2 * SWA_GROUP + hq] for hq in heads]
        if stack:
            sk = jnp.concatenate([jnp.full((tq, 1), s, F32) for s in sinks], axis=0)
            s = _dot_nt(jnp.concatenate(qs, axis=0), kn)
            if bias is not None:
                s = s + bias
            o4 = _sink_softmax_pv(s, sk, vn)
            outs += [o4[g * tq:(g + 1) * tq] for g in range(SWA_GROUP)]
        else:
            for q, sk in zip(qs, sinks):
                s = _dot_nt(q, kn)
                if bias is not None:
                    s = s + bias
                outs.append(_sink_softmax_pv(s, sk, vn))
    return jnp.concatenate(outs, axis=1)


def _swa_dense_kernel(sink_ref, q_ref, k_ref, v_ref, o_ref):
    o_ref[0] = _sink_attention(q_ref[0], k_ref[0], v_ref[0], None, sink_ref, pl.program_id(1), stack=False)


def swa_dense(q, k, v, sink, batch, seq):
    nq = SWA_HEADS * SWA_HD
    q3 = q.reshape(batch, seq, nq)
    k3 = k.reshape(batch, seq, -1)
    v3 = v.reshape(batch, seq, -1)
    out = pl.pallas_call(
        _swa_dense_kernel,
        grid=(batch, 2),
        in_specs=[pl.BlockSpec(memory_space=pltpu.SMEM),
                  pl.BlockSpec((1, seq, nq // 2), lambda b, j: (b, 0, j)),
                  pl.BlockSpec((1, seq, LANES), lambda b, j: (b, 0, j)),
                  pl.BlockSpec((1, seq, LANES), lambda b, j: (b, 0, j))],
        out_specs=pl.BlockSpec((1, seq, nq // 2), lambda b, j: (b, 0, j)),
        out_shape=jax.ShapeDtypeStruct((batch, seq, nq), F32),
        compiler_params=_cparams(("parallel", "parallel")),
        name="swa_dense",
    )(sink, q3, k3, v3)
    return out.reshape(batch * seq, nq)


def _swa_window_kernel(sink_ref, q_ref, kc_ref, vc_ref, kp_ref, k0_ref, kn_ref, vp_ref, v0_ref, vn_ref, o_ref):
    i = pl.program_id(2)
    nb = pl.num_programs(2)
    w = SWA_WINDOW
    n_ctx = kc_ref.shape[1]
    k_all = jnp.concatenate([kc_ref[0], kp_ref[0], k0_ref[0], kn_ref[0]], axis=0)
    v_all = jnp.concatenate([vc_ref[0], vp_ref[0], v0_ref[0], vn_ref[0]], axis=0)
    shape = (SWA_GROUP * w, n_ctx + 3 * w)
    r = lax.broadcasted_iota(jnp.int32, shape, 0) % w
    col = lax.broadcasted_iota(jnp.int32, shape, 1) - n_ctx
    prev_bias = jnp.where(i > 0, 0.0, NEG_BIG)
    next_bias = jnp.where(i < nb - 1, 0.0, NEG_BIG)
    bias = jnp.where(
        col < w,
        jnp.where(col < 0, 0.0, jnp.where(col >= r, prev_bias, NEG_BIG)),
        jnp.where(col < 2 * w, 0.0, jnp.where(col - 2 * w <= r, next_bias, NEG_BIG)))
    o_ref[0] = _sink_attention(q_ref[0], k_all, v_all, bias, sink_ref, pl.program_id(1), stack=True)


def swa_window(q, k, v, k_ctx, v_ctx, sink, batch, seq, n_ctx):
    nq = SWA_HEADS * SWA_HD
    w = SWA_WINDOW
    nb = seq // w
    q3 = q.reshape(batch, seq, nq)
    k3 = k.reshape(batch, seq, -1)
    v3 = v.reshape(batch, seq, -1)
    ctx = pl.BlockSpec((1, n_ctx, LANES), lambda b, j, i: (b, 0, j))
    prv = pl.BlockSpec((1, w, LANES), lambda b, j, i: (b, jnp.maximum(i - 1, 0), j))
    cur = pl.BlockSpec((1, w, LANES), lambda b, j, i: (b, i, j))
    nxt = pl.BlockSpec((1, w, LANES), lambda b, j, i: (b, jnp.minimum(i + 1, nb - 1), j))
    out = pl.pallas_call(
        _swa_window_kernel,
        grid=(batch, 2, nb),
        in_specs=[pl.BlockSpec(memory_space=pltpu.SMEM),
                  pl.BlockSpec((1, w, nq // 2), lambda b, j, i: (b, i, j)),
                  ctx, ctx, prv, cur, nxt, prv, cur, nxt],
        out_specs=pl.BlockSpec((1, w, nq // 2), lambda b, j, i: (b, i, j)),
        out_shape=jax.ShapeDtypeStruct((batch, seq, nq), F32),
        compiler_params=_cparams(("parallel", "parallel", "arbitrary")),
        name="swa_window",
    )(sink, q3, k_ctx, v_ctx, k3, k3, k3, v3, v3, v3)
    return out.reshape(batch * seq, nq)


def _router_kernel(x_ref, g_ref, sh_ref, sc_ref, rw_ref, rb_ref, tri_ref, h_ref, idx_ref, gate_ref, rank_ref,
                   cnt_ref, cnt, *, tiles_per_part):
    @pl.when(pl.program_id(0) % tiles_per_part == 0)
    def _():
        cnt[...] = jnp.zeros_like(cnt)

    h = _modnorm(x_ref[...], g_ref[...], sh_ref[0], sc_ref[0])
    bits = lax.bitcast_convert_type(h.astype(BF16).astype(F32), jnp.uint32)
    half = h.shape[1] // 2
    h_ref[...] = (bits[:, :half] >> 16) | (bits[:, half:] & jnp.uint32(0xFFFF0000))
    logits = _dot_hp(h, rw_ref[...]) + rb_ref[...]
    lane = lax.broadcasted_iota(jnp.int32, logits.shape, 1)
    work = logits
    vals, idxs = [], []
    for _ in range(TOP_K):
        m = jnp.max(work, axis=-1, keepdims=True)
        ik = jnp.min(jnp.where(work == m, lane, LANES), axis=-1, keepdims=True)
        vals.append(m)
        idxs.append(ik)
        work = jnp.where(lane == ik, 2.0 * NEG_BIG, work)
    es = [jnp.exp(v - vals[0]) for v in vals]
    denom = es[0] + es[1] + es[2] + es[3]
    idx_out = jnp.zeros(logits.shape, jnp.int32)
    gate_out = jnp.zeros(logits.shape, F32)
    for k in range(TOP_K):
        idx_out = jnp.where(lane == k, idxs[k], idx_out)
        gate_out = jnp.where(lane == k, es[k] / denom, gate_out)
    idx_ref[...] = idx_out
    gate_ref[...] = gate_out
    chosen = jnp.zeros(logits.shape, F32)
    for k in range(TOP_K):
        chosen = jnp.where(lane == idxs[k], 1.0, chosen)
    before = _dot(tri_ref[...], chosen.astype(BF16)) + cnt[...]
    rank_out = jnp.zeros(logits.shape, jnp.int32)
    for k in range(TOP_K):
        rk = jnp.sum(jnp.where(lane == idxs[k], before, 0.0), axis=-1, keepdims=True)
        rank_out = jnp.where(lane == k, rk.astype(jnp.int32), rank_out)
    rank_ref[...] = rank_out
    cnt[...] = cnt[...] + jnp.sum(chosen, axis=0, keepdims=True)
    cnt_ref[0] = cnt[...]


def moe_router(x, gain, shift, scale, rw_pad, rb_pad, tokens_per_group, tm=256):
    t, d = x.shape
    tiles_per_group = tokens_per_group // tm
    tiles_per_part = MOE_PART // tm
    grp = lambda i: (i // tiles_per_group, 0, 0)
    tile = lambda w: pl.BlockSpec((tm, w), lambda i: (i, 0))
    tri = jnp.asarray(np.tril(np.ones((tm, tm), np.float32), -1), BF16)
    return pl.pallas_call(
        functools.partial(_router_kernel, tiles_per_part=tiles_per_part),
        grid=(t // tm,),
        in_specs=[tile(d),
                  pl.BlockSpec((1, d), lambda i: (0, 0)),
                  pl.BlockSpec((1, 1, d), grp),
                  pl.BlockSpec((1, 1, d), grp),
                  pl.BlockSpec((d, LANES), lambda i: (0, 0)),
                  pl.BlockSpec((1, LANES), lambda i: (0, 0)),
                  pl.BlockSpec((tm, tm), lambda i: (0, 0))],
        out_specs=[tile(d // 2), tile(LANES), tile(LANES), tile(LANES),
                   pl.BlockSpec((1, 1, LANES), lambda i: (i // tiles_per_part, 0, 0))],
        out_shape=[jax.ShapeDtypeStruct((t, d // 2), jnp.uint32),
                   jax.ShapeDtypeStruct((t, LANES), jnp.int32),
                   jax.ShapeDtypeStruct((t, LANES), F32),
                   jax.ShapeDtypeStruct((t, LANES), jnp.int32),
                   jax.ShapeDtypeStruct((t // MOE_PART, 1, LANES), F32)],
        scratch_shapes=[pltpu.VMEM((1, LANES), F32)],
        compiler_params=_cparams(("arbitrary",)),
        name="moe_router",
    )(x, gain.reshape(1, d), shift, scale, rw_pad, rb_pad, tri)


FFN_COLS = 512
FFN_PIECES = 2 * D_FF // FFN_COLS + D_MODEL // FFN_COLS
MOE_PART = 4096
ADD_BATCH = 8


def _unpack_rows(words):
    lo = lax.bitcast_convert_type(words << 16, F32)
    hi = lax.bitcast_convert_type(words & jnp.uint32(0xFFFF0000), F32)
    return jnp.concatenate([lo, hi], axis=1).astype(BF16)


def _ffn_block(x_ref, y_ref, wgu_ref, bgu_ref, wd_ref, bd_ref, between):
    xb = _unpack_rows(x_ref[...])
    acts = []
    for c in range(D_FF // FFN_COLS):
        lo, hi = c * FFN_COLS, (c + 1) * FFN_COLS
        between(2 * c)
        gate = _dot(xb, wgu_ref[0, :, lo:hi]) + bgu_ref[0][:, lo:hi]
        between(2 * c + 1)
        up = _dot(xb, wgu_ref[0, :, D_FF + lo:D_FF + hi]) + bgu_ref[0][:, D_FF + lo:D_FF + hi]
        gate = jnp.minimum(gate, SWIGLU_LIMIT)
        up = jnp.clip(up, -SWIGLU_LIMIT, SWIGLU_LIMIT)
        acts.append((gate * _sigmoid(SWIGLU_ALPHA * gate) * (up + 1.0)).astype(BF16))
    act = jnp.concatenate(acts, axis=1)
    for n in range(D_MODEL // FFN_COLS):
        lo, hi = n * FFN_COLS, (n + 1) * FFN_COLS
        between(2 * D_FF // FFN_COLS + n)
        y_ref[:, lo:hi] = _dot(act, wd_ref[0, :, lo:hi]) + bd_ref[0][:, lo:hi]


def _expert_kernel(nb_ref, bs_ref, loc_ref, gate_ref, hp_hbm, wgu_ref, bgu_ref, wd_ref, bd_ref, out_hbm,
                   hbuf, acc, xbuf, ybuf, sem):
    part = pl.program_id(0)
    e = pl.program_id(1)
    idx = part * N_EXPERTS + e
    nb = nb_ref[idx]
    b0 = bs_ref[idx]
    shares = np.array_split(np.arange(MOE_ROWS), FFN_PIECES)

    def fetch_rows(blk):
        base = blk * MOE_ROWS
        dst = xbuf.at[blk % 2]

        def emit(k):
            for r in shares[k]:
                r = int(r)
                dst[r:r + 1, :] = hbuf[pl.ds(loc_ref[base + r], 1), :]
        return emit

    def add_rows(blk):
        base = blk * MOE_ROWS
        src = ybuf.at[blk % 2]

        def emit(k):
            for batch in np.array_split(shares[k], max(1, len(shares[k]) // ADD_BATCH)):
                rows = [loc_ref[base + int(r)] for r in batch]
                new = [acc[pl.ds(row, 1), :] + gate_ref[base + int(r)] * src[int(r):int(r) + 1, :]
                       for row, r in zip(rows, batch)]
                for row, val in zip(rows, new):
                    acc[pl.ds(row, 1), :] = val
        return emit

    def emit_all(f):
        for k in range(FFN_PIECES):
            f(k)

    @pl.when(e == 0)
    def _():
        load = pltpu.make_async_copy(hp_hbm.at[pl.ds(part * MOE_PART, MOE_PART), :],
                                     hbuf.at[pl.ds(0, MOE_PART), :], sem.at[0])
        load.start()
        hbuf[MOE_PART:, :] = jnp.zeros((hbuf.shape[0] - MOE_PART, hbuf.shape[1]), hbuf.dtype)
        acc[...] = jnp.zeros_like(acc)
        ybuf[...] = jnp.zeros_like(ybuf)
        load.wait()
        emit_all(fetch_rows(b0))

    def block(i, carry):
        fetch, add = fetch_rows(i + 1), add_rows(i - 1)

        def between(k):
            fetch(k)
            add(k)
        _ffn_block(xbuf.at[i % 2], ybuf.at[i % 2], wgu_ref, bgu_ref, wd_ref, bd_ref, between)
        return carry

    lax.fori_loop(b0, b0 + nb, block, 0)

    @pl.when(e == N_EXPERTS - 1)
    def _():
        emit_all(add_rows(b0 + nb - 1))
        store = pltpu.make_async_copy(acc.at[pl.ds(0, MOE_PART), :],
                                      out_hbm.at[pl.ds(part * MOE_PART, MOE_PART), :], sem.at[0])
        store.start()
        store.wait()


def moe_experts(hp, n_blk, blk_start, row_loc, row_gate, w_gu, b_gu, w_down, b_down):
    t = hp.shape[0]
    d = D_MODEL
    per_expert = lambda p, e, nb, bs, sl: (e, 0, 0)
    grid_spec = pltpu.PrefetchScalarGridSpec(
        num_scalar_prefetch=3,
        grid=(t // MOE_PART, N_EXPERTS),
        in_specs=[pl.BlockSpec(memory_space=pltpu.SMEM),
                  pl.BlockSpec(memory_space=pl.ANY),
                  pl.BlockSpec((1, d, 2 * D_FF), per_expert),
                  pl.BlockSpec((1, 1, 2 * D_FF), per_expert),
                  pl.BlockSpec((1, D_FF, d), per_expert),
                  pl.BlockSpec((1, 1, d), per_expert)],
        out_specs=pl.BlockSpec(memory_space=pl.ANY),
        scratch_shapes=[pltpu.VMEM((MOE_PART + 8, d // 2), jnp.uint32),
                        pltpu.VMEM((MOE_PART + 8, d), F32),
                        pltpu.VMEM((2, MOE_ROWS, d // 2), jnp.uint32),
                        pltpu.VMEM((2, MOE_ROWS, d), F32),
                        pltpu.SemaphoreType.DMA((1,))],
    )
    return pl.pallas_call(
        _expert_kernel,
        grid_spec=grid_spec,
        out_shape=jax.ShapeDtypeStruct((t, d), F32),
        compiler_params=pltpu.CompilerParams(dimension_semantics=("arbitrary", "arbitrary"),
                                             vmem_limit_bytes=EXPERT_VMEM_LIMIT),
        name="moe_experts",
    )(n_blk, blk_start, row_loc, row_gate, hp, w_gu, b_gu.reshape(N_EXPERTS, 1, -1), w_down,
      b_down.reshape(N_EXPERTS, 1, -1))


def _residual_kernel(y_ref, x_ref, g_ref, o_ref):
    o_ref[...] = x_ref[...] + g_ref[0] * y_ref[...]


def gated_residual(y, x, gate_mod, tokens_per_group, tm=512):
    t, d = x.shape
    tiles_per_group = tokens_per_group // tm
    tile = pl.BlockSpec((tm, d), lambda i: (i, 0))
    return pl.pallas_call(
        _residual_kernel,
        grid=(t // tm,),
        in_specs=[tile, tile, pl.BlockSpec((1, 1, d), lambda i: (i // tiles_per_group, 0, 0))],
        out_specs=tile,
        out_shape=jax.ShapeDtypeStruct((t, d), F32),
        compiler_params=_cparams(("parallel",)),
        name="moe_residual",
    )(y, x, gate_mod)


def _assignment_tables(top_idx, rank, gates, counts):
    t = top_idx.shape[0]
    n_parts = t // MOE_PART
    n_assign = t * TOP_K
    part_blocks = MOE_PART * TOP_K // MOE_ROWS + N_EXPERTS + 1
    n_rows = (1 + n_parts * part_blocks) * MOE_ROWS
    padded = (counts + MOE_ROWS - 1) // MOE_ROWS * MOE_ROWS
    part_first = (1 + jnp.arange(n_parts, dtype=jnp.int32) * part_blocks) * MOE_ROWS
    row_start = part_first[:, None] + jnp.cumsum(padded, axis=1) - padded
    experts = jnp.arange(N_EXPERTS, dtype=jnp.int32)
    by_part = top_idx.reshape(n_parts, MOE_PART, TOP_K)
    start_of = jnp.sum(jnp.where(by_part[..., None] == experts, row_start[:, None, None, :], 0), axis=-1)
    dest = (start_of.reshape(t, TOP_K) + rank).reshape(-1).astype(jnp.int32)
    local = (jnp.arange(n_assign, dtype=jnp.int32) // TOP_K) % MOE_PART
    vals = jnp.stack([local, lax.bitcast_convert_type(gates[:, :TOP_K].reshape(-1), jnp.int32)], axis=1)
    init = jnp.broadcast_to(jnp.array([MOE_PART, 0], jnp.int32), (n_rows, 2))
    table = init.at[dest].set(vals)
    row_loc = table[:, 0]
    row_gate = lax.bitcast_convert_type(table[:, 1], F32)
    return row_loc, row_gate, (padded // MOE_ROWS).reshape(-1).astype(jnp.int32), \
        (row_start // MOE_ROWS).reshape(-1).astype(jnp.int32)


def moe_layer(x, p, shift, scale, gate_mod, tokens_per_group):
    hp, idx, gates, rank, cnt = moe_router(x, p['norm2'], shift, scale, p['rw_pad'], p['rb_pad'], tokens_per_group)
    counts = cnt[:, 0, :N_EXPERTS].astype(jnp.int32)
    row_loc, row_gate, n_blk, blk_start = _assignment_tables(idx[:, :TOP_K], rank[:, :TOP_K], gates, counts)
    y = moe_experts(hp, n_blk, blk_start, row_loc, row_gate, p['w_gu_bf'], p['b_gu'], p['w_down_bf'], p['b_down'])
    return gated_residual(y, x, gate_mod, tokens_per_group)


def _pad_lanes(a, width):
    return jnp.pad(a, [(0, 0)] * (a.ndim - 1) + [(0, width - a.shape[-1])])


def _prep_common(p):
    p['rw_pad'] = _pad_lanes(p['router_w'], LANES)
    p['rb_pad'] = jnp.concatenate(
        [p['router_b'].astype(F32), jnp.full((LANES - N_EXPERTS,), NEG_BIG, F32)]).reshape(1, LANES)
    p['w_out'] = p['w_out'].astype(BF16)
    p['w_gu_bf'] = p['w_gu'].astype(BF16)
    p['w_down_bf'] = p['w_down'].astype(BF16)
    return p


def _prep_layer0(p):
    p = _prep_common(dict(p))
    p['w_in'] = _pad_lanes(p['w_in'], AB_IN_PAD).astype(BF16)
    q_b = p['q_b'].reshape(MLA_Q_LORA, MLA_HEADS, MLA_QK)
    p['q_b_pad'] = _pad_lanes(q_b, LANES).reshape(MLA_Q_LORA, MLA_HEADS * LANES).astype(BF16)
    kv_b = p['kv_b'].reshape(MLA_KV_LORA, MLA_HEADS, MLA_NOPE + MLA_V)
    p['wk_pad'] = _pad_lanes(kv_b[:, :, :MLA_NOPE], LANES).reshape(MLA_KV_LORA, MLA_HEADS * LANES).astype(BF16)
    p['wv'] = kv_b[:, :, MLA_NOPE:].reshape(MLA_KV_LORA, MLA_HEADS * MLA_V).astype(BF16)
    p['q_norm_pad'] = _pad_lanes(p['q_norm'].reshape(1, -1), LANES)
    p['k_norm_pad'] = _pad_lanes(p['k_norm'].reshape(1, -1), LANES)
    return p


def _prep_layer1(p):
    p = _prep_common(dict(p))
    p['w_in'] = p['w_in'].astype(BF16)
    p['q_norm2'] = jnp.tile(p['q_norm'].reshape(1, -1), (1, LANES // SWA_HD))
    p['k_norm2'] = jnp.tile(p['k_norm'].reshape(1, -1), (1, LANES // SWA_HD))
    return p


def _group_forward(x3, mods, p0, p1, lb_logits, caches, latent):
    batch, seq, d = x3.shape
    t = batch * seq
    x = x3.reshape(t, d)
    tpg = seq if latent else t
    sh1, sc1, g1, sh2, sc2, g2 = mods[0]

    z = modnorm_matmul(x, p0['norm1'], sh1, sc1, p0['w_in'], tpg)
    tab_b = _rope_lane_tables(seq, MLA_ROPE, LANES, MLA_NOPE) if latent else None
    s0f, s0b = (caches['hg_f'], caches['hg_b']) if latent else (None, None)
    o_f, o_b, s_f, s_b = hgrn_bidir(z, lb_logits, s0f, s0b, batch, seq)
    q = mla_queries(z, p0['qa_norm'], p0['q_b_pad'], p0['q_norm_pad'], tab_b, seq)
    ckv, k, v = mla_keys_values(z, 2816 // LANES, z, 2944 // LANES, p0['kva_norm'], p0['wk_pad'], p0['wv'],
                                p0['k_norm_pad'], tab_b, seq, norm_input=True)
    n_k = seq
    if latent:
        n_ctx = caches['ckv'].shape[1]
        ckv_c = caches['ckv'].reshape(batch * n_ctx, MLA_KV_LORA)
        kpe_c = _pad_lanes(caches['kpe'].reshape(batch * n_ctx, MLA_ROPE), LANES)
        _, k_c, v_c = mla_keys_values(ckv_c, 0, kpe_c, 0, p0['kva_norm'], p0['wk_pad'], p0['wv'],
                                      p0['k_norm_pad'], None, n_ctx, norm_input=False)
        cat = lambda a, b: jnp.concatenate([a.reshape(batch, n_ctx, -1), b.reshape(batch, seq, -1)],
                                           axis=1).reshape(batch * (n_ctx + seq), -1)
        k, v = cat(k_c, k), cat(v_c, v)
        n_k = n_ctx + seq
    o_mla = mla_attention(q, k, v, batch, seq, n_k)
    x = out_proj_layer0(o_f, o_b, z, p0['hg_out_norm'], o_mla, p0['w_out'], x, g1, tpg)
    x = moe_layer(x, p0, sh2, sc2, g2, tpg)
    state0 = (s_f, s_b, ckv.reshape(batch, seq, MLA_KV_LORA), z[:, 2944:2944 + MLA_ROPE].reshape(batch, seq, MLA_ROPE))

    sh1, sc1, g1, sh2, sc2, g2 = mods[1]
    z = modnorm_matmul(x, p1['norm1'], sh1, sc1, p1['w_in'], tpg)
    tab_c = _rope_lane_tables(seq, SWA_HD, SWA_HD, 0) if latent else None
    q, k_cache, k, v = swa_prep(z, p1['q_norm2'], p1['k_norm2'], tab_c, seq)
    sink = p1['sink'].astype(F32)
    if latent:
        n_ctx = caches['k1'].shape[1]
        k_c = caches['k1'].reshape(batch, n_ctx, -1).astype(BF16)
        v_c = caches['v1'].reshape(batch, n_ctx, -1).astype(BF16)
        a = swa_window(q, k, v, k_c, v_c, sink, batch, seq, n_ctx)
    else:
        a = swa_dense(q, k, v, sink, batch, seq)
    x = out_proj_layer1(a, p1['w_out'], x, g1, tpg)
    x = moe_layer(x, p1, sh2, sc2, g2, tpg)
    nkv = SWA_KV_HEADS * SWA_HD
    state1 = (k_cache.reshape(batch, seq, SWA_KV_HEADS, SWA_HD),
              z[:, SWA_HEADS * SWA_HD + nkv:].reshape(batch, seq, SWA_KV_HEADS, SWA_HD))
    return x.reshape(batch, seq, d), state0, state1


def kernel(x_prompt, x_sample, state_l0_hgrn_fwd, state_l0_hgrn_bwd, cache_l0_mla_ckv, cache_l0_mla_kpe, cache_l1_k, cache_l1_v, c, c_ctx, hgrn_lb_logits, l0_ada_w, l0_ada_b, l0_norm1, l0_norm2, l0_w_in, l0_hg_out_norm, l0_qa_norm, l0_q_b, l0_kva_norm, l0_kv_b, l0_q_norm, l0_k_norm, l0_w_out, l0_router_w, l0_router_b, l0_w_gu, l0_b_gu, l0_w_down, l0_b_down, l1_ada_w, l1_ada_b, l1_norm1, l1_norm2, l1_w_in, l1_q_norm, l1_k_norm, l1_sink, l1_w_out, l1_router_w, l1_router_b, l1_w_gu, l1_b_gu, l1_w_down, l1_b_down):
    p0 = _prep_layer0(dict(norm1=l0_norm1, norm2=l0_norm2, w_in=l0_w_in, hg_out_norm=l0_hg_out_norm,
                           qa_norm=l0_qa_norm, q_b=l0_q_b, kva_norm=l0_kva_norm, kv_b=l0_kv_b,
                           q_norm=l0_q_norm, k_norm=l0_k_norm, w_out=l0_w_out, router_w=l0_router_w,
                           router_b=l0_router_b, w_gu=l0_w_gu, b_gu=l0_b_gu, w_down=l0_w_down,
                           b_down=l0_b_down))
    p1 = _prep_layer1(dict(norm1=l1_norm1, norm2=l1_norm2, w_in=l1_w_in, q_norm=l1_q_norm, k_norm=l1_k_norm,
                           sink=l1_sink, w_out=l1_w_out, router_w=l1_router_w, router_b=l1_router_b,
                           w_gu=l1_w_gu, b_gu=l1_b_gu, w_down=l1_w_down, b_down=l1_b_down))
    dec_batch = c.shape[0]
    d = c.shape[1]
    cond8 = jnp.concatenate([c_ctx[None, :], c, jnp.zeros((8 - 1 - dec_batch, d), F32)], axis=0)
    mods_ctx, mods_lat = [], []
    for w, b in ((l0_ada_w, l0_ada_b), (l1_ada_w, l1_ada_b)):
        mod = ada_params(cond8, w, b)
        mods_ctx.append([m.reshape(1, 1, d) for m in jnp.split(mod[0:1], 6, axis=-1)])
        mods_lat.append([m.reshape(dec_batch, 1, d) for m in jnp.split(mod[1:1 + dec_batch], 6, axis=-1)])

    y_prompt, st0, st1 = _group_forward(x_prompt, mods_ctx, p0, p1, hgrn_lb_logits, None, latent=False)
    caches = dict(hg_f=state_l0_hgrn_fwd, hg_b=state_l0_hgrn_bwd, ckv=cache_l0_mla_ckv, kpe=cache_l0_mla_kpe,
                  k1=cache_l1_k, v1=cache_l1_v)
    y_sample, _, _ = _group_forward(x_sample, mods_lat, p0, p1, hgrn_lb_logits, caches, latent=True)
    return (y_prompt, y_sample, st0[0], st0[1], st0[2], st0[3], st1[0], st1[1])
```

```python
import functools

import numpy as np
import jax
import jax.numpy as jnp
from jax import lax
from jax.experimental import pallas as pl
from jax.experimental.pallas import tpu as pltpu

F32 = jnp.float32
BF16 = jnp.bfloat16

D_MODEL = 1024
GRID_W = 64
ROPE_THETA = 10000.0
EPS = 1e-6
HG_HEADS = 4
HG_DK = 128
HG_DV = 128
MLA_HEADS = 8
MLA_NOPE = 64
MLA_ROPE = 32
MLA_V = 64
MLA_QK = MLA_NOPE + MLA_ROPE
MLA_Q_LORA = 256
MLA_KV_LORA = 128
MLA_SCALE = MLA_QK ** -0.5
SWA_HEADS = 16
SWA_KV_HEADS = 4
SWA_HD = 64
SWA_WINDOW = 128
SWA_SCALE = SWA_HD ** -0.5
SWA_GROUP = SWA_HEADS // SWA_KV_HEADS
N_EXPERTS = 32
TOP_K = 4
D_FF = 1024
SWIGLU_LIMIT = 7.0
SWIGLU_ALPHA = 1.702

LANES = 128
HG_CHUNK = 128
HG_LEVELS = 7
HG_MXU_LEVELS = 3
AB_IN_PAD = 3072
MOE_ROWS = 256
NEG_BIG = -1e30
VMEM_LIMIT = 48 * 1024 * 1024
EXPERT_VMEM_LIMIT = 56 * 1024 * 1024


def _cparams(sem):
    return pltpu.CompilerParams(dimension_semantics=sem, vmem_limit_bytes=VMEM_LIMIT)


def _dot(a, b):
    return jnp.dot(a, b, preferred_element_type=F32)


def _dot_nt(a, b):
    return lax.dot_general(a, b, (((1,), (1,)), ((), ())), preferred_element_type=F32)


def _dot_tn(a, b):
    return lax.dot_general(a, b, (((0,), (0,)), ((), ())), preferred_element_type=F32)


def _split2(x):
    hi = x.astype(BF16)
    lo = (x - hi.astype(F32)).astype(BF16)
    return hi, lo


def _dot_hp(a, b):
    ah, al = _split2(a)
    bh, bl = _split2(b)
    return _dot(ah, bh) + _dot(ah, bl) + _dot(al, bh)


def _sigmoid(x):
    return 1.0 / (1.0 + jnp.exp(-x))


def _modnorm(x, gain, shift, scale):
    y = x * lax.rsqrt(jnp.mean(x * x, axis=-1, keepdims=True) + EPS)
    return y * gain * (1.0 + scale) + shift


def _ada_kernel(c_ref, w_ref, b_ref, o_ref):
    c = c_ref[...]
    o_ref[...] = _dot_hp(c * _sigmoid(c), w_ref[...]) + b_ref[...]


def ada_params(cond8, w, b):
    n = w.shape[1]
    tn = 1024
    return pl.pallas_call(
        _ada_kernel,
        grid=(n // tn,),
        in_specs=[pl.BlockSpec((8, D_MODEL), lambda j: (0, 0)),
                  pl.BlockSpec((D_MODEL, tn), lambda j: (0, j)),
                  pl.BlockSpec((1, tn), lambda j: (0, j))],
        out_specs=pl.BlockSpec((8, tn), lambda j: (0, j)),
        out_shape=jax.ShapeDtypeStruct((8, n), F32),
        compiler_params=_cparams(("parallel",)),
        name="ada_params",
    )(cond8, w, b.reshape(1, n))


def _modnorm_matmul_kernel(x_ref, g_ref, sh_ref, sc_ref, w_ref, o_ref):
    h = _modnorm(x_ref[...], g_ref[...], sh_ref[0], sc_ref[0])
    o_ref[...] = _dot(h.astype(BF16), w_ref[...])


def modnorm_matmul(x, gain, shift, scale, w_bf16, tokens_per_group, tm=512):
    t, d = x.shape
    n = w_bf16.shape[1]
    tiles_per_group = tokens_per_group // tm
    grp = lambda i: (i // tiles_per_group, 0, 0)
    return pl.pallas_call(
        _modnorm_matmul_kernel,
        grid=(t // tm,),
        in_specs=[pl.BlockSpec((tm, d), lambda i: (i, 0)),
                  pl.BlockSpec((1, d), lambda i: (0, 0)),
                  pl.BlockSpec((1, 1, d), grp),
                  pl.BlockSpec((1, 1, d), grp),
                  pl.BlockSpec((d, n), lambda i: (0, 0))],
        out_specs=pl.BlockSpec((tm, n), lambda i: (i, 0)),
        out_shape=jax.ShapeDtypeStruct((t, n), F32),
        compiler_params=_cparams(("parallel",)),
        name="modnorm_matmul",
    )(x, gain.reshape(1, d), shift, scale, w_bf16)


def _hgrn_constants():
    c = HG_CHUNK
    t = np.arange(c)[:, None]
    u = np.arange(c)[None, :]
    mats = [(u <= t), (u > t)]
    for l in range(HG_MXU_LEVELS):
        m = 1 << l
        r = (t // (2 * m)) * (2 * m) + m - 1
        mats.append((u > np.minimum(t, r)) & (u <= np.maximum(t, r)))
    fwd = np.concatenate(mats, axis=0).astype(np.float32)
    bwd = np.concatenate([mm[::-1, ::-1] for mm in mats], axis=0).astype(np.float32)
    x = np.bitwise_xor(t, u)
    lvl = np.where(x > 0, np.floor(np.log2(np.maximum(x, 1))), HG_LEVELS).astype(np.int32)
    lv_f = np.where(t >= u, lvl, -1).astype(np.int32)
    return fwd, bwd, lv_f, lv_f.T.copy()


def _hgrn_direction(qs, fpres, vs, lbs, mcat, lv, sts, forward):
    c = HG_CHUNK
    n = len(qs)
    fs = [lb + (1.0 - lb) * _sigmoid(fp) for lb, fp in zip(lbs, fpres)]
    kks = [1.0 - f for f in fs]
    parts = []
    for f in fs:
        parts += list(_split2(jnp.log(f)))
    x_all = _dot(mcat, jnp.concatenate(parts, axis=1))
    xs = [x_all[:, 2 * i * c:(2 * i + 1) * c] + x_all[:, (2 * i + 1) * c:(2 * i + 2) * c] for i in range(n)]
    gs = [x[0:c] for x in xs]
    qbs = [q.astype(BF16) for q in qs]
    kbs = [kk.astype(BF16) for kk in kks]
    vbs = [v.astype(BF16) for v in vs]
    os_ = [_dot_nt((q * jnp.exp(g)).astype(BF16), st.astype(BF16)) for q, g, st in zip(qs, gs, sts)]
    accs = [jnp.where(lv == HG_LEVELS, _dot_nt(qb, kb), 0.0) for qb, kb in zip(qbs, kbs)]
    for l in range(HG_LEVELS):
        for i in range(n):
            if l < HG_MXU_LEVELS:
                x = xs[i][(2 + l) * c:(3 + l) * c]
            else:
                m = 1 << l
                ref_rows = [j * 2 * m + (m - 1 if forward else m) for j in range(c // (2 * m))]
                g_ref = jnp.concatenate(
                    [jnp.broadcast_to(gs[i][r:r + 1, :], (2 * m, c)) for r in ref_rows], axis=0)
                x = -jnp.abs(gs[i] - g_ref)
            e = jnp.exp(x)
            p = _dot_nt((qs[i] * e).astype(BF16), (kks[i] * e).astype(BF16))
            accs[i] = jnp.where(lv == l, p, accs[i])
    edge_row = c - 1 if forward else 0
    outs, new_sts = [], []
    for i in range(n):
        outs.append(os_[i] + _dot(accs[i].astype(BF16), vbs[i]))
        k_end = (kks[i] * jnp.exp(xs[i][c:2 * c])).astype(BF16)
        new_sts.append(sts[i] * jnp.exp(gs[i][edge_row:edge_row + 1, :]) + _dot_tn(vbs[i], k_end))
    return outs, new_sts


def _hgrn_kernel(*refs, has_init):
    if has_init:
        (qf_ref, qb_ref, ff_ref, fb_ref, vf_ref, vb_ref, lbl_ref, mf_ref, mb_ref, lvf_ref, lvb_ref,
         s0f_ref, s0b_ref, of_ref, ob_ref, sf_ref, sb_ref, stf, stb) = refs
    else:
        (qf_ref, qb_ref, ff_ref, fb_ref, vf_ref, vb_ref, lbl_ref, mf_ref, mb_ref, lvf_ref, lvb_ref,
         of_ref, ob_ref, sf_ref, sb_ref, stf, stb) = refs
    c = pl.program_id(1)
    nc = pl.num_programs(1)

    @pl.when(c == 0)
    def _():
        for h in range(HG_HEADS):
            if has_init:
                stf[h] = s0f_ref[0, h].T
                stb[h] = s0b_ref[0, h].T
            else:
                stf[h] = jnp.zeros((HG_DV, HG_DK), F32)
                stb[h] = jnp.zeros((HG_DV, HG_DK), F32)

    rows = [lbl_ref[:, j, :] for j in range(lbl_ref.shape[1])]
    mx = functools.reduce(jnp.maximum, rows)
    ex = [jnp.exp(r - mx) for r in rows]
    lb = ex[0] / functools.reduce(lambda a, b: a + b, ex)

    heads = [slice(h * LANES, (h + 1) * LANES) for h in range(HG_HEADS)]
    o_f, st_f = _hgrn_direction([qf_ref[0, :, hs] for hs in heads], [ff_ref[0, :, hs] for hs in heads],
                                [vf_ref[0, :, hs] for hs in heads], [lb[0:1, hs] for hs in heads],
                                mf_ref[...], lvf_ref[...], [stf[h] for h in range(HG_HEADS)], True)
    o_b, st_b = _hgrn_direction([qb_ref[0, :, hs] for hs in heads], [fb_ref[0, :, hs] for hs in heads],
                                [vb_ref[0, :, hs] for hs in heads], [lb[1:2, hs] for hs in heads],
                                mb_ref[...], lvb_ref[...], [stb[h] for h in range(HG_HEADS)], False)
    for h, hs in enumerate(heads):
        of_ref[0, :, hs] = o_f[h]
        ob_ref[0, :, hs] = o_b[h]
        stf[h] = st_f[h]
        stb[h] = st_b[h]

    @pl.when(c == nc - 1)
    def _():
        for h in range(HG_HEADS):
            sf_ref[0, h] = stf[h].T
            sb_ref[0, h] = stb[h].T


def hgrn_bidir(z, lb_logits, s0f, s0b, batch, seq):
    nc = seq // HG_CHUNK
    z3 = z.reshape(batch, seq, z.shape[1])
    mf, mb, lvf, lvb = _hgrn_constants()
    has_init = s0f is not None
    width = HG_HEADS * LANES
    blk = (1, HG_CHUNK, width)
    fwd = lambda off: pl.BlockSpec(blk, lambda b, c: (b, c, off))
    bwd = lambda off: pl.BlockSpec(blk, lambda b, c: (b, nc - 1 - c, off))
    full = lambda a: pl.BlockSpec(a.shape, lambda b, c: (0,) * a.ndim)
    st_spec = pl.BlockSpec((1, HG_HEADS, HG_DK, HG_DV), lambda b, c: (b, 0, 0, 0))
    consts = [jnp.asarray(mf, BF16), jnp.asarray(mb, BF16), jnp.asarray(lvf), jnp.asarray(lvb)]
    in_specs = [fwd(0), bwd(0), fwd(1), bwd(2), fwd(3), bwd(3),
                pl.BlockSpec(lb_logits.shape, lambda b, c: (0, 0, 0))]
    in_specs += [full(a) for a in consts]
    args = [z3] * 6 + [lb_logits] + consts
    if has_init:
        in_specs += [st_spec, st_spec]
        args += [s0f, s0b]
    o_shape = jax.ShapeDtypeStruct((batch, seq, width), F32)
    s_shape = jax.ShapeDtypeStruct((batch, HG_HEADS, HG_DK, HG_DV), F32)
    o_f, o_b, s_f, s_b = pl.pallas_call(
        functools.partial(_hgrn_kernel, has_init=has_init),
        grid=(batch, nc),
        in_specs=in_specs,
        out_specs=[pl.BlockSpec(blk, lambda b, c: (b, c, 0)),
                   pl.BlockSpec(blk, lambda b, c: (b, nc - 1 - c, 0)),
                   st_spec, st_spec],
        out_shape=[o_shape, o_shape, s_shape, s_shape],
        scratch_shapes=[pltpu.VMEM((HG_HEADS, HG_DV, HG_DK), F32), pltpu.VMEM((HG_HEADS, HG_DV, HG_DK), F32)],
        compiler_params=_cparams(("parallel", "arbitrary")),
        name="hgrn_bidir",
    )(*args)
    t = batch * seq
    return o_f.reshape(t, -1), o_b.reshape(t, -1), s_f, s_b


def _axial_tables(n_tokens, n_rot):
    t = jnp.arange(n_tokens)
    row = (t // GRID_W).astype(F32)
    col = (t % GRID_W).astype(F32)
    n_freq = n_rot // 4
    inv = jnp.power(ROPE_THETA, -jnp.arange(n_freq, dtype=F32) / n_freq)
    ang = jnp.concatenate([row[:, None] * inv, col[:, None] * inv], axis=-1)
    return jnp.cos(ang), jnp.sin(ang)


def _rope_lane_tables(n_tokens, n_rot, head_width, first_rot_lane):
    cos, sin = _axial_tables(n_tokens, n_rot)
    half = n_rot // 2
    c_head = jnp.ones((n_tokens, head_width), F32)
    sa_head = jnp.zeros((n_tokens, head_width), F32)
    sb_head = jnp.zeros((n_tokens, head_width), F32)
    a0, a1, a2 = first_rot_lane, first_rot_lane + half, first_rot_lane + n_rot
    c_head = c_head.at[:, a0:a1].set(cos).at[:, a1:a2].set(cos)
    sa_head = sa_head.at[:, a0:a1].set(-sin)
    sb_head = sb_head.at[:, a1:a2].set(sin)
    reps = LANES // head_width
    tile = lambda a: jnp.tile(a, (1, reps))
    return tile(c_head), tile(sa_head), tile(sb_head)


def _rope(x, c, sa, sb, half):
    return x * c + pltpu.roll(x, LANES - half, 1) * sa + pltpu.roll(x, half, 1) * sb


def _mla_q_kernel(*refs, rope):
    if rope:
        qa_ref, qan_ref, qb_ref, qn_ref, c_ref, sa_ref, sb_ref, o_ref = refs
    else:
        qa_ref, qan_ref, qb_ref, qn_ref, o_ref = refs
    qa = qa_ref[...]
    qn = qa * lax.rsqrt(jnp.mean(qa * qa, axis=-1, keepdims=True) + EPS) * qan_ref[...]
    qfull = _dot(qn.astype(BF16), qb_ref[...])
    outs = []
    for h in range(MLA_HEADS):
        qh = qfull[:, h * LANES:(h + 1) * LANES]
        ms = jnp.sum(qh * qh, axis=-1, keepdims=True) * (1.0 / MLA_QK)
        qh = qh * lax.rsqrt(ms + EPS) * qn_ref[...]
        if rope:
            qh = _rope(qh, c_ref[...], sa_ref[...], sb_ref[...], MLA_ROPE // 2)
        outs.append((qh * MLA_SCALE).astype(BF16))
    o_ref[...] = jnp.concatenate(outs, axis=1)


def mla_queries(z, qa_norm, q_b_pad, q_norm_pad, tables, tokens_per_batch, tm=256):
    t = z.shape[0]
    rope = tables is not None
    row = lambda i: (0, 0)
    in_specs = [pl.BlockSpec((tm, MLA_Q_LORA), lambda i: (i, 2560 // MLA_Q_LORA)),
                pl.BlockSpec((1, MLA_Q_LORA), row),
                pl.BlockSpec(q_b_pad.shape, row),
                pl.BlockSpec((1, LANES), row)]
    args = [z, qa_norm.reshape(1, -1), q_b_pad, q_norm_pad]
    if rope:
        tpb = tokens_per_batch // tm
        in_specs += [pl.BlockSpec((tm, LANES), lambda i: (i % tpb, 0))] * 3
        args += list(tables)
    return pl.pallas_call(
        functools.partial(_mla_q_kernel, rope=rope),
        grid=(t // tm,),
        in_specs=in_specs,
        out_specs=pl.BlockSpec((tm, MLA_HEADS * LANES), lambda i: (i, 0)),
        out_shape=jax.ShapeDtypeStruct((t, MLA_HEADS * LANES), BF16),
        compiler_params=_cparams(("parallel",)),
        name="mla_queries",
    )(*args)


def _mla_kv_kernel(*refs, norm_input, rope):
    if rope:
        kva_ref, kpe_ref, kvan_ref, wk_ref, wv_ref, kn_ref, c_ref, sa_ref, sb_ref, ckv_ref, k_ref, v_ref = refs
    else:
        kva_ref, kpe_ref, kvan_ref, wk_ref, wv_ref, kn_ref, ckv_ref, k_ref, v_ref = refs
    ckv = kva_ref[...]
    if norm_input:
        ckv = ckv * lax.rsqrt(jnp.mean(ckv * ckv, axis=-1, keepdims=True) + EPS) * kvan_ref[...]
    ckv_ref[...] = ckv
    cb = ckv.astype(BF16)
    knope = _dot(cb, wk_ref[...])
    v_ref[...] = _dot(cb, wv_ref[...]).astype(BF16)
    kpe = pltpu.roll(kpe_ref[...], MLA_NOPE, 1)
    outs = []
    for h in range(MLA_HEADS):
        kh = knope[:, h * LANES:(h + 1) * LANES] + kpe
        ms = jnp.sum(kh * kh, axis=-1, keepdims=True) * (1.0 / MLA_QK)
        kh = kh * lax.rsqrt(ms + EPS) * kn_ref[...]
        if rope:
            kh = _rope(kh, c_ref[...], sa_ref[...], sb_ref[...], MLA_ROPE // 2)
        outs.append(kh.astype(BF16))
    k_ref[...] = jnp.concatenate(outs, axis=1)


def mla_keys_values(kva_src, kva_col, kpe_src, kpe_col, kva_norm, wk_pad, wv, k_norm_pad, tables,
                    tokens_per_batch, norm_input, tm=256):
    t = kva_src.shape[0]
    rope = tables is not None
    row = lambda i: (0, 0)
    in_specs = [pl.BlockSpec((tm, LANES), lambda i: (i, kva_col)),
                pl.BlockSpec((tm, LANES), lambda i: (i, kpe_col)),
                pl.BlockSpec((1, LANES), row),
                pl.BlockSpec(wk_pad.shape, row),
                pl.BlockSpec(wv.shape, row),
                pl.BlockSpec((1, LANES), row)]
    args = [kva_src, kpe_src, kva_norm.reshape(1, -1), wk_pad, wv, k_norm_pad]
    if rope:
        tpb = tokens_per_batch // tm
        in_specs += [pl.BlockSpec((tm, LANES), lambda i: (i % tpb, 0))] * 3
        args += list(tables)
    return pl.pallas_call(
        functools.partial(_mla_kv_kernel, norm_input=norm_input, rope=rope),
        grid=(t // tm,),
        in_specs=in_specs,
        out_specs=[pl.BlockSpec((tm, LANES), lambda i: (i, 0)),
                   pl.BlockSpec((tm, MLA_HEADS * LANES), lambda i: (i, 0)),
                   pl.BlockSpec((tm, MLA_HEADS * MLA_V), lambda i: (i, 0))],
        out_shape=[jax.ShapeDtypeStruct((t, LANES), F32),
                   jax.ShapeDtypeStruct((t, MLA_HEADS * LANES), BF16),
                   jax.ShapeDtypeStruct((t, MLA_HEADS * MLA_V), BF16)],
        compiler_params=_cparams(("parallel",)),
        name="mla_keys_values",
    )(*args)


def _mla_attn_kernel(q_ref, k_ref, v_ref, o_ref):
    outs = []
    for j in range(2):
        q = q_ref[0][:, j * LANES:(j + 1) * LANES]
        k = k_ref[0][:, j * LANES:(j + 1) * LANES]
        v = v_ref[0][:, j * MLA_V:(j + 1) * MLA_V]
        s = _dot_nt(q, k)
        p = jnp.exp(s - jnp.max(s, axis=-1, keepdims=True))
        l = jnp.sum(p, axis=-1, keepdims=True)
        outs.append(_dot(p.astype(BF16), v) / l)
    o_ref[0] = jnp.concatenate(outs, axis=1)


def mla_attention(q, k, v, batch, n_q, n_k, tq=256):
    q3 = q.reshape(batch, n_q, -1)
    k3 = k.reshape(batch, n_k, -1)
    v3 = v.reshape(batch, n_k, -1)
    out = pl.pallas_call(
        _mla_attn_kernel,
        grid=(batch, MLA_HEADS // 2, n_q // tq),
        in_specs=[pl.BlockSpec((1, tq, 2 * LANES), lambda b, j, i: (b, i, j)),
                  pl.BlockSpec((1, n_k, 2 * LANES), lambda b, j, i: (b, 0, j)),
                  pl.BlockSpec((1, n_k, 2 * MLA_V), lambda b, j, i: (b, 0, j))],
        out_specs=pl.BlockSpec((1, tq, 2 * MLA_V), lambda b, j, i: (b, i, j)),
        out_shape=jax.ShapeDtypeStruct((batch, n_q, MLA_HEADS * MLA_V), F32),
        compiler_params=_cparams(("parallel", "parallel", "arbitrary")),
        name="mla_attention",
    )(q3, k3, v3)
    return out.reshape(batch * n_q, -1)


def _out0_kernel(of_ref, ob_ref, ag_ref, hgn_ref, om_ref, w_ref, x_ref, g_ref, o_ref):
    o = of_ref[...] + ob_ref[...]
    ag = ag_ref[...]
    parts = []
    for h in range(HG_HEADS):
        oh = o[:, h * HG_DV:(h + 1) * HG_DV]
        oh = oh * lax.rsqrt(jnp.mean(oh * oh, axis=-1, keepdims=True) + EPS) * hgn_ref[...]
        gh = ag[:, h * HG_DV:(h + 1) * HG_DV]
        parts.append((oh * (gh * _sigmoid(gh))).astype(BF16))
    oa = jnp.concatenate(parts, axis=1)
    n_a = HG_HEADS * HG_DV
    mix = _dot(oa, w_ref[0:n_a, :]) + _dot(om_ref[...].astype(BF16), w_ref[n_a:, :])
    o_ref[...] = x_ref[...] + g_ref[0] * mix


def out_proj_layer0(o_f, o_b, z, hg_norm, o_mla, w_out_bf16, x, gate, tokens_per_group, tm=512):
    t, d = x.shape
    n_a = HG_HEADS * HG_DV
    tiles_per_group = tokens_per_group // tm
    tile = lambda w: pl.BlockSpec((tm, w), lambda i: (i, 0))
    return pl.pallas_call(
        _out0_kernel,
        grid=(t // tm,),
        in_specs=[tile(n_a), tile(n_a),
                  pl.BlockSpec((tm, n_a), lambda i: (i, 2048 // n_a)),
                  pl.BlockSpec((1, HG_DV), lambda i: (0, 0)),
                  tile(o_mla.shape[1]),
                  pl.BlockSpec(w_out_bf16.shape, lambda i: (0, 0)),
                  tile(d),
                  pl.BlockSpec((1, 1, d), lambda i: (i // tiles_per_group, 0, 0))],
        out_specs=tile(d),
        out_shape=jax.ShapeDtypeStruct((t, d), F32),
        compiler_params=_cparams(("parallel",)),
        name="out_proj_layer0",
    )(o_f, o_b, z, hg_norm.reshape(1, -1), o_mla, w_out_bf16, x, gate)


def _out1_kernel(a_ref, w_ref, x_ref, g_ref, o_ref):
    o_ref[...] = x_ref[...] + g_ref[0] * _dot(a_ref[...].astype(BF16), w_ref[...])


def out_proj_layer1(a, w_out_bf16, x, gate, tokens_per_group, tm=512):
    t, d = x.shape
    tiles_per_group = tokens_per_group // tm
    tile = lambda w: pl.BlockSpec((tm, w), lambda i: (i, 0))
    return pl.pallas_call(
        _out1_kernel,
        grid=(t // tm,),
        in_specs=[tile(a.shape[1]),
                  pl.BlockSpec(w_out_bf16.shape, lambda i: (0, 0)),
                  tile(d),
                  pl.BlockSpec((1, 1, d), lambda i: (i // tiles_per_group, 0, 0))],
        out_specs=tile(d),
        out_shape=jax.ShapeDtypeStruct((t, d), F32),
        compiler_params=_cparams(("parallel",)),
        name="out_proj_layer1",
    )(a, w_out_bf16, x, gate)


def _head_rms(x, gain2):
    sq = x * x
    lane = lax.broadcasted_iota(jnp.int32, x.shape, 1)
    first = lane < SWA_HD
    lo = jnp.sum(jnp.where(first, sq, 0.0), axis=-1, keepdims=True)
    hi = jnp.sum(jnp.where(first, 0.0, sq), axis=-1, keepdims=True)
    ms = jnp.where(first, lo, hi) * (1.0 / SWA_HD)
    return x * lax.rsqrt(ms + EPS) * gain2


def _swa_prep_kernel(*refs, rope):
    if rope:
        zq_ref, zk_ref, zv_ref, qn_ref, kn_ref, c_ref, sa_ref, sb_ref, q_ref, kc_ref, k_ref, v_ref = refs
    else:
        zq_ref, zk_ref, zv_ref, qn_ref, kn_ref, q_ref, kc_ref, k_ref, v_ref = refs
    half = SWA_HD // 2

    def rot(x):
        return _rope(x, c_ref[...], sa_ref[...], sb_ref[...], half) if rope else x

    zq = zq_ref[...]
    qs = []
    for p in range(zq.shape[1] // LANES):
        x = _head_rms(zq[:, p * LANES:(p + 1) * LANES], qn_ref[...])
        qs.append((rot(x) * SWA_SCALE).astype(BF16))
    q_ref[...] = jnp.concatenate(qs, axis=1)
    zk = zk_ref[...]
    kn, kr = [], []
    for p in range(zk.shape[1] // LANES):
        x = _head_rms(zk[:, p * LANES:(p + 1) * LANES], kn_ref[...])
        kn.append(x)
        kr.append(rot(x).astype(BF16))
    kc_ref[...] = jnp.concatenate(kn, axis=1)
    k_ref[...] = jnp.concatenate(kr, axis=1)
    v_ref[...] = zv_ref[...].astype(BF16)


def swa_prep(z, q_norm2, k_norm2, tables, tokens_per_batch, tm=256):
    t = z.shape[0]
    nq = SWA_HEADS * SWA_HD
    nkv = SWA_KV_HEADS * SWA_HD
    rope = tables is not None
    row = lambda i: (0, 0)
    in_specs = [pl.BlockSpec((tm, nq), lambda i: (i, 0)),
                pl.BlockSpec((tm, nkv), lambda i: (i, nq // nkv)),
                pl.BlockSpec((tm, nkv), lambda i: (i, nq // nkv + 1)),
                pl.BlockSpec((1, LANES), row),
                pl.BlockSpec((1, LANES), row)]
    args = [z, z, z, q_norm2, k_norm2]
    if rope:
        tpb = tokens_per_batch // tm
        in_specs += [pl.BlockSpec((tm, LANES), lambda i: (i % tpb, 0))] * 3
        args += list(tables)
    tile = lambda w: pl.BlockSpec((tm, w), lambda i: (i, 0))
    return pl.pallas_call(
        functools.partial(_swa_prep_kernel, rope=rope),
        grid=(t // tm,),
        in_specs=in_specs,
        out_specs=[tile(nq), tile(nkv), tile(nkv), tile(nkv)],
        out_shape=[jax.ShapeDtypeStruct((t, nq), BF16),
                   jax.ShapeDtypeStruct((t, nkv), F32),
                   jax.ShapeDtypeStruct((t, nkv), BF16),
                   jax.ShapeDtypeStruct((t, nkv), BF16)],
        compiler_params=_cparams(("parallel",)),
        name="swa_prep",
    )(*args)


def _sink_softmax_pv(s, sk, vn):
    m = jnp.maximum(jnp.max(s, axis=-1, keepdims=True), sk)
    p = jnp.exp(s - m)
    l = jnp.sum(p, axis=-1, keepdims=True) + jnp.exp(sk - m)
    return _dot(p.astype(BF16), vn) / l


def _sink_attention(q8, k_all, v_all, bias, sink_ref, pair, stack):
    outs = []
    tq = q8.shape[0]
    for n in range(2):
        kn = k_all[:, n * SWA_HD:(n + 1) * SWA_HD]
        vn = v_all[:, n * SWA_HD:(n + 1) * SWA_HD]
        heads = [n * SWA_GROUP + g for g in range(SWA_GROUP)]
        qs = [q8[:, hq * SWA_HD:(hq + 1) * SWA_HD] for hq in heads]
        sinks = [sink_ref[pair * 2 * SWA_GROUP + hq] for hq in heads]
        if stack:
            sk = jnp.concatenate([jnp.full((tq, 1), s, F32) for s in sinks], axis=0)
            s = _dot_nt(jnp.concatenate(qs, axis=0), kn)
            if bias is not None:
                s = s + bias
            o4 = _sink_softmax_pv(s, sk, vn)
            outs += [o4[g * tq:(g + 1) * tq] for g in range(SWA_GROUP)]
        else:
            for q, sk in zip(qs, sinks):
                s = _dot_nt(q, kn)
                if bias is not None:
                    s = s + bias
                outs.append(_sink_softmax_pv(s, sk, vn))
    return jnp.concatenate(outs, axis=1)


def _swa_dense_kernel(sink_ref, q_ref, k_ref, v_ref, o_ref):
    o_ref[0] = _sink_attention(q_ref[0], k_ref[0], v_ref[0], None, sink_ref, pl.program_id(1), stack=False)


def swa_dense(q, k, v, sink, batch, seq):
    nq = SWA_HEADS * SWA_HD
    q3 = q.reshape(batch, seq, nq)
    k3 = k.reshape(batch, seq, -1)
    v3 = v.reshape(batch, seq, -1)
    out = pl.pallas_call(
        _swa_dense_kernel,
        grid=(batch, 2),
        in_specs=[pl.BlockSpec(memory_space=pltpu.SMEM),
                  pl.BlockSpec((1, seq, nq // 2), lambda b, j: (b, 0, j)),
                  pl.BlockSpec((1, seq, LANES), lambda b, j: (b, 0, j)),
                  pl.BlockSpec((1, seq, LANES), lambda b, j: (b, 0, j))],
        out_specs=pl.BlockSpec((1, seq, nq // 2), lambda b, j: (b, 0, j)),
        out_shape=jax.ShapeDtypeStruct((batch, seq, nq), F32),
        compiler_params=_cparams(("parallel", "parallel")),
        name="swa_dense",
    )(sink, q3, k3, v3)
    return out.reshape(batch * seq, nq)


def _swa_window_kernel(sink_ref, q_ref, kc_ref, vc_ref, kp_ref, k0_ref, kn_ref, vp_ref, v0_ref, vn_ref, o_ref):
    i = pl.program_id(2)
    nb = pl.num_programs(2)
    w = SWA_WINDOW
    n_ctx = kc_ref.shape[1]
    k_all = jnp.concatenate([kc_ref[0], kp_ref[0], k0_ref[0], kn_ref[0]], axis=0)
    v_all = jnp.concatenate([vc_ref[0], vp_ref[0], v0_ref[0], vn_ref[0]], axis=0)
    shape = (SWA_GROUP * w, n_ctx + 3 * w)
    r = lax.broadcasted_iota(jnp.int32, shape, 0) % w
    col = lax.broadcasted_iota(jnp.int32, shape, 1) - n_ctx
    prev_bias = jnp.where(i > 0, 0.0, NEG_BIG)
    next_bias = jnp.where(i < nb - 1, 0.0, NEG_BIG)
    bias = jnp.where(
        col < w,
        jnp.where(col < 0, 0.0, jnp.where(col >= r, prev_bias, NEG_BIG)),
        jnp.where(col < 2 * w, 0.0, jnp.where(col - 2 * w <= r, next_bias, NEG_BIG)))
    o_ref[0] = _sink_attention(q_ref[0], k_all, v_all, bias, sink_ref, pl.program_id(1), stack=True)


def swa_window(q, k, v, k_ctx, v_ctx, sink, batch, seq, n_ctx):
    nq = SWA_HEADS * SWA_HD
    w = SWA_WINDOW
    nb = seq // w
    q3 = q.reshape(batch, seq, nq)
    k3 = k.reshape(batch, seq, -1)
    v3 = v.reshape(batch, seq, -1)
    ctx = pl.BlockSpec((1, n_ctx, LANES), lambda b, j, i: (b, 0, j))
    prv = pl.BlockSpec((1, w, LANES), lambda b, j, i: (b, jnp.maximum(i - 1, 0), j))
    cur = pl.BlockSpec((1, w, LANES), lambda b, j, i: (b, i, j))
    nxt = pl.BlockSpec((1, w, LANES), lambda b, j, i: (b, jnp.minimum(i + 1, nb - 1), j))
    out = pl.pallas_call(
        _swa_window_kernel,
        grid=(batch, 2, nb),
        in_specs=[pl.BlockSpec(memory_space=pltpu.SMEM),
                  pl.BlockSpec((1, w, nq // 2), lambda b, j, i: (b, i, j)),
                  ctx, ctx, prv, cur, nxt, prv, cur, nxt],
        out_specs=pl.BlockSpec((1, w, nq // 2), lambda b, j, i: (b, i, j)),
        out_shape=jax.ShapeDtypeStruct((batch, seq, nq), F32),
        compiler_params=_cparams(("parallel", "parallel", "arbitrary")),
        name="swa_window",
    )(sink, q3, k_ctx, v_ctx, k3, k3, k3, v3, v3, v3)
    return out.reshape(batch * seq, nq)


def _router_kernel(x_ref, g_ref, sh_ref, sc_ref, rw_ref, rb_ref, tri_ref, h_ref, idx_ref, gate_ref, rank_ref,
                   cnt_ref, cnt, *, tiles_per_part):
    @pl.when(pl.program_id(0) % tiles_per_part == 0)
    def _():
        cnt[...] = jnp.zeros_like(cnt)

    h = _modnorm(x_ref[...], g_ref[...], sh_ref[0], sc_ref[0])
    bits = lax.bitcast_convert_type(h.astype(BF16).astype(F32), jnp.uint32)
    half = h.shape[1] // 2
    h_ref[...] = (bits[:, :half] >> 16) | (bits[:, half:] & jnp.uint32(0xFFFF0000))
    wh, wl = _split2(rw_ref[...])
    hh, hl = _split2(h)
    logits = _dot_nt(wh, hh) + _dot_nt(wh, hl) + _dot_nt(wl, hh) + rb_ref[...]
    row = lax.broadcasted_iota(jnp.int32, logits.shape, 0)
    work = logits
    vals, idxs = [], []
    for _ in range(TOP_K):
        m = jnp.max(work, axis=0, keepdims=True)
        ik = jnp.min(jnp.where(work == m, row, LANES), axis=0, keepdims=True)
        vals.append(m)
        idxs.append(ik)
        work = jnp.where(row == ik, 2.0 * NEG_BIG, work)
    es = [jnp.exp(v - vals[0]) for v in vals]
    denom = es[0] + es[1] + es[2] + es[3]
    chosen = jnp.zeros(logits.shape, F32)
    for k in range(TOP_K):
        chosen = jnp.where(row == idxs[k], 1.0, chosen)
    chosen_b = chosen.astype(BF16)
    seen = cnt[...]
    before = _dot(chosen_b, tri_ref[...]) + jnp.concatenate([seen] * (logits.shape[1] // LANES), axis=1)
    out_row = lax.broadcasted_iota(jnp.int32, idx_ref.shape, 0)
    idx_out = jnp.zeros(idx_ref.shape, jnp.int32)
    gate_out = jnp.zeros(idx_ref.shape, F32)
    rank_out = jnp.zeros(idx_ref.shape, jnp.int32)
    for k in range(TOP_K):
        rk = jnp.sum(jnp.where(row == idxs[k], before, 0.0), axis=0, keepdims=True)
        idx_out = jnp.where(out_row == k, idxs[k], idx_out)
        gate_out = jnp.where(out_row == k, es[k] / denom, gate_out)
        rank_out = jnp.where(out_row == k, rk.astype(jnp.int32), rank_out)
    idx_ref[...] = idx_out
    gate_ref[...] = gate_out
    rank_ref[...] = rank_out
    cnt[...] = seen + _dot(chosen_b, jnp.ones((logits.shape[1], LANES), BF16))
    cnt_ref[0] = cnt[...]


def moe_router(x, gain, shift, scale, rw_t, rb_col, tokens_per_group, tm=256):
    t, d = x.shape
    tiles_per_group = tokens_per_group // tm
    tiles_per_part = MOE_PART // tm
    grp = lambda i: (i // tiles_per_group, 0, 0)
    tile = lambda w: pl.BlockSpec((tm, w), lambda i: (i, 0))
    slab = pl.BlockSpec((8, tm), lambda i: (0, i))
    tri = jnp.asarray(np.triu(np.ones((tm, tm), np.float32), 1), BF16)
    hp, idx, gates, rank, cnt = pl.pallas_call(
        functools.partial(_router_kernel, tiles_per_part=tiles_per_part),
        grid=(t // tm,),
        in_specs=[tile(d),
                  pl.BlockSpec((1, d), lambda i: (0, 0)),
                  pl.BlockSpec((1, 1, d), grp),
                  pl.BlockSpec((1, 1, d), grp),
                  pl.BlockSpec((LANES, d), lambda i: (0, 0)),
                  pl.BlockSpec((LANES, 1), lambda i: (0, 0)),
                  pl.BlockSpec((tm, tm), lambda i: (0, 0))],
        out_specs=[tile(d // 2), slab, slab, slab,
                   pl.BlockSpec((1, LANES, LANES), lambda i: (i // tiles_per_part, 0, 0))],
        out_shape=[jax.ShapeDtypeStruct((t, d // 2), jnp.uint32),
                   jax.ShapeDtypeStruct((8, t), jnp.int32),
                   jax.ShapeDtypeStruct((8, t), F32),
                   jax.ShapeDtypeStruct((8, t), jnp.int32),
                   jax.ShapeDtypeStruct((t // MOE_PART, LANES, LANES), F32)],
        scratch_shapes=[pltpu.VMEM((LANES, LANES), F32)],
        compiler_params=_cparams(("arbitrary",)),
        name="moe_router",
    )(x, gain.reshape(1, d), shift, scale, rw_t, rb_col, tri)
    per_token = lambda a: a[:TOP_K].T
    return hp, per_token(idx), per_token(gates), per_token(rank), cnt[:, :N_EXPERTS, 0].astype(jnp.int32)


FFN_COLS = 512
FFN_PIECES = 2 * D_FF // FFN_COLS + D_MODEL // FFN_COLS
MOE_PART = 4096
ADD_BATCH = 8


def _unpack_rows(words):
    lo = lax.bitcast_convert_type(words << 16, F32)
    hi = lax.bitcast_convert_type(words & jnp.uint32(0xFFFF0000), F32)
    return jnp.concatenate([lo, hi], axis=1).astype(BF16)


def _ffn_block(x_ref, y_ref, wgu_ref, bgu_ref, wd_ref, bd_ref, between):
    xb = _unpack_rows(x_ref[...])
    acts = []
    for c in range(D_FF // FFN_COLS):
        lo, hi = c * FFN_COLS, (c + 1) * FFN_COLS
        between(2 * c)
        gate = _dot(xb, wgu_ref[0, :, lo:hi]) + bgu_ref[0][:, lo:hi]
        between(2 * c + 1)
        up = _dot(xb, wgu_ref[0, :, D_FF + lo:D_FF + hi]) + bgu_ref[0][:, D_FF + lo:D_FF + hi]
        gate = jnp.minimum(gate, SWIGLU_LIMIT)
        up = jnp.clip(up, -SWIGLU_LIMIT, SWIGLU_LIMIT)
        acts.append((gate * _sigmoid(SWIGLU_ALPHA * gate) * (up + 1.0)).astype(BF16))
    act = jnp.concatenate(acts, axis=1)
    for n in range(D_MODEL // FFN_COLS):
        lo, hi = n * FFN_COLS, (n + 1) * FFN_COLS
        between(2 * D_FF // FFN_COLS + n)
        y_ref[:, lo:hi] = _dot(act, wd_ref[0, :, lo:hi]) + bd_ref[0][:, lo:hi]


def _expert_kernel(nb_ref, bs_ref, loc_ref, gate_ref, hp_hbm, wgu_ref, bgu_ref, wd_ref, bd_ref, out_hbm,
                   hbuf, acc, xbuf, ybuf, sem):
    part = pl.program_id(0)
    e = pl.program_id(1)
    idx = part * N_EXPERTS + e
    nb = nb_ref[idx]
    b0 = bs_ref[idx]
    shares = np.array_split(np.arange(MOE_ROWS), FFN_PIECES)

    def fetch_rows(blk):
        base = blk * MOE_ROWS
        dst = xbuf.at[blk % 2]

        def emit(k):
            for r in shares[k]:
                r = int(r)
                dst[r:r + 1, :] = hbuf[pl.ds(loc_ref[base + r], 1), :]
        return emit

    def add_rows(blk):
        base = blk * MOE_ROWS
        src = ybuf.at[blk % 2]

        def emit(k):
            for batch in np.array_split(shares[k], max(1, len(shares[k]) // ADD_BATCH)):
                rows = [loc_ref[base + int(r)] for r in batch]
                new = [acc[pl.ds(row, 1), :] + gate_ref[base + int(r)] * src[int(r):int(r) + 1, :]
                       for row, r in zip(rows, batch)]
                for row, val in zip(rows, new):
                    acc[pl.ds(row, 1), :] = val
        return emit

    def emit_all(f):
        for k in range(FFN_PIECES):
            f(k)

    @pl.when(e == 0)
    def _():
        load = pltpu.make_async_copy(hp_hbm.at[pl.ds(part * MOE_PART, MOE_PART), :],
                                     hbuf.at[pl.ds(0, MOE_PART), :], sem.at[0])
        load.start()
        hbuf[MOE_PART:, :] = jnp.zeros((hbuf.shape[0] - MOE_PART, hbuf.shape[1]), hbuf.dtype)
        acc[...] = jnp.zeros_like(acc)
        ybuf[...] = jnp.zeros_like(ybuf)
        load.wait()
        emit_all(fetch_rows(b0))

    def block(i, carry):
        fetch, add = fetch_rows(i + 1), add_rows(i - 1)

        def between(k):
            fetch(k)
            add(k)
        _ffn_block(xbuf.at[i % 2], ybuf.at[i % 2], wgu_ref, bgu_ref, wd_ref, bd_ref, between)
        return carry

    lax.fori_loop(b0, b0 + nb, block, 0)

    @pl.when(e == N_EXPERTS - 1)
    def _():
        emit_all(add_rows(b0 + nb - 1))
        store = pltpu.make_async_copy(acc.at[pl.ds(0, MOE_PART), :],
                                      out_hbm.at[pl.ds(part * MOE_PART, MOE_PART), :], sem.at[0])
        store.start()
        store.wait()


def moe_experts(hp, n_blk, blk_start, row_loc, row_gate, w_gu, b_gu, w_down, b_down):
    t = hp.shape[0]
    d = D_MODEL
    per_expert = lambda p, e, nb, bs, sl: (e, 0, 0)
    grid_spec = pltpu.PrefetchScalarGridSpec(
        num_scalar_prefetch=3,
        grid=(t // MOE_PART, N_EXPERTS),
        in_specs=[pl.BlockSpec(memory_space=pltpu.SMEM),
                  pl.BlockSpec(memory_space=pl.ANY),
                  pl.BlockSpec((1, d, 2 * D_FF), per_expert),
                  pl.BlockSpec((1, 1, 2 * D_FF), per_expert),
                  pl.BlockSpec((1, D_FF, d), per_expert),
                  pl.BlockSpec((1, 1, d), per_expert)],
        out_specs=pl.BlockSpec(memory_space=pl.ANY),
        scratch_shapes=[pltpu.VMEM((MOE_PART + 8, d // 2), jnp.uint32),
                        pltpu.VMEM((MOE_PART + 8, d), F32),
                        pltpu.VMEM((2, MOE_ROWS, d // 2), jnp.uint32),
                        pltpu.VMEM((2, MOE_ROWS, d), F32),
                        pltpu.SemaphoreType.DMA((1,))],
    )
    return pl.pallas_call(
        _expert_kernel,
        grid_spec=grid_spec,
        out_shape=jax.ShapeDtypeStruct((t, d), F32),
        compiler_params=pltpu.CompilerParams(dimension_semantics=("arbitrary", "arbitrary"),
                                             vmem_limit_bytes=EXPERT_VMEM_LIMIT),
        name="moe_experts",
    )(n_blk, blk_start, row_loc, row_gate, hp, w_gu, b_gu.reshape(N_EXPERTS, 1, -1), w_down,
      b_down.reshape(N_EXPERTS, 1, -1))


def _residual_kernel(y_ref, x_ref, g_ref, o_ref):
    o_ref[...] = x_ref[...] + g_ref[0] * y_ref[...]


def gated_residual(y, x, gate_mod, tokens_per_group, tm=512):
    t, d = x.shape
    tiles_per_group = tokens_per_group // tm
    tile = pl.BlockSpec((tm, d), lambda i: (i, 0))
    return pl.pallas_call(
        _residual_kernel,
        grid=(t // tm,),
        in_specs=[tile, tile, pl.BlockSpec((1, 1, d), lambda i: (i // tiles_per_group, 0, 0))],
        out_specs=tile,
        out_shape=jax.ShapeDtypeStruct((t, d), F32),
        compiler_params=_cparams(("parallel",)),
        name="moe_residual",
    )(y, x, gate_mod)


def _assignment_tables(top_idx, rank, gates, counts):
    t = top_idx.shape[0]
    n_parts = t // MOE_PART
    n_assign = t * TOP_K
    part_blocks = MOE_PART * TOP_K // MOE_ROWS + N_EXPERTS + 1
    n_rows = (1 + n_parts * part_blocks) * MOE_ROWS
    padded = (counts + MOE_ROWS - 1) // MOE_ROWS * MOE_ROWS
    part_first = (1 + jnp.arange(n_parts, dtype=jnp.int32) * part_blocks) * MOE_ROWS
    row_start = part_first[:, None] + jnp.cumsum(padded, axis=1) - padded
    experts = jnp.arange(N_EXPERTS, dtype=jnp.int32)
    by_part = top_idx.reshape(n_parts, MOE_PART, TOP_K)
    start_of = jnp.sum(jnp.where(by_part[..., None] == experts, row_start[:, None, None, :], 0), axis=-1)
    dest = (start_of.reshape(t, TOP_K) + rank).reshape(-1).astype(jnp.int32)
    local = (jnp.arange(n_assign, dtype=jnp.int32) // TOP_K) % MOE_PART
    vals = jnp.stack([local, lax.bitcast_convert_type(gates.reshape(-1), jnp.int32)], axis=1)
    init = jnp.broadcast_to(jnp.array([MOE_PART, 0], jnp.int32), (n_rows, 2))
    table = init.at[dest].set(vals)
    row_loc = table[:, 0]
    row_gate = lax.bitcast_convert_type(table[:, 1], F32)
    return row_loc, row_gate, (padded // MOE_ROWS).reshape(-1).astype(jnp.int32), \
        (row_start // MOE_ROWS).reshape(-1).astype(jnp.int32)


def moe_layer(x, p, shift, scale, gate_mod, tokens_per_group):
    hp, idx, gates, rank, counts = moe_router(x, p['norm2'], shift, scale, p['rw_t'], p['rb_col'], tokens_per_group)
    row_loc, row_gate, n_blk, blk_start = _assignment_tables(idx, rank, gates, counts)
    y = moe_experts(hp, n_blk, blk_start, row_loc, row_gate, p['w_gu_bf'], p['b_gu'], p['w_down_bf'], p['b_down'])
    return gated_residual(y, x, gate_mod, tokens_per_group)


def _pad_lanes(a, width):
    return jnp.pad(a, [(0, 0)] * (a.ndim - 1) + [(0, width - a.shape[-1])])


def _prep_common(p):
    p['rw_t'] = _pad_lanes(p['router_w'], LANES).T
    p['rb_col'] = jnp.concatenate(
        [p['router_b'].astype(F32), jnp.full((LANES - N_EXPERTS,), NEG_BIG, F32)]).reshape(LANES, 1)
    p['w_out'] = p['w_out'].astype(BF16)
    p['w_gu_bf'] = p['w_gu'].astype(BF16)
    p['w_down_bf'] = p['w_down'].astype(BF16)
    return p


def _prep_layer0(p):
    p = _prep_common(dict(p))
    p['w_in'] = _pad_lanes(p['w_in'], AB_IN_PAD).astype(BF16)
    q_b = p['q_b'].reshape(MLA_Q_LORA, MLA_HEADS, MLA_QK)
    p['q_b_pad'] = _pad_lanes(q_b, LANES).reshape(MLA_Q_LORA, MLA_HEADS * LANES).astype(BF16)
    kv_b = p['kv_b'].reshape(MLA_KV_LORA, MLA_HEADS, MLA_NOPE + MLA_V)
    p['wk_pad'] = _pad_lanes(kv_b[:, :, :MLA_NOPE], LANES).reshape(MLA_KV_LORA, MLA_HEADS * LANES).astype(BF16)
    p['wv'] = kv_b[:, :, MLA_NOPE:].reshape(MLA_KV_LORA, MLA_HEADS * MLA_V).astype(BF16)
    p['q_norm_pad'] = _pad_lanes(p['q_norm'].reshape(1, -1), LANES)
    p['k_norm_pad'] = _pad_lanes(p['k_norm'].reshape(1, -1), LANES)
    return p


def _prep_layer1(p):
    p = _prep_common(dict(p))
    p['w_in'] = p['w_in'].astype(BF16)
    p['q_norm2'] = jnp.tile(p['q_norm'].reshape(1, -1), (1, LANES // SWA_HD))
    p['k_norm2'] = jnp.tile(p['k_norm'].reshape(1, -1), (1, LANES // SWA_HD))
    return p


def _group_forward(x3, mods, p0, p1, lb_logits, caches, latent):
    batch, seq, d = x3.shape
    t = batch * seq
    x = x3.reshape(t, d)
    tpg = seq if latent else t
    sh1, sc1, g1, sh2, sc2, g2 = mods[0]

    z = modnorm_matmul(x, p0['norm1'], sh1, sc1, p0['w_in'], tpg)
    tab_b = _rope_lane_tables(seq, MLA_ROPE, LANES, MLA_NOPE) if latent else None
    s0f, s0b = (caches['hg_f'], caches['hg_b']) if latent else (None, None)
    o_f, o_b, s_f, s_b = hgrn_bidir(z, lb_logits, s0f, s0b, batch, seq)
    q = mla_queries(z, p0['qa_norm'], p0['q_b_pad'], p0['q_norm_pad'], tab_b, seq)
    ckv, k, v = mla_keys_values(z, 2816 // LANES, z, 2944 // LANES, p0['kva_norm'], p0['wk_pad'], p0['wv'],
                                p0['k_norm_pad'], tab_b, seq, norm_input=True)
    n_k = seq
    if latent:
        n_ctx = caches['ckv'].shape[1]
        ckv_c = caches['ckv'].reshape(batch * n_ctx, MLA_KV_LORA)
        kpe_c = _pad_lanes(caches['kpe'].reshape(batch * n_ctx, MLA_ROPE), LANES)
        _, k_c, v_c = mla_keys_values(ckv_c, 0, kpe_c, 0, p0['kva_norm'], p0['wk_pad'], p0['wv'],
                                      p0['k_norm_pad'], None, n_ctx, norm_input=False)
        cat = lambda a, b: jnp.concatenate([a.reshape(batch, n_ctx, -1), b.reshape(batch, seq, -1)],
                                           axis=1).reshape(batch * (n_ctx + seq), -1)
        k, v = cat(k_c, k), cat(v_c, v)
        n_k = n_ctx + seq
    o_mla = mla_attention(q, k, v, batch, seq, n_k)
    x = out_proj_layer0(o_f, o_b, z, p0['hg_out_norm'], o_mla, p0['w_out'], x, g1, tpg)
    x = moe_layer(x, p0, sh2, sc2, g2, tpg)
    state0 = (s_f, s_b, ckv.reshape(batch, seq, MLA_KV_LORA), z[:, 2944:2944 + MLA_ROPE].reshape(batch, seq, MLA_ROPE))

    sh1, sc1, g1, sh2, sc2, g2 = mods[1]
    z = modnorm_matmul(x, p1['norm1'], sh1, sc1, p1['w_in'], tpg)
    tab_c = _rope_lane_tables(seq, SWA_HD, SWA_HD, 0) if latent else None
    q, k_cache, k, v = swa_prep(z, p1['q_norm2'], p1['k_norm2'], tab_c, seq)
    sink = p1['sink'].astype(F32)
    if latent:
        n_ctx = caches['k1'].shape[1]
        k_c = caches['k1'].reshape(batch, n_ctx, -1).astype(BF16)
        v_c = caches['v1'].reshape(batch, n_ctx, -1).astype(BF16)
        a = swa_window(q, k, v, k_c, v_c, sink, batch, seq, n_ctx)
    else:
        a = swa_dense(q, k, v, sink, batch, seq)
    x = out_proj_layer1(a, p1['w_out'], x, g1, tpg)
    x = moe_layer(x, p1, sh2, sc2, g2, tpg)
    nkv = SWA_KV_HEADS * SWA_HD
    state1 = (k_cache.reshape(batch, seq, SWA_KV_HEADS, SWA_HD),
              z[:, SWA_HEADS * SWA_HD + nkv:].reshape(batch, seq, SWA_KV_HEADS, SWA_HD))
    return x.reshape(batch, seq, d), state0, state1


def kernel(x_prompt, x_sample, state_l0_hgrn_fwd, state_l0_hgrn_bwd, cache_l0_mla_ckv, cache_l0_mla_kpe, cache_l1_k, cache_l1_v, c, c_ctx, hgrn_lb_logits, l0_ada_w, l0_ada_b, l0_norm1, l0_norm2, l0_w_in, l0_hg_out_norm, l0_qa_norm, l0_q_b, l0_kva_norm, l0_kv_b, l0_q_norm, l0_k_norm, l0_w_out, l0_router_w, l0_router_b, l0_w_gu, l0_b_gu, l0_w_down, l0_b_down, l1_ada_w, l1_ada_b, l1_norm1, l1_norm2, l1_w_in, l1_q_norm, l1_k_norm, l1_sink, l1_w_out, l1_router_w, l1_router_b, l1_w_gu, l1_b_gu, l1_w_down, l1_b_down):
    p0 = _prep_layer0(dict(norm1=l0_norm1, norm2=l0_norm2, w_in=l0_w_in, hg_out_norm=l0_hg_out_norm,
                           qa_norm=l0_qa_norm, q_b=l0_q_b, kva_norm=l0_kva_norm, kv_b=l0_kv_b,
                           q_norm=l0_q_norm, k_norm=l0_k_norm, w_out=l0_w_out, router_w=l0_router_w,
                           router_b=l0_router_b, w_gu=l0_w_gu, b_gu=l0_b_gu, w_down=l0_w_down,
                           b_down=l0_b_down))
    p1 = _prep_layer1(dict(norm1=l1_norm1, norm2=l1_norm2, w_in=l1_w_in, q_norm=l1_q_norm, k_norm=l1_k_norm,
                           sink=l1_sink, w_out=l1_w_out, router_w=l1_router_w, router_b=l1_router_b,
                           w_gu=l1_w_gu, b_gu=l1_b_gu, w_down=l1_w_down, b_down=l1_b_down))
    dec_batch = c.shape[0]
    d = c.shape[1]
    cond8 = jnp.concatenate([c_ctx[None, :], c, jnp.zeros((8 - 1 - dec_batch, d), F32)], axis=0)
    mods_ctx, mods_lat = [], []
    for w, b in ((l0_ada_w, l0_ada_b), (l1_ada_w, l1_ada_b)):
        mod = ada_params(cond8, w, b)
        mods_ctx.append([m.reshape(1, 1, d) for m in jnp.split(mod[0:1], 6, axis=-1)])
        mods_lat.append([m.reshape(dec_batch, 1, d) for m in jnp.split(mod[1:1 + dec_batch], 6, axis=-1)])

    y_prompt, st0, st1 = _group_forward(x_prompt, mods_ctx, p0, p1, hgrn_lb_logits, None, latent=False)
    caches = dict(hg_f=state_l0_hgrn_fwd, hg_b=state_l0_hgrn_bwd, ckv=cache_l0_mla_ckv, kpe=cache_l0_mla_kpe,
                  k1=cache_l1_k, v1=cache_l1_v)
    y_sample, _, _ = _group_forward(x_sample, mods_lat, p0, p1, hgrn_lb_logits, caches, latent=True)
    return (y_prompt, y_sample, st0[0], st0[1], st0[2], st0[3], st1[0], st1[1])
```

```python
import functools

import numpy as np
import jax
import jax.numpy as jnp
from jax import lax
from jax.experimental import pallas as pl
from jax.experimental.pallas import tpu as pltpu

F32 = jnp.float32
BF16 = jnp.bfloat16

D_MODEL = 1024
GRID_W = 64
ROPE_THETA = 10000.0
EPS = 1e-6
HG_HEADS = 4
HG_DK = 128
HG_DV = 128
MLA_HEADS = 8
MLA_NOPE = 64
MLA_ROPE = 32
MLA_V = 64
MLA_QK = MLA_NOPE + MLA_ROPE
MLA_Q_LORA = 256
MLA_KV_LORA = 128
MLA_SCALE = MLA_QK ** -0.5
SWA_HEADS = 16
SWA_KV_HEADS = 4
SWA_HD = 64
SWA_WINDOW = 128
SWA_SCALE = SWA_HD ** -0.5
SWA_GROUP = SWA_HEADS // SWA_KV_HEADS
N_EXPERTS = 32
TOP_K = 4
D_FF = 1024
SWIGLU_LIMIT = 7.0
SWIGLU_ALPHA = 1.702

LANES = 128
HG_CHUNK = 128
HG_LEVELS = 7
HG_MXU_LEVELS = 3
AB_IN_PAD = 3072
MOE_ROWS = 256
NEG_BIG = -1e30
VMEM_LIMIT = 48 * 1024 * 1024
EXPERT_VMEM_LIMIT = 56 * 1024 * 1024


def _cparams(sem):
    return pltpu.CompilerParams(dimension_semantics=sem, vmem_limit_bytes=VMEM_LIMIT)


def _dot(a, b):
    return jnp.dot(a, b, preferred_element_type=F32)


def _dot_nt(a, b):
    return lax.dot_general(a, b, (((1,), (1,)), ((), ())), preferred_element_type=F32)


def _dot_tn(a, b):
    return lax.dot_general(a, b, (((0,), (0,)), ((), ())), preferred_element_type=F32)


def _split2(x):
    hi = x.astype(BF16)
    lo = (x - hi.astype(F32)).astype(BF16)
    return hi, lo


def _dot_hp(a, b):
    ah, al = _split2(a)
    bh, bl = _split2(b)
    return _dot(ah, bh) + _dot(ah, bl) + _dot(al, bh)


def _sigmoid(x):
    return 1.0 / (1.0 + jnp.exp(-x))


def _modnorm(x, gain, shift, scale):
    y = x * lax.rsqrt(jnp.mean(x * x, axis=-1, keepdims=True) + EPS)
    return y * gain * (1.0 + scale) + shift


def _ada_kernel(c_ref, w_ref, b_ref, o_ref):
    c = c_ref[...]
    o_ref[...] = _dot_hp(c * _sigmoid(c), w_ref[...]) + b_ref[...]


def ada_params(cond8, w, b):
    n = w.shape[1]
    tn = 1024
    return pl.pallas_call(
        _ada_kernel,
        grid=(n // tn,),
        in_specs=[pl.BlockSpec((8, D_MODEL), lambda j: (0, 0)),
                  pl.BlockSpec((D_MODEL, tn), lambda j: (0, j)),
                  pl.BlockSpec((1, tn), lambda j: (0, j))],
        out_specs=pl.BlockSpec((8, tn), lambda j: (0, j)),
        out_shape=jax.ShapeDtypeStruct((8, n), F32),
        compiler_params=_cparams(("parallel",)),
        name="ada_params",
    )(cond8, w, b.reshape(1, n))


def _modnorm_matmul_kernel(x_ref, g_ref, sh_ref, sc_ref, w_ref, o_ref):
    h = _modnorm(x_ref[...], g_ref[...], sh_ref[0], sc_ref[0])
    o_ref[...] = _dot(h.astype(BF16), w_ref[...])


def modnorm_matmul(x, gain, shift, scale, w_bf16, tokens_per_group, tm=512):
    t, d = x.shape
    n = w_bf16.shape[1]
    tiles_per_group = tokens_per_group // tm
    grp = lambda i: (i // tiles_per_group, 0, 0)
    return pl.pallas_call(
        _modnorm_matmul_kernel,
        grid=(t // tm,),
        in_specs=[pl.BlockSpec((tm, d), lambda i: (i, 0)),
                  pl.BlockSpec((1, d), lambda i: (0, 0)),
                  pl.BlockSpec((1, 1, d), grp),
                  pl.BlockSpec((1, 1, d), grp),
                  pl.BlockSpec((d, n), lambda i: (0, 0))],
        out_specs=pl.BlockSpec((tm, n), lambda i: (i, 0)),
        out_shape=jax.ShapeDtypeStruct((t, n), F32),
        compiler_params=_cparams(("parallel",)),
        name="modnorm_matmul",
    )(x, gain.reshape(1, d), shift, scale, w_bf16)


def _hgrn_constants():
    c = HG_CHUNK
    t = np.arange(c)[:, None]
    u = np.arange(c)[None, :]
    mats = [(u <= t), (u > t)]
    for l in range(HG_MXU_LEVELS):
        m = 1 << l
        r = (t // (2 * m)) * (2 * m) + m - 1
        mats.append((u > np.minimum(t, r)) & (u <= np.maximum(t, r)))
    fwd = np.concatenate(mats, axis=0).astype(np.float32)
    bwd = np.concatenate([mm[::-1, ::-1] for mm in mats], axis=0).astype(np.float32)
    x = np.bitwise_xor(t, u)
    lvl = np.where(x > 0, np.floor(np.log2(np.maximum(x, 1))), HG_LEVELS).astype(np.int32)
    lv_f = np.where(t >= u, lvl, -1).astype(np.int32)
    return fwd, bwd, lv_f, lv_f.T.copy()


def _hgrn_chunk(qs, fpres, vs, lbs, sts, forwards, mcats, lvs):
    c = HG_CHUNK
    n = len(qs)
    fs = [lb + (1.0 - lb) * _sigmoid(fp) for lb, fp in zip(lbs, fpres)]
    kks = [1.0 - f for f in fs]
    logs = [_split2(jnp.log(f)) for f in fs]
    xs = [None] * n
    for fwd in (True, False):
        ids = [i for i in range(n) if forwards[i] == fwd]
        parts = [half for i in ids for half in logs[i]]
        x_all = _dot(mcats[0 if fwd else 1], jnp.concatenate(parts, axis=1))
        for j, i in enumerate(ids):
            xs[i] = x_all[:, 2 * j * c:(2 * j + 1) * c] + x_all[:, (2 * j + 1) * c:(2 * j + 2) * c]
    gs = [x[0:c] for x in xs]
    lv_of = [lvs[0 if f else 1] for f in forwards]
    qbs = [q.astype(BF16) for q in qs]
    kbs = [kk.astype(BF16) for kk in kks]
    vbs = [v.astype(BF16) for v in vs]
    os_ = [_dot_nt((q * jnp.exp(g)).astype(BF16), st.astype(BF16)) for q, g, st in zip(qs, gs, sts)]
    accs = [jnp.where(lv == HG_LEVELS, _dot_nt(qb, kb), 0.0) for lv, qb, kb in zip(lv_of, qbs, kbs)]
    for l in range(HG_LEVELS):
        for i in range(n):
            if l < HG_MXU_LEVELS:
                x = xs[i][(2 + l) * c:(3 + l) * c]
            else:
                m = 1 << l
                ref_rows = [j * 2 * m + (m - 1 if forwards[i] else m) for j in range(c // (2 * m))]
                g_ref = jnp.concatenate(
                    [jnp.broadcast_to(gs[i][r:r + 1, :], (2 * m, c)) for r in ref_rows], axis=0)
                x = -jnp.abs(gs[i] - g_ref)
            e = jnp.exp(x)
            p = _dot_nt((qs[i] * e).astype(BF16), (kks[i] * e).astype(BF16))
            accs[i] = jnp.where(lv_of[i] == l, p, accs[i])
    outs, new_sts = [], []
    for i in range(n):
        edge_row = c - 1 if forwards[i] else 0
        outs.append(os_[i] + _dot(accs[i].astype(BF16), vbs[i]))
        k_end = (kks[i] * jnp.exp(xs[i][c:2 * c])).astype(BF16)
        new_sts.append(sts[i] * jnp.exp(gs[i][edge_row:edge_row + 1, :]) + _dot_tn(vbs[i], k_end))
    return outs, new_sts


def _hgrn_kernel(*refs, has_init):
    if has_init:
        (qf_ref, qb_ref, ff_ref, fb_ref, vf_ref, vb_ref, lbl_ref, mf_ref, mb_ref, lvf_ref, lvb_ref,
         s0f_ref, s0b_ref, of_ref, ob_ref, sf_ref, sb_ref, stf, stb) = refs
    else:
        (qf_ref, qb_ref, ff_ref, fb_ref, vf_ref, vb_ref, lbl_ref, mf_ref, mb_ref, lvf_ref, lvb_ref,
         of_ref, ob_ref, sf_ref, sb_ref, stf, stb) = refs
    c = pl.program_id(1)
    nc = pl.num_programs(1)

    @pl.when(c == 0)
    def _():
        for h in range(HG_HEADS):
            if has_init:
                stf[h] = s0f_ref[0, h].T
                stb[h] = s0b_ref[0, h].T
            else:
                stf[h] = jnp.zeros((HG_DV, HG_DK), F32)
                stb[h] = jnp.zeros((HG_DV, HG_DK), F32)

    rows = [lbl_ref[:, j, :] for j in range(lbl_ref.shape[1])]
    mx = functools.reduce(jnp.maximum, rows)
    ex = [jnp.exp(r - mx) for r in rows]
    lb = ex[0] / functools.reduce(lambda a, b: a + b, ex)

    heads = [slice(h * LANES, (h + 1) * LANES) for h in range(HG_HEADS)]
    nh = HG_HEADS
    outs, sts = _hgrn_chunk(
        [qf_ref[0, :, hs] for hs in heads] + [qb_ref[0, :, hs] for hs in heads],
        [ff_ref[0, :, hs] for hs in heads] + [fb_ref[0, :, hs] for hs in heads],
        [vf_ref[0, :, hs] for hs in heads] + [vb_ref[0, :, hs] for hs in heads],
        [lb[0:1, hs] for hs in heads] + [lb[1:2, hs] for hs in heads],
        [stf[h] for h in range(nh)] + [stb[h] for h in range(nh)],
        [True] * nh + [False] * nh,
        (mf_ref[...], mb_ref[...]), (lvf_ref[...], lvb_ref[...]))
    o_f, o_b, st_f, st_b = outs[:nh], outs[nh:], sts[:nh], sts[nh:]
    for h, hs in enumerate(heads):
        of_ref[0, :, hs] = o_f[h]
        ob_ref[0, :, hs] = o_b[h]
        stf[h] = st_f[h]
        stb[h] = st_b[h]

    @pl.when(c == nc - 1)
    def _():
        for h in range(HG_HEADS):
            sf_ref[0, h] = stf[h].T
            sb_ref[0, h] = stb[h].T


def hgrn_bidir(z, lb_logits, s0f, s0b, batch, seq):
    nc = seq // HG_CHUNK
    z3 = z.reshape(batch, seq, z.shape[1])
    mf, mb, lvf, lvb = _hgrn_constants()
    has_init = s0f is not None
    width = HG_HEADS * LANES
    blk = (1, HG_CHUNK, width)
    fwd = lambda off: pl.BlockSpec(blk, lambda b, c: (b, c, off))
    bwd = lambda off: pl.BlockSpec(blk, lambda b, c: (b, nc - 1 - c, off))
    full = lambda a: pl.BlockSpec(a.shape, lambda b, c: (0,) * a.ndim)
    st_spec = pl.BlockSpec((1, HG_HEADS, HG_DK, HG_DV), lambda b, c: (b, 0, 0, 0))
    consts = [jnp.asarray(mf, BF16), jnp.asarray(mb, BF16), jnp.asarray(lvf), jnp.asarray(lvb)]
    in_specs = [fwd(0), bwd(0), fwd(1), bwd(2), fwd(3), bwd(3),
                pl.BlockSpec(lb_logits.shape, lambda b, c: (0, 0, 0))]
    in_specs += [full(a) for a in consts]
    args = [z3] * 6 + [lb_logits] + consts
    if has_init:
        in_specs += [st_spec, st_spec]
        args += [s0f, s0b]
    o_shape = jax.ShapeDtypeStruct((batch, seq, width), F32)
    s_shape = jax.ShapeDtypeStruct((batch, HG_HEADS, HG_DK, HG_DV), F32)
    o_f, o_b, s_f, s_b = pl.pallas_call(
        functools.partial(_hgrn_kernel, has_init=has_init),
        grid=(batch, nc),
        in_specs=in_specs,
        out_specs=[pl.BlockSpec(blk, lambda b, c: (b, c, 0)),
                   pl.BlockSpec(blk, lambda b, c: (b, nc - 1 - c, 0)),
                   st_spec, st_spec],
        out_shape=[o_shape, o_shape, s_shape, s_shape],
        scratch_shapes=[pltpu.VMEM((HG_HEADS, HG_DV, HG_DK), F32), pltpu.VMEM((HG_HEADS, HG_DV, HG_DK), F32)],
        compiler_params=_cparams(("parallel", "arbitrary")),
        name="hgrn_bidir",
    )(*args)
    t = batch * seq
    return o_f.reshape(t, -1), o_b.reshape(t, -1), s_f, s_b


def _axial_tables(n_tokens, n_rot):
    t = jnp.arange(n_tokens)
    row = (t // GRID_W).astype(F32)
    col = (t % GRID_W).astype(F32)
    n_freq = n_rot // 4
    inv = jnp.power(ROPE_THETA, -jnp.arange(n_freq, dtype=F32) / n_freq)
    ang = jnp.concatenate([row[:, None] * inv, col[:, None] * inv], axis=-1)
    return jnp.cos(ang), jnp.sin(ang)


def _rope_lane_tables(n_tokens, n_rot, head_width, first_rot_lane):
    cos, sin = _axial_tables(n_tokens, n_rot)
    half = n_rot // 2
    c_head = jnp.ones((n_tokens, head_width), F32)
    sa_head = jnp.zeros((n_tokens, head_width), F32)
    sb_head = jnp.zeros((n_tokens, head_width), F32)
    a0, a1, a2 = first_rot_lane, first_rot_lane + half, first_rot_lane + n_rot
    c_head = c_head.at[:, a0:a1].set(cos).at[:, a1:a2].set(cos)
    sa_head = sa_head.at[:, a0:a1].set(-sin)
    sb_head = sb_head.at[:, a1:a2].set(sin)
    reps = LANES // head_width
    tile = lambda a: jnp.tile(a, (1, reps))
    return tile(c_head), tile(sa_head), tile(sb_head)


def _rope(x, c, sa, sb, half):
    return x * c + pltpu.roll(x, LANES - half, 1) * sa + pltpu.roll(x, half, 1) * sb


def _mla_q_kernel(*refs, rope):
    if rope:
        qa_ref, qan_ref, qb_ref, qn_ref, c_ref, sa_ref, sb_ref, o_ref = refs
    else:
        qa_ref, qan_ref, qb_ref, qn_ref, o_ref = refs
    qa = qa_ref[...]
    qn = qa * lax.rsqrt(jnp.mean(qa * qa, axis=-1, keepdims=True) + EPS) * qan_ref[...]
    qfull = _dot(qn.astype(BF16), qb_ref[...])
    outs = []
    for h in range(MLA_HEADS):
        qh = qfull[:, h * LANES:(h + 1) * LANES]
        ms = jnp.sum(qh * qh, axis=-1, keepdims=True) * (1.0 / MLA_QK)
        qh = qh * lax.rsqrt(ms + EPS) * qn_ref[...]
        if rope:
            qh = _rope(qh, c_ref[...], sa_ref[...], sb_ref[...], MLA_ROPE // 2)
        outs.append((qh * MLA_SCALE).astype(BF16))
    o_ref[...] = jnp.concatenate(outs, axis=1)


def mla_queries(z, qa_norm, q_b_pad, q_norm_pad, tables, tokens_per_batch, tm=256):
    t = z.shape[0]
    rope = tables is not None
    row = lambda i: (0, 0)
    in_specs = [pl.BlockSpec((tm, MLA_Q_LORA), lambda i: (i, 2560 // MLA_Q_LORA)),
                pl.BlockSpec((1, MLA_Q_LORA), row),
                pl.BlockSpec(q_b_pad.shape, row),
                pl.BlockSpec((1, LANES), row)]
    args = [z, qa_norm.reshape(1, -1), q_b_pad, q_norm_pad]
    if rope:
        tpb = tokens_per_batch // tm
        in_specs += [pl.BlockSpec((tm, LANES), lambda i: (i % tpb, 0))] * 3
        args += list(tables)
    return pl.pallas_call(
        functools.partial(_mla_q_kernel, rope=rope),
        grid=(t // tm,),
        in_specs=in_specs,
        out_specs=pl.BlockSpec((tm, MLA_HEADS * LANES), lambda i: (i, 0)),
        out_shape=jax.ShapeDtypeStruct((t, MLA_HEADS * LANES), BF16),
        compiler_params=_cparams(("parallel",)),
        name="mla_queries",
    )(*args)


def _mla_kv_kernel(*refs, norm_input, rope):
    if rope:
        kva_ref, kpe_ref, kvan_ref, wk_ref, wv_ref, kn_ref, c_ref, sa_ref, sb_ref, ckv_ref, k_ref, v_ref = refs
    else:
        kva_ref, kpe_ref, kvan_ref, wk_ref, wv_ref, kn_ref, ckv_ref, k_ref, v_ref = refs
    ckv = kva_ref[...]
    if norm_input:
        ckv = ckv * lax.rsqrt(jnp.mean(ckv * ckv, axis=-1, keepdims=True) + EPS) * kvan_ref[...]
    ckv_ref[...] = ckv
    cb = ckv.astype(BF16)
    knope = _dot(cb, wk_ref[...])
    v_ref[...] = _dot(cb, wv_ref[...]).astype(BF16)
    kpe = pltpu.roll(kpe_ref[...], MLA_NOPE, 1)
    outs = []
    for h in range(MLA_HEADS):
        kh = knope[:, h * LANES:(h + 1) * LANES] + kpe
        ms = jnp.sum(kh * kh, axis=-1, keepdims=True) * (1.0 / MLA_QK)
        kh = kh * lax.rsqrt(ms + EPS) * kn_ref[...]
        if rope:
            kh = _rope(kh, c_ref[...], sa_ref[...], sb_ref[...], MLA_ROPE // 2)
        outs.append(kh.astype(BF16))
    k_ref[...] = jnp.concatenate(outs, axis=1)


def mla_keys_values(kva_src, kva_col, kpe_src, kpe_col, kva_norm, wk_pad, wv, k_norm_pad, tables,
                    tokens_per_batch, norm_input, tm=256):
    t = kva_src.shape[0]
    rope = tables is not None
    row = lambda i: (0, 0)
    in_specs = [pl.BlockSpec((tm, LANES), lambda i: (i, kva_col)),
                pl.BlockSpec((tm, LANES), lambda i: (i, kpe_col)),
                pl.BlockSpec((1, LANES), row),
                pl.BlockSpec(wk_pad.shape, row),
                pl.BlockSpec(wv.shape, row),
                pl.BlockSpec((1, LANES), row)]
    args = [kva_src, kpe_src, kva_norm.reshape(1, -1), wk_pad, wv, k_norm_pad]
    if rope:
        tpb = tokens_per_batch // tm
        in_specs += [pl.BlockSpec((tm, LANES), lambda i: (i % tpb, 0))] * 3
        args += list(tables)
    return pl.pallas_call(
        functools.partial(_mla_kv_kernel, norm_input=norm_input, rope=rope),
        grid=(t // tm,),
        in_specs=in_specs,
        out_specs=[pl.BlockSpec((tm, LANES), lambda i: (i, 0)),
                   pl.BlockSpec((tm, MLA_HEADS * LANES), lambda i: (i, 0)),
                   pl.BlockSpec((tm, MLA_HEADS * MLA_V), lambda i: (i, 0))],
        out_shape=[jax.ShapeDtypeStruct((t, LANES), F32),
                   jax.ShapeDtypeStruct((t, MLA_HEADS * LANES), BF16),
                   jax.ShapeDtypeStruct((t, MLA_HEADS * MLA_V), BF16)],
        compiler_params=_cparams(("parallel",)),
        name="mla_keys_values",
    )(*args)


def _mla_attn_kernel(q_ref, k_ref, v_ref, o_ref):
    outs = []
    for j in range(2):
        q = q_ref[0][:, j * LANES:(j + 1) * LANES]
        k = k_ref[0][:, j * LANES:(j + 1) * LANES]
        v = v_ref[0][:, j * MLA_V:(j + 1) * MLA_V]
        s = _dot_nt(q, k)
        p = jnp.exp(s - jnp.max(s, axis=-1, keepdims=True))
        l = jnp.sum(p, axis=-1, keepdims=True)
        outs.append(_dot(p.astype(BF16), v) / l)
    o_ref[0] = jnp.concatenate(outs, axis=1)


def mla_attention(q, k, v, batch, n_q, n_k, tq=256):
    q3 = q.reshape(batch, n_q, -1)
    k3 = k.reshape(batch, n_k, -1)
    v3 = v.reshape(batch, n_k, -1)
    out = pl.pallas_call(
        _mla_attn_kernel,
        grid=(batch, MLA_HEADS // 2, n_q // tq),
        in_specs=[pl.BlockSpec((1, tq, 2 * LANES), lambda b, j, i: (b, i, j)),
                  pl.BlockSpec((1, n_k, 2 * LANES), lambda b, j, i: (b, 0, j)),
                  pl.BlockSpec((1, n_k, 2 * MLA_V), lambda b, j, i: (b, 0, j))],
        out_specs=pl.BlockSpec((1, tq, 2 * MLA_V), lambda b, j, i: (b, i, j)),
        out_shape=jax.ShapeDtypeStruct((batch, n_q, MLA_HEADS * MLA_V), F32),
        compiler_params=_cparams(("parallel", "parallel", "arbitrary")),
        name="mla_attention",
    )(q3, k3, v3)
    return out.reshape(batch * n_q, -1)


def _out0_kernel(of_ref, ob_ref, ag_ref, hgn_ref, om_ref, w_ref, x_ref, g_ref, o_ref):
    o = of_ref[...] + ob_ref[...]
    ag = ag_ref[...]
    parts = []
    for h in range(HG_HEADS):
        oh = o[:, h * HG_DV:(h + 1) * HG_DV]
        oh = oh * lax.rsqrt(jnp.mean(oh * oh, axis=-1, keepdims=True) + EPS) * hgn_ref[...]
        gh = ag[:, h * HG_DV:(h + 1) * HG_DV]
        parts.append((oh * (gh * _sigmoid(gh))).astype(BF16))
    oa = jnp.concatenate(parts, axis=1)
    n_a = HG_HEADS * HG_DV
    mix = _dot(oa, w_ref[0:n_a, :]) + _dot(om_ref[...].astype(BF16), w_ref[n_a:, :])
    o_ref[...] = x_ref[...] + g_ref[0] * mix


def out_proj_layer0(o_f, o_b, z, hg_norm, o_mla, w_out_bf16, x, gate, tokens_per_group, tm=512):
    t, d = x.shape
    n_a = HG_HEADS * HG_DV
    tiles_per_group = tokens_per_group // tm
    tile = lambda w: pl.BlockSpec((tm, w), lambda i: (i, 0))
    return pl.pallas_call(
        _out0_kernel,
        grid=(t // tm,),
        in_specs=[tile(n_a), tile(n_a),
                  pl.BlockSpec((tm, n_a), lambda i: (i, 2048 // n_a)),
                  pl.BlockSpec((1, HG_DV), lambda i: (0, 0)),
                  tile(o_mla.shape[1]),
                  pl.BlockSpec(w_out_bf16.shape, lambda i: (0, 0)),
                  tile(d),
                  pl.BlockSpec((1, 1, d), lambda i: (i // tiles_per_group, 0, 0))],
        out_specs=tile(d),
        out_shape=jax.ShapeDtypeStruct((t, d), F32),
        compiler_params=_cparams(("parallel",)),
        name="out_proj_layer0",
    )(o_f, o_b, z, hg_norm.reshape(1, -1), o_mla, w_out_bf16, x, gate)


def _out1_kernel(a_ref, w_ref, x_ref, g_ref, o_ref):
    o_ref[...] = x_ref[...] + g_ref[0] * _dot(a_ref[...].astype(BF16), w_ref[...])


def out_proj_layer1(a, w_out_bf16, x, gate, tokens_per_group, tm=512):
    t, d = x.shape
    tiles_per_group = tokens_per_group // tm
    tile = lambda w: pl.BlockSpec((tm, w), lambda i: (i, 0))
    return pl.pallas_call(
        _out1_kernel,
        grid=(t // tm,),
        in_specs=[tile(a.shape[1]),
                  pl.BlockSpec(w_out_bf16.shape, lambda i: (0, 0)),
                  tile(d),
                  pl.BlockSpec((1, 1, d), lambda i: (i // tiles_per_group, 0, 0))],
        out_specs=tile(d),
        out_shape=jax.ShapeDtypeStruct((t, d), F32),
        compiler_params=_cparams(("parallel",)),
        name="out_proj_layer1",
    )(a, w_out_bf16, x, gate)


def _head_rms(x, gain2):
    sq = x * x
    lane = lax.broadcasted_iota(jnp.int32, x.shape, 1)
    first = lane < SWA_HD
    lo = jnp.sum(jnp.where(first, sq, 0.0), axis=-1, keepdims=True)
    hi = jnp.sum(jnp.where(first, 0.0, sq), axis=-1, keepdims=True)
    ms = jnp.where(first, lo, hi) * (1.0 / SWA_HD)
    return x * lax.rsqrt(ms + EPS) * gain2


def _swa_prep_kernel(*refs, rope):
    if rope:
        zq_ref, zk_ref, zv_ref, qn_ref, kn_ref, c_ref, sa_ref, sb_ref, q_ref, kc_ref, k_ref, v_ref = refs
    else:
        zq_ref, zk_ref, zv_ref, qn_ref, kn_ref, q_ref, kc_ref, k_ref, v_ref = refs
    half = SWA_HD // 2

    def rot(x):
        return _rope(x, c_ref[...], sa_ref[...], sb_ref[...], half) if rope else x

    zq = zq_ref[...]
    qs = []
    for p in range(zq.shape[1] // LANES):
        x = _head_rms(zq[:, p * LANES:(p + 1) * LANES], qn_ref[...])
        qs.append((rot(x) * SWA_SCALE).astype(BF16))
    q_ref[...] = jnp.concatenate(qs, axis=1)
    zk = zk_ref[...]
    kn, kr = [], []
    for p in range(zk.shape[1] // LANES):
        x = _head_rms(zk[:, p * LANES:(p + 1) * LANES], kn_ref[...])
        kn.append(x)
        kr.append(rot(x).astype(BF16))
    kc_ref[...] = jnp.concatenate(kn, axis=1)
    k_ref[...] = jnp.concatenate(kr, axis=1)
    v_ref[...] = zv_ref[...].astype(BF16)


def swa_prep(z, q_norm2, k_norm2, tables, tokens_per_batch, tm=256):
    t = z.shape[0]
    nq = SWA_HEADS * SWA_HD
    nkv = SWA_KV_HEADS * SWA_HD
    rope = tables is not None
    row = lambda i: (0, 0)
    in_specs = [pl.BlockSpec((tm, nq), lambda i: (i, 0)),
                pl.BlockSpec((tm, nkv), lambda i: (i, nq // nkv)),
                pl.BlockSpec((tm, nkv), lambda i: (i, nq // nkv + 1)),
                pl.BlockSpec((1, LANES), row),
                pl.BlockSpec((1, LANES), row)]
    args = [z, z, z, q_norm2, k_norm2]
    if rope:
        tpb = tokens_per_batch // tm
        in_specs += [pl.BlockSpec((tm, LANES), lambda i: (i % tpb, 0))] * 3
        args += list(tables)
    tile = lambda w: pl.BlockSpec((tm, w), lambda i: (i, 0))
    return pl.pallas_call(
        functools.partial(_swa_prep_kernel, rope=rope),
        grid=(t // tm,),
        in_specs=in_specs,
        out_specs=[tile(nq), tile(nkv), tile(nkv), tile(nkv)],
        out_shape=[jax.ShapeDtypeStruct((t, nq), BF16),
                   jax.ShapeDtypeStruct((t, nkv), F32),
                   jax.ShapeDtypeStruct((t, nkv), BF16),
                   jax.ShapeDtypeStruct((t, nkv), BF16)],
        compiler_params=_cparams(("parallel",)),
        name="swa_prep",
    )(*args)


def _sink_softmax_pv(s, sk, vn):
    m = jnp.maximum(jnp.max(s, axis=-1, keepdims=True), sk)
    p = jnp.exp(s - m)
    l = jnp.sum(p, axis=-1, keepdims=True) + jnp.exp(sk - m)
    return _dot(p.astype(BF16), vn) / l


def _sink_attention(q8, k_all, v_all, bias, sink_ref, pair, stack):
    outs = []
    tq = q8.shape[0]
    for n in range(2):
        kn = k_all[:, n * SWA_HD:(n + 1) * SWA_HD]
        vn = v_all[:, n * SWA_HD:(n + 1) * SWA_HD]
        heads = [n * SWA_GROUP + g for g in range(SWA_GROUP)]
        qs = [q8[:, hq * SWA_HD:(hq + 1) * SWA_HD] for hq in heads]
        sinks = [sink_ref[pair * 2 * SWA_GROUP + hq] for hq in heads]
        if stack:
            sk = jnp.concatenate([jnp.full((tq, 1), s, F32) for s in sinks], axis=0)
            s = _dot_nt(jnp.concatenate(qs, axis=0), kn)
            if bias is not None:
                s = s + bias
            o4 = _sink_softmax_pv(s, sk, vn)
            outs += [o4[g * tq:(g + 1) * tq] for g in range(SWA_GROUP)]
        else:
            for q, sk in zip(qs, sinks):
                s = _dot_nt(q, kn)
                if bias is not None:
                    s = s + bias
                outs.append(_sink_softmax_pv(s, sk, vn))
    return jnp.concatenate(outs, axis=1)


def _swa_dense_kernel(sink_ref, q_ref, k_ref, v_ref, o_ref):
    o_ref[0] = _sink_attention(q_ref[0], k_ref[0], v_ref[0], None, sink_ref, pl.program_id(1), stack=False)


def swa_dense(q, k, v, sink, batch, seq):
    nq = SWA_HEADS * SWA_HD
    q3 = q.reshape(batch, seq, nq)
    k3 = k.reshape(batch, seq, -1)
    v3 = v.reshape(batch, seq, -1)
    out = pl.pallas_call(
        _swa_dense_kernel,
        grid=(batch, 2),
        in_specs=[pl.BlockSpec(memory_space=pltpu.SMEM),
                  pl.BlockSpec((1, seq, nq // 2), lambda b, j: (b, 0, j)),
                  pl.BlockSpec((1, seq, LANES), lambda b, j: (b, 0, j)),
                  pl.BlockSpec((1, seq, LANES), lambda b, j: (b, 0, j))],
        out_specs=pl.BlockSpec((1, seq, nq // 2), lambda b, j: (b, 0, j)),
        out_shape=jax.ShapeDtypeStruct((batch, seq, nq), F32),
        compiler_params=_cparams(("parallel", "parallel")),
        name="swa_dense",
    )(sink, q3, k3, v3)
    return out.reshape(batch * seq, nq)


def _swa_window_kernel(sink_ref, q_ref, kc_ref, kp_ref, k0_ref, kn_ref, vc_ref, vp_ref, v0_ref, vn_ref, o_ref):
    i = pl.program_id(2)
    nb = pl.num_programs(2)
    pair = pl.program_id(1)
    w = SWA_WINDOW
    n_ctx = kc_ref.shape[1]
    n_q = SWA_GROUP * w
    k_all = jnp.concatenate([kc_ref[0], kp_ref[0], k0_ref[0], kn_ref[0]], axis=0)
    vt_all = jnp.concatenate([vc_ref[0], vp_ref[0], v0_ref[0], vn_ref[0]], axis=1)
    shape = (n_ctx + 3 * w, n_q)
    key = lax.broadcasted_iota(jnp.int32, shape, 0) - n_ctx
    col = lax.broadcasted_iota(jnp.int32, shape, 1)
    r = col % w
    prev_bias = jnp.where(i > 0, 0.0, NEG_BIG)
    next_bias = jnp.where(i < nb - 1, 0.0, NEG_BIG)
    bias = jnp.where(
        key < w,
        jnp.where(key < 0, 0.0, jnp.where(key >= r, prev_bias, NEG_BIG)),
        jnp.where(key < 2 * w, 0.0, jnp.where(key - 2 * w <= r, next_bias, NEG_BIG)))
    lane = lax.broadcasted_iota(jnp.int32, (1, n_q), 1)
    q8 = q_ref[0]
    kv = range(2)
    heads = [[n * SWA_GROUP + g for g in range(SWA_GROUP)] for n in kv]
    q4s = [jnp.concatenate([q8[:, hq * SWA_HD:(hq + 1) * SWA_HD] for hq in heads[n]], axis=0) for n in kv]
    sks = []
    for n in kv:
        sk = jnp.zeros((1, n_q), F32)
        for g, hq in enumerate(heads[n]):
            sk = jnp.where(lane // w == g, sink_ref[pair * 2 * SWA_GROUP + hq], sk)
        sks.append(sk)
    ss = [_dot_nt(k_all[:, n * SWA_HD:(n + 1) * SWA_HD], q4s[n]) + bias for n in kv]
    ms = [jnp.maximum(jnp.max(ss[n], axis=0, keepdims=True), sks[n]) for n in kv]
    ps = [jnp.exp(ss[n] - ms[n]) for n in kv]
    ls = [jnp.sum(ps[n], axis=0, keepdims=True) + jnp.exp(sks[n] - ms[n]) for n in kv]
    ots = [_dot(vt_all[n * SWA_HD:(n + 1) * SWA_HD, :], ps[n].astype(BF16)) / ls[n] for n in kv]
    squares = []
    for n in kv:
        for g in range(0, SWA_GROUP, 2):
            two = jnp.concatenate([ots[n][:, g * w:(g + 1) * w], ots[n][:, (g + 1) * w:(g + 2) * w]], axis=0)
            squares.append(two.T)
    o_ref[0] = jnp.concatenate(squares, axis=1)


def swa_window(q, k, v, k_ctx, v_ctx, sink, batch, seq, n_ctx):
    nq = SWA_HEADS * SWA_HD
    w = SWA_WINDOW
    nb = seq // w
    q3 = q.reshape(batch, seq, nq)
    k3 = k.reshape(batch, seq, -1)
    vt = jnp.swapaxes(v.reshape(batch, seq, -1), 1, 2)
    vt_ctx = jnp.swapaxes(v_ctx, 1, 2)
    ctx = pl.BlockSpec((1, n_ctx, LANES), lambda b, j, i: (b, 0, j))
    prv = pl.BlockSpec((1, w, LANES), lambda b, j, i: (b, jnp.maximum(i - 1, 0), j))
    cur = pl.BlockSpec((1, w, LANES), lambda b, j, i: (b, i, j))
    nxt = pl.BlockSpec((1, w, LANES), lambda b, j, i: (b, jnp.minimum(i + 1, nb - 1), j))
    ctx_t = pl.BlockSpec((1, LANES, n_ctx), lambda b, j, i: (b, j, 0))
    prv_t = pl.BlockSpec((1, LANES, w), lambda b, j, i: (b, j, jnp.maximum(i - 1, 0)))
    cur_t = pl.BlockSpec((1, LANES, w), lambda b, j, i: (b, j, i))
    nxt_t = pl.BlockSpec((1, LANES, w), lambda b, j, i: (b, j, jnp.minimum(i + 1, nb - 1)))
    out = pl.pallas_call(
        _swa_window_kernel,
        grid=(batch, 2, nb),
        in_specs=[pl.BlockSpec(memory_space=pltpu.SMEM),
                  pl.BlockSpec((1, w, nq // 2), lambda b, j, i: (b, i, j)),
                  ctx, prv, cur, nxt, ctx_t, prv_t, cur_t, nxt_t],
        out_specs=pl.BlockSpec((1, w, nq // 2), lambda b, j, i: (b, i, j)),
        out_shape=jax.ShapeDtypeStruct((batch, seq, nq), F32),
        compiler_params=_cparams(("parallel", "parallel", "arbitrary")),
        name="swa_window",
    )(sink, q3, k_ctx, k3, k3, k3, vt_ctx, vt, vt, vt)
    return out.reshape(batch * seq, nq)


def _router_kernel(x_ref, g_ref, sh_ref, sc_ref, rw_ref, rb_ref, tri_ref, h_ref, idx_ref, gate_ref, rank_ref,
                   cnt_ref, cnt, *, tiles_per_part):
    @pl.when(pl.program_id(0) % tiles_per_part == 0)
    def _():
        cnt[...] = jnp.zeros_like(cnt)

    h = _modnorm(x_ref[...], g_ref[...], sh_ref[0], sc_ref[0])
    bits = lax.bitcast_convert_type(h.astype(BF16).astype(F32), jnp.uint32)
    half = h.shape[1] // 2
    h_ref[...] = (bits[:, :half] >> 16) | (bits[:, half:] & jnp.uint32(0xFFFF0000))
    wh, wl = _split2(rw_ref[...])
    hh, hl = _split2(h)
    logits = _dot_nt(wh, hh) + _dot_nt(wh, hl) + _dot_nt(wl, hh) + rb_ref[...]
    row = lax.broadcasted_iota(jnp.int32, logits.shape, 0)
    work = logits
    vals, idxs = [], []
    for _ in range(TOP_K):
        m = jnp.max(work, axis=0, keepdims=True)
        ik = jnp.min(jnp.where(work == m, row, LANES), axis=0, keepdims=True)
        vals.append(m)
        idxs.append(ik)
        work = jnp.where(row == ik, 2.0 * NEG_BIG, work)
    es = [jnp.exp(v - vals[0]) for v in vals]
    denom = es[0] + es[1] + es[2] + es[3]
    chosen = jnp.zeros(logits.shape, F32)
    for k in range(TOP_K):
        chosen = jnp.where(row == idxs[k], 1.0, chosen)
    chosen_b = chosen.astype(BF16)
    seen = cnt[...]
    before = _dot(chosen_b, tri_ref[...]) + jnp.concatenate([seen] * (logits.shape[1] // LANES), axis=1)
    out_row = lax.broadcasted_iota(jnp.int32, idx_ref.shape, 0)
    idx_out = jnp.zeros(idx_ref.shape, jnp.int32)
    gate_out = jnp.zeros(idx_ref.shape, F32)
    rank_out = jnp.zeros(idx_ref.shape, jnp.int32)
    for k in range(TOP_K):
        rk = jnp.sum(jnp.where(row == idxs[k], before, 0.0), axis=0, keepdims=True)
        idx_out = jnp.where(out_row == k, idxs[k], idx_out)
        gate_out = jnp.where(out_row == k, es[k] / denom, gate_out)
        rank_out = jnp.where(out_row == k, rk.astype(jnp.int32), rank_out)
    idx_ref[...] = idx_out
    gate_ref[...] = gate_out
    rank_ref[...] = rank_out
    cnt[...] = seen + _dot(chosen_b, jnp.ones((logits.shape[1], LANES), BF16))
    cnt_ref[0] = cnt[...]


def moe_router(x, gain, shift, scale, rw_t, rb_col, tokens_per_group, tm=256):
    t, d = x.shape
    tiles_per_group = tokens_per_group // tm
    tiles_per_part = MOE_PART // tm
    grp = lambda i: (i // tiles_per_group, 0, 0)
    tile = lambda w: pl.BlockSpec((tm, w), lambda i: (i, 0))
    slab = pl.BlockSpec((8, tm), lambda i: (0, i))
    tri = jnp.asarray(np.triu(np.ones((tm, tm), np.float32), 1), BF16)
    hp, idx, gates, rank, cnt = pl.pallas_call(
        functools.partial(_router_kernel, tiles_per_part=tiles_per_part),
        grid=(t // tm,),
        in_specs=[tile(d),
                  pl.BlockSpec((1, d), lambda i: (0, 0)),
                  pl.BlockSpec((1, 1, d), grp),
                  pl.BlockSpec((1, 1, d), grp),
                  pl.BlockSpec((LANES, d), lambda i: (0, 0)),
                  pl.BlockSpec((LANES, 1), lambda i: (0, 0)),
                  pl.BlockSpec((tm, tm), lambda i: (0, 0))],
        out_specs=[tile(d // 2), slab, slab, slab,
                   pl.BlockSpec((1, LANES, LANES), lambda i: (i // tiles_per_part, 0, 0))],
        out_shape=[jax.ShapeDtypeStruct((t, d // 2), jnp.uint32),
                   jax.ShapeDtypeStruct((8, t), jnp.int32),
                   jax.ShapeDtypeStruct((8, t), F32),
                   jax.ShapeDtypeStruct((8, t), jnp.int32),
                   jax.ShapeDtypeStruct((t // MOE_PART, LANES, LANES), F32)],
        scratch_shapes=[pltpu.VMEM((LANES, LANES), F32)],
        compiler_params=_cparams(("arbitrary",)),
        name="moe_router",
    )(x, gain.reshape(1, d), shift, scale, rw_t, rb_col, tri)
    per_token = lambda a: a[:TOP_K].T
    return hp, per_token(idx), per_token(gates), per_token(rank), cnt[:, :N_EXPERTS, 0].astype(jnp.int32)


FFN_COLS = 512
FFN_PIECES = 2 * D_FF // FFN_COLS + D_MODEL // FFN_COLS
MOE_PART = 4096
ADD_BATCH = 8


def _unpack_rows(words):
    lo = lax.bitcast_convert_type(words << 16, F32)
    hi = lax.bitcast_convert_type(words & jnp.uint32(0xFFFF0000), F32)
    return jnp.concatenate([lo, hi], axis=1).astype(BF16)


def _ffn_block(x_ref, y_ref, wgu_ref, bgu_ref, wd_ref, bd_ref, between):
    xb = _unpack_rows(x_ref[...])
    acts = []
    for c in range(D_FF // FFN_COLS):
        lo, hi = c * FFN_COLS, (c + 1) * FFN_COLS
        between(2 * c)
        gate = _dot(xb, wgu_ref[0, :, lo:hi]) + bgu_ref[0][:, lo:hi]
        between(2 * c + 1)
        up = _dot(xb, wgu_ref[0, :, D_FF + lo:D_FF + hi]) + bgu_ref[0][:, D_FF + lo:D_FF + hi]
        gate = jnp.minimum(gate, SWIGLU_LIMIT)
        up = jnp.clip(up, -SWIGLU_LIMIT, SWIGLU_LIMIT)
        acts.append((gate * _sigmoid(SWIGLU_ALPHA * gate) * (up + 1.0)).astype(BF16))
    act = jnp.concatenate(acts, axis=1)
    for n in range(D_MODEL // FFN_COLS):
        lo, hi = n * FFN_COLS, (n + 1) * FFN_COLS
        between(2 * D_FF // FFN_COLS + n)
        y_ref[:, lo:hi] = _dot(act, wd_ref[0, :, lo:hi]) + bd_ref[0][:, lo:hi]


def _expert_kernel(nb_ref, bs_ref, loc_ref, gate_ref, hp_hbm, wgu_ref, bgu_ref, wd_ref, bd_ref, out_hbm,
                   hbuf, acc, xbuf, ybuf, sem):
    part = pl.program_id(0)
    e = pl.program_id(1)
    idx = part * N_EXPERTS + e
    nb = nb_ref[idx]
    b0 = bs_ref[idx]
    shares = np.array_split(np.arange(MOE_ROWS), FFN_PIECES)

    def fetch_rows(blk):
        base = blk * MOE_ROWS
        dst = xbuf.at[blk % 2]

        def emit(k):
            for r in shares[k]:
                r = int(r)
                dst[r:r + 1, :] = hbuf[pl.ds(loc_ref[base + r], 1), :]
        return emit

    def add_rows(blk):
        base = blk * MOE_ROWS
        src = ybuf.at[blk % 2]

        def emit(k):
            for batch in np.array_split(shares[k], max(1, len(shares[k]) // ADD_BATCH)):
                rows = [loc_ref[base + int(r)] for r in batch]
                new = [acc[pl.ds(row, 1), :] + gate_ref[base + int(r)] * src[int(r):int(r) + 1, :]
                       for row, r in zip(rows, batch)]
                for row, val in zip(rows, new):
                    acc[pl.ds(row, 1), :] = val
        return emit

    def emit_all(f):
        for k in range(FFN_PIECES):
            f(k)

    @pl.when(e == 0)
    def _():
        load = pltpu.make_async_copy(hp_hbm.at[pl.ds(part * MOE_PART, MOE_PART), :],
                                     hbuf.at[pl.ds(0, MOE_PART), :], sem.at[0])
        load.start()
        hbuf[MOE_PART:, :] = jnp.zeros((hbuf.shape[0] - MOE_PART, hbuf.shape[1]), hbuf.dtype)
        acc[...] = jnp.zeros_like(acc)
        ybuf[...] = jnp.zeros_like(ybuf)
        load.wait()
        emit_all(fetch_rows(b0))

    def block(i, carry):
        fetch, add = fetch_rows(i + 1), add_rows(i - 1)

        def between(k):
            fetch(k)
            add(k)
        _ffn_block(xbuf.at[i % 2], ybuf.at[i % 2], wgu_ref, bgu_ref, wd_ref, bd_ref, between)
        return carry

    lax.fori_loop(b0, b0 + nb, block, 0)

    @pl.when(e == N_EXPERTS - 1)
    def _():
        emit_all(add_rows(b0 + nb - 1))
        store = pltpu.make_async_copy(acc.at[pl.ds(0, MOE_PART), :],
                                      out_hbm.at[pl.ds(part * MOE_PART, MOE_PART), :], sem.at[0])
        store.start()
        store.wait()


def moe_experts(hp, n_blk, blk_start, row_loc, row_gate, w_gu, b_gu, w_down, b_down):
    t = hp.shape[0]
    d = D_MODEL
    per_expert = lambda p, e, nb, bs, sl: (e, 0, 0)
    grid_spec = pltpu.PrefetchScalarGridSpec(
        num_scalar_prefetch=3,
        grid=(t // MOE_PART, N_EXPERTS),
        in_specs=[pl.BlockSpec(memory_space=pltpu.SMEM),
                  pl.BlockSpec(memory_space=pl.ANY),
                  pl.BlockSpec((1, d, 2 * D_FF), per_expert),
                  pl.BlockSpec((1, 1, 2 * D_FF), per_expert),
                  pl.BlockSpec((1, D_FF, d), per_expert),
                  pl.BlockSpec((1, 1, d), per_expert)],
        out_specs=pl.BlockSpec(memory_space=pl.ANY),
        scratch_shapes=[pltpu.VMEM((MOE_PART + 8, d // 2), jnp.uint32),
                        pltpu.VMEM((MOE_PART + 8, d), F32),
                        pltpu.VMEM((2, MOE_ROWS, d // 2), jnp.uint32),
                        pltpu.VMEM((2, MOE_ROWS, d), F32),
                        pltpu.SemaphoreType.DMA((1,))],
    )
    return pl.pallas_call(
        _expert_kernel,
        grid_spec=grid_spec,
        out_shape=jax.ShapeDtypeStruct((t, d), F32),
        compiler_params=pltpu.CompilerParams(dimension_semantics=("arbitrary", "arbitrary"),
                                             vmem_limit_bytes=EXPERT_VMEM_LIMIT),
        name="moe_experts",
    )(n_blk, blk_start, row_loc, row_gate, hp, w_gu, b_gu.reshape(N_EXPERTS, 1, -1), w_down,
      b_down.reshape(N_EXPERTS, 1, -1))


def _residual_kernel(y_ref, x_ref, g_ref, o_ref):
    o_ref[...] = x_ref[...] + g_ref[0] * y_ref[...]


def gated_residual(y, x, gate_mod, tokens_per_group, tm=512):
    t, d = x.shape
    tiles_per_group = tokens_per_group // tm
    tile = pl.BlockSpec((tm, d), lambda i: (i, 0))
    return pl.pallas_call(
        _residual_kernel,
        grid=(t // tm,),
        in_specs=[tile, tile, pl.BlockSpec((1, 1, d), lambda i: (i // tiles_per_group, 0, 0))],
        out_specs=tile,
        out_shape=jax.ShapeDtypeStruct((t, d), F32),
        compiler_params=_cparams(("parallel",)),
        name="moe_residual",
    )(y, x, gate_mod)


def _assignment_tables(top_idx, rank, gates, counts):
    t = top_idx.shape[0]
    n_parts = t // MOE_PART
    n_assign = t * TOP_K
    part_blocks = MOE_PART * TOP_K // MOE_ROWS + N_EXPERTS + 1
    n_rows = (1 + n_parts * part_blocks) * MOE_ROWS
    padded = (counts + MOE_ROWS - 1) // MOE_ROWS * MOE_ROWS
    part_first = (1 + jnp.arange(n_parts, dtype=jnp.int32) * part_blocks) * MOE_ROWS
    row_start = part_first[:, None] + jnp.cumsum(padded, axis=1) - padded
    experts = jnp.arange(N_EXPERTS, dtype=jnp.int32)
    by_part = top_idx.reshape(n_parts, MOE_PART, TOP_K)
    start_of = jnp.sum(jnp.where(by_part[..., None] == experts, row_start[:, None, None, :], 0), axis=-1)
    dest = (start_of.reshape(t, TOP_K) + rank).reshape(-1).astype(jnp.int32)
    local = (jnp.arange(n_assign, dtype=jnp.int32) // TOP_K) % MOE_PART
    vals = jnp.stack([local, lax.bitcast_convert_type(gates.reshape(-1), jnp.int32)], axis=1)
    init = jnp.broadcast_to(jnp.array([MOE_PART, 0], jnp.int32), (n_rows, 2))
    table = init.at[dest].set(vals)
    row_loc = table[:, 0]
    row_gate = lax.bitcast_convert_type(table[:, 1], F32)
    return row_loc, row_gate, (padded // MOE_ROWS).reshape(-1).astype(jnp.int32), \
        (row_start // MOE_ROWS).reshape(-1).astype(jnp.int32)


def moe_layer(x, p, shift, scale, gate_mod, tokens_per_group):
    hp, idx, gates, rank, counts = moe_router(x, p['norm2'], shift, scale, p['rw_t'], p['rb_col'], tokens_per_group)
    row_loc, row_gate, n_blk, blk_start = _assignment_tables(idx, rank, gates, counts)
    y = moe_experts(hp, n_blk, blk_start, row_loc, row_gate, p['w_gu_bf'], p['b_gu'], p['w_down_bf'], p['b_down'])
    return gated_residual(y, x, gate_mod, tokens_per_group)


def _pad_lanes(a, width):
    return jnp.pad(a, [(0, 0)] * (a.ndim - 1) + [(0, width - a.shape[-1])])


def _prep_common(p):
    p['rw_t'] = _pad_lanes(p['router_w'], LANES).T
    p['rb_col'] = jnp.concatenate(
        [p['router_b'].astype(F32), jnp.full((LANES - N_EXPERTS,), NEG_BIG, F32)]).reshape(LANES, 1)
    p['w_out'] = p['w_out'].astype(BF16)
    p['w_gu_bf'] = p['w_gu'].astype(BF16)
    p['w_down_bf'] = p['w_down'].astype(BF16)
    return p


def _prep_layer0(p):
    p = _prep_common(dict(p))
    p['w_in'] = _pad_lanes(p['w_in'], AB_IN_PAD).astype(BF16)
    q_b = p['q_b'].reshape(MLA_Q_LORA, MLA_HEADS, MLA_QK)
    p['q_b_pad'] = _pad_lanes(q_b, LANES).reshape(MLA_Q_LORA, MLA_HEADS * LANES).astype(BF16)
    kv_b = p['kv_b'].reshape(MLA_KV_LORA, MLA_HEADS, MLA_NOPE + MLA_V)
    p['wk_pad'] = _pad_lanes(kv_b[:, :, :MLA_NOPE], LANES).reshape(MLA_KV_LORA, MLA_HEADS * LANES).astype(BF16)
    p['wv'] = kv_b[:, :, MLA_NOPE:].reshape(MLA_KV_LORA, MLA_HEADS * MLA_V).astype(BF16)
    p['q_norm_pad'] = _pad_lanes(p['q_norm'].reshape(1, -1), LANES)
    p['k_norm_pad'] = _pad_lanes(p['k_norm'].reshape(1, -1), LANES)
    return p


def _prep_layer1(p):
    p = _prep_common(dict(p))
    p['w_in'] = p['w_in'].astype(BF16)
    p['q_norm2'] = jnp.tile(p['q_norm'].reshape(1, -1), (1, LANES // SWA_HD))
    p['k_norm2'] = jnp.tile(p['k_norm'].reshape(1, -1), (1, LANES // SWA_HD))
    return p


def _group_forward(x3, mods, p0, p1, lb_logits, caches, latent):
    batch, seq, d = x3.shape
    t = batch * seq
    x = x3.reshape(t, d)
    tpg = seq if latent else t
    sh1, sc1, g1, sh2, sc2, g2 = mods[0]

    z = modnorm_matmul(x, p0['norm1'], sh1, sc1, p0['w_in'], tpg)
    tab_b = _rope_lane_tables(seq, MLA_ROPE, LANES, MLA_NOPE) if latent else None
    s0f, s0b = (caches['hg_f'], caches['hg_b']) if latent else (None, None)
    o_f, o_b, s_f, s_b = hgrn_bidir(z, lb_logits, s0f, s0b, batch, seq)
    q = mla_queries(z, p0['qa_norm'], p0['q_b_pad'], p0['q_norm_pad'], tab_b, seq)
    ckv, k, v = mla_keys_values(z, 2816 // LANES, z, 2944 // LANES, p0['kva_norm'], p0['wk_pad'], p0['wv'],
                                p0['k_norm_pad'], tab_b, seq, norm_input=True)
    n_k = seq
    if latent:
        n_ctx = caches['ckv'].shape[1]
        ckv_c = caches['ckv'].reshape(batch * n_ctx, MLA_KV_LORA)
        kpe_c = _pad_lanes(caches['kpe'].reshape(batch * n_ctx, MLA_ROPE), LANES)
        _, k_c, v_c = mla_keys_values(ckv_c, 0, kpe_c, 0, p0['kva_norm'], p0['wk_pad'], p0['wv'],
                                      p0['k_norm_pad'], None, n_ctx, norm_input=False)
        cat = lambda a, b: jnp.concatenate([a.reshape(batch, n_ctx, -1), b.reshape(batch, seq, -1)],
                                           axis=1).reshape(batch * (n_ctx + seq), -1)
        k, v = cat(k_c, k), cat(v_c, v)
        n_k = n_ctx + seq
    o_mla = mla_attention(q, k, v, batch, seq, n_k)
    x = out_proj_layer0(o_f, o_b, z, p0['hg_out_norm'], o_mla, p0['w_out'], x, g1, tpg)
    x = moe_layer(x, p0, sh2, sc2, g2, tpg)
    state0 = (s_f, s_b, ckv.reshape(batch, seq, MLA_KV_LORA), z[:, 2944:2944 + MLA_ROPE].reshape(batch, seq, MLA_ROPE))

    sh1, sc1, g1, sh2, sc2, g2 = mods[1]
    z = modnorm_matmul(x, p1['norm1'], sh1, sc1, p1['w_in'], tpg)
    tab_c = _rope_lane_tables(seq, SWA_HD, SWA_HD, 0) if latent else None
    q, k_cache, k, v = swa_prep(z, p1['q_norm2'], p1['k_norm2'], tab_c, seq)
    sink = p1['sink'].astype(F32)
    if latent:
        n_ctx = caches['k1'].shape[1]
        k_c = caches['k1'].reshape(batch, n_ctx, -1).astype(BF16)
        v_c = caches['v1'].reshape(batch, n_ctx, -1).astype(BF16)
        a = swa_window(q, k, v, k_c, v_c, sink, batch, seq, n_ctx)
    else:
        a = swa_dense(q, k, v, sink, batch, seq)
    x = out_proj_layer1(a, p1['w_out'], x, g1, tpg)
    x = moe_layer(x, p1, sh2, sc2, g2, tpg)
    nkv = SWA_KV_HEADS * SWA_HD
    state1 = (k_cache.reshape(batch, seq, SWA_KV_HEADS, SWA_HD),
              z[:, SWA_HEADS * SWA_HD + nkv:].reshape(batch, seq, SWA_KV_HEADS, SWA_HD))
    return x.reshape(batch, seq, d), state0, state1


def kernel(x_prompt, x_sample, state_l0_hgrn_fwd, state_l0_hgrn_bwd, cache_l0_mla_ckv, cache_l0_mla_kpe, cache_l1_k, cache_l1_v, c, c_ctx, hgrn_lb_logits, l0_ada_w, l0_ada_b, l0_norm1, l0_norm2, l0_w_in, l0_hg_out_norm, l0_qa_norm, l0_q_b, l0_kva_norm, l0_kv_b, l0_q_norm, l0_k_norm, l0_w_out, l0_router_w, l0_router_b, l0_w_gu, l0_b_gu, l0_w_down, l0_b_down, l1_ada_w, l1_ada_b, l1_norm1, l1_norm2, l1_w_in, l1_q_norm, l1_k_norm, l1_sink, l1_w_out, l1_router_w, l1_router_b, l1_w_gu, l1_b_gu, l1_w_down, l1_b_down):
    p0 = _prep_layer0(dict(norm1=l0_norm1, norm2=l0_norm2, w_in=l0_w_in, hg_out_norm=l0_hg_out_norm,
                           qa_norm=l0_qa_norm, q_b=l0_q_b, kva_norm=l0_kva_norm, kv_b=l0_kv_b,
                           q_norm=l0_q_norm, k_norm=l0_k_norm, w_out=l0_w_out, router_w=l0_router_w,
                           router_b=l0_router_b, w_gu=l0_w_gu, b_gu=l0_b_gu, w_down=l0_w_down,
                           b_down=l0_b_down))
    p1 = _prep_layer1(dict(norm1=l1_norm1, norm2=l1_norm2, w_in=l1_w_in, q_norm=l1_q_norm, k_norm=l1_k_norm,
                           sink=l1_sink, w_out=l1_w_out, router_w=l1_router_w, router_b=l1_router_b,
                           w_gu=l1_w_gu, b_gu=l1_b_gu, w_down=l1_w_down, b_down=l1_b_down))
    dec_batch = c.shape[0]
    d = c.shape[1]
    cond8 = jnp.concatenate([c_ctx[None, :], c, jnp.zeros((8 - 1 - dec_batch, d), F32)], axis=0)
    mods_ctx, mods_lat = [], []
    for w, b in ((l0_ada_w, l0_ada_b), (l1_ada_w, l1_ada_b)):
        mod = ada_params(cond8, w, b)
        mods_ctx.append([m.reshape(1, 1, d) for m in jnp.split(mod[0:1], 6, axis=-1)])
        mods_lat.append([m.reshape(dec_batch, 1, d) for m in jnp.split(mod[1:1 + dec_batch], 6, axis=-1)])

    y_prompt, st0, st1 = _group_forward(x_prompt, mods_ctx, p0, p1, hgrn_lb_logits, None, latent=False)
    caches = dict(hg_f=state_l0_hgrn_fwd, hg_b=state_l0_hgrn_bwd, ckv=cache_l0_mla_ckv, kpe=cache_l0_mla_kpe,
                  k1=cache_l1_k, v1=cache_l1_v)
    y_sample, _, _ = _group_forward(x_sample, mods_lat, p0, p1, hgrn_lb_logits, caches, latent=True)
    return (y_prompt, y_sample, st0[0], st0[1], st0[2], st0[3], st1[0], st1[1])
```

```python
import functools

import numpy as np
import jax
import jax.numpy as jnp
from jax import lax
from jax.experimental import pallas as pl
from jax.experimental.pallas import tpu as pltpu

F32 = jnp.float32
BF16 = jnp.bfloat16

D_MODEL = 1024
GRID_W = 64
ROPE_THETA = 10000.0
EPS = 1e-6
HG_HEADS = 4
HG_DK = 128
HG_DV = 128
MLA_HEADS = 8
MLA_NOPE = 64
MLA_ROPE = 32
MLA_V = 64
MLA_QK = MLA_NOPE + MLA_ROPE
MLA_Q_LORA = 256
MLA_KV_LORA = 128
MLA_SCALE = MLA_QK ** -0.5
SWA_HEADS = 16
SWA_KV_HEADS = 4
SWA_HD = 64
SWA_WINDOW = 128
SWA_SCALE = SWA_HD ** -0.5
SWA_GROUP = SWA_HEADS // SWA_KV_HEADS
N_EXPERTS = 32
TOP_K = 4
D_FF = 1024
SWIGLU_LIMIT = 7.0
SWIGLU_ALPHA = 1.702

LANES = 128
HG_CHUNK = 128
HG_LEVELS = 7
HG_MXU_LEVELS = 3
AB_IN_PAD = 3072
MOE_ROWS = 256
NEG_BIG = -1e30
VMEM_LIMIT = 48 * 1024 * 1024
EXPERT_VMEM_LIMIT = 56 * 1024 * 1024


def _cparams(sem):
    return pltpu.CompilerParams(dimension_semantics=sem, vmem_limit_bytes=VMEM_LIMIT)


def _dot(a, b):
    return jnp.dot(a, b, preferred_element_type=F32)


def _dot_nt(a, b):
    return lax.dot_general(a, b, (((1,), (1,)), ((), ())), preferred_element_type=F32)


def _dot_tn(a, b):
    return lax.dot_general(a, b, (((0,), (0,)), ((), ())), preferred_element_type=F32)


def _split2(x):
    hi = x.astype(BF16)
    lo = (x - hi.astype(F32)).astype(BF16)
    return hi, lo


def _dot_hp(a, b):
    ah, al = _split2(a)
    bh, bl = _split2(b)
    return _dot(ah, bh) + _dot(ah, bl) + _dot(al, bh)


def _sigmoid(x):
    return 1.0 / (1.0 + jnp.exp(-x))


def _modnorm(x, gain, shift, scale):
    y = x * lax.rsqrt(jnp.mean(x * x, axis=-1, keepdims=True) + EPS)
    return y * gain * (1.0 + scale) + shift


def _ada_kernel(c_ref, w_ref, b_ref, o_ref):
    c = c_ref[...]
    o_ref[...] = _dot_hp(c * _sigmoid(c), w_ref[...]) + b_ref[...]


def ada_params(cond8, w, b):
    n = w.shape[1]
    tn = 1024
    return pl.pallas_call(
        _ada_kernel,
        grid=(n // tn,),
        in_specs=[pl.BlockSpec((8, D_MODEL), lambda j: (0, 0)),
                  pl.BlockSpec((D_MODEL, tn), lambda j: (0, j)),
                  pl.BlockSpec((1, tn), lambda j: (0, j))],
        out_specs=pl.BlockSpec((8, tn), lambda j: (0, j)),
        out_shape=jax.ShapeDtypeStruct((8, n), F32),
        compiler_params=_cparams(("parallel",)),
        name="ada_params",
    )(cond8, w, b.reshape(1, n))


def _modnorm_matmul_kernel(x_ref, g_ref, sh_ref, sc_ref, w_ref, o_ref):
    h = _modnorm(x_ref[...], g_ref[...], sh_ref[0], sc_ref[0])
    o_ref[...] = _dot(h.astype(BF16), w_ref[...])


def modnorm_matmul(x, gain, shift, scale, w_bf16, tokens_per_group, tm=512):
    t, d = x.shape
    n = w_bf16.shape[1]
    tiles_per_group = tokens_per_group // tm
    grp = lambda i: (i // tiles_per_group, 0, 0)
    return pl.pallas_call(
        _modnorm_matmul_kernel,
        grid=(t // tm,),
        in_specs=[pl.BlockSpec((tm, d), lambda i: (i, 0)),
                  pl.BlockSpec((1, d), lambda i: (0, 0)),
                  pl.BlockSpec((1, 1, d), grp),
                  pl.BlockSpec((1, 1, d), grp),
                  pl.BlockSpec((d, n), lambda i: (0, 0))],
        out_specs=pl.BlockSpec((tm, n), lambda i: (i, 0)),
        out_shape=jax.ShapeDtypeStruct((t, n), F32),
        compiler_params=_cparams(("parallel",)),
        name="modnorm_matmul",
    )(x, gain.reshape(1, d), shift, scale, w_bf16)


def _hgrn_constants():
    c = HG_CHUNK
    t = np.arange(c)[:, None]
    u = np.arange(c)[None, :]
    mats = [(u <= t), (u > t)]
    for l in range(HG_MXU_LEVELS):
        m = 1 << l
        r = (t // (2 * m)) * (2 * m) + m - 1
        mats.append((u > np.minimum(t, r)) & (u <= np.maximum(t, r)))
    fwd = np.concatenate(mats, axis=0).astype(np.float32)
    bwd = np.concatenate([mm[::-1, ::-1] for mm in mats], axis=0).astype(np.float32)
    x = np.bitwise_xor(t, u)
    lvl = np.where(x > 0, np.floor(np.log2(np.maximum(x, 1))), HG_LEVELS).astype(np.int32)
    lv_f = np.where(t >= u, lvl, -1).astype(np.int32)
    return fwd, bwd, lv_f, lv_f.T.copy()


def _hgrn_chunk(qs, fpres, vs, lbs, sts, forwards, mcats, lvs):
    c = HG_CHUNK
    n = len(qs)
    fs = [lb + (1.0 - lb) * _sigmoid(fp) for lb, fp in zip(lbs, fpres)]
    kks = [1.0 - f for f in fs]
    logs = [_split2(jnp.log(f)) for f in fs]
    xs = [None] * n
    for fwd in (True, False):
        ids = [i for i in range(n) if forwards[i] == fwd]
        parts = [half for i in ids for half in logs[i]]
        x_all = _dot(mcats[0 if fwd else 1], jnp.concatenate(parts, axis=1))
        for j, i in enumerate(ids):
            xs[i] = x_all[:, 2 * j * c:(2 * j + 1) * c] + x_all[:, (2 * j + 1) * c:(2 * j + 2) * c]
    gs = [x[0:c] for x in xs]
    lv_of = [lvs[0 if f else 1] for f in forwards]
    qbs = [q.astype(BF16) for q in qs]
    kbs = [kk.astype(BF16) for kk in kks]
    vbs = [v.astype(BF16) for v in vs]
    os_ = [_dot_nt((q * jnp.exp(g)).astype(BF16), st.astype(BF16)) for q, g, st in zip(qs, gs, sts)]
    accs = [jnp.where(lv == HG_LEVELS, _dot_nt(qb, kb), 0.0) for lv, qb, kb in zip(lv_of, qbs, kbs)]
    for l in range(HG_LEVELS):
        for i in range(n):
            if l < HG_MXU_LEVELS:
                x = xs[i][(2 + l) * c:(3 + l) * c]
            else:
                m = 1 << l
                ref_rows = [j * 2 * m + (m - 1 if forwards[i] else m) for j in range(c // (2 * m))]
                g_ref = jnp.concatenate(
                    [jnp.broadcast_to(gs[i][r:r + 1, :], (2 * m, c)) for r in ref_rows], axis=0)
                x = -jnp.abs(gs[i] - g_ref)
            e = jnp.exp(x)
            p = _dot_nt((qs[i] * e).astype(BF16), (kks[i] * e).astype(BF16))
            accs[i] = jnp.where(lv_of[i] == l, p, accs[i])
    outs, new_sts = [], []
    for i in range(n):
        edge_row = c - 1 if forwards[i] else 0
        outs.append(os_[i] + _dot(accs[i].astype(BF16), vbs[i]))
        k_end = (kks[i] * jnp.exp(xs[i][c:2 * c])).astype(BF16)
        new_sts.append(sts[i] * jnp.exp(gs[i][edge_row:edge_row + 1, :]) + _dot_tn(vbs[i], k_end))
    return outs, new_sts


def _hgrn_kernel(*refs, has_init):
    if has_init:
        (qf_ref, qb_ref, ff_ref, fb_ref, vf_ref, vb_ref, lbl_ref, mf_ref, mb_ref, lvf_ref, lvb_ref,
         s0f_ref, s0b_ref, of_ref, ob_ref, sf_ref, sb_ref, stf, stb) = refs
    else:
        (qf_ref, qb_ref, ff_ref, fb_ref, vf_ref, vb_ref, lbl_ref, mf_ref, mb_ref, lvf_ref, lvb_ref,
         of_ref, ob_ref, sf_ref, sb_ref, stf, stb) = refs
    c = pl.program_id(1)
    nc = pl.num_programs(1)

    @pl.when(c == 0)
    def _():
        for h in range(HG_HEADS):
            if has_init:
                stf[h] = s0f_ref[0, h].T
                stb[h] = s0b_ref[0, h].T
            else:
                stf[h] = jnp.zeros((HG_DV, HG_DK), F32)
                stb[h] = jnp.zeros((HG_DV, HG_DK), F32)

    rows = [lbl_ref[:, j, :] for j in range(lbl_ref.shape[1])]
    mx = functools.reduce(jnp.maximum, rows)
    ex = [jnp.exp(r - mx) for r in rows]
    lb = ex[0] / functools.reduce(lambda a, b: a + b, ex)

    heads = [slice(h * LANES, (h + 1) * LANES) for h in range(HG_HEADS)]
    nh = HG_HEADS
    outs, sts = _hgrn_chunk(
        [qf_ref[0, :, hs] for hs in heads] + [qb_ref[0, :, hs] for hs in heads],
        [ff_ref[0, :, hs] for hs in heads] + [fb_ref[0, :, hs] for hs in heads],
        [vf_ref[0, :, hs] for hs in heads] + [vb_ref[0, :, hs] for hs in heads],
        [lb[0:1, hs] for hs in heads] + [lb[1:2, hs] for hs in heads],
        [stf[h] for h in range(nh)] + [stb[h] for h in range(nh)],
        [True] * nh + [False] * nh,
        (mf_ref[...], mb_ref[...]), (lvf_ref[...], lvb_ref[...]))
    o_f, o_b, st_f, st_b = outs[:nh], outs[nh:], sts[:nh], sts[nh:]
    for h, hs in enumerate(heads):
        of_ref[0, :, hs] = o_f[h]
        ob_ref[0, :, hs] = o_b[h]
        stf[h] = st_f[h]
        stb[h] = st_b[h]

    @pl.when(c == nc - 1)
    def _():
        for h in range(HG_HEADS):
            sf_ref[0, h] = stf[h].T
            sb_ref[0, h] = stb[h].T


def hgrn_bidir(z, lb_logits, s0f, s0b, batch, seq):
    nc = seq // HG_CHUNK
    z3 = z.reshape(batch, seq, z.shape[1])
    mf, mb, lvf, lvb = _hgrn_constants()
    has_init = s0f is not None
    width = HG_HEADS * LANES
    blk = (1, HG_CHUNK, width)
    fwd = lambda off: pl.BlockSpec(blk, lambda b, c: (b, c, off))
    bwd = lambda off: pl.BlockSpec(blk, lambda b, c: (b, nc - 1 - c, off))
    full = lambda a: pl.BlockSpec(a.shape, lambda b, c: (0,) * a.ndim)
    st_spec = pl.BlockSpec((1, HG_HEADS, HG_DK, HG_DV), lambda b, c: (b, 0, 0, 0))
    consts = [jnp.asarray(mf, BF16), jnp.asarray(mb, BF16), jnp.asarray(lvf), jnp.asarray(lvb)]
    in_specs = [fwd(0), bwd(0), fwd(1), bwd(2), fwd(3), bwd(3),
                pl.BlockSpec(lb_logits.shape, lambda b, c: (0, 0, 0))]
    in_specs += [full(a) for a in consts]
    args = [z3] * 6 + [lb_logits] + consts
    if has_init:
        in_specs += [st_spec, st_spec]
        args += [s0f, s0b]
    o_shape = jax.ShapeDtypeStruct((batch, seq, width), F32)
    s_shape = jax.ShapeDtypeStruct((batch, HG_HEADS, HG_DK, HG_DV), F32)
    o_f, o_b, s_f, s_b = pl.pallas_call(
        functools.partial(_hgrn_kernel, has_init=has_init),
        grid=(batch, nc),
        in_specs=in_specs,
        out_specs=[pl.BlockSpec(blk, lambda b, c: (b, c, 0)),
                   pl.BlockSpec(blk, lambda b, c: (b, nc - 1 - c, 0)),
                   st_spec, st_spec],
        out_shape=[o_shape, o_shape, s_shape, s_shape],
        scratch_shapes=[pltpu.VMEM((HG_HEADS, HG_DV, HG_DK), F32), pltpu.VMEM((HG_HEADS, HG_DV, HG_DK), F32)],
        compiler_params=_cparams(("parallel", "arbitrary")),
        name="hgrn_bidir",
    )(*args)
    t = batch * seq
    return o_f.reshape(t, -1), o_b.reshape(t, -1), s_f, s_b


def _axial_tables(n_tokens, n_rot):
    t = jnp.arange(n_tokens)
    row = (t // GRID_W).astype(F32)
    col = (t % GRID_W).astype(F32)
    n_freq = n_rot // 4
    inv = jnp.power(ROPE_THETA, -jnp.arange(n_freq, dtype=F32) / n_freq)
    ang = jnp.concatenate([row[:, None] * inv, col[:, None] * inv], axis=-1)
    return jnp.cos(ang), jnp.sin(ang)


def _rope_lane_tables(n_tokens, n_rot, head_width, first_rot_lane):
    cos, sin = _axial_tables(n_tokens, n_rot)
    half = n_rot // 2
    c_head = jnp.ones((n_tokens, head_width), F32)
    sa_head = jnp.zeros((n_tokens, head_width), F32)
    sb_head = jnp.zeros((n_tokens, head_width), F32)
    a0, a1, a2 = first_rot_lane, first_rot_lane + half, first_rot_lane + n_rot
    c_head = c_head.at[:, a0:a1].set(cos).at[:, a1:a2].set(cos)
    sa_head = sa_head.at[:, a0:a1].set(-sin)
    sb_head = sb_head.at[:, a1:a2].set(sin)
    reps = LANES // head_width
    tile = lambda a: jnp.tile(a, (1, reps))
    return tile(c_head), tile(sa_head), tile(sb_head)


def _rope(x, c, sa, sb, half):
    return x * c + pltpu.roll(x, LANES - half, 1) * sa + pltpu.roll(x, half, 1) * sb


def _mla_q_kernel(*refs, rope):
    if rope:
        qa_ref, qan_ref, qb_ref, qn_ref, c_ref, sa_ref, sb_ref, o_ref = refs
    else:
        qa_ref, qan_ref, qb_ref, qn_ref, o_ref = refs
    qa = qa_ref[...]
    qn = qa * lax.rsqrt(jnp.mean(qa * qa, axis=-1, keepdims=True) + EPS) * qan_ref[...]
    qfull = _dot(qn.astype(BF16), qb_ref[...])
    outs = []
    for h in range(MLA_HEADS):
        qh = qfull[:, h * LANES:(h + 1) * LANES]
        ms = jnp.sum(qh * qh, axis=-1, keepdims=True) * (1.0 / MLA_QK)
        qh = qh * lax.rsqrt(ms + EPS) * qn_ref[...]
        if rope:
            qh = _rope(qh, c_ref[...], sa_ref[...], sb_ref[...], MLA_ROPE // 2)
        outs.append((qh * MLA_SCALE).astype(BF16))
    o_ref[...] = jnp.concatenate(outs, axis=1)


def mla_queries(z, qa_norm, q_b_pad, q_norm_pad, tables, tokens_per_batch, tm=256):
    t = z.shape[0]
    rope = tables is not None
    row = lambda i: (0, 0)
    in_specs = [pl.BlockSpec((tm, MLA_Q_LORA), lambda i: (i, 2560 // MLA_Q_LORA)),
                pl.BlockSpec((1, MLA_Q_LORA), row),
                pl.BlockSpec(q_b_pad.shape, row),
                pl.BlockSpec((1, LANES), row)]
    args = [z, qa_norm.reshape(1, -1), q_b_pad, q_norm_pad]
    if rope:
        tpb = tokens_per_batch // tm
        in_specs += [pl.BlockSpec((tm, LANES), lambda i: (i % tpb, 0))] * 3
        args += list(tables)
    return pl.pallas_call(
        functools.partial(_mla_q_kernel, rope=rope),
        grid=(t // tm,),
        in_specs=in_specs,
        out_specs=pl.BlockSpec((tm, MLA_HEADS * LANES), lambda i: (i, 0)),
        out_shape=jax.ShapeDtypeStruct((t, MLA_HEADS * LANES), BF16),
        compiler_params=_cparams(("parallel",)),
        name="mla_queries",
    )(*args)


def _mla_kv_kernel(*refs, norm_input, rope):
    if rope:
        kva_ref, kpe_ref, kvan_ref, wk_ref, wv_ref, kn_ref, c_ref, sa_ref, sb_ref, ckv_ref, k_ref, v_ref = refs
    else:
        kva_ref, kpe_ref, kvan_ref, wk_ref, wv_ref, kn_ref, ckv_ref, k_ref, v_ref = refs
    ckv = kva_ref[...]
    if norm_input:
        ckv = ckv * lax.rsqrt(jnp.mean(ckv * ckv, axis=-1, keepdims=True) + EPS) * kvan_ref[...]
    ckv_ref[...] = ckv
    cb = ckv.astype(BF16)
    knope = _dot(cb, wk_ref[...])
    v_ref[...] = _dot(cb, wv_ref[...]).astype(BF16)
    kpe = pltpu.roll(kpe_ref[...], MLA_NOPE, 1)
    outs = []
    for h in range(MLA_HEADS):
        kh = knope[:, h * LANES:(h + 1) * LANES] + kpe
        ms = jnp.sum(kh * kh, axis=-1, keepdims=True) * (1.0 / MLA_QK)
        kh = kh * lax.rsqrt(ms + EPS) * kn_ref[...]
        if rope:
            kh = _rope(kh, c_ref[...], sa_ref[...], sb_ref[...], MLA_ROPE // 2)
        outs.append(kh.astype(BF16))
    k_ref[...] = jnp.concatenate(outs, axis=1)


def mla_keys_values(kva_src, kva_col, kpe_src, kpe_col, kva_norm, wk_pad, wv, k_norm_pad, tables,
                    tokens_per_batch, norm_input, tm=256):
    t = kva_src.shape[0]
    rope = tables is not None
    row = lambda i: (0, 0)
    in_specs = [pl.BlockSpec((tm, LANES), lambda i: (i, kva_col)),
                pl.BlockSpec((tm, LANES), lambda i: (i, kpe_col)),
                pl.BlockSpec((1, LANES), row),
                pl.BlockSpec(wk_pad.shape, row),
                pl.BlockSpec(wv.shape, row),
                pl.BlockSpec((1, LANES), row)]
    args = [kva_src, kpe_src, kva_norm.reshape(1, -1), wk_pad, wv, k_norm_pad]
    if rope:
        tpb = tokens_per_batch // tm
        in_specs += [pl.BlockSpec((tm, LANES), lambda i: (i % tpb, 0))] * 3
        args += list(tables)
    return pl.pallas_call(
        functools.partial(_mla_kv_kernel, norm_input=norm_input, rope=rope),
        grid=(t // tm,),
        in_specs=in_specs,
        out_specs=[pl.BlockSpec((tm, LANES), lambda i: (i, 0)),
                   pl.BlockSpec((tm, MLA_HEADS * LANES), lambda i: (i, 0)),
                   pl.BlockSpec((tm, MLA_HEADS * MLA_V), lambda i: (i, 0))],
        out_shape=[jax.ShapeDtypeStruct((t, LANES), F32),
                   jax.ShapeDtypeStruct((t, MLA_HEADS * LANES), BF16),
                   jax.ShapeDtypeStruct((t, MLA_HEADS * MLA_V), BF16)],
        compiler_params=_cparams(("parallel",)),
        name="mla_keys_values",
    )(*args)


def _mla_attn_kernel(q_ref, k_ref, vt_ref, o_ref):
    heads = range(2)
    ss = [_dot_nt(k_ref[0][:, j * LANES:(j + 1) * LANES], q_ref[0][:, j * LANES:(j + 1) * LANES]) for j in heads]
    ms = [jnp.max(s, axis=0, keepdims=True) for s in ss]
    ps = [jnp.exp(s - m) for s, m in zip(ss, ms)]
    ls = [jnp.sum(p, axis=0, keepdims=True) for p in ps]
    ots = [_dot(vt_ref[0][j * MLA_V:(j + 1) * MLA_V, :], ps[j].astype(BF16)) / ls[j] for j in heads]
    o_ref[0] = jnp.concatenate(ots, axis=0).T


def mla_attention(q, k, v, batch, n_q, n_k, tq=256):
    q3 = q.reshape(batch, n_q, -1)
    k3 = k.reshape(batch, n_k, -1)
    vt = jnp.swapaxes(v.reshape(batch, n_k, -1), 1, 2)
    out = pl.pallas_call(
        _mla_attn_kernel,
        grid=(batch, MLA_HEADS // 2, n_q // tq),
        in_specs=[pl.BlockSpec((1, tq, 2 * LANES), lambda b, j, i: (b, i, j)),
                  pl.BlockSpec((1, n_k, 2 * LANES), lambda b, j, i: (b, 0, j)),
                  pl.BlockSpec((1, 2 * MLA_V, n_k), lambda b, j, i: (b, j, 0))],
        out_specs=pl.BlockSpec((1, tq, 2 * MLA_V), lambda b, j, i: (b, i, j)),
        out_shape=jax.ShapeDtypeStruct((batch, n_q, MLA_HEADS * MLA_V), F32),
        compiler_params=_cparams(("parallel", "parallel", "arbitrary")),
        name="mla_attention",
    )(q3, k3, vt)
    return out.reshape(batch * n_q, -1)


def _out0_kernel(of_ref, ob_ref, ag_ref, hgn_ref, om_ref, w_ref, x_ref, g_ref, o_ref):
    o = of_ref[...] + ob_ref[...]
    ag = ag_ref[...]
    parts = []
    for h in range(HG_HEADS):
        oh = o[:, h * HG_DV:(h + 1) * HG_DV]
        oh = oh * lax.rsqrt(jnp.mean(oh * oh, axis=-1, keepdims=True) + EPS) * hgn_ref[...]
        gh = ag[:, h * HG_DV:(h + 1) * HG_DV]
        parts.append((oh * (gh * _sigmoid(gh))).astype(BF16))
    oa = jnp.concatenate(parts, axis=1)
    n_a = HG_HEADS * HG_DV
    mix = _dot(oa, w_ref[0:n_a, :]) + _dot(om_ref[...].astype(BF16), w_ref[n_a:, :])
    o_ref[...] = x_ref[...] + g_ref[0] * mix


def out_proj_layer0(o_f, o_b, z, hg_norm, o_mla, w_out_bf16, x, gate, tokens_per_group, tm=512):
    t, d = x.shape
    n_a = HG_HEADS * HG_DV
    tiles_per_group = tokens_per_group // tm
    tile = lambda w: pl.BlockSpec((tm, w), lambda i: (i, 0))
    return pl.pallas_call(
        _out0_kernel,
        grid=(t // tm,),
        in_specs=[tile(n_a), tile(n_a),
                  pl.BlockSpec((tm, n_a), lambda i: (i, 2048 // n_a)),
                  pl.BlockSpec((1, HG_DV), lambda i: (0, 0)),
                  tile(o_mla.shape[1]),
                  pl.BlockSpec(w_out_bf16.shape, lambda i: (0, 0)),
                  tile(d),
                  pl.BlockSpec((1, 1, d), lambda i: (i // tiles_per_group, 0, 0))],
        out_specs=tile(d),
        out_shape=jax.ShapeDtypeStruct((t, d), F32),
        compiler_params=_cparams(("parallel",)),
        name="out_proj_layer0",
    )(o_f, o_b, z, hg_norm.reshape(1, -1), o_mla, w_out_bf16, x, gate)


def _out1_kernel(a_ref, w_ref, x_ref, g_ref, o_ref):
    o_ref[...] = x_ref[...] + g_ref[0] * _dot(a_ref[...].astype(BF16), w_ref[...])


def out_proj_layer1(a, w_out_bf16, x, gate, tokens_per_group, tm=512):
    t, d = x.shape
    tiles_per_group = tokens_per_group // tm
    tile = lambda w: pl.BlockSpec((tm, w), lambda i: (i, 0))
    return pl.pallas_call(
        _out1_kernel,
        grid=(t // tm,),
        in_specs=[tile(a.shape[1]),
                  pl.BlockSpec(w_out_bf16.shape, lambda i: (0, 0)),
                  tile(d),
                  pl.BlockSpec((1, 1, d), lambda i: (i // tiles_per_group, 0, 0))],
        out_specs=tile(d),
        out_shape=jax.ShapeDtypeStruct((t, d), F32),
        compiler_params=_cparams(("parallel",)),
        name="out_proj_layer1",
    )(a, w_out_bf16, x, gate)


def _head_rms(x, gain2):
    sq = x * x
    lane = lax.broadcasted_iota(jnp.int32, x.shape, 1)
    first = lane < SWA_HD
    lo = jnp.sum(jnp.where(first, sq, 0.0), axis=-1, keepdims=True)
    hi = jnp.sum(jnp.where(first, 0.0, sq), axis=-1, keepdims=True)
    ms = jnp.where(first, lo, hi) * (1.0 / SWA_HD)
    return x * lax.rsqrt(ms + EPS) * gain2


def _swa_prep_kernel(*refs, rope):
    if rope:
        zq_ref, zk_ref, zv_ref, qn_ref, kn_ref, c_ref, sa_ref, sb_ref, q_ref, kc_ref, k_ref, v_ref = refs
    else:
        zq_ref, zk_ref, zv_ref, qn_ref, kn_ref, q_ref, kc_ref, k_ref, v_ref = refs
    half = SWA_HD // 2

    def rot(x):
        return _rope(x, c_ref[...], sa_ref[...], sb_ref[...], half) if rope else x

    zq = zq_ref[...]
    qs = []
    for p in range(zq.shape[1] // LANES):
        x = _head_rms(zq[:, p * LANES:(p + 1) * LANES], qn_ref[...])
        qs.append((rot(x) * SWA_SCALE).astype(BF16))
    q_ref[...] = jnp.concatenate(qs, axis=1)
    zk = zk_ref[...]
    kn, kr = [], []
    for p in range(zk.shape[1] // LANES):
        x = _head_rms(zk[:, p * LANES:(p + 1) * LANES], kn_ref[...])
        kn.append(x)
        kr.append(rot(x).astype(BF16))
    kc_ref[...] = jnp.concatenate(kn, axis=1)
    k_ref[...] = jnp.concatenate(kr, axis=1)
    v_ref[...] = zv_ref[...].astype(BF16)


def swa_prep(z, q_norm2, k_norm2, tables, tokens_per_batch, tm=256):
    t = z.shape[0]
    nq = SWA_HEADS * SWA_HD
    nkv = SWA_KV_HEADS * SWA_HD
    rope = tables is not None
    row = lambda i: (0, 0)
    in_specs = [pl.BlockSpec((tm, nq), lambda i: (i, 0)),
                pl.BlockSpec((tm, nkv), lambda i: (i, nq // nkv)),
                pl.BlockSpec((tm, nkv), lambda i: (i, nq // nkv + 1)),
                pl.BlockSpec((1, LANES), row),
                pl.BlockSpec((1, LANES), row)]
    args = [z, z, z, q_norm2, k_norm2]
    if rope:
        tpb = tokens_per_batch // tm
        in_specs += [pl.BlockSpec((tm, LANES), lambda i: (i % tpb, 0))] * 3
        args += list(tables)
    tile = lambda w: pl.BlockSpec((tm, w), lambda i: (i, 0))
    return pl.pallas_call(
        functools.partial(_swa_prep_kernel, rope=rope),
        grid=(t // tm,),
        in_specs=in_specs,
        out_specs=[tile(nq), tile(nkv), tile(nkv), tile(nkv)],
        out_shape=[jax.ShapeDtypeStruct((t, nq), BF16),
                   jax.ShapeDtypeStruct((t, nkv), F32),
                   jax.ShapeDtypeStruct((t, nkv), BF16),
                   jax.ShapeDtypeStruct((t, nkv), BF16)],
        compiler_params=_cparams(("parallel",)),
        name="swa_prep",
    )(*args)


def _sink_attention_t(q8, k_all, vt_all, bias, sink_ref, pair):
    tq = q8.shape[0]
    n_q = SWA_GROUP * tq
    lane = lax.broadcasted_iota(jnp.int32, (1, n_q), 1)
    kv = range(2)
    heads = [[n * SWA_GROUP + g for g in range(SWA_GROUP)] for n in kv]
    q4s = [jnp.concatenate([q8[:, hq * SWA_HD:(hq + 1) * SWA_HD] for hq in heads[n]], axis=0) for n in kv]
    sks = []
    for n in kv:
        sk = jnp.zeros((1, n_q), F32)
        for g, hq in enumerate(heads[n]):
            sk = jnp.where(lane // tq == g, sink_ref[pair * 2 * SWA_GROUP + hq], sk)
        sks.append(sk)
    ss = [_dot_nt(k_all[:, n * SWA_HD:(n + 1) * SWA_HD], q4s[n]) for n in kv]
    if bias is not None:
        ss = [s + bias for s in ss]
    ms = [jnp.maximum(jnp.max(ss[n], axis=0, keepdims=True), sks[n]) for n in kv]
    ps = [jnp.exp(ss[n] - ms[n]) for n in kv]
    ls = [jnp.sum(ps[n], axis=0, keepdims=True) + jnp.exp(sks[n] - ms[n]) for n in kv]
    ots = [_dot(vt_all[n * SWA_HD:(n + 1) * SWA_HD, :], ps[n].astype(BF16)) / ls[n] for n in kv]
    pairs = []
    for n in kv:
        for g in range(0, SWA_GROUP, 2):
            two = jnp.concatenate([ots[n][:, g * tq:(g + 1) * tq], ots[n][:, (g + 1) * tq:(g + 2) * tq]], axis=0)
            pairs.append(two.T)
    return jnp.concatenate(pairs, axis=1)


def _swa_dense_kernel(sink_ref, q_ref, k_ref, vt_ref, o_ref):
    o_ref[0] = _sink_attention_t(q_ref[0], k_ref[0], vt_ref[0], None, sink_ref, pl.program_id(1))


def swa_dense(q, k, v, sink, batch, seq):
    nq = SWA_HEADS * SWA_HD
    q3 = q.reshape(batch, seq, nq)
    k3 = k.reshape(batch, seq, -1)
    vt = jnp.swapaxes(v.reshape(batch, seq, -1), 1, 2)
    out = pl.pallas_call(
        _swa_dense_kernel,
        grid=(batch, 2),
        in_specs=[pl.BlockSpec(memory_space=pltpu.SMEM),
                  pl.BlockSpec((1, seq, nq // 2), lambda b, j: (b, 0, j)),
                  pl.BlockSpec((1, seq, LANES), lambda b, j: (b, 0, j)),
                  pl.BlockSpec((1, LANES, seq), lambda b, j: (b, j, 0))],
        out_specs=pl.BlockSpec((1, seq, nq // 2), lambda b, j: (b, 0, j)),
        out_shape=jax.ShapeDtypeStruct((batch, seq, nq), F32),
        compiler_params=_cparams(("parallel", "parallel")),
        name="swa_dense",
    )(sink, q3, k3, vt)
    return out.reshape(batch * seq, nq)


def _swa_window_kernel(sink_ref, q_ref, kc_ref, kp_ref, k0_ref, kn_ref, vc_ref, vp_ref, v0_ref, vn_ref, o_ref):
    i = pl.program_id(2)
    nb = pl.num_programs(2)
    w = SWA_WINDOW
    n_ctx = kc_ref.shape[1]
    k_all = jnp.concatenate([kc_ref[0], kp_ref[0], k0_ref[0], kn_ref[0]], axis=0)
    vt_all = jnp.concatenate([vc_ref[0], vp_ref[0], v0_ref[0], vn_ref[0]], axis=1)
    shape = (n_ctx + 3 * w, SWA_GROUP * w)
    key = lax.broadcasted_iota(jnp.int32, shape, 0) - n_ctx
    r = lax.broadcasted_iota(jnp.int32, shape, 1) % w
    prev_bias = jnp.where(i > 0, 0.0, NEG_BIG)
    next_bias = jnp.where(i < nb - 1, 0.0, NEG_BIG)
    bias = jnp.where(
        key < w,
        jnp.where(key < 0, 0.0, jnp.where(key >= r, prev_bias, NEG_BIG)),
        jnp.where(key < 2 * w, 0.0, jnp.where(key - 2 * w <= r, next_bias, NEG_BIG)))
    o_ref[0] = _sink_attention_t(q_ref[0], k_all, vt_all, bias, sink_ref, pl.program_id(1))


def swa_window(q, k, v, k_ctx, v_ctx, sink, batch, seq, n_ctx):
    nq = SWA_HEADS * SWA_HD
    w = SWA_WINDOW
    nb = seq // w
    q3 = q.reshape(batch, seq, nq)
    k3 = k.reshape(batch, seq, -1)
    vt = jnp.swapaxes(v.reshape(batch, seq, -1), 1, 2)
    vt_ctx = jnp.swapaxes(v_ctx, 1, 2)
    ctx = pl.BlockSpec((1, n_ctx, LANES), lambda b, j, i: (b, 0, j))
    prv = pl.BlockSpec((1, w, LANES), lambda b, j, i: (b, jnp.maximum(i - 1, 0), j))
    cur = pl.BlockSpec((1, w, LANES), lambda b, j, i: (b, i, j))
    nxt = pl.BlockSpec((1, w, LANES), lambda b, j, i: (b, jnp.minimum(i + 1, nb - 1), j))
    ctx_t = pl.BlockSpec((1, LANES, n_ctx), lambda b, j, i: (b, j, 0))
    prv_t = pl.BlockSpec((1, LANES, w), lambda b, j, i: (b, j, jnp.maximum(i - 1, 0)))
    cur_t = pl.BlockSpec((1, LANES, w), lambda b, j, i: (b, j, i))
    nxt_t = pl.BlockSpec((1, LANES, w), lambda b, j, i: (b, j, jnp.minimum(i + 1, nb - 1)))
    out = pl.pallas_call(
        _swa_window_kernel,
        grid=(batch, 2, nb),
        in_specs=[pl.BlockSpec(memory_space=pltpu.SMEM),
                  pl.BlockSpec((1, w, nq // 2), lambda b, j, i: (b, i, j)),
                  ctx, prv, cur, nxt, ctx_t, prv_t, cur_t, nxt_t],
        out_specs=pl.BlockSpec((1, w, nq // 2), lambda b, j, i: (b, i, j)),
        out_shape=jax.ShapeDtypeStruct((batch, seq, nq), F32),
        compiler_params=_cparams(("parallel", "parallel", "arbitrary")),
        name="swa_window",
    )(sink, q3, k_ctx, k3, k3, k3, vt_ctx, vt, vt, vt)
    return out.reshape(batch * seq, nq)


def _router_kernel(x_ref, g_ref, sh_ref, sc_ref, rw_ref, rb_ref, tri_ref, h_ref, idx_ref, gate_ref, rank_ref,
                   cnt_ref, cnt, *, tiles_per_part):
    @pl.when(pl.program_id(0) % tiles_per_part == 0)
    def _():
        cnt[...] = jnp.zeros_like(cnt)

    h = _modnorm(x_ref[...], g_ref[...], sh_ref[0], sc_ref[0])
    bits = lax.bitcast_convert_type(h.astype(BF16).astype(F32), jnp.uint32)
    half = h.shape[1] // 2
    h_ref[...] = (bits[:, :half] >> 16) | (bits[:, half:] & jnp.uint32(0xFFFF0000))
    wh, wl = _split2(rw_ref[...])
    hh, hl = _split2(h)
    logits = _dot_nt(wh, hh) + _dot_nt(wh, hl) + _dot_nt(wl, hh) + rb_ref[...]
    row = lax.broadcasted_iota(jnp.int32, logits.shape, 0)
    work = logits
    vals, idxs = [], []
    for _ in range(TOP_K):
        m = jnp.max(work, axis=0, keepdims=True)
        ik = jnp.min(jnp.where(work == m, row, LANES), axis=0, keepdims=True)
        vals.append(m)
        idxs.append(ik)
        work = jnp.where(row == ik, 2.0 * NEG_BIG, work)
    es = [jnp.exp(v - vals[0]) for v in vals]
    denom = es[0] + es[1] + es[2] + es[3]
    chosen = jnp.zeros(logits.shape, F32)
    for k in range(TOP_K):
        chosen = jnp.where(row == idxs[k], 1.0, chosen)
    chosen_b = chosen.astype(BF16)
    seen = cnt[...]
    before = _dot(chosen_b, tri_ref[...]) + jnp.concatenate([seen] * (logits.shape[1] // LANES), axis=1)
    out_row = lax.broadcasted_iota(jnp.int32, idx_ref.shape, 0)
    idx_out = jnp.zeros(idx_ref.shape, jnp.int32)
    gate_out = jnp.zeros(idx_ref.shape, F32)
    rank_out = jnp.zeros(idx_ref.shape, jnp.int32)
    for k in range(TOP_K):
        rk = jnp.sum(jnp.where(row == idxs[k], before, 0.0), axis=0, keepdims=True)
        idx_out = jnp.where(out_row == k, idxs[k], idx_out)
        gate_out = jnp.where(out_row == k, es[k] / denom, gate_out)
        rank_out = jnp.where(out_row == k, rk.astype(jnp.int32), rank_out)
    idx_ref[...] = idx_out
    gate_ref[...] = gate_out
    rank_ref[...] = rank_out
    cnt[...] = seen + _dot(chosen_b, jnp.ones((logits.shape[1], LANES), BF16))
    cnt_ref[0] = cnt[...]


def moe_router(x, gain, shift, scale, rw_t, rb_col, tokens_per_group, tm=256):
    t, d = x.shape
    tiles_per_group = tokens_per_group // tm
    tiles_per_part = MOE_PART // tm
    grp = lambda i: (i // tiles_per_group, 0, 0)
    tile = lambda w: pl.BlockSpec((tm, w), lambda i: (i, 0))
    slab = pl.BlockSpec((8, tm), lambda i: (0, i))
    tri = jnp.asarray(np.triu(np.ones((tm, tm), np.float32), 1), BF16)
    hp, idx, gates, rank, cnt = pl.pallas_call(
        functools.partial(_router_kernel, tiles_per_part=tiles_per_part),
        grid=(t // tm,),
        in_specs=[tile(d),
                  pl.BlockSpec((1, d), lambda i: (0, 0)),
                  pl.BlockSpec((1, 1, d), grp),
                  pl.BlockSpec((1, 1, d), grp),
                  pl.BlockSpec((LANES, d), lambda i: (0, 0)),
                  pl.BlockSpec((LANES, 1), lambda i: (0, 0)),
                  pl.BlockSpec((tm, tm), lambda i: (0, 0))],
        out_specs=[tile(d // 2), slab, slab, slab,
                   pl.BlockSpec((1, LANES, LANES), lambda i: (i // tiles_per_part, 0, 0))],
        out_shape=[jax.ShapeDtypeStruct((t, d // 2), jnp.uint32),
                   jax.ShapeDtypeStruct((8, t), jnp.int32),
                   jax.ShapeDtypeStruct((8, t), F32),
                   jax.ShapeDtypeStruct((8, t), jnp.int32),
                   jax.ShapeDtypeStruct((t // MOE_PART, LANES, LANES), F32)],
        scratch_shapes=[pltpu.VMEM((LANES, LANES), F32)],
        compiler_params=_cparams(("arbitrary",)),
        name="moe_router",
    )(x, gain.reshape(1, d), shift, scale, rw_t, rb_col, tri)
    per_token = lambda a: a[:TOP_K].T
    return hp, per_token(idx), per_token(gates), per_token(rank), cnt[:, :N_EXPERTS, 0].astype(jnp.int32)


FFN_COLS = 512
FFN_PIECES = 2 * D_FF // FFN_COLS + D_MODEL // FFN_COLS
MOE_PART = 4096
ADD_BATCH = 8


def _unpack_rows(words):
    lo = lax.bitcast_convert_type(words << 16, F32)
    hi = lax.bitcast_convert_type(words & jnp.uint32(0xFFFF0000), F32)
    return jnp.concatenate([lo, hi], axis=1).astype(BF16)


def _ffn_block(x_ref, y_ref, wgu_ref, bgu_ref, wd_ref, bd_ref, between):
    xb = _unpack_rows(x_ref[...])
    acts = []
    for c in range(D_FF // FFN_COLS):
        lo, hi = c * FFN_COLS, (c + 1) * FFN_COLS
        between(2 * c)
        gate = _dot(xb, wgu_ref[0, :, lo:hi]) + bgu_ref[0][:, lo:hi]
        between(2 * c + 1)
        up = _dot(xb, wgu_ref[0, :, D_FF + lo:D_FF + hi]) + bgu_ref[0][:, D_FF + lo:D_FF + hi]
        gate = jnp.minimum(gate, SWIGLU_LIMIT)
        up = jnp.clip(up, -SWIGLU_LIMIT, SWIGLU_LIMIT)
        acts.append((gate * _sigmoid(SWIGLU_ALPHA * gate) * (up + 1.0)).astype(BF16))
    act = jnp.concatenate(acts, axis=1)
    for n in range(D_MODEL // FFN_COLS):
        lo, hi = n * FFN_COLS, (n + 1) * FFN_COLS
        between(2 * D_FF // FFN_COLS + n)
        y_ref[:, lo:hi] = _dot(act, wd_ref[0, :, lo:hi]) + bd_ref[0][:, lo:hi]


def _expert_kernel(nb_ref, bs_ref, loc_ref, gate_ref, hp_hbm, wgu_ref, bgu_ref, wd_ref, bd_ref, out_hbm,
                   hbuf, acc, xbuf, ybuf, sem):
    part = pl.program_id(0)
    e = pl.program_id(1)
    idx = part * N_EXPERTS + e
    nb = nb_ref[idx]
    b0 = bs_ref[idx]
    shares = np.array_split(np.arange(MOE_ROWS), FFN_PIECES)

    def fetch_rows(blk):
        base = blk * MOE_ROWS
        dst = xbuf.at[blk % 2]

        def emit(k):
            for r in shares[k]:
                r = int(r)
                dst[r:r + 1, :] = hbuf[pl.ds(loc_ref[base + r], 1), :]
        return emit

    def add_rows(blk):
        base = blk * MOE_ROWS
        src = ybuf.at[blk % 2]

        def emit(k):
            for batch in np.array_split(shares[k], max(1, len(shares[k]) // ADD_BATCH)):
                rows = [loc_ref[base + int(r)] for r in batch]
                new = [acc[pl.ds(row, 1), :] + gate_ref[base + int(r)] * src[int(r):int(r) + 1, :]
                       for row, r in zip(rows, batch)]
                for row, val in zip(rows, new):
                    acc[pl.ds(row, 1), :] = val
        return emit

    def emit_all(f):
        for k in range(FFN_PIECES):
            f(k)

    @pl.when(e == 0)
    def _():
        load = pltpu.make_async_copy(hp_hbm.at[pl.ds(part * MOE_PART, MOE_PART), :],
                                     hbuf.at[pl.ds(0, MOE_PART), :], sem.at[0])
        load.start()
        hbuf[MOE_PART:, :] = jnp.zeros((hbuf.shape[0] - MOE_PART, hbuf.shape[1]), hbuf.dtype)
        acc[...] = jnp.zeros_like(acc)
        ybuf[...] = jnp.zeros_like(ybuf)
        load.wait()
        emit_all(fetch_rows(b0))

    def block(i, carry):
        fetch, add = fetch_rows(i + 1), add_rows(i - 1)

        def between(k):
            fetch(k)
            add(k)
        _ffn_block(xbuf.at[i % 2], ybuf.at[i % 2], wgu_ref, bgu_ref, wd_ref, bd_ref, between)
        return carry

    lax.fori_loop(b0, b0 + nb, block, 0)

    @pl.when(e == N_EXPERTS - 1)
    def _():
        emit_all(add_rows(b0 + nb - 1))
        store = pltpu.make_async_copy(acc.at[pl.ds(0, MOE_PART), :],
                                      out_hbm.at[pl.ds(part * MOE_PART, MOE_PART), :], sem.at[0])
        store.start()
        store.wait()


def moe_experts(hp, n_blk, blk_start, row_loc, row_gate, w_gu, b_gu, w_down, b_down):
    t = hp.shape[0]
    d = D_MODEL
    per_expert = lambda p, e, nb, bs, sl: (e, 0, 0)
    grid_spec = pltpu.PrefetchScalarGridSpec(
        num_scalar_prefetch=3,
        grid=(t // MOE_PART, N_EXPERTS),
        in_specs=[pl.BlockSpec(memory_space=pltpu.SMEM),
                  pl.BlockSpec(memory_space=pl.ANY),
                  pl.BlockSpec((1, d, 2 * D_FF), per_expert),
                  pl.BlockSpec((1, 1, 2 * D_FF), per_expert),
                  pl.BlockSpec((1, D_FF, d), per_expert),
                  pl.BlockSpec((1, 1, d), per_expert)],
        out_specs=pl.BlockSpec(memory_space=pl.ANY),
        scratch_shapes=[pltpu.VMEM((MOE_PART + 8, d // 2), jnp.uint32),
                        pltpu.VMEM((MOE_PART + 8, d), F32),
                        pltpu.VMEM((2, MOE_ROWS, d // 2), jnp.uint32),
                        pltpu.VMEM((2, MOE_ROWS, d), F32),
                        pltpu.SemaphoreType.DMA((1,))],
    )
    return pl.pallas_call(
        _expert_kernel,
        grid_spec=grid_spec,
        out_shape=jax.ShapeDtypeStruct((t, d), F32),
        compiler_params=pltpu.CompilerParams(dimension_semantics=("arbitrary", "arbitrary"),
                                             vmem_limit_bytes=EXPERT_VMEM_LIMIT),
        name="moe_experts",
    )(n_blk, blk_start, row_loc, row_gate, hp, w_gu, b_gu.reshape(N_EXPERTS, 1, -1), w_down,
      b_down.reshape(N_EXPERTS, 1, -1))


def _residual_kernel(y_ref, x_ref, g_ref, o_ref):
    o_ref[...] = x_ref[...] + g_ref[0] * y_ref[...]


def gated_residual(y, x, gate_mod, tokens_per_group, tm=512):
    t, d = x.shape
    tiles_per_group = tokens_per_group // tm
    tile = pl.BlockSpec((tm, d), lambda i: (i, 0))
    return pl.pallas_call(
        _residual_kernel,
        grid=(t // tm,),
        in_specs=[tile, tile, pl.BlockSpec((1, 1, d), lambda i: (i // tiles_per_group, 0, 0))],
        out_specs=tile,
        out_shape=jax.ShapeDtypeStruct((t, d), F32),
        compiler_params=_cparams(("parallel",)),
        name="moe_residual",
    )(y, x, gate_mod)


def _assignment_tables(top_idx, rank, gates, counts):
    t = top_idx.shape[0]
    n_parts = t // MOE_PART
    n_assign = t * TOP_K
    part_blocks = MOE_PART * TOP_K // MOE_ROWS + N_EXPERTS + 1
    n_rows = (1 + n_parts * part_blocks) * MOE_ROWS
    padded = (counts + MOE_ROWS - 1) // MOE_ROWS * MOE_ROWS
    part_first = (1 + jnp.arange(n_parts, dtype=jnp.int32) * part_blocks) * MOE_ROWS
    row_start = part_first[:, None] + jnp.cumsum(padded, axis=1) - padded
    experts = jnp.arange(N_EXPERTS, dtype=jnp.int32)
    by_part = top_idx.reshape(n_parts, MOE_PART, TOP_K)
    start_of = jnp.sum(jnp.where(by_part[..., None] == experts, row_start[:, None, None, :], 0), axis=-1)
    dest = (start_of.reshape(t, TOP_K) + rank).reshape(-1).astype(jnp.int32)
    local = (jnp.arange(n_assign, dtype=jnp.int32) // TOP_K) % MOE_PART
    vals = jnp.stack([local, lax.bitcast_convert_type(gates.reshape(-1), jnp.int32)], axis=1)
    init = jnp.broadcast_to(jnp.array([MOE_PART, 0], jnp.int32), (n_rows, 2))
    table = init.at[dest].set(vals)
    row_loc = table[:, 0]
    row_gate = lax.bitcast_convert_type(table[:, 1], F32)
    return row_loc, row_gate, (padded // MOE_ROWS).reshape(-1).astype(jnp.int32), \
        (row_start // MOE_ROWS).reshape(-1).astype(jnp.int32)


def moe_layer(x, p, shift, scale, gate_mod, tokens_per_group):
    hp, idx, gates, rank, counts = moe_router(x, p['norm2'], shift, scale, p['rw_t'], p['rb_col'], tokens_per_group)
    row_loc, row_gate, n_blk, blk_start = _assignment_tables(idx, rank, gates, counts)
    y = moe_experts(hp, n_blk, blk_start, row_loc, row_gate, p['w_gu_bf'], p['b_gu'], p['w_down_bf'], p['b_down'])
    return gated_residual(y, x, gate_mod, tokens_per_group)


def _pad_lanes(a, width):
    return jnp.pad(a, [(0, 0)] * (a.ndim - 1) + [(0, width - a.shape[-1])])


def _prep_common(p):
    p['rw_t'] = _pad_lanes(p['router_w'], LANES).T
    p['rb_col'] = jnp.concatenate(
        [p['router_b'].astype(F32), jnp.full((LANES - N_EXPERTS,), NEG_BIG, F32)]).reshape(LANES, 1)
    p['w_out'] = p['w_out'].astype(BF16)
    p['w_gu_bf'] = p['w_gu'].astype(BF16)
    p['w_down_bf'] = p['w_down'].astype(BF16)
    return p


def _prep_layer0(p):
    p = _prep_common(dict(p))
    p['w_in'] = _pad_lanes(p['w_in'], AB_IN_PAD).astype(BF16)
    q_b = p['q_b'].reshape(MLA_Q_LORA, MLA_HEADS, MLA_QK)
    p['q_b_pad'] = _pad_lanes(q_b, LANES).reshape(MLA_Q_LORA, MLA_HEADS * LANES).astype(BF16)
    kv_b = p['kv_b'].reshape(MLA_KV_LORA, MLA_HEADS, MLA_NOPE + MLA_V)
    p['wk_pad'] = _pad_lanes(kv_b[:, :, :MLA_NOPE], LANES).reshape(MLA_KV_LORA, MLA_HEADS * LANES).astype(BF16)
    p['wv'] = kv_b[:, :, MLA_NOPE:].reshape(MLA_KV_LORA, MLA_HEADS * MLA_V).astype(BF16)
    p['q_norm_pad'] = _pad_lanes(p['q_norm'].reshape(1, -1), LANES)
    p['k_norm_pad'] = _pad_lanes(p['k_norm'].reshape(1, -1), LANES)
    return p


def _prep_layer1(p):
    p = _prep_common(dict(p))
    p['w_in'] = p['w_in'].astype(BF16)
    p['q_norm2'] = jnp.tile(p['q_norm'].reshape(1, -1), (1, LANES // SWA_HD))
    p['k_norm2'] = jnp.tile(p['k_norm'].reshape(1, -1), (1, LANES // SWA_HD))
    return p


def _group_forward(x3, mods, p0, p1, lb_logits, caches, latent):
    batch, seq, d = x3.shape
    t = batch * seq
    x = x3.reshape(t, d)
    tpg = seq if latent else t
    sh1, sc1, g1, sh2, sc2, g2 = mods[0]

    z = modnorm_matmul(x, p0['norm1'], sh1, sc1, p0['w_in'], tpg)
    tab_b = _rope_lane_tables(seq, MLA_ROPE, LANES, MLA_NOPE) if latent else None
    s0f, s0b = (caches['hg_f'], caches['hg_b']) if latent else (None, None)
    o_f, o_b, s_f, s_b = hgrn_bidir(z, lb_logits, s0f, s0b, batch, seq)
    q = mla_queries(z, p0['qa_norm'], p0['q_b_pad'], p0['q_norm_pad'], tab_b, seq)
    ckv, k, v = mla_keys_values(z, 2816 // LANES, z, 2944 // LANES, p0['kva_norm'], p0['wk_pad'], p0['wv'],
                                p0['k_norm_pad'], tab_b, seq, norm_input=True)
    n_k = seq
    if latent:
        n_ctx = caches['ckv'].shape[1]
        ckv_c = caches['ckv'].reshape(batch * n_ctx, MLA_KV_LORA)
        kpe_c = _pad_lanes(caches['kpe'].reshape(batch * n_ctx, MLA_ROPE), LANES)
        _, k_c, v_c = mla_keys_values(ckv_c, 0, kpe_c, 0, p0['kva_norm'], p0['wk_pad'], p0['wv'],
                                      p0['k_norm_pad'], None, n_ctx, norm_input=False)
        cat = lambda a, b: jnp.concatenate([a.reshape(batch, n_ctx, -1), b.reshape(batch, seq, -1)],
                                           axis=1).reshape(batch * (n_ctx + seq), -1)
        k, v = cat(k_c, k), cat(v_c, v)
        n_k = n_ctx + seq
    o_mla = mla_attention(q, k, v, batch, seq, n_k)
    x = out_proj_layer0(o_f, o_b, z, p0['hg_out_norm'], o_mla, p0['w_out'], x, g1, tpg)
    x = moe_layer(x, p0, sh2, sc2, g2, tpg)
    state0 = (s_f, s_b, ckv.reshape(batch, seq, MLA_KV_LORA), z[:, 2944:2944 + MLA_ROPE].reshape(batch, seq, MLA_ROPE))

    sh1, sc1, g1, sh2, sc2, g2 = mods[1]
    z = modnorm_matmul(x, p1['norm1'], sh1, sc1, p1['w_in'], tpg)
    tab_c = _rope_lane_tables(seq, SWA_HD, SWA_HD, 0) if latent else None
    q, k_cache, k, v = swa_prep(z, p1['q_norm2'], p1['k_norm2'], tab_c, seq)
    sink = p1['sink'].astype(F32)
    if latent:
        n_ctx = caches['k1'].shape[1]
        k_c = caches['k1'].reshape(batch, n_ctx, -1).astype(BF16)
        v_c = caches['v1'].reshape(batch, n_ctx, -1).astype(BF16)
        a = swa_window(q, k, v, k_c, v_c, sink, batch, seq, n_ctx)
    else:
        a = swa_dense(q, k, v, sink, batch, seq)
    x = out_proj_layer1(a, p1['w_out'], x, g1, tpg)
    x = moe_layer(x, p1, sh2, sc2, g2, tpg)
    nkv = SWA_KV_HEADS * SWA_HD
    state1 = (k_cache.reshape(batch, seq, SWA_KV_HEADS, SWA_HD),
              z[:, SWA_HEADS * SWA_HD + nkv:].reshape(batch, seq, SWA_KV_HEADS, SWA_HD))
    return x.reshape(batch, seq, d), state0, state1


def kernel(x_prompt, x_sample, state_l0_hgrn_fwd, state_l0_hgrn_bwd, cache_l0_mla_ckv, cache_l0_mla_kpe, cache_l1_k, cache_l1_v, c, c_ctx, hgrn_lb_logits, l0_ada_w, l0_ada_b, l0_norm1, l0_norm2, l0_w_in, l0_hg_out_norm, l0_qa_norm, l0_q_b, l0_kva_norm, l0_kv_b, l0_q_norm, l0_k_norm, l0_w_out, l0_router_w, l0_router_b, l0_w_gu, l0_b_gu, l0_w_down, l0_b_down, l1_ada_w, l1_ada_b, l1_norm1, l1_norm2, l1_w_in, l1_q_norm, l1_k_norm, l1_sink, l1_w_out, l1_router_w, l1_router_b, l1_w_gu, l1_b_gu, l1_w_down, l1_b_down):
    p0 = _prep_layer0(dict(norm1=l0_norm1, norm2=l0_norm2, w_in=l0_w_in, hg_out_norm=l0_hg_out_norm,
                           qa_norm=l0_qa_norm, q_b=l0_q_b, kva_norm=l0_kva_norm, kv_b=l0_kv_b,
                           q_norm=l0_q_norm, k_norm=l0_k_norm, w_out=l0_w_out, router_w=l0_router_w,
                           router_b=l0_router_b, w_gu=l0_w_gu, b_gu=l0_b_gu, w_down=l0_w_down,
                           b_down=l0_b_down))
    p1 = _prep_layer1(dict(norm1=l1_norm1, norm2=l1_norm2, w_in=l1_w_in, q_norm=l1_q_norm, k_norm=l1_k_norm,
                           sink=l1_sink, w_out=l1_w_out, router_w=l1_router_w, router_b=l1_router_b,
                           w_gu=l1_w_gu, b_gu=l1_b_gu, w_down=l1_w_down, b_down=l1_b_down))
    dec_batch = c.shape[0]
    d = c.shape[1]
    cond8 = jnp.concatenate([c_ctx[None, :], c, jnp.zeros((8 - 1 - dec_batch, d), F32)], axis=0)
    mods_ctx, mods_lat = [], []
    for w, b in ((l0_ada_w, l0_ada_b), (l1_ada_w, l1_ada_b)):
        mod = ada_params(cond8, w, b)
        mods_ctx.append([m.reshape(1, 1, d) for m in jnp.split(mod[0:1], 6, axis=-1)])
        mods_lat.append([m.reshape(dec_batch, 1, d) for m in jnp.split(mod[1:1 + dec_batch], 6, axis=-1)])

    y_prompt, st0, st1 = _group_forward(x_prompt, mods_ctx, p0, p1, hgrn_lb_logits, None, latent=False)
    caches = dict(hg_f=state_l0_hgrn_fwd, hg_b=state_l0_hgrn_bwd, ckv=cache_l0_mla_ckv, kpe=cache_l0_mla_kpe,
                  k1=cache_l1_k, v1=cache_l1_v)
    y_sample, _, _ = _group_forward(x_sample, mods_lat, p0, p1, hgrn_lb_logits, caches, latent=True)
    return (y_prompt, y_sample, st0[0], st0[1], st0[2], st0[3], st1[0], st1[1])
```

```python
import functools

import numpy as np
import jax
import jax.numpy as jnp
from jax import lax
from jax.experimental import pallas as pl
from jax.experimental.pallas import tpu as pltpu

F32 = jnp.float32
BF16 = jnp.bfloat16

D_MODEL = 1024
GRID_W = 64
ROPE_THETA = 10000.0
EPS = 1e-6
HG_HEADS = 4
HG_DK = 128
HG_DV = 128
MLA_HEADS = 8
MLA_NOPE = 64
MLA_ROPE = 32
MLA_V = 64
MLA_QK = MLA_NOPE + MLA_ROPE
MLA_Q_LORA = 256
MLA_KV_LORA = 128
MLA_SCALE = MLA_QK ** -0.5
SWA_HEADS = 16
SWA_KV_HEADS = 4
SWA_HD = 64
SWA_WINDOW = 128
SWA_SCALE = SWA_HD ** -0.5
SWA_GROUP = SWA_HEADS // SWA_KV_HEADS
N_EXPERTS = 32
TOP_K = 4
D_FF = 1024
SWIGLU_LIMIT = 7.0
SWIGLU_ALPHA = 1.702

LANES = 128
HG_CHUNK = 128
HG_LEVELS = 7
HG_MXU_LEVELS = 3
AB_IN_PAD = 3072
MOE_ROWS = 256
NEG_BIG = -1e30
VMEM_LIMIT = 48 * 1024 * 1024
EXPERT_VMEM_LIMIT = 56 * 1024 * 1024


def _cparams(sem):
    return pltpu.CompilerParams(dimension_semantics=sem, vmem_limit_bytes=VMEM_LIMIT)


def _dot(a, b):
    return jnp.dot(a, b, preferred_element_type=F32)


def _dot_nt(a, b):
    return lax.dot_general(a, b, (((1,), (1,)), ((), ())), preferred_element_type=F32)


def _dot_tn(a, b):
    return lax.dot_general(a, b, (((0,), (0,)), ((), ())), preferred_element_type=F32)


def _split2(x):
    hi = x.astype(BF16)
    lo = (x - hi.astype(F32)).astype(BF16)
    return hi, lo


def _dot_hp(a, b):
    ah, al = _split2(a)
    bh, bl = _split2(b)
    return _dot(ah, bh) + _dot(ah, bl) + _dot(al, bh)


def _sigmoid(x):
    return 1.0 / (1.0 + jnp.exp(-x))


def _modnorm(x, gain, shift, scale):
    y = x * lax.rsqrt(jnp.mean(x * x, axis=-1, keepdims=True) + EPS)
    return y * gain * (1.0 + scale) + shift


def _ada_kernel(c_ref, w_ref, b_ref, o_ref):
    c = c_ref[...]
    o_ref[...] = _dot_hp(c * _sigmoid(c), w_ref[...]) + b_ref[...]


def ada_params(cond8, w, b):
    n = w.shape[1]
    tn = 1024
    return pl.pallas_call(
        _ada_kernel,
        grid=(n // tn,),
        in_specs=[pl.BlockSpec((8, D_MODEL), lambda j: (0, 0)),
                  pl.BlockSpec((D_MODEL, tn), lambda j: (0, j)),
                  pl.BlockSpec((1, tn), lambda j: (0, j))],
        out_specs=pl.BlockSpec((8, tn), lambda j: (0, j)),
        out_shape=jax.ShapeDtypeStruct((8, n), F32),
        compiler_params=_cparams(("parallel",)),
        name="ada_params",
    )(cond8, w, b.reshape(1, n))


def _modnorm_matmul_kernel(x_ref, g_ref, sh_ref, sc_ref, w_ref, o_ref):
    h = _modnorm(x_ref[...], g_ref[...], sh_ref[0], sc_ref[0])
    o_ref[...] = _dot(h.astype(BF16), w_ref[...])


def modnorm_matmul(x, gain, shift, scale, w_bf16, tokens_per_group, tm=512):
    t, d = x.shape
    n = w_bf16.shape[1]
    tiles_per_group = tokens_per_group // tm
    grp = lambda i: (i // tiles_per_group, 0, 0)
    return pl.pallas_call(
        _modnorm_matmul_kernel,
        grid=(t // tm,),
        in_specs=[pl.BlockSpec((tm, d), lambda i: (i, 0)),
                  pl.BlockSpec((1, d), lambda i: (0, 0)),
                  pl.BlockSpec((1, 1, d), grp),
                  pl.BlockSpec((1, 1, d), grp),
                  pl.BlockSpec((d, n), lambda i: (0, 0))],
        out_specs=pl.BlockSpec((tm, n), lambda i: (i, 0)),
        out_shape=jax.ShapeDtypeStruct((t, n), F32),
        compiler_params=_cparams(("parallel",)),
        name="modnorm_matmul",
    )(x, gain.reshape(1, d), shift, scale, w_bf16)


def _hgrn_constants():
    c = HG_CHUNK
    t = np.arange(c)[:, None]
    u = np.arange(c)[None, :]
    mats = [(u <= t), (u > t)]
    for l in range(HG_MXU_LEVELS):
        m = 1 << l
        r = (t // (2 * m)) * (2 * m) + m - 1
        mats.append((u > np.minimum(t, r)) & (u <= np.maximum(t, r)))
    fwd = np.concatenate(mats, axis=0).astype(np.float32)
    bwd = np.concatenate([mm[::-1, ::-1] for mm in mats], axis=0).astype(np.float32)
    x = np.bitwise_xor(t, u)
    lvl = np.where(x > 0, np.floor(np.log2(np.maximum(x, 1))), HG_LEVELS).astype(np.int32)
    lv_f = np.where(t >= u, lvl, -1).astype(np.int32)
    return fwd, bwd, lv_f, lv_f.T.copy()


def _hgrn_chunk(qs, fpres, vs, lbs, sts, forwards, mcats, lvs):
    c = HG_CHUNK
    n = len(qs)
    fs = [lb + (1.0 - lb) * _sigmoid(fp) for lb, fp in zip(lbs, fpres)]
    kks = [1.0 - f for f in fs]
    logs = [_split2(jnp.log(f)) for f in fs]
    xs = [None] * n
    for fwd in (True, False):
        ids = [i for i in range(n) if forwards[i] == fwd]
        parts = [half for i in ids for half in logs[i]]
        x_all = _dot(mcats[0 if fwd else 1], jnp.concatenate(parts, axis=1))
        for j, i in enumerate(ids):
            xs[i] = x_all[:, 2 * j * c:(2 * j + 1) * c] + x_all[:, (2 * j + 1) * c:(2 * j + 2) * c]
    gs = [x[0:c] for x in xs]
    lv_of = [lvs[0 if f else 1] for f in forwards]
    qbs = [q.astype(BF16) for q in qs]
    kbs = [kk.astype(BF16) for kk in kks]
    vbs = [v.astype(BF16) for v in vs]
    os_ = [_dot_nt((q * jnp.exp(g)).astype(BF16), st.astype(BF16)) for q, g, st in zip(qs, gs, sts)]
    accs = [jnp.where(lv == HG_LEVELS, _dot_nt(qb, kb), 0.0) for lv, qb, kb in zip(lv_of, qbs, kbs)]
    for l in range(HG_LEVELS):
        for i in range(n):
            if l < HG_MXU_LEVELS:
                x = xs[i][(2 + l) * c:(3 + l) * c]
            else:
                m = 1 << l
                ref_rows = [j * 2 * m + (m - 1 if forwards[i] else m) for j in range(c // (2 * m))]
                g_ref = jnp.concatenate(
                    [jnp.broadcast_to(gs[i][r:r + 1, :], (2 * m, c)) for r in ref_rows], axis=0)
                x = -jnp.abs(gs[i] - g_ref)
            e = jnp.exp(x)
            p = _dot_nt((qs[i] * e).astype(BF16), (kks[i] * e).astype(BF16))
            accs[i] = jnp.where(lv_of[i] == l, p, accs[i])
    outs, new_sts = [], []
    for i in range(n):
        edge_row = c - 1 if forwards[i] else 0
        outs.append(os_[i] + _dot(accs[i].astype(BF16), vbs[i]))
        k_end = (kks[i] * jnp.exp(xs[i][c:2 * c])).astype(BF16)
        new_sts.append(sts[i] * jnp.exp(gs[i][edge_row:edge_row + 1, :]) + _dot_tn(vbs[i], k_end))
    return outs, new_sts


def _hgrn_kernel(*refs, has_init):
    if has_init:
        (qf_ref, qb_ref, ff_ref, fb_ref, vf_ref, vb_ref, lbl_ref, mf_ref, mb_ref, lvf_ref, lvb_ref,
         s0f_ref, s0b_ref, of_ref, ob_ref, sf_ref, sb_ref, stf, stb) = refs
    else:
        (qf_ref, qb_ref, ff_ref, fb_ref, vf_ref, vb_ref, lbl_ref, mf_ref, mb_ref, lvf_ref, lvb_ref,
         of_ref, ob_ref, sf_ref, sb_ref, stf, stb) = refs
    c = pl.program_id(1)
    nc = pl.num_programs(1)

    @pl.when(c == 0)
    def _():
        for h in range(HG_HEADS):
            if has_init:
                stf[h] = s0f_ref[0, h].T
                stb[h] = s0b_ref[0, h].T
            else:
                stf[h] = jnp.zeros((HG_DV, HG_DK), F32)
                stb[h] = jnp.zeros((HG_DV, HG_DK), F32)

    rows = [lbl_ref[:, j, :] for j in range(lbl_ref.shape[1])]
    mx = functools.reduce(jnp.maximum, rows)
    ex = [jnp.exp(r - mx) for r in rows]
    lb = ex[0] / functools.reduce(lambda a, b: a + b, ex)

    heads = [slice(h * LANES, (h + 1) * LANES) for h in range(HG_HEADS)]
    nh = HG_HEADS
    outs, sts = _hgrn_chunk(
        [qf_ref[0, :, hs] for hs in heads] + [qb_ref[0, :, hs] for hs in heads],
        [ff_ref[0, :, hs] for hs in heads] + [fb_ref[0, :, hs] for hs in heads],
        [vf_ref[0, :, hs] for hs in heads] + [vb_ref[0, :, hs] for hs in heads],
        [lb[0:1, hs] for hs in heads] + [lb[1:2, hs] for hs in heads],
        [stf[h] for h in range(nh)] + [stb[h] for h in range(nh)],
        [True] * nh + [False] * nh,
        (mf_ref[...], mb_ref[...]), (lvf_ref[...], lvb_ref[...]))
    o_f, o_b, st_f, st_b = outs[:nh], outs[nh:], sts[:nh], sts[nh:]
    for h, hs in enumerate(heads):
        of_ref[0, :, hs] = o_f[h]
        ob_ref[0, :, hs] = o_b[h]
        stf[h] = st_f[h]
        stb[h] = st_b[h]

    @pl.when(c == nc - 1)
    def _():
        for h in range(HG_HEADS):
            sf_ref[0, h] = stf[h].T
            sb_ref[0, h] = stb[h].T


def hgrn_bidir(z, lb_logits, s0f, s0b, batch, seq):
    nc = seq // HG_CHUNK
    z3 = z.reshape(batch, seq, z.shape[1])
    mf, mb, lvf, lvb = _hgrn_constants()
    has_init = s0f is not None
    width = HG_HEADS * LANES
    blk = (1, HG_CHUNK, width)
    fwd = lambda off: pl.BlockSpec(blk, lambda b, c: (b, c, off))
    bwd = lambda off: pl.BlockSpec(blk, lambda b, c: (b, nc - 1 - c, off))
    full = lambda a: pl.BlockSpec(a.shape, lambda b, c: (0,) * a.ndim)
    st_spec = pl.BlockSpec((1, HG_HEADS, HG_DK, HG_DV), lambda b, c: (b, 0, 0, 0))
    consts = [jnp.asarray(mf, BF16), jnp.asarray(mb, BF16), jnp.asarray(lvf), jnp.asarray(lvb)]
    in_specs = [fwd(0), bwd(0), fwd(1), bwd(2), fwd(3), bwd(3),
                pl.BlockSpec(lb_logits.shape, lambda b, c: (0, 0, 0))]
    in_specs += [full(a) for a in consts]
    args = [z3] * 6 + [lb_logits] + consts
    if has_init:
        in_specs += [st_spec, st_spec]
        args += [s0f, s0b]
    o_shape = jax.ShapeDtypeStruct((batch, seq, width), F32)
    s_shape = jax.ShapeDtypeStruct((batch, HG_HEADS, HG_DK, HG_DV), F32)
    o_f, o_b, s_f, s_b = pl.pallas_call(
        functools.partial(_hgrn_kernel, has_init=has_init),
        grid=(batch, nc),
        in_specs=in_specs,
        out_specs=[pl.BlockSpec(blk, lambda b, c: (b, c, 0)),
                   pl.BlockSpec(blk, lambda b, c: (b, nc - 1 - c, 0)),
                   st_spec, st_spec],
        out_shape=[o_shape, o_shape, s_shape, s_shape],
        scratch_shapes=[pltpu.VMEM((HG_HEADS, HG_DV, HG_DK), F32), pltpu.VMEM((HG_HEADS, HG_DV, HG_DK), F32)],
        compiler_params=_cparams(("parallel", "arbitrary")),
        name="hgrn_bidir",
    )(*args)
    t = batch * seq
    return o_f.reshape(t, -1), o_b.reshape(t, -1), s_f, s_b


def _axial_tables(n_tokens, n_rot):
    t = jnp.arange(n_tokens)
    row = (t // GRID_W).astype(F32)
    col = (t % GRID_W).astype(F32)
    n_freq = n_rot // 4
    inv = jnp.power(ROPE_THETA, -jnp.arange(n_freq, dtype=F32) / n_freq)
    ang = jnp.concatenate([row[:, None] * inv, col[:, None] * inv], axis=-1)
    return jnp.cos(ang), jnp.sin(ang)


def _rope_lane_tables(n_tokens, n_rot, head_width, first_rot_lane):
    cos, sin = _axial_tables(n_tokens, n_rot)
    half = n_rot // 2
    c_head = jnp.ones((n_tokens, head_width), F32)
    sa_head = jnp.zeros((n_tokens, head_width), F32)
    sb_head = jnp.zeros((n_tokens, head_width), F32)
    a0, a1, a2 = first_rot_lane, first_rot_lane + half, first_rot_lane + n_rot
    c_head = c_head.at[:, a0:a1].set(cos).at[:, a1:a2].set(cos)
    sa_head = sa_head.at[:, a0:a1].set(-sin)
    sb_head = sb_head.at[:, a1:a2].set(sin)
    reps = LANES // head_width
    tile = lambda a: jnp.tile(a, (1, reps))
    return tile(c_head), tile(sa_head), tile(sb_head)


def _rope(x, c, sa, sb, half):
    return x * c + pltpu.roll(x, LANES - half, 1) * sa + pltpu.roll(x, half, 1) * sb


def _mla_q_kernel(*refs, rope):
    if rope:
        qa_ref, qan_ref, qb_ref, qn_ref, c_ref, sa_ref, sb_ref, o_ref = refs
    else:
        qa_ref, qan_ref, qb_ref, qn_ref, o_ref = refs
    qa = qa_ref[...]
    qn = qa * lax.rsqrt(jnp.mean(qa * qa, axis=-1, keepdims=True) + EPS) * qan_ref[...]
    qfull = _dot(qn.astype(BF16), qb_ref[...])
    outs = []
    for h in range(MLA_HEADS):
        qh = qfull[:, h * LANES:(h + 1) * LANES]
        ms = jnp.sum(qh * qh, axis=-1, keepdims=True) * (1.0 / MLA_QK)
        qh = qh * lax.rsqrt(ms + EPS) * qn_ref[...]
        if rope:
            qh = _rope(qh, c_ref[...], sa_ref[...], sb_ref[...], MLA_ROPE // 2)
        outs.append((qh * MLA_SCALE).astype(BF16))
    o_ref[...] = jnp.concatenate(outs, axis=1)


def mla_queries(z, qa_norm, q_b_pad, q_norm_pad, tables, tokens_per_batch, tm=256):
    t = z.shape[0]
    rope = tables is not None
    row = lambda i: (0, 0)
    in_specs = [pl.BlockSpec((tm, MLA_Q_LORA), lambda i: (i, 2560 // MLA_Q_LORA)),
                pl.BlockSpec((1, MLA_Q_LORA), row),
                pl.BlockSpec(q_b_pad.shape, row),
                pl.BlockSpec((1, LANES), row)]
    args = [z, qa_norm.reshape(1, -1), q_b_pad, q_norm_pad]
    if rope:
        tpb = tokens_per_batch // tm
        in_specs += [pl.BlockSpec((tm, LANES), lambda i: (i % tpb, 0))] * 3
        args += list(tables)
    return pl.pallas_call(
        functools.partial(_mla_q_kernel, rope=rope),
        grid=(t // tm,),
        in_specs=in_specs,
        out_specs=pl.BlockSpec((tm, MLA_HEADS * LANES), lambda i: (i, 0)),
        out_shape=jax.ShapeDtypeStruct((t, MLA_HEADS * LANES), BF16),
        compiler_params=_cparams(("parallel",)),
        name="mla_queries",
    )(*args)


def _mla_kv_kernel(*refs, norm_input, rope):
    if rope:
        kva_ref, kpe_ref, kvan_ref, wk_ref, wv_ref, kn_ref, c_ref, sa_ref, sb_ref, ckv_ref, k_ref, v_ref = refs
    else:
        kva_ref, kpe_ref, kvan_ref, wk_ref, wv_ref, kn_ref, ckv_ref, k_ref, v_ref = refs
    ckv = kva_ref[...]
    if norm_input:
        ckv = ckv * lax.rsqrt(jnp.mean(ckv * ckv, axis=-1, keepdims=True) + EPS) * kvan_ref[...]
    ckv_ref[...] = ckv
    cb = ckv.astype(BF16)
    knope = _dot(cb, wk_ref[...])
    v_ref[...] = _dot(cb, wv_ref[...]).astype(BF16)
    kpe = pltpu.roll(kpe_ref[...], MLA_NOPE, 1)
    outs = []
    for h in range(MLA_HEADS):
        kh = knope[:, h * LANES:(h + 1) * LANES] + kpe
        ms = jnp.sum(kh * kh, axis=-1, keepdims=True) * (1.0 / MLA_QK)
        kh = kh * lax.rsqrt(ms + EPS) * kn_ref[...]
        if rope:
            kh = _rope(kh, c_ref[...], sa_ref[...], sb_ref[...], MLA_ROPE // 2)
        outs.append(kh.astype(BF16))
    k_ref[...] = jnp.concatenate(outs, axis=1)


def mla_keys_values(kva_src, kva_col, kpe_src, kpe_col, kva_norm, wk_pad, wv, k_norm_pad, tables,
                    tokens_per_batch, norm_input, tm=256):
    t = kva_src.shape[0]
    rope = tables is not None
    row = lambda i: (0, 0)
    in_specs = [pl.BlockSpec((tm, LANES), lambda i: (i, kva_col)),
                pl.BlockSpec((tm, LANES), lambda i: (i, kpe_col)),
                pl.BlockSpec((1, LANES), row),
                pl.BlockSpec(wk_pad.shape, row),
                pl.BlockSpec(wv.shape, row),
                pl.BlockSpec((1, LANES), row)]
    args = [kva_src, kpe_src, kva_norm.reshape(1, -1), wk_pad, wv, k_norm_pad]
    if rope:
        tpb = tokens_per_batch // tm
        in_specs += [pl.BlockSpec((tm, LANES), lambda i: (i % tpb, 0))] * 3
        args += list(tables)
    return pl.pallas_call(
        functools.partial(_mla_kv_kernel, norm_input=norm_input, rope=rope),
        grid=(t // tm,),
        in_specs=in_specs,
        out_specs=[pl.BlockSpec((tm, LANES), lambda i: (i, 0)),
                   pl.BlockSpec((tm, MLA_HEADS * LANES), lambda i: (i, 0)),
                   pl.BlockSpec((tm, MLA_HEADS * MLA_V), lambda i: (i, 0))],
        out_shape=[jax.ShapeDtypeStruct((t, LANES), F32),
                   jax.ShapeDtypeStruct((t, MLA_HEADS * LANES), BF16),
                   jax.ShapeDtypeStruct((t, MLA_HEADS * MLA_V), BF16)],
        compiler_params=_cparams(("parallel",)),
        name="mla_keys_values",
    )(*args)


def _mla_attn_kernel(q_ref, k_ref, vt_ref, o_ref):
    heads = range(2)
    ss = [_dot_nt(k_ref[0][:, j * LANES:(j + 1) * LANES], q_ref[0][:, j * LANES:(j + 1) * LANES]) for j in heads]
    ms = [jnp.max(s, axis=0, keepdims=True) for s in ss]
    ps = [jnp.exp(s - m) for s, m in zip(ss, ms)]
    ls = [jnp.sum(p, axis=0, keepdims=True) for p in ps]
    ots = [_dot(vt_ref[0][j * MLA_V:(j + 1) * MLA_V, :], ps[j].astype(BF16)) / ls[j] for j in heads]
    o_ref[0] = jnp.concatenate(ots, axis=0).T


def mla_attention(q, k, v, batch, n_q, n_k, tq=256):
    q3 = q.reshape(batch, n_q, -1)
    k3 = k.reshape(batch, n_k, -1)
    vt = jnp.swapaxes(v.reshape(batch, n_k, -1), 1, 2)
    out = pl.pallas_call(
        _mla_attn_kernel,
        grid=(batch, MLA_HEADS // 2, n_q // tq),
        in_specs=[pl.BlockSpec((1, tq, 2 * LANES), lambda b, j, i: (b, i, j)),
                  pl.BlockSpec((1, n_k, 2 * LANES), lambda b, j, i: (b, 0, j)),
                  pl.BlockSpec((1, 2 * MLA_V, n_k), lambda b, j, i: (b, j, 0))],
        out_specs=pl.BlockSpec((1, tq, 2 * MLA_V), lambda b, j, i: (b, i, j)),
        out_shape=jax.ShapeDtypeStruct((batch, n_q, MLA_HEADS * MLA_V), F32),
        compiler_params=_cparams(("parallel", "parallel", "arbitrary")),
        name="mla_attention",
    )(q3, k3, vt)
    return out.reshape(batch * n_q, -1)


def _out0_kernel(of_ref, ob_ref, ag_ref, hgn_ref, om_ref, w_ref, x_ref, g_ref, o_ref):
    o = of_ref[...] + ob_ref[...]
    ag = ag_ref[...]
    parts = []
    for h in range(HG_HEADS):
        oh = o[:, h * HG_DV:(h + 1) * HG_DV]
        oh = oh * lax.rsqrt(jnp.mean(oh * oh, axis=-1, keepdims=True) + EPS) * hgn_ref[...]
        gh = ag[:, h * HG_DV:(h + 1) * HG_DV]
        parts.append((oh * (gh * _sigmoid(gh))).astype(BF16))
    oa = jnp.concatenate(parts, axis=1)
    n_a = HG_HEADS * HG_DV
    mix = _dot(oa, w_ref[0:n_a, :]) + _dot(om_ref[...].astype(BF16), w_ref[n_a:, :])
    o_ref[...] = x_ref[...] + g_ref[0] * mix


def out_proj_layer0(o_f, o_b, z, hg_norm, o_mla, w_out_bf16, x, gate, tokens_per_group, tm=512):
    t, d = x.shape
    n_a = HG_HEADS * HG_DV
    tiles_per_group = tokens_per_group // tm
    tile = lambda w: pl.BlockSpec((tm, w), lambda i: (i, 0))
    return pl.pallas_call(
        _out0_kernel,
        grid=(t // tm,),
        in_specs=[tile(n_a), tile(n_a),
                  pl.BlockSpec((tm, n_a), lambda i: (i, 2048 // n_a)),
                  pl.BlockSpec((1, HG_DV), lambda i: (0, 0)),
                  tile(o_mla.shape[1]),
                  pl.BlockSpec(w_out_bf16.shape, lambda i: (0, 0)),
                  tile(d),
                  pl.BlockSpec((1, 1, d), lambda i: (i // tiles_per_group, 0, 0))],
        out_specs=tile(d),
        out_shape=jax.ShapeDtypeStruct((t, d), F32),
        compiler_params=_cparams(("parallel",)),
        name="out_proj_layer0",
    )(o_f, o_b, z, hg_norm.reshape(1, -1), o_mla, w_out_bf16, x, gate)


def _out1_kernel(a_ref, w_ref, x_ref, g_ref, o_ref):
    o_ref[...] = x_ref[...] + g_ref[0] * _dot(a_ref[...].astype(BF16), w_ref[...])


def out_proj_layer1(a, w_out_bf16, x, gate, tokens_per_group, tm=512):
    t, d = x.shape
    tiles_per_group = tokens_per_group // tm
    tile = lambda w: pl.BlockSpec((tm, w), lambda i: (i, 0))
    return pl.pallas_call(
        _out1_kernel,
        grid=(t // tm,),
        in_specs=[tile(a.shape[1]),
                  pl.BlockSpec(w_out_bf16.shape, lambda i: (0, 0)),
                  tile(d),
                  pl.BlockSpec((1, 1, d), lambda i: (i // tiles_per_group, 0, 0))],
        out_specs=tile(d),
        out_shape=jax.ShapeDtypeStruct((t, d), F32),
        compiler_params=_cparams(("parallel",)),
        name="out_proj_layer1",
    )(a, w_out_bf16, x, gate)


def _head_rms(x, gain2):
    sq = x * x
    lane = lax.broadcasted_iota(jnp.int32, x.shape, 1)
    first = lane < SWA_HD
    lo = jnp.sum(jnp.where(first, sq, 0.0), axis=-1, keepdims=True)
    hi = jnp.sum(jnp.where(first, 0.0, sq), axis=-1, keepdims=True)
    ms = jnp.where(first, lo, hi) * (1.0 / SWA_HD)
    return x * lax.rsqrt(ms + EPS) * gain2


def _swa_prep_kernel(*refs, rope):
    if rope:
        zq_ref, zk_ref, zv_ref, qn_ref, kn_ref, c_ref, sa_ref, sb_ref, q_ref, kc_ref, k_ref, v_ref = refs
    else:
        zq_ref, zk_ref, zv_ref, qn_ref, kn_ref, q_ref, kc_ref, k_ref, v_ref = refs
    half = SWA_HD // 2

    def rot(x):
        return _rope(x, c_ref[...], sa_ref[...], sb_ref[...], half) if rope else x

    zq = zq_ref[...]
    qs = []
    for p in range(zq.shape[1] // LANES):
        x = _head_rms(zq[:, p * LANES:(p + 1) * LANES], qn_ref[...])
        qs.append((rot(x) * SWA_SCALE).astype(BF16))
    q_ref[...] = jnp.concatenate(qs, axis=1)
    zk = zk_ref[...]
    kn, kr = [], []
    for p in range(zk.shape[1] // LANES):
        x = _head_rms(zk[:, p * LANES:(p + 1) * LANES], kn_ref[...])
        kn.append(x)
        kr.append(rot(x).astype(BF16))
    kc_ref[...] = jnp.concatenate(kn, axis=1)
    k_ref[...] = jnp.concatenate(kr, axis=1)
    v_ref[...] = zv_ref[...].astype(BF16)


def swa_prep(z, q_norm2, k_norm2, tables, tokens_per_batch, tm=256):
    t = z.shape[0]
    nq = SWA_HEADS * SWA_HD
    nkv = SWA_KV_HEADS * SWA_HD
    rope = tables is not None
    row = lambda i: (0, 0)
    in_specs = [pl.BlockSpec((tm, nq), lambda i: (i, 0)),
                pl.BlockSpec((tm, nkv), lambda i: (i, nq // nkv)),
                pl.BlockSpec((tm, nkv), lambda i: (i, nq // nkv + 1)),
                pl.BlockSpec((1, LANES), row),
                pl.BlockSpec((1, LANES), row)]
    args = [z, z, z, q_norm2, k_norm2]
    if rope:
        tpb = tokens_per_batch // tm
        in_specs += [pl.BlockSpec((tm, LANES), lambda i: (i % tpb, 0))] * 3
        args += list(tables)
    tile = lambda w: pl.BlockSpec((tm, w), lambda i: (i, 0))
    return pl.pallas_call(
        functools.partial(_swa_prep_kernel, rope=rope),
        grid=(t // tm,),
        in_specs=in_specs,
        out_specs=[tile(nq), tile(nkv), tile(nkv), tile(nkv)],
        out_shape=[jax.ShapeDtypeStruct((t, nq), BF16),
                   jax.ShapeDtypeStruct((t, nkv), F32),
                   jax.ShapeDtypeStruct((t, nkv), BF16),
                   jax.ShapeDtypeStruct((t, nkv), BF16)],
        compiler_params=_cparams(("parallel",)),
        name="swa_prep",
    )(*args)


def _sink_attention_t(q8, k_all, vt_all, bias, sink_ref, pair):
    tq = q8.shape[0]
    n_q = SWA_GROUP * tq
    lane = lax.broadcasted_iota(jnp.int32, (1, n_q), 1)
    kv = range(2)
    heads = [[n * SWA_GROUP + g for g in range(SWA_GROUP)] for n in kv]
    q4s = [jnp.concatenate([q8[:, hq * SWA_HD:(hq + 1) * SWA_HD] for hq in heads[n]], axis=0) for n in kv]
    sks = []
    for n in kv:
        sk = jnp.zeros((1, n_q), F32)
        for g, hq in enumerate(heads[n]):
            sk = jnp.where(lane // tq == g, sink_ref[pair * 2 * SWA_GROUP + hq], sk)
        sks.append(sk)
    ss = [_dot_nt(k_all[:, n * SWA_HD:(n + 1) * SWA_HD], q4s[n]) for n in kv]
    if bias is not None:
        ss = [s + bias for s in ss]
    ms = [jnp.maximum(jnp.max(ss[n], axis=0, keepdims=True), sks[n]) for n in kv]
    ps = [jnp.exp(ss[n] - ms[n]) for n in kv]
    ls = [jnp.sum(ps[n], axis=0, keepdims=True) + jnp.exp(sks[n] - ms[n]) for n in kv]
    ots = [_dot(vt_all[n * SWA_HD:(n + 1) * SWA_HD, :], ps[n].astype(BF16)) / ls[n] for n in kv]
    pairs = []
    for n in kv:
        for g in range(0, SWA_GROUP, 2):
            two = jnp.concatenate([ots[n][:, g * tq:(g + 1) * tq], ots[n][:, (g + 1) * tq:(g + 2) * tq]], axis=0)
            pairs.append(two.T)
    return jnp.concatenate(pairs, axis=1)


def _swa_dense_kernel(sink_ref, q_ref, k_ref, vt_ref, o_ref):
    o_ref[0] = _sink_attention_t(q_ref[0], k_ref[0], vt_ref[0], None, sink_ref, pl.program_id(1))


def swa_dense(q, k, v, sink, batch, seq):
    nq = SWA_HEADS * SWA_HD
    q3 = q.reshape(batch, seq, nq)
    k3 = k.reshape(batch, seq, -1)
    vt = jnp.swapaxes(v.reshape(batch, seq, -1), 1, 2)
    out = pl.pallas_call(
        _swa_dense_kernel,
        grid=(batch, 2),
        in_specs=[pl.BlockSpec(memory_space=pltpu.SMEM),
                  pl.BlockSpec((1, seq, nq // 2), lambda b, j: (b, 0, j)),
                  pl.BlockSpec((1, seq, LANES), lambda b, j: (b, 0, j)),
                  pl.BlockSpec((1, LANES, seq), lambda b, j: (b, j, 0))],
        out_specs=pl.BlockSpec((1, seq, nq // 2), lambda b, j: (b, 0, j)),
        out_shape=jax.ShapeDtypeStruct((batch, seq, nq), F32),
        compiler_params=_cparams(("parallel", "parallel")),
        name="swa_dense",
    )(sink, q3, k3, vt)
    return out.reshape(batch * seq, nq)


def _swa_window_kernel(sink_ref, q_ref, kc_ref, kp_ref, k0_ref, kn_ref, vc_ref, vp_ref, v0_ref, vn_ref, o_ref):
    i = pl.program_id(2)
    nb = pl.num_programs(2)
    w = SWA_WINDOW
    n_ctx = kc_ref.shape[1]
    k_all = jnp.concatenate([kc_ref[0], kp_ref[0], k0_ref[0], kn_ref[0]], axis=0)
    vt_all = jnp.concatenate([vc_ref[0], vp_ref[0], v0_ref[0], vn_ref[0]], axis=1)
    shape = (n_ctx + 3 * w, SWA_GROUP * w)
    key = lax.broadcasted_iota(jnp.int32, shape, 0) - n_ctx
    r = lax.broadcasted_iota(jnp.int32, shape, 1) % w
    prev_bias = jnp.where(i > 0, 0.0, NEG_BIG)
    next_bias = jnp.where(i < nb - 1, 0.0, NEG_BIG)
    bias = jnp.where(
        key < w,
        jnp.where(key < 0, 0.0, jnp.where(key >= r, prev_bias, NEG_BIG)),
        jnp.where(key < 2 * w, 0.0, jnp.where(key - 2 * w <= r, next_bias, NEG_BIG)))
    o_ref[0] = _sink_attention_t(q_ref[0], k_all, vt_all, bias, sink_ref, pl.program_id(1))


def swa_window(q, k, v, k_ctx, v_ctx, sink, batch, seq, n_ctx):
    nq = SWA_HEADS * SWA_HD
    w = SWA_WINDOW
    nb = seq // w
    q3 = q.reshape(batch, seq, nq)
    k3 = k.reshape(batch, seq, -1)
    vt = jnp.swapaxes(v.reshape(batch, seq, -1), 1, 2)
    vt_ctx = jnp.swapaxes(v_ctx, 1, 2)
    ctx = pl.BlockSpec((1, n_ctx, LANES), lambda b, j, i: (b, 0, j))
    prv = pl.BlockSpec((1, w, LANES), lambda b, j, i: (b, jnp.maximum(i - 1, 0), j))
    cur = pl.BlockSpec((1, w, LANES), lambda b, j, i: (b, i, j))
    nxt = pl.BlockSpec((1, w, LANES), lambda b, j, i: (b, jnp.minimum(i + 1, nb - 1), j))
    ctx_t = pl.BlockSpec((1, LANES, n_ctx), lambda b, j, i: (b, j, 0))
    prv_t = pl.BlockSpec((1, LANES, w), lambda b, j, i: (b, j, jnp.maximum(i - 1, 0)))
    cur_t = pl.BlockSpec((1, LANES, w), lambda b, j, i: (b, j, i))
    nxt_t = pl.BlockSpec((1, LANES, w), lambda b, j, i: (b, j, jnp.minimum(i + 1, nb - 1)))
    out = pl.pallas_call(
        _swa_window_kernel,
        grid=(batch, 2, nb),
        in_specs=[pl.BlockSpec(memory_space=pltpu.SMEM),
                  pl.BlockSpec((1, w, nq // 2), lambda b, j, i: (b, i, j)),
                  ctx, prv, cur, nxt, ctx_t, prv_t, cur_t, nxt_t],
        out_specs=pl.BlockSpec((1, w, nq // 2), lambda b, j, i: (b, i, j)),
        out_shape=jax.ShapeDtypeStruct((batch, seq, nq), F32),
        compiler_params=_cparams(("parallel", "parallel", "arbitrary")),
        name="swa_window",
    )(sink, q3, k_ctx, k3, k3, k3, vt_ctx, vt, vt, vt)
    return out.reshape(batch * seq, nq)


def _router_kernel(x_ref, g_ref, sh_ref, sc_ref, rw_ref, rb_ref, tri_ref, h_ref, idx_ref, gate_ref, rank_ref,
                   cnt_ref, cnt, *, tiles_per_part):
    @pl.when(pl.program_id(0) % tiles_per_part == 0)
    def _():
        cnt[...] = jnp.zeros_like(cnt)

    h = _modnorm(x_ref[...], g_ref[...], sh_ref[0], sc_ref[0])
    bits = lax.bitcast_convert_type(h.astype(BF16).astype(F32), jnp.uint32)
    half = h.shape[1] // 2
    h_ref[...] = (bits[:, :half] >> 16) | (bits[:, half:] & jnp.uint32(0xFFFF0000))
    wh, wl = _split2(rw_ref[...])
    hh, hl = _split2(h)
    logits = _dot_nt(wh, hh) + _dot_nt(wh, hl) + _dot_nt(wl, hh) + rb_ref[...]
    row = lax.broadcasted_iota(jnp.int32, logits.shape, 0)
    work = logits
    vals, idxs = [], []
    for _ in range(TOP_K):
        m = jnp.max(work, axis=0, keepdims=True)
        ik = jnp.min(jnp.where(work == m, row, LANES), axis=0, keepdims=True)
        vals.append(m)
        idxs.append(ik)
        work = jnp.where(row == ik, 2.0 * NEG_BIG, work)
    es = [jnp.exp(v - vals[0]) for v in vals]
    denom = es[0] + es[1] + es[2] + es[3]
    chosen = jnp.zeros(logits.shape, F32)
    for k in range(TOP_K):
        chosen = jnp.where(row == idxs[k], 1.0, chosen)
    chosen_b = chosen.astype(BF16)
    seen = cnt[...]
    before = _dot(chosen_b, tri_ref[...]) + jnp.concatenate([seen] * (logits.shape[1] // LANES), axis=1)
    out_row = lax.broadcasted_iota(jnp.int32, idx_ref.shape, 0)
    idx_out = jnp.zeros(idx_ref.shape, jnp.int32)
    gate_out = jnp.zeros(idx_ref.shape, F32)
    rank_out = jnp.zeros(idx_ref.shape, jnp.int32)
    for k in range(TOP_K):
        rk = jnp.sum(jnp.where(row == idxs[k], before, 0.0), axis=0, keepdims=True)
        idx_out = jnp.where(out_row == k, idxs[k], idx_out)
        gate_out = jnp.where(out_row == k, es[k] / denom, gate_out)
        rank_out = jnp.where(out_row == k, rk.astype(jnp.int32), rank_out)
    idx_ref[...] = idx_out
    gate_ref[...] = gate_out
    rank_ref[...] = rank_out
    cnt[...] = seen + _dot(chosen_b, jnp.ones((logits.shape[1], LANES), BF16))
    cnt_ref[0] = cnt[...]


def moe_router(x, gain, shift, scale, rw_t, rb_col, tokens_per_group, tm=256):
    t, d = x.shape
    tiles_per_group = tokens_per_group // tm
    tiles_per_part = MOE_PART // tm
    grp = lambda i: (i // tiles_per_group, 0, 0)
    tile = lambda w: pl.BlockSpec((tm, w), lambda i: (i, 0))
    slab = pl.BlockSpec((8, tm), lambda i: (0, i))
    tri = jnp.asarray(np.triu(np.ones((tm, tm), np.float32), 1), BF16)
    hp, idx, gates, rank, cnt = pl.pallas_call(
        functools.partial(_router_kernel, tiles_per_part=tiles_per_part),
        grid=(t // tm,),
        in_specs=[tile(d),
                  pl.BlockSpec((1, d), lambda i: (0, 0)),
                  pl.BlockSpec((1, 1, d), grp),
                  pl.BlockSpec((1, 1, d), grp),
                  pl.BlockSpec((LANES, d), lambda i: (0, 0)),
                  pl.BlockSpec((LANES, 1), lambda i: (0, 0)),
                  pl.BlockSpec((tm, tm), lambda i: (0, 0))],
        out_specs=[tile(d // 2), slab, slab, slab,
                   pl.BlockSpec((1, LANES, LANES), lambda i: (i // tiles_per_part, 0, 0))],
        out_shape=[jax.ShapeDtypeStruct((t, d // 2), jnp.uint32),
                   jax.ShapeDtypeStruct((8, t), jnp.int32),
                   jax.ShapeDtypeStruct((8, t), F32),
                   jax.ShapeDtypeStruct((8, t), jnp.int32),
                   jax.ShapeDtypeStruct((t // MOE_PART, LANES, LANES), F32)],
        scratch_shapes=[pltpu.VMEM((LANES, LANES), F32)],
        compiler_params=_cparams(("arbitrary",)),
        name="moe_router",
    )(x, gain.reshape(1, d), shift, scale, rw_t, rb_col, tri)
    per_token = lambda a: a[:TOP_K].T
    return hp, per_token(idx), per_token(gates), per_token(rank), cnt[:, :N_EXPERTS, 0].astype(jnp.int32)


FFN_COLS = 512
FFN_PIECES = 2 * D_FF // FFN_COLS + D_MODEL // FFN_COLS
MOE_PART = 4096
ADD_BATCH = 8


def _unpack_rows(words):
    lo = lax.bitcast_convert_type(words << 16, F32)
    hi = lax.bitcast_convert_type(words & jnp.uint32(0xFFFF0000), F32)
    return jnp.concatenate([lo, hi], axis=1).astype(BF16)


def _ffn_block(x_ref, y_ref, wgu_ref, bgu_ref, wd_ref, bd_ref, between):
    xb = _unpack_rows(x_ref[...])
    acts = []
    for c in range(D_FF // FFN_COLS):
        lo, hi = c * FFN_COLS, (c + 1) * FFN_COLS
        between(2 * c)
        gate = _dot(xb, wgu_ref[0, :, lo:hi]) + bgu_ref[0][:, lo:hi]
        between(2 * c + 1)
        up = _dot(xb, wgu_ref[0, :, D_FF + lo:D_FF + hi]) + bgu_ref[0][:, D_FF + lo:D_FF + hi]
        gate = jnp.minimum(gate, SWIGLU_LIMIT)
        up = jnp.clip(up, -SWIGLU_LIMIT, SWIGLU_LIMIT)
        acts.append((gate * _sigmoid(SWIGLU_ALPHA * gate) * (up + 1.0)).astype(BF16))
    act = jnp.concatenate(acts, axis=1)
    for n in range(D_MODEL // FFN_COLS):
        lo, hi = n * FFN_COLS, (n + 1) * FFN_COLS
        between(2 * D_FF // FFN_COLS + n)
        y_ref[:, lo:hi] = _dot(act, wd_ref[0, :, lo:hi]) + bd_ref[0][:, lo:hi]


def _expert_kernel(nb_ref, bs_ref, loc_ref, gate_ref, hp_hbm, wgu_ref, bgu_ref, wd_ref, bd_ref, out_hbm,
                   hbuf, acc, xbuf, ybuf, sem):
    part = pl.program_id(0)
    e = pl.program_id(1)
    idx = part * N_EXPERTS + e
    nb = nb_ref[idx]
    b0 = bs_ref[idx]
    shares = np.array_split(np.arange(MOE_ROWS), FFN_PIECES)

    def fetch_rows(blk):
        base = blk * MOE_ROWS
        dst = xbuf.at[blk % 2]

        def emit(k):
            for r in shares[k]:
                r = int(r)
                dst[r:r + 1, :] = hbuf[pl.ds(loc_ref[base + r], 1), :]
        return emit

    def add_rows(blk):
        base = blk * MOE_ROWS
        src = ybuf.at[blk % 2]

        def emit(k):
            for batch in np.array_split(shares[k], max(1, len(shares[k]) // ADD_BATCH)):
                rows = [loc_ref[base + int(r)] for r in batch]
                new = [acc[pl.ds(row, 1), :] + gate_ref[base + int(r)] * src[int(r):int(r) + 1, :]
                       for row, r in zip(rows, batch)]
                for row, val in zip(rows, new):
                    acc[pl.ds(row, 1), :] = val
        return emit

    def emit_all(f):
        for k in range(FFN_PIECES):
            f(k)

    @pl.when(e == 0)
    def _():
        load = pltpu.make_async_copy(hp_hbm.at[pl.ds(part * MOE_PART, MOE_PART), :],
                                     hbuf.at[pl.ds(0, MOE_PART), :], sem.at[0])
        load.start()
        hbuf[MOE_PART:, :] = jnp.zeros((hbuf.shape[0] - MOE_PART, hbuf.shape[1]), hbuf.dtype)
        acc[...] = jnp.zeros_like(acc)
        ybuf[...] = jnp.zeros_like(ybuf)
        load.wait()
        emit_all(fetch_rows(b0))

    def block(i, carry):
        fetch, add = fetch_rows(i + 1), add_rows(i - 1)

        def between(k):
            fetch(k)
            add(k)
        _ffn_block(xbuf.at[i % 2], ybuf.at[i % 2], wgu_ref, bgu_ref, wd_ref, bd_ref, between)
        return carry

    lax.fori_loop(b0, b0 + nb, block, 0)

    @pl.when(e == N_EXPERTS - 1)
    def _():
        emit_all(add_rows(b0 + nb - 1))
        store = pltpu.make_async_copy(acc.at[pl.ds(0, MOE_PART), :],
                                      out_hbm.at[pl.ds(part * MOE_PART, MOE_PART), :], sem.at[0])
        store.start()
        store.wait()


def moe_experts(hp, n_blk, blk_start, row_loc, row_gate, w_gu, b_gu, w_down, b_down):
    t = hp.shape[0]
    d = D_MODEL
    per_expert = lambda p, e, nb, bs, sl: (e, 0, 0)
    grid_spec = pltpu.PrefetchScalarGridSpec(
        num_scalar_prefetch=3,
        grid=(t // MOE_PART, N_EXPERTS),
        in_specs=[pl.BlockSpec(memory_space=pltpu.SMEM),
                  pl.BlockSpec(memory_space=pl.ANY),
                  pl.BlockSpec((1, d, 2 * D_FF), per_expert),
                  pl.BlockSpec((1, 1, 2 * D_FF), per_expert),
                  pl.BlockSpec((1, D_FF, d), per_expert),
                  pl.BlockSpec((1, 1, d), per_expert)],
        out_specs=pl.BlockSpec(memory_space=pl.ANY),
        scratch_shapes=[pltpu.VMEM((MOE_PART + 8, d // 2), jnp.uint32),
                        pltpu.VMEM((MOE_PART + 8, d), F32),
                        pltpu.VMEM((2, MOE_ROWS, d // 2), jnp.uint32),
                        pltpu.VMEM((2, MOE_ROWS, d), F32),
                        pltpu.SemaphoreType.DMA((1,))],
    )
    return pl.pallas_call(
        _expert_kernel,
        grid_spec=grid_spec,
        out_shape=jax.ShapeDtypeStruct((t, d), F32),
        compiler_params=pltpu.CompilerParams(dimension_semantics=("arbitrary", "arbitrary"),
                                             vmem_limit_bytes=EXPERT_VMEM_LIMIT),
        name="moe_experts",
    )(n_blk, blk_start, row_loc, row_gate, hp, w_gu, b_gu.reshape(N_EXPERTS, 1, -1), w_down,
      b_down.reshape(N_EXPERTS, 1, -1))


def _residual_kernel(y_ref, x_ref, g_ref, o_ref):
    o_ref[...] = x_ref[...] + g_ref[0] * y_ref[...]


def gated_residual(y, x, gate_mod, tokens_per_group, tm=512):
    t, d = x.shape
    tiles_per_group = tokens_per_group // tm
    tile = pl.BlockSpec((tm, d), lambda i: (i, 0))
    return pl.pallas_call(
        _residual_kernel,
        grid=(t // tm,),
        in_specs=[tile, tile, pl.BlockSpec((1, 1, d), lambda i: (i // tiles_per_group, 0, 0))],
        out_specs=tile,
        out_shape=jax.ShapeDtypeStruct((t, d), F32),
        compiler_params=_cparams(("parallel",)),
        name="moe_residual",
    )(y, x, gate_mod)


def _assignment_tables(top_idx, rank, gates, counts):
    t = top_idx.shape[0]
    n_parts = t // MOE_PART
    n_assign = t * TOP_K
    part_blocks = MOE_PART * TOP_K // MOE_ROWS + N_EXPERTS + 1
    n_rows = (1 + n_parts * part_blocks) * MOE_ROWS
    padded = (counts + MOE_ROWS - 1) // MOE_ROWS * MOE_ROWS
    part_first = (1 + jnp.arange(n_parts, dtype=jnp.int32) * part_blocks) * MOE_ROWS
    row_start = part_first[:, None] + jnp.cumsum(padded, axis=1) - padded
    experts = jnp.arange(N_EXPERTS, dtype=jnp.int32)
    by_part = top_idx.reshape(n_parts, MOE_PART, TOP_K)
    start_of = jnp.sum(jnp.where(by_part[..., None] == experts, row_start[:, None, None, :], 0), axis=-1)
    dest = (start_of.reshape(t, TOP_K) + rank).reshape(-1).astype(jnp.int32)
    local = (jnp.arange(n_assign, dtype=jnp.int32) // TOP_K) % MOE_PART
    vals = jnp.stack([local, lax.bitcast_convert_type(gates.reshape(-1), jnp.int32)], axis=1)
    init = jnp.broadcast_to(jnp.array([MOE_PART, 0], jnp.int32), (n_rows, 2))
    table = init.at[dest].set(vals, unique_indices=True, mode='promise_in_bounds')
    row_loc = table[:, 0]
    row_gate = lax.bitcast_convert_type(table[:, 1], F32)
    return row_loc, row_gate, (padded // MOE_ROWS).reshape(-1).astype(jnp.int32), \
        (row_start // MOE_ROWS).reshape(-1).astype(jnp.int32)


def moe_layer(x, p, shift, scale, gate_mod, tokens_per_group):
    hp, idx, gates, rank, counts = moe_router(x, p['norm2'], shift, scale, p['rw_t'], p['rb_col'], tokens_per_group)
    row_loc, row_gate, n_blk, blk_start = _assignment_tables(idx, rank, gates, counts)
    y = moe_experts(hp, n_blk, blk_start, row_loc, row_gate, p['w_gu_bf'], p['b_gu'], p['w_down_bf'], p['b_down'])
    return gated_residual(y, x, gate_mod, tokens_per_group)


def _pad_lanes(a, width):
    return jnp.pad(a, [(0, 0)] * (a.ndim - 1) + [(0, width - a.shape[-1])])


def _prep_common(p):
    p['rw_t'] = _pad_lanes(p['router_w'], LANES).T
    p['rb_col'] = jnp.concatenate(
        [p['router_b'].astype(F32), jnp.full((LANES - N_EXPERTS,), NEG_BIG, F32)]).reshape(LANES, 1)
    p['w_out'] = p['w_out'].astype(BF16)
    p['w_gu_bf'] = p['w_gu'].astype(BF16)
    p['w_down_bf'] = p['w_down'].astype(BF16)
    return p


def _prep_layer0(p):
    p = _prep_common(dict(p))
    p['w_in'] = _pad_lanes(p['w_in'], AB_IN_PAD).astype(BF16)
    q_b = p['q_b'].reshape(MLA_Q_LORA, MLA_HEADS, MLA_QK)
    p['q_b_pad'] = _pad_lanes(q_b, LANES).reshape(MLA_Q_LORA, MLA_HEADS * LANES).astype(BF16)
    kv_b = p['kv_b'].reshape(MLA_KV_LORA, MLA_HEADS, MLA_NOPE + MLA_V)
    p['wk_pad'] = _pad_lanes(kv_b[:, :, :MLA_NOPE], LANES).reshape(MLA_KV_LORA, MLA_HEADS * LANES).astype(BF16)
    p['wv'] = kv_b[:, :, MLA_NOPE:].reshape(MLA_KV_LORA, MLA_HEADS * MLA_V).astype(BF16)
    p['q_norm_pad'] = _pad_lanes(p['q_norm'].reshape(1, -1), LANES)
    p['k_norm_pad'] = _pad_lanes(p['k_norm'].reshape(1, -1), LANES)
    return p


def _prep_layer1(p):
    p = _prep_common(dict(p))
    p['w_in'] = p['w_in'].astype(BF16)
    p['q_norm2'] = jnp.tile(p['q_norm'].reshape(1, -1), (1, LANES // SWA_HD))
    p['k_norm2'] = jnp.tile(p['k_norm'].reshape(1, -1), (1, LANES // SWA_HD))
    return p


def _group_forward(x3, mods, p0, p1, lb_logits, caches, latent):
    batch, seq, d = x3.shape
    t = batch * seq
    x = x3.reshape(t, d)
    tpg = seq if latent else t
    sh1, sc1, g1, sh2, sc2, g2 = mods[0]

    z = modnorm_matmul(x, p0['norm1'], sh1, sc1, p0['w_in'], tpg)
    tab_b = _rope_lane_tables(seq, MLA_ROPE, LANES, MLA_NOPE) if latent else None
    s0f, s0b = (caches['hg_f'], caches['hg_b']) if latent else (None, None)
    o_f, o_b, s_f, s_b = hgrn_bidir(z, lb_logits, s0f, s0b, batch, seq)
    q = mla_queries(z, p0['qa_norm'], p0['q_b_pad'], p0['q_norm_pad'], tab_b, seq)
    ckv, k, v = mla_keys_values(z, 2816 // LANES, z, 2944 // LANES, p0['kva_norm'], p0['wk_pad'], p0['wv'],
                                p0['k_norm_pad'], tab_b, seq, norm_input=True)
    n_k = seq
    if latent:
        n_ctx = caches['ckv'].shape[1]
        ckv_c = caches['ckv'].reshape(batch * n_ctx, MLA_KV_LORA)
        kpe_c = _pad_lanes(caches['kpe'].reshape(batch * n_ctx, MLA_ROPE), LANES)
        _, k_c, v_c = mla_keys_values(ckv_c, 0, kpe_c, 0, p0['kva_norm'], p0['wk_pad'], p0['wv'],
                                      p0['k_norm_pad'], None, n_ctx, norm_input=False)
        cat = lambda a, b: jnp.concatenate([a.reshape(batch, n_ctx, -1), b.reshape(batch, seq, -1)],
                                           axis=1).reshape(batch * (n_ctx + seq), -1)
        k, v = cat(k_c, k), cat(v_c, v)
        n_k = n_ctx + seq
    o_mla = mla_attention(q, k, v, batch, seq, n_k)
    x = out_proj_layer0(o_f, o_b, z, p0['hg_out_norm'], o_mla, p0['w_out'], x, g1, tpg)
    x = moe_layer(x, p0, sh2, sc2, g2, tpg)
    state0 = (s_f, s_b, ckv.reshape(batch, seq, MLA_KV_LORA), z[:, 2944:2944 + MLA_ROPE].reshape(batch, seq, MLA_ROPE))

    sh1, sc1, g1, sh2, sc2, g2 = mods[1]
    z = modnorm_matmul(x, p1['norm1'], sh1, sc1, p1['w_in'], tpg)
    tab_c = _rope_lane_tables(seq, SWA_HD, SWA_HD, 0) if latent else None
    q, k_cache, k, v = swa_prep(z, p1['q_norm2'], p1['k_norm2'], tab_c, seq)
    sink = p1['sink'].astype(F32)
    if latent:
        n_ctx = caches['k1'].shape[1]
        k_c = caches['k1'].reshape(batch, n_ctx, -1).astype(BF16)
        v_c = caches['v1'].reshape(batch, n_ctx, -1).astype(BF16)
        a = swa_window(q, k, v, k_c, v_c, sink, batch, seq, n_ctx)
    else:
        a = swa_dense(q, k, v, sink, batch, seq)
    x = out_proj_layer1(a, p1['w_out'], x, g1, tpg)
    x = moe_layer(x, p1, sh2, sc2, g2, tpg)
    nkv = SWA_KV_HEADS * SWA_HD
    state1 = (k_cache.reshape(batch, seq, SWA_KV_HEADS, SWA_HD),
              z[:, SWA_HEADS * SWA_HD + nkv:].reshape(batch, seq, SWA_KV_HEADS, SWA_HD))
    return x.reshape(batch, seq, d), state0, state1


def kernel(x_prompt, x_sample, state_l0_hgrn_fwd, state_l0_hgrn_bwd, cache_l0_mla_ckv, cache_l0_mla_kpe, cache_l1_k, cache_l1_v, c, c_ctx, hgrn_lb_logits, l0_ada_w, l0_ada_b, l0_norm1, l0_norm2, l0_w_in, l0_hg_out_norm, l0_qa_norm, l0_q_b, l0_kva_norm, l0_kv_b, l0_q_norm, l0_k_norm, l0_w_out, l0_router_w, l0_router_b, l0_w_gu, l0_b_gu, l0_w_down, l0_b_down, l1_ada_w, l1_ada_b, l1_norm1, l1_norm2, l1_w_in, l1_q_norm, l1_k_norm, l1_sink, l1_w_out, l1_router_w, l1_router_b, l1_w_gu, l1_b_gu, l1_w_down, l1_b_down):
    p0 = _prep_layer0(dict(norm1=l0_norm1, norm2=l0_norm2, w_in=l0_w_in, hg_out_norm=l0_hg_out_norm,
                           qa_norm=l0_qa_norm, q_b=l0_q_b, kva_norm=l0_kva_norm, kv_b=l0_kv_b,
                           q_norm=l0_q_norm, k_norm=l0_k_norm, w_out=l0_w_out, router_w=l0_router_w,
                           router_b=l0_router_b, w_gu=l0_w_gu, b_gu=l0_b_gu, w_down=l0_w_down,
                           b_down=l0_b_down))
    p1 = _prep_layer1(dict(norm1=l1_norm1, norm2=l1_norm2, w_in=l1_w_in, q_norm=l1_q_norm, k_norm=l1_k_norm,
                           sink=l1_sink, w_out=l1_w_out, router_w=l1_router_w, router_b=l1_router_b,
                           w_gu=l1_w_gu, b_gu=l1_b_gu, w_down=l1_w_down, b_down=l1_b_down))
    dec_batch = c.shape[0]
    d = c.shape[1]
    cond8 = jnp.concatenate([c_ctx[None, :], c, jnp.zeros((8 - 1 - dec_batch, d), F32)], axis=0)
    mods_ctx, mods_lat = [], []
    for w, b in ((l0_ada_w, l0_ada_b), (l1_ada_w, l1_ada_b)):
        mod = ada_params(cond8, w, b)
        mods_ctx.append([m.reshape(1, 1, d) for m in jnp.split(mod[0:1], 6, axis=-1)])
        mods_lat.append([m.reshape(dec_batch, 1, d) for m in jnp.split(mod[1:1 + dec_batch], 6, axis=-1)])

    y_prompt, st0, st1 = _group_forward(x_prompt, mods_ctx, p0, p1, hgrn_lb_logits, None, latent=False)
    caches = dict(hg_f=state_l0_hgrn_fwd, hg_b=state_l0_hgrn_bwd, ckv=cache_l0_mla_ckv, kpe=cache_l0_mla_kpe,
                  k1=cache_l1_k, v1=cache_l1_v)
    y_sample, _, _ = _group_forward(x_sample, mods_lat, p0, p1, hgrn_lb_logits, caches, latent=True)
    return (y_prompt, y_sample, st0[0], st0[1], st0[2], st0[3], st1[0], st1[1])
```

```python
import functools

import numpy as np
import jax
import jax.numpy as jnp
from jax import lax
from jax.experimental import pallas as pl
from jax.experimental.pallas import tpu as pltpu

F32 = jnp.float32
BF16 = jnp.bfloat16

D_MODEL = 1024
GRID_W = 64
ROPE_THETA = 10000.0
EPS = 1e-6
HG_HEADS = 4
HG_DK = 128
HG_DV = 128
MLA_HEADS = 8
MLA_NOPE = 64
MLA_ROPE = 32
MLA_V = 64
MLA_QK = MLA_NOPE + MLA_ROPE
MLA_Q_LORA = 256
MLA_KV_LORA = 128
MLA_SCALE = MLA_QK ** -0.5
SWA_HEADS = 16
SWA_KV_HEADS = 4
SWA_HD = 64
SWA_WINDOW = 128
SWA_SCALE = SWA_HD ** -0.5
SWA_GROUP = SWA_HEADS // SWA_KV_HEADS
N_EXPERTS = 32
TOP_K = 4
D_FF = 1024
SWIGLU_LIMIT = 7.0
SWIGLU_ALPHA = 1.702

LANES = 128
HG_CHUNK = 128
HG_LEVELS = 7
HG_MXU_LEVELS = 0
AB_IN_PAD = 3072
MOE_ROWS = 256
NEG_BIG = -1e30
VMEM_LIMIT = 48 * 1024 * 1024
EXPERT_VMEM_LIMIT = 56 * 1024 * 1024


def _cparams(sem):
    return pltpu.CompilerParams(dimension_semantics=sem, vmem_limit_bytes=VMEM_LIMIT)


def _dot(a, b):
    return jnp.dot(a, b, preferred_element_type=F32)


def _dot_nt(a, b):
    return lax.dot_general(a, b, (((1,), (1,)), ((), ())), preferred_element_type=F32)


def _dot_tn(a, b):
    return lax.dot_general(a, b, (((0,), (0,)), ((), ())), preferred_element_type=F32)


def _split2(x):
    hi = x.astype(BF16)
    lo = (x - hi.astype(F32)).astype(BF16)
    return hi, lo


def _dot_hp(a, b):
    ah, al = _split2(a)
    bh, bl = _split2(b)
    return _dot(ah, bh) + _dot(ah, bl) + _dot(al, bh)


def _sigmoid(x):
    return 1.0 / (1.0 + jnp.exp(-x))


def _modnorm(x, gain, shift, scale):
    y = x * lax.rsqrt(jnp.mean(x * x, axis=-1, keepdims=True) + EPS)
    return y * gain * (1.0 + scale) + shift


def _ada_kernel(c_ref, w_ref, b_ref, o_ref):
    c = c_ref[...]
    o_ref[...] = _dot_hp(c * _sigmoid(c), w_ref[...]) + b_ref[...]


def ada_params(cond8, w, b):
    n = w.shape[1]
    tn = 1024
    return pl.pallas_call(
        _ada_kernel,
        grid=(n // tn,),
        in_specs=[pl.BlockSpec((8, D_MODEL), lambda j: (0, 0)),
                  pl.BlockSpec((D_MODEL, tn), lambda j: (0, j)),
                  pl.BlockSpec((1, tn), lambda j: (0, j))],
        out_specs=pl.BlockSpec((8, tn), lambda j: (0, j)),
        out_shape=jax.ShapeDtypeStruct((8, n), F32),
        compiler_params=_cparams(("parallel",)),
        name="ada_params",
    )(cond8, w, b.reshape(1, n))


def _modnorm_matmul_kernel(x_ref, g_ref, sh_ref, sc_ref, w_ref, o_ref):
    h = _modnorm(x_ref[...], g_ref[...], sh_ref[0], sc_ref[0])
    o_ref[...] = _dot(h.astype(BF16), w_ref[...])


def modnorm_matmul(x, gain, shift, scale, w_bf16, tokens_per_group, tm=512):
    t, d = x.shape
    n = w_bf16.shape[1]
    tiles_per_group = tokens_per_group // tm
    grp = lambda i: (i // tiles_per_group, 0, 0)
    return pl.pallas_call(
        _modnorm_matmul_kernel,
        grid=(t // tm,),
        in_specs=[pl.BlockSpec((tm, d), lambda i: (i, 0)),
                  pl.BlockSpec((1, d), lambda i: (0, 0)),
                  pl.BlockSpec((1, 1, d), grp),
                  pl.BlockSpec((1, 1, d), grp),
                  pl.BlockSpec((d, n), lambda i: (0, 0))],
        out_specs=pl.BlockSpec((tm, n), lambda i: (i, 0)),
        out_shape=jax.ShapeDtypeStruct((t, n), F32),
        compiler_params=_cparams(("parallel",)),
        name="modnorm_matmul",
    )(x, gain.reshape(1, d), shift, scale, w_bf16)


def _hgrn_constants():
    c = HG_CHUNK
    t = np.arange(c)[:, None]
    u = np.arange(c)[None, :]
    mats = [(u <= t), (u > t)]
    for l in range(HG_MXU_LEVELS):
        m = 1 << l
        r = (t // (2 * m)) * (2 * m) + m - 1
        mats.append((u > np.minimum(t, r)) & (u <= np.maximum(t, r)))
    fwd = np.concatenate(mats, axis=0).astype(np.float32)
    bwd = np.concatenate([mm[::-1, ::-1] for mm in mats], axis=0).astype(np.float32)
    x = np.bitwise_xor(t, u)
    lvl = np.where(x > 0, np.floor(np.log2(np.maximum(x, 1))), HG_LEVELS).astype(np.int32)
    lv_f = np.where(t >= u, lvl, -1).astype(np.int32)
    return fwd, bwd, lv_f, lv_f.T.copy()


def _hgrn_chunk(qs, fpres, vs, lbs, sts, forwards, mcats, lvs):
    c = HG_CHUNK
    n = len(qs)
    fs = [lb + (1.0 - lb) * _sigmoid(fp) for lb, fp in zip(lbs, fpres)]
    kks = [1.0 - f for f in fs]
    logs = [_split2(jnp.log(f)) for f in fs]
    xs = [None] * n
    for fwd in (True, False):
        ids = [i for i in range(n) if forwards[i] == fwd]
        parts = [half for i in ids for half in logs[i]]
        x_all = _dot(mcats[0 if fwd else 1], jnp.concatenate(parts, axis=1))
        for j, i in enumerate(ids):
            xs[i] = x_all[:, 2 * j * c:(2 * j + 1) * c] + x_all[:, (2 * j + 1) * c:(2 * j + 2) * c]
    gs = [x[0:c] for x in xs]
    lv_of = [lvs[0 if f else 1] for f in forwards]
    qbs = [q.astype(BF16) for q in qs]
    kbs = [kk.astype(BF16) for kk in kks]
    vbs = [v.astype(BF16) for v in vs]
    os_ = [_dot_nt((q * jnp.exp(g)).astype(BF16), st.astype(BF16)) for q, g, st in zip(qs, gs, sts)]
    accs = [jnp.where(lv == HG_LEVELS, _dot_nt(qb, kb), 0.0) for lv, qb, kb in zip(lv_of, qbs, kbs)]
    for l in range(HG_LEVELS):
        for i in range(n):
            if l < HG_MXU_LEVELS:
                x = xs[i][(2 + l) * c:(3 + l) * c]
            else:
                m = 1 << l
                ref_rows = [j * 2 * m + (m - 1 if forwards[i] else m) for j in range(c // (2 * m))]
                g_ref = jnp.concatenate(
                    [jnp.broadcast_to(gs[i][r:r + 1, :], (2 * m, c)) for r in ref_rows], axis=0)
                x = -jnp.abs(gs[i] - g_ref)
            e = jnp.exp(x)
            p = _dot_nt((qs[i] * e).astype(BF16), (kks[i] * e).astype(BF16))
            accs[i] = jnp.where(lv_of[i] == l, p, accs[i])
    outs, new_sts = [], []
    for i in range(n):
        edge_row = c - 1 if forwards[i] else 0
        outs.append(os_[i] + _dot(accs[i].astype(BF16), vbs[i]))
        k_end = (kks[i] * jnp.exp(xs[i][c:2 * c])).astype(BF16)
        new_sts.append(sts[i] * jnp.exp(gs[i][edge_row:edge_row + 1, :]) + _dot_tn(vbs[i], k_end))
    return outs, new_sts


def _hgrn_kernel(*refs, has_init):
    if has_init:
        (qf_ref, qb_ref, ff_ref, fb_ref, vf_ref, vb_ref, lbl_ref, mf_ref, mb_ref, lvf_ref, lvb_ref,
         s0f_ref, s0b_ref, of_ref, ob_ref, sf_ref, sb_ref, stf, stb) = refs
    else:
        (qf_ref, qb_ref, ff_ref, fb_ref, vf_ref, vb_ref, lbl_ref, mf_ref, mb_ref, lvf_ref, lvb_ref,
         of_ref, ob_ref, sf_ref, sb_ref, stf, stb) = refs
    c = pl.program_id(1)
    nc = pl.num_programs(1)

    @pl.when(c == 0)
    def _():
        for h in range(HG_HEADS):
            if has_init:
                stf[h] = s0f_ref[0, h].T
                stb[h] = s0b_ref[0, h].T
            else:
                stf[h] = jnp.zeros((HG_DV, HG_DK), F32)
                stb[h] = jnp.zeros((HG_DV, HG_DK), F32)

    rows = [lbl_ref[:, j, :] for j in range(lbl_ref.shape[1])]
    mx = functools.reduce(jnp.maximum, rows)
    ex = [jnp.exp(r - mx) for r in rows]
    lb = ex[0] / functools.reduce(lambda a, b: a + b, ex)

    heads = [slice(h * LANES, (h + 1) * LANES) for h in range(HG_HEADS)]
    nh = HG_HEADS
    outs, sts = _hgrn_chunk(
        [qf_ref[0, :, hs] for hs in heads] + [qb_ref[0, :, hs] for hs in heads],
        [ff_ref[0, :, hs] for hs in heads] + [fb_ref[0, :, hs] for hs in heads],
        [vf_ref[0, :, hs] for hs in heads] + [vb_ref[0, :, hs] for hs in heads],
        [lb[0:1, hs] for hs in heads] + [lb[1:2, hs] for hs in heads],
        [stf[h] for h in range(nh)] + [stb[h] for h in range(nh)],
        [True] * nh + [False] * nh,
        (mf_ref[...], mb_ref[...]), (lvf_ref[...], lvb_ref[...]))
    o_f, o_b, st_f, st_b = outs[:nh], outs[nh:], sts[:nh], sts[nh:]
    for h, hs in enumerate(heads):
        of_ref[0, :, hs] = o_f[h]
        ob_ref[0, :, hs] = o_b[h]
        stf[h] = st_f[h]
        stb[h] = st_b[h]

    @pl.when(c == nc - 1)
    def _():
        for h in range(HG_HEADS):
            sf_ref[0, h] = stf[h].T
            sb_ref[0, h] = stb[h].T


def hgrn_bidir(z, lb_logits, s0f, s0b, batch, seq):
    nc = seq // HG_CHUNK
    z3 = z.reshape(batch, seq, z.shape[1])
    mf, mb, lvf, lvb = _hgrn_constants()
    has_init = s0f is not None
    width = HG_HEADS * LANES
    blk = (1, HG_CHUNK, width)
    fwd = lambda off: pl.BlockSpec(blk, lambda b, c: (b, c, off))
    bwd = lambda off: pl.BlockSpec(blk, lambda b, c: (b, nc - 1 - c, off))
    full = lambda a: pl.BlockSpec(a.shape, lambda b, c: (0,) * a.ndim)
    st_spec = pl.BlockSpec((1, HG_HEADS, HG_DK, HG_DV), lambda b, c: (b, 0, 0, 0))
    consts = [jnp.asarray(mf, BF16), jnp.asarray(mb, BF16), jnp.asarray(lvf), jnp.asarray(lvb)]
    in_specs = [fwd(0), bwd(0), fwd(1), bwd(2), fwd(3), bwd(3),
                pl.BlockSpec(lb_logits.shape, lambda b, c: (0, 0, 0))]
    in_specs += [full(a) for a in consts]
    args = [z3] * 6 + [lb_logits] + consts
    if has_init:
        in_specs += [st_spec, st_spec]
        args += [s0f, s0b]
    o_shape = jax.ShapeDtypeStruct((batch, seq, width), F32)
    s_shape = jax.ShapeDtypeStruct((batch, HG_HEADS, HG_DK, HG_DV), F32)
    o_f, o_b, s_f, s_b = pl.pallas_call(
        functools.partial(_hgrn_kernel, has_init=has_init),
        grid=(batch, nc),
        in_specs=in_specs,
        out_specs=[pl.BlockSpec(blk, lambda b, c: (b, c, 0)),
                   pl.BlockSpec(blk, lambda b, c: (b, nc - 1 - c, 0)),
                   st_spec, st_spec],
        out_shape=[o_shape, o_shape, s_shape, s_shape],
        scratch_shapes=[pltpu.VMEM((HG_HEADS, HG_DV, HG_DK), F32), pltpu.VMEM((HG_HEADS, HG_DV, HG_DK), F32)],
        compiler_params=_cparams(("parallel", "arbitrary")),
        name="hgrn_bidir",
    )(*args)
    t = batch * seq
    return o_f.reshape(t, -1), o_b.reshape(t, -1), s_f, s_b


def _axial_tables(n_tokens, n_rot):
    t = jnp.arange(n_tokens)
    row = (t // GRID_W).astype(F32)
    col = (t % GRID_W).astype(F32)
    n_freq = n_rot // 4
    inv = jnp.power(ROPE_THETA, -jnp.arange(n_freq, dtype=F32) / n_freq)
    ang = jnp.concatenate([row[:, None] * inv, col[:, None] * inv], axis=-1)
    return jnp.cos(ang), jnp.sin(ang)


def _rope_lane_tables(n_tokens, n_rot, head_width, first_rot_lane):
    cos, sin = _axial_tables(n_tokens, n_rot)
    half = n_rot // 2
    c_head = jnp.ones((n_tokens, head_width), F32)
    sa_head = jnp.zeros((n_tokens, head_width), F32)
    sb_head = jnp.zeros((n_tokens, head_width), F32)
    a0, a1, a2 = first_rot_lane, first_rot_lane + half, first_rot_lane + n_rot
    c_head = c_head.at[:, a0:a1].set(cos).at[:, a1:a2].set(cos)
    sa_head = sa_head.at[:, a0:a1].set(-sin)
    sb_head = sb_head.at[:, a1:a2].set(sin)
    reps = LANES // head_width
    tile = lambda a: jnp.tile(a, (1, reps))
    return tile(c_head), tile(sa_head), tile(sb_head)


def _rope(x, c, sa, sb, half):
    return x * c + pltpu.roll(x, LANES - half, 1) * sa + pltpu.roll(x, half, 1) * sb


def _mla_q_kernel(*refs, rope):
    if rope:
        qa_ref, qan_ref, qb_ref, qn_ref, c_ref, sa_ref, sb_ref, o_ref = refs
    else:
        qa_ref, qan_ref, qb_ref, qn_ref, o_ref = refs
    qa = qa_ref[...]
    qn = qa * lax.rsqrt(jnp.mean(qa * qa, axis=-1, keepdims=True) + EPS) * qan_ref[...]
    qfull = _dot(qn.astype(BF16), qb_ref[...])
    outs = []
    for h in range(MLA_HEADS):
        qh = qfull[:, h * LANES:(h + 1) * LANES]
        ms = jnp.sum(qh * qh, axis=-1, keepdims=True) * (1.0 / MLA_QK)
        qh = qh * lax.rsqrt(ms + EPS) * qn_ref[...]
        if rope:
            qh = _rope(qh, c_ref[...], sa_ref[...], sb_ref[...], MLA_ROPE // 2)
        outs.append((qh * MLA_SCALE).astype(BF16))
    o_ref[...] = jnp.concatenate(outs, axis=1)


def mla_queries(z, qa_norm, q_b_pad, q_norm_pad, tables, tokens_per_batch, tm=256):
    t = z.shape[0]
    rope = tables is not None
    row = lambda i: (0, 0)
    in_specs = [pl.BlockSpec((tm, MLA_Q_LORA), lambda i: (i, 2560 // MLA_Q_LORA)),
                pl.BlockSpec((1, MLA_Q_LORA), row),
                pl.BlockSpec(q_b_pad.shape, row),
                pl.BlockSpec((1, LANES), row)]
    args = [z, qa_norm.reshape(1, -1), q_b_pad, q_norm_pad]
    if rope:
        tpb = tokens_per_batch // tm
        in_specs += [pl.BlockSpec((tm, LANES), lambda i: (i % tpb, 0))] * 3
        args += list(tables)
    return pl.pallas_call(
        functools.partial(_mla_q_kernel, rope=rope),
        grid=(t // tm,),
        in_specs=in_specs,
        out_specs=pl.BlockSpec((tm, MLA_HEADS * LANES), lambda i: (i, 0)),
        out_shape=jax.ShapeDtypeStruct((t, MLA_HEADS * LANES), BF16),
        compiler_params=_cparams(("parallel",)),
        name="mla_queries",
    )(*args)


def _mla_kv_kernel(*refs, norm_input, rope):
    if rope:
        kva_ref, kpe_ref, kvan_ref, wk_ref, wv_ref, kn_ref, c_ref, sa_ref, sb_ref, ckv_ref, k_ref, v_ref = refs
    else:
        kva_ref, kpe_ref, kvan_ref, wk_ref, wv_ref, kn_ref, ckv_ref, k_ref, v_ref = refs
    ckv = kva_ref[...]
    if norm_input:
        ckv = ckv * lax.rsqrt(jnp.mean(ckv * ckv, axis=-1, keepdims=True) + EPS) * kvan_ref[...]
    ckv_ref[...] = ckv
    cb = ckv.astype(BF16)
    knope = _dot(cb, wk_ref[...])
    v_ref[...] = _dot(cb, wv_ref[...]).astype(BF16)
    kpe = pltpu.roll(kpe_ref[...], MLA_NOPE, 1)
    outs = []
    for h in range(MLA_HEADS):
        kh = knope[:, h * LANES:(h + 1) * LANES] + kpe
        ms = jnp.sum(kh * kh, axis=-1, keepdims=True) * (1.0 / MLA_QK)
        kh = kh * lax.rsqrt(ms + EPS) * kn_ref[...]
        if rope:
            kh = _rope(kh, c_ref[...], sa_ref[...], sb_ref[...], MLA_ROPE // 2)
        outs.append(kh.astype(BF16))
    k_ref[...] = jnp.concatenate(outs, axis=1)


def mla_keys_values(kva_src, kva_col, kpe_src, kpe_col, kva_norm, wk_pad, wv, k_norm_pad, tables,
                    tokens_per_batch, norm_input, tm=256):
    t = kva_src.shape[0]
    rope = tables is not None
    row = lambda i: (0, 0)
    in_specs = [pl.BlockSpec((tm, LANES), lambda i: (i, kva_col)),
                pl.BlockSpec((tm, LANES), lambda i: (i, kpe_col)),
                pl.BlockSpec((1, LANES), row),
                pl.BlockSpec(wk_pad.shape, row),
                pl.BlockSpec(wv.shape, row),
                pl.BlockSpec((1, LANES), row)]
    args = [kva_src, kpe_src, kva_norm.reshape(1, -1), wk_pad, wv, k_norm_pad]
    if rope:
        tpb = tokens_per_batch // tm
        in_specs += [pl.BlockSpec((tm, LANES), lambda i: (i % tpb, 0))] * 3
        args += list(tables)
    return pl.pallas_call(
        functools.partial(_mla_kv_kernel, norm_input=norm_input, rope=rope),
        grid=(t // tm,),
        in_specs=in_specs,
        out_specs=[pl.BlockSpec((tm, LANES), lambda i: (i, 0)),
                   pl.BlockSpec((tm, MLA_HEADS * LANES), lambda i: (i, 0)),
                   pl.BlockSpec((tm, MLA_HEADS * MLA_V), lambda i: (i, 0))],
        out_shape=[jax.ShapeDtypeStruct((t, LANES), F32),
                   jax.ShapeDtypeStruct((t, MLA_HEADS * LANES), BF16),
                   jax.ShapeDtypeStruct((t, MLA_HEADS * MLA_V), BF16)],
        compiler_params=_cparams(("parallel",)),
        name="mla_keys_values",
    )(*args)


def _mla_attn_kernel(q_ref, k_ref, vt_ref, o_ref):
    heads = range(2)
    ss = [_dot_nt(k_ref[0][:, j * LANES:(j + 1) * LANES], q_ref[0][:, j * LANES:(j + 1) * LANES]) for j in heads]
    ms = [jnp.max(s, axis=0, keepdims=True) for s in ss]
    ps = [jnp.exp(s - m) for s, m in zip(ss, ms)]
    ls = [jnp.sum(p, axis=0, keepdims=True) for p in ps]
    ots = [_dot(vt_ref[0][j * MLA_V:(j + 1) * MLA_V, :], ps[j].astype(BF16)) / ls[j] for j in heads]
    o_ref[0] = jnp.concatenate(ots, axis=0).T


def mla_attention(q, k, v, batch, n_q, n_k, tq=256):
    q3 = q.reshape(batch, n_q, -1)
    k3 = k.reshape(batch, n_k, -1)
    vt = jnp.swapaxes(v.reshape(batch, n_k, -1), 1, 2)
    out = pl.pallas_call(
        _mla_attn_kernel,
        grid=(batch, MLA_HEADS // 2, n_q // tq),
        in_specs=[pl.BlockSpec((1, tq, 2 * LANES), lambda b, j, i: (b, i, j)),
                  pl.BlockSpec((1, n_k, 2 * LANES), lambda b, j, i: (b, 0, j)),
                  pl.BlockSpec((1, 2 * MLA_V, n_k), lambda b, j, i: (b, j, 0))],
        out_specs=pl.BlockSpec((1, tq, 2 * MLA_V), lambda b, j, i: (b, i, j)),
        out_shape=jax.ShapeDtypeStruct((batch, n_q, MLA_HEADS * MLA_V), F32),
        compiler_params=_cparams(("parallel", "parallel", "arbitrary")),
        name="mla_attention",
    )(q3, k3, vt)
    return out.reshape(batch * n_q, -1)


def _out0_kernel(of_ref, ob_ref, ag_ref, hgn_ref, om_ref, w_ref, x_ref, g_ref, o_ref):
    o = of_ref[...] + ob_ref[...]
    ag = ag_ref[...]
    parts = []
    for h in range(HG_HEADS):
        oh = o[:, h * HG_DV:(h + 1) * HG_DV]
        oh = oh * lax.rsqrt(jnp.mean(oh * oh, axis=-1, keepdims=True) + EPS) * hgn_ref[...]
        gh = ag[:, h * HG_DV:(h + 1) * HG_DV]
        parts.append((oh * (gh * _sigmoid(gh))).astype(BF16))
    oa = jnp.concatenate(parts, axis=1)
    n_a = HG_HEADS * HG_DV
    mix = _dot(oa, w_ref[0:n_a, :]) + _dot(om_ref[...].astype(BF16), w_ref[n_a:, :])
    o_ref[...] = x_ref[...] + g_ref[0] * mix


def out_proj_layer0(o_f, o_b, z, hg_norm, o_mla, w_out_bf16, x, gate, tokens_per_group, tm=512):
    t, d = x.shape
    n_a = HG_HEADS * HG_DV
    tiles_per_group = tokens_per_group // tm
    tile = lambda w: pl.BlockSpec((tm, w), lambda i: (i, 0))
    return pl.pallas_call(
        _out0_kernel,
        grid=(t // tm,),
        in_specs=[tile(n_a), tile(n_a),
                  pl.BlockSpec((tm, n_a), lambda i: (i, 2048 // n_a)),
                  pl.BlockSpec((1, HG_DV), lambda i: (0, 0)),
                  tile(o_mla.shape[1]),
                  pl.BlockSpec(w_out_bf16.shape, lambda i: (0, 0)),
                  tile(d),
                  pl.BlockSpec((1, 1, d), lambda i: (i // tiles_per_group, 0, 0))],
        out_specs=tile(d),
        out_shape=jax.ShapeDtypeStruct((t, d), F32),
        compiler_params=_cparams(("parallel",)),
        name="out_proj_layer0",
    )(o_f, o_b, z, hg_norm.reshape(1, -1), o_mla, w_out_bf16, x, gate)


def _out1_kernel(a_ref, w_ref, x_ref, g_ref, o_ref):
    o_ref[...] = x_ref[...] + g_ref[0] * _dot(a_ref[...].astype(BF16), w_ref[...])


def out_proj_layer1(a, w_out_bf16, x, gate, tokens_per_group, tm=512):
    t, d = x.shape
    tiles_per_group = tokens_per_group // tm
    tile = lambda w: pl.BlockSpec((tm, w), lambda i: (i, 0))
    return pl.pallas_call(
        _out1_kernel,
        grid=(t // tm,),
        in_specs=[tile(a.shape[1]),
                  pl.BlockSpec(w_out_bf16.shape, lambda i: (0, 0)),
                  tile(d),
                  pl.BlockSpec((1, 1, d), lambda i: (i // tiles_per_group, 0, 0))],
        out_specs=tile(d),
        out_shape=jax.ShapeDtypeStruct((t, d), F32),
        compiler_params=_cparams(("parallel",)),
        name="out_proj_layer1",
    )(a, w_out_bf16, x, gate)


def _head_rms(x, gain2):
    sq = x * x
    lane = lax.broadcasted_iota(jnp.int32, x.shape, 1)
    first = lane < SWA_HD
    lo = jnp.sum(jnp.where(first, sq, 0.0), axis=-1, keepdims=True)
    hi = jnp.sum(jnp.where(first, 0.0, sq), axis=-1, keepdims=True)
    ms = jnp.where(first, lo, hi) * (1.0 / SWA_HD)
    return x * lax.rsqrt(ms + EPS) * gain2


def _swa_prep_kernel(*refs, rope):
    if rope:
        zq_ref, zk_ref, zv_ref, qn_ref, kn_ref, c_ref, sa_ref, sb_ref, q_ref, kc_ref, k_ref, v_ref = refs
    else:
        zq_ref, zk_ref, zv_ref, qn_ref, kn_ref, q_ref, kc_ref, k_ref, v_ref = refs
    half = SWA_HD // 2

    def rot(x):
        return _rope(x, c_ref[...], sa_ref[...], sb_ref[...], half) if rope else x

    zq = zq_ref[...]
    qs = []
    for p in range(zq.shape[1] // LANES):
        x = _head_rms(zq[:, p * LANES:(p + 1) * LANES], qn_ref[...])
        qs.append((rot(x) * SWA_SCALE).astype(BF16))
    q_ref[...] = jnp.concatenate(qs, axis=1)
    zk = zk_ref[...]
    kn, kr = [], []
    for p in range(zk.shape[1] // LANES):
        x = _head_rms(zk[:, p * LANES:(p + 1) * LANES], kn_ref[...])
        kn.append(x)
        kr.append(rot(x).astype(BF16))
    kc_ref[...] = jnp.concatenate(kn, axis=1)
    k_ref[...] = jnp.concatenate(kr, axis=1)
    v_ref[...] = zv_ref[...].astype(BF16)


def swa_prep(z, q_norm2, k_norm2, tables, tokens_per_batch, tm=256):
    t = z.shape[0]
    nq = SWA_HEADS * SWA_HD
    nkv = SWA_KV_HEADS * SWA_HD
    rope = tables is not None
    row = lambda i: (0, 0)
    in_specs = [pl.BlockSpec((tm, nq), lambda i: (i, 0)),
                pl.BlockSpec((tm, nkv), lambda i: (i, nq // nkv)),
                pl.BlockSpec((tm, nkv), lambda i: (i, nq // nkv + 1)),
                pl.BlockSpec((1, LANES), row),
                pl.BlockSpec((1, LANES), row)]
    args = [z, z, z, q_norm2, k_norm2]
    if rope:
        tpb = tokens_per_batch // tm
        in_specs += [pl.BlockSpec((tm, LANES), lambda i: (i % tpb, 0))] * 3
        args += list(tables)
    tile = lambda w: pl.BlockSpec((tm, w), lambda i: (i, 0))
    return pl.pallas_call(
        functools.partial(_swa_prep_kernel, rope=rope),
        grid=(t // tm,),
        in_specs=in_specs,
        out_specs=[tile(nq), tile(nkv), tile(nkv), tile(nkv)],
        out_shape=[jax.ShapeDtypeStruct((t, nq), BF16),
                   jax.ShapeDtypeStruct((t, nkv), F32),
                   jax.ShapeDtypeStruct((t, nkv), BF16),
                   jax.ShapeDtypeStruct((t, nkv), BF16)],
        compiler_params=_cparams(("parallel",)),
        name="swa_prep",
    )(*args)


def _sink_attention_t(q8, k_all, vt_all, bias, sink_ref, pair):
    tq = q8.shape[0]
    n_q = SWA_GROUP * tq
    lane = lax.broadcasted_iota(jnp.int32, (1, n_q), 1)
    kv = range(2)
    heads = [[n * SWA_GROUP + g for g in range(SWA_GROUP)] for n in kv]
    q4s = [jnp.concatenate([q8[:, hq * SWA_HD:(hq + 1) * SWA_HD] for hq in heads[n]], axis=0) for n in kv]
    sks = []
    for n in kv:
        sk = jnp.zeros((1, n_q), F32)
        for g, hq in enumerate(heads[n]):
            sk = jnp.where(lane // tq == g, sink_ref[pair * 2 * SWA_GROUP + hq], sk)
        sks.append(sk)
    ss = [_dot_nt(k_all[:, n * SWA_HD:(n + 1) * SWA_HD], q4s[n]) for n in kv]
    if bias is not None:
        ss = [s + bias for s in ss]
    ms = [jnp.maximum(jnp.max(ss[n], axis=0, keepdims=True), sks[n]) for n in kv]
    ps = [jnp.exp(ss[n] - ms[n]) for n in kv]
    ls = [jnp.sum(ps[n], axis=0, keepdims=True) + jnp.exp(sks[n] - ms[n]) for n in kv]
    ots = [_dot(vt_all[n * SWA_HD:(n + 1) * SWA_HD, :], ps[n].astype(BF16)) / ls[n] for n in kv]
    pairs = []
    for n in kv:
        for g in range(0, SWA_GROUP, 2):
            two = jnp.concatenate([ots[n][:, g * tq:(g + 1) * tq], ots[n][:, (g + 1) * tq:(g + 2) * tq]], axis=0)
            pairs.append(two.T)
    return jnp.concatenate(pairs, axis=1)


def _swa_dense_kernel(sink_ref, q_ref, k_ref, vt_ref, o_ref):
    o_ref[0] = _sink_attention_t(q_ref[0], k_ref[0], vt_ref[0], None, sink_ref, pl.program_id(1))


def swa_dense(q, k, v, sink, batch, seq):
    nq = SWA_HEADS * SWA_HD
    q3 = q.reshape(batch, seq, nq)
    k3 = k.reshape(batch, seq, -1)
    vt = jnp.swapaxes(v.reshape(batch, seq, -1), 1, 2)
    out = pl.pallas_call(
        _swa_dense_kernel,
        grid=(batch, 2),
        in_specs=[pl.BlockSpec(memory_space=pltpu.SMEM),
                  pl.BlockSpec((1, seq, nq // 2), lambda b, j: (b, 0, j)),
                  pl.BlockSpec((1, seq, LANES), lambda b, j: (b, 0, j)),
                  pl.BlockSpec((1, LANES, seq), lambda b, j: (b, j, 0))],
        out_specs=pl.BlockSpec((1, seq, nq // 2), lambda b, j: (b, 0, j)),
        out_shape=jax.ShapeDtypeStruct((batch, seq, nq), F32),
        compiler_params=_cparams(("parallel", "parallel")),
        name="swa_dense",
    )(sink, q3, k3, vt)
    return out.reshape(batch * seq, nq)


def _swa_window_kernel(sink_ref, q_ref, kc_ref, kp_ref, k0_ref, kn_ref, vc_ref, vp_ref, v0_ref, vn_ref, o_ref):
    i = pl.program_id(2)
    nb = pl.num_programs(2)
    w = SWA_WINDOW
    n_ctx = kc_ref.shape[1]
    k_all = jnp.concatenate([kc_ref[0], kp_ref[0], k0_ref[0], kn_ref[0]], axis=0)
    vt_all = jnp.concatenate([vc_ref[0], vp_ref[0], v0_ref[0], vn_ref[0]], axis=1)
    shape = (n_ctx + 3 * w, SWA_GROUP * w)
    key = lax.broadcasted_iota(jnp.int32, shape, 0) - n_ctx
    r = lax.broadcasted_iota(jnp.int32, shape, 1) % w
    prev_bias = jnp.where(i > 0, 0.0, NEG_BIG)
    next_bias = jnp.where(i < nb - 1, 0.0, NEG_BIG)
    bias = jnp.where(
        key < w,
        jnp.where(key < 0, 0.0, jnp.where(key >= r, prev_bias, NEG_BIG)),
        jnp.where(key < 2 * w, 0.0, jnp.where(key - 2 * w <= r, next_bias, NEG_BIG)))
    o_ref[0] = _sink_attention_t(q_ref[0], k_all, vt_all, bias, sink_ref, pl.program_id(1))


def swa_window(q, k, v, k_ctx, v_ctx, sink, batch, seq, n_ctx):
    nq = SWA_HEADS * SWA_HD
    w = SWA_WINDOW
    nb = seq // w
    q3 = q.reshape(batch, seq, nq)
    k3 = k.reshape(batch, seq, -1)
    vt = jnp.swapaxes(v.reshape(batch, seq, -1), 1, 2)
    vt_ctx = jnp.swapaxes(v_ctx, 1, 2)
    ctx = pl.BlockSpec((1, n_ctx, LANES), lambda b, j, i: (b, 0, j))
    prv = pl.BlockSpec((1, w, LANES), lambda b, j, i: (b, jnp.maximum(i - 1, 0), j))
    cur = pl.BlockSpec((1, w, LANES), lambda b, j, i: (b, i, j))
    nxt = pl.BlockSpec((1, w, LANES), lambda b, j, i: (b, jnp.minimum(i + 1, nb - 1), j))
    ctx_t = pl.BlockSpec((1, LANES, n_ctx), lambda b, j, i: (b, j, 0))
    prv_t = pl.BlockSpec((1, LANES, w), lambda b, j, i: (b, j, jnp.maximum(i - 1, 0)))
    cur_t = pl.BlockSpec((1, LANES, w), lambda b, j, i: (b, j, i))
    nxt_t = pl.BlockSpec((1, LANES, w), lambda b, j, i: (b, j, jnp.minimum(i + 1, nb - 1)))
    out = pl.pallas_call(
        _swa_window_kernel,
        grid=(batch, 2, nb),
        in_specs=[pl.BlockSpec(memory_space=pltpu.SMEM),
                  pl.BlockSpec((1, w, nq // 2), lambda b, j, i: (b, i, j)),
                  ctx, prv, cur, nxt, ctx_t, prv_t, cur_t, nxt_t],
        out_specs=pl.BlockSpec((1, w, nq // 2), lambda b, j, i: (b, i, j)),
        out_shape=jax.ShapeDtypeStruct((batch, seq, nq), F32),
        compiler_params=_cparams(("parallel", "parallel", "arbitrary")),
        name="swa_window",
    )(sink, q3, k_ctx, k3, k3, k3, vt_ctx, vt, vt, vt)
    return out.reshape(batch * seq, nq)


def _router_kernel(x_ref, g_ref, sh_ref, sc_ref, rw_ref, rb_ref, tri_ref, h_ref, idx_ref, gate_ref, rank_ref,
                   cnt_ref, cnt, *, tiles_per_part):
    @pl.when(pl.program_id(0) % tiles_per_part == 0)
    def _():
        cnt[...] = jnp.zeros_like(cnt)

    h = _modnorm(x_ref[...], g_ref[...], sh_ref[0], sc_ref[0])
    bits = lax.bitcast_convert_type(h.astype(BF16).astype(F32), jnp.uint32)
    half = h.shape[1] // 2
    h_ref[...] = (bits[:, :half] >> 16) | (bits[:, half:] & jnp.uint32(0xFFFF0000))
    wh, wl = _split2(rw_ref[...])
    hh, hl = _split2(h)
    logits = _dot_nt(wh, hh) + _dot_nt(wh, hl) + _dot_nt(wl, hh) + rb_ref[...]
    row = lax.broadcasted_iota(jnp.int32, logits.shape, 0)
    work = logits
    vals, idxs = [], []
    for _ in range(TOP_K):
        m = jnp.max(work, axis=0, keepdims=True)
        ik = jnp.min(jnp.where(work == m, row, LANES), axis=0, keepdims=True)
        vals.append(m)
        idxs.append(ik)
        work = jnp.where(row == ik, 2.0 * NEG_BIG, work)
    es = [jnp.exp(v - vals[0]) for v in vals]
    denom = es[0] + es[1] + es[2] + es[3]
    chosen = jnp.zeros(logits.shape, F32)
    for k in range(TOP_K):
        chosen = jnp.where(row == idxs[k], 1.0, chosen)
    chosen_b = chosen.astype(BF16)
    seen = cnt[...]
    before = _dot(chosen_b, tri_ref[...]) + jnp.concatenate([seen] * (logits.shape[1] // LANES), axis=1)
    out_row = lax.broadcasted_iota(jnp.int32, idx_ref.shape, 0)
    idx_out = jnp.zeros(idx_ref.shape, jnp.int32)
    gate_out = jnp.zeros(idx_ref.shape, F32)
    rank_out = jnp.zeros(idx_ref.shape, jnp.int32)
    for k in range(TOP_K):
        rk = jnp.sum(jnp.where(row == idxs[k], before, 0.0), axis=0, keepdims=True)
        idx_out = jnp.where(out_row == k, idxs[k], idx_out)
        gate_out = jnp.where(out_row == k, es[k] / denom, gate_out)
        rank_out = jnp.where(out_row == k, rk.astype(jnp.int32), rank_out)
    idx_ref[...] = idx_out
    gate_ref[...] = gate_out
    rank_ref[...] = rank_out
    cnt[...] = seen + _dot(chosen_b, jnp.ones((logits.shape[1], LANES), BF16))
    cnt_ref[0] = cnt[...]


def moe_router(x, gain, shift, scale, rw_t, rb_col, tokens_per_group, tm=256):
    t, d = x.shape
    tiles_per_group = tokens_per_group // tm
    tiles_per_part = MOE_PART // tm
    grp = lambda i: (i // tiles_per_group, 0, 0)
    tile = lambda w: pl.BlockSpec((tm, w), lambda i: (i, 0))
    slab = pl.BlockSpec((8, tm), lambda i: (0, i))
    tri = jnp.asarray(np.triu(np.ones((tm, tm), np.float32), 1), BF16)
    hp, idx, gates, rank, cnt = pl.pallas_call(
        functools.partial(_router_kernel, tiles_per_part=tiles_per_part),
        grid=(t // tm,),
        in_specs=[tile(d),
                  pl.BlockSpec((1, d), lambda i: (0, 0)),
                  pl.BlockSpec((1, 1, d), grp),
                  pl.BlockSpec((1, 1, d), grp),
                  pl.BlockSpec((LANES, d), lambda i: (0, 0)),
                  pl.BlockSpec((LANES, 1), lambda i: (0, 0)),
                  pl.BlockSpec((tm, tm), lambda i: (0, 0))],
        out_specs=[tile(d // 2), slab, slab, slab,
                   pl.BlockSpec((1, LANES, LANES), lambda i: (i // tiles_per_part, 0, 0))],
        out_shape=[jax.ShapeDtypeStruct((t, d // 2), jnp.uint32),
                   jax.ShapeDtypeStruct((8, t), jnp.int32),
                   jax.ShapeDtypeStruct((8, t), F32),
                   jax.ShapeDtypeStruct((8, t), jnp.int32),
                   jax.ShapeDtypeStruct((t // MOE_PART, LANES, LANES), F32)],
        scratch_shapes=[pltpu.VMEM((LANES, LANES), F32)],
        compiler_params=_cparams(("arbitrary",)),
        name="moe_router",
    )(x, gain.reshape(1, d), shift, scale, rw_t, rb_col, tri)
    per_token = lambda a: a[:TOP_K].T
    return hp, per_token(idx), per_token(gates), per_token(rank), cnt[:, :N_EXPERTS, 0].astype(jnp.int32)


FFN_COLS = 512
FFN_PIECES = 2 * D_FF // FFN_COLS + D_MODEL // FFN_COLS
MOE_PART = 4096
ADD_BATCH = 8


def _unpack_rows(words):
    lo = lax.bitcast_convert_type(words << 16, F32)
    hi = lax.bitcast_convert_type(words & jnp.uint32(0xFFFF0000), F32)
    return jnp.concatenate([lo, hi], axis=1).astype(BF16)


def _ffn_block(x_ref, y_ref, wgu_ref, bgu_ref, wd_ref, bd_ref, between):
    xb = _unpack_rows(x_ref[...])
    acts = []
    for c in range(D_FF // FFN_COLS):
        lo, hi = c * FFN_COLS, (c + 1) * FFN_COLS
        between(2 * c)
        gate = _dot(xb, wgu_ref[0, :, lo:hi]) + bgu_ref[0][:, lo:hi]
        between(2 * c + 1)
        up = _dot(xb, wgu_ref[0, :, D_FF + lo:D_FF + hi]) + bgu_ref[0][:, D_FF + lo:D_FF + hi]
        gate = jnp.minimum(gate, SWIGLU_LIMIT)
        up = jnp.clip(up, -SWIGLU_LIMIT, SWIGLU_LIMIT)
        acts.append((gate * _sigmoid(SWIGLU_ALPHA * gate) * (up + 1.0)).astype(BF16))
    act = jnp.concatenate(acts, axis=1)
    for n in range(D_MODEL // FFN_COLS):
        lo, hi = n * FFN_COLS, (n + 1) * FFN_COLS
        between(2 * D_FF // FFN_COLS + n)
        y_ref[:, lo:hi] = _dot(act, wd_ref[0, :, lo:hi]) + bd_ref[0][:, lo:hi]


def _expert_kernel(nb_ref, bs_ref, loc_ref, gate_ref, hp_hbm, wgu_ref, bgu_ref, wd_ref, bd_ref, out_hbm,
                   hbuf, acc, xbuf, ybuf, sem):
    part = pl.program_id(0)
    e = pl.program_id(1)
    idx = part * N_EXPERTS + e
    nb = nb_ref[idx]
    b0 = bs_ref[idx]
    shares = np.array_split(np.arange(MOE_ROWS), FFN_PIECES)

    def fetch_rows(blk):
        base = blk * MOE_ROWS
        dst = xbuf.at[blk % 2]

        def emit(k):
            for r in shares[k]:
                r = int(r)
                dst[r:r + 1, :] = hbuf[pl.ds(loc_ref[base + r], 1), :]
        return emit

    def add_rows(blk):
        base = blk * MOE_ROWS
        src = ybuf.at[blk % 2]

        def emit(k):
            for batch in np.array_split(shares[k], max(1, len(shares[k]) // ADD_BATCH)):
                rows = [loc_ref[base + int(r)] for r in batch]
                new = [acc[pl.ds(row, 1), :] + gate_ref[base + int(r)] * src[int(r):int(r) + 1, :]
                       for row, r in zip(rows, batch)]
                for row, val in zip(rows, new):
                    acc[pl.ds(row, 1), :] = val
        return emit

    def emit_all(f):
        for k in range(FFN_PIECES):
            f(k)

    @pl.when(e == 0)
    def _():
        load = pltpu.make_async_copy(hp_hbm.at[pl.ds(part * MOE_PART, MOE_PART), :],
                                     hbuf.at[pl.ds(0, MOE_PART), :], sem.at[0])
        load.start()
        hbuf[MOE_PART:, :] = jnp.zeros((hbuf.shape[0] - MOE_PART, hbuf.shape[1]), hbuf.dtype)
        acc[...] = jnp.zeros_like(acc)
        ybuf[...] = jnp.zeros_like(ybuf)
        load.wait()
        emit_all(fetch_rows(b0))

    def block(i, carry):
        fetch, add = fetch_rows(i + 1), add_rows(i - 1)

        def between(k):
            fetch(k)
            add(k)
        _ffn_block(xbuf.at[i % 2], ybuf.at[i % 2], wgu_ref, bgu_ref, wd_ref, bd_ref, between)
        return carry

    lax.fori_loop(b0, b0 + nb, block, 0)

    @pl.when(e == N_EXPERTS - 1)
    def _():
        emit_all(add_rows(b0 + nb - 1))
        store = pltpu.make_async_copy(acc.at[pl.ds(0, MOE_PART), :],
                                      out_hbm.at[pl.ds(part * MOE_PART, MOE_PART), :], sem.at[0])
        store.start()
        store.wait()


def moe_experts(hp, n_blk, blk_start, row_loc, row_gate, w_gu, b_gu, w_down, b_down):
    t = hp.shape[0]
    d = D_MODEL
    per_expert = lambda p, e, nb, bs, sl: (e, 0, 0)
    grid_spec = pltpu.PrefetchScalarGridSpec(
        num_scalar_prefetch=3,
        grid=(t // MOE_PART, N_EXPERTS),
        in_specs=[pl.BlockSpec(memory_space=pltpu.SMEM),
                  pl.BlockSpec(memory_space=pl.ANY),
                  pl.BlockSpec((1, d, 2 * D_FF), per_expert),
                  pl.BlockSpec((1, 1, 2 * D_FF), per_expert),
                  pl.BlockSpec((1, D_FF, d), per_expert),
                  pl.BlockSpec((1, 1, d), per_expert)],
        out_specs=pl.BlockSpec(memory_space=pl.ANY),
        scratch_shapes=[pltpu.VMEM((MOE_PART + 8, d // 2), jnp.uint32),
                        pltpu.VMEM((MOE_PART + 8, d), F32),
                        pltpu.VMEM((2, MOE_ROWS, d // 2), jnp.uint32),
                        pltpu.VMEM((2, MOE_ROWS, d), F32),
                        pltpu.SemaphoreType.DMA((1,))],
    )
    return pl.pallas_call(
        _expert_kernel,
        grid_spec=grid_spec,
        out_shape=jax.ShapeDtypeStruct((t, d), F32),
        compiler_params=pltpu.CompilerParams(dimension_semantics=("arbitrary", "arbitrary"),
                                             vmem_limit_bytes=EXPERT_VMEM_LIMIT),
        name="moe_experts",
    )(n_blk, blk_start, row_loc, row_gate, hp, w_gu, b_gu.reshape(N_EXPERTS, 1, -1), w_down,
      b_down.reshape(N_EXPERTS, 1, -1))


def _residual_kernel(y_ref, x_ref, g_ref, o_ref):
    o_ref[...] = x_ref[...] + g_ref[0] * y_ref[...]


def gated_residual(y, x, gate_mod, tokens_per_group, tm=512):
    t, d = x.shape
    tiles_per_group = tokens_per_group // tm
    tile = pl.BlockSpec((tm, d), lambda i: (i, 0))
    return pl.pallas_call(
        _residual_kernel,
        grid=(t // tm,),
        in_specs=[tile, tile, pl.BlockSpec((1, 1, d), lambda i: (i // tiles_per_group, 0, 0))],
        out_specs=tile,
        out_shape=jax.ShapeDtypeStruct((t, d), F32),
        compiler_params=_cparams(("parallel",)),
        name="moe_residual",
    )(y, x, gate_mod)


def _assignment_tables(top_idx, rank, gates, counts):
    t = top_idx.shape[0]
    n_parts = t // MOE_PART
    n_assign = t * TOP_K
    part_blocks = MOE_PART * TOP_K // MOE_ROWS + N_EXPERTS + 1
    n_rows = (1 + n_parts * part_blocks) * MOE_ROWS
    padded = (counts + MOE_ROWS - 1) // MOE_ROWS * MOE_ROWS
    part_first = (1 + jnp.arange(n_parts, dtype=jnp.int32) * part_blocks) * MOE_ROWS
    row_start = part_first[:, None] + jnp.cumsum(padded, axis=1) - padded
    experts = jnp.arange(N_EXPERTS, dtype=jnp.int32)
    by_part = top_idx.reshape(n_parts, MOE_PART, TOP_K)
    start_of = jnp.sum(jnp.where(by_part[..., None] == experts, row_start[:, None, None, :], 0), axis=-1)
    dest = (start_of.reshape(t, TOP_K) + rank).reshape(-1).astype(jnp.int32)
    local = (jnp.arange(n_assign, dtype=jnp.int32) // TOP_K) % MOE_PART
    vals = jnp.stack([local, lax.bitcast_convert_type(gates.reshape(-1), jnp.int32)], axis=1)
    init = jnp.broadcast_to(jnp.array([MOE_PART, 0], jnp.int32), (n_rows, 2))
    table = init.at[dest].set(vals)
    row_loc = table[:, 0]
    row_gate = lax.bitcast_convert_type(table[:, 1], F32)
    return row_loc, row_gate, (padded // MOE_ROWS).reshape(-1).astype(jnp.int32), \
        (row_start // MOE_ROWS).reshape(-1).astype(jnp.int32)


def moe_layer(x, p, shift, scale, gate_mod, tokens_per_group):
    hp, idx, gates, rank, counts = moe_router(x, p['norm2'], shift, scale, p['rw_t'], p['rb_col'], tokens_per_group)
    row_loc, row_gate, n_blk, blk_start = _assignment_tables(idx, rank, gates, counts)
    y = moe_experts(hp, n_blk, blk_start, row_loc, row_gate, p['w_gu_bf'], p['b_gu'], p['w_down_bf'], p['b_down'])
    return gated_residual(y, x, gate_mod, tokens_per_group)


def _pad_lanes(a, width):
    return jnp.pad(a, [(0, 0)] * (a.ndim - 1) + [(0, width - a.shape[-1])])


def _prep_common(p):
    p['rw_t'] = _pad_lanes(p['router_w'], LANES).T
    p['rb_col'] = jnp.concatenate(
        [p['router_b'].astype(F32), jnp.full((LANES - N_EXPERTS,), NEG_BIG, F32)]).reshape(LANES, 1)
    p['w_out'] = p['w_out'].astype(BF16)
    p['w_gu_bf'] = p['w_gu'].astype(BF16)
    p['w_down_bf'] = p['w_down'].astype(BF16)
    return p


def _prep_layer0(p):
    p = _prep_common(dict(p))
    p['w_in'] = _pad_lanes(p['w_in'], AB_IN_PAD).astype(BF16)
    q_b = p['q_b'].reshape(MLA_Q_LORA, MLA_HEADS, MLA_QK)
    p['q_b_pad'] = _pad_lanes(q_b, LANES).reshape(MLA_Q_LORA, MLA_HEADS * LANES).astype(BF16)
    kv_b = p['kv_b'].reshape(MLA_KV_LORA, MLA_HEADS, MLA_NOPE + MLA_V)
    p['wk_pad'] = _pad_lanes(kv_b[:, :, :MLA_NOPE], LANES).reshape(MLA_KV_LORA, MLA_HEADS * LANES).astype(BF16)
    p['wv'] = kv_b[:, :, MLA_NOPE:].reshape(MLA_KV_LORA, MLA_HEADS * MLA_V).astype(BF16)
    p['q_norm_pad'] = _pad_lanes(p['q_norm'].reshape(1, -1), LANES)
    p['k_norm_pad'] = _pad_lanes(p['k_norm'].reshape(1, -1), LANES)
    return p


def _prep_layer1(p):
    p = _prep_common(dict(p))
    p['w_in'] = p['w_in'].astype(BF16)
    p['q_norm2'] = jnp.tile(p['q_norm'].reshape(1, -1), (1, LANES // SWA_HD))
    p['k_norm2'] = jnp.tile(p['k_norm'].reshape(1, -1), (1, LANES // SWA_HD))
    return p


def _group_forward(x3, mods, p0, p1, lb_logits, caches, latent):
    batch, seq, d = x3.shape
    t = batch * seq
    x = x3.reshape(t, d)
    tpg = seq if latent else t
    sh1, sc1, g1, sh2, sc2, g2 = mods[0]

    z = modnorm_matmul(x, p0['norm1'], sh1, sc1, p0['w_in'], tpg)
    tab_b = _rope_lane_tables(seq, MLA_ROPE, LANES, MLA_NOPE) if latent else None
    s0f, s0b = (caches['hg_f'], caches['hg_b']) if latent else (None, None)
    o_f, o_b, s_f, s_b = hgrn_bidir(z, lb_logits, s0f, s0b, batch, seq)
    q = mla_queries(z, p0['qa_norm'], p0['q_b_pad'], p0['q_norm_pad'], tab_b, seq)
    ckv, k, v = mla_keys_values(z, 2816 // LANES, z, 2944 // LANES, p0['kva_norm'], p0['wk_pad'], p0['wv'],
                                p0['k_norm_pad'], tab_b, seq, norm_input=True)
    n_k = seq
    if latent:
        n_ctx = caches['ckv'].shape[1]
        ckv_c = caches['ckv'].reshape(batch * n_ctx, MLA_KV_LORA)
        kpe_c = _pad_lanes(caches['kpe'].reshape(batch * n_ctx, MLA_ROPE), LANES)
        _, k_c, v_c = mla_keys_values(ckv_c, 0, kpe_c, 0, p0['kva_norm'], p0['wk_pad'], p0['wv'],
                                      p0['k_norm_pad'], None, n_ctx, norm_input=False)
        cat = lambda a, b: jnp.concatenate([a.reshape(batch, n_ctx, -1), b.reshape(batch, seq, -1)],
                                           axis=1).reshape(batch * (n_ctx + seq), -1)
        k, v = cat(k_c, k), cat(v_c, v)
        n_k = n_ctx + seq
    o_mla = mla_attention(q, k, v, batch, seq, n_k)
    x = out_proj_layer0(o_f, o_b, z, p0['hg_out_norm'], o_mla, p0['w_out'], x, g1, tpg)
    x = moe_layer(x, p0, sh2, sc2, g2, tpg)
    state0 = (s_f, s_b, ckv.reshape(batch, seq, MLA_KV_LORA), z[:, 2944:2944 + MLA_ROPE].reshape(batch, seq, MLA_ROPE))

    sh1, sc1, g1, sh2, sc2, g2 = mods[1]
    z = modnorm_matmul(x, p1['norm1'], sh1, sc1, p1['w_in'], tpg)
    tab_c = _rope_lane_tables(seq, SWA_HD, SWA_HD, 0) if latent else None
    q, k_cache, k, v = swa_prep(z, p1['q_norm2'], p1['k_norm2'], tab_c, seq)
    sink = p1['sink'].astype(F32)
    if latent:
        n_ctx = caches['k1'].shape[1]
        k_c = caches['k1'].reshape(batch, n_ctx, -1).astype(BF16)
        v_c = caches['v1'].reshape(batch, n_ctx, -1).astype(BF16)
        a = swa_window(q, k, v, k_c, v_c, sink, batch, seq, n_ctx)
    else:
        a = swa_dense(q, k, v, sink, batch, seq)
    x = out_proj_layer1(a, p1['w_out'], x, g1, tpg)
    x = moe_layer(x, p1, sh2, sc2, g2, tpg)
    nkv = SWA_KV_HEADS * SWA_HD
    state1 = (k_cache.reshape(batch, seq, SWA_KV_HEADS, SWA_HD),
              z[:, SWA_HEADS * SWA_HD + nkv:].reshape(batch, seq, SWA_KV_HEADS, SWA_HD))
    return x.reshape(batch, seq, d), state0, state1


def kernel(x_prompt, x_sample, state_l0_hgrn_fwd, state_l0_hgrn_bwd, cache_l0_mla_ckv, cache_l0_mla_kpe, cache_l1_k, cache_l1_v, c, c_ctx, hgrn_lb_logits, l0_ada_w, l0_ada_b, l0_norm1, l0_norm2, l0_w_in, l0_hg_out_norm, l0_qa_norm, l0_q_b, l0_kva_norm, l0_kv_b, l0_q_norm, l0_k_norm, l0_w_out, l0_router_w, l0_router_b, l0_w_gu, l0_b_gu, l0_w_down, l0_b_down, l1_ada_w, l1_ada_b, l1_norm1, l1_norm2, l1_w_in, l1_q_norm, l1_k_norm, l1_sink, l1_w_out, l1_router_w, l1_router_b, l1_w_gu, l1_b_gu, l1_w_down, l1_b_down):
    p0 = _prep_layer0(dict(norm1=l0_norm1, norm2=l0_norm2, w_in=l0_w_in, hg_out_norm=l0_hg_out_norm,
                           qa_norm=l0_qa_norm, q_b=l0_q_b, kva_norm=l0_kva_norm, kv_b=l0_kv_b,
                           q_norm=l0_q_norm, k_norm=l0_k_norm, w_out=l0_w_out, router_w=l0_router_w,
                           router_b=l0_router_b, w_gu=l0_w_gu, b_gu=l0_b_gu, w_down=l0_w_down,
                           b_down=l0_b_down))
    p1 = _prep_layer1(dict(norm1=l1_norm1, norm2=l1_norm2, w_in=l1_w_in, q_norm=l1_q_norm, k_norm=l1_k_norm,
                           sink=l1_sink, w_out=l1_w_out, router_w=l1_router_w, router_b=l1_router_b,
                           w_gu=l1_w_gu, b_gu=l1_b_gu, w_down=l1_w_down, b_down=l1_b_down))
    dec_batch = c.shape[0]
    d = c.shape[1]
    cond8 = jnp.concatenate([c_ctx[None, :], c, jnp.zeros((8 - 1 - dec_batch, d), F32)], axis=0)
    mods_ctx, mods_lat = [], []
    for w, b in ((l0_ada_w, l0_ada_b), (l1_ada_w, l1_ada_b)):
        mod = ada_params(cond8, w, b)
        mods_ctx.append([m.reshape(1, 1, d) for m in jnp.split(mod[0:1], 6, axis=-1)])
        mods_lat.append([m.reshape(dec_batch, 1, d) for m in jnp.split(mod[1:1 + dec_batch], 6, axis=-1)])

    y_prompt, st0, st1 = _group_forward(x_prompt, mods_ctx, p0, p1, hgrn_lb_logits, None, latent=False)
    caches = dict(hg_f=state_l0_hgrn_fwd, hg_b=state_l0_hgrn_bwd, ckv=cache_l0_mla_ckv, kpe=cache_l0_mla_kpe,
                  k1=cache_l1_k, v1=cache_l1_v)
    y_sample, _, _ = _group_forward(x_sample, mods_lat, p0, p1, hgrn_lb_logits, caches, latent=True)
    return (y_prompt, y_sample, st0[0], st0[1], st0[2], st0[3], st1[0], st1[1])
```

```python
import functools

import numpy as np
import jax
import jax.numpy as jnp
from jax import lax
from jax.experimental import pallas as pl
from jax.experimental.pallas import tpu as pltpu

F32 = jnp.float32
BF16 = jnp.bfloat16

D_MODEL = 1024
GRID_W = 64
ROPE_THETA = 10000.0
EPS = 1e-6
HG_HEADS = 4
HG_DK = 128
HG_DV = 128
MLA_HEADS = 8
MLA_NOPE = 64
MLA_ROPE = 32
MLA_V = 64
MLA_QK = MLA_NOPE + MLA_ROPE
MLA_Q_LORA = 256
MLA_KV_LORA = 128
MLA_SCALE = MLA_QK ** -0.5
SWA_HEADS = 16
SWA_KV_HEADS = 4
SWA_HD = 64
SWA_WINDOW = 128
SWA_SCALE = SWA_HD ** -0.5
SWA_GROUP = SWA_HEADS // SWA_KV_HEADS
N_EXPERTS = 32
TOP_K = 4
D_FF = 1024
SWIGLU_LIMIT = 7.0
SWIGLU_ALPHA = 1.702

LANES = 128
HG_CHUNK = 128
HG_LEVELS = 7
HG_MXU_LEVELS = 0
AB_IN_PAD = 3072
AB_COL_GATE = 3 * HG_HEADS * HG_DK + HG_HEADS * HG_DV
AB_COL_QA = AB_COL_GATE + HG_HEADS * HG_DV
AB_COL_KVA = AB_COL_QA + MLA_Q_LORA
AB_COL_KPE = AB_COL_KVA + MLA_KV_LORA
MOE_ROWS = 256
NEG_BIG = -1e30
VMEM_LIMIT = 48 * 1024 * 1024
EXPERT_VMEM_LIMIT = 56 * 1024 * 1024


def _cparams(sem):
    return pltpu.CompilerParams(dimension_semantics=sem, vmem_limit_bytes=VMEM_LIMIT)


def _dot(a, b):
    return jnp.dot(a, b, preferred_element_type=F32)


def _dot_nt(a, b):
    return lax.dot_general(a, b, (((1,), (1,)), ((), ())), preferred_element_type=F32)


def _dot_tn(a, b):
    return lax.dot_general(a, b, (((0,), (0,)), ((), ())), preferred_element_type=F32)


def _split2(x):
    hi = x.astype(BF16)
    lo = (x - hi.astype(F32)).astype(BF16)
    return hi, lo


def _dot_hp(a, b):
    ah, al = _split2(a)
    bh, bl = _split2(b)
    return _dot(ah, bh) + _dot(ah, bl) + _dot(al, bh)


def _sigmoid(x):
    return 1.0 / (1.0 + jnp.exp(-x))


def _modnorm(x, gain, shift, scale):
    y = x * lax.rsqrt(jnp.mean(x * x, axis=-1, keepdims=True) + EPS)
    return y * gain * (1.0 + scale) + shift


def _ada_kernel(c_ref, w_ref, b_ref, o_ref):
    c = c_ref[...]
    o_ref[...] = _dot_hp(c * _sigmoid(c), w_ref[...]) + b_ref[...]


def ada_params(cond8, w, b):
    n = w.shape[1]
    tn = 1024
    return pl.pallas_call(
        _ada_kernel,
        grid=(n // tn,),
        in_specs=[pl.BlockSpec((8, D_MODEL), lambda j: (0, 0)),
                  pl.BlockSpec((D_MODEL, tn), lambda j: (0, j)),
                  pl.BlockSpec((1, tn), lambda j: (0, j))],
        out_specs=pl.BlockSpec((8, tn), lambda j: (0, j)),
        out_shape=jax.ShapeDtypeStruct((8, n), F32),
        compiler_params=_cparams(("parallel",)),
        name="ada_params",
    )(cond8, w, b.reshape(1, n))


def _modnorm_matmul_kernel(x_ref, g_ref, sh_ref, sc_ref, w_ref, o_ref):
    h = _modnorm(x_ref[...], g_ref[...], sh_ref[0], sc_ref[0])
    o_ref[...] = _dot(h.astype(BF16), w_ref[...])


def modnorm_matmul(x, gain, shift, scale, w_bf16, tokens_per_group, tm=512):
    t, d = x.shape
    n = w_bf16.shape[1]
    tiles_per_group = tokens_per_group // tm
    grp = lambda i: (i // tiles_per_group, 0, 0)
    return pl.pallas_call(
        _modnorm_matmul_kernel,
        grid=(t // tm,),
        in_specs=[pl.BlockSpec((tm, d), lambda i: (i, 0)),
                  pl.BlockSpec((1, d), lambda i: (0, 0)),
                  pl.BlockSpec((1, 1, d), grp),
                  pl.BlockSpec((1, 1, d), grp),
                  pl.BlockSpec((d, n), lambda i: (0, 0))],
        out_specs=pl.BlockSpec((tm, n), lambda i: (i, 0)),
        out_shape=jax.ShapeDtypeStruct((t, n), F32),
        compiler_params=_cparams(("parallel",)),
        name="modnorm_matmul",
    )(x, gain.reshape(1, d), shift, scale, w_bf16)


def _hgrn_constants():
    c = HG_CHUNK
    t = np.arange(c)[:, None]
    u = np.arange(c)[None, :]
    mats = [(u <= t), (u > t)]
    for l in range(HG_MXU_LEVELS):
        m = 1 << l
        r = (t // (2 * m)) * (2 * m) + m - 1
        mats.append((u > np.minimum(t, r)) & (u <= np.maximum(t, r)))
    fwd = np.concatenate(mats, axis=0).astype(np.float32)
    bwd = np.concatenate([mm[::-1, ::-1] for mm in mats], axis=0).astype(np.float32)
    x = np.bitwise_xor(t, u)
    lvl = np.where(x > 0, np.floor(np.log2(np.maximum(x, 1))), HG_LEVELS).astype(np.int32)
    lv_f = np.where(t >= u, lvl, -1).astype(np.int32)
    return fwd, bwd, lv_f, lv_f.T.copy()


def _hgrn_chunk(qs, fpres, vs, lbs, sts, forwards, mcats, lvs):
    c = HG_CHUNK
    n = len(qs)
    fs = [lb + (1.0 - lb) * _sigmoid(fp) for lb, fp in zip(lbs, fpres)]
    kks = [1.0 - f for f in fs]
    logs = [_split2(jnp.log(f)) for f in fs]
    xs = [None] * n
    for fwd in (True, False):
        ids = [i for i in range(n) if forwards[i] == fwd]
        parts = [half for i in ids for half in logs[i]]
        x_all = _dot(mcats[0 if fwd else 1], jnp.concatenate(parts, axis=1))
        for j, i in enumerate(ids):
            xs[i] = x_all[:, 2 * j * c:(2 * j + 1) * c] + x_all[:, (2 * j + 1) * c:(2 * j + 2) * c]
    gs = [x[0:c] for x in xs]
    lv_of = [lvs[0 if f else 1] for f in forwards]
    qbs = [q.astype(BF16) for q in qs]
    kbs = [kk.astype(BF16) for kk in kks]
    vbs = [v.astype(BF16) for v in vs]
    os_ = [_dot_nt((q * jnp.exp(g)).astype(BF16), st.astype(BF16)) for q, g, st in zip(qs, gs, sts)]
    accs = [jnp.where(lv == HG_LEVELS, _dot_nt(qb, kb), 0.0) for lv, qb, kb in zip(lv_of, qbs, kbs)]
    for l in range(HG_LEVELS):
        for i in range(n):
            if l < HG_MXU_LEVELS:
                x = xs[i][(2 + l) * c:(3 + l) * c]
            else:
                m = 1 << l
                ref_rows = [j * 2 * m + (m - 1 if forwards[i] else m) for j in range(c // (2 * m))]
                g_ref = jnp.concatenate(
                    [jnp.broadcast_to(gs[i][r:r + 1, :], (2 * m, c)) for r in ref_rows], axis=0)
                x = -jnp.abs(gs[i] - g_ref)
            e = jnp.exp(x)
            p = _dot_nt((qs[i] * e).astype(BF16), (kks[i] * e).astype(BF16))
            accs[i] = jnp.where(lv_of[i] == l, p, accs[i])
    outs, new_sts = [], []
    for i in range(n):
        edge_row = c - 1 if forwards[i] else 0
        outs.append(os_[i] + _dot(accs[i].astype(BF16), vbs[i]))
        k_end = (kks[i] * jnp.exp(xs[i][c:2 * c])).astype(BF16)
        new_sts.append(sts[i] * jnp.exp(gs[i][edge_row:edge_row + 1, :]) + _dot_tn(vbs[i], k_end))
    return outs, new_sts


def _hgrn_kernel(*refs, has_init):
    if has_init:
        (qf_ref, qb_ref, ff_ref, fb_ref, vf_ref, vb_ref, lbl_ref, mf_ref, mb_ref, lvf_ref, lvb_ref,
         s0f_ref, s0b_ref, of_ref, ob_ref, sf_ref, sb_ref, stf, stb) = refs
    else:
        (qf_ref, qb_ref, ff_ref, fb_ref, vf_ref, vb_ref, lbl_ref, mf_ref, mb_ref, lvf_ref, lvb_ref,
         of_ref, ob_ref, sf_ref, sb_ref, stf, stb) = refs
    c = pl.program_id(1)
    nc = pl.num_programs(1)

    @pl.when(c == 0)
    def _():
        for h in range(HG_HEADS):
            if has_init:
                stf[h] = s0f_ref[0, h].T
                stb[h] = s0b_ref[0, h].T
            else:
                stf[h] = jnp.zeros((HG_DV, HG_DK), F32)
                stb[h] = jnp.zeros((HG_DV, HG_DK), F32)

    rows = [lbl_ref[:, j, :] for j in range(lbl_ref.shape[1])]
    mx = functools.reduce(jnp.maximum, rows)
    ex = [jnp.exp(r - mx) for r in rows]
    lb = ex[0] / functools.reduce(lambda a, b: a + b, ex)

    heads = [slice(h * LANES, (h + 1) * LANES) for h in range(HG_HEADS)]
    nh = HG_HEADS
    outs, sts = _hgrn_chunk(
        [qf_ref[0, :, hs] for hs in heads] + [qb_ref[0, :, hs] for hs in heads],
        [ff_ref[0, :, hs] for hs in heads] + [fb_ref[0, :, hs] for hs in heads],
        [vf_ref[0, :, hs] for hs in heads] + [vb_ref[0, :, hs] for hs in heads],
        [lb[0:1, hs] for hs in heads] + [lb[1:2, hs] for hs in heads],
        [stf[h] for h in range(nh)] + [stb[h] for h in range(nh)],
        [True] * nh + [False] * nh,
        (mf_ref[...], mb_ref[...]), (lvf_ref[...], lvb_ref[...]))
    o_f, o_b, st_f, st_b = outs[:nh], outs[nh:], sts[:nh], sts[nh:]
    for h, hs in enumerate(heads):
        of_ref[0, :, hs] = o_f[h]
        ob_ref[0, :, hs] = o_b[h]
        stf[h] = st_f[h]
        stb[h] = st_b[h]

    @pl.when(c == nc - 1)
    def _():
        for h in range(HG_HEADS):
            sf_ref[0, h] = stf[h].T
            sb_ref[0, h] = stb[h].T


def hgrn_bidir(z, lb_logits, s0f, s0b, batch, seq):
    nc = seq // HG_CHUNK
    z3 = z.reshape(batch, seq, z.shape[1])
    mf, mb, lvf, lvb = _hgrn_constants()
    has_init = s0f is not None
    width = HG_HEADS * LANES
    blk = (1, HG_CHUNK, width)
    fwd = lambda off: pl.BlockSpec(blk, lambda b, c: (b, c, off))
    bwd = lambda off: pl.BlockSpec(blk, lambda b, c: (b, nc - 1 - c, off))
    full = lambda a: pl.BlockSpec(a.shape, lambda b, c: (0,) * a.ndim)
    st_spec = pl.BlockSpec((1, HG_HEADS, HG_DK, HG_DV), lambda b, c: (b, 0, 0, 0))
    consts = [jnp.asarray(mf, BF16), jnp.asarray(mb, BF16), jnp.asarray(lvf), jnp.asarray(lvb)]
    in_specs = [fwd(0), bwd(0), fwd(1), bwd(2), fwd(3), bwd(3),
                pl.BlockSpec(lb_logits.shape, lambda b, c: (0, 0, 0))]
    in_specs += [full(a) for a in consts]
    args = [z3] * 6 + [lb_logits] + consts
    if has_init:
        in_specs += [st_spec, st_spec]
        args += [s0f, s0b]
    o_shape = jax.ShapeDtypeStruct((batch, seq, width), F32)
    s_shape = jax.ShapeDtypeStruct((batch, HG_HEADS, HG_DK, HG_DV), F32)
    o_f, o_b, s_f, s_b = pl.pallas_call(
        functools.partial(_hgrn_kernel, has_init=has_init),
        grid=(batch, nc),
        in_specs=in_specs,
        out_specs=[pl.BlockSpec(blk, lambda b, c: (b, c, 0)),
                   pl.BlockSpec(blk, lambda b, c: (b, nc - 1 - c, 0)),
                   st_spec, st_spec],
        out_shape=[o_shape, o_shape, s_shape, s_shape],
        scratch_shapes=[pltpu.VMEM((HG_HEADS, HG_DV, HG_DK), F32), pltpu.VMEM((HG_HEADS, HG_DV, HG_DK), F32)],
        compiler_params=_cparams(("parallel", "arbitrary")),
        name="hgrn_bidir",
    )(*args)
    t = batch * seq
    return o_f.reshape(t, -1), o_b.reshape(t, -1), s_f, s_b


def _axial_tables(n_tokens, n_rot):
    t = jnp.arange(n_tokens)
    row = (t // GRID_W).astype(F32)
    col = (t % GRID_W).astype(F32)
    n_freq = n_rot // 4
    inv = jnp.power(ROPE_THETA, -jnp.arange(n_freq, dtype=F32) / n_freq)
    ang = jnp.concatenate([row[:, None] * inv, col[:, None] * inv], axis=-1)
    return jnp.cos(ang), jnp.sin(ang)


def _rope_lane_tables(n_tokens, n_rot, head_width, first_rot_lane):
    cos, sin = _axial_tables(n_tokens, n_rot)
    half = n_rot // 2
    c_head = jnp.ones((n_tokens, head_width), F32)
    sa_head = jnp.zeros((n_tokens, head_width), F32)
    sb_head = jnp.zeros((n_tokens, head_width), F32)
    a0, a1, a2 = first_rot_lane, first_rot_lane + half, first_rot_lane + n_rot
    c_head = c_head.at[:, a0:a1].set(cos).at[:, a1:a2].set(cos)
    sa_head = sa_head.at[:, a0:a1].set(-sin)
    sb_head = sb_head.at[:, a1:a2].set(sin)
    reps = LANES // head_width
    tile = lambda a: jnp.tile(a, (1, reps))
    return tile(c_head), tile(sa_head), tile(sb_head)


def _rope(x, c, sa, sb, half):
    return x * c + pltpu.roll(x, LANES - half, 1) * sa + pltpu.roll(x, half, 1) * sb


def _mla_q_kernel(*refs, rope):
    if rope:
        qa_ref, qan_ref, qb_ref, qn_ref, c_ref, sa_ref, sb_ref, o_ref = refs
    else:
        qa_ref, qan_ref, qb_ref, qn_ref, o_ref = refs
    qa = qa_ref[...]
    qn = qa * lax.rsqrt(jnp.mean(qa * qa, axis=-1, keepdims=True) + EPS) * qan_ref[...]
    qfull = _dot(qn.astype(BF16), qb_ref[...])
    outs = []
    for h in range(MLA_HEADS):
        qh = qfull[:, h * LANES:(h + 1) * LANES]
        ms = jnp.sum(qh * qh, axis=-1, keepdims=True) * (1.0 / MLA_QK)
        qh = qh * lax.rsqrt(ms + EPS) * qn_ref[...]
        if rope:
            qh = _rope(qh, c_ref[...], sa_ref[...], sb_ref[...], MLA_ROPE // 2)
        outs.append((qh * MLA_SCALE).astype(BF16))
    o_ref[...] = jnp.concatenate(outs, axis=1)


def mla_queries(z, qa_norm, q_b_pad, q_norm_pad, tables, tokens_per_batch, tm=256):
    t = z.shape[0]
    rope = tables is not None
    row = lambda i: (0, 0)
    in_specs = [pl.BlockSpec((tm, MLA_Q_LORA), lambda i: (i, AB_COL_QA // MLA_Q_LORA)),
                pl.BlockSpec((1, MLA_Q_LORA), row),
                pl.BlockSpec(q_b_pad.shape, row),
                pl.BlockSpec((1, LANES), row)]
    args = [z, qa_norm.reshape(1, -1), q_b_pad, q_norm_pad]
    if rope:
        tpb = tokens_per_batch // tm
        in_specs += [pl.BlockSpec((tm, LANES), lambda i: (i % tpb, 0))] * 3
        args += list(tables)
    return pl.pallas_call(
        functools.partial(_mla_q_kernel, rope=rope),
        grid=(t // tm,),
        in_specs=in_specs,
        out_specs=pl.BlockSpec((tm, MLA_HEADS * LANES), lambda i: (i, 0)),
        out_shape=jax.ShapeDtypeStruct((t, MLA_HEADS * LANES), BF16),
        compiler_params=_cparams(("parallel",)),
        name="mla_queries",
    )(*args)


def _mla_kv_kernel(*refs, norm_input, rope):
    if rope:
        kva_ref, kpe_ref, kvan_ref, wk_ref, wv_ref, kn_ref, c_ref, sa_ref, sb_ref, ckv_ref, k_ref, v_ref = refs
    else:
        kva_ref, kpe_ref, kvan_ref, wk_ref, wv_ref, kn_ref, ckv_ref, k_ref, v_ref = refs
    ckv = kva_ref[...]
    if norm_input:
        ckv = ckv * lax.rsqrt(jnp.mean(ckv * ckv, axis=-1, keepdims=True) + EPS) * kvan_ref[...]
    ckv_ref[...] = ckv
    cb = ckv.astype(BF16)
    knope = _dot(cb, wk_ref[...])
    v_ref[...] = _dot(cb, wv_ref[...]).astype(BF16)
    kpe = pltpu.roll(kpe_ref[...], MLA_NOPE, 1)
    outs = []
    for h in range(MLA_HEADS):
        kh = knope[:, h * LANES:(h + 1) * LANES] + kpe
        ms = jnp.sum(kh * kh, axis=-1, keepdims=True) * (1.0 / MLA_QK)
        kh = kh * lax.rsqrt(ms + EPS) * kn_ref[...]
        if rope:
            kh = _rope(kh, c_ref[...], sa_ref[...], sb_ref[...], MLA_ROPE // 2)
        outs.append(kh.astype(BF16))
    k_ref[...] = jnp.concatenate(outs, axis=1)


def mla_keys_values(kva_src, kva_col, kpe_src, kpe_col, kva_norm, wk_pad, wv, k_norm_pad, tables,
                    tokens_per_batch, norm_input, tm=256):
    t = kva_src.shape[0]
    rope = tables is not None
    row = lambda i: (0, 0)
    in_specs = [pl.BlockSpec((tm, LANES), lambda i: (i, kva_col)),
                pl.BlockSpec((tm, LANES), lambda i: (i, kpe_col)),
                pl.BlockSpec((1, LANES), row),
                pl.BlockSpec(wk_pad.shape, row),
                pl.BlockSpec(wv.shape, row),
                pl.BlockSpec((1, LANES), row)]
    args = [kva_src, kpe_src, kva_norm.reshape(1, -1), wk_pad, wv, k_norm_pad]
    if rope:
        tpb = tokens_per_batch // tm
        in_specs += [pl.BlockSpec((tm, LANES), lambda i: (i % tpb, 0))] * 3
        args += list(tables)
    return pl.pallas_call(
        functools.partial(_mla_kv_kernel, norm_input=norm_input, rope=rope),
        grid=(t // tm,),
        in_specs=in_specs,
        out_specs=[pl.BlockSpec((tm, LANES), lambda i: (i, 0)),
                   pl.BlockSpec((tm, MLA_HEADS * LANES), lambda i: (i, 0)),
                   pl.BlockSpec((tm, MLA_HEADS * MLA_V), lambda i: (i, 0))],
        out_shape=[jax.ShapeDtypeStruct((t, LANES), F32),
                   jax.ShapeDtypeStruct((t, MLA_HEADS * LANES), BF16),
                   jax.ShapeDtypeStruct((t, MLA_HEADS * MLA_V), BF16)],
        compiler_params=_cparams(("parallel",)),
        name="mla_keys_values",
    )(*args)


def _mla_attn_kernel(q_ref, k_ref, vt_ref, o_ref):
    heads = range(2)
    ss = [_dot_nt(k_ref[0][:, j * LANES:(j + 1) * LANES], q_ref[0][:, j * LANES:(j + 1) * LANES]) for j in heads]
    ms = [jnp.max(s, axis=0, keepdims=True) for s in ss]
    ps = [jnp.exp(s - m) for s, m in zip(ss, ms)]
    ls = [jnp.sum(p, axis=0, keepdims=True) for p in ps]
    ots = [_dot(vt_ref[0][j * MLA_V:(j + 1) * MLA_V, :], ps[j].astype(BF16)) / ls[j] for j in heads]
    o_ref[0] = jnp.concatenate(ots, axis=0).T


def mla_attention(q, k, v, batch, n_q, n_k, tq=512):
    tq = min(tq, n_q)
    q3 = q.reshape(batch, n_q, -1)
    k3 = k.reshape(batch, n_k, -1)
    vt = jnp.swapaxes(v.reshape(batch, n_k, -1), 1, 2)
    out = pl.pallas_call(
        _mla_attn_kernel,
        grid=(batch, MLA_HEADS // 2, n_q // tq),
        in_specs=[pl.BlockSpec((1, tq, 2 * LANES), lambda b, j, i: (b, i, j)),
                  pl.BlockSpec((1, n_k, 2 * LANES), lambda b, j, i: (b, 0, j)),
                  pl.BlockSpec((1, 2 * MLA_V, n_k), lambda b, j, i: (b, j, 0))],
        out_specs=pl.BlockSpec((1, tq, 2 * MLA_V), lambda b, j, i: (b, i, j)),
        out_shape=jax.ShapeDtypeStruct((batch, n_q, MLA_HEADS * MLA_V), F32),
        compiler_params=_cparams(("parallel", "parallel", "arbitrary")),
        name="mla_attention",
    )(q3, k3, vt)
    return out.reshape(batch * n_q, -1)


def _out0_kernel(of_ref, ob_ref, ag_ref, hgn_ref, om_ref, w_ref, x_ref, g_ref, o_ref):
    o = of_ref[...] + ob_ref[...]
    ag = ag_ref[...]
    parts = []
    for h in range(HG_HEADS):
        oh = o[:, h * HG_DV:(h + 1) * HG_DV]
        oh = oh * lax.rsqrt(jnp.mean(oh * oh, axis=-1, keepdims=True) + EPS) * hgn_ref[...]
        gh = ag[:, h * HG_DV:(h + 1) * HG_DV]
        parts.append((oh * (gh * _sigmoid(gh))).astype(BF16))
    oa = jnp.concatenate(parts, axis=1)
    n_a = HG_HEADS * HG_DV
    mix = _dot(oa, w_ref[0:n_a, :]) + _dot(om_ref[...].astype(BF16), w_ref[n_a:, :])
    o_ref[...] = x_ref[...] + g_ref[0] * mix


def out_proj_layer0(o_f, o_b, z, hg_norm, o_mla, w_out_bf16, x, gate, tokens_per_group, tm=512):
    t, d = x.shape
    n_a = HG_HEADS * HG_DV
    tiles_per_group = tokens_per_group // tm
    tile = lambda w: pl.BlockSpec((tm, w), lambda i: (i, 0))
    return pl.pallas_call(
        _out0_kernel,
        grid=(t // tm,),
        in_specs=[tile(n_a), tile(n_a),
                  pl.BlockSpec((tm, n_a), lambda i: (i, AB_COL_GATE // n_a)),
                  pl.BlockSpec((1, HG_DV), lambda i: (0, 0)),
                  tile(o_mla.shape[1]),
                  pl.BlockSpec(w_out_bf16.shape, lambda i: (0, 0)),
                  tile(d),
                  pl.BlockSpec((1, 1, d), lambda i: (i // tiles_per_group, 0, 0))],
        out_specs=tile(d),
        out_shape=jax.ShapeDtypeStruct((t, d), F32),
        compiler_params=_cparams(("parallel",)),
        name="out_proj_layer0",
    )(o_f, o_b, z, hg_norm.reshape(1, -1), o_mla, w_out_bf16, x, gate)


def _out1_kernel(a_ref, w_ref, x_ref, g_ref, o_ref):
    o_ref[...] = x_ref[...] + g_ref[0] * _dot(a_ref[...].astype(BF16), w_ref[...])


def out_proj_layer1(a, w_out_bf16, x, gate, tokens_per_group, tm=512):
    t, d = x.shape
    tiles_per_group = tokens_per_group // tm
    tile = lambda w: pl.BlockSpec((tm, w), lambda i: (i, 0))
    return pl.pallas_call(
        _out1_kernel,
        grid=(t // tm,),
        in_specs=[tile(a.shape[1]),
                  pl.BlockSpec(w_out_bf16.shape, lambda i: (0, 0)),
                  tile(d),
                  pl.BlockSpec((1, 1, d), lambda i: (i // tiles_per_group, 0, 0))],
        out_specs=tile(d),
        out_shape=jax.ShapeDtypeStruct((t, d), F32),
        compiler_params=_cparams(("parallel",)),
        name="out_proj_layer1",
    )(a, w_out_bf16, x, gate)


def _head_rms(x, gain2):
    sq = x * x
    lane = lax.broadcasted_iota(jnp.int32, x.shape, 1)
    first = lane < SWA_HD
    lo = jnp.sum(jnp.where(first, sq, 0.0), axis=-1, keepdims=True)
    hi = jnp.sum(jnp.where(first, 0.0, sq), axis=-1, keepdims=True)
    ms = jnp.where(first, lo, hi) * (1.0 / SWA_HD)
    return x * lax.rsqrt(ms + EPS) * gain2


def _swa_prep_kernel(*refs, rope):
    if rope:
        zq_ref, zk_ref, zv_ref, qn_ref, kn_ref, c_ref, sa_ref, sb_ref, q_ref, kc_ref, k_ref, v_ref = refs
    else:
        zq_ref, zk_ref, zv_ref, qn_ref, kn_ref, q_ref, kc_ref, k_ref, v_ref = refs
    half = SWA_HD // 2

    def rot(x):
        return _rope(x, c_ref[...], sa_ref[...], sb_ref[...], half) if rope else x

    zq = zq_ref[...]
    qs = []
    for p in range(zq.shape[1] // LANES):
        x = _head_rms(zq[:, p * LANES:(p + 1) * LANES], qn_ref[...])
        qs.append((rot(x) * SWA_SCALE).astype(BF16))
    q_ref[...] = jnp.concatenate(qs, axis=1)
    zk = zk_ref[...]
    kn, kr = [], []
    for p in range(zk.shape[1] // LANES):
        x = _head_rms(zk[:, p * LANES:(p + 1) * LANES], kn_ref[...])
        kn.append(x)
        kr.append(rot(x).astype(BF16))
    kc_ref[...] = jnp.concatenate(kn, axis=1)
    k_ref[...] = jnp.concatenate(kr, axis=1)
    v_ref[...] = zv_ref[...].astype(BF16)


def swa_prep(z, q_norm2, k_norm2, tables, tokens_per_batch, tm=256):
    t = z.shape[0]
    nq = SWA_HEADS * SWA_HD
    nkv = SWA_KV_HEADS * SWA_HD
    rope = tables is not None
    row = lambda i: (0, 0)
    in_specs = [pl.BlockSpec((tm, nq), lambda i: (i, 0)),
                pl.BlockSpec((tm, nkv), lambda i: (i, nq // nkv)),
                pl.BlockSpec((tm, nkv), lambda i: (i, nq // nkv + 1)),
                pl.BlockSpec((1, LANES), row),
                pl.BlockSpec((1, LANES), row)]
    args = [z, z, z, q_norm2, k_norm2]
    if rope:
        tpb = tokens_per_batch // tm
        in_specs += [pl.BlockSpec((tm, LANES), lambda i: (i % tpb, 0))] * 3
        args += list(tables)
    tile = lambda w: pl.BlockSpec((tm, w), lambda i: (i, 0))
    return pl.pallas_call(
        functools.partial(_swa_prep_kernel, rope=rope),
        grid=(t // tm,),
        in_specs=in_specs,
        out_specs=[tile(nq), tile(nkv), tile(nkv), tile(nkv)],
        out_shape=[jax.ShapeDtypeStruct((t, nq), BF16),
                   jax.ShapeDtypeStruct((t, nkv), F32),
                   jax.ShapeDtypeStruct((t, nkv), BF16),
                   jax.ShapeDtypeStruct((t, nkv), BF16)],
        compiler_params=_cparams(("parallel",)),
        name="swa_prep",
    )(*args)


def _sink_attention_t(q8, k_all, vt_all, bias, sink_ref, pair):
    tq = q8.shape[0]
    n_q = SWA_GROUP * tq
    lane = lax.broadcasted_iota(jnp.int32, (1, n_q), 1)
    kv = range(2)
    heads = [[n * SWA_GROUP + g for g in range(SWA_GROUP)] for n in kv]
    q4s = [jnp.concatenate([q8[:, hq * SWA_HD:(hq + 1) * SWA_HD] for hq in heads[n]], axis=0) for n in kv]
    sks = []
    for n in kv:
        sk = jnp.zeros((1, n_q), F32)
        for g, hq in enumerate(heads[n]):
            sk = jnp.where(lane // tq == g, sink_ref[pair * 2 * SWA_GROUP + hq], sk)
        sks.append(sk)
    ss = [_dot_nt(k_all[:, n * SWA_HD:(n + 1) * SWA_HD], q4s[n]) for n in kv]
    if bias is not None:
        ss = [s + bias for s in ss]
    ms = [jnp.maximum(jnp.max(ss[n], axis=0, keepdims=True), sks[n]) for n in kv]
    ps = [jnp.exp(ss[n] - ms[n]) for n in kv]
    ls = [jnp.sum(ps[n], axis=0, keepdims=True) + jnp.exp(sks[n] - ms[n]) for n in kv]
    ots = [_dot(vt_all[n * SWA_HD:(n + 1) * SWA_HD, :], ps[n].astype(BF16)) / ls[n] for n in kv]
    pairs = []
    for n in kv:
        for g in range(0, SWA_GROUP, 2):
            two = jnp.concatenate([ots[n][:, g * tq:(g + 1) * tq], ots[n][:, (g + 1) * tq:(g + 2) * tq]], axis=0)
            pairs.append(two.T)
    return jnp.concatenate(pairs, axis=1)


def _swa_dense_kernel(sink_ref, q_ref, k_ref, vt_ref, o_ref):
    o_ref[0] = _sink_attention_t(q_ref[0], k_ref[0], vt_ref[0], None, sink_ref, pl.program_id(1))


def swa_dense(q, k, v, sink, batch, seq):
    nq = SWA_HEADS * SWA_HD
    q3 = q.reshape(batch, seq, nq)
    k3 = k.reshape(batch, seq, -1)
    vt = jnp.swapaxes(v.reshape(batch, seq, -1), 1, 2)
    out = pl.pallas_call(
        _swa_dense_kernel,
        grid=(batch, 2),
        in_specs=[pl.BlockSpec(memory_space=pltpu.SMEM),
                  pl.BlockSpec((1, seq, nq // 2), lambda b, j: (b, 0, j)),
                  pl.BlockSpec((1, seq, LANES), lambda b, j: (b, 0, j)),
                  pl.BlockSpec((1, LANES, seq), lambda b, j: (b, j, 0))],
        out_specs=pl.BlockSpec((1, seq, nq // 2), lambda b, j: (b, 0, j)),
        out_shape=jax.ShapeDtypeStruct((batch, seq, nq), F32),
        compiler_params=_cparams(("parallel", "parallel")),
        name="swa_dense",
    )(sink, q3, k3, vt)
    return out.reshape(batch * seq, nq)


def _swa_window_kernel(sink_ref, q_ref, kc_ref, kp_ref, k0_ref, kn_ref, vc_ref, vp_ref, v0_ref, vn_ref, o_ref):
    i = pl.program_id(2)
    nb = pl.num_programs(2)
    w = SWA_WINDOW
    n_ctx = kc_ref.shape[1]
    k_all = jnp.concatenate([kc_ref[0], kp_ref[0], k0_ref[0], kn_ref[0]], axis=0)
    vt_all = jnp.concatenate([vc_ref[0], vp_ref[0], v0_ref[0], vn_ref[0]], axis=1)
    shape = (n_ctx + 3 * w, SWA_GROUP * w)
    key = lax.broadcasted_iota(jnp.int32, shape, 0) - n_ctx
    r = lax.broadcasted_iota(jnp.int32, shape, 1) % w
    prev_bias = jnp.where(i > 0, 0.0, NEG_BIG)
    next_bias = jnp.where(i < nb - 1, 0.0, NEG_BIG)
    bias = jnp.where(
        key < w,
        jnp.where(key < 0, 0.0, jnp.where(key >= r, prev_bias, NEG_BIG)),
        jnp.where(key < 2 * w, 0.0, jnp.where(key - 2 * w <= r, next_bias, NEG_BIG)))
    o_ref[0] = _sink_attention_t(q_ref[0], k_all, vt_all, bias, sink_ref, pl.program_id(1))


def swa_window(q, k, v, k_ctx, v_ctx, sink, batch, seq, n_ctx):
    nq = SWA_HEADS * SWA_HD
    w = SWA_WINDOW
    nb = seq // w
    q3 = q.reshape(batch, seq, nq)
    k3 = k.reshape(batch, seq, -1)
    vt = jnp.swapaxes(v.reshape(batch, seq, -1), 1, 2)
    vt_ctx = jnp.swapaxes(v_ctx, 1, 2)
    ctx = pl.BlockSpec((1, n_ctx, LANES), lambda b, j, i: (b, 0, j))
    prv = pl.BlockSpec((1, w, LANES), lambda b, j, i: (b, jnp.maximum(i - 1, 0), j))
    cur = pl.BlockSpec((1, w, LANES), lambda b, j, i: (b, i, j))
    nxt = pl.BlockSpec((1, w, LANES), lambda b, j, i: (b, jnp.minimum(i + 1, nb - 1), j))
    ctx_t = pl.BlockSpec((1, LANES, n_ctx), lambda b, j, i: (b, j, 0))
    prv_t = pl.BlockSpec((1, LANES, w), lambda b, j, i: (b, j, jnp.maximum(i - 1, 0)))
    cur_t = pl.BlockSpec((1, LANES, w), lambda b, j, i: (b, j, i))
    nxt_t = pl.BlockSpec((1, LANES, w), lambda b, j, i: (b, j, jnp.minimum(i + 1, nb - 1)))
    out = pl.pallas_call(
        _swa_window_kernel,
        grid=(batch, 2, nb),
        in_specs=[pl.BlockSpec(memory_space=pltpu.SMEM),
                  pl.BlockSpec((1, w, nq // 2), lambda b, j, i: (b, i, j)),
                  ctx, prv, cur, nxt, ctx_t, prv_t, cur_t, nxt_t],
        out_specs=pl.BlockSpec((1, w, nq // 2), lambda b, j, i: (b, i, j)),
        out_shape=jax.ShapeDtypeStruct((batch, seq, nq), F32),
        compiler_params=_cparams(("parallel", "parallel", "arbitrary")),
        name="swa_window",
    )(sink, q3, k_ctx, k3, k3, k3, vt_ctx, vt, vt, vt)
    return out.reshape(batch * seq, nq)


def _router_kernel(x_ref, g_ref, sh_ref, sc_ref, rw_ref, rb_ref, tri_ref, h_ref, idx_ref, gate_ref, rank_ref,
                   cnt_ref, cnt, *, tiles_per_part):
    @pl.when(pl.program_id(0) % tiles_per_part == 0)
    def _():
        cnt[...] = jnp.zeros_like(cnt)

    h = _modnorm(x_ref[...], g_ref[...], sh_ref[0], sc_ref[0])
    bits = lax.bitcast_convert_type(h.astype(BF16).astype(F32), jnp.uint32)
    half = h.shape[1] // 2
    h_ref[...] = (bits[:, :half] >> 16) | (bits[:, half:] & jnp.uint32(0xFFFF0000))
    wh, wl = _split2(rw_ref[...])
    hh, hl = _split2(h)
    logits = _dot_nt(wh, hh) + _dot_nt(wh, hl) + _dot_nt(wl, hh) + rb_ref[...]
    row = lax.broadcasted_iota(jnp.int32, logits.shape, 0)
    work = logits
    vals, idxs = [], []
    for _ in range(TOP_K):
        m = jnp.max(work, axis=0, keepdims=True)
        ik = jnp.min(jnp.where(work == m, row, LANES), axis=0, keepdims=True)
        vals.append(m)
        idxs.append(ik)
        work = jnp.where(row == ik, 2.0 * NEG_BIG, work)
    es = [jnp.exp(v - vals[0]) for v in vals]
    denom = es[0] + es[1] + es[2] + es[3]
    chosen = jnp.zeros(logits.shape, F32)
    for k in range(TOP_K):
        chosen = jnp.where(row == idxs[k], 1.0, chosen)
    chosen_b = chosen.astype(BF16)
    seen = cnt[...]
    before = _dot(chosen_b, tri_ref[...]) + jnp.concatenate([seen] * (logits.shape[1] // LANES), axis=1)
    out_row = lax.broadcasted_iota(jnp.int32, idx_ref.shape, 0)
    idx_out = jnp.zeros(idx_ref.shape, jnp.int32)
    gate_out = jnp.zeros(idx_ref.shape, F32)
    rank_out = jnp.zeros(idx_ref.shape, jnp.int32)
    for k in range(TOP_K):
        rk = jnp.sum(jnp.where(row == idxs[k], before, 0.0), axis=0, keepdims=True)
        idx_out = jnp.where(out_row == k, idxs[k], idx_out)
        gate_out = jnp.where(out_row == k, es[k] / denom, gate_out)
        rank_out = jnp.where(out_row == k, rk.astype(jnp.int32), rank_out)
    idx_ref[...] = idx_out
    gate_ref[...] = gate_out
    rank_ref[...] = rank_out
    cnt[...] = seen + _dot(chosen_b, jnp.ones((logits.shape[1], LANES), BF16))
    cnt_ref[0] = cnt[...]


def moe_router(x, gain, shift, scale, rw_t, rb_col, tokens_per_group, tm=256):
    t, d = x.shape
    tiles_per_group = tokens_per_group // tm
    tiles_per_part = MOE_PART // tm
    grp = lambda i: (i // tiles_per_group, 0, 0)
    tile = lambda w: pl.BlockSpec((tm, w), lambda i: (i, 0))
    slab = pl.BlockSpec((8, tm), lambda i: (0, i))
    tri = jnp.asarray(np.triu(np.ones((tm, tm), np.float32), 1), BF16)
    hp, idx, gates, rank, cnt = pl.pallas_call(
        functools.partial(_router_kernel, tiles_per_part=tiles_per_part),
        grid=(t // tm,),
        in_specs=[tile(d),
                  pl.BlockSpec((1, d), lambda i: (0, 0)),
                  pl.BlockSpec((1, 1, d), grp),
                  pl.BlockSpec((1, 1, d), grp),
                  pl.BlockSpec((LANES, d), lambda i: (0, 0)),
                  pl.BlockSpec((LANES, 1), lambda i: (0, 0)),
                  pl.BlockSpec((tm, tm), lambda i: (0, 0))],
        out_specs=[tile(d // 2), slab, slab, slab,
                   pl.BlockSpec((1, LANES, LANES), lambda i: (i // tiles_per_part, 0, 0))],
        out_shape=[jax.ShapeDtypeStruct((t, d // 2), jnp.uint32),
                   jax.ShapeDtypeStruct((8, t), jnp.int32),
                   jax.ShapeDtypeStruct((8, t), F32),
                   jax.ShapeDtypeStruct((8, t), jnp.int32),
                   jax.ShapeDtypeStruct((t // MOE_PART, LANES, LANES), F32)],
        scratch_shapes=[pltpu.VMEM((LANES, LANES), F32)],
        compiler_params=_cparams(("arbitrary",)),
        name="moe_router",
    )(x, gain.reshape(1, d), shift, scale, rw_t, rb_col, tri)
    per_token = lambda a: a[:TOP_K].T
    return hp, per_token(idx), per_token(gates), per_token(rank), cnt[:, :N_EXPERTS, 0].astype(jnp.int32)


FFN_COLS = 512
FFN_PIECES = 2 * D_FF // FFN_COLS + D_MODEL // FFN_COLS
MOE_PART = 4096
ADD_BATCH = 8


def _unpack_rows(words):
    lo = lax.bitcast_convert_type(words << 16, F32)
    hi = lax.bitcast_convert_type(words & jnp.uint32(0xFFFF0000), F32)
    return jnp.concatenate([lo, hi], axis=1).astype(BF16)


def _ffn_block(x_ref, y_ref, wgu_ref, bgu_ref, wd_ref, bd_ref, between):
    xb = _unpack_rows(x_ref[...])
    acts = []
    for c in range(D_FF // FFN_COLS):
        lo, hi = c * FFN_COLS, (c + 1) * FFN_COLS
        between(2 * c)
        gate = _dot(xb, wgu_ref[0, :, lo:hi]) + bgu_ref[0][:, lo:hi]
        between(2 * c + 1)
        up = _dot(xb, wgu_ref[0, :, D_FF + lo:D_FF + hi]) + bgu_ref[0][:, D_FF + lo:D_FF + hi]
        gate = jnp.minimum(gate, SWIGLU_LIMIT)
        up = jnp.clip(up, -SWIGLU_LIMIT, SWIGLU_LIMIT)
        acts.append((gate * _sigmoid(SWIGLU_ALPHA * gate) * (up + 1.0)).astype(BF16))
    act = jnp.concatenate(acts, axis=1)
    for n in range(D_MODEL // FFN_COLS):
        lo, hi = n * FFN_COLS, (n + 1) * FFN_COLS
        between(2 * D_FF // FFN_COLS + n)
        y_ref[:, lo:hi] = _dot(act, wd_ref[0, :, lo:hi]) + bd_ref[0][:, lo:hi]


def _expert_kernel(nb_ref, bs_ref, loc_ref, gate_ref, hp_hbm, wgu_ref, bgu_ref, wd_ref, bd_ref, out_hbm,
                   hbuf, acc, xbuf, ybuf, sem):
    part = pl.program_id(0)
    e = pl.program_id(1)
    idx = part * N_EXPERTS + e
    nb = nb_ref[idx]
    b0 = bs_ref[idx]
    shares = np.array_split(np.arange(MOE_ROWS), FFN_PIECES)

    def fetch_rows(blk):
        base = blk * MOE_ROWS
        dst = xbuf.at[blk % 2]

        def emit(k):
            for r in shares[k]:
                r = int(r)
                dst[r:r + 1, :] = hbuf[pl.ds(loc_ref[base + r], 1), :]
        return emit

    def add_rows(blk):
        base = blk * MOE_ROWS
        src = ybuf.at[blk % 2]

        def emit(k):
            for batch in np.array_split(shares[k], max(1, len(shares[k]) // ADD_BATCH)):
                rows = [loc_ref[base + int(r)] for r in batch]
                new = [acc[pl.ds(row, 1), :] + gate_ref[base + int(r)] * src[int(r):int(r) + 1, :]
                       for row, r in zip(rows, batch)]
                for row, val in zip(rows, new):
                    acc[pl.ds(row, 1), :] = val
        return emit

    def emit_all(f):
        for k in range(FFN_PIECES):
            f(k)

    @pl.when(e == 0)
    def _():
        load = pltpu.make_async_copy(hp_hbm.at[pl.ds(part * MOE_PART, MOE_PART), :],
                                     hbuf.at[pl.ds(0, MOE_PART), :], sem.at[0])
        load.start()
        hbuf[MOE_PART:, :] = jnp.zeros((hbuf.shape[0] - MOE_PART, hbuf.shape[1]), hbuf.dtype)
        acc[...] = jnp.zeros_like(acc)
        ybuf[...] = jnp.zeros_like(ybuf)
        load.wait()
        emit_all(fetch_rows(b0))

    def block(i, carry):
        fetch, add = fetch_rows(i + 1), add_rows(i - 1)

        def between(k):
            fetch(k)
            add(k)
        _ffn_block(xbuf.at[i % 2], ybuf.at[i % 2], wgu_ref, bgu_ref, wd_ref, bd_ref, between)
        return carry

    lax.fori_loop(b0, b0 + nb, block, 0)

    @pl.when(e == N_EXPERTS - 1)
    def _():
        emit_all(add_rows(b0 + nb - 1))
        store = pltpu.make_async_copy(acc.at[pl.ds(0, MOE_PART), :],
                                      out_hbm.at[pl.ds(part * MOE_PART, MOE_PART), :], sem.at[0])
        store.start()
        store.wait()


def moe_experts(hp, n_blk, blk_start, row_loc, row_gate, w_gu, b_gu, w_down, b_down):
    t = hp.shape[0]
    d = D_MODEL
    per_expert = lambda p, e, nb, bs, sl: (e, 0, 0)
    grid_spec = pltpu.PrefetchScalarGridSpec(
        num_scalar_prefetch=3,
        grid=(t // MOE_PART, N_EXPERTS),
        in_specs=[pl.BlockSpec(memory_space=pltpu.SMEM),
                  pl.BlockSpec(memory_space=pl.ANY),
                  pl.BlockSpec((1, d, 2 * D_FF), per_expert),
                  pl.BlockSpec((1, 1, 2 * D_FF), per_expert),
                  pl.BlockSpec((1, D_FF, d), per_expert),
                  pl.BlockSpec((1, 1, d), per_expert)],
        out_specs=pl.BlockSpec(memory_space=pl.ANY),
        scratch_shapes=[pltpu.VMEM((MOE_PART + 8, d // 2), jnp.uint32),
                        pltpu.VMEM((MOE_PART + 8, d), F32),
                        pltpu.VMEM((2, MOE_ROWS, d // 2), jnp.uint32),
                        pltpu.VMEM((2, MOE_ROWS, d), F32),
                        pltpu.SemaphoreType.DMA((1,))],
    )
    return pl.pallas_call(
        _expert_kernel,
        grid_spec=grid_spec,
        out_shape=jax.ShapeDtypeStruct((t, d), F32),
        compiler_params=pltpu.CompilerParams(dimension_semantics=("arbitrary", "arbitrary"),
                                             vmem_limit_bytes=EXPERT_VMEM_LIMIT),
        name="moe_experts",
    )(n_blk, blk_start, row_loc, row_gate, hp, w_gu, b_gu.reshape(N_EXPERTS, 1, -1), w_down,
      b_down.reshape(N_EXPERTS, 1, -1))


def _residual_kernel(y_ref, x_ref, g_ref, o_ref):
    o_ref[...] = x_ref[...] + g_ref[0] * y_ref[...]


def gated_residual(y, x, gate_mod, tokens_per_group, tm=512):
    t, d = x.shape
    tiles_per_group = tokens_per_group // tm
    tile = pl.BlockSpec((tm, d), lambda i: (i, 0))
    return pl.pallas_call(
        _residual_kernel,
        grid=(t // tm,),
        in_specs=[tile, tile, pl.BlockSpec((1, 1, d), lambda i: (i // tiles_per_group, 0, 0))],
        out_specs=tile,
        out_shape=jax.ShapeDtypeStruct((t, d), F32),
        compiler_params=_cparams(("parallel",)),
        name="moe_residual",
    )(y, x, gate_mod)


def _assignment_tables(top_idx, rank, gates, counts):
    t = top_idx.shape[0]
    n_parts = t // MOE_PART
    n_assign = t * TOP_K
    part_blocks = MOE_PART * TOP_K // MOE_ROWS + N_EXPERTS + 1
    n_rows = (1 + n_parts * part_blocks) * MOE_ROWS
    padded = (counts + MOE_ROWS - 1) // MOE_ROWS * MOE_ROWS
    part_first = (1 + jnp.arange(n_parts, dtype=jnp.int32) * part_blocks) * MOE_ROWS
    row_start = part_first[:, None] + jnp.cumsum(padded, axis=1) - padded
    experts = jnp.arange(N_EXPERTS, dtype=jnp.int32)
    by_part = top_idx.reshape(n_parts, MOE_PART, TOP_K)
    start_of = jnp.sum(jnp.where(by_part[..., None] == experts, row_start[:, None, None, :], 0), axis=-1)
    dest = (start_of.reshape(t, TOP_K) + rank).reshape(-1).astype(jnp.int32)
    local = (jnp.arange(n_assign, dtype=jnp.int32) // TOP_K) % MOE_PART
    vals = jnp.stack([local, lax.bitcast_convert_type(gates.reshape(-1), jnp.int32)], axis=1)
    init = jnp.broadcast_to(jnp.array([MOE_PART, 0], jnp.int32), (n_rows, 2))
    table = init.at[dest].set(vals)
    row_loc = table[:, 0]
    row_gate = lax.bitcast_convert_type(table[:, 1], F32)
    return row_loc, row_gate, (padded // MOE_ROWS).reshape(-1).astype(jnp.int32), \
        (row_start // MOE_ROWS).reshape(-1).astype(jnp.int32)


def moe_layer(x, p, shift, scale, gate_mod, tokens_per_group):
    hp, idx, gates, rank, counts = moe_router(x, p['norm2'], shift, scale, p['rw_t'], p['rb_col'], tokens_per_group)
    row_loc, row_gate, n_blk, blk_start = _assignment_tables(idx, rank, gates, counts)
    y = moe_experts(hp, n_blk, blk_start, row_loc, row_gate, p['w_gu_bf'], p['b_gu'], p['w_down_bf'], p['b_down'])
    return gated_residual(y, x, gate_mod, tokens_per_group)


def _pad_lanes(a, width):
    return jnp.pad(a, [(0, 0)] * (a.ndim - 1) + [(0, width - a.shape[-1])])


def _prep_common(p):
    p['rw_t'] = _pad_lanes(p['router_w'], LANES).T
    p['rb_col'] = jnp.concatenate(
        [p['router_b'].astype(F32), jnp.full((LANES - N_EXPERTS,), NEG_BIG, F32)]).reshape(LANES, 1)
    p['w_out'] = p['w_out'].astype(BF16)
    p['w_gu_bf'] = p['w_gu'].astype(BF16)
    p['w_down_bf'] = p['w_down'].astype(BF16)
    return p


def _prep_layer0(p):
    p = _prep_common(dict(p))
    p['w_in'] = _pad_lanes(p['w_in'], AB_IN_PAD).astype(BF16)
    q_b = p['q_b'].reshape(MLA_Q_LORA, MLA_HEADS, MLA_QK)
    p['q_b_pad'] = _pad_lanes(q_b, LANES).reshape(MLA_Q_LORA, MLA_HEADS * LANES).astype(BF16)
    kv_b = p['kv_b'].reshape(MLA_KV_LORA, MLA_HEADS, MLA_NOPE + MLA_V)
    p['wk_pad'] = _pad_lanes(kv_b[:, :, :MLA_NOPE], LANES).reshape(MLA_KV_LORA, MLA_HEADS * LANES).astype(BF16)
    p['wv'] = kv_b[:, :, MLA_NOPE:].reshape(MLA_KV_LORA, MLA_HEADS * MLA_V).astype(BF16)
    p['q_norm_pad'] = _pad_lanes(p['q_norm'].reshape(1, -1), LANES)
    p['k_norm_pad'] = _pad_lanes(p['k_norm'].reshape(1, -1), LANES)
    return p


def _prep_layer1(p):
    p = _prep_common(dict(p))
    p['w_in'] = p['w_in'].astype(BF16)
    p['q_norm2'] = jnp.tile(p['q_norm'].reshape(1, -1), (1, LANES // SWA_HD))
    p['k_norm2'] = jnp.tile(p['k_norm'].reshape(1, -1), (1, LANES // SWA_HD))
    return p


def _group_forward(x3, mods, p0, p1, lb_logits, caches, latent):
    batch, seq, d = x3.shape
    t = batch * seq
    x = x3.reshape(t, d)
    tpg = seq if latent else t
    sh1, sc1, g1, sh2, sc2, g2 = mods[0]

    z = modnorm_matmul(x, p0['norm1'], sh1, sc1, p0['w_in'], tpg)
    tab_b = _rope_lane_tables(seq, MLA_ROPE, LANES, MLA_NOPE) if latent else None
    s0f, s0b = (caches['hg_f'], caches['hg_b']) if latent else (None, None)
    o_f, o_b, s_f, s_b = hgrn_bidir(z, lb_logits, s0f, s0b, batch, seq)
    q = mla_queries(z, p0['qa_norm'], p0['q_b_pad'], p0['q_norm_pad'], tab_b, seq)
    ckv, k, v = mla_keys_values(z, AB_COL_KVA // LANES, z, AB_COL_KPE // LANES, p0['kva_norm'], p0['wk_pad'], p0['wv'],
                                p0['k_norm_pad'], tab_b, seq, norm_input=True)
    n_k = seq
    if latent:
        n_ctx = caches['ckv'].shape[1]
        ckv_c = caches['ckv'].reshape(batch * n_ctx, MLA_KV_LORA)
        kpe_c = _pad_lanes(caches['kpe'].reshape(batch * n_ctx, MLA_ROPE), LANES)
        _, k_c, v_c = mla_keys_values(ckv_c, 0, kpe_c, 0, p0['kva_norm'], p0['wk_pad'], p0['wv'],
                                      p0['k_norm_pad'], None, n_ctx, norm_input=False)
        cat = lambda a, b: jnp.concatenate([a.reshape(batch, n_ctx, -1), b.reshape(batch, seq, -1)],
                                           axis=1).reshape(batch * (n_ctx + seq), -1)
        k, v = cat(k_c, k), cat(v_c, v)
        n_k = n_ctx + seq
    o_mla = mla_attention(q, k, v, batch, seq, n_k)
    x = out_proj_layer0(o_f, o_b, z, p0['hg_out_norm'], o_mla, p0['w_out'], x, g1, tpg)
    x = moe_layer(x, p0, sh2, sc2, g2, tpg)
    state0 = (s_f, s_b, ckv.reshape(batch, seq, MLA_KV_LORA), z[:, AB_COL_KPE:AB_COL_KPE + MLA_ROPE].reshape(batch, seq, MLA_ROPE))

    sh1, sc1, g1, sh2, sc2, g2 = mods[1]
    z = modnorm_matmul(x, p1['norm1'], sh1, sc1, p1['w_in'], tpg)
    tab_c = _rope_lane_tables(seq, SWA_HD, SWA_HD, 0) if latent else None
    q, k_cache, k, v = swa_prep(z, p1['q_norm2'], p1['k_norm2'], tab_c, seq)
    sink = p1['sink'].astype(F32)
    if latent:
        n_ctx = caches['k1'].shape[1]
        k_c = caches['k1'].reshape(batch, n_ctx, -1).astype(BF16)
        v_c = caches['v1'].reshape(batch, n_ctx, -1).astype(BF16)
        a = swa_window(q, k, v, k_c, v_c, sink, batch, seq, n_ctx)
    else:
        a = swa_dense(q, k, v, sink, batch, seq)
    x = out_proj_layer1(a, p1['w_out'], x, g1, tpg)
    x = moe_layer(x, p1, sh2, sc2, g2, tpg)
    nkv = SWA_KV_HEADS * SWA_HD
    state1 = (k_cache.reshape(batch, seq, SWA_KV_HEADS, SWA_HD),
              z[:, SWA_HEADS * SWA_HD + nkv:].reshape(batch, seq, SWA_KV_HEADS, SWA_HD))
    return x.reshape(batch, seq, d), state0, state1


def kernel(x_prompt, x_sample, state_l0_hgrn_fwd, state_l0_hgrn_bwd, cache_l0_mla_ckv, cache_l0_mla_kpe, cache_l1_k, cache_l1_v, c, c_ctx, hgrn_lb_logits, l0_ada_w, l0_ada_b, l0_norm1, l0_norm2, l0_w_in, l0_hg_out_norm, l0_qa_norm, l0_q_b, l0_kva_norm, l0_kv_b, l0_q_norm, l0_k_norm, l0_w_out, l0_router_w, l0_router_b, l0_w_gu, l0_b_gu, l0_w_down, l0_b_down, l1_ada_w, l1_ada_b, l1_norm1, l1_norm2, l1_w_in, l1_q_norm, l1_k_norm, l1_sink, l1_w_out, l1_router_w, l1_router_b, l1_w_gu, l1_b_gu, l1_w_down, l1_b_down):
    p0 = _prep_layer0(dict(norm1=l0_norm1, norm2=l0_norm2, w_in=l0_w_in, hg_out_norm=l0_hg_out_norm,
                           qa_norm=l0_qa_norm, q_b=l0_q_b, kva_norm=l0_kva_norm, kv_b=l0_kv_b,
                           q_norm=l0_q_norm, k_norm=l0_k_norm, w_out=l0_w_out, router_w=l0_router_w,
                           router_b=l0_router_b, w_gu=l0_w_gu, b_gu=l0_b_gu, w_down=l0_w_down,
                           b_down=l0_b_down))
    p1 = _prep_layer1(dict(norm1=l1_norm1, norm2=l1_norm2, w_in=l1_w_in, q_norm=l1_q_norm, k_norm=l1_k_norm,
                           sink=l1_sink, w_out=l1_w_out, router_w=l1_router_w, router_b=l1_router_b,
                           w_gu=l1_w_gu, b_gu=l1_b_gu, w_down=l1_w_down, b_down=l1_b_down))
    dec_batch = c.shape[0]
    d = c.shape[1]
    cond8 = jnp.concatenate([c_ctx[None, :], c, jnp.zeros((8 - 1 - dec_batch, d), F32)], axis=0)
    mods_ctx, mods_lat = [], []
    for w, b in ((l0_ada_w, l0_ada_b), (l1_ada_w, l1_ada_b)):
        mod = ada_params(cond8, w, b)
        mods_ctx.append([m.reshape(1, 1, d) for m in jnp.split(mod[0:1], 6, axis=-1)])
        mods_lat.append([m.reshape(dec_batch, 1, d) for m in jnp.split(mod[1:1 + dec_batch], 6, axis=-1)])

    y_prompt, st0, st1 = _group_forward(x_prompt, mods_ctx, p0, p1, hgrn_lb_logits, None, latent=False)
    caches = dict(hg_f=state_l0_hgrn_fwd, hg_b=state_l0_hgrn_bwd, ckv=cache_l0_mla_ckv, kpe=cache_l0_mla_kpe,
                  k1=cache_l1_k, v1=cache_l1_v)
    y_sample, _, _ = _group_forward(x_sample, mods_lat, p0, p1, hgrn_lb_logits, caches, latent=True)
    return (y_prompt, y_sample, st0[0], st0[1], st0[2], st0[3], st1[0], st1[1])
```

```python
import functools

import numpy as np
import jax
import jax.numpy as jnp
from jax import lax
from jax.experimental import pallas as pl
from jax.experimental.pallas import tpu as pltpu

F32 = jnp.float32
BF16 = jnp.bfloat16

D_MODEL = 1024
GRID_W = 64
ROPE_THETA = 10000.0
EPS = 1e-6
HG_HEADS = 4
HG_DK = 128
HG_DV = 128
MLA_HEADS = 8
MLA_NOPE = 64
MLA_ROPE = 32
MLA_V = 64
MLA_QK = MLA_NOPE + MLA_ROPE
MLA_Q_LORA = 256
MLA_KV_LORA = 128
MLA_SCALE = MLA_QK ** -0.5
SWA_HEADS = 16
SWA_KV_HEADS = 4
SWA_HD = 64
SWA_WINDOW = 128
SWA_SCALE = SWA_HD ** -0.5
SWA_GROUP = SWA_HEADS // SWA_KV_HEADS
N_EXPERTS = 32
TOP_K = 4
D_FF = 1024
SWIGLU_LIMIT = 7.0
SWIGLU_ALPHA = 1.702

LANES = 128
HG_CHUNK = 128
HG_LEVELS = 7
HG_MXU_LEVELS = 0
AB_IN_PAD = 3072
AB_COL_GATE = 3 * HG_HEADS * HG_DK + HG_HEADS * HG_DV
AB_COL_QA = AB_COL_GATE + HG_HEADS * HG_DV
AB_COL_KVA = AB_COL_QA + MLA_Q_LORA
AB_COL_KPE = AB_COL_KVA + MLA_KV_LORA
MOE_ROWS = 256
NEG_BIG = -1e30
VMEM_LIMIT = 48 * 1024 * 1024
EXPERT_VMEM_LIMIT = 56 * 1024 * 1024


def _cparams(sem):
    return pltpu.CompilerParams(dimension_semantics=sem, vmem_limit_bytes=VMEM_LIMIT)


def _dot(a, b):
    return jnp.dot(a, b, preferred_element_type=F32)


def _dot_nt(a, b):
    return lax.dot_general(a, b, (((1,), (1,)), ((), ())), preferred_element_type=F32)


def _dot_tn(a, b):
    return lax.dot_general(a, b, (((0,), (0,)), ((), ())), preferred_element_type=F32)


def _split2(x):
    hi = x.astype(BF16)
    lo = (x - hi.astype(F32)).astype(BF16)
    return hi, lo


def _dot_hp(a, b):
    ah, al = _split2(a)
    bh, bl = _split2(b)
    return _dot(ah, bh) + _dot(ah, bl) + _dot(al, bh)


def _sigmoid(x):
    return 1.0 / (1.0 + jnp.exp(-x))


def _modnorm(x, gain, shift, scale):
    y = x * lax.rsqrt(jnp.mean(x * x, axis=-1, keepdims=True) + EPS)
    return y * gain * (1.0 + scale) + shift


def _ada_kernel(c_ref, w_ref, b_ref, o_ref):
    c = c_ref[...]
    o_ref[...] = _dot_hp(c * _sigmoid(c), w_ref[...]) + b_ref[...]


def ada_params(cond8, w, b):
    n = w.shape[1]
    tn = 1024
    return pl.pallas_call(
        _ada_kernel,
        grid=(n // tn,),
        in_specs=[pl.BlockSpec((8, D_MODEL), lambda j: (0, 0)),
                  pl.BlockSpec((D_MODEL, tn), lambda j: (0, j)),
                  pl.BlockSpec((1, tn), lambda j: (0, j))],
        out_specs=pl.BlockSpec((8, tn), lambda j: (0, j)),
        out_shape=jax.ShapeDtypeStruct((8, n), F32),
        compiler_params=_cparams(("parallel",)),
        name="ada_params",
    )(cond8, w, b.reshape(1, n))


def _modnorm_matmul_kernel(x_ref, g_ref, sh_ref, sc_ref, w_ref, o_ref):
    h = _modnorm(x_ref[...], g_ref[...], sh_ref[0], sc_ref[0])
    o_ref[...] = _dot(h.astype(BF16), w_ref[...])


def modnorm_matmul(x, gain, shift, scale, w_bf16, tokens_per_group, tm=512):
    t, d = x.shape
    n = w_bf16.shape[1]
    tiles_per_group = tokens_per_group // tm
    grp = lambda i: (i // tiles_per_group, 0, 0)
    return pl.pallas_call(
        _modnorm_matmul_kernel,
        grid=(t // tm,),
        in_specs=[pl.BlockSpec((tm, d), lambda i: (i, 0)),
                  pl.BlockSpec((1, d), lambda i: (0, 0)),
                  pl.BlockSpec((1, 1, d), grp),
                  pl.BlockSpec((1, 1, d), grp),
                  pl.BlockSpec((d, n), lambda i: (0, 0))],
        out_specs=pl.BlockSpec((tm, n), lambda i: (i, 0)),
        out_shape=jax.ShapeDtypeStruct((t, n), F32),
        compiler_params=_cparams(("parallel",)),
        name="modnorm_matmul",
    )(x, gain.reshape(1, d), shift, scale, w_bf16)


def _hgrn_constants():
    c = HG_CHUNK
    t = np.arange(c)[:, None]
    u = np.arange(c)[None, :]
    mats = [(u <= t), (u > t)]
    for l in range(HG_MXU_LEVELS):
        m = 1 << l
        r = (t // (2 * m)) * (2 * m) + m - 1
        mats.append((u > np.minimum(t, r)) & (u <= np.maximum(t, r)))
    fwd = np.concatenate(mats, axis=0).astype(np.float32)
    bwd = np.concatenate([mm[::-1, ::-1] for mm in mats], axis=0).astype(np.float32)
    x = np.bitwise_xor(t, u)
    lvl = np.where(x > 0, np.floor(np.log2(np.maximum(x, 1))), HG_LEVELS).astype(np.int32)
    lv_f = np.where(t >= u, lvl, -1).astype(np.int32)
    return fwd, bwd, lv_f, lv_f.T.copy()


def _hgrn_chunk(qs, fpres, vs, lbs, sts, forwards, mcats, lvs):
    c = HG_CHUNK
    n = len(qs)
    fs = [lb + (1.0 - lb) * _sigmoid(fp) for lb, fp in zip(lbs, fpres)]
    kks = [1.0 - f for f in fs]
    logs = [_split2(jnp.log(f)) for f in fs]
    xs = [None] * n
    for fwd in (True, False):
        ids = [i for i in range(n) if forwards[i] == fwd]
        parts = [half for i in ids for half in logs[i]]
        x_all = _dot(mcats[0 if fwd else 1], jnp.concatenate(parts, axis=1))
        for j, i in enumerate(ids):
            xs[i] = x_all[:, 2 * j * c:(2 * j + 1) * c] + x_all[:, (2 * j + 1) * c:(2 * j + 2) * c]
    gs = [x[0:c] for x in xs]
    lv_of = [lvs[0 if f else 1] for f in forwards]
    qbs = [q.astype(BF16) for q in qs]
    kbs = [kk.astype(BF16) for kk in kks]
    vbs = [v.astype(BF16) for v in vs]
    os_ = [_dot_nt((q * jnp.exp(g)).astype(BF16), st.astype(BF16)) for q, g, st in zip(qs, gs, sts)]
    accs = [jnp.where(lv == HG_LEVELS, _dot_nt(qb, kb), 0.0) for lv, qb, kb in zip(lv_of, qbs, kbs)]
    for l in range(HG_LEVELS):
        for i in range(n):
            if l < HG_MXU_LEVELS:
                x = xs[i][(2 + l) * c:(3 + l) * c]
            else:
                m = 1 << l
                ref_rows = [j * 2 * m + (m - 1 if forwards[i] else m) for j in range(c // (2 * m))]
                g_ref = jnp.concatenate(
                    [jnp.broadcast_to(gs[i][r:r + 1, :], (2 * m, c)) for r in ref_rows], axis=0)
                x = -jnp.abs(gs[i] - g_ref)
            e = jnp.exp(x)
            p = _dot_nt((qs[i] * e).astype(BF16), (kks[i] * e).astype(BF16))
            accs[i] = jnp.where(lv_of[i] == l, p, accs[i])
    outs, new_sts = [], []
    for i in range(n):
        edge_row = c - 1 if forwards[i] else 0
        outs.append(os_[i] + _dot(accs[i].astype(BF16), vbs[i]))
        k_end = (kks[i] * jnp.exp(xs[i][c:2 * c])).astype(BF16)
        new_sts.append(sts[i] * jnp.exp(gs[i][edge_row:edge_row + 1, :]) + _dot_tn(vbs[i], k_end))
    return outs, new_sts


def _hgrn_kernel(*refs, has_init):
    if has_init:
        (qf_ref, qb_ref, ff_ref, fb_ref, vf_ref, vb_ref, lbl_ref, mf_ref, mb_ref, lvf_ref, lvb_ref,
         s0f_ref, s0b_ref, of_ref, ob_ref, sf_ref, sb_ref, stf, stb) = refs
    else:
        (qf_ref, qb_ref, ff_ref, fb_ref, vf_ref, vb_ref, lbl_ref, mf_ref, mb_ref, lvf_ref, lvb_ref,
         of_ref, ob_ref, sf_ref, sb_ref, stf, stb) = refs
    c = pl.program_id(1)
    nc = pl.num_programs(1)

    @pl.when(c == 0)
    def _():
        for h in range(HG_HEADS):
            if has_init:
                stf[h] = s0f_ref[0, h].T
                stb[h] = s0b_ref[0, h].T
            else:
                stf[h] = jnp.zeros((HG_DV, HG_DK), F32)
                stb[h] = jnp.zeros((HG_DV, HG_DK), F32)

    rows = [lbl_ref[:, j, :] for j in range(lbl_ref.shape[1])]
    mx = functools.reduce(jnp.maximum, rows)
    ex = [jnp.exp(r - mx) for r in rows]
    lb = ex[0] / functools.reduce(lambda a, b: a + b, ex)

    heads = [slice(h * LANES, (h + 1) * LANES) for h in range(HG_HEADS)]
    nh = HG_HEADS
    outs, sts = _hgrn_chunk(
        [qf_ref[0, :, hs] for hs in heads] + [qb_ref[0, :, hs] for hs in heads],
        [ff_ref[0, :, hs] for hs in heads] + [fb_ref[0, :, hs] for hs in heads],
        [vf_ref[0, :, hs] for hs in heads] + [vb_ref[0, :, hs] for hs in heads],
        [lb[0:1, hs] for hs in heads] + [lb[1:2, hs] for hs in heads],
        [stf[h] for h in range(nh)] + [stb[h] for h in range(nh)],
        [True] * nh + [False] * nh,
        (mf_ref[...], mb_ref[...]), (lvf_ref[...], lvb_ref[...]))
    o_f, o_b, st_f, st_b = outs[:nh], outs[nh:], sts[:nh], sts[nh:]
    for h, hs in enumerate(heads):
        of_ref[0, :, hs] = o_f[h]
        ob_ref[0, :, hs] = o_b[h]
        stf[h] = st_f[h]
        stb[h] = st_b[h]

    @pl.when(c == nc - 1)
    def _():
        for h in range(HG_HEADS):
            sf_ref[0, h] = stf[h].T
            sb_ref[0, h] = stb[h].T


def hgrn_bidir(z, lb_logits, s0f, s0b, batch, seq):
    nc = seq // HG_CHUNK
    z3 = z.reshape(batch, seq, z.shape[1])
    mf, mb, lvf, lvb = _hgrn_constants()
    has_init = s0f is not None
    width = HG_HEADS * LANES
    blk = (1, HG_CHUNK, width)
    fwd = lambda off: pl.BlockSpec(blk, lambda b, c: (b, c, off))
    bwd = lambda off: pl.BlockSpec(blk, lambda b, c: (b, nc - 1 - c, off))
    full = lambda a: pl.BlockSpec(a.shape, lambda b, c: (0,) * a.ndim)
    st_spec = pl.BlockSpec((1, HG_HEADS, HG_DK, HG_DV), lambda b, c: (b, 0, 0, 0))
    consts = [jnp.asarray(mf, BF16), jnp.asarray(mb, BF16), jnp.asarray(lvf), jnp.asarray(lvb)]
    in_specs = [fwd(0), bwd(0), fwd(1), bwd(2), fwd(3), bwd(3),
                pl.BlockSpec(lb_logits.shape, lambda b, c: (0, 0, 0))]
    in_specs += [full(a) for a in consts]
    args = [z3] * 6 + [lb_logits] + consts
    if has_init:
        in_specs += [st_spec, st_spec]
        args += [s0f, s0b]
    o_shape = jax.ShapeDtypeStruct((batch, seq, width), F32)
    s_shape = jax.ShapeDtypeStruct((batch, HG_HEADS, HG_DK, HG_DV), F32)
    o_f, o_b, s_f, s_b = pl.pallas_call(
        functools.partial(_hgrn_kernel, has_init=has_init),
        grid=(batch, nc),
        in_specs=in_specs,
        out_specs=[pl.BlockSpec(blk, lambda b, c: (b, c, 0)),
                   pl.BlockSpec(blk, lambda b, c: (b, nc - 1 - c, 0)),
                   st_spec, st_spec],
        out_shape=[o_shape, o_shape, s_shape, s_shape],
        scratch_shapes=[pltpu.VMEM((HG_HEADS, HG_DV, HG_DK), F32), pltpu.VMEM((HG_HEADS, HG_DV, HG_DK), F32)],
        compiler_params=_cparams(("parallel", "arbitrary")),
        name="hgrn_bidir",
    )(*args)
    t = batch * seq
    return o_f.reshape(t, -1), o_b.reshape(t, -1), s_f, s_b


def _axial_tables(n_tokens, n_rot):
    t = jnp.arange(n_tokens)
    row = (t // GRID_W).astype(F32)
    col = (t % GRID_W).astype(F32)
    n_freq = n_rot // 4
    inv = jnp.power(ROPE_THETA, -jnp.arange(n_freq, dtype=F32) / n_freq)
    ang = jnp.concatenate([row[:, None] * inv, col[:, None] * inv], axis=-1)
    return jnp.cos(ang), jnp.sin(ang)


def _rope_lane_tables(n_tokens, n_rot, head_width, first_rot_lane):
    cos, sin = _axial_tables(n_tokens, n_rot)
    half = n_rot // 2
    c_head = jnp.ones((n_tokens, head_width), F32)
    sa_head = jnp.zeros((n_tokens, head_width), F32)
    sb_head = jnp.zeros((n_tokens, head_width), F32)
    a0, a1, a2 = first_rot_lane, first_rot_lane + half, first_rot_lane + n_rot
    c_head = c_head.at[:, a0:a1].set(cos).at[:, a1:a2].set(cos)
    sa_head = sa_head.at[:, a0:a1].set(-sin)
    sb_head = sb_head.at[:, a1:a2].set(sin)
    reps = LANES // head_width
    tile = lambda a: jnp.tile(a, (1, reps))
    return tile(c_head), tile(sa_head), tile(sb_head)


def _rope(x, c, sa, sb, half):
    return x * c + pltpu.roll(x, LANES - half, 1) * sa + pltpu.roll(x, half, 1) * sb


def _mla_q_kernel(*refs, rope):
    if rope:
        qa_ref, qan_ref, qb_ref, qn_ref, c_ref, sa_ref, sb_ref, o_ref = refs
    else:
        qa_ref, qan_ref, qb_ref, qn_ref, o_ref = refs
    qa = qa_ref[...]
    qn = qa * lax.rsqrt(jnp.mean(qa * qa, axis=-1, keepdims=True) + EPS) * qan_ref[...]
    qfull = _dot(qn.astype(BF16), qb_ref[...])
    outs = []
    for h in range(MLA_HEADS):
        qh = qfull[:, h * LANES:(h + 1) * LANES]
        ms = jnp.sum(qh * qh, axis=-1, keepdims=True) * (1.0 / MLA_QK)
        qh = qh * lax.rsqrt(ms + EPS) * qn_ref[...]
        if rope:
            qh = _rope(qh, c_ref[...], sa_ref[...], sb_ref[...], MLA_ROPE // 2)
        outs.append((qh * MLA_SCALE).astype(BF16))
    o_ref[...] = jnp.concatenate(outs, axis=1)


def mla_queries(z, qa_norm, q_b_pad, q_norm_pad, tables, tokens_per_batch, tm=512):
    t = z.shape[0]
    rope = tables is not None
    row = lambda i: (0, 0)
    in_specs = [pl.BlockSpec((tm, MLA_Q_LORA), lambda i: (i, AB_COL_QA // MLA_Q_LORA)),
                pl.BlockSpec((1, MLA_Q_LORA), row),
                pl.BlockSpec(q_b_pad.shape, row),
                pl.BlockSpec((1, LANES), row)]
    args = [z, qa_norm.reshape(1, -1), q_b_pad, q_norm_pad]
    if rope:
        tpb = tokens_per_batch // tm
        in_specs += [pl.BlockSpec((tm, LANES), lambda i: (i % tpb, 0))] * 3
        args += list(tables)
    return pl.pallas_call(
        functools.partial(_mla_q_kernel, rope=rope),
        grid=(t // tm,),
        in_specs=in_specs,
        out_specs=pl.BlockSpec((tm, MLA_HEADS * LANES), lambda i: (i, 0)),
        out_shape=jax.ShapeDtypeStruct((t, MLA_HEADS * LANES), BF16),
        compiler_params=_cparams(("parallel",)),
        name="mla_queries",
    )(*args)


def _mla_kv_kernel(*refs, norm_input, rope):
    if rope:
        kva_ref, kpe_ref, kvan_ref, wk_ref, wv_ref, kn_ref, c_ref, sa_ref, sb_ref, ckv_ref, k_ref, v_ref = refs
    else:
        kva_ref, kpe_ref, kvan_ref, wk_ref, wv_ref, kn_ref, ckv_ref, k_ref, v_ref = refs
    ckv = kva_ref[...]
    if norm_input:
        ckv = ckv * lax.rsqrt(jnp.mean(ckv * ckv, axis=-1, keepdims=True) + EPS) * kvan_ref[...]
    ckv_ref[...] = ckv
    cb = ckv.astype(BF16)
    knope = _dot(cb, wk_ref[...])
    v_ref[...] = _dot(cb, wv_ref[...]).astype(BF16)
    kpe = pltpu.roll(kpe_ref[...], MLA_NOPE, 1)
    outs = []
    for h in range(MLA_HEADS):
        kh = knope[:, h * LANES:(h + 1) * LANES] + kpe
        ms = jnp.sum(kh * kh, axis=-1, keepdims=True) * (1.0 / MLA_QK)
        kh = kh * lax.rsqrt(ms + EPS) * kn_ref[...]
        if rope:
            kh = _rope(kh, c_ref[...], sa_ref[...], sb_ref[...], MLA_ROPE // 2)
        outs.append(kh.astype(BF16))
    k_ref[...] = jnp.concatenate(outs, axis=1)


def mla_keys_values(kva_src, kva_col, kpe_src, kpe_col, kva_norm, wk_pad, wv, k_norm_pad, tables,
                    tokens_per_batch, norm_input, tm=512):
    t = kva_src.shape[0]
    rope = tables is not None
    row = lambda i: (0, 0)
    in_specs = [pl.BlockSpec((tm, LANES), lambda i: (i, kva_col)),
                pl.BlockSpec((tm, LANES), lambda i: (i, kpe_col)),
                pl.BlockSpec((1, LANES), row),
                pl.BlockSpec(wk_pad.shape, row),
                pl.BlockSpec(wv.shape, row),
                pl.BlockSpec((1, LANES), row)]
    args = [kva_src, kpe_src, kva_norm.reshape(1, -1), wk_pad, wv, k_norm_pad]
    if rope:
        tpb = tokens_per_batch // tm
        in_specs += [pl.BlockSpec((tm, LANES), lambda i: (i % tpb, 0))] * 3
        args += list(tables)
    return pl.pallas_call(
        functools.partial(_mla_kv_kernel, norm_input=norm_input, rope=rope),
        grid=(t // tm,),
        in_specs=in_specs,
        out_specs=[pl.BlockSpec((tm, LANES), lambda i: (i, 0)),
                   pl.BlockSpec((tm, MLA_HEADS * LANES), lambda i: (i, 0)),
                   pl.BlockSpec((tm, MLA_HEADS * MLA_V), lambda i: (i, 0))],
        out_shape=[jax.ShapeDtypeStruct((t, LANES), F32),
                   jax.ShapeDtypeStruct((t, MLA_HEADS * LANES), BF16),
                   jax.ShapeDtypeStruct((t, MLA_HEADS * MLA_V), BF16)],
        compiler_params=_cparams(("parallel",)),
        name="mla_keys_values",
    )(*args)


def _mla_attn_kernel(q_ref, k_ref, vt_ref, o_ref):
    heads = range(2)
    ss = [_dot_nt(k_ref[0][:, j * LANES:(j + 1) * LANES], q_ref[0][:, j * LANES:(j + 1) * LANES]) for j in heads]
    ms = [jnp.max(s, axis=0, keepdims=True) for s in ss]
    ps = [jnp.exp(s - m) for s, m in zip(ss, ms)]
    ls = [jnp.sum(p, axis=0, keepdims=True) for p in ps]
    ots = [_dot(vt_ref[0][j * MLA_V:(j + 1) * MLA_V, :], ps[j].astype(BF16)) / ls[j] for j in heads]
    o_ref[0] = jnp.concatenate(ots, axis=0).T


def mla_attention(q, k, v, batch, n_q, n_k, tq=512):
    tq = min(tq, n_q)
    q3 = q.reshape(batch, n_q, -1)
    k3 = k.reshape(batch, n_k, -1)
    vt = jnp.swapaxes(v.reshape(batch, n_k, -1), 1, 2)
    out = pl.pallas_call(
        _mla_attn_kernel,
        grid=(batch, MLA_HEADS // 2, n_q // tq),
        in_specs=[pl.BlockSpec((1, tq, 2 * LANES), lambda b, j, i: (b, i, j)),
                  pl.BlockSpec((1, n_k, 2 * LANES), lambda b, j, i: (b, 0, j)),
                  pl.BlockSpec((1, 2 * MLA_V, n_k), lambda b, j, i: (b, j, 0))],
        out_specs=pl.BlockSpec((1, tq, 2 * MLA_V), lambda b, j, i: (b, i, j)),
        out_shape=jax.ShapeDtypeStruct((batch, n_q, MLA_HEADS * MLA_V), F32),
        compiler_params=_cparams(("parallel", "parallel", "arbitrary")),
        name="mla_attention",
    )(q3, k3, vt)
    return out.reshape(batch * n_q, -1)


def _out0_kernel(of_ref, ob_ref, ag_ref, hgn_ref, om_ref, w_ref, x_ref, g_ref, o_ref):
    o = of_ref[...] + ob_ref[...]
    ag = ag_ref[...]
    parts = []
    for h in range(HG_HEADS):
        oh = o[:, h * HG_DV:(h + 1) * HG_DV]
        oh = oh * lax.rsqrt(jnp.mean(oh * oh, axis=-1, keepdims=True) + EPS) * hgn_ref[...]
        gh = ag[:, h * HG_DV:(h + 1) * HG_DV]
        parts.append((oh * (gh * _sigmoid(gh))).astype(BF16))
    oa = jnp.concatenate(parts, axis=1)
    n_a = HG_HEADS * HG_DV
    mix = _dot(oa, w_ref[0:n_a, :]) + _dot(om_ref[...].astype(BF16), w_ref[n_a:, :])
    o_ref[...] = x_ref[...] + g_ref[0] * mix


def out_proj_layer0(o_f, o_b, z, hg_norm, o_mla, w_out_bf16, x, gate, tokens_per_group, tm=512):
    t, d = x.shape
    n_a = HG_HEADS * HG_DV
    tiles_per_group = tokens_per_group // tm
    tile = lambda w: pl.BlockSpec((tm, w), lambda i: (i, 0))
    return pl.pallas_call(
        _out0_kernel,
        grid=(t // tm,),
        in_specs=[tile(n_a), tile(n_a),
                  pl.BlockSpec((tm, n_a), lambda i: (i, AB_COL_GATE // n_a)),
                  pl.BlockSpec((1, HG_DV), lambda i: (0, 0)),
                  tile(o_mla.shape[1]),
                  pl.BlockSpec(w_out_bf16.shape, lambda i: (0, 0)),
                  tile(d),
                  pl.BlockSpec((1, 1, d), lambda i: (i // tiles_per_group, 0, 0))],
        out_specs=tile(d),
        out_shape=jax.ShapeDtypeStruct((t, d), F32),
        compiler_params=_cparams(("parallel",)),
        name="out_proj_layer0",
    )(o_f, o_b, z, hg_norm.reshape(1, -1), o_mla, w_out_bf16, x, gate)


def _out1_kernel(a_ref, w_ref, x_ref, g_ref, o_ref):
    o_ref[...] = x_ref[...] + g_ref[0] * _dot(a_ref[...].astype(BF16), w_ref[...])


def out_proj_layer1(a, w_out_bf16, x, gate, tokens_per_group, tm=512):
    t, d = x.shape
    tiles_per_group = tokens_per_group // tm
    tile = lambda w: pl.BlockSpec((tm, w), lambda i: (i, 0))
    return pl.pallas_call(
        _out1_kernel,
        grid=(t // tm,),
        in_specs=[tile(a.shape[1]),
                  pl.BlockSpec(w_out_bf16.shape, lambda i: (0, 0)),
                  tile(d),
                  pl.BlockSpec((1, 1, d), lambda i: (i // tiles_per_group, 0, 0))],
        out_specs=tile(d),
        out_shape=jax.ShapeDtypeStruct((t, d), F32),
        compiler_params=_cparams(("parallel",)),
        name="out_proj_layer1",
    )(a, w_out_bf16, x, gate)


def _head_rms(x, gain2):
    sq = x * x
    lane = lax.broadcasted_iota(jnp.int32, x.shape, 1)
    first = lane < SWA_HD
    lo = jnp.sum(jnp.where(first, sq, 0.0), axis=-1, keepdims=True)
    hi = jnp.sum(jnp.where(first, 0.0, sq), axis=-1, keepdims=True)
    ms = jnp.where(first, lo, hi) * (1.0 / SWA_HD)
    return x * lax.rsqrt(ms + EPS) * gain2


def _swa_prep_kernel(*refs, rope):
    if rope:
        zq_ref, zk_ref, zv_ref, qn_ref, kn_ref, c_ref, sa_ref, sb_ref, q_ref, kc_ref, k_ref, v_ref = refs
    else:
        zq_ref, zk_ref, zv_ref, qn_ref, kn_ref, q_ref, kc_ref, k_ref, v_ref = refs
    half = SWA_HD // 2

    def rot(x):
        return _rope(x, c_ref[...], sa_ref[...], sb_ref[...], half) if rope else x

    zq = zq_ref[...]
    qs = []
    for p in range(zq.shape[1] // LANES):
        x = _head_rms(zq[:, p * LANES:(p + 1) * LANES], qn_ref[...])
        qs.append((rot(x) * SWA_SCALE).astype(BF16))
    q_ref[...] = jnp.concatenate(qs, axis=1)
    zk = zk_ref[...]
    kn, kr = [], []
    for p in range(zk.shape[1] // LANES):
        x = _head_rms(zk[:, p * LANES:(p + 1) * LANES], kn_ref[...])
        kn.append(x)
        kr.append(rot(x).astype(BF16))
    kc_ref[...] = jnp.concatenate(kn, axis=1)
    k_ref[...] = jnp.concatenate(kr, axis=1)
    v_ref[...] = zv_ref[...].astype(BF16)


def swa_prep(z, q_norm2, k_norm2, tables, tokens_per_batch, tm=512):
    t = z.shape[0]
    nq = SWA_HEADS * SWA_HD
    nkv = SWA_KV_HEADS * SWA_HD
    rope = tables is not None
    row = lambda i: (0, 0)
    in_specs = [pl.BlockSpec((tm, nq), lambda i: (i, 0)),
                pl.BlockSpec((tm, nkv), lambda i: (i, nq // nkv)),
                pl.BlockSpec((tm, nkv), lambda i: (i, nq // nkv + 1)),
                pl.BlockSpec((1, LANES), row),
                pl.BlockSpec((1, LANES), row)]
    args = [z, z, z, q_norm2, k_norm2]
    if rope:
        tpb = tokens_per_batch // tm
        in_specs += [pl.BlockSpec((tm, LANES), lambda i: (i % tpb, 0))] * 3
        args += list(tables)
    tile = lambda w: pl.BlockSpec((tm, w), lambda i: (i, 0))
    return pl.pallas_call(
        functools.partial(_swa_prep_kernel, rope=rope),
        grid=(t // tm,),
        in_specs=in_specs,
        out_specs=[tile(nq), tile(nkv), tile(nkv), tile(nkv)],
        out_shape=[jax.ShapeDtypeStruct((t, nq), BF16),
                   jax.ShapeDtypeStruct((t, nkv), F32),
                   jax.ShapeDtypeStruct((t, nkv), BF16),
                   jax.ShapeDtypeStruct((t, nkv), BF16)],
        compiler_params=_cparams(("parallel",)),
        name="swa_prep",
    )(*args)


def _sink_attention_t(q8, k_all, vt_all, bias, sink_ref, pair):
    tq = q8.shape[0]
    n_q = SWA_GROUP * tq
    lane = lax.broadcasted_iota(jnp.int32, (1, n_q), 1)
    kv = range(2)
    heads = [[n * SWA_GROUP + g for g in range(SWA_GROUP)] for n in kv]
    q4s = [jnp.concatenate([q8[:, hq * SWA_HD:(hq + 1) * SWA_HD] for hq in heads[n]], axis=0) for n in kv]
    sks = []
    for n in kv:
        sk = jnp.zeros((1, n_q), F32)
        for g, hq in enumerate(heads[n]):
            sk = jnp.where(lane // tq == g, sink_ref[pair * 2 * SWA_GROUP + hq], sk)
        sks.append(sk)
    ss = [_dot_nt(k_all[:, n * SWA_HD:(n + 1) * SWA_HD], q4s[n]) for n in kv]
    if bias is not None:
        ss = [s + bias for s in ss]
    ms = [jnp.maximum(jnp.max(ss[n], axis=0, keepdims=True), sks[n]) for n in kv]
    ps = [jnp.exp(ss[n] - ms[n]) for n in kv]
    ls = [jnp.sum(ps[n], axis=0, keepdims=True) + jnp.exp(sks[n] - ms[n]) for n in kv]
    ots = [_dot(vt_all[n * SWA_HD:(n + 1) * SWA_HD, :], ps[n].astype(BF16)) / ls[n] for n in kv]
    pairs = []
    for n in kv:
        for g in range(0, SWA_GROUP, 2):
            two = jnp.concatenate([ots[n][:, g * tq:(g + 1) * tq], ots[n][:, (g + 1) * tq:(g + 2) * tq]], axis=0)
            pairs.append(two.T)
    return jnp.concatenate(pairs, axis=1)


def _swa_dense_kernel(sink_ref, q_ref, k_ref, vt_ref, o_ref):
    o_ref[0] = _sink_attention_t(q_ref[0], k_ref[0], vt_ref[0], None, sink_ref, pl.program_id(1))


def swa_dense(q, k, v, sink, batch, seq):
    nq = SWA_HEADS * SWA_HD
    q3 = q.reshape(batch, seq, nq)
    k3 = k.reshape(batch, seq, -1)
    vt = jnp.swapaxes(v.reshape(batch, seq, -1), 1, 2)
    out = pl.pallas_call(
        _swa_dense_kernel,
        grid=(batch, 2),
        in_specs=[pl.BlockSpec(memory_space=pltpu.SMEM),
                  pl.BlockSpec((1, seq, nq // 2), lambda b, j: (b, 0, j)),
                  pl.BlockSpec((1, seq, LANES), lambda b, j: (b, 0, j)),
                  pl.BlockSpec((1, LANES, seq), lambda b, j: (b, j, 0))],
        out_specs=pl.BlockSpec((1, seq, nq // 2), lambda b, j: (b, 0, j)),
        out_shape=jax.ShapeDtypeStruct((batch, seq, nq), F32),
        compiler_params=_cparams(("parallel", "parallel")),
        name="swa_dense",
    )(sink, q3, k3, vt)
    return out.reshape(batch * seq, nq)


def _swa_window_kernel(sink_ref, q_ref, kc_ref, kp_ref, k0_ref, kn_ref, vc_ref, vp_ref, v0_ref, vn_ref, o_ref):
    i = pl.program_id(2)
    nb = pl.num_programs(2)
    w = SWA_WINDOW
    n_ctx = kc_ref.shape[1]
    k_all = jnp.concatenate([kc_ref[0], kp_ref[0], k0_ref[0], kn_ref[0]], axis=0)
    vt_all = jnp.concatenate([vc_ref[0], vp_ref[0], v0_ref[0], vn_ref[0]], axis=1)
    shape = (n_ctx + 3 * w, SWA_GROUP * w)
    key = lax.broadcasted_iota(jnp.int32, shape, 0) - n_ctx
    r = lax.broadcasted_iota(jnp.int32, shape, 1) % w
    prev_bias = jnp.where(i > 0, 0.0, NEG_BIG)
    next_bias = jnp.where(i < nb - 1, 0.0, NEG_BIG)
    bias = jnp.where(
        key < w,
        jnp.where(key < 0, 0.0, jnp.where(key >= r, prev_bias, NEG_BIG)),
        jnp.where(key < 2 * w, 0.0, jnp.where(key - 2 * w <= r, next_bias, NEG_BIG)))
    o_ref[0] = _sink_attention_t(q_ref[0], k_all, vt_all, bias, sink_ref, pl.program_id(1))


def swa_window(q, k, v, k_ctx, v_ctx, sink, batch, seq, n_ctx):
    nq = SWA_HEADS * SWA_HD
    w = SWA_WINDOW
    nb = seq // w
    q3 = q.reshape(batch, seq, nq)
    k3 = k.reshape(batch, seq, -1)
    vt = jnp.swapaxes(v.reshape(batch, seq, -1), 1, 2)
    vt_ctx = jnp.swapaxes(v_ctx, 1, 2)
    ctx = pl.BlockSpec((1, n_ctx, LANES), lambda b, j, i: (b, 0, j))
    prv = pl.BlockSpec((1, w, LANES), lambda b, j, i: (b, jnp.maximum(i - 1, 0), j))
    cur = pl.BlockSpec((1, w, LANES), lambda b, j, i: (b, i, j))
    nxt = pl.BlockSpec((1, w, LANES), lambda b, j, i: (b, jnp.minimum(i + 1, nb - 1), j))
    ctx_t = pl.BlockSpec((1, LANES, n_ctx), lambda b, j, i: (b, j, 0))
    prv_t = pl.BlockSpec((1, LANES, w), lambda b, j, i: (b, j, jnp.maximum(i - 1, 0)))
    cur_t = pl.BlockSpec((1, LANES, w), lambda b, j, i: (b, j, i))
    nxt_t = pl.BlockSpec((1, LANES, w), lambda b, j, i: (b, j, jnp.minimum(i + 1, nb - 1)))
    out = pl.pallas_call(
        _swa_window_kernel,
        grid=(batch, 2, nb),
        in_specs=[pl.BlockSpec(memory_space=pltpu.SMEM),
                  pl.BlockSpec((1, w, nq // 2), lambda b, j, i: (b, i, j)),
                  ctx, prv, cur, nxt, ctx_t, prv_t, cur_t, nxt_t],
        out_specs=pl.BlockSpec((1, w, nq // 2), lambda b, j, i: (b, i, j)),
        out_shape=jax.ShapeDtypeStruct((batch, seq, nq), F32),
        compiler_params=_cparams(("parallel", "parallel", "arbitrary")),
        name="swa_window",
    )(sink, q3, k_ctx, k3, k3, k3, vt_ctx, vt, vt, vt)
    return out.reshape(batch * seq, nq)


def _router_kernel(x_ref, g_ref, sh_ref, sc_ref, rw_ref, rb_ref, tri_ref, h_ref, idx_ref, gate_ref, rank_ref,
                   cnt_ref, cnt, *, tiles_per_part):
    @pl.when(pl.program_id(0) % tiles_per_part == 0)
    def _():
        cnt[...] = jnp.zeros_like(cnt)

    h = _modnorm(x_ref[...], g_ref[...], sh_ref[0], sc_ref[0])
    bits = lax.bitcast_convert_type(h.astype(BF16).astype(F32), jnp.uint32)
    half = h.shape[1] // 2
    h_ref[...] = (bits[:, :half] >> 16) | (bits[:, half:] & jnp.uint32(0xFFFF0000))
    wh, wl = _split2(rw_ref[...])
    hh, hl = _split2(h)
    logits = _dot_nt(wh, hh) + _dot_nt(wh, hl) + _dot_nt(wl, hh) + rb_ref[...]
    row = lax.broadcasted_iota(jnp.int32, logits.shape, 0)
    work = logits
    vals, idxs = [], []
    for _ in range(TOP_K):
        m = jnp.max(work, axis=0, keepdims=True)
        ik = jnp.min(jnp.where(work == m, row, LANES), axis=0, keepdims=True)
        vals.append(m)
        idxs.append(ik)
        work = jnp.where(row == ik, 2.0 * NEG_BIG, work)
    es = [jnp.exp(v - vals[0]) for v in vals]
    denom = es[0] + es[1] + es[2] + es[3]
    chosen = jnp.zeros(logits.shape, F32)
    for k in range(TOP_K):
        chosen = jnp.where(row == idxs[k], 1.0, chosen)
    chosen_b = chosen.astype(BF16)
    seen = cnt[...]
    before = _dot(chosen_b, tri_ref[...]) + jnp.concatenate([seen] * (logits.shape[1] // LANES), axis=1)
    out_row = lax.broadcasted_iota(jnp.int32, idx_ref.shape, 0)
    idx_out = jnp.zeros(idx_ref.shape, jnp.int32)
    gate_out = jnp.zeros(idx_ref.shape, F32)
    rank_out = jnp.zeros(idx_ref.shape, jnp.int32)
    for k in range(TOP_K):
        rk = jnp.sum(jnp.where(row == idxs[k], before, 0.0), axis=0, keepdims=True)
        idx_out = jnp.where(out_row == k, idxs[k], idx_out)
        gate_out = jnp.where(out_row == k, es[k] / denom, gate_out)
        rank_out = jnp.where(out_row == k, rk.astype(jnp.int32), rank_out)
    idx_ref[...] = idx_out
    gate_ref[...] = gate_out
    rank_ref[...] = rank_out
    cnt[...] = seen + _dot(chosen_b, jnp.ones((logits.shape[1], LANES), BF16))
    cnt_ref[0] = cnt[...]


def moe_router(x, gain, shift, scale, rw_t, rb_col, tokens_per_group, tm=256):
    t, d = x.shape
    tiles_per_group = tokens_per_group // tm
    tiles_per_part = MOE_PART // tm
    grp = lambda i: (i // tiles_per_group, 0, 0)
    tile = lambda w: pl.BlockSpec((tm, w), lambda i: (i, 0))
    slab = pl.BlockSpec((8, tm), lambda i: (0, i))
    tri = jnp.asarray(np.triu(np.ones((tm, tm), np.float32), 1), BF16)
    hp, idx, gates, rank, cnt = pl.pallas_call(
        functools.partial(_router_kernel, tiles_per_part=tiles_per_part),
        grid=(t // tm,),
        in_specs=[tile(d),
                  pl.BlockSpec((1, d), lambda i: (0, 0)),
                  pl.BlockSpec((1, 1, d), grp),
                  pl.BlockSpec((1, 1, d), grp),
                  pl.BlockSpec((LANES, d), lambda i: (0, 0)),
                  pl.BlockSpec((LANES, 1), lambda i: (0, 0)),
                  pl.BlockSpec((tm, tm), lambda i: (0, 0))],
        out_specs=[tile(d // 2), slab, slab, slab,
                   pl.BlockSpec((1, LANES, LANES), lambda i: (i // tiles_per_part, 0, 0))],
        out_shape=[jax.ShapeDtypeStruct((t, d // 2), jnp.uint32),
                   jax.ShapeDtypeStruct((8, t), jnp.int32),
                   jax.ShapeDtypeStruct((8, t), F32),
                   jax.ShapeDtypeStruct((8, t), jnp.int32),
                   jax.ShapeDtypeStruct((t // MOE_PART, LANES, LANES), F32)],
        scratch_shapes=[pltpu.VMEM((LANES, LANES), F32)],
        compiler_params=_cparams(("arbitrary",)),
        name="moe_router",
    )(x, gain.reshape(1, d), shift, scale, rw_t, rb_col, tri)
    per_token = lambda a: a[:TOP_K].T
    return hp, per_token(idx), per_token(gates), per_token(rank), cnt[:, :N_EXPERTS, 0].astype(jnp.int32)


FFN_COLS = 512
FFN_PIECES = 2 * D_FF // FFN_COLS + D_MODEL // FFN_COLS
MOE_PART = 4096
ADD_BATCH = 8


def _unpack_rows(words):
    lo = lax.bitcast_convert_type(words << 16, F32)
    hi = lax.bitcast_convert_type(words & jnp.uint32(0xFFFF0000), F32)
    return jnp.concatenate([lo, hi], axis=1).astype(BF16)


def _ffn_block(x_ref, y_ref, wgu_ref, bgu_ref, wd_ref, bd_ref, between):
    xb = _unpack_rows(x_ref[...])
    acts = []
    for c in range(D_FF // FFN_COLS):
        lo, hi = c * FFN_COLS, (c + 1) * FFN_COLS
        between(2 * c)
        gate = _dot(xb, wgu_ref[0, :, lo:hi]) + bgu_ref[0][:, lo:hi]
        between(2 * c + 1)
        up = _dot(xb, wgu_ref[0, :, D_FF + lo:D_FF + hi]) + bgu_ref[0][:, D_FF + lo:D_FF + hi]
        gate = jnp.minimum(gate, SWIGLU_LIMIT)
        up = jnp.clip(up, -SWIGLU_LIMIT, SWIGLU_LIMIT)
        acts.append((gate * _sigmoid(SWIGLU_ALPHA * gate) * (up + 1.0)).astype(BF16))
    act = jnp.concatenate(acts, axis=1)
    for n in range(D_MODEL // FFN_COLS):
        lo, hi = n * FFN_COLS, (n + 1) * FFN_COLS
        between(2 * D_FF // FFN_COLS + n)
        y_ref[:, lo:hi] = _dot(act, wd_ref[0, :, lo:hi]) + bd_ref[0][:, lo:hi]


def _expert_kernel(nb_ref, bs_ref, loc_ref, gate_ref, hp_hbm, wgu_ref, bgu_ref, wd_ref, bd_ref, out_hbm,
                   hbuf, acc, xbuf, ybuf, sem):
    part = pl.program_id(0)
    e = pl.program_id(1)
    idx = part * N_EXPERTS + e
    nb = nb_ref[idx]
    b0 = bs_ref[idx]
    shares = np.array_split(np.arange(MOE_ROWS), FFN_PIECES)

    def fetch_rows(blk):
        base = blk * MOE_ROWS
        dst = xbuf.at[blk % 2]

        def emit(k):
            for r in shares[k]:
                r = int(r)
                dst[r:r + 1, :] = hbuf[pl.ds(loc_ref[base + r], 1), :]
        return emit

    def add_rows(blk):
        base = blk * MOE_ROWS
        src = ybuf.at[blk % 2]

        def emit(k):
            for batch in np.array_split(shares[k], max(1, len(shares[k]) // ADD_BATCH)):
                rows = [loc_ref[base + int(r)] for r in batch]
                new = [acc[pl.ds(row, 1), :] + gate_ref[base + int(r)] * src[int(r):int(r) + 1, :]
                       for row, r in zip(rows, batch)]
                for row, val in zip(rows, new):
                    acc[pl.ds(row, 1), :] = val
        return emit

    def emit_all(f):
        for k in range(FFN_PIECES):
            f(k)

    @pl.when(e == 0)
    def _():
        load = pltpu.make_async_copy(hp_hbm.at[pl.ds(part * MOE_PART, MOE_PART), :],
                                     hbuf.at[pl.ds(0, MOE_PART), :], sem.at[0])
        load.start()
        hbuf[MOE_PART:, :] = jnp.zeros((hbuf.shape[0] - MOE_PART, hbuf.shape[1]), hbuf.dtype)
        acc[...] = jnp.zeros_like(acc)
        ybuf[...] = jnp.zeros_like(ybuf)
        load.wait()
        emit_all(fetch_rows(b0))

    def block(i, carry):
        fetch, add = fetch_rows(i + 1), add_rows(i - 1)

        def between(k):
            fetch(k)
            add(k)
        _ffn_block(xbuf.at[i % 2], ybuf.at[i % 2], wgu_ref, bgu_ref, wd_ref, bd_ref, between)
        return carry

    lax.fori_loop(b0, b0 + nb, block, 0)

    @pl.when(e == N_EXPERTS - 1)
    def _():
        emit_all(add_rows(b0 + nb - 1))
        store = pltpu.make_async_copy(acc.at[pl.ds(0, MOE_PART), :],
                                      out_hbm.at[pl.ds(part * MOE_PART, MOE_PART), :], sem.at[0])
        store.start()
        store.wait()


def moe_experts(hp, n_blk, blk_start, row_loc, row_gate, w_gu, b_gu, w_down, b_down):
    t = hp.shape[0]
    d = D_MODEL
    per_expert = lambda p, e, nb, bs, sl: (e, 0, 0)
    grid_spec = pltpu.PrefetchScalarGridSpec(
        num_scalar_prefetch=3,
        grid=(t // MOE_PART, N_EXPERTS),
        in_specs=[pl.BlockSpec(memory_space=pltpu.SMEM),
                  pl.BlockSpec(memory_space=pl.ANY),
                  pl.BlockSpec((1, d, 2 * D_FF), per_expert),
                  pl.BlockSpec((1, 1, 2 * D_FF), per_expert),
                  pl.BlockSpec((1, D_FF, d), per_expert),
                  pl.BlockSpec((1, 1, d), per_expert)],
        out_specs=pl.BlockSpec(memory_space=pl.ANY),
        scratch_shapes=[pltpu.VMEM((MOE_PART + 8, d // 2), jnp.uint32),
                        pltpu.VMEM((MOE_PART + 8, d), F32),
                        pltpu.VMEM((2, MOE_ROWS, d // 2), jnp.uint32),
                        pltpu.VMEM((2, MOE_ROWS, d), F32),
                        pltpu.SemaphoreType.DMA((1,))],
    )
    return pl.pallas_call(
        _expert_kernel,
        grid_spec=grid_spec,
        out_shape=jax.ShapeDtypeStruct((t, d), F32),
        compiler_params=pltpu.CompilerParams(dimension_semantics=("arbitrary", "arbitrary"),
                                             vmem_limit_bytes=EXPERT_VMEM_LIMIT),
        name="moe_experts",
    )(n_blk, blk_start, row_loc, row_gate, hp, w_gu, b_gu.reshape(N_EXPERTS, 1, -1), w_down,
      b_down.reshape(N_EXPERTS, 1, -1))


def _residual_kernel(y_ref, x_ref, g_ref, o_ref):
    o_ref[...] = x_ref[...] + g_ref[0] * y_ref[...]


def gated_residual(y, x, gate_mod, tokens_per_group, tm=512):
    t, d = x.shape
    tiles_per_group = tokens_per_group // tm
    tile = pl.BlockSpec((tm, d), lambda i: (i, 0))
    return pl.pallas_call(
        _residual_kernel,
        grid=(t // tm,),
        in_specs=[tile, tile, pl.BlockSpec((1, 1, d), lambda i: (i // tiles_per_group, 0, 0))],
        out_specs=tile,
        out_shape=jax.ShapeDtypeStruct((t, d), F32),
        compiler_params=_cparams(("parallel",)),
        name="moe_residual",
    )(y, x, gate_mod)


def _assignment_tables(top_idx, rank, gates, counts):
    t = top_idx.shape[0]
    n_parts = t // MOE_PART
    n_assign = t * TOP_K
    part_blocks = MOE_PART * TOP_K // MOE_ROWS + N_EXPERTS + 1
    n_rows = (1 + n_parts * part_blocks) * MOE_ROWS
    padded = (counts + MOE_ROWS - 1) // MOE_ROWS * MOE_ROWS
    part_first = (1 + jnp.arange(n_parts, dtype=jnp.int32) * part_blocks) * MOE_ROWS
    row_start = part_first[:, None] + jnp.cumsum(padded, axis=1) - padded
    experts = jnp.arange(N_EXPERTS, dtype=jnp.int32)
    by_part = top_idx.reshape(n_parts, MOE_PART, TOP_K)
    start_of = jnp.sum(jnp.where(by_part[..., None] == experts, row_start[:, None, None, :], 0), axis=-1)
    dest = (start_of.reshape(t, TOP_K) + rank).reshape(-1).astype(jnp.int32)
    local = (jnp.arange(n_assign, dtype=jnp.int32) // TOP_K) % MOE_PART
    vals = jnp.stack([local, lax.bitcast_convert_type(gates.reshape(-1), jnp.int32)], axis=1)
    init = jnp.broadcast_to(jnp.array([MOE_PART, 0], jnp.int32), (n_rows, 2))
    table = init.at[dest].set(vals)
    row_loc = table[:, 0]
    row_gate = lax.bitcast_convert_type(table[:, 1], F32)
    return row_loc, row_gate, (padded // MOE_ROWS).reshape(-1).astype(jnp.int32), \
        (row_start // MOE_ROWS).reshape(-1).astype(jnp.int32)


def moe_layer(x, p, shift, scale, gate_mod, tokens_per_group):
    hp, idx, gates, rank, counts = moe_router(x, p['norm2'], shift, scale, p['rw_t'], p['rb_col'], tokens_per_group)
    row_loc, row_gate, n_blk, blk_start = _assignment_tables(idx, rank, gates, counts)
    y = moe_experts(hp, n_blk, blk_start, row_loc, row_gate, p['w_gu_bf'], p['b_gu'], p['w_down_bf'], p['b_down'])
    return gated_residual(y, x, gate_mod, tokens_per_group)


def _pad_lanes(a, width):
    return jnp.pad(a, [(0, 0)] * (a.ndim - 1) + [(0, width - a.shape[-1])])


def _prep_common(p):
    p['rw_t'] = _pad_lanes(p['router_w'], LANES).T
    p['rb_col'] = jnp.concatenate(
        [p['router_b'].astype(F32), jnp.full((LANES - N_EXPERTS,), NEG_BIG, F32)]).reshape(LANES, 1)
    p['w_out'] = p['w_out'].astype(BF16)
    p['w_gu_bf'] = p['w_gu'].astype(BF16)
    p['w_down_bf'] = p['w_down'].astype(BF16)
    return p


def _prep_layer0(p):
    p = _prep_common(dict(p))
    p['w_in'] = _pad_lanes(p['w_in'], AB_IN_PAD).astype(BF16)
    q_b = p['q_b'].reshape(MLA_Q_LORA, MLA_HEADS, MLA_QK)
    p['q_b_pad'] = _pad_lanes(q_b, LANES).reshape(MLA_Q_LORA, MLA_HEADS * LANES).astype(BF16)
    kv_b = p['kv_b'].reshape(MLA_KV_LORA, MLA_HEADS, MLA_NOPE + MLA_V)
    p['wk_pad'] = _pad_lanes(kv_b[:, :, :MLA_NOPE], LANES).reshape(MLA_KV_LORA, MLA_HEADS * LANES).astype(BF16)
    p['wv'] = kv_b[:, :, MLA_NOPE:].reshape(MLA_KV_LORA, MLA_HEADS * MLA_V).astype(BF16)
    p['q_norm_pad'] = _pad_lanes(p['q_norm'].reshape(1, -1), LANES)
    p['k_norm_pad'] = _pad_lanes(p['k_norm'].reshape(1, -1), LANES)
    return p


def _prep_layer1(p):
    p = _prep_common(dict(p))
    p['w_in'] = p['w_in'].astype(BF16)
    p['q_norm2'] = jnp.tile(p['q_norm'].reshape(1, -1), (1, LANES // SWA_HD))
    p['k_norm2'] = jnp.tile(p['k_norm'].reshape(1, -1), (1, LANES // SWA_HD))
    return p


def _group_forward(x3, mods, p0, p1, lb_logits, caches, latent):
    batch, seq, d = x3.shape
    t = batch * seq
    x = x3.reshape(t, d)
    tpg = seq if latent else t
    sh1, sc1, g1, sh2, sc2, g2 = mods[0]

    z = modnorm_matmul(x, p0['norm1'], sh1, sc1, p0['w_in'], tpg)
    tab_b = _rope_lane_tables(seq, MLA_ROPE, LANES, MLA_NOPE) if latent else None
    s0f, s0b = (caches['hg_f'], caches['hg_b']) if latent else (None, None)
    o_f, o_b, s_f, s_b = hgrn_bidir(z, lb_logits, s0f, s0b, batch, seq)
    q = mla_queries(z, p0['qa_norm'], p0['q_b_pad'], p0['q_norm_pad'], tab_b, seq)
    ckv, k, v = mla_keys_values(z, AB_COL_KVA // LANES, z, AB_COL_KPE // LANES, p0['kva_norm'], p0['wk_pad'], p0['wv'],
                                p0['k_norm_pad'], tab_b, seq, norm_input=True)
    n_k = seq
    if latent:
        n_ctx = caches['ckv'].shape[1]
        ckv_c = caches['ckv'].reshape(batch * n_ctx, MLA_KV_LORA)
        kpe_c = _pad_lanes(caches['kpe'].reshape(batch * n_ctx, MLA_ROPE), LANES)
        _, k_c, v_c = mla_keys_values(ckv_c, 0, kpe_c, 0, p0['kva_norm'], p0['wk_pad'], p0['wv'],
                                      p0['k_norm_pad'], None, n_ctx, norm_input=False)
        cat = lambda a, b: jnp.concatenate([a.reshape(batch, n_ctx, -1), b.reshape(batch, seq, -1)],
                                           axis=1).reshape(batch * (n_ctx + seq), -1)
        k, v = cat(k_c, k), cat(v_c, v)
        n_k = n_ctx + seq
    o_mla = mla_attention(q, k, v, batch, seq, n_k)
    x = out_proj_layer0(o_f, o_b, z, p0['hg_out_norm'], o_mla, p0['w_out'], x, g1, tpg)
    x = moe_layer(x, p0, sh2, sc2, g2, tpg)
    state0 = (s_f, s_b, ckv.reshape(batch, seq, MLA_KV_LORA), z[:, AB_COL_KPE:AB_COL_KPE + MLA_ROPE].reshape(batch, seq, MLA_ROPE))

    sh1, sc1, g1, sh2, sc2, g2 = mods[1]
    z = modnorm_matmul(x, p1['norm1'], sh1, sc1, p1['w_in'], tpg)
    tab_c = _rope_lane_tables(seq, SWA_HD, SWA_HD, 0) if latent else None
    q, k_cache, k, v = swa_prep(z, p1['q_norm2'], p1['k_norm2'], tab_c, seq)
    sink = p1['sink'].astype(F32)
    if latent:
        n_ctx = caches['k1'].shape[1]
        k_c = caches['k1'].reshape(batch, n_ctx, -1).astype(BF16)
        v_c = caches['v1'].reshape(batch, n_ctx, -1).astype(BF16)
        a = swa_window(q, k, v, k_c, v_c, sink, batch, seq, n_ctx)
    else:
        a = swa_dense(q, k, v, sink, batch, seq)
    x = out_proj_layer1(a, p1['w_out'], x, g1, tpg)
    x = moe_layer(x, p1, sh2, sc2, g2, tpg)
    nkv = SWA_KV_HEADS * SWA_HD
    state1 = (k_cache.reshape(batch, seq, SWA_KV_HEADS, SWA_HD),
              z[:, SWA_HEADS * SWA_HD + nkv:].reshape(batch, seq, SWA_KV_HEADS, SWA_HD))
    return x.reshape(batch, seq, d), state0, state1


def kernel(x_prompt, x_sample, state_l0_hgrn_fwd, state_l0_hgrn_bwd, cache_l0_mla_ckv, cache_l0_mla_kpe, cache_l1_k, cache_l1_v, c, c_ctx, hgrn_lb_logits, l0_ada_w, l0_ada_b, l0_norm1, l0_norm2, l0_w_in, l0_hg_out_norm, l0_qa_norm, l0_q_b, l0_kva_norm, l0_kv_b, l0_q_norm, l0_k_norm, l0_w_out, l0_router_w, l0_router_b, l0_w_gu, l0_b_gu, l0_w_down, l0_b_down, l1_ada_w, l1_ada_b, l1_norm1, l1_norm2, l1_w_in, l1_q_norm, l1_k_norm, l1_sink, l1_w_out, l1_router_w, l1_router_b, l1_w_gu, l1_b_gu, l1_w_down, l1_b_down):
    p0 = _prep_layer0(dict(norm1=l0_norm1, norm2=l0_norm2, w_in=l0_w_in, hg_out_norm=l0_hg_out_norm,
                           qa_norm=l0_qa_norm, q_b=l0_q_b, kva_norm=l0_kva_norm, kv_b=l0_kv_b,
                           q_norm=l0_q_norm, k_norm=l0_k_norm, w_out=l0_w_out, router_w=l0_router_w,
                           router_b=l0_router_b, w_gu=l0_w_gu, b_gu=l0_b_gu, w_down=l0_w_down,
                           b_down=l0_b_down))
    p1 = _prep_layer1(dict(norm1=l1_norm1, norm2=l1_norm2, w_in=l1_w_in, q_norm=l1_q_norm, k_norm=l1_k_norm,
                           sink=l1_sink, w_out=l1_w_out, router_w=l1_router_w, router_b=l1_router_b,
                           w_gu=l1_w_gu, b_gu=l1_b_gu, w_down=l1_w_down, b_down=l1_b_down))
    dec_batch = c.shape[0]
    d = c.shape[1]
    cond8 = jnp.concatenate([c_ctx[None, :], c, jnp.zeros((8 - 1 - dec_batch, d), F32)], axis=0)
    mods_ctx, mods_lat = [], []
    for w, b in ((l0_ada_w, l0_ada_b), (l1_ada_w, l1_ada_b)):
        mod = ada_params(cond8, w, b)
        mods_ctx.append([m.reshape(1, 1, d) for m in jnp.split(mod[0:1], 6, axis=-1)])
        mods_lat.append([m.reshape(dec_batch, 1, d) for m in jnp.split(mod[1:1 + dec_batch], 6, axis=-1)])

    y_prompt, st0, st1 = _group_forward(x_prompt, mods_ctx, p0, p1, hgrn_lb_logits, None, latent=False)
    caches = dict(hg_f=state_l0_hgrn_fwd, hg_b=state_l0_hgrn_bwd, ckv=cache_l0_mla_ckv, kpe=cache_l0_mla_kpe,
                  k1=cache_l1_k, v1=cache_l1_v)
    y_sample, _, _ = _group_forward(x_sample, mods_lat, p0, p1, hgrn_lb_logits, caches, latent=True)
    return (y_prompt, y_sample, st0[0], st0[1], st0[2], st0[3], st1[0], st1[1])
```

```python
import functools

import numpy as np
import jax
import jax.numpy as jnp
from jax import lax
from jax.experimental import pallas as pl
from jax.experimental.pallas import tpu as pltpu

F32 = jnp.float32
BF16 = jnp.bfloat16

D_MODEL = 1024
GRID_W = 64
ROPE_THETA = 10000.0
EPS = 1e-6
HG_HEADS = 4
HG_DK = 128
HG_DV = 128
MLA_HEADS = 8
MLA_NOPE = 64
MLA_ROPE = 32
MLA_V = 64
MLA_QK = MLA_NOPE + MLA_ROPE
MLA_Q_LORA = 256
MLA_KV_LORA = 128
MLA_SCALE = MLA_QK ** -0.5
SWA_HEADS = 16
SWA_KV_HEADS = 4
SWA_HD = 64
SWA_WINDOW = 128
SWA_SCALE = SWA_HD ** -0.5
SWA_GROUP = SWA_HEADS // SWA_KV_HEADS
N_EXPERTS = 32
TOP_K = 4
D_FF = 1024
SWIGLU_LIMIT = 7.0
SWIGLU_ALPHA = 1.702

LANES = 128
HG_CHUNK = 128
HG_LEVELS = 7
HG_MXU_LEVELS = 0
AB_IN_PAD = 3072
AB_COL_GATE = 3 * HG_HEADS * HG_DK + HG_HEADS * HG_DV
AB_COL_QA = AB_COL_GATE + HG_HEADS * HG_DV
AB_COL_KVA = AB_COL_QA + MLA_Q_LORA
AB_COL_KPE = AB_COL_KVA + MLA_KV_LORA
MOE_ROWS = 256
NEG_BIG = -1e30
VMEM_LIMIT = 48 * 1024 * 1024
EXPERT_VMEM_LIMIT = 56 * 1024 * 1024


def _cparams(sem):
    return pltpu.CompilerParams(dimension_semantics=sem, vmem_limit_bytes=VMEM_LIMIT)


def _dot(a, b):
    return jnp.dot(a, b, preferred_element_type=F32)


def _dot_nt(a, b):
    return lax.dot_general(a, b, (((1,), (1,)), ((), ())), preferred_element_type=F32)


def _dot_tn(a, b):
    return lax.dot_general(a, b, (((0,), (0,)), ((), ())), preferred_element_type=F32)


def _split2(x):
    hi = x.astype(BF16)
    lo = (x - hi.astype(F32)).astype(BF16)
    return hi, lo


def _dot_hp(a, b):
    ah, al = _split2(a)
    bh, bl = _split2(b)
    return _dot(ah, bh) + _dot(ah, bl) + _dot(al, bh)


def _sigmoid(x):
    return 1.0 / (1.0 + jnp.exp(-x))


def _modnorm(x, gain, shift, scale):
    y = x * lax.rsqrt(jnp.mean(x * x, axis=-1, keepdims=True) + EPS)
    return y * gain * (1.0 + scale) + shift


def _ada_kernel(c_ref, w_ref, b_ref, o_ref):
    c = c_ref[...]
    o_ref[...] = _dot_hp(c * _sigmoid(c), w_ref[...]) + b_ref[...]


def ada_params(cond8, w, b):
    n = w.shape[1]
    tn = 1024
    return pl.pallas_call(
        _ada_kernel,
        grid=(n // tn,),
        in_specs=[pl.BlockSpec((8, D_MODEL), lambda j: (0, 0)),
                  pl.BlockSpec((D_MODEL, tn), lambda j: (0, j)),
                  pl.BlockSpec((1, tn), lambda j: (0, j))],
        out_specs=pl.BlockSpec((8, tn), lambda j: (0, j)),
        out_shape=jax.ShapeDtypeStruct((8, n), F32),
        compiler_params=_cparams(("parallel",)),
        name="ada_params",
    )(cond8, w, b.reshape(1, n))


def _modnorm_matmul_kernel(x_ref, g_ref, sh_ref, sc_ref, w_ref, o_ref):
    h = _modnorm(x_ref[...], g_ref[...], sh_ref[0], sc_ref[0])
    o_ref[...] = _dot(h.astype(BF16), w_ref[...])


def modnorm_matmul(x, gain, shift, scale, w_bf16, tokens_per_group, tm=512):
    t, d = x.shape
    n = w_bf16.shape[1]
    tiles_per_group = tokens_per_group // tm
    grp = lambda i: (i // tiles_per_group, 0, 0)
    return pl.pallas_call(
        _modnorm_matmul_kernel,
        grid=(t // tm,),
        in_specs=[pl.BlockSpec((tm, d), lambda i: (i, 0)),
                  pl.BlockSpec((1, d), lambda i: (0, 0)),
                  pl.BlockSpec((1, 1, d), grp),
                  pl.BlockSpec((1, 1, d), grp),
                  pl.BlockSpec((d, n), lambda i: (0, 0))],
        out_specs=pl.BlockSpec((tm, n), lambda i: (i, 0)),
        out_shape=jax.ShapeDtypeStruct((t, n), F32),
        compiler_params=_cparams(("parallel",)),
        name="modnorm_matmul",
    )(x, gain.reshape(1, d), shift, scale, w_bf16)


def _hgrn_constants():
    c = HG_CHUNK
    t = np.arange(c)[:, None]
    u = np.arange(c)[None, :]
    mats = [(u <= t), (u > t)]
    for l in range(HG_MXU_LEVELS):
        m = 1 << l
        r = (t // (2 * m)) * (2 * m) + m - 1
        mats.append((u > np.minimum(t, r)) & (u <= np.maximum(t, r)))
    fwd = np.concatenate(mats, axis=0).astype(np.float32)
    bwd = np.concatenate([mm[::-1, ::-1] for mm in mats], axis=0).astype(np.float32)
    x = np.bitwise_xor(t, u)
    lvl = np.where(x > 0, np.floor(np.log2(np.maximum(x, 1))), HG_LEVELS).astype(np.int32)
    lv_f = np.where(t >= u, lvl, -1).astype(np.int32)
    return fwd, bwd, lv_f, lv_f.T.copy()


def _hgrn_chunk(qs, fpres, vs, lbs, sts, forwards, mcats, lvs):
    c = HG_CHUNK
    n = len(qs)
    fs = [lb + (1.0 - lb) * _sigmoid(fp) for lb, fp in zip(lbs, fpres)]
    kks = [1.0 - f for f in fs]
    logs = [_split2(jnp.log(f)) for f in fs]
    xs = [None] * n
    for fwd in (True, False):
        ids = [i for i in range(n) if forwards[i] == fwd]
        parts = [half for i in ids for half in logs[i]]
        x_all = _dot(mcats[0 if fwd else 1], jnp.concatenate(parts, axis=1))
        for j, i in enumerate(ids):
            xs[i] = x_all[:, 2 * j * c:(2 * j + 1) * c] + x_all[:, (2 * j + 1) * c:(2 * j + 2) * c]
    gs = [x[0:c] for x in xs]
    lv_of = [lvs[0 if f else 1] for f in forwards]
    qbs = [q.astype(BF16) for q in qs]
    kbs = [kk.astype(BF16) for kk in kks]
    vbs = [v.astype(BF16) for v in vs]
    os_ = [_dot_nt((q * jnp.exp(g)).astype(BF16), st.astype(BF16)) for q, g, st in zip(qs, gs, sts)]
    accs = [jnp.where(lv == HG_LEVELS, _dot_nt(qb, kb), 0.0) for lv, qb, kb in zip(lv_of, qbs, kbs)]
    for l in range(HG_LEVELS):
        for i in range(n):
            if l < HG_MXU_LEVELS:
                x = xs[i][(2 + l) * c:(3 + l) * c]
            else:
                m = 1 << l
                ref_rows = [j * 2 * m + (m - 1 if forwards[i] else m) for j in range(c // (2 * m))]
                g_ref = jnp.concatenate(
                    [jnp.broadcast_to(gs[i][r:r + 1, :], (2 * m, c)) for r in ref_rows], axis=0)
                x = -jnp.abs(gs[i] - g_ref)
            e = jnp.exp(x)
            p = _dot_nt((qs[i] * e).astype(BF16), (kks[i] * e).astype(BF16))
            accs[i] = jnp.where(lv_of[i] == l, p, accs[i])
    outs, new_sts = [], []
    for i in range(n):
        edge_row = c - 1 if forwards[i] else 0
        outs.append(os_[i] + _dot(accs[i].astype(BF16), vbs[i]))
        k_end = (kks[i] * jnp.exp(xs[i][c:2 * c])).astype(BF16)
        new_sts.append(sts[i] * jnp.exp(gs[i][edge_row:edge_row + 1, :]) + _dot_tn(vbs[i], k_end))
    return outs, new_sts


def _hgrn_kernel(*refs, has_init):
    if has_init:
        (qf_ref, qb_ref, ff_ref, fb_ref, vf_ref, vb_ref, lbl_ref, mf_ref, mb_ref, lvf_ref, lvb_ref,
         s0f_ref, s0b_ref, of_ref, ob_ref, sf_ref, sb_ref, stf, stb) = refs
    else:
        (qf_ref, qb_ref, ff_ref, fb_ref, vf_ref, vb_ref, lbl_ref, mf_ref, mb_ref, lvf_ref, lvb_ref,
         of_ref, ob_ref, sf_ref, sb_ref, stf, stb) = refs
    c = pl.program_id(1)
    nc = pl.num_programs(1)

    @pl.when(c == 0)
    def _():
        for h in range(HG_HEADS):
            if has_init:
                stf[h] = s0f_ref[0, h].T
                stb[h] = s0b_ref[0, h].T
            else:
                stf[h] = jnp.zeros((HG_DV, HG_DK), F32)
                stb[h] = jnp.zeros((HG_DV, HG_DK), F32)

    rows = [lbl_ref[:, j, :] for j in range(lbl_ref.shape[1])]
    mx = functools.reduce(jnp.maximum, rows)
    ex = [jnp.exp(r - mx) for r in rows]
    lb = ex[0] / functools.reduce(lambda a, b: a + b, ex)

    heads = [slice(h * LANES, (h + 1) * LANES) for h in range(HG_HEADS)]
    nh = HG_HEADS
    outs, sts = _hgrn_chunk(
        [qf_ref[0, :, hs] for hs in heads] + [qb_ref[0, :, hs] for hs in heads],
        [ff_ref[0, :, hs] for hs in heads] + [fb_ref[0, :, hs] for hs in heads],
        [vf_ref[0, :, hs] for hs in heads] + [vb_ref[0, :, hs] for hs in heads],
        [lb[0:1, hs] for hs in heads] + [lb[1:2, hs] for hs in heads],
        [stf[h] for h in range(nh)] + [stb[h] for h in range(nh)],
        [True] * nh + [False] * nh,
        (mf_ref[...], mb_ref[...]), (lvf_ref[...], lvb_ref[...]))
    o_f, o_b, st_f, st_b = outs[:nh], outs[nh:], sts[:nh], sts[nh:]
    for h, hs in enumerate(heads):
        of_ref[0, :, hs] = o_f[h]
        ob_ref[0, :, hs] = o_b[h]
        stf[h] = st_f[h]
        stb[h] = st_b[h]

    @pl.when(c == nc - 1)
    def _():
        for h in range(HG_HEADS):
            sf_ref[0, h] = stf[h].T
            sb_ref[0, h] = stb[h].T


def hgrn_bidir(z, lb_logits, s0f, s0b, batch, seq):
    nc = seq // HG_CHUNK
    z3 = z.reshape(batch, seq, z.shape[1])
    mf, mb, lvf, lvb = _hgrn_constants()
    has_init = s0f is not None
    width = HG_HEADS * LANES
    blk = (1, HG_CHUNK, width)
    fwd = lambda off: pl.BlockSpec(blk, lambda b, c: (b, c, off))
    bwd = lambda off: pl.BlockSpec(blk, lambda b, c: (b, nc - 1 - c, off))
    full = lambda a: pl.BlockSpec(a.shape, lambda b, c: (0,) * a.ndim)
    st_spec = pl.BlockSpec((1, HG_HEADS, HG_DK, HG_DV), lambda b, c: (b, 0, 0, 0))
    consts = [jnp.asarray(mf, BF16), jnp.asarray(mb, BF16), jnp.asarray(lvf), jnp.asarray(lvb)]
    in_specs = [fwd(0), bwd(0), fwd(1), bwd(2), fwd(3), bwd(3),
                pl.BlockSpec(lb_logits.shape, lambda b, c: (0, 0, 0))]
    in_specs += [full(a) for a in consts]
    args = [z3] * 6 + [lb_logits] + consts
    if has_init:
        in_specs += [st_spec, st_spec]
        args += [s0f, s0b]
    o_shape = jax.ShapeDtypeStruct((batch, seq, width), F32)
    s_shape = jax.ShapeDtypeStruct((batch, HG_HEADS, HG_DK, HG_DV), F32)
    o_f, o_b, s_f, s_b = pl.pallas_call(
        functools.partial(_hgrn_kernel, has_init=has_init),
        grid=(batch, nc),
        in_specs=in_specs,
        out_specs=[pl.BlockSpec(blk, lambda b, c: (b, c, 0)),
                   pl.BlockSpec(blk, lambda b, c: (b, nc - 1 - c, 0)),
                   st_spec, st_spec],
        out_shape=[o_shape, o_shape, s_shape, s_shape],
        scratch_shapes=[pltpu.VMEM((HG_HEADS, HG_DV, HG_DK), F32), pltpu.VMEM((HG_HEADS, HG_DV, HG_DK), F32)],
        compiler_params=_cparams(("parallel", "arbitrary")),
        name="hgrn_bidir",
    )(*args)
    t = batch * seq
    return o_f.reshape(t, -1), o_b.reshape(t, -1), s_f, s_b


def _axial_tables(n_tokens, n_rot):
    t = jnp.arange(n_tokens)
    row = (t // GRID_W).astype(F32)
    col = (t % GRID_W).astype(F32)
    n_freq = n_rot // 4
    inv = jnp.power(ROPE_THETA, -jnp.arange(n_freq, dtype=F32) / n_freq)
    ang = jnp.concatenate([row[:, None] * inv, col[:, None] * inv], axis=-1)
    return jnp.cos(ang), jnp.sin(ang)


def _rope_lane_tables(n_tokens, n_rot, head_width, first_rot_lane):
    cos, sin = _axial_tables(n_tokens, n_rot)
    half = n_rot // 2
    c_head = jnp.ones((n_tokens, head_width), F32)
    sa_head = jnp.zeros((n_tokens, head_width), F32)
    sb_head = jnp.zeros((n_tokens, head_width), F32)
    a0, a1, a2 = first_rot_lane, first_rot_lane + half, first_rot_lane + n_rot
    c_head = c_head.at[:, a0:a1].set(cos).at[:, a1:a2].set(cos)
    sa_head = sa_head.at[:, a0:a1].set(-sin)
    sb_head = sb_head.at[:, a1:a2].set(sin)
    reps = LANES // head_width
    tile = lambda a: jnp.tile(a, (1, reps))
    return tile(c_head), tile(sa_head), tile(sb_head)


def _rope(x, c, sa, sb, half):
    return x * c + pltpu.roll(x, LANES - half, 1) * sa + pltpu.roll(x, half, 1) * sb


def _mla_q_kernel(*refs, rope):
    if rope:
        qa_ref, qan_ref, qb_ref, qn_ref, c_ref, sa_ref, sb_ref, o_ref = refs
    else:
        qa_ref, qan_ref, qb_ref, qn_ref, o_ref = refs
    qa = qa_ref[...]
    qn = qa * lax.rsqrt(jnp.mean(qa * qa, axis=-1, keepdims=True) + EPS) * qan_ref[...]
    qfull = _dot(qn.astype(BF16), qb_ref[...])
    outs = []
    for h in range(MLA_HEADS):
        qh = qfull[:, h * LANES:(h + 1) * LANES]
        ms = jnp.sum(qh * qh, axis=-1, keepdims=True) * (1.0 / MLA_QK)
        qh = qh * lax.rsqrt(ms + EPS) * qn_ref[...]
        if rope:
            qh = _rope(qh, c_ref[...], sa_ref[...], sb_ref[...], MLA_ROPE // 2)
        outs.append((qh * MLA_SCALE).astype(BF16))
    o_ref[...] = jnp.concatenate(outs, axis=1)


def mla_queries(z, qa_norm, q_b_pad, q_norm_pad, tables, tokens_per_batch, tm=512):
    t = z.shape[0]
    rope = tables is not None
    row = lambda i: (0, 0)
    in_specs = [pl.BlockSpec((tm, MLA_Q_LORA), lambda i: (i, AB_COL_QA // MLA_Q_LORA)),
                pl.BlockSpec((1, MLA_Q_LORA), row),
                pl.BlockSpec(q_b_pad.shape, row),
                pl.BlockSpec((1, LANES), row)]
    args = [z, qa_norm.reshape(1, -1), q_b_pad, q_norm_pad]
    if rope:
        tpb = tokens_per_batch // tm
        in_specs += [pl.BlockSpec((tm, LANES), lambda i: (i % tpb, 0))] * 3
        args += list(tables)
    return pl.pallas_call(
        functools.partial(_mla_q_kernel, rope=rope),
        grid=(t // tm,),
        in_specs=in_specs,
        out_specs=pl.BlockSpec((tm, MLA_HEADS * LANES), lambda i: (i, 0)),
        out_shape=jax.ShapeDtypeStruct((t, MLA_HEADS * LANES), BF16),
        compiler_params=_cparams(("parallel",)),
        name="mla_queries",
    )(*args)


def _mla_kv_kernel(*refs, norm_input, rope):
    if rope:
        kva_ref, kpe_ref, kvan_ref, wk_ref, wv_ref, kn_ref, c_ref, sa_ref, sb_ref, ckv_ref, k_ref, v_ref = refs
    else:
        kva_ref, kpe_ref, kvan_ref, wk_ref, wv_ref, kn_ref, ckv_ref, k_ref, v_ref = refs
    ckv = kva_ref[...]
    if norm_input:
        ckv = ckv * lax.rsqrt(jnp.mean(ckv * ckv, axis=-1, keepdims=True) + EPS) * kvan_ref[...]
    ckv_ref[...] = ckv
    cb = ckv.astype(BF16)
    knope = _dot(cb, wk_ref[...])
    v_ref[...] = _dot(cb, wv_ref[...]).astype(BF16)
    kpe = pltpu.roll(kpe_ref[...], MLA_NOPE, 1)
    outs = []
    for h in range(MLA_HEADS):
        kh = knope[:, h * LANES:(h + 1) * LANES] + kpe
        ms = jnp.sum(kh * kh, axis=-1, keepdims=True) * (1.0 / MLA_QK)
        kh = kh * lax.rsqrt(ms + EPS) * kn_ref[...]
        if rope:
            kh = _rope(kh, c_ref[...], sa_ref[...], sb_ref[...], MLA_ROPE // 2)
        outs.append(kh.astype(BF16))
    k_ref[...] = jnp.concatenate(outs, axis=1)


def mla_keys_values(kva_src, kva_col, kpe_src, kpe_col, kva_norm, wk_pad, wv, k_norm_pad, tables,
                    tokens_per_batch, norm_input, tm=512):
    t = kva_src.shape[0]
    rope = tables is not None
    row = lambda i: (0, 0)
    in_specs = [pl.BlockSpec((tm, LANES), lambda i: (i, kva_col)),
                pl.BlockSpec((tm, LANES), lambda i: (i, kpe_col)),
                pl.BlockSpec((1, LANES), row),
                pl.BlockSpec(wk_pad.shape, row),
                pl.BlockSpec(wv.shape, row),
                pl.BlockSpec((1, LANES), row)]
    args = [kva_src, kpe_src, kva_norm.reshape(1, -1), wk_pad, wv, k_norm_pad]
    if rope:
        tpb = tokens_per_batch // tm
        in_specs += [pl.BlockSpec((tm, LANES), lambda i: (i % tpb, 0))] * 3
        args += list(tables)
    return pl.pallas_call(
        functools.partial(_mla_kv_kernel, norm_input=norm_input, rope=rope),
        grid=(t // tm,),
        in_specs=in_specs,
        out_specs=[pl.BlockSpec((tm, LANES), lambda i: (i, 0)),
                   pl.BlockSpec((tm, MLA_HEADS * LANES), lambda i: (i, 0)),
                   pl.BlockSpec((tm, MLA_HEADS * MLA_V), lambda i: (i, 0))],
        out_shape=[jax.ShapeDtypeStruct((t, LANES), F32),
                   jax.ShapeDtypeStruct((t, MLA_HEADS * LANES), BF16),
                   jax.ShapeDtypeStruct((t, MLA_HEADS * MLA_V), BF16)],
        compiler_params=_cparams(("parallel",)),
        name="mla_keys_values",
    )(*args)


def _mla_attn_kernel(q_ref, k_ref, vt_ref, o_ref):
    heads = range(q_ref.shape[2] // LANES)
    ss = [_dot_nt(k_ref[0][:, j * LANES:(j + 1) * LANES], q_ref[0][:, j * LANES:(j + 1) * LANES]) for j in heads]
    ms = [jnp.max(s, axis=0, keepdims=True) for s in ss]
    ps = [jnp.exp(s - m) for s, m in zip(ss, ms)]
    ls = [jnp.sum(p, axis=0, keepdims=True) for p in ps]
    ots = [_dot(vt_ref[0][j * MLA_V:(j + 1) * MLA_V, :], ps[j].astype(BF16)) / ls[j] for j in heads]
    for j in range(0, len(ots), 2):
        o_ref[0, :, j * MLA_V:(j + 2) * MLA_V] = jnp.concatenate(ots[j:j + 2], axis=0).T


def mla_attention(q, k, v, batch, n_q, n_k, tq=512):
    tq = min(tq, n_q)
    hps = 2 if n_k > 1024 else MLA_HEADS
    q3 = q.reshape(batch, n_q, -1)
    k3 = k.reshape(batch, n_k, -1)
    vt = jnp.swapaxes(v.reshape(batch, n_k, -1), 1, 2)
    out = pl.pallas_call(
        _mla_attn_kernel,
        grid=(batch, MLA_HEADS // hps, n_q // tq),
        in_specs=[pl.BlockSpec((1, tq, hps * LANES), lambda b, j, i: (b, i, j)),
                  pl.BlockSpec((1, n_k, hps * LANES), lambda b, j, i: (b, 0, j)),
                  pl.BlockSpec((1, hps * MLA_V, n_k), lambda b, j, i: (b, j, 0))],
        out_specs=pl.BlockSpec((1, tq, hps * MLA_V), lambda b, j, i: (b, i, j)),
        out_shape=jax.ShapeDtypeStruct((batch, n_q, MLA_HEADS * MLA_V), F32),
        compiler_params=_cparams(("parallel", "parallel", "arbitrary")),
        name="mla_attention",
    )(q3, k3, vt)
    return out.reshape(batch * n_q, -1)


def _out0_kernel(of_ref, ob_ref, ag_ref, hgn_ref, om_ref, w_ref, x_ref, g_ref, o_ref):
    o = of_ref[...] + ob_ref[...]
    ag = ag_ref[...]
    parts = []
    for h in range(HG_HEADS):
        oh = o[:, h * HG_DV:(h + 1) * HG_DV]
        oh = oh * lax.rsqrt(jnp.mean(oh * oh, axis=-1, keepdims=True) + EPS) * hgn_ref[...]
        gh = ag[:, h * HG_DV:(h + 1) * HG_DV]
        parts.append((oh * (gh * _sigmoid(gh))).astype(BF16))
    oa = jnp.concatenate(parts, axis=1)
    n_a = HG_HEADS * HG_DV
    mix = _dot(oa, w_ref[0:n_a, :]) + _dot(om_ref[...].astype(BF16), w_ref[n_a:, :])
    o_ref[...] = x_ref[...] + g_ref[0] * mix


def out_proj_layer0(o_f, o_b, z, hg_norm, o_mla, w_out_bf16, x, gate, tokens_per_group, tm=512):
    t, d = x.shape
    n_a = HG_HEADS * HG_DV
    tiles_per_group = tokens_per_group // tm
    tile = lambda w: pl.BlockSpec((tm, w), lambda i: (i, 0))
    return pl.pallas_call(
        _out0_kernel,
        grid=(t // tm,),
        in_specs=[tile(n_a), tile(n_a),
                  pl.BlockSpec((tm, n_a), lambda i: (i, AB_COL_GATE // n_a)),
                  pl.BlockSpec((1, HG_DV), lambda i: (0, 0)),
                  tile(o_mla.shape[1]),
                  pl.BlockSpec(w_out_bf16.shape, lambda i: (0, 0)),
                  tile(d),
                  pl.BlockSpec((1, 1, d), lambda i: (i // tiles_per_group, 0, 0))],
        out_specs=tile(d),
        out_shape=jax.ShapeDtypeStruct((t, d), F32),
        compiler_params=_cparams(("parallel",)),
        name="out_proj_layer0",
    )(o_f, o_b, z, hg_norm.reshape(1, -1), o_mla, w_out_bf16, x, gate)


def _out1_kernel(a_ref, w_ref, x_ref, g_ref, o_ref):
    o_ref[...] = x_ref[...] + g_ref[0] * _dot(a_ref[...].astype(BF16), w_ref[...])


def out_proj_layer1(a, w_out_bf16, x, gate, tokens_per_group, tm=512):
    t, d = x.shape
    tiles_per_group = tokens_per_group // tm
    tile = lambda w: pl.BlockSpec((tm, w), lambda i: (i, 0))
    return pl.pallas_call(
        _out1_kernel,
        grid=(t // tm,),
        in_specs=[tile(a.shape[1]),
                  pl.BlockSpec(w_out_bf16.shape, lambda i: (0, 0)),
                  tile(d),
                  pl.BlockSpec((1, 1, d), lambda i: (i // tiles_per_group, 0, 0))],
        out_specs=tile(d),
        out_shape=jax.ShapeDtypeStruct((t, d), F32),
        compiler_params=_cparams(("parallel",)),
        name="out_proj_layer1",
    )(a, w_out_bf16, x, gate)


def _head_rms(x, gain2):
    sq = x * x
    lane = lax.broadcasted_iota(jnp.int32, x.shape, 1)
    first = lane < SWA_HD
    lo = jnp.sum(jnp.where(first, sq, 0.0), axis=-1, keepdims=True)
    hi = jnp.sum(jnp.where(first, 0.0, sq), axis=-1, keepdims=True)
    ms = jnp.where(first, lo, hi) * (1.0 / SWA_HD)
    return x * lax.rsqrt(ms + EPS) * gain2


def _swa_prep_kernel(*refs, rope):
    if rope:
        zq_ref, zk_ref, zv_ref, qn_ref, kn_ref, c_ref, sa_ref, sb_ref, q_ref, kc_ref, k_ref, v_ref = refs
    else:
        zq_ref, zk_ref, zv_ref, qn_ref, kn_ref, q_ref, kc_ref, k_ref, v_ref = refs
    half = SWA_HD // 2

    def rot(x):
        return _rope(x, c_ref[...], sa_ref[...], sb_ref[...], half) if rope else x

    zq = zq_ref[...]
    qs = []
    for p in range(zq.shape[1] // LANES):
        x = _head_rms(zq[:, p * LANES:(p + 1) * LANES], qn_ref[...])
        qs.append((rot(x) * SWA_SCALE).astype(BF16))
    q_ref[...] = jnp.concatenate(qs, axis=1)
    zk = zk_ref[...]
    kn, kr = [], []
    for p in range(zk.shape[1] // LANES):
        x = _head_rms(zk[:, p * LANES:(p + 1) * LANES], kn_ref[...])
        kn.append(x)
        kr.append(rot(x).astype(BF16))
    kc_ref[...] = jnp.concatenate(kn, axis=1)
    k_ref[...] = jnp.concatenate(kr, axis=1)
    v_ref[...] = zv_ref[...].astype(BF16)


def swa_prep(z, q_norm2, k_norm2, tables, tokens_per_batch, tm=512):
    t = z.shape[0]
    nq = SWA_HEADS * SWA_HD
    nkv = SWA_KV_HEADS * SWA_HD
    rope = tables is not None
    row = lambda i: (0, 0)
    in_specs = [pl.BlockSpec((tm, nq), lambda i: (i, 0)),
                pl.BlockSpec((tm, nkv), lambda i: (i, nq // nkv)),
                pl.BlockSpec((tm, nkv), lambda i: (i, nq // nkv + 1)),
                pl.BlockSpec((1, LANES), row),
                pl.BlockSpec((1, LANES), row)]
    args = [z, z, z, q_norm2, k_norm2]
    if rope:
        tpb = tokens_per_batch // tm
        in_specs += [pl.BlockSpec((tm, LANES), lambda i: (i % tpb, 0))] * 3
        args += list(tables)
    tile = lambda w: pl.BlockSpec((tm, w), lambda i: (i, 0))
    return pl.pallas_call(
        functools.partial(_swa_prep_kernel, rope=rope),
        grid=(t // tm,),
        in_specs=in_specs,
        out_specs=[tile(nq), tile(nkv), tile(nkv), tile(nkv)],
        out_shape=[jax.ShapeDtypeStruct((t, nq), BF16),
                   jax.ShapeDtypeStruct((t, nkv), F32),
                   jax.ShapeDtypeStruct((t, nkv), BF16),
                   jax.ShapeDtypeStruct((t, nkv), BF16)],
        compiler_params=_cparams(("parallel",)),
        name="swa_prep",
    )(*args)


def _sink_attention_t(q8, k_all, vt_all, bias, sink_ref, pair):
    tq = q8.shape[0]
    n_q = SWA_GROUP * tq
    lane = lax.broadcasted_iota(jnp.int32, (1, n_q), 1)
    kv = range(2)
    heads = [[n * SWA_GROUP + g for g in range(SWA_GROUP)] for n in kv]
    q4s = [jnp.concatenate([q8[:, hq * SWA_HD:(hq + 1) * SWA_HD] for hq in heads[n]], axis=0) for n in kv]
    sks = []
    for n in kv:
        sk = jnp.zeros((1, n_q), F32)
        for g, hq in enumerate(heads[n]):
            sk = jnp.where(lane // tq == g, sink_ref[pair * 2 * SWA_GROUP + hq], sk)
        sks.append(sk)
    ss = [_dot_nt(k_all[:, n * SWA_HD:(n + 1) * SWA_HD], q4s[n]) for n in kv]
    if bias is not None:
        ss = [s + bias for s in ss]
    ms = [jnp.maximum(jnp.max(ss[n], axis=0, keepdims=True), sks[n]) for n in kv]
    ps = [jnp.exp(ss[n] - ms[n]) for n in kv]
    ls = [jnp.sum(ps[n], axis=0, keepdims=True) + jnp.exp(sks[n] - ms[n]) for n in kv]
    ots = [_dot(vt_all[n * SWA_HD:(n + 1) * SWA_HD, :], ps[n].astype(BF16)) / ls[n] for n in kv]
    pairs = []
    for n in kv:
        for g in range(0, SWA_GROUP, 2):
            two = jnp.concatenate([ots[n][:, g * tq:(g + 1) * tq], ots[n][:, (g + 1) * tq:(g + 2) * tq]], axis=0)
            pairs.append(two.T)
    return jnp.concatenate(pairs, axis=1)


def _swa_dense_kernel(sink_ref, q_ref, k_ref, vt_ref, o_ref):
    o_ref[0] = _sink_attention_t(q_ref[0], k_ref[0], vt_ref[0], None, sink_ref, pl.program_id(1))


def swa_dense(q, k, v, sink, batch, seq):
    nq = SWA_HEADS * SWA_HD
    q3 = q.reshape(batch, seq, nq)
    k3 = k.reshape(batch, seq, -1)
    vt = jnp.swapaxes(v.reshape(batch, seq, -1), 1, 2)
    out = pl.pallas_call(
        _swa_dense_kernel,
        grid=(batch, 2),
        in_specs=[pl.BlockSpec(memory_space=pltpu.SMEM),
                  pl.BlockSpec((1, seq, nq // 2), lambda b, j: (b, 0, j)),
                  pl.BlockSpec((1, seq, LANES), lambda b, j: (b, 0, j)),
                  pl.BlockSpec((1, LANES, seq), lambda b, j: (b, j, 0))],
        out_specs=pl.BlockSpec((1, seq, nq // 2), lambda b, j: (b, 0, j)),
        out_shape=jax.ShapeDtypeStruct((batch, seq, nq), F32),
        compiler_params=_cparams(("parallel", "parallel")),
        name="swa_dense",
    )(sink, q3, k3, vt)
    return out.reshape(batch * seq, nq)


def _swa_window_kernel(sink_ref, q_ref, kc_ref, kp_ref, k0_ref, kn_ref, vc_ref, vp_ref, v0_ref, vn_ref, o_ref):
    i = pl.program_id(2)
    nb = pl.num_programs(2)
    w = SWA_WINDOW
    n_ctx = kc_ref.shape[1]
    k_all = jnp.concatenate([kc_ref[0], kp_ref[0], k0_ref[0], kn_ref[0]], axis=0)
    vt_all = jnp.concatenate([vc_ref[0], vp_ref[0], v0_ref[0], vn_ref[0]], axis=1)
    shape = (n_ctx + 3 * w, SWA_GROUP * w)
    key = lax.broadcasted_iota(jnp.int32, shape, 0) - n_ctx
    r = lax.broadcasted_iota(jnp.int32, shape, 1) % w
    prev_bias = jnp.where(i > 0, 0.0, NEG_BIG)
    next_bias = jnp.where(i < nb - 1, 0.0, NEG_BIG)
    bias = jnp.where(
        key < w,
        jnp.where(key < 0, 0.0, jnp.where(key >= r, prev_bias, NEG_BIG)),
        jnp.where(key < 2 * w, 0.0, jnp.where(key - 2 * w <= r, next_bias, NEG_BIG)))
    o_ref[0] = _sink_attention_t(q_ref[0], k_all, vt_all, bias, sink_ref, pl.program_id(1))


def swa_window(q, k, v, k_ctx, v_ctx, sink, batch, seq, n_ctx):
    nq = SWA_HEADS * SWA_HD
    w = SWA_WINDOW
    nb = seq // w
    q3 = q.reshape(batch, seq, nq)
    k3 = k.reshape(batch, seq, -1)
    vt = jnp.swapaxes(v.reshape(batch, seq, -1), 1, 2)
    vt_ctx = jnp.swapaxes(v_ctx, 1, 2)
    ctx = pl.BlockSpec((1, n_ctx, LANES), lambda b, j, i: (b, 0, j))
    prv = pl.BlockSpec((1, w, LANES), lambda b, j, i: (b, jnp.maximum(i - 1, 0), j))
    cur = pl.BlockSpec((1, w, LANES), lambda b, j, i: (b, i, j))
    nxt = pl.BlockSpec((1, w, LANES), lambda b, j, i: (b, jnp.minimum(i + 1, nb - 1), j))
    ctx_t = pl.BlockSpec((1, LANES, n_ctx), lambda b, j, i: (b, j, 0))
    prv_t = pl.BlockSpec((1, LANES, w), lambda b, j, i: (b, j, jnp.maximum(i - 1, 0)))
    cur_t = pl.BlockSpec((1, LANES, w), lambda b, j, i: (b, j, i))
    nxt_t = pl.BlockSpec((1, LANES, w), lambda b, j, i: (b, j, jnp.minimum(i + 1, nb - 1)))
    out = pl.pallas_call(
        _swa_window_kernel,
        grid=(batch, 2, nb),
        in_specs=[pl.BlockSpec(memory_space=pltpu.SMEM),
                  pl.BlockSpec((1, w, nq // 2), lambda b, j, i: (b, i, j)),
                  ctx, prv, cur, nxt, ctx_t, prv_t, cur_t, nxt_t],
        out_specs=pl.BlockSpec((1, w, nq // 2), lambda b, j, i: (b, i, j)),
        out_shape=jax.ShapeDtypeStruct((batch, seq, nq), F32),
        compiler_params=_cparams(("parallel", "parallel", "arbitrary")),
        name="swa_window",
    )(sink, q3, k_ctx, k3, k3, k3, vt_ctx, vt, vt, vt)
    return out.reshape(batch * seq, nq)


def _router_kernel(x_ref, g_ref, sh_ref, sc_ref, rw_ref, rb_ref, tri_ref, h_ref, idx_ref, gate_ref, rank_ref,
                   cnt_ref, cnt, *, tiles_per_part):
    @pl.when(pl.program_id(0) % tiles_per_part == 0)
    def _():
        cnt[...] = jnp.zeros_like(cnt)

    h = _modnorm(x_ref[...], g_ref[...], sh_ref[0], sc_ref[0])
    bits = lax.bitcast_convert_type(h.astype(BF16).astype(F32), jnp.uint32)
    half = h.shape[1] // 2
    h_ref[...] = (bits[:, :half] >> 16) | (bits[:, half:] & jnp.uint32(0xFFFF0000))
    wh, wl = _split2(rw_ref[...])
    hh, hl = _split2(h)
    logits = _dot_nt(wh, hh) + _dot_nt(wh, hl) + _dot_nt(wl, hh) + rb_ref[...]
    row = lax.broadcasted_iota(jnp.int32, logits.shape, 0)
    work = logits
    vals, idxs = [], []
    for _ in range(TOP_K):
        m = jnp.max(work, axis=0, keepdims=True)
        ik = jnp.min(jnp.where(work == m, row, LANES), axis=0, keepdims=True)
        vals.append(m)
        idxs.append(ik)
        work = jnp.where(row == ik, 2.0 * NEG_BIG, work)
    es = [jnp.exp(v - vals[0]) for v in vals]
    denom = es[0] + es[1] + es[2] + es[3]
    chosen = jnp.zeros(logits.shape, F32)
    for k in range(TOP_K):
        chosen = jnp.where(row == idxs[k], 1.0, chosen)
    chosen_b = chosen.astype(BF16)
    seen = cnt[...]
    before = _dot(chosen_b, tri_ref[...]) + jnp.concatenate([seen] * (logits.shape[1] // LANES), axis=1)
    out_row = lax.broadcasted_iota(jnp.int32, idx_ref.shape, 0)
    idx_out = jnp.zeros(idx_ref.shape, jnp.int32)
    gate_out = jnp.zeros(idx_ref.shape, F32)
    rank_out = jnp.zeros(idx_ref.shape, jnp.int32)
    for k in range(TOP_K):
        rk = jnp.sum(jnp.where(row == idxs[k], before, 0.0), axis=0, keepdims=True)
        idx_out = jnp.where(out_row == k, idxs[k], idx_out)
        gate_out = jnp.where(out_row == k, es[k] / denom, gate_out)
        rank_out = jnp.where(out_row == k, rk.astype(jnp.int32), rank_out)
    idx_ref[...] = idx_out
    gate_ref[...] = gate_out
    rank_ref[...] = rank_out
    cnt[...] = seen + _dot(chosen_b, jnp.ones((logits.shape[1], LANES), BF16))
    cnt_ref[0] = cnt[...]


def moe_router(x, gain, shift, scale, rw_t, rb_col, tokens_per_group, tm=512):
    t, d = x.shape
    tiles_per_group = tokens_per_group // tm
    tiles_per_part = MOE_PART // tm
    grp = lambda i: (i // tiles_per_group, 0, 0)
    tile = lambda w: pl.BlockSpec((tm, w), lambda i: (i, 0))
    slab = pl.BlockSpec((8, tm), lambda i: (0, i))
    tri = jnp.asarray(np.triu(np.ones((tm, tm), np.float32), 1), BF16)
    hp, idx, gates, rank, cnt = pl.pallas_call(
        functools.partial(_router_kernel, tiles_per_part=tiles_per_part),
        grid=(t // tm,),
        in_specs=[tile(d),
                  pl.BlockSpec((1, d), lambda i: (0, 0)),
                  pl.BlockSpec((1, 1, d), grp),
                  pl.BlockSpec((1, 1, d), grp),
                  pl.BlockSpec((LANES, d), lambda i: (0, 0)),
                  pl.BlockSpec((LANES, 1), lambda i: (0, 0)),
                  pl.BlockSpec((tm, tm), lambda i: (0, 0))],
        out_specs=[tile(d // 2), slab, slab, slab,
                   pl.BlockSpec((1, LANES, LANES), lambda i: (i // tiles_per_part, 0, 0))],
        out_shape=[jax.ShapeDtypeStruct((t, d // 2), jnp.uint32),
                   jax.ShapeDtypeStruct((8, t), jnp.int32),
                   jax.ShapeDtypeStruct((8, t), F32),
                   jax.ShapeDtypeStruct((8, t), jnp.int32),
                   jax.ShapeDtypeStruct((t // MOE_PART, LANES, LANES), F32)],
        scratch_shapes=[pltpu.VMEM((LANES, LANES), F32)],
        compiler_params=_cparams(("arbitrary",)),
        name="moe_router",
    )(x, gain.reshape(1, d), shift, scale, rw_t, rb_col, tri)
    per_token = lambda a: a[:TOP_K].T
    return hp, per_token(idx), per_token(gates), per_token(rank), cnt[:, :N_EXPERTS, 0].astype(jnp.int32)


FFN_COLS = 512
FFN_PIECES = 2 * D_FF // FFN_COLS + D_MODEL // FFN_COLS
MOE_PART = 4096
ADD_BATCH = 8


def _unpack_rows(words):
    lo = lax.bitcast_convert_type(words << 16, F32)
    hi = lax.bitcast_convert_type(words & jnp.uint32(0xFFFF0000), F32)
    return jnp.concatenate([lo, hi], axis=1).astype(BF16)


def _ffn_block(x_ref, y_ref, wgu_ref, bgu_ref, wd_ref, bd_ref, between):
    xb = _unpack_rows(x_ref[...])
    acts = []
    for c in range(D_FF // FFN_COLS):
        lo, hi = c * FFN_COLS, (c + 1) * FFN_COLS
        between(2 * c)
        gate = _dot(xb, wgu_ref[0, :, lo:hi]) + bgu_ref[0][:, lo:hi]
        between(2 * c + 1)
        up = _dot(xb, wgu_ref[0, :, D_FF + lo:D_FF + hi]) + bgu_ref[0][:, D_FF + lo:D_FF + hi]
        gate = jnp.minimum(gate, SWIGLU_LIMIT)
        up = jnp.clip(up, -SWIGLU_LIMIT, SWIGLU_LIMIT)
        acts.append((gate * _sigmoid(SWIGLU_ALPHA * gate) * (up + 1.0)).astype(BF16))
    act = jnp.concatenate(acts, axis=1)
    for n in range(D_MODEL // FFN_COLS):
        lo, hi = n * FFN_COLS, (n + 1) * FFN_COLS
        between(2 * D_FF // FFN_COLS + n)
        y_ref[:, lo:hi] = _dot(act, wd_ref[0, :, lo:hi]) + bd_ref[0][:, lo:hi]


def _expert_kernel(nb_ref, bs_ref, loc_ref, gate_ref, hp_hbm, wgu_ref, bgu_ref, wd_ref, bd_ref, out_hbm,
                   hbuf, acc, xbuf, ybuf, sem):
    part = pl.program_id(0)
    e = pl.program_id(1)
    idx = part * N_EXPERTS + e
    nb = nb_ref[idx]
    b0 = bs_ref[idx]
    shares = np.array_split(np.arange(MOE_ROWS), FFN_PIECES)

    def fetch_rows(blk):
        base = blk * MOE_ROWS
        dst = xbuf.at[blk % 2]

        def emit(k):
            for r in shares[k]:
                r = int(r)
                dst[r:r + 1, :] = hbuf[pl.ds(loc_ref[base + r], 1), :]
        return emit

    def add_rows(blk):
        base = blk * MOE_ROWS
        src = ybuf.at[blk % 2]

        def emit(k):
            for batch in np.array_split(shares[k], max(1, len(shares[k]) // ADD_BATCH)):
                rows = [loc_ref[base + int(r)] for r in batch]
                new = [acc[pl.ds(row, 1), :] + gate_ref[base + int(r)] * src[int(r):int(r) + 1, :]
                       for row, r in zip(rows, batch)]
                for row, val in zip(rows, new):
                    acc[pl.ds(row, 1), :] = val
        return emit

    def emit_all(f):
        for k in range(FFN_PIECES):
            f(k)

    @pl.when(e == 0)
    def _():
        load = pltpu.make_async_copy(hp_hbm.at[pl.ds(part * MOE_PART, MOE_PART), :],
                                     hbuf.at[pl.ds(0, MOE_PART), :], sem.at[0])
        load.start()
        hbuf[MOE_PART:, :] = jnp.zeros((hbuf.shape[0] - MOE_PART, hbuf.shape[1]), hbuf.dtype)
        acc[...] = jnp.zeros_like(acc)
        ybuf[...] = jnp.zeros_like(ybuf)
        load.wait()
        emit_all(fetch_rows(b0))

    def block(i, carry):
        fetch, add = fetch_rows(i + 1), add_rows(i - 1)

        def between(k):
            fetch(k)
            add(k)
        _ffn_block(xbuf.at[i % 2], ybuf.at[i % 2], wgu_ref, bgu_ref, wd_ref, bd_ref, between)
        return carry

    lax.fori_loop(b0, b0 + nb, block, 0)

    @pl.when(e == N_EXPERTS - 1)
    def _():
        emit_all(add_rows(b0 + nb - 1))
        store = pltpu.make_async_copy(acc.at[pl.ds(0, MOE_PART), :],
                                      out_hbm.at[pl.ds(part * MOE_PART, MOE_PART), :], sem.at[0])
        store.start()
        store.wait()


def moe_experts(hp, n_blk, blk_start, row_loc, row_gate, w_gu, b_gu, w_down, b_down):
    t = hp.shape[0]
    d = D_MODEL
    per_expert = lambda p, e, nb, bs, sl: (e, 0, 0)
    grid_spec = pltpu.PrefetchScalarGridSpec(
        num_scalar_prefetch=3,
        grid=(t // MOE_PART, N_EXPERTS),
        in_specs=[pl.BlockSpec(memory_space=pltpu.SMEM),
                  pl.BlockSpec(memory_space=pl.ANY),
                  pl.BlockSpec((1, d, 2 * D_FF), per_expert),
                  pl.BlockSpec((1, 1, 2 * D_FF), per_expert),
                  pl.BlockSpec((1, D_FF, d), per_expert),
                  pl.BlockSpec((1, 1, d), per_expert)],
        out_specs=pl.BlockSpec(memory_space=pl.ANY),
        scratch_shapes=[pltpu.VMEM((MOE_PART + 8, d // 2), jnp.uint32),
                        pltpu.VMEM((MOE_PART + 8, d), F32),
                        pltpu.VMEM((2, MOE_ROWS, d // 2), jnp.uint32),
                        pltpu.VMEM((2, MOE_ROWS, d), F32),
                        pltpu.SemaphoreType.DMA((1,))],
    )
    return pl.pallas_call(
        _expert_kernel,
        grid_spec=grid_spec,
        out_shape=jax.ShapeDtypeStruct((t, d), F32),
        compiler_params=pltpu.CompilerParams(dimension_semantics=("arbitrary", "arbitrary"),
                                             vmem_limit_bytes=EXPERT_VMEM_LIMIT),
        name="moe_experts",
    )(n_blk, blk_start, row_loc, row_gate, hp, w_gu, b_gu.reshape(N_EXPERTS, 1, -1), w_down,
      b_down.reshape(N_EXPERTS, 1, -1))


def _residual_kernel(y_ref, x_ref, g_ref, o_ref):
    o_ref[...] = x_ref[...] + g_ref[0] * y_ref[...]


def gated_residual(y, x, gate_mod, tokens_per_group, tm=512):
    t, d = x.shape
    tiles_per_group = tokens_per_group // tm
    tile = pl.BlockSpec((tm, d), lambda i: (i, 0))
    return pl.pallas_call(
        _residual_kernel,
        grid=(t // tm,),
        in_specs=[tile, tile, pl.BlockSpec((1, 1, d), lambda i: (i // tiles_per_group, 0, 0))],
        out_specs=tile,
        out_shape=jax.ShapeDtypeStruct((t, d), F32),
        compiler_params=_cparams(("parallel",)),
        name="moe_residual",
    )(y, x, gate_mod)


def _assignment_tables(top_idx, rank, gates, counts):
    t = top_idx.shape[0]
    n_parts = t // MOE_PART
    n_assign = t * TOP_K
    part_blocks = MOE_PART * TOP_K // MOE_ROWS + N_EXPERTS + 1
    n_rows = (1 + n_parts * part_blocks) * MOE_ROWS
    padded = (counts + MOE_ROWS - 1) // MOE_ROWS * MOE_ROWS
    part_first = (1 + jnp.arange(n_parts, dtype=jnp.int32) * part_blocks) * MOE_ROWS
    row_start = part_first[:, None] + jnp.cumsum(padded, axis=1) - padded
    experts = jnp.arange(N_EXPERTS, dtype=jnp.int32)
    by_part = top_idx.reshape(n_parts, MOE_PART, TOP_K)
    start_of = jnp.sum(jnp.where(by_part[..., None] == experts, row_start[:, None, None, :], 0), axis=-1)
    dest = (start_of.reshape(t, TOP_K) + rank).reshape(-1).astype(jnp.int32)
    local = (jnp.arange(n_assign, dtype=jnp.int32) // TOP_K) % MOE_PART
    vals = jnp.stack([local, lax.bitcast_convert_type(gates.reshape(-1), jnp.int32)], axis=1)
    init = jnp.broadcast_to(jnp.array([MOE_PART, 0], jnp.int32), (n_rows, 2))
    table = init.at[dest].set(vals)
    row_loc = table[:, 0]
    row_gate = lax.bitcast_convert_type(table[:, 1], F32)
    return row_loc, row_gate, (padded // MOE_ROWS).reshape(-1).astype(jnp.int32), \
        (row_start // MOE_ROWS).reshape(-1).astype(jnp.int32)


def moe_layer(x, p, shift, scale, gate_mod, tokens_per_group):
    hp, idx, gates, rank, counts = moe_router(x, p['norm2'], shift, scale, p['rw_t'], p['rb_col'], tokens_per_group)
    row_loc, row_gate, n_blk, blk_start = _assignment_tables(idx, rank, gates, counts)
    y = moe_experts(hp, n_blk, blk_start, row_loc, row_gate, p['w_gu_bf'], p['b_gu'], p['w_down_bf'], p['b_down'])
    return gated_residual(y, x, gate_mod, tokens_per_group)


def _pad_lanes(a, width):
    return jnp.pad(a, [(0, 0)] * (a.ndim - 1) + [(0, width - a.shape[-1])])


def _prep_common(p):
    p['rw_t'] = _pad_lanes(p['router_w'], LANES).T
    p['rb_col'] = jnp.concatenate(
        [p['router_b'].astype(F32), jnp.full((LANES - N_EXPERTS,), NEG_BIG, F32)]).reshape(LANES, 1)
    p['w_out'] = p['w_out'].astype(BF16)
    p['w_gu_bf'] = p['w_gu'].astype(BF16)
    p['w_down_bf'] = p['w_down'].astype(BF16)
    return p


def _prep_layer0(p):
    p = _prep_common(dict(p))
    p['w_in'] = _pad_lanes(p['w_in'], AB_IN_PAD).astype(BF16)
    q_b = p['q_b'].reshape(MLA_Q_LORA, MLA_HEADS, MLA_QK)
    p['q_b_pad'] = _pad_lanes(q_b, LANES).reshape(MLA_Q_LORA, MLA_HEADS * LANES).astype(BF16)
    kv_b = p['kv_b'].reshape(MLA_KV_LORA, MLA_HEADS, MLA_NOPE + MLA_V)
    p['wk_pad'] = _pad_lanes(kv_b[:, :, :MLA_NOPE], LANES).reshape(MLA_KV_LORA, MLA_HEADS * LANES).astype(BF16)
    p['wv'] = kv_b[:, :, MLA_NOPE:].reshape(MLA_KV_LORA, MLA_HEADS * MLA_V).astype(BF16)
    p['q_norm_pad'] = _pad_lanes(p['q_norm'].reshape(1, -1), LANES)
    p['k_norm_pad'] = _pad_lanes(p['k_norm'].reshape(1, -1), LANES)
    return p


def _prep_layer1(p):
    p = _prep_common(dict(p))
    p['w_in'] = p['w_in'].astype(BF16)
    p['q_norm2'] = jnp.tile(p['q_norm'].reshape(1, -1), (1, LANES // SWA_HD))
    p['k_norm2'] = jnp.tile(p['k_norm'].reshape(1, -1), (1, LANES // SWA_HD))
    return p


def _group_forward(x3, mods, p0, p1, lb_logits, caches, latent):
    batch, seq, d = x3.shape
    t = batch * seq
    x = x3.reshape(t, d)
    tpg = seq if latent else t
    sh1, sc1, g1, sh2, sc2, g2 = mods[0]

    z = modnorm_matmul(x, p0['norm1'], sh1, sc1, p0['w_in'], tpg)
    tab_b = _rope_lane_tables(seq, MLA_ROPE, LANES, MLA_NOPE) if latent else None
    s0f, s0b = (caches['hg_f'], caches['hg_b']) if latent else (None, None)
    o_f, o_b, s_f, s_b = hgrn_bidir(z, lb_logits, s0f, s0b, batch, seq)
    q = mla_queries(z, p0['qa_norm'], p0['q_b_pad'], p0['q_norm_pad'], tab_b, seq)
    ckv, k, v = mla_keys_values(z, AB_COL_KVA // LANES, z, AB_COL_KPE // LANES, p0['kva_norm'], p0['wk_pad'], p0['wv'],
                                p0['k_norm_pad'], tab_b, seq, norm_input=True)
    n_k = seq
    if latent:
        n_ctx = caches['ckv'].shape[1]
        ckv_c = caches['ckv'].reshape(batch * n_ctx, MLA_KV_LORA)
        kpe_c = _pad_lanes(caches['kpe'].reshape(batch * n_ctx, MLA_ROPE), LANES)
        _, k_c, v_c = mla_keys_values(ckv_c, 0, kpe_c, 0, p0['kva_norm'], p0['wk_pad'], p0['wv'],
                                      p0['k_norm_pad'], None, n_ctx, norm_input=False)
        cat = lambda a, b: jnp.concatenate([a.reshape(batch, n_ctx, -1), b.reshape(batch, seq, -1)],
                                           axis=1).reshape(batch * (n_ctx + seq), -1)
        k, v = cat(k_c, k), cat(v_c, v)
        n_k = n_ctx + seq
    o_mla = mla_attention(q, k, v, batch, seq, n_k)
    x = out_proj_layer0(o_f, o_b, z, p0['hg_out_norm'], o_mla, p0['w_out'], x, g1, tpg)
    x = moe_layer(x, p0, sh2, sc2, g2, tpg)
    state0 = (s_f, s_b, ckv.reshape(batch, seq, MLA_KV_LORA), z[:, AB_COL_KPE:AB_COL_KPE + MLA_ROPE].reshape(batch, seq, MLA_ROPE))

    sh1, sc1, g1, sh2, sc2, g2 = mods[1]
    z = modnorm_matmul(x, p1['norm1'], sh1, sc1, p1['w_in'], tpg)
    tab_c = _rope_lane_tables(seq, SWA_HD, SWA_HD, 0) if latent else None
    q, k_cache, k, v = swa_prep(z, p1['q_norm2'], p1['k_norm2'], tab_c, seq)
    sink = p1['sink'].astype(F32)
    if latent:
        n_ctx = caches['k1'].shape[1]
        k_c = caches['k1'].reshape(batch, n_ctx, -1).astype(BF16)
        v_c = caches['v1'].reshape(batch, n_ctx, -1).astype(BF16)
        a = swa_window(q, k, v, k_c, v_c, sink, batch, seq, n_ctx)
    else:
        a = swa_dense(q, k, v, sink, batch, seq)
    x = out_proj_layer1(a, p1['w_out'], x, g1, tpg)
    x = moe_layer(x, p1, sh2, sc2, g2, tpg)
    nkv = SWA_KV_HEADS * SWA_HD
    state1 = (k_cache.reshape(batch, seq, SWA_KV_HEADS, SWA_HD),
              z[:, SWA_HEADS * SWA_HD + nkv:].reshape(batch, seq, SWA_KV_HEADS, SWA_HD))
    return x.reshape(batch, seq, d), state0, state1


def kernel(x_prompt, x_sample, state_l0_hgrn_fwd, state_l0_hgrn_bwd, cache_l0_mla_ckv, cache_l0_mla_kpe, cache_l1_k, cache_l1_v, c, c_ctx, hgrn_lb_logits, l0_ada_w, l0_ada_b, l0_norm1, l0_norm2, l0_w_in, l0_hg_out_norm, l0_qa_norm, l0_q_b, l0_kva_norm, l0_kv_b, l0_q_norm, l0_k_norm, l0_w_out, l0_router_w, l0_router_b, l0_w_gu, l0_b_gu, l0_w_down, l0_b_down, l1_ada_w, l1_ada_b, l1_norm1, l1_norm2, l1_w_in, l1_q_norm, l1_k_norm, l1_sink, l1_w_out, l1_router_w, l1_router_b, l1_w_gu, l1_b_gu, l1_w_down, l1_b_down):
    p0 = _prep_layer0(dict(norm1=l0_norm1, norm2=l0_norm2, w_in=l0_w_in, hg_out_norm=l0_hg_out_norm,
                           qa_norm=l0_qa_norm, q_b=l0_q_b, kva_norm=l0_kva_norm, kv_b=l0_kv_b,
                           q_norm=l0_q_norm, k_norm=l0_k_norm, w_out=l0_w_out, router_w=l0_router_w,
                           router_b=l0_router_b, w_gu=l0_w_gu, b_gu=l0_b_gu, w_down=l0_w_down,
                           b_down=l0_b_down))
    p1 = _prep_layer1(dict(norm1=l1_norm1, norm2=l1_norm2, w_in=l1_w_in, q_norm=l1_q_norm, k_norm=l1_k_norm,
                           sink=l1_sink, w_out=l1_w_out, router_w=l1_router_w, router_b=l1_router_b,
                           w_gu=l1_w_gu, b_gu=l1_b_gu, w_down=l1_w_down, b_down=l1_b_down))
    dec_batch = c.shape[0]
    d = c.shape[1]
    cond8 = jnp.concatenate([c_ctx[None, :], c, jnp.zeros((8 - 1 - dec_batch, d), F32)], axis=0)
    mods_ctx, mods_lat = [], []
    for w, b in ((l0_ada_w, l0_ada_b), (l1_ada_w, l1_ada_b)):
        mod = ada_params(cond8, w, b)
        mods_ctx.append([m.reshape(1, 1, d) for m in jnp.split(mod[0:1], 6, axis=-1)])
        mods_lat.append([m.reshape(dec_batch, 1, d) for m in jnp.split(mod[1:1 + dec_batch], 6, axis=-1)])

    y_prompt, st0, st1 = _group_forward(x_prompt, mods_ctx, p0, p1, hgrn_lb_logits, None, latent=False)
    caches = dict(hg_f=state_l0_hgrn_fwd, hg_b=state_l0_hgrn_bwd, ckv=cache_l0_mla_ckv, kpe=cache_l0_mla_kpe,
                  k1=cache_l1_k, v1=cache_l1_v)
    y_sample, _, _ = _group_forward(x_sample, mods_lat, p0, p1, hgrn_lb_logits, caches, latent=True)
    return (y_prompt, y_sample, st0[0], st0[1], st0[2], st0[3], st1[0], st1[1])
```

```python
import functools

import numpy as np
import jax
import jax.numpy as jnp
from jax import lax
from jax.experimental import pallas as pl
from jax.experimental.pallas import tpu as pltpu

F32 = jnp.float32
BF16 = jnp.bfloat16

D_MODEL = 1024
GRID_W = 64
ROPE_THETA = 10000.0
EPS = 1e-6
HG_HEADS = 4
HG_DK = 128
HG_DV = 128
MLA_HEADS = 8
MLA_NOPE = 64
MLA_ROPE = 32
MLA_V = 64
MLA_QK = MLA_NOPE + MLA_ROPE
MLA_Q_LORA = 256
MLA_KV_LORA = 128
MLA_SCALE = MLA_QK ** -0.5
SWA_HEADS = 16
SWA_KV_HEADS = 4
SWA_HD = 64
SWA_WINDOW = 128
SWA_SCALE = SWA_HD ** -0.5
SWA_GROUP = SWA_HEADS // SWA_KV_HEADS
N_EXPERTS = 32
TOP_K = 4
D_FF = 1024
SWIGLU_LIMIT = 7.0
SWIGLU_ALPHA = 1.702

LANES = 128
HG_CHUNK = 128
HG_LEVELS = 7
HG_MXU_LEVELS = 0
AB_IN_PAD = 3072
AB_COL_GATE = 3 * HG_HEADS * HG_DK + HG_HEADS * HG_DV
AB_COL_QA = AB_COL_GATE + HG_HEADS * HG_DV
AB_COL_KVA = AB_COL_QA + MLA_Q_LORA
AB_COL_KPE = AB_COL_KVA + MLA_KV_LORA
MOE_ROWS = 256
NEG_BIG = -1e30
VMEM_LIMIT = 48 * 1024 * 1024
EXPERT_VMEM_LIMIT = 56 * 1024 * 1024


def _cparams(sem):
    return pltpu.CompilerParams(dimension_semantics=sem, vmem_limit_bytes=VMEM_LIMIT)


def _dot(a, b):
    return jnp.dot(a, b, preferred_element_type=F32)


def _dot_nt(a, b):
    return lax.dot_general(a, b, (((1,), (1,)), ((), ())), preferred_element_type=F32)


def _dot_tn(a, b):
    return lax.dot_general(a, b, (((0,), (0,)), ((), ())), preferred_element_type=F32)


def _split2(x):
    hi = x.astype(BF16)
    lo = (x - hi.astype(F32)).astype(BF16)
    return hi, lo


def _dot_hp(a, b):
    ah, al = _split2(a)
    bh, bl = _split2(b)
    return _dot(ah, bh) + _dot(ah, bl) + _dot(al, bh)


def _sigmoid(x):
    return 1.0 / (1.0 + jnp.exp(-x))


def _modnorm(x, gain, shift, scale):
    y = x * lax.rsqrt(jnp.mean(x * x, axis=-1, keepdims=True) + EPS)
    return y * gain * (1.0 + scale) + shift


def _ada_kernel(c_ref, w_ref, b_ref, o_ref):
    c = c_ref[...]
    o_ref[...] = _dot_hp(c * _sigmoid(c), w_ref[...]) + b_ref[...]


def ada_params(cond8, w, b):
    n = w.shape[1]
    tn = 1024
    return pl.pallas_call(
        _ada_kernel,
        grid=(n // tn,),
        in_specs=[pl.BlockSpec((8, D_MODEL), lambda j: (0, 0)),
                  pl.BlockSpec((D_MODEL, tn), lambda j: (0, j)),
                  pl.BlockSpec((1, tn), lambda j: (0, j))],
        out_specs=pl.BlockSpec((8, tn), lambda j: (0, j)),
        out_shape=jax.ShapeDtypeStruct((8, n), F32),
        compiler_params=_cparams(("parallel",)),
        name="ada_params",
    )(cond8, w, b.reshape(1, n))


def _modnorm_matmul_kernel(x_ref, g_ref, sh_ref, sc_ref, w_ref, o_ref):
    h = _modnorm(x_ref[...], g_ref[...], sh_ref[0], sc_ref[0])
    o_ref[...] = _dot(h.astype(BF16), w_ref[...])


def modnorm_matmul(x, gain, shift, scale, w_bf16, tokens_per_group, tm=512):
    t, d = x.shape
    n = w_bf16.shape[1]
    tiles_per_group = tokens_per_group // tm
    grp = lambda i: (i // tiles_per_group, 0, 0)
    return pl.pallas_call(
        _modnorm_matmul_kernel,
        grid=(t // tm,),
        in_specs=[pl.BlockSpec((tm, d), lambda i: (i, 0)),
                  pl.BlockSpec((1, d), lambda i: (0, 0)),
                  pl.BlockSpec((1, 1, d), grp),
                  pl.BlockSpec((1, 1, d), grp),
                  pl.BlockSpec((d, n), lambda i: (0, 0))],
        out_specs=pl.BlockSpec((tm, n), lambda i: (i, 0)),
        out_shape=jax.ShapeDtypeStruct((t, n), F32),
        compiler_params=_cparams(("parallel",)),
        name="modnorm_matmul",
    )(x, gain.reshape(1, d), shift, scale, w_bf16)


def _hgrn_constants():
    c = HG_CHUNK
    t = np.arange(c)[:, None]
    u = np.arange(c)[None, :]
    mats = [(u <= t), (u > t)]
    for l in range(HG_MXU_LEVELS):
        m = 1 << l
        r = (t // (2 * m)) * (2 * m) + m - 1
        mats.append((u > np.minimum(t, r)) & (u <= np.maximum(t, r)))
    fwd = np.concatenate(mats, axis=0).astype(np.float32)
    bwd = np.concatenate([mm[::-1, ::-1] for mm in mats], axis=0).astype(np.float32)
    x = np.bitwise_xor(t, u)
    lvl = np.where(x > 0, np.floor(np.log2(np.maximum(x, 1))), HG_LEVELS).astype(np.int32)
    lv_f = np.where(t >= u, lvl, -1).astype(np.int32)
    return fwd, bwd, lv_f, lv_f.T.copy()


def _hgrn_chunk(qs, fpres, vs, lbs, sts, forwards, mcats, lvs):
    c = HG_CHUNK
    n = len(qs)
    fs = [lb + (1.0 - lb) * _sigmoid(fp) for lb, fp in zip(lbs, fpres)]
    kks = [1.0 - f for f in fs]
    logs = [_split2(jnp.log(f)) for f in fs]
    xs = [None] * n
    for fwd in (True, False):
        ids = [i for i in range(n) if forwards[i] == fwd]
        parts = [half for i in ids for half in logs[i]]
        x_all = _dot(mcats[0 if fwd else 1], jnp.concatenate(parts, axis=1))
        for j, i in enumerate(ids):
            xs[i] = x_all[:, 2 * j * c:(2 * j + 1) * c] + x_all[:, (2 * j + 1) * c:(2 * j + 2) * c]
    gs = [x[0:c] for x in xs]
    lv_of = [lvs[0 if f else 1] for f in forwards]
    qbs = [q.astype(BF16) for q in qs]
    kbs = [kk.astype(BF16) for kk in kks]
    vbs = [v.astype(BF16) for v in vs]
    os_ = [_dot_nt((q * jnp.exp(g)).astype(BF16), st.astype(BF16)) for q, g, st in zip(qs, gs, sts)]
    accs = [jnp.where(lv == HG_LEVELS, _dot_nt(qb, kb), 0.0) for lv, qb, kb in zip(lv_of, qbs, kbs)]
    for l in range(HG_LEVELS):
        for i in range(n):
            if l < HG_MXU_LEVELS:
                x = xs[i][(2 + l) * c:(3 + l) * c]
            else:
                m = 1 << l
                ref_rows = [j * 2 * m + (m - 1 if forwards[i] else m) for j in range(c // (2 * m))]
                g_ref = jnp.concatenate(
                    [jnp.broadcast_to(gs[i][r:r + 1, :], (2 * m, c)) for r in ref_rows], axis=0)
                x = -jnp.abs(gs[i] - g_ref)
            e = jnp.exp(x)
            p = _dot_nt((qs[i] * e).astype(BF16), (kks[i] * e).astype(BF16))
            accs[i] = jnp.where(lv_of[i] == l, p, accs[i])
    outs, new_sts = [], []
    for i in range(n):
        edge_row = c - 1 if forwards[i] else 0
        outs.append(os_[i] + _dot(accs[i].astype(BF16), vbs[i]))
        k_end = (kks[i] * jnp.exp(xs[i][c:2 * c])).astype(BF16)
        new_sts.append(sts[i] * jnp.exp(gs[i][edge_row:edge_row + 1, :]) + _dot_tn(vbs[i], k_end))
    return outs, new_sts


def _hgrn_kernel(*refs, has_init):
    if has_init:
        (qf_ref, qb_ref, ff_ref, fb_ref, vf_ref, vb_ref, lbl_ref, mf_ref, mb_ref, lvf_ref, lvb_ref,
         s0f_ref, s0b_ref, of_ref, ob_ref, sf_ref, sb_ref, stf, stb) = refs
    else:
        (qf_ref, qb_ref, ff_ref, fb_ref, vf_ref, vb_ref, lbl_ref, mf_ref, mb_ref, lvf_ref, lvb_ref,
         of_ref, ob_ref, sf_ref, sb_ref, stf, stb) = refs
    c = pl.program_id(1)
    nc = pl.num_programs(1)

    @pl.when(c == 0)
    def _():
        for h in range(HG_HEADS):
            if has_init:
                stf[h] = s0f_ref[0, h].T
                stb[h] = s0b_ref[0, h].T
            else:
                stf[h] = jnp.zeros((HG_DV, HG_DK), F32)
                stb[h] = jnp.zeros((HG_DV, HG_DK), F32)

    rows = [lbl_ref[:, j, :] for j in range(lbl_ref.shape[1])]
    mx = functools.reduce(jnp.maximum, rows)
    ex = [jnp.exp(r - mx) for r in rows]
    lb = ex[0] / functools.reduce(lambda a, b: a + b, ex)

    heads = [slice(h * LANES, (h + 1) * LANES) for h in range(HG_HEADS)]
    nh = HG_HEADS
    outs, sts = _hgrn_chunk(
        [qf_ref[0, :, hs] for hs in heads] + [qb_ref[0, :, hs] for hs in heads],
        [ff_ref[0, :, hs] for hs in heads] + [fb_ref[0, :, hs] for hs in heads],
        [vf_ref[0, :, hs] for hs in heads] + [vb_ref[0, :, hs] for hs in heads],
        [lb[0:1, hs] for hs in heads] + [lb[1:2, hs] for hs in heads],
        [stf[h] for h in range(nh)] + [stb[h] for h in range(nh)],
        [True] * nh + [False] * nh,
        (mf_ref[...], mb_ref[...]), (lvf_ref[...], lvb_ref[...]))
    o_f, o_b, st_f, st_b = outs[:nh], outs[nh:], sts[:nh], sts[nh:]
    for h, hs in enumerate(heads):
        of_ref[0, :, hs] = o_f[h]
        ob_ref[0, :, hs] = o_b[h]
        stf[h] = st_f[h]
        stb[h] = st_b[h]

    @pl.when(c == nc - 1)
    def _():
        for h in range(HG_HEADS):
            sf_ref[0, h] = stf[h].T
            sb_ref[0, h] = stb[h].T


def hgrn_bidir(z, lb_logits, s0f, s0b, batch, seq):
    nc = seq // HG_CHUNK
    z3 = z.reshape(batch, seq, z.shape[1])
    mf, mb, lvf, lvb = _hgrn_constants()
    has_init = s0f is not None
    width = HG_HEADS * LANES
    blk = (1, HG_CHUNK, width)
    fwd = lambda off: pl.BlockSpec(blk, lambda b, c: (b, c, off))
    bwd = lambda off: pl.BlockSpec(blk, lambda b, c: (b, nc - 1 - c, off))
    full = lambda a: pl.BlockSpec(a.shape, lambda b, c: (0,) * a.ndim)
    st_spec = pl.BlockSpec((1, HG_HEADS, HG_DK, HG_DV), lambda b, c: (b, 0, 0, 0))
    consts = [jnp.asarray(mf, BF16), jnp.asarray(mb, BF16), jnp.asarray(lvf), jnp.asarray(lvb)]
    in_specs = [fwd(0), bwd(0), fwd(1), bwd(2), fwd(3), bwd(3),
                pl.BlockSpec(lb_logits.shape, lambda b, c: (0, 0, 0))]
    in_specs += [full(a) for a in consts]
    args = [z3] * 6 + [lb_logits] + consts
    if has_init:
        in_specs += [st_spec, st_spec]
        args += [s0f, s0b]
    o_shape = jax.ShapeDtypeStruct((batch, seq, width), F32)
    s_shape = jax.ShapeDtypeStruct((batch, HG_HEADS, HG_DK, HG_DV), F32)
    o_f, o_b, s_f, s_b = pl.pallas_call(
        functools.partial(_hgrn_kernel, has_init=has_init),
        grid=(batch, nc),
        in_specs=in_specs,
        out_specs=[pl.BlockSpec(blk, lambda b, c: (b, c, 0)),
                   pl.BlockSpec(blk, lambda b, c: (b, nc - 1 - c, 0)),
                   st_spec, st_spec],
        out_shape=[o_shape, o_shape, s_shape, s_shape],
        scratch_shapes=[pltpu.VMEM((HG_HEADS, HG_DV, HG_DK), F32), pltpu.VMEM((HG_HEADS, HG_DV, HG_DK), F32)],
        compiler_params=_cparams(("parallel", "arbitrary")),
        name="hgrn_bidir",
    )(*args)
    t = batch * seq
    return o_f.reshape(t, -1), o_b.reshape(t, -1), s_f, s_b


def _axial_tables(n_tokens, n_rot):
    t = jnp.arange(n_tokens)
    row = (t // GRID_W).astype(F32)
    col = (t % GRID_W).astype(F32)
    n_freq = n_rot // 4
    inv = jnp.power(ROPE_THETA, -jnp.arange(n_freq, dtype=F32) / n_freq)
    ang = jnp.concatenate([row[:, None] * inv, col[:, None] * inv], axis=-1)
    return jnp.cos(ang), jnp.sin(ang)


def _rope_lane_tables(n_tokens, n_rot, head_width, first_rot_lane):
    cos, sin = _axial_tables(n_tokens, n_rot)
    half = n_rot // 2
    c_head = jnp.ones((n_tokens, head_width), F32)
    sa_head = jnp.zeros((n_tokens, head_width), F32)
    sb_head = jnp.zeros((n_tokens, head_width), F32)
    a0, a1, a2 = first_rot_lane, first_rot_lane + half, first_rot_lane + n_rot
    c_head = c_head.at[:, a0:a1].set(cos).at[:, a1:a2].set(cos)
    sa_head = sa_head.at[:, a0:a1].set(-sin)
    sb_head = sb_head.at[:, a1:a2].set(sin)
    reps = LANES // head_width
    tile = lambda a: jnp.tile(a, (1, reps))
    return tile(c_head), tile(sa_head), tile(sb_head)


def _rope(x, c, sa, sb, half):
    return x * c + pltpu.roll(x, LANES - half, 1) * sa + pltpu.roll(x, half, 1) * sb


def _mla_q_kernel(*refs, rope):
    if rope:
        qa_ref, qan_ref, qb_ref, qn_ref, c_ref, sa_ref, sb_ref, o_ref = refs
    else:
        qa_ref, qan_ref, qb_ref, qn_ref, o_ref = refs
    qa = qa_ref[...]
    qn = qa * lax.rsqrt(jnp.mean(qa * qa, axis=-1, keepdims=True) + EPS) * qan_ref[...]
    qfull = _dot(qn.astype(BF16), qb_ref[...])
    outs = []
    for h in range(MLA_HEADS):
        qh = qfull[:, h * LANES:(h + 1) * LANES]
        ms = jnp.sum(qh * qh, axis=-1, keepdims=True) * (1.0 / MLA_QK)
        qh = qh * lax.rsqrt(ms + EPS) * qn_ref[...]
        if rope:
            qh = _rope(qh, c_ref[...], sa_ref[...], sb_ref[...], MLA_ROPE // 2)
        outs.append((qh * MLA_SCALE).astype(BF16))
    o_ref[...] = jnp.concatenate(outs, axis=1)


def mla_queries(z, qa_norm, q_b_pad, q_norm_pad, tables, tokens_per_batch, tm=512):
    t = z.shape[0]
    rope = tables is not None
    row = lambda i: (0, 0)
    in_specs = [pl.BlockSpec((tm, MLA_Q_LORA), lambda i: (i, AB_COL_QA // MLA_Q_LORA)),
                pl.BlockSpec((1, MLA_Q_LORA), row),
                pl.BlockSpec(q_b_pad.shape, row),
                pl.BlockSpec((1, LANES), row)]
    args = [z, qa_norm.reshape(1, -1), q_b_pad, q_norm_pad]
    if rope:
        tpb = tokens_per_batch // tm
        in_specs += [pl.BlockSpec((tm, LANES), lambda i: (i % tpb, 0))] * 3
        args += list(tables)
    return pl.pallas_call(
        functools.partial(_mla_q_kernel, rope=rope),
        grid=(t // tm,),
        in_specs=in_specs,
        out_specs=pl.BlockSpec((tm, MLA_HEADS * LANES), lambda i: (i, 0)),
        out_shape=jax.ShapeDtypeStruct((t, MLA_HEADS * LANES), BF16),
        compiler_params=_cparams(("parallel",)),
        name="mla_queries",
    )(*args)


def _mla_kv_kernel(*refs, norm_input, rope):
    if rope:
        kva_ref, kpe_ref, kvan_ref, wk_ref, wv_ref, kn_ref, c_ref, sa_ref, sb_ref, ckv_ref, k_ref, v_ref = refs
    else:
        kva_ref, kpe_ref, kvan_ref, wk_ref, wv_ref, kn_ref, ckv_ref, k_ref, v_ref = refs
    ckv = kva_ref[...]
    if norm_input:
        ckv = ckv * lax.rsqrt(jnp.mean(ckv * ckv, axis=-1, keepdims=True) + EPS) * kvan_ref[...]
    ckv_ref[...] = ckv
    cb = ckv.astype(BF16)
    knope = _dot(cb, wk_ref[...])
    v_ref[...] = _dot(cb, wv_ref[...]).astype(BF16)
    kpe = pltpu.roll(kpe_ref[...], MLA_NOPE, 1)
    outs = []
    for h in range(MLA_HEADS):
        kh = knope[:, h * LANES:(h + 1) * LANES] + kpe
        ms = jnp.sum(kh * kh, axis=-1, keepdims=True) * (1.0 / MLA_QK)
        kh = kh * lax.rsqrt(ms + EPS) * kn_ref[...]
        if rope:
            kh = _rope(kh, c_ref[...], sa_ref[...], sb_ref[...], MLA_ROPE // 2)
        outs.append(kh.astype(BF16))
    k_ref[...] = jnp.concatenate(outs, axis=1)


def mla_keys_values(kva_src, kva_col, kpe_src, kpe_col, kva_norm, wk_pad, wv, k_norm_pad, tables,
                    tokens_per_batch, norm_input, tm=512):
    t = kva_src.shape[0]
    rope = tables is not None
    row = lambda i: (0, 0)
    in_specs = [pl.BlockSpec((tm, LANES), lambda i: (i, kva_col)),
                pl.BlockSpec((tm, LANES), lambda i: (i, kpe_col)),
                pl.BlockSpec((1, LANES), row),
                pl.BlockSpec(wk_pad.shape, row),
                pl.BlockSpec(wv.shape, row),
                pl.BlockSpec((1, LANES), row)]
    args = [kva_src, kpe_src, kva_norm.reshape(1, -1), wk_pad, wv, k_norm_pad]
    if rope:
        tpb = tokens_per_batch // tm
        in_specs += [pl.BlockSpec((tm, LANES), lambda i: (i % tpb, 0))] * 3
        args += list(tables)
    return pl.pallas_call(
        functools.partial(_mla_kv_kernel, norm_input=norm_input, rope=rope),
        grid=(t // tm,),
        in_specs=in_specs,
        out_specs=[pl.BlockSpec((tm, LANES), lambda i: (i, 0)),
                   pl.BlockSpec((tm, MLA_HEADS * LANES), lambda i: (i, 0)),
                   pl.BlockSpec((tm, MLA_HEADS * MLA_V), lambda i: (i, 0))],
        out_shape=[jax.ShapeDtypeStruct((t, LANES), F32),
                   jax.ShapeDtypeStruct((t, MLA_HEADS * LANES), BF16),
                   jax.ShapeDtypeStruct((t, MLA_HEADS * MLA_V), BF16)],
        compiler_params=_cparams(("parallel",)),
        name="mla_keys_values",
    )(*args)


def _mla_attn_kernel(q_ref, k_ref, vt_ref, o_ref):
    heads = range(q_ref.shape[2] // LANES)
    ss = [_dot_nt(k_ref[0][:, j * LANES:(j + 1) * LANES], q_ref[0][:, j * LANES:(j + 1) * LANES]) for j in heads]
    ms = [jnp.max(s, axis=0, keepdims=True) for s in ss]
    ps = [jnp.exp(s - m) for s, m in zip(ss, ms)]
    ls = [jnp.sum(p, axis=0, keepdims=True) for p in ps]
    ots = [_dot(vt_ref[0][j * MLA_V:(j + 1) * MLA_V, :], ps[j].astype(BF16)) / ls[j] for j in heads]
    for j in range(0, len(ots), 2):
        o_ref[0, :, j * MLA_V:(j + 2) * MLA_V] = jnp.concatenate(ots[j:j + 2], axis=0).T


def mla_attention(q, k, v, batch, n_q, n_k, tq=512):
    tq = min(tq, n_q)
    hps = 2 if n_k > 1024 else MLA_HEADS
    q3 = q.reshape(batch, n_q, -1)
    k3 = k.reshape(batch, n_k, -1)
    vt = jnp.swapaxes(v.reshape(batch, n_k, -1), 1, 2)
    out = pl.pallas_call(
        _mla_attn_kernel,
        grid=(batch, MLA_HEADS // hps, n_q // tq),
        in_specs=[pl.BlockSpec((1, tq, hps * LANES), lambda b, j, i: (b, i, j)),
                  pl.BlockSpec((1, n_k, hps * LANES), lambda b, j, i: (b, 0, j)),
                  pl.BlockSpec((1, hps * MLA_V, n_k), lambda b, j, i: (b, j, 0))],
        out_specs=pl.BlockSpec((1, tq, hps * MLA_V), lambda b, j, i: (b, i, j)),
        out_shape=jax.ShapeDtypeStruct((batch, n_q, MLA_HEADS * MLA_V), F32),
        compiler_params=_cparams(("parallel", "parallel", "arbitrary")),
        name="mla_attention",
    )(q3, k3, vt)
    return out.reshape(batch * n_q, -1)


def _out0_kernel(of_ref, ob_ref, ag_ref, hgn_ref, om_ref, w_ref, x_ref, g_ref, o_ref):
    o = of_ref[...] + ob_ref[...]
    ag = ag_ref[...]
    parts = []
    for h in range(HG_HEADS):
        oh = o[:, h * HG_DV:(h + 1) * HG_DV]
        oh = oh * lax.rsqrt(jnp.mean(oh * oh, axis=-1, keepdims=True) + EPS) * hgn_ref[...]
        gh = ag[:, h * HG_DV:(h + 1) * HG_DV]
        parts.append((oh * (gh * _sigmoid(gh))).astype(BF16))
    oa = jnp.concatenate(parts, axis=1)
    n_a = HG_HEADS * HG_DV
    mix = _dot(oa, w_ref[0:n_a, :]) + _dot(om_ref[...].astype(BF16), w_ref[n_a:, :])
    o_ref[...] = x_ref[...] + g_ref[0] * mix


def out_proj_layer0(o_f, o_b, z, hg_norm, o_mla, w_out_bf16, x, gate, tokens_per_group, tm=512):
    t, d = x.shape
    n_a = HG_HEADS * HG_DV
    tiles_per_group = tokens_per_group // tm
    tile = lambda w: pl.BlockSpec((tm, w), lambda i: (i, 0))
    return pl.pallas_call(
        _out0_kernel,
        grid=(t // tm,),
        in_specs=[tile(n_a), tile(n_a),
                  pl.BlockSpec((tm, n_a), lambda i: (i, AB_COL_GATE // n_a)),
                  pl.BlockSpec((1, HG_DV), lambda i: (0, 0)),
                  tile(o_mla.shape[1]),
                  pl.BlockSpec(w_out_bf16.shape, lambda i: (0, 0)),
                  tile(d),
                  pl.BlockSpec((1, 1, d), lambda i: (i // tiles_per_group, 0, 0))],
        out_specs=tile(d),
        out_shape=jax.ShapeDtypeStruct((t, d), F32),
        compiler_params=_cparams(("parallel",)),
        name="out_proj_layer0",
    )(o_f, o_b, z, hg_norm.reshape(1, -1), o_mla, w_out_bf16, x, gate)


def _out1_kernel(a_ref, w_ref, x_ref, g_ref, o_ref):
    o_ref[...] = x_ref[...] + g_ref[0] * _dot(a_ref[...].astype(BF16), w_ref[...])


def out_proj_layer1(a, w_out_bf16, x, gate, tokens_per_group, tm=512):
    t, d = x.shape
    tiles_per_group = tokens_per_group // tm
    tile = lambda w: pl.BlockSpec((tm, w), lambda i: (i, 0))
    return pl.pallas_call(
        _out1_kernel,
        grid=(t // tm,),
        in_specs=[tile(a.shape[1]),
                  pl.BlockSpec(w_out_bf16.shape, lambda i: (0, 0)),
                  tile(d),
                  pl.BlockSpec((1, 1, d), lambda i: (i // tiles_per_group, 0, 0))],
        out_specs=tile(d),
        out_shape=jax.ShapeDtypeStruct((t, d), F32),
        compiler_params=_cparams(("parallel",)),
        name="out_proj_layer1",
    )(a, w_out_bf16, x, gate)


def _head_rms(x, gain2):
    sq = x * x
    lane = lax.broadcasted_iota(jnp.int32, x.shape, 1)
    first = lane < SWA_HD
    lo = jnp.sum(jnp.where(first, sq, 0.0), axis=-1, keepdims=True)
    hi = jnp.sum(jnp.where(first, 0.0, sq), axis=-1, keepdims=True)
    ms = jnp.where(first, lo, hi) * (1.0 / SWA_HD)
    return x * lax.rsqrt(ms + EPS) * gain2


def _swa_prep_kernel(*refs, rope):
    if rope:
        zq_ref, zk_ref, zv_ref, qn_ref, kn_ref, c_ref, sa_ref, sb_ref, q_ref, kc_ref, k_ref, v_ref = refs
    else:
        zq_ref, zk_ref, zv_ref, qn_ref, kn_ref, q_ref, kc_ref, k_ref, v_ref = refs
    half = SWA_HD // 2

    def rot(x):
        return _rope(x, c_ref[...], sa_ref[...], sb_ref[...], half) if rope else x

    zq = zq_ref[...]
    qs = []
    for p in range(zq.shape[1] // LANES):
        x = _head_rms(zq[:, p * LANES:(p + 1) * LANES], qn_ref[...])
        qs.append((rot(x) * SWA_SCALE).astype(BF16))
    q_ref[...] = jnp.concatenate(qs, axis=1)
    zk = zk_ref[...]
    kn, kr = [], []
    for p in range(zk.shape[1] // LANES):
        x = _head_rms(zk[:, p * LANES:(p + 1) * LANES], kn_ref[...])
        kn.append(x)
        kr.append(rot(x).astype(BF16))
    kc_ref[...] = jnp.concatenate(kn, axis=1)
    k_ref[...] = jnp.concatenate(kr, axis=1)
    v_ref[...] = zv_ref[...].astype(BF16)


def swa_prep(z, q_norm2, k_norm2, tables, tokens_per_batch, tm=512):
    t = z.shape[0]
    nq = SWA_HEADS * SWA_HD
    nkv = SWA_KV_HEADS * SWA_HD
    rope = tables is not None
    row = lambda i: (0, 0)
    in_specs = [pl.BlockSpec((tm, nq), lambda i: (i, 0)),
                pl.BlockSpec((tm, nkv), lambda i: (i, nq // nkv)),
                pl.BlockSpec((tm, nkv), lambda i: (i, nq // nkv + 1)),
                pl.BlockSpec((1, LANES), row),
                pl.BlockSpec((1, LANES), row)]
    args = [z, z, z, q_norm2, k_norm2]
    if rope:
        tpb = tokens_per_batch // tm
        in_specs += [pl.BlockSpec((tm, LANES), lambda i: (i % tpb, 0))] * 3
        args += list(tables)
    tile = lambda w: pl.BlockSpec((tm, w), lambda i: (i, 0))
    return pl.pallas_call(
        functools.partial(_swa_prep_kernel, rope=rope),
        grid=(t // tm,),
        in_specs=in_specs,
        out_specs=[tile(nq), tile(nkv), tile(nkv), tile(nkv)],
        out_shape=[jax.ShapeDtypeStruct((t, nq), BF16),
                   jax.ShapeDtypeStruct((t, nkv), F32),
                   jax.ShapeDtypeStruct((t, nkv), BF16),
                   jax.ShapeDtypeStruct((t, nkv), BF16)],
        compiler_params=_cparams(("parallel",)),
        name="swa_prep",
    )(*args)


def _sink_attention_t(q8, k_all, vt_all, bias, sink_ref):
    tq = q8.shape[0]
    n_q = SWA_GROUP * tq
    lane = lax.broadcasted_iota(jnp.int32, (1, n_q), 1)
    kv = range(k_all.shape[1] // SWA_HD)
    heads = [[n * SWA_GROUP + g for g in range(SWA_GROUP)] for n in kv]
    q4s = [jnp.concatenate([q8[:, hq * SWA_HD:(hq + 1) * SWA_HD] for hq in heads[n]], axis=0) for n in kv]
    sks = []
    for n in kv:
        sk = jnp.zeros((1, n_q), F32)
        for g, hq in enumerate(heads[n]):
            sk = jnp.where(lane // tq == g, sink_ref[hq], sk)
        sks.append(sk)
    ss = [_dot_nt(k_all[:, n * SWA_HD:(n + 1) * SWA_HD], q4s[n]) for n in kv]
    if bias is not None:
        ss = [s + bias for s in ss]
    ms = [jnp.maximum(jnp.max(ss[n], axis=0, keepdims=True), sks[n]) for n in kv]
    ps = [jnp.exp(ss[n] - ms[n]) for n in kv]
    ls = [jnp.sum(ps[n], axis=0, keepdims=True) + jnp.exp(sks[n] - ms[n]) for n in kv]
    ots = [_dot(vt_all[n * SWA_HD:(n + 1) * SWA_HD, :], ps[n].astype(BF16)) / ls[n] for n in kv]
    pairs = []
    for n in kv:
        for g in range(0, SWA_GROUP, 2):
            two = jnp.concatenate([ots[n][:, g * tq:(g + 1) * tq], ots[n][:, (g + 1) * tq:(g + 2) * tq]], axis=0)
            pairs.append(two.T)
    return jnp.concatenate(pairs, axis=1)


def _swa_dense_kernel(sink_ref, q_ref, k_ref, vt_ref, o_ref):
    o_ref[0] = _sink_attention_t(q_ref[0], k_ref[0], vt_ref[0], None, sink_ref)


def swa_dense(q, k, v, sink, batch, seq):
    nq = SWA_HEADS * SWA_HD
    nkv = SWA_KV_HEADS * SWA_HD
    q3 = q.reshape(batch, seq, nq)
    k3 = k.reshape(batch, seq, -1)
    vt = jnp.swapaxes(v.reshape(batch, seq, -1), 1, 2)
    out = pl.pallas_call(
        _swa_dense_kernel,
        grid=(batch,),
        in_specs=[pl.BlockSpec(memory_space=pltpu.SMEM),
                  pl.BlockSpec((1, seq, nq), lambda b: (b, 0, 0)),
                  pl.BlockSpec((1, seq, nkv), lambda b: (b, 0, 0)),
                  pl.BlockSpec((1, nkv, seq), lambda b: (b, 0, 0))],
        out_specs=pl.BlockSpec((1, seq, nq), lambda b: (b, 0, 0)),
        out_shape=jax.ShapeDtypeStruct((batch, seq, nq), F32),
        compiler_params=_cparams(("parallel",)),
        name="swa_dense",
    )(sink, q3, k3, vt)
    return out.reshape(batch * seq, nq)


def _swa_window_kernel(sink_ref, q_ref, kc_ref, kp_ref, k0_ref, kn_ref, vc_ref, vp_ref, v0_ref, vn_ref, o_ref):
    i = pl.program_id(1)
    nb = pl.num_programs(1)
    w = SWA_WINDOW
    n_ctx = kc_ref.shape[1]
    k_all = jnp.concatenate([kc_ref[0], kp_ref[0], k0_ref[0], kn_ref[0]], axis=0)
    vt_all = jnp.concatenate([vc_ref[0], vp_ref[0], v0_ref[0], vn_ref[0]], axis=1)
    shape = (n_ctx + 3 * w, SWA_GROUP * w)
    key = lax.broadcasted_iota(jnp.int32, shape, 0) - n_ctx
    r = lax.broadcasted_iota(jnp.int32, shape, 1) % w
    prev_bias = jnp.where(i > 0, 0.0, NEG_BIG)
    next_bias = jnp.where(i < nb - 1, 0.0, NEG_BIG)
    bias = jnp.where(
        key < w,
        jnp.where(key < 0, 0.0, jnp.where(key >= r, prev_bias, NEG_BIG)),
        jnp.where(key < 2 * w, 0.0, jnp.where(key - 2 * w <= r, next_bias, NEG_BIG)))
    o_ref[0] = _sink_attention_t(q_ref[0], k_all, vt_all, bias, sink_ref)


def swa_window(q, k, v, k_ctx, v_ctx, sink, batch, seq, n_ctx):
    nq = SWA_HEADS * SWA_HD
    w = SWA_WINDOW
    nb = seq // w
    q3 = q.reshape(batch, seq, nq)
    k3 = k.reshape(batch, seq, -1)
    vt = jnp.swapaxes(v.reshape(batch, seq, -1), 1, 2)
    vt_ctx = jnp.swapaxes(v_ctx, 1, 2)
    nkv = SWA_KV_HEADS * SWA_HD
    ctx = pl.BlockSpec((1, n_ctx, nkv), lambda b, i: (b, 0, 0))
    prv = pl.BlockSpec((1, w, nkv), lambda b, i: (b, jnp.maximum(i - 1, 0), 0))
    cur = pl.BlockSpec((1, w, nkv), lambda b, i: (b, i, 0))
    nxt = pl.BlockSpec((1, w, nkv), lambda b, i: (b, jnp.minimum(i + 1, nb - 1), 0))
    ctx_t = pl.BlockSpec((1, nkv, n_ctx), lambda b, i: (b, 0, 0))
    prv_t = pl.BlockSpec((1, nkv, w), lambda b, i: (b, 0, jnp.maximum(i - 1, 0)))
    cur_t = pl.BlockSpec((1, nkv, w), lambda b, i: (b, 0, i))
    nxt_t = pl.BlockSpec((1, nkv, w), lambda b, i: (b, 0, jnp.minimum(i + 1, nb - 1)))
    out = pl.pallas_call(
        _swa_window_kernel,
        grid=(batch, nb),
        in_specs=[pl.BlockSpec(memory_space=pltpu.SMEM),
                  pl.BlockSpec((1, w, nq), lambda b, i: (b, i, 0)),
                  ctx, prv, cur, nxt, ctx_t, prv_t, cur_t, nxt_t],
        out_specs=pl.BlockSpec((1, w, nq), lambda b, i: (b, i, 0)),
        out_shape=jax.ShapeDtypeStruct((batch, seq, nq), F32),
        compiler_params=_cparams(("parallel", "arbitrary")),
        name="swa_window",
    )(sink, q3, k_ctx, k3, k3, k3, vt_ctx, vt, vt, vt)
    return out.reshape(batch * seq, nq)


def _router_kernel(x_ref, g_ref, sh_ref, sc_ref, rw_ref, rb_ref, tri_ref, h_ref, idx_ref, gate_ref, rank_ref,
                   cnt_ref, cnt, *, tiles_per_part):
    @pl.when(pl.program_id(0) % tiles_per_part == 0)
    def _():
        cnt[...] = jnp.zeros_like(cnt)

    h = _modnorm(x_ref[...], g_ref[...], sh_ref[0], sc_ref[0])
    bits = lax.bitcast_convert_type(h.astype(BF16).astype(F32), jnp.uint32)
    half = h.shape[1] // 2
    h_ref[...] = (bits[:, :half] >> 16) | (bits[:, half:] & jnp.uint32(0xFFFF0000))
    wh, wl = _split2(rw_ref[...])
    hh, hl = _split2(h)
    logits = _dot_nt(wh, hh) + _dot_nt(wh, hl) + _dot_nt(wl, hh) + rb_ref[...]
    row = lax.broadcasted_iota(jnp.int32, logits.shape, 0)
    work = logits
    vals, idxs = [], []
    for _ in range(TOP_K):
        m = jnp.max(work, axis=0, keepdims=True)
        ik = jnp.min(jnp.where(work == m, row, LANES), axis=0, keepdims=True)
        vals.append(m)
        idxs.append(ik)
        work = jnp.where(row == ik, 2.0 * NEG_BIG, work)
    es = [jnp.exp(v - vals[0]) for v in vals]
    denom = es[0] + es[1] + es[2] + es[3]
    chosen = jnp.zeros(logits.shape, F32)
    for k in range(TOP_K):
        chosen = jnp.where(row == idxs[k], 1.0, chosen)
    chosen_b = chosen.astype(BF16)
    seen = cnt[...]
    before = _dot(chosen_b, tri_ref[...]) + jnp.concatenate([seen] * (logits.shape[1] // LANES), axis=1)
    out_row = lax.broadcasted_iota(jnp.int32, idx_ref.shape, 0)
    idx_out = jnp.zeros(idx_ref.shape, jnp.int32)
    gate_out = jnp.zeros(idx_ref.shape, F32)
    rank_out = jnp.zeros(idx_ref.shape, jnp.int32)
    for k in range(TOP_K):
        rk = jnp.sum(jnp.where(row == idxs[k], before, 0.0), axis=0, keepdims=True)
        idx_out = jnp.where(out_row == k, idxs[k], idx_out)
        gate_out = jnp.where(out_row == k, es[k] / denom, gate_out)
        rank_out = jnp.where(out_row == k, rk.astype(jnp.int32), rank_out)
    idx_ref[...] = idx_out
    gate_ref[...] = gate_out
    rank_ref[...] = rank_out
    cnt[...] = seen + _dot(chosen_b, jnp.ones((logits.shape[1], LANES), BF16))
    cnt_ref[0] = cnt[...]


def moe_router(x, gain, shift, scale, rw_t, rb_col, tokens_per_group, tm=512):
    t, d = x.shape
    tiles_per_group = tokens_per_group // tm
    tiles_per_part = MOE_PART // tm
    grp = lambda i: (i // tiles_per_group, 0, 0)
    tile = lambda w: pl.BlockSpec((tm, w), lambda i: (i, 0))
    slab = pl.BlockSpec((8, tm), lambda i: (0, i))
    tri = jnp.asarray(np.triu(np.ones((tm, tm), np.float32), 1), BF16)
    hp, idx, gates, rank, cnt = pl.pallas_call(
        functools.partial(_router_kernel, tiles_per_part=tiles_per_part),
        grid=(t // tm,),
        in_specs=[tile(d),
                  pl.BlockSpec((1, d), lambda i: (0, 0)),
                  pl.BlockSpec((1, 1, d), grp),
                  pl.BlockSpec((1, 1, d), grp),
                  pl.BlockSpec((LANES, d), lambda i: (0, 0)),
                  pl.BlockSpec((LANES, 1), lambda i: (0, 0)),
                  pl.BlockSpec((tm, tm), lambda i: (0, 0))],
        out_specs=[tile(d // 2), slab, slab, slab,
                   pl.BlockSpec((1, LANES, LANES), lambda i: (i // tiles_per_part, 0, 0))],
        out_shape=[jax.ShapeDtypeStruct((t, d // 2), jnp.uint32),
                   jax.ShapeDtypeStruct((8, t), jnp.int32),
                   jax.ShapeDtypeStruct((8, t), F32),
                   jax.ShapeDtypeStruct((8, t), jnp.int32),
                   jax.ShapeDtypeStruct((t // MOE_PART, LANES, LANES), F32)],
        scratch_shapes=[pltpu.VMEM((LANES, LANES), F32)],
        compiler_params=_cparams(("arbitrary",)),
        name="moe_router",
    )(x, gain.reshape(1, d), shift, scale, rw_t, rb_col, tri)
    per_token = lambda a: a[:TOP_K].T
    return hp, per_token(idx), per_token(gates), per_token(rank), cnt[:, :N_EXPERTS, 0].astype(jnp.int32)


FFN_COLS = 512
FFN_PIECES = 2 * D_FF // FFN_COLS + D_MODEL // FFN_COLS
MOE_PART = 4096
ADD_BATCH = 8


def _unpack_rows(words):
    lo = lax.bitcast_convert_type(words << 16, F32)
    hi = lax.bitcast_convert_type(words & jnp.uint32(0xFFFF0000), F32)
    return jnp.concatenate([lo, hi], axis=1).astype(BF16)


def _ffn_block(x_ref, y_ref, wgu_ref, bgu_ref, wd_ref, bd_ref, between):
    xb = _unpack_rows(x_ref[...])
    acts = []
    for c in range(D_FF // FFN_COLS):
        lo, hi = c * FFN_COLS, (c + 1) * FFN_COLS
        between(2 * c)
        gate = _dot(xb, wgu_ref[0, :, lo:hi]) + bgu_ref[0][:, lo:hi]
        between(2 * c + 1)
        up = _dot(xb, wgu_ref[0, :, D_FF + lo:D_FF + hi]) + bgu_ref[0][:, D_FF + lo:D_FF + hi]
        gate = jnp.minimum(gate, SWIGLU_LIMIT)
        up = jnp.clip(up, -SWIGLU_LIMIT, SWIGLU_LIMIT)
        acts.append((gate * _sigmoid(SWIGLU_ALPHA * gate) * (up + 1.0)).astype(BF16))
    act = jnp.concatenate(acts, axis=1)
    for n in range(D_MODEL // FFN_COLS):
        lo, hi = n * FFN_COLS, (n + 1) * FFN_COLS
        between(2 * D_FF // FFN_COLS + n)
        y_ref[:, lo:hi] = _dot(act, wd_ref[0, :, lo:hi]) + bd_ref[0][:, lo:hi]


def _expert_kernel(nb_ref, bs_ref, loc_ref, gate_ref, hp_hbm, wgu_ref, bgu_ref, wd_ref, bd_ref, out_hbm,
                   hbuf, acc, xbuf, ybuf, sem):
    part = pl.program_id(0)
    e = pl.program_id(1)
    idx = part * N_EXPERTS + e
    nb = nb_ref[idx]
    b0 = bs_ref[idx]
    shares = np.array_split(np.arange(MOE_ROWS), FFN_PIECES)

    def fetch_rows(blk):
        base = blk * MOE_ROWS
        dst = xbuf.at[blk % 2]

        def emit(k):
            for r in shares[k]:
                r = int(r)
                dst[r:r + 1, :] = hbuf[pl.ds(loc_ref[base + r], 1), :]
        return emit

    def add_rows(blk):
        base = blk * MOE_ROWS
        src = ybuf.at[blk % 2]

        def emit(k):
            for batch in np.array_split(shares[k], max(1, len(shares[k]) // ADD_BATCH)):
                rows = [loc_ref[base + int(r)] for r in batch]
                new = [acc[pl.ds(row, 1), :] + gate_ref[base + int(r)] * src[int(r):int(r) + 1, :]
                       for row, r in zip(rows, batch)]
                for row, val in zip(rows, new):
                    acc[pl.ds(row, 1), :] = val
        return emit

    def emit_all(f):
        for k in range(FFN_PIECES):
            f(k)

    @pl.when(e == 0)
    def _():
        load = pltpu.make_async_copy(hp_hbm.at[pl.ds(part * MOE_PART, MOE_PART), :],
                                     hbuf.at[pl.ds(0, MOE_PART), :], sem.at[0])
        load.start()
        hbuf[MOE_PART:, :] = jnp.zeros((hbuf.shape[0] - MOE_PART, hbuf.shape[1]), hbuf.dtype)
        acc[...] = jnp.zeros_like(acc)
        ybuf[...] = jnp.zeros_like(ybuf)
        load.wait()
        emit_all(fetch_rows(b0))

    def block(i, carry):
        fetch, add = fetch_rows(i + 1), add_rows(i - 1)

        def between(k):
            fetch(k)
            add(k)
        _ffn_block(xbuf.at[i % 2], ybuf.at[i % 2], wgu_ref, bgu_ref, wd_ref, bd_ref, between)
        return carry

    lax.fori_loop(b0, b0 + nb, block, 0)

    @pl.when(e == N_EXPERTS - 1)
    def _():
        emit_all(add_rows(b0 + nb - 1))
        store = pltpu.make_async_copy(acc.at[pl.ds(0, MOE_PART), :],
                                      out_hbm.at[pl.ds(part * MOE_PART, MOE_PART), :], sem.at[0])
        store.start()
        store.wait()


def moe_experts(hp, n_blk, blk_start, row_loc, row_gate, w_gu, b_gu, w_down, b_down):
    t = hp.shape[0]
    d = D_MODEL
    per_expert = lambda p, e, nb, bs, sl: (e, 0, 0)
    grid_spec = pltpu.PrefetchScalarGridSpec(
        num_scalar_prefetch=3,
        grid=(t // MOE_PART, N_EXPERTS),
        in_specs=[pl.BlockSpec(memory_space=pltpu.SMEM),
                  pl.BlockSpec(memory_space=pl.ANY),
                  pl.BlockSpec((1, d, 2 * D_FF), per_expert),
                  pl.BlockSpec((1, 1, 2 * D_FF), per_expert),
                  pl.BlockSpec((1, D_FF, d), per_expert),
                  pl.BlockSpec((1, 1, d), per_expert)],
        out_specs=pl.BlockSpec(memory_space=pl.ANY),
        scratch_shapes=[pltpu.VMEM((MOE_PART + 8, d // 2), jnp.uint32),
                        pltpu.VMEM((MOE_PART + 8, d), F32),
                        pltpu.VMEM((2, MOE_ROWS, d // 2), jnp.uint32),
                        pltpu.VMEM((2, MOE_ROWS, d), F32),
                        pltpu.SemaphoreType.DMA((1,))],
    )
    return pl.pallas_call(
        _expert_kernel,
        grid_spec=grid_spec,
        out_shape=jax.ShapeDtypeStruct((t, d), F32),
        compiler_params=pltpu.CompilerParams(dimension_semantics=("arbitrary", "arbitrary"),
                                             vmem_limit_bytes=EXPERT_VMEM_LIMIT),
        name="moe_experts",
    )(n_blk, blk_start, row_loc, row_gate, hp, w_gu, b_gu.reshape(N_EXPERTS, 1, -1), w_down,
      b_down.reshape(N_EXPERTS, 1, -1))


def _residual_kernel(y_ref, x_ref, g_ref, o_ref):
    o_ref[...] = x_ref[...] + g_ref[0] * y_ref[...]


def gated_residual(y, x, gate_mod, tokens_per_group, tm=512):
    t, d = x.shape
    tiles_per_group = tokens_per_group // tm
    tile = pl.BlockSpec((tm, d), lambda i: (i, 0))
    return pl.pallas_call(
        _residual_kernel,
        grid=(t // tm,),
        in_specs=[tile, tile, pl.BlockSpec((1, 1, d), lambda i: (i // tiles_per_group, 0, 0))],
        out_specs=tile,
        out_shape=jax.ShapeDtypeStruct((t, d), F32),
        compiler_params=_cparams(("parallel",)),
        name="moe_residual",
    )(y, x, gate_mod)


def _assignment_tables(top_idx, rank, gates, counts):
    t = top_idx.shape[0]
    n_parts = t // MOE_PART
    n_assign = t * TOP_K
    part_blocks = MOE_PART * TOP_K // MOE_ROWS + N_EXPERTS + 1
    n_rows = (1 + n_parts * part_blocks) * MOE_ROWS
    padded = (counts + MOE_ROWS - 1) // MOE_ROWS * MOE_ROWS
    part_first = (1 + jnp.arange(n_parts, dtype=jnp.int32) * part_blocks) * MOE_ROWS
    row_start = part_first[:, None] + jnp.cumsum(padded, axis=1) - padded
    experts = jnp.arange(N_EXPERTS, dtype=jnp.int32)
    by_part = top_idx.reshape(n_parts, MOE_PART, TOP_K)
    start_of = jnp.sum(jnp.where(by_part[..., None] == experts, row_start[:, None, None, :], 0), axis=-1)
    dest = (start_of.reshape(t, TOP_K) + rank).reshape(-1).astype(jnp.int32)
    local = (jnp.arange(n_assign, dtype=jnp.int32) // TOP_K) % MOE_PART
    vals = jnp.stack([local, lax.bitcast_convert_type(gates.reshape(-1), jnp.int32)], axis=1)
    init = jnp.broadcast_to(jnp.array([MOE_PART, 0], jnp.int32), (n_rows, 2))
    table = init.at[dest].set(vals)
    row_loc = table[:, 0]
    row_gate = lax.bitcast_convert_type(table[:, 1], F32)
    return row_loc, row_gate, (padded // MOE_ROWS).reshape(-1).astype(jnp.int32), \
        (row_start // MOE_ROWS).reshape(-1).astype(jnp.int32)


def moe_layer(x, p, shift, scale, gate_mod, tokens_per_group):
    hp, idx, gates, rank, counts = moe_router(x, p['norm2'], shift, scale, p['rw_t'], p['rb_col'], tokens_per_group)
    row_loc, row_gate, n_blk, blk_start = _assignment_tables(idx, rank, gates, counts)
    y = moe_experts(hp, n_blk, blk_start, row_loc, row_gate, p['w_gu_bf'], p['b_gu'], p['w_down_bf'], p['b_down'])
    return gated_residual(y, x, gate_mod, tokens_per_group)


def _pad_lanes(a, width):
    return jnp.pad(a, [(0, 0)] * (a.ndim - 1) + [(0, width - a.shape[-1])])


def _prep_common(p):
    p['rw_t'] = _pad_lanes(p['router_w'], LANES).T
    p['rb_col'] = jnp.concatenate(
        [p['router_b'].astype(F32), jnp.full((LANES - N_EXPERTS,), NEG_BIG, F32)]).reshape(LANES, 1)
    p['w_out'] = p['w_out'].astype(BF16)
    p['w_gu_bf'] = p['w_gu'].astype(BF16)
    p['w_down_bf'] = p['w_down'].astype(BF16)
    return p


def _prep_layer0(p):
    p = _prep_common(dict(p))
    p['w_in'] = _pad_lanes(p['w_in'], AB_IN_PAD).astype(BF16)
    q_b = p['q_b'].reshape(MLA_Q_LORA, MLA_HEADS, MLA_QK)
    p['q_b_pad'] = _pad_lanes(q_b, LANES).reshape(MLA_Q_LORA, MLA_HEADS * LANES).astype(BF16)
    kv_b = p['kv_b'].reshape(MLA_KV_LORA, MLA_HEADS, MLA_NOPE + MLA_V)
    p['wk_pad'] = _pad_lanes(kv_b[:, :, :MLA_NOPE], LANES).reshape(MLA_KV_LORA, MLA_HEADS * LANES).astype(BF16)
    p['wv'] = kv_b[:, :, MLA_NOPE:].reshape(MLA_KV_LORA, MLA_HEADS * MLA_V).astype(BF16)
    p['q_norm_pad'] = _pad_lanes(p['q_norm'].reshape(1, -1), LANES)
    p['k_norm_pad'] = _pad_lanes(p['k_norm'].reshape(1, -1), LANES)
    return p


def _prep_layer1(p):
    p = _prep_common(dict(p))
    p['w_in'] = p['w_in'].astype(BF16)
    p['q_norm2'] = jnp.tile(p['q_norm'].reshape(1, -1), (1, LANES // SWA_HD))
    p['k_norm2'] = jnp.tile(p['k_norm'].reshape(1, -1), (1, LANES // SWA_HD))
    return p


def _group_forward(x3, mods, p0, p1, lb_logits, caches, latent):
    batch, seq, d = x3.shape
    t = batch * seq
    x = x3.reshape(t, d)
    tpg = seq if latent else t
    sh1, sc1, g1, sh2, sc2, g2 = mods[0]

    z = modnorm_matmul(x, p0['norm1'], sh1, sc1, p0['w_in'], tpg)
    tab_b = _rope_lane_tables(seq, MLA_ROPE, LANES, MLA_NOPE) if latent else None
    s0f, s0b = (caches['hg_f'], caches['hg_b']) if latent else (None, None)
    o_f, o_b, s_f, s_b = hgrn_bidir(z, lb_logits, s0f, s0b, batch, seq)
    q = mla_queries(z, p0['qa_norm'], p0['q_b_pad'], p0['q_norm_pad'], tab_b, seq)
    ckv, k, v = mla_keys_values(z, AB_COL_KVA // LANES, z, AB_COL_KPE // LANES, p0['kva_norm'], p0['wk_pad'], p0['wv'],
                                p0['k_norm_pad'], tab_b, seq, norm_input=True)
    n_k = seq
    if latent:
        n_ctx = caches['ckv'].shape[1]
        ckv_c = caches['ckv'].reshape(batch * n_ctx, MLA_KV_LORA)
        kpe_c = _pad_lanes(caches['kpe'].reshape(batch * n_ctx, MLA_ROPE), LANES)
        _, k_c, v_c = mla_keys_values(ckv_c, 0, kpe_c, 0, p0['kva_norm'], p0['wk_pad'], p0['wv'],
                                      p0['k_norm_pad'], None, n_ctx, norm_input=False)
        cat = lambda a, b: jnp.concatenate([a.reshape(batch, n_ctx, -1), b.reshape(batch, seq, -1)],
                                           axis=1).reshape(batch * (n_ctx + seq), -1)
        k, v = cat(k_c, k), cat(v_c, v)
        n_k = n_ctx + seq
    o_mla = mla_attention(q, k, v, batch, seq, n_k)
    x = out_proj_layer0(o_f, o_b, z, p0['hg_out_norm'], o_mla, p0['w_out'], x, g1, tpg)
    x = moe_layer(x, p0, sh2, sc2, g2, tpg)
    state0 = (s_f, s_b, ckv.reshape(batch, seq, MLA_KV_LORA), z[:, AB_COL_KPE:AB_COL_KPE + MLA_ROPE].reshape(batch, seq, MLA_ROPE))

    sh1, sc1, g1, sh2, sc2, g2 = mods[1]
    z = modnorm_matmul(x, p1['norm1'], sh1, sc1, p1['w_in'], tpg)
    tab_c = _rope_lane_tables(seq, SWA_HD, SWA_HD, 0) if latent else None
    q, k_cache, k, v = swa_prep(z, p1['q_norm2'], p1['k_norm2'], tab_c, seq)
    sink = p1['sink'].astype(F32)
    if latent:
        n_ctx = caches['k1'].shape[1]
        k_c = caches['k1'].reshape(batch, n_ctx, -1).astype(BF16)
        v_c = caches['v1'].reshape(batch, n_ctx, -1).astype(BF16)
        a = swa_window(q, k, v, k_c, v_c, sink, batch, seq, n_ctx)
    else:
        a = swa_dense(q, k, v, sink, batch, seq)
    x = out_proj_layer1(a, p1['w_out'], x, g1, tpg)
    x = moe_layer(x, p1, sh2, sc2, g2, tpg)
    nkv = SWA_KV_HEADS * SWA_HD
    state1 = (k_cache.reshape(batch, seq, SWA_KV_HEADS, SWA_HD),
              z[:, SWA_HEADS * SWA_HD + nkv:].reshape(batch, seq, SWA_KV_HEADS, SWA_HD))
    return x.reshape(batch, seq, d), state0, state1


def kernel(x_prompt, x_sample, state_l0_hgrn_fwd, state_l0_hgrn_bwd, cache_l0_mla_ckv, cache_l0_mla_kpe, cache_l1_k, cache_l1_v, c, c_ctx, hgrn_lb_logits, l0_ada_w, l0_ada_b, l0_norm1, l0_norm2, l0_w_in, l0_hg_out_norm, l0_qa_norm, l0_q_b, l0_kva_norm, l0_kv_b, l0_q_norm, l0_k_norm, l0_w_out, l0_router_w, l0_router_b, l0_w_gu, l0_b_gu, l0_w_down, l0_b_down, l1_ada_w, l1_ada_b, l1_norm1, l1_norm2, l1_w_in, l1_q_norm, l1_k_norm, l1_sink, l1_w_out, l1_router_w, l1_router_b, l1_w_gu, l1_b_gu, l1_w_down, l1_b_down):
    p0 = _prep_layer0(dict(norm1=l0_norm1, norm2=l0_norm2, w_in=l0_w_in, hg_out_norm=l0_hg_out_norm,
                           qa_norm=l0_qa_norm, q_b=l0_q_b, kva_norm=l0_kva_norm, kv_b=l0_kv_b,
                           q_norm=l0_q_norm, k_norm=l0_k_norm, w_out=l0_w_out, router_w=l0_router_w,
                           router_b=l0_router_b, w_gu=l0_w_gu, b_gu=l0_b_gu, w_down=l0_w_down,
                           b_down=l0_b_down))
    p1 = _prep_layer1(dict(norm1=l1_norm1, norm2=l1_norm2, w_in=l1_w_in, q_norm=l1_q_norm, k_norm=l1_k_norm,
                           sink=l1_sink, w_out=l1_w_out, router_w=l1_router_w, router_b=l1_router_b,
                           w_gu=l1_w_gu, b_gu=l1_b_gu, w_down=l1_w_down, b_down=l1_b_down))
    dec_batch = c.shape[0]
    d = c.shape[1]
    cond8 = jnp.concatenate([c_ctx[None, :], c, jnp.zeros((8 - 1 - dec_batch, d), F32)], axis=0)
    mods_ctx, mods_lat = [], []
    for w, b in ((l0_ada_w, l0_ada_b), (l1_ada_w, l1_ada_b)):
        mod = ada_params(cond8, w, b)
        mods_ctx.append([m.reshape(1, 1, d) for m in jnp.split(mod[0:1], 6, axis=-1)])
        mods_lat.append([m.reshape(dec_batch, 1, d) for m in jnp.split(mod[1:1 + dec_batch], 6, axis=-1)])

    y_prompt, st0, st1 = _group_forward(x_prompt, mods_ctx, p0, p1, hgrn_lb_logits, None, latent=False)
    caches = dict(hg_f=state_l0_hgrn_fwd, hg_b=state_l0_hgrn_bwd, ckv=cache_l0_mla_ckv, kpe=cache_l0_mla_kpe,
                  k1=cache_l1_k, v1=cache_l1_v)
    y_sample, _, _ = _group_forward(x_sample, mods_lat, p0, p1, hgrn_lb_logits, caches, latent=True)
    return (y_prompt, y_sample, st0[0], st0[1], st0[2], st0[3], st1[0], st1[1])
```

```python
import functools

import numpy as np
import jax
import jax.numpy as jnp
from jax import lax
from jax.experimental import pallas as pl
from jax.experimental.pallas import tpu as pltpu

F32 = jnp.float32
BF16 = jnp.bfloat16

D_MODEL = 1024
GRID_W = 64
ROPE_THETA = 10000.0
EPS = 1e-6
HG_HEADS = 4
HG_DK = 128
HG_DV = 128
MLA_HEADS = 8
MLA_NOPE = 64
MLA_ROPE = 32
MLA_V = 64
MLA_QK = MLA_NOPE + MLA_ROPE
MLA_Q_LORA = 256
MLA_KV_LORA = 128
MLA_SCALE = MLA_QK ** -0.5
SWA_HEADS = 16
SWA_KV_HEADS = 4
SWA_HD = 64
SWA_WINDOW = 128
SWA_SCALE = SWA_HD ** -0.5
SWA_GROUP = SWA_HEADS // SWA_KV_HEADS
N_EXPERTS = 32
TOP_K = 4
D_FF = 1024
SWIGLU_LIMIT = 7.0
SWIGLU_ALPHA = 1.702

LANES = 128
HG_CHUNK = 128
HG_LEVELS = 7
HG_BATCH = 2
HG_MXU_LEVELS = 0
AB_IN_PAD = 3072
AB_COL_GATE = 3 * HG_HEADS * HG_DK + HG_HEADS * HG_DV
AB_COL_QA = AB_COL_GATE + HG_HEADS * HG_DV
AB_COL_KVA = AB_COL_QA + MLA_Q_LORA
AB_COL_KPE = AB_COL_KVA + MLA_KV_LORA
MOE_ROWS = 256
NEG_BIG = -1e30
VMEM_LIMIT = 48 * 1024 * 1024
EXPERT_VMEM_LIMIT = 56 * 1024 * 1024


def _cparams(sem):
    return pltpu.CompilerParams(dimension_semantics=sem, vmem_limit_bytes=VMEM_LIMIT)


def _dot(a, b):
    return jnp.dot(a, b, preferred_element_type=F32)


def _dot_nt(a, b):
    return lax.dot_general(a, b, (((1,), (1,)), ((), ())), preferred_element_type=F32)


def _dot_tn(a, b):
    return lax.dot_general(a, b, (((0,), (0,)), ((), ())), preferred_element_type=F32)


def _split2(x):
    hi = x.astype(BF16)
    lo = (x - hi.astype(F32)).astype(BF16)
    return hi, lo


def _dot_hp(a, b):
    ah, al = _split2(a)
    bh, bl = _split2(b)
    return _dot(ah, bh) + _dot(ah, bl) + _dot(al, bh)


def _sigmoid(x):
    return 1.0 / (1.0 + jnp.exp(-x))


def _modnorm(x, gain, shift, scale):
    y = x * lax.rsqrt(jnp.mean(x * x, axis=-1, keepdims=True) + EPS)
    return y * gain * (1.0 + scale) + shift


def _ada_kernel(c_ref, w_ref, b_ref, o_ref):
    c = c_ref[...]
    o_ref[...] = _dot_hp(c * _sigmoid(c), w_ref[...]) + b_ref[...]


def ada_params(cond8, w, b):
    n = w.shape[1]
    tn = 1024
    return pl.pallas_call(
        _ada_kernel,
        grid=(n // tn,),
        in_specs=[pl.BlockSpec((8, D_MODEL), lambda j: (0, 0)),
                  pl.BlockSpec((D_MODEL, tn), lambda j: (0, j)),
                  pl.BlockSpec((1, tn), lambda j: (0, j))],
        out_specs=pl.BlockSpec((8, tn), lambda j: (0, j)),
        out_shape=jax.ShapeDtypeStruct((8, n), F32),
        compiler_params=_cparams(("parallel",)),
        name="ada_params",
    )(cond8, w, b.reshape(1, n))


def _modnorm_matmul_kernel(x_ref, g_ref, sh_ref, sc_ref, w_ref, o_ref):
    h = _modnorm(x_ref[...], g_ref[...], sh_ref[0], sc_ref[0])
    o_ref[...] = _dot(h.astype(BF16), w_ref[...])


def modnorm_matmul(x, gain, shift, scale, w_bf16, tokens_per_group, tm=512):
    t, d = x.shape
    n = w_bf16.shape[1]
    tiles_per_group = tokens_per_group // tm
    grp = lambda i: (i // tiles_per_group, 0, 0)
    return pl.pallas_call(
        _modnorm_matmul_kernel,
        grid=(t // tm,),
        in_specs=[pl.BlockSpec((tm, d), lambda i: (i, 0)),
                  pl.BlockSpec((1, d), lambda i: (0, 0)),
                  pl.BlockSpec((1, 1, d), grp),
                  pl.BlockSpec((1, 1, d), grp),
                  pl.BlockSpec((d, n), lambda i: (0, 0))],
        out_specs=pl.BlockSpec((tm, n), lambda i: (i, 0)),
        out_shape=jax.ShapeDtypeStruct((t, n), F32),
        compiler_params=_cparams(("parallel",)),
        name="modnorm_matmul",
    )(x, gain.reshape(1, d), shift, scale, w_bf16)


def _hgrn_constants():
    c = HG_CHUNK
    t = np.arange(c)[:, None]
    u = np.arange(c)[None, :]
    mats = [(u <= t), (u > t)]
    for l in range(HG_MXU_LEVELS):
        m = 1 << l
        r = (t // (2 * m)) * (2 * m) + m - 1
        mats.append((u > np.minimum(t, r)) & (u <= np.maximum(t, r)))
    fwd = np.concatenate(mats, axis=0).astype(np.float32)
    bwd = np.concatenate([mm[::-1, ::-1] for mm in mats], axis=0).astype(np.float32)
    x = np.bitwise_xor(t, u)
    lvl = np.where(x > 0, np.floor(np.log2(np.maximum(x, 1))), HG_LEVELS).astype(np.int32)
    lv_f = np.where(t >= u, lvl, -1).astype(np.int32)
    return fwd, bwd, lv_f, lv_f.T.copy()


def _hgrn_chunk(qs, fpres, vs, lbs, sts, forwards, mcats, lvs):
    c = HG_CHUNK
    n = len(qs)
    fs = [lb + (1.0 - lb) * _sigmoid(fp) for lb, fp in zip(lbs, fpres)]
    kks = [1.0 - f for f in fs]
    logs = [_split2(jnp.log(f)) for f in fs]
    xs = [None] * n
    for fwd in (True, False):
        ids = [i for i in range(n) if forwards[i] == fwd]
        parts = [half for i in ids for half in logs[i]]
        x_all = _dot(mcats[0 if fwd else 1], jnp.concatenate(parts, axis=1))
        for j, i in enumerate(ids):
            xs[i] = x_all[:, 2 * j * c:(2 * j + 1) * c] + x_all[:, (2 * j + 1) * c:(2 * j + 2) * c]
    gs = [x[0:c] for x in xs]
    lv_of = [lvs[0 if f else 1] for f in forwards]
    qbs = [q.astype(BF16) for q in qs]
    kbs = [kk.astype(BF16) for kk in kks]
    vbs = [v.astype(BF16) for v in vs]
    os_ = [_dot_nt((q * jnp.exp(g)).astype(BF16), st.astype(BF16)) for q, g, st in zip(qs, gs, sts)]
    accs = [jnp.where(lv == HG_LEVELS, _dot_nt(qb, kb), 0.0) for lv, qb, kb in zip(lv_of, qbs, kbs)]
    for l in range(HG_LEVELS):
        for i in range(n):
            if l < HG_MXU_LEVELS:
                x = xs[i][(2 + l) * c:(3 + l) * c]
            else:
                m = 1 << l
                ref_rows = [j * 2 * m + (m - 1 if forwards[i] else m) for j in range(c // (2 * m))]
                g_ref = jnp.concatenate(
                    [jnp.broadcast_to(gs[i][r:r + 1, :], (2 * m, c)) for r in ref_rows], axis=0)
                x = -jnp.abs(gs[i] - g_ref)
            e = jnp.exp(x)
            p = _dot_nt((qs[i] * e).astype(BF16), (kks[i] * e).astype(BF16))
            accs[i] = jnp.where(lv_of[i] == l, p, accs[i])
    outs, new_sts = [], []
    for i in range(n):
        edge_row = c - 1 if forwards[i] else 0
        outs.append(os_[i] + _dot(accs[i].astype(BF16), vbs[i]))
        k_end = (kks[i] * jnp.exp(xs[i][c:2 * c])).astype(BF16)
        new_sts.append(sts[i] * jnp.exp(gs[i][edge_row:edge_row + 1, :]) + _dot_tn(vbs[i], k_end))
    return outs, new_sts


def _hgrn_kernel(*refs, has_init):
    if has_init:
        (qf_ref, qb_ref, ff_ref, fb_ref, vf_ref, vb_ref, lbl_ref, mf_ref, mb_ref, lvf_ref, lvb_ref,
         s0f_ref, s0b_ref, of_ref, ob_ref, sf_ref, sb_ref, stf, stb) = refs
    else:
        (qf_ref, qb_ref, ff_ref, fb_ref, vf_ref, vb_ref, lbl_ref, mf_ref, mb_ref, lvf_ref, lvb_ref,
         of_ref, ob_ref, sf_ref, sb_ref, stf, stb) = refs
    c = pl.program_id(1)
    nc = pl.num_programs(1)
    nb = qf_ref.shape[0]
    nh = HG_HEADS
    chains = [(bb, h) for bb in range(nb) for h in range(nh)]

    @pl.when(c == 0)
    def _():
        for i, (bb, h) in enumerate(chains):
            if has_init:
                stf[i] = s0f_ref[bb, h].T
                stb[i] = s0b_ref[bb, h].T
            else:
                stf[i] = jnp.zeros((HG_DV, HG_DK), F32)
                stb[i] = jnp.zeros((HG_DV, HG_DK), F32)

    rows = [lbl_ref[:, j, :] for j in range(lbl_ref.shape[1])]
    mx = functools.reduce(jnp.maximum, rows)
    ex = [jnp.exp(r - mx) for r in rows]
    lb = ex[0] / functools.reduce(lambda a, b: a + b, ex)

    lanes = lambda h: slice(h * LANES, (h + 1) * LANES)
    n = len(chains)
    outs, sts = _hgrn_chunk(
        [qf_ref[bb, :, lanes(h)] for bb, h in chains] + [qb_ref[bb, :, lanes(h)] for bb, h in chains],
        [ff_ref[bb, :, lanes(h)] for bb, h in chains] + [fb_ref[bb, :, lanes(h)] for bb, h in chains],
        [vf_ref[bb, :, lanes(h)] for bb, h in chains] + [vb_ref[bb, :, lanes(h)] for bb, h in chains],
        [lb[0:1, lanes(h)] for _, h in chains] + [lb[1:2, lanes(h)] for _, h in chains],
        [stf[i] for i in range(n)] + [stb[i] for i in range(n)],
        [True] * n + [False] * n,
        (mf_ref[...], mb_ref[...]), (lvf_ref[...], lvb_ref[...]))
    for i, (bb, h) in enumerate(chains):
        of_ref[bb, :, lanes(h)] = outs[i]
        ob_ref[bb, :, lanes(h)] = outs[n + i]
        stf[i] = sts[i]
        stb[i] = sts[n + i]

    @pl.when(c == nc - 1)
    def _():
        for i, (bb, h) in enumerate(chains):
            sf_ref[bb, h] = stf[i].T
            sb_ref[bb, h] = stb[i].T


def hgrn_bidir(z, lb_logits, s0f, s0b, batch, seq):
    nc = seq // HG_CHUNK
    z3 = z.reshape(batch, seq, z.shape[1])
    mf, mb, lvf, lvb = _hgrn_constants()
    has_init = s0f is not None
    width = HG_HEADS * LANES
    bt = HG_BATCH
    blk = (bt, HG_CHUNK, width)
    fwd = lambda off: pl.BlockSpec(blk, lambda b, c: (b, c, off))
    bwd = lambda off: pl.BlockSpec(blk, lambda b, c: (b, nc - 1 - c, off))
    full = lambda a: pl.BlockSpec(a.shape, lambda b, c: (0,) * a.ndim)
    st_spec = pl.BlockSpec((bt, HG_HEADS, HG_DK, HG_DV), lambda b, c: (b, 0, 0, 0))
    consts = [jnp.asarray(mf, BF16), jnp.asarray(mb, BF16), jnp.asarray(lvf), jnp.asarray(lvb)]
    in_specs = [fwd(0), bwd(0), fwd(1), bwd(2), fwd(3), bwd(3),
                pl.BlockSpec(lb_logits.shape, lambda b, c: (0, 0, 0))]
    in_specs += [full(a) for a in consts]
    args = [z3] * 6 + [lb_logits] + consts
    if has_init:
        in_specs += [st_spec, st_spec]
        args += [s0f, s0b]
    o_shape = jax.ShapeDtypeStruct((batch, seq, width), F32)
    s_shape = jax.ShapeDtypeStruct((batch, HG_HEADS, HG_DK, HG_DV), F32)
    o_f, o_b, s_f, s_b = pl.pallas_call(
        functools.partial(_hgrn_kernel, has_init=has_init),
        grid=(batch // bt, nc),
        in_specs=in_specs,
        out_specs=[pl.BlockSpec(blk, lambda b, c: (b, c, 0)),
                   pl.BlockSpec(blk, lambda b, c: (b, nc - 1 - c, 0)),
                   st_spec, st_spec],
        out_shape=[o_shape, o_shape, s_shape, s_shape],
        scratch_shapes=[pltpu.VMEM((bt * HG_HEADS, HG_DV, HG_DK), F32)] * 2,
        compiler_params=_cparams(("parallel", "arbitrary")),
        name="hgrn_bidir",
    )(*args)
    t = batch * seq
    return o_f.reshape(t, -1), o_b.reshape(t, -1), s_f, s_b


def _axial_tables(n_tokens, n_rot):
    t = jnp.arange(n_tokens)
    row = (t // GRID_W).astype(F32)
    col = (t % GRID_W).astype(F32)
    n_freq = n_rot // 4
    inv = jnp.power(ROPE_THETA, -jnp.arange(n_freq, dtype=F32) / n_freq)
    ang = jnp.concatenate([row[:, None] * inv, col[:, None] * inv], axis=-1)
    return jnp.cos(ang), jnp.sin(ang)


def _rope_lane_tables(n_tokens, n_rot, head_width, first_rot_lane):
    cos, sin = _axial_tables(n_tokens, n_rot)
    half = n_rot // 2
    c_head = jnp.ones((n_tokens, head_width), F32)
    sa_head = jnp.zeros((n_tokens, head_width), F32)
    sb_head = jnp.zeros((n_tokens, head_width), F32)
    a0, a1, a2 = first_rot_lane, first_rot_lane + half, first_rot_lane + n_rot
    c_head = c_head.at[:, a0:a1].set(cos).at[:, a1:a2].set(cos)
    sa_head = sa_head.at[:, a0:a1].set(-sin)
    sb_head = sb_head.at[:, a1:a2].set(sin)
    reps = LANES // head_width
    tile = lambda a: jnp.tile(a, (1, reps))
    return tile(c_head), tile(sa_head), tile(sb_head)


def _rope(x, c, sa, sb, half):
    return x * c + pltpu.roll(x, LANES - half, 1) * sa + pltpu.roll(x, half, 1) * sb


def _mla_q_kernel(*refs, rope):
    if rope:
        qa_ref, qan_ref, qb_ref, qn_ref, c_ref, sa_ref, sb_ref, o_ref = refs
    else:
        qa_ref, qan_ref, qb_ref, qn_ref, o_ref = refs
    qa = qa_ref[...]
    qn = qa * lax.rsqrt(jnp.mean(qa * qa, axis=-1, keepdims=True) + EPS) * qan_ref[...]
    qfull = _dot(qn.astype(BF16), qb_ref[...])
    outs = []
    for h in range(MLA_HEADS):
        qh = qfull[:, h * LANES:(h + 1) * LANES]
        ms = jnp.sum(qh * qh, axis=-1, keepdims=True) * (1.0 / MLA_QK)
        qh = qh * lax.rsqrt(ms + EPS) * qn_ref[...]
        if rope:
            qh = _rope(qh, c_ref[...], sa_ref[...], sb_ref[...], MLA_ROPE // 2)
        outs.append((qh * MLA_SCALE).astype(BF16))
    o_ref[...] = jnp.concatenate(outs, axis=1)


def mla_queries(z, qa_norm, q_b_pad, q_norm_pad, tables, tokens_per_batch, tm=512):
    t = z.shape[0]
    rope = tables is not None
    row = lambda i: (0, 0)
    in_specs = [pl.BlockSpec((tm, MLA_Q_LORA), lambda i: (i, AB_COL_QA // MLA_Q_LORA)),
                pl.BlockSpec((1, MLA_Q_LORA), row),
                pl.BlockSpec(q_b_pad.shape, row),
                pl.BlockSpec((1, LANES), row)]
    args = [z, qa_norm.reshape(1, -1), q_b_pad, q_norm_pad]
    if rope:
        tpb = tokens_per_batch // tm
        in_specs += [pl.BlockSpec((tm, LANES), lambda i: (i % tpb, 0))] * 3
        args += list(tables)
    return pl.pallas_call(
        functools.partial(_mla_q_kernel, rope=rope),
        grid=(t // tm,),
        in_specs=in_specs,
        out_specs=pl.BlockSpec((tm, MLA_HEADS * LANES), lambda i: (i, 0)),
        out_shape=jax.ShapeDtypeStruct((t, MLA_HEADS * LANES), BF16),
        compiler_params=_cparams(("parallel",)),
        name="mla_queries",
    )(*args)


def _mla_kv_kernel(*refs, norm_input, rope):
    if rope:
        kva_ref, kpe_ref, kvan_ref, wk_ref, wv_ref, kn_ref, c_ref, sa_ref, sb_ref, ckv_ref, k_ref, v_ref = refs
    else:
        kva_ref, kpe_ref, kvan_ref, wk_ref, wv_ref, kn_ref, ckv_ref, k_ref, v_ref = refs
    ckv = kva_ref[...]
    if norm_input:
        ckv = ckv * lax.rsqrt(jnp.mean(ckv * ckv, axis=-1, keepdims=True) + EPS) * kvan_ref[...]
    ckv_ref[...] = ckv
    cb = ckv.astype(BF16)
    knope = _dot(cb, wk_ref[...])
    v_ref[...] = _dot(cb, wv_ref[...]).astype(BF16)
    kpe = pltpu.roll(kpe_ref[...], MLA_NOPE, 1)
    outs = []
    for h in range(MLA_HEADS):
        kh = knope[:, h * LANES:(h + 1) * LANES] + kpe
        ms = jnp.sum(kh * kh, axis=-1, keepdims=True) * (1.0 / MLA_QK)
        kh = kh * lax.rsqrt(ms + EPS) * kn_ref[...]
        if rope:
            kh = _rope(kh, c_ref[...], sa_ref[...], sb_ref[...], MLA_ROPE // 2)
        outs.append(kh.astype(BF16))
    k_ref[...] = jnp.concatenate(outs, axis=1)


def mla_keys_values(kva_src, kva_col, kpe_src, kpe_col, kva_norm, wk_pad, wv, k_norm_pad, tables,
                    tokens_per_batch, norm_input, tm=512):
    t = kva_src.shape[0]
    rope = tables is not None
    row = lambda i: (0, 0)
    in_specs = [pl.BlockSpec((tm, LANES), lambda i: (i, kva_col)),
                pl.BlockSpec((tm, LANES), lambda i: (i, kpe_col)),
                pl.BlockSpec((1, LANES), row),
                pl.BlockSpec(wk_pad.shape, row),
                pl.BlockSpec(wv.shape, row),
                pl.BlockSpec((1, LANES), row)]
    args = [kva_src, kpe_src, kva_norm.reshape(1, -1), wk_pad, wv, k_norm_pad]
    if rope:
        tpb = tokens_per_batch // tm
        in_specs += [pl.BlockSpec((tm, LANES), lambda i: (i % tpb, 0))] * 3
        args += list(tables)
    return pl.pallas_call(
        functools.partial(_mla_kv_kernel, norm_input=norm_input, rope=rope),
        grid=(t // tm,),
        in_specs=in_specs,
        out_specs=[pl.BlockSpec((tm, LANES), lambda i: (i, 0)),
                   pl.BlockSpec((tm, MLA_HEADS * LANES), lambda i: (i, 0)),
                   pl.BlockSpec((tm, MLA_HEADS * MLA_V), lambda i: (i, 0))],
        out_shape=[jax.ShapeDtypeStruct((t, LANES), F32),
                   jax.ShapeDtypeStruct((t, MLA_HEADS * LANES), BF16),
                   jax.ShapeDtypeStruct((t, MLA_HEADS * MLA_V), BF16)],
        compiler_params=_cparams(("parallel",)),
        name="mla_keys_values",
    )(*args)


def _mla_attn_kernel(q_ref, k_ref, vt_ref, o_ref):
    heads = range(q_ref.shape[2] // LANES)
    ss = [_dot_nt(k_ref[0][:, j * LANES:(j + 1) * LANES], q_ref[0][:, j * LANES:(j + 1) * LANES]) for j in heads]
    ms = [jnp.max(s, axis=0, keepdims=True) for s in ss]
    ps = [jnp.exp(s - m) for s, m in zip(ss, ms)]
    ls = [jnp.sum(p, axis=0, keepdims=True) for p in ps]
    ots = [_dot(vt_ref[0][j * MLA_V:(j + 1) * MLA_V, :], ps[j].astype(BF16)) / ls[j] for j in heads]
    for j in range(0, len(ots), 2):
        o_ref[0, :, j * MLA_V:(j + 2) * MLA_V] = jnp.concatenate(ots[j:j + 2], axis=0).T


def mla_attention(q, k, v, batch, n_q, n_k, tq=512):
    tq = min(tq, n_q)
    hps = 2 if n_k > 1024 else MLA_HEADS
    q3 = q.reshape(batch, n_q, -1)
    k3 = k.reshape(batch, n_k, -1)
    vt = jnp.swapaxes(v.reshape(batch, n_k, -1), 1, 2)
    out = pl.pallas_call(
        _mla_attn_kernel,
        grid=(batch, MLA_HEADS // hps, n_q // tq),
        in_specs=[pl.BlockSpec((1, tq, hps * LANES), lambda b, j, i: (b, i, j)),
                  pl.BlockSpec((1, n_k, hps * LANES), lambda b, j, i: (b, 0, j)),
                  pl.BlockSpec((1, hps * MLA_V, n_k), lambda b, j, i: (b, j, 0))],
        out_specs=pl.BlockSpec((1, tq, hps * MLA_V), lambda b, j, i: (b, i, j)),
        out_shape=jax.ShapeDtypeStruct((batch, n_q, MLA_HEADS * MLA_V), F32),
        compiler_params=_cparams(("parallel", "parallel", "arbitrary")),
        name="mla_attention",
    )(q3, k3, vt)
    return out.reshape(batch * n_q, -1)


def _out0_kernel(of_ref, ob_ref, ag_ref, hgn_ref, om_ref, w_ref, x_ref, g_ref, o_ref):
    o = of_ref[...] + ob_ref[...]
    ag = ag_ref[...]
    parts = []
    for h in range(HG_HEADS):
        oh = o[:, h * HG_DV:(h + 1) * HG_DV]
        oh = oh * lax.rsqrt(jnp.mean(oh * oh, axis=-1, keepdims=True) + EPS) * hgn_ref[...]
        gh = ag[:, h * HG_DV:(h + 1) * HG_DV]
        parts.append((oh * (gh * _sigmoid(gh))).astype(BF16))
    oa = jnp.concatenate(parts, axis=1)
    n_a = HG_HEADS * HG_DV
    mix = _dot(oa, w_ref[0:n_a, :]) + _dot(om_ref[...].astype(BF16), w_ref[n_a:, :])
    o_ref[...] = x_ref[...] + g_ref[0] * mix


def out_proj_layer0(o_f, o_b, z, hg_norm, o_mla, w_out_bf16, x, gate, tokens_per_group, tm=512):
    t, d = x.shape
    n_a = HG_HEADS * HG_DV
    tiles_per_group = tokens_per_group // tm
    tile = lambda w: pl.BlockSpec((tm, w), lambda i: (i, 0))
    return pl.pallas_call(
        _out0_kernel,
        grid=(t // tm,),
        in_specs=[tile(n_a), tile(n_a),
                  pl.BlockSpec((tm, n_a), lambda i: (i, AB_COL_GATE // n_a)),
                  pl.BlockSpec((1, HG_DV), lambda i: (0, 0)),
                  tile(o_mla.shape[1]),
                  pl.BlockSpec(w_out_bf16.shape, lambda i: (0, 0)),
                  tile(d),
                  pl.BlockSpec((1, 1, d), lambda i: (i // tiles_per_group, 0, 0))],
        out_specs=tile(d),
        out_shape=jax.ShapeDtypeStruct((t, d), F32),
        compiler_params=_cparams(("parallel",)),
        name="out_proj_layer0",
    )(o_f, o_b, z, hg_norm.reshape(1, -1), o_mla, w_out_bf16, x, gate)


def _out1_kernel(a_ref, w_ref, x_ref, g_ref, o_ref):
    o_ref[...] = x_ref[...] + g_ref[0] * _dot(a_ref[...].astype(BF16), w_ref[...])


def out_proj_layer1(a, w_out_bf16, x, gate, tokens_per_group, tm=512):
    t, d = x.shape
    tiles_per_group = tokens_per_group // tm
    tile = lambda w: pl.BlockSpec((tm, w), lambda i: (i, 0))
    return pl.pallas_call(
        _out1_kernel,
        grid=(t // tm,),
        in_specs=[tile(a.shape[1]),
                  pl.BlockSpec(w_out_bf16.shape, lambda i: (0, 0)),
                  tile(d),
                  pl.BlockSpec((1, 1, d), lambda i: (i // tiles_per_group, 0, 0))],
        out_specs=tile(d),
        out_shape=jax.ShapeDtypeStruct((t, d), F32),
        compiler_params=_cparams(("parallel",)),
        name="out_proj_layer1",
    )(a, w_out_bf16, x, gate)


def _head_rms(x, gain2):
    sq = x * x
    lane = lax.broadcasted_iota(jnp.int32, x.shape, 1)
    first = lane < SWA_HD
    lo = jnp.sum(jnp.where(first, sq, 0.0), axis=-1, keepdims=True)
    hi = jnp.sum(jnp.where(first, 0.0, sq), axis=-1, keepdims=True)
    ms = jnp.where(first, lo, hi) * (1.0 / SWA_HD)
    return x * lax.rsqrt(ms + EPS) * gain2


def _swa_prep_kernel(*refs, rope):
    if rope:
        zq_ref, zk_ref, zv_ref, qn_ref, kn_ref, c_ref, sa_ref, sb_ref, q_ref, kc_ref, k_ref, v_ref = refs
    else:
        zq_ref, zk_ref, zv_ref, qn_ref, kn_ref, q_ref, kc_ref, k_ref, v_ref = refs
    half = SWA_HD // 2

    def rot(x):
        return _rope(x, c_ref[...], sa_ref[...], sb_ref[...], half) if rope else x

    zq = zq_ref[...]
    qs = []
    for p in range(zq.shape[1] // LANES):
        x = _head_rms(zq[:, p * LANES:(p + 1) * LANES], qn_ref[...])
        qs.append((rot(x) * SWA_SCALE).astype(BF16))
    q_ref[...] = jnp.concatenate(qs, axis=1)
    zk = zk_ref[...]
    kn, kr = [], []
    for p in range(zk.shape[1] // LANES):
        x = _head_rms(zk[:, p * LANES:(p + 1) * LANES], kn_ref[...])
        kn.append(x)
        kr.append(rot(x).astype(BF16))
    kc_ref[...] = jnp.concatenate(kn, axis=1)
    k_ref[...] = jnp.concatenate(kr, axis=1)
    v_ref[...] = zv_ref[...].astype(BF16)


def swa_prep(z, q_norm2, k_norm2, tables, tokens_per_batch, tm=512):
    t = z.shape[0]
    nq = SWA_HEADS * SWA_HD
    nkv = SWA_KV_HEADS * SWA_HD
    rope = tables is not None
    row = lambda i: (0, 0)
    in_specs = [pl.BlockSpec((tm, nq), lambda i: (i, 0)),
                pl.BlockSpec((tm, nkv), lambda i: (i, nq // nkv)),
                pl.BlockSpec((tm, nkv), lambda i: (i, nq // nkv + 1)),
                pl.BlockSpec((1, LANES), row),
                pl.BlockSpec((1, LANES), row)]
    args = [z, z, z, q_norm2, k_norm2]
    if rope:
        tpb = tokens_per_batch // tm
        in_specs += [pl.BlockSpec((tm, LANES), lambda i: (i % tpb, 0))] * 3
        args += list(tables)
    tile = lambda w: pl.BlockSpec((tm, w), lambda i: (i, 0))
    return pl.pallas_call(
        functools.partial(_swa_prep_kernel, rope=rope),
        grid=(t // tm,),
        in_specs=in_specs,
        out_specs=[tile(nq), tile(nkv), tile(nkv), tile(nkv)],
        out_shape=[jax.ShapeDtypeStruct((t, nq), BF16),
                   jax.ShapeDtypeStruct((t, nkv), F32),
                   jax.ShapeDtypeStruct((t, nkv), BF16),
                   jax.ShapeDtypeStruct((t, nkv), BF16)],
        compiler_params=_cparams(("parallel",)),
        name="swa_prep",
    )(*args)


def _sink_attention_t(q8, k_all, vt_all, bias, sink_ref):
    tq = q8.shape[0]
    n_q = SWA_GROUP * tq
    lane = lax.broadcasted_iota(jnp.int32, (1, n_q), 1)
    kv = range(k_all.shape[1] // SWA_HD)
    heads = [[n * SWA_GROUP + g for g in range(SWA_GROUP)] for n in kv]
    q4s = [jnp.concatenate([q8[:, hq * SWA_HD:(hq + 1) * SWA_HD] for hq in heads[n]], axis=0) for n in kv]
    sks = []
    for n in kv:
        sk = jnp.zeros((1, n_q), F32)
        for g, hq in enumerate(heads[n]):
            sk = jnp.where(lane // tq == g, sink_ref[hq], sk)
        sks.append(sk)
    ss = [_dot_nt(k_all[:, n * SWA_HD:(n + 1) * SWA_HD], q4s[n]) for n in kv]
    if bias is not None:
        ss = [s + bias for s in ss]
    ms = [jnp.maximum(jnp.max(ss[n], axis=0, keepdims=True), sks[n]) for n in kv]
    ps = [jnp.exp(ss[n] - ms[n]) for n in kv]
    ls = [jnp.sum(ps[n], axis=0, keepdims=True) + jnp.exp(sks[n] - ms[n]) for n in kv]
    ots = [_dot(vt_all[n * SWA_HD:(n + 1) * SWA_HD, :], ps[n].astype(BF16)) / ls[n] for n in kv]
    pairs = []
    for n in kv:
        for g in range(0, SWA_GROUP, 2):
            two = jnp.concatenate([ots[n][:, g * tq:(g + 1) * tq], ots[n][:, (g + 1) * tq:(g + 2) * tq]], axis=0)
            pairs.append(two.T)
    return jnp.concatenate(pairs, axis=1)


def _swa_dense_kernel(sink_ref, q_ref, k_ref, vt_ref, o_ref):
    o_ref[0] = _sink_attention_t(q_ref[0], k_ref[0], vt_ref[0], None, sink_ref)


def swa_dense(q, k, v, sink, batch, seq):
    nq = SWA_HEADS * SWA_HD
    nkv = SWA_KV_HEADS * SWA_HD
    q3 = q.reshape(batch, seq, nq)
    k3 = k.reshape(batch, seq, -1)
    vt = jnp.swapaxes(v.reshape(batch, seq, -1), 1, 2)
    out = pl.pallas_call(
        _swa_dense_kernel,
        grid=(batch,),
        in_specs=[pl.BlockSpec(memory_space=pltpu.SMEM),
                  pl.BlockSpec((1, seq, nq), lambda b: (b, 0, 0)),
                  pl.BlockSpec((1, seq, nkv), lambda b: (b, 0, 0)),
                  pl.BlockSpec((1, nkv, seq), lambda b: (b, 0, 0))],
        out_specs=pl.BlockSpec((1, seq, nq), lambda b: (b, 0, 0)),
        out_shape=jax.ShapeDtypeStruct((batch, seq, nq), F32),
        compiler_params=_cparams(("parallel",)),
        name="swa_dense",
    )(sink, q3, k3, vt)
    return out.reshape(batch * seq, nq)


def _swa_window_kernel(sink_ref, q_ref, kc_ref, kp_ref, k0_ref, kn_ref, vc_ref, vp_ref, v0_ref, vn_ref, o_ref):
    i = pl.program_id(1)
    nb = pl.num_programs(1)
    w = SWA_WINDOW
    n_ctx = kc_ref.shape[1]
    k_all = jnp.concatenate([kc_ref[0], kp_ref[0], k0_ref[0], kn_ref[0]], axis=0)
    vt_all = jnp.concatenate([vc_ref[0], vp_ref[0], v0_ref[0], vn_ref[0]], axis=1)
    shape = (n_ctx + 3 * w, SWA_GROUP * w)
    key = lax.broadcasted_iota(jnp.int32, shape, 0) - n_ctx
    r = lax.broadcasted_iota(jnp.int32, shape, 1) % w
    prev_bias = jnp.where(i > 0, 0.0, NEG_BIG)
    next_bias = jnp.where(i < nb - 1, 0.0, NEG_BIG)
    bias = jnp.where(
        key < w,
        jnp.where(key < 0, 0.0, jnp.where(key >= r, prev_bias, NEG_BIG)),
        jnp.where(key < 2 * w, 0.0, jnp.where(key - 2 * w <= r, next_bias, NEG_BIG)))
    o_ref[0] = _sink_attention_t(q_ref[0], k_all, vt_all, bias, sink_ref)


def swa_window(q, k, v, k_ctx, v_ctx, sink, batch, seq, n_ctx):
    nq = SWA_HEADS * SWA_HD
    w = SWA_WINDOW
    nb = seq // w
    q3 = q.reshape(batch, seq, nq)
    k3 = k.reshape(batch, seq, -1)
    vt = jnp.swapaxes(v.reshape(batch, seq, -1), 1, 2)
    vt_ctx = jnp.swapaxes(v_ctx, 1, 2)
    nkv = SWA_KV_HEADS * SWA_HD
    ctx = pl.BlockSpec((1, n_ctx, nkv), lambda b, i: (b, 0, 0))
    prv = pl.BlockSpec((1, w, nkv), lambda b, i: (b, jnp.maximum(i - 1, 0), 0))
    cur = pl.BlockSpec((1, w, nkv), lambda b, i: (b, i, 0))
    nxt = pl.BlockSpec((1, w, nkv), lambda b, i: (b, jnp.minimum(i + 1, nb - 1), 0))
    ctx_t = pl.BlockSpec((1, nkv, n_ctx), lambda b, i: (b, 0, 0))
    prv_t = pl.BlockSpec((1, nkv, w), lambda b, i: (b, 0, jnp.maximum(i - 1, 0)))
    cur_t = pl.BlockSpec((1, nkv, w), lambda b, i: (b, 0, i))
    nxt_t = pl.BlockSpec((1, nkv, w), lambda b, i: (b, 0, jnp.minimum(i + 1, nb - 1)))
    out = pl.pallas_call(
        _swa_window_kernel,
        grid=(batch, nb),
        in_specs=[pl.BlockSpec(memory_space=pltpu.SMEM),
                  pl.BlockSpec((1, w, nq), lambda b, i: (b, i, 0)),
                  ctx, prv, cur, nxt, ctx_t, prv_t, cur_t, nxt_t],
        out_specs=pl.BlockSpec((1, w, nq), lambda b, i: (b, i, 0)),
        out_shape=jax.ShapeDtypeStruct((batch, seq, nq), F32),
        compiler_params=_cparams(("parallel", "arbitrary")),
        name="swa_window",
    )(sink, q3, k_ctx, k3, k3, k3, vt_ctx, vt, vt, vt)
    return out.reshape(batch * seq, nq)


def _router_kernel(x_ref, g_ref, sh_ref, sc_ref, rw_ref, rb_ref, tri_ref, h_ref, idx_ref, gate_ref, rank_ref,
                   cnt_ref, cnt, *, tiles_per_part):
    @pl.when(pl.program_id(0) % tiles_per_part == 0)
    def _():
        cnt[...] = jnp.zeros_like(cnt)

    h = _modnorm(x_ref[...], g_ref[...], sh_ref[0], sc_ref[0])
    bits = lax.bitcast_convert_type(h.astype(BF16).astype(F32), jnp.uint32)
    half = h.shape[1] // 2
    h_ref[...] = (bits[:, :half] >> 16) | (bits[:, half:] & jnp.uint32(0xFFFF0000))
    wh, wl = _split2(rw_ref[...])
    hh, hl = _split2(h)
    logits = _dot_nt(wh, hh) + _dot_nt(wh, hl) + _dot_nt(wl, hh) + rb_ref[...]
    row = lax.broadcasted_iota(jnp.int32, logits.shape, 0)
    work = logits
    vals, idxs = [], []
    for _ in range(TOP_K):
        m = jnp.max(work, axis=0, keepdims=True)
        ik = jnp.min(jnp.where(work == m, row, LANES), axis=0, keepdims=True)
        vals.append(m)
        idxs.append(ik)
        work = jnp.where(row == ik, 2.0 * NEG_BIG, work)
    es = [jnp.exp(v - vals[0]) for v in vals]
    denom = es[0] + es[1] + es[2] + es[3]
    chosen = jnp.zeros(logits.shape, F32)
    for k in range(TOP_K):
        chosen = jnp.where(row == idxs[k], 1.0, chosen)
    chosen_b = chosen.astype(BF16)
    seen = cnt[...]
    before = _dot(chosen_b, tri_ref[...]) + jnp.concatenate([seen] * (logits.shape[1] // LANES), axis=1)
    out_row = lax.broadcasted_iota(jnp.int32, idx_ref.shape, 0)
    idx_out = jnp.zeros(idx_ref.shape, jnp.int32)
    gate_out = jnp.zeros(idx_ref.shape, F32)
    rank_out = jnp.zeros(idx_ref.shape, jnp.int32)
    for k in range(TOP_K):
        rk = jnp.sum(jnp.where(row == idxs[k], before, 0.0), axis=0, keepdims=True)
        idx_out = jnp.where(out_row == k, idxs[k], idx_out)
        gate_out = jnp.where(out_row == k, es[k] / denom, gate_out)
        rank_out = jnp.where(out_row == k, rk.astype(jnp.int32), rank_out)
    idx_ref[...] = idx_out
    gate_ref[...] = gate_out
    rank_ref[...] = rank_out
    cnt[...] = seen + _dot(chosen_b, jnp.ones((logits.shape[1], LANES), BF16))
    cnt_ref[0] = cnt[...]


def moe_router(x, gain, shift, scale, rw_t, rb_col, tokens_per_group, tm=512):
    t, d = x.shape
    tiles_per_group = tokens_per_group // tm
    tiles_per_part = MOE_PART // tm
    grp = lambda i: (i // tiles_per_group, 0, 0)
    tile = lambda w: pl.BlockSpec((tm, w), lambda i: (i, 0))
    slab = pl.BlockSpec((8, tm), lambda i: (0, i))
    tri = jnp.asarray(np.triu(np.ones((tm, tm), np.float32), 1), BF16)
    hp, idx, gates, rank, cnt = pl.pallas_call(
        functools.partial(_router_kernel, tiles_per_part=tiles_per_part),
        grid=(t // tm,),
        in_specs=[tile(d),
                  pl.BlockSpec((1, d), lambda i: (0, 0)),
                  pl.BlockSpec((1, 1, d), grp),
                  pl.BlockSpec((1, 1, d), grp),
                  pl.BlockSpec((LANES, d), lambda i: (0, 0)),
                  pl.BlockSpec((LANES, 1), lambda i: (0, 0)),
                  pl.BlockSpec((tm, tm), lambda i: (0, 0))],
        out_specs=[tile(d // 2), slab, slab, slab,
                   pl.BlockSpec((1, LANES, LANES), lambda i: (i // tiles_per_part, 0, 0))],
        out_shape=[jax.ShapeDtypeStruct((t, d // 2), jnp.uint32),
                   jax.ShapeDtypeStruct((8, t), jnp.int32),
                   jax.ShapeDtypeStruct((8, t), F32),
                   jax.ShapeDtypeStruct((8, t), jnp.int32),
                   jax.ShapeDtypeStruct((t // MOE_PART, LANES, LANES), F32)],
        scratch_shapes=[pltpu.VMEM((LANES, LANES), F32)],
        compiler_params=_cparams(("arbitrary",)),
        name="moe_router",
    )(x, gain.reshape(1, d), shift, scale, rw_t, rb_col, tri)
    per_token = lambda a: a[:TOP_K].T
    return hp, per_token(idx), per_token(gates), per_token(rank), cnt[:, :N_EXPERTS, 0].astype(jnp.int32)


FFN_COLS = 512
FFN_PIECES = 2 * D_FF // FFN_COLS + D_MODEL // FFN_COLS
MOE_PART = 4096
ADD_BATCH = 8


def _unpack_rows(words):
    lo = lax.bitcast_convert_type(words << 16, F32)
    hi = lax.bitcast_convert_type(words & jnp.uint32(0xFFFF0000), F32)
    return jnp.concatenate([lo, hi], axis=1).astype(BF16)


def _ffn_block(x_ref, y_ref, wgu_ref, bgu_ref, wd_ref, bd_ref, between):
    xb = _unpack_rows(x_ref[...])
    acts = []
    for c in range(D_FF // FFN_COLS):
        lo, hi = c * FFN_COLS, (c + 1) * FFN_COLS
        between(2 * c)
        gate = _dot(xb, wgu_ref[0, :, lo:hi]) + bgu_ref[0][:, lo:hi]
        between(2 * c + 1)
        up = _dot(xb, wgu_ref[0, :, D_FF + lo:D_FF + hi]) + bgu_ref[0][:, D_FF + lo:D_FF + hi]
        gate = jnp.minimum(gate, SWIGLU_LIMIT)
        up = jnp.clip(up, -SWIGLU_LIMIT, SWIGLU_LIMIT)
        acts.append((gate * _sigmoid(SWIGLU_ALPHA * gate) * (up + 1.0)).astype(BF16))
    act = jnp.concatenate(acts, axis=1)
    for n in range(D_MODEL // FFN_COLS):
        lo, hi = n * FFN_COLS, (n + 1) * FFN_COLS
        between(2 * D_FF // FFN_COLS + n)
        y_ref[:, lo:hi] = _dot(act, wd_ref[0, :, lo:hi]) + bd_ref[0][:, lo:hi]


def _expert_kernel(nb_ref, bs_ref, loc_ref, gate_ref, hp_hbm, wgu_ref, bgu_ref, wd_ref, bd_ref, out_hbm,
                   hbuf, acc, xbuf, ybuf, sem):
    part = pl.program_id(0)
    e = pl.program_id(1)
    idx = part * N_EXPERTS + e
    nb = nb_ref[idx]
    b0 = bs_ref[idx]
    shares = np.array_split(np.arange(MOE_ROWS), FFN_PIECES)

    def fetch_rows(blk):
        base = blk * MOE_ROWS
        dst = xbuf.at[blk % 2]

        def emit(k):
            for r in shares[k]:
                r = int(r)
                dst[r:r + 1, :] = hbuf[pl.ds(loc_ref[base + r], 1), :]
        return emit

    def add_rows(blk):
        base = blk * MOE_ROWS
        src = ybuf.at[blk % 2]

        def emit(k):
            for batch in np.array_split(shares[k], max(1, len(shares[k]) // ADD_BATCH)):
                rows = [loc_ref[base + int(r)] for r in batch]
                new = [acc[pl.ds(row, 1), :] + gate_ref[base + int(r)] * src[int(r):int(r) + 1, :]
                       for row, r in zip(rows, batch)]
                for row, val in zip(rows, new):
                    acc[pl.ds(row, 1), :] = val
        return emit

    def emit_all(f):
        for k in range(FFN_PIECES):
            f(k)

    @pl.when(e == 0)
    def _():
        load = pltpu.make_async_copy(hp_hbm.at[pl.ds(part * MOE_PART, MOE_PART), :],
                                     hbuf.at[pl.ds(0, MOE_PART), :], sem.at[0])
        load.start()
        hbuf[MOE_PART:, :] = jnp.zeros((hbuf.shape[0] - MOE_PART, hbuf.shape[1]), hbuf.dtype)
        acc[...] = jnp.zeros_like(acc)
        ybuf[...] = jnp.zeros_like(ybuf)
        load.wait()
        emit_all(fetch_rows(b0))

    def block(i, carry):
        fetch, add = fetch_rows(i + 1), add_rows(i - 1)

        def between(k):
            fetch(k)
            add(k)
        _ffn_block(xbuf.at[i % 2], ybuf.at[i % 2], wgu_ref, bgu_ref, wd_ref, bd_ref, between)
        return carry

    lax.fori_loop(b0, b0 + nb, block, 0)

    @pl.when(e == N_EXPERTS - 1)
    def _():
        emit_all(add_rows(b0 + nb - 1))
        store = pltpu.make_async_copy(acc.at[pl.ds(0, MOE_PART), :],
                                      out_hbm.at[pl.ds(part * MOE_PART, MOE_PART), :], sem.at[0])
        store.start()
        store.wait()


def moe_experts(hp, n_blk, blk_start, row_loc, row_gate, w_gu, b_gu, w_down, b_down):
    t = hp.shape[0]
    d = D_MODEL
    per_expert = lambda p, e, nb, bs, sl: (e, 0, 0)
    grid_spec = pltpu.PrefetchScalarGridSpec(
        num_scalar_prefetch=3,
        grid=(t // MOE_PART, N_EXPERTS),
        in_specs=[pl.BlockSpec(memory_space=pltpu.SMEM),
                  pl.BlockSpec(memory_space=pl.ANY),
                  pl.BlockSpec((1, d, 2 * D_FF), per_expert),
                  pl.BlockSpec((1, 1, 2 * D_FF), per_expert),
                  pl.BlockSpec((1, D_FF, d), per_expert),
                  pl.BlockSpec((1, 1, d), per_expert)],
        out_specs=pl.BlockSpec(memory_space=pl.ANY),
        scratch_shapes=[pltpu.VMEM((MOE_PART + 8, d // 2), jnp.uint32),
                        pltpu.VMEM((MOE_PART + 8, d), F32),
                        pltpu.VMEM((2, MOE_ROWS, d // 2), jnp.uint32),
                        pltpu.VMEM((2, MOE_ROWS, d), F32),
                        pltpu.SemaphoreType.DMA((1,))],
    )
    return pl.pallas_call(
        _expert_kernel,
        grid_spec=grid_spec,
        out_shape=jax.ShapeDtypeStruct((t, d), F32),
        compiler_params=pltpu.CompilerParams(dimension_semantics=("arbitrary", "arbitrary"),
                                             vmem_limit_bytes=EXPERT_VMEM_LIMIT),
        name="moe_experts",
    )(n_blk, blk_start, row_loc, row_gate, hp, w_gu, b_gu.reshape(N_EXPERTS, 1, -1), w_down,
      b_down.reshape(N_EXPERTS, 1, -1))


def _residual_kernel(y_ref, x_ref, g_ref, o_ref):
    o_ref[...] = x_ref[...] + g_ref[0] * y_ref[...]


def gated_residual(y, x, gate_mod, tokens_per_group, tm=512):
    t, d = x.shape
    tiles_per_group = tokens_per_group // tm
    tile = pl.BlockSpec((tm, d), lambda i: (i, 0))
    return pl.pallas_call(
        _residual_kernel,
        grid=(t // tm,),
        in_specs=[tile, tile, pl.BlockSpec((1, 1, d), lambda i: (i // tiles_per_group, 0, 0))],
        out_specs=tile,
        out_shape=jax.ShapeDtypeStruct((t, d), F32),
        compiler_params=_cparams(("parallel",)),
        name="moe_residual",
    )(y, x, gate_mod)


def _assignment_tables(top_idx, rank, gates, counts):
    t = top_idx.shape[0]
    n_parts = t // MOE_PART
    n_assign = t * TOP_K
    part_blocks = MOE_PART * TOP_K // MOE_ROWS + N_EXPERTS + 1
    n_rows = (1 + n_parts * part_blocks) * MOE_ROWS
    padded = (counts + MOE_ROWS - 1) // MOE_ROWS * MOE_ROWS
    part_first = (1 + jnp.arange(n_parts, dtype=jnp.int32) * part_blocks) * MOE_ROWS
    row_start = part_first[:, None] + jnp.cumsum(padded, axis=1) - padded
    experts = jnp.arange(N_EXPERTS, dtype=jnp.int32)
    by_part = top_idx.reshape(n_parts, MOE_PART, TOP_K)
    start_of = jnp.sum(jnp.where(by_part[..., None] == experts, row_start[:, None, None, :], 0), axis=-1)
    dest = (start_of.reshape(t, TOP_K) + rank).reshape(-1).astype(jnp.int32)
    local = (jnp.arange(n_assign, dtype=jnp.int32) // TOP_K) % MOE_PART
    vals = jnp.stack([local, lax.bitcast_convert_type(gates.reshape(-1), jnp.int32)], axis=1)
    init = jnp.broadcast_to(jnp.array([MOE_PART, 0], jnp.int32), (n_rows, 2))
    table = init.at[dest].set(vals)
    row_loc = table[:, 0]
    row_gate = lax.bitcast_convert_type(table[:, 1], F32)
    return row_loc, row_gate, (padded // MOE_ROWS).reshape(-1).astype(jnp.int32), \
        (row_start // MOE_ROWS).reshape(-1).astype(jnp.int32)


def moe_layer(x, p, shift, scale, gate_mod, tokens_per_group):
    hp, idx, gates, rank, counts = moe_router(x, p['norm2'], shift, scale, p['rw_t'], p['rb_col'], tokens_per_group)
    row_loc, row_gate, n_blk, blk_start = _assignment_tables(idx, rank, gates, counts)
    y = moe_experts(hp, n_blk, blk_start, row_loc, row_gate, p['w_gu_bf'], p['b_gu'], p['w_down_bf'], p['b_down'])
    return gated_residual(y, x, gate_mod, tokens_per_group)


def _pad_lanes(a, width):
    return jnp.pad(a, [(0, 0)] * (a.ndim - 1) + [(0, width - a.shape[-1])])


def _prep_common(p):
    p['rw_t'] = _pad_lanes(p['router_w'], LANES).T
    p['rb_col'] = jnp.concatenate(
        [p['router_b'].astype(F32), jnp.full((LANES - N_EXPERTS,), NEG_BIG, F32)]).reshape(LANES, 1)
    p['w_out'] = p['w_out'].astype(BF16)
    p['w_gu_bf'] = p['w_gu'].astype(BF16)
    p['w_down_bf'] = p['w_down'].astype(BF16)
    return p


def _prep_layer0(p):
    p = _prep_common(dict(p))
    p['w_in'] = _pad_lanes(p['w_in'], AB_IN_PAD).astype(BF16)
    q_b = p['q_b'].reshape(MLA_Q_LORA, MLA_HEADS, MLA_QK)
    p['q_b_pad'] = _pad_lanes(q_b, LANES).reshape(MLA_Q_LORA, MLA_HEADS * LANES).astype(BF16)
    kv_b = p['kv_b'].reshape(MLA_KV_LORA, MLA_HEADS, MLA_NOPE + MLA_V)
    p['wk_pad'] = _pad_lanes(kv_b[:, :, :MLA_NOPE], LANES).reshape(MLA_KV_LORA, MLA_HEADS * LANES).astype(BF16)
    p['wv'] = kv_b[:, :, MLA_NOPE:].reshape(MLA_KV_LORA, MLA_HEADS * MLA_V).astype(BF16)
    p['q_norm_pad'] = _pad_lanes(p['q_norm'].reshape(1, -1), LANES)
    p['k_norm_pad'] = _pad_lanes(p['k_norm'].reshape(1, -1), LANES)
    return p


def _prep_layer1(p):
    p = _prep_common(dict(p))
    p['w_in'] = p['w_in'].astype(BF16)
    p['q_norm2'] = jnp.tile(p['q_norm'].reshape(1, -1), (1, LANES // SWA_HD))
    p['k_norm2'] = jnp.tile(p['k_norm'].reshape(1, -1), (1, LANES // SWA_HD))
    return p


def _group_forward(x3, mods, p0, p1, lb_logits, caches, latent):
    batch, seq, d = x3.shape
    t = batch * seq
    x = x3.reshape(t, d)
    tpg = seq if latent else t
    sh1, sc1, g1, sh2, sc2, g2 = mods[0]

    z = modnorm_matmul(x, p0['norm1'], sh1, sc1, p0['w_in'], tpg)
    tab_b = _rope_lane_tables(seq, MLA_ROPE, LANES, MLA_NOPE) if latent else None
    s0f, s0b = (caches['hg_f'], caches['hg_b']) if latent else (None, None)
    o_f, o_b, s_f, s_b = hgrn_bidir(z, lb_logits, s0f, s0b, batch, seq)
    q = mla_queries(z, p0['qa_norm'], p0['q_b_pad'], p0['q_norm_pad'], tab_b, seq)
    ckv, k, v = mla_keys_values(z, AB_COL_KVA // LANES, z, AB_COL_KPE // LANES, p0['kva_norm'], p0['wk_pad'], p0['wv'],
                                p0['k_norm_pad'], tab_b, seq, norm_input=True)
    n_k = seq
    if latent:
        n_ctx = caches['ckv'].shape[1]
        ckv_c = caches['ckv'].reshape(batch * n_ctx, MLA_KV_LORA)
        kpe_c = _pad_lanes(caches['kpe'].reshape(batch * n_ctx, MLA_ROPE), LANES)
        _, k_c, v_c = mla_keys_values(ckv_c, 0, kpe_c, 0, p0['kva_norm'], p0['wk_pad'], p0['wv'],
                                      p0['k_norm_pad'], None, n_ctx, norm_input=False)
        cat = lambda a, b: jnp.concatenate([a.reshape(batch, n_ctx, -1), b.reshape(batch, seq, -1)],
                                           axis=1).reshape(batch * (n_ctx + seq), -1)
        k, v = cat(k_c, k), cat(v_c, v)
        n_k = n_ctx + seq
    o_mla = mla_attention(q, k, v, batch, seq, n_k)
    x = out_proj_layer0(o_f, o_b, z, p0['hg_out_norm'], o_mla, p0['w_out'], x, g1, tpg)
    x = moe_layer(x, p0, sh2, sc2, g2, tpg)
    state0 = (s_f, s_b, ckv.reshape(batch, seq, MLA_KV_LORA), z[:, AB_COL_KPE:AB_COL_KPE + MLA_ROPE].reshape(batch, seq, MLA_ROPE))

    sh1, sc1, g1, sh2, sc2, g2 = mods[1]
    z = modnorm_matmul(x, p1['norm1'], sh1, sc1, p1['w_in'], tpg)
    tab_c = _rope_lane_tables(seq, SWA_HD, SWA_HD, 0) if latent else None
    q, k_cache, k, v = swa_prep(z, p1['q_norm2'], p1['k_norm2'], tab_c, seq)
    sink = p1['sink'].astype(F32)
    if latent:
        n_ctx = caches['k1'].shape[1]
        k_c = caches['k1'].reshape(batch, n_ctx, -1).astype(BF16)
        v_c = caches['v1'].reshape(batch, n_ctx, -1).astype(BF16)
        a = swa_window(q, k, v, k_c, v_c, sink, batch, seq, n_ctx)
    else:
        a = swa_dense(q, k, v, sink, batch, seq)
    x = out_proj_layer1(a, p1['w_out'], x, g1, tpg)
    x = moe_layer(x, p1, sh2, sc2, g2, tpg)
    nkv = SWA_KV_HEADS * SWA_HD
    state1 = (k_cache.reshape(batch, seq, SWA_KV_HEADS, SWA_HD),
              z[:, SWA_HEADS * SWA_HD + nkv:].reshape(batch, seq, SWA_KV_HEADS, SWA_HD))
    return x.reshape(batch, seq, d), state0, state1


def kernel(x_prompt, x_sample, state_l0_hgrn_fwd, state_l0_hgrn_bwd, cache_l0_mla_ckv, cache_l0_mla_kpe, cache_l1_k, cache_l1_v, c, c_ctx, hgrn_lb_logits, l0_ada_w, l0_ada_b, l0_norm1, l0_norm2, l0_w_in, l0_hg_out_norm, l0_qa_norm, l0_q_b, l0_kva_norm, l0_kv_b, l0_q_norm, l0_k_norm, l0_w_out, l0_router_w, l0_router_b, l0_w_gu, l0_b_gu, l0_w_down, l0_b_down, l1_ada_w, l1_ada_b, l1_norm1, l1_norm2, l1_w_in, l1_q_norm, l1_k_norm, l1_sink, l1_w_out, l1_router_w, l1_router_b, l1_w_gu, l1_b_gu, l1_w_down, l1_b_down):
    p0 = _prep_layer0(dict(norm1=l0_norm1, norm2=l0_norm2, w_in=l0_w_in, hg_out_norm=l0_hg_out_norm,
                           qa_norm=l0_qa_norm, q_b=l0_q_b, kva_norm=l0_kva_norm, kv_b=l0_kv_b,
                           q_norm=l0_q_norm, k_norm=l0_k_norm, w_out=l0_w_out, router_w=l0_router_w,
                           router_b=l0_router_b, w_gu=l0_w_gu, b_gu=l0_b_gu, w_down=l0_w_down,
                           b_down=l0_b_down))
    p1 = _prep_layer1(dict(norm1=l1_norm1, norm2=l1_norm2, w_in=l1_w_in, q_norm=l1_q_norm, k_norm=l1_k_norm,
                           sink=l1_sink, w_out=l1_w_out, router_w=l1_router_w, router_b=l1_router_b,
                           w_gu=l1_w_gu, b_gu=l1_b_gu, w_down=l1_w_down, b_down=l1_b_down))
    dec_batch = c.shape[0]
    d = c.shape[1]
    cond8 = jnp.concatenate([c_ctx[None, :], c, jnp.zeros((8 - 1 - dec_batch, d), F32)], axis=0)
    mods_ctx, mods_lat = [], []
    for w, b in ((l0_ada_w, l0_ada_b), (l1_ada_w, l1_ada_b)):
        mod = ada_params(cond8, w, b)
        mods_ctx.append([m.reshape(1, 1, d) for m in jnp.split(mod[0:1], 6, axis=-1)])
        mods_lat.append([m.reshape(dec_batch, 1, d) for m in jnp.split(mod[1:1 + dec_batch], 6, axis=-1)])

    y_prompt, st0, st1 = _group_forward(x_prompt, mods_ctx, p0, p1, hgrn_lb_logits, None, latent=False)
    caches = dict(hg_f=state_l0_hgrn_fwd, hg_b=state_l0_hgrn_bwd, ckv=cache_l0_mla_ckv, kpe=cache_l0_mla_kpe,
                  k1=cache_l1_k, v1=cache_l1_v)
    y_sample, _, _ = _group_forward(x_sample, mods_lat, p0, p1, hgrn_lb_logits, caches, latent=True)
    return (y_prompt, y_sample, st0[0], st0[1], st0[2], st0[3], st1[0], st1[1])
```
